```python
import math
import jax, jax.numpy as jnp
from jax import lax
import numpy as np

D_MODEL = 1024
BATCH = 8
SEQ = 16384
DEPTH = 2

NORM_EPS = 1e-6
CONV_K = 4
ATTN_HEADS = 8
ATTN_HEAD_DIM = 64
ATTN_WIDTH = ATTN_HEADS * ATTN_HEAD_DIM
DILATED_PATTERNS = ((128, 1), (512, 4), (2048, 16))
ATTN_BLOCK = 128
LRU_WIDTH = D_MODEL // 2
LRU_BLOCKS = 8
LRU_C = 8.0
AB_IN = 3 * ATTN_WIDTH + 2 * LRU_WIDTH
DN_HEADS = 8
DN_HEAD_DIM = 128
DN_WIDTH = DN_HEADS * DN_HEAD_DIM
DN_CHUNK = 64
DN_IN = 4 * DN_WIDTH + 2 * DN_HEADS
XA_HEADS = 4
XA_HEAD_DIM = D_MODEL // XA_HEADS
N_MEM = 256
D_FF = ((8 * D_MODEL // 3 + 127) // 128) * 128

kernel_name = "hybrid_dilated_attn_rglru_gdn_macaron"


def rmsnorm(x, g):
    xf = x.astype(jnp.float32)
    y = xf * lax.rsqrt(jnp.mean(xf * xf, axis=-1, keepdims=True) + NORM_EPS)
    return (y * g.astype(jnp.float32)).astype(x.dtype)


def swiglu(h, w_in, w_out):
    gate, up = jnp.split(h @ w_in, 2, axis=-1)
    return (jax.nn.silu(gate) * up) @ w_out


def causal_dwconv(x, w):
    C = x.shape[-1]
    return lax.conv_general_dilated(
        x, w[:, None, :].astype(x.dtype), window_strides=(1,),
        padding=((w.shape[0] - 1, 0),), dimension_numbers=('NWC', 'WIO', 'NWC'),
        feature_group_count=C)


def dilated_window_attention(q, k, v, dilation, n_back):
    B, S, H, hd = q.shape
    L = S // dilation
    Bd = B * dilation
    Lp = -(-L // ATTN_BLOCK) * ATTN_BLOCK
    nb = Lp // ATTN_BLOCK

    def split(t):
        t = t.reshape(B, L, dilation, H, hd).transpose(0, 2, 1, 3, 4).reshape(Bd, L, H, hd)
        return jnp.pad(t, ((0, 0), (0, Lp - L), (0, 0), (0, 0))).reshape(Bd, nb, ATTN_BLOCK, H, hd)

    qb, kb, vb = split(q), split(k), split(v)

    def with_prev(t):
        prev = jnp.concatenate([jnp.zeros_like(t[:, :1]), t[:, :-1]], axis=1)
        return jnp.concatenate([prev, t], axis=2)

    kk, vv = with_prev(kb), with_prev(vb)
    s = jnp.einsum('bnqhd,bnkhd->bnhqk', qb, kk).astype(jnp.float32) * (hd ** -0.5)
    qi = jnp.arange(ATTN_BLOCK)[:, None]
    kj = jnp.arange(2 * ATTN_BLOCK)[None, :]
    dist = qi + ATTN_BLOCK - kj
    band = (dist >= 0) & (dist <= n_back)
    not_first = (jnp.arange(nb) > 0)[:, None, None] | (kj >= ATTN_BLOCK)[None]
    valid = band[None] & not_first
    s = jnp.where(valid[None, :, None], s, -jnp.inf)
    m = jnp.max(s, axis=-1, keepdims=True)
    p = jnp.exp(s - m)
    den = jnp.sum(p, axis=-1)
    o = jnp.einsum('bnhqk,bnkhd->bnqhd', p, vv.astype(jnp.float32))
    den_t = jnp.moveaxis(den, 2, 3)
    o = o / den_t[..., None]
    lse = jnp.moveaxis(m[..., 0], 2, 3) + jnp.log(den_t)
    o = o.reshape(Bd, Lp, H, hd)[:, :L].reshape(B, dilation, L, H, hd).transpose(0, 2, 1, 3, 4)
    lse = lse.reshape(Bd, Lp, H)[:, :L].reshape(B, dilation, L, H).transpose(0, 2, 1, 3)
    return o.reshape(B, S, H, hd), lse.reshape(B, S, H)


def linear_scan(a, b):
    def comb(l, r):
        return (l[0] * r[0], r[0] * l[1] + r[1])
    _, h = lax.associative_scan(comb, (a, b), axis=1)
    return h


def attn_lru_mixer(h, w_in, conv_w, conv_b, w_a, b_a, w_x, b_x, lam, w_out):
    B, S, _ = h.shape
    A = ATTN_WIDTH
    q, k, v, xr, gr = jnp.split(h @ w_in, [A, 2 * A, 3 * A, 3 * A + LRU_WIDTH], axis=-1)
    shp = (B, S, ATTN_HEADS, ATTN_HEAD_DIM)
    q, k, v = q.reshape(shp), k.reshape(shp), v.reshape(shp)
    outs, lses = [], []
    for window, dil in DILATED_PATTERNS:
        o, l = dilated_window_attention(q, k, v, dil, window // dil)
        outs.append(o)
        lses.append(l)
    wts = jax.nn.softmax(jnp.stack(lses, 0), axis=0)
    attn = jnp.einsum('gbsh,gbshd->bshd', wts, jnp.stack(outs, 0))
    attn = attn.reshape(B, S, A).astype(h.dtype)
    xc = (causal_dwconv(xr, conv_w) + conv_b).astype(jnp.float32)
    xb = xc.reshape(B, S, LRU_BLOCKS, LRU_WIDTH // LRU_BLOCKS)
    r = jax.nn.sigmoid(jnp.einsum('bsnj,njk->bsnk', xb, w_a.astype(jnp.float32)).reshape(B, S, LRU_WIDTH)
                       + b_a.astype(jnp.float32))
    i = jax.nn.sigmoid(jnp.einsum('bsnj,njk->bsnk', xb, w_x.astype(jnp.float32)).reshape(B, S, LRU_WIDTH)
                       + b_x.astype(jnp.float32))
    log_a = -LRU_C * r * jax.nn.softplus(-lam.astype(jnp.float32))
    a = jnp.exp(log_a)
    mult = jnp.sqrt(-jnp.expm1(2.0 * log_a))
    hs = linear_scan(a, mult * i * xc)
    y = hs.astype(h.dtype) * jax.nn.gelu(gr)
    return jnp.concatenate([attn, y], axis=-1) @ w_out


def l2norm(t):
    return t * lax.rsqrt(jnp.sum(t * t, axis=-1, keepdims=True) + 1e-6)


def gated_delta_rule_chunked(q, k, v, g, beta):
    B, S, H, dk = q.shape
    dv = v.shape[-1]
    C = DN_CHUNK
    N = S // C

    def ch(t):
        t = jnp.moveaxis(t, 2, 1)
        return t.reshape((B, H, N, C) + t.shape[3:])

    q = ch(q * (dk ** -0.5))
    k, v, g, beta = ch(k), ch(v), ch(g), ch(beta)
    gcum = jnp.cumsum(g, axis=-1)
    tril = jnp.tril(jnp.ones((C, C), bool))
    strict = jnp.tril(jnp.ones((C, C), bool), -1)
    decay = jnp.exp(jnp.where(tril, gcum[..., :, None] - gcum[..., None, :], -jnp.inf))
    kb = k * beta[..., None]
    kkt = jnp.einsum('bhnid,bhnjd->bhnij', kb, k) * decay
    amat = jnp.eye(C, dtype=jnp.float32) + jnp.where(strict, kkt, 0.0)
    rhs = jnp.concatenate([v * beta[..., None], kb * jnp.exp(gcum)[..., None]], axis=-1)
    sol = lax.linalg.triangular_solve(amat, rhs, left_side=True, lower=True, unit_diagonal=True)
    u, w = sol[..., :dv], sol[..., dv:]
    qk = jnp.einsum('bhnid,bhnjd->bhnij', q, k) * decay

    def step(state, inp):
        q_i, k_i, u_i, w_i, qk_i, g_i = inp
        v_new = u_i - jnp.einsum('bhcd,bhde->bhce', w_i, state)
        o = (jnp.einsum('bhcd,bhde->bhce', q_i * jnp.exp(g_i)[..., None], state)
             + jnp.einsum('bhij,bhje->bhie', qk_i, v_new))
        g_last = g_i[..., -1]
        state = (state * jnp.exp(g_last)[..., None, None]
                 + jnp.einsum('bhcd,bhce->bhde', k_i * jnp.exp(g_last[..., None] - g_i)[..., None], v_new))
        return state, o

    xs = tuple(jnp.moveaxis(t, 2, 0) for t in (q, k, u, w, qk, gcum))
    state0 = jnp.zeros((B, H, dk, dv), jnp.float32)
    _, o = lax.scan(step, state0, xs)
    o = jnp.transpose(o, (1, 0, 3, 2, 4)).reshape(B, S, H, dv)
    return o


def deltanet_mixer(h, w_in, conv_w, a_log, dt_bias, o_norm, w_out):
    B, S, _ = h.shape
    W = DN_WIDTH
    qkv, z, a, b = jnp.split(h @ w_in, [3 * W, 4 * W, 4 * W + DN_HEADS], axis=-1)
    qkv = jax.nn.silu(causal_dwconv(qkv, conv_w)).astype(jnp.float32)
    q, k, v = jnp.split(qkv, 3, axis=-1)
    shp = (B, S, DN_HEADS, DN_HEAD_DIM)
    q, k, v = l2norm(q.reshape(shp)), l2norm(k.reshape(shp)), v.reshape(shp)
    beta = jax.nn.sigmoid(b.astype(jnp.float32))
    g = -jnp.exp(a_log.astype(jnp.float32)) * jax.nn.softplus(a.astype(jnp.float32) + dt_bias.astype(jnp.float32))
    o = gated_delta_rule_chunked(q, k, v, g, beta)
    o = rmsnorm(o, o_norm) * jax.nn.silu(z.reshape(shp).astype(jnp.float32))
    return o.reshape(B, S, W).astype(h.dtype) @ w_out


def cross_attention(h, mem_h, wq, wkv, wo):
    B, S, _ = h.shape
    M = mem_h.shape[1]
    q = (h @ wq).reshape(B, S, XA_HEADS, XA_HEAD_DIM)
    k, v = jnp.split(mem_h @ wkv, 2, axis=-1)
    k = k.reshape(B, M, XA_HEADS, XA_HEAD_DIM)
    v = v.reshape(B, M, XA_HEADS, XA_HEAD_DIM)
    s = jnp.einsum('bshd,bmhd->bhsm', q, k).astype(jnp.float32) * (XA_HEAD_DIM ** -0.5)
    p = jax.nn.softmax(s, axis=-1).astype(v.dtype)
    o = jnp.einsum('bhsm,bmhd->bshd', p, v).reshape(B, S, D_MODEL)
    return o @ wo


def _fwd_setup_inputs(seed: int = 0) -> dict:
    key = jax.random.key(seed)
    keys = iter(jax.random.split(key, 64))
    n_even = (DEPTH + 1) // 2
    n_odd = DEPTH // 2
    f32 = jnp.float32

    def nrm(shape, fan_in):
        return jax.random.normal(next(keys), shape, f32) * (fan_in ** -0.5)

    def gain(shape):
        return 1.0 + 0.02 * jax.random.normal(next(keys), shape, f32)

    def small(shape):
        return 0.01 * jax.random.normal(next(keys), shape, f32)

    x = jax.random.normal(next(keys), (BATCH, SEQ, D_MODEL), f32)
    mem = jax.random.normal(next(keys), (BATCH, N_MEM, D_MODEL), f32)
    blk = LRU_WIDTH // LRU_BLOCKS
    a0 = jax.random.uniform(next(keys), (n_even, LRU_WIDTH), f32, 0.9, 0.999)
    a_base = a0 ** (1.0 / LRU_C)
    lam = jnp.log(a_base) - jnp.log1p(-a_base)
    dt = jnp.exp(jax.random.uniform(next(keys), (n_odd, DN_HEADS), f32, math.log(1e-3), math.log(0.1)))
    dt_bias = dt + jnp.log(-jnp.expm1(-dt))
    a_log = jnp.log(jax.random.uniform(next(keys), (n_odd, DN_HEADS), f32, 1.0, 16.0))
    return {
        "x": x,
        "mem": mem,
        "ffn1_norm": gain((DEPTH, D_MODEL)),
        "ffn1_w_in": nrm((DEPTH, D_MODEL, 2 * D_FF), D_MODEL),
        "ffn1_w_out": nrm((DEPTH, D_FF, D_MODEL), D_FF),
        "mix_norm": gain((DEPTH, D_MODEL)),
        "xa_norm": gain((DEPTH, D_MODEL)),
        "xa_mem_norm": gain((DEPTH, D_MODEL)),
        "xa_wq": nrm((DEPTH, D_MODEL, D_MODEL), D_MODEL),
        "xa_wkv": nrm((DEPTH, D_MODEL, 2 * D_MODEL), D_MODEL),
        "xa_wo": nrm((DEPTH, D_MODEL, D_MODEL), D_MODEL),
        "ffn2_norm": gain((DEPTH, D_MODEL)),
        "ffn2_w_in": nrm((DEPTH, D_MODEL, 2 * D_FF), D_MODEL),
        "ffn2_w_out": nrm((DEPTH, D_FF, D_MODEL), D_FF),
        "ab_w_in": nrm((n_even, D_MODEL, AB_IN), D_MODEL),
        "lru_conv_w": nrm((n_even, CONV_K, LRU_WIDTH), CONV_K),
        "lru_conv_b": small((n_even, LRU_WIDTH)),
        "lru_w_a": nrm((n_even, LRU_BLOCKS, blk, blk), blk),
        "lru_b_a": small((n_even, LRU_WIDTH)),
        "lru_w_x": nrm((n_even, LRU_BLOCKS, blk, blk), blk),
        "lru_b_x": small((n_even, LRU_WIDTH)),
        "lru_lambda": lam,
        "ab_w_out": nrm((n_even, ATTN_WIDTH + LRU_WIDTH, D_MODEL), ATTN_WIDTH + LRU_WIDTH),
        "dn_w_in": nrm((n_odd, D_MODEL, DN_IN), D_MODEL),
        "dn_conv_w": nrm((n_odd, CONV_K, 3 * DN_WIDTH), CONV_K),
        "dn_a_log": a_log,
        "dn_dt_bias": dt_bias,
        "dn_o_norm": gain((n_odd, DN_HEAD_DIM)),
        "dn_w_out": nrm((n_odd, DN_WIDTH, D_MODEL), DN_WIDTH),
        "final_norm": gain((D_MODEL,)),
    }


def _fwd_reference(x, mem, ffn1_norm, ffn1_w_in, ffn1_w_out, mix_norm, xa_norm, xa_mem_norm,
              xa_wq, xa_wkv, xa_wo, ffn2_norm, ffn2_w_in, ffn2_w_out,
              ab_w_in, lru_conv_w, lru_conv_b, lru_w_a, lru_b_a, lru_w_x, lru_b_x, lru_lambda, ab_w_out,
              dn_w_in, dn_conv_w, dn_a_log, dn_dt_bias, dn_o_norm, dn_w_out, final_norm):
    for layer in range(DEPTH):
        x = x + 0.5 * swiglu(rmsnorm(x, ffn1_norm[layer]), ffn1_w_in[layer], ffn1_w_out[layer])
        h = rmsnorm(x, mix_norm[layer])
        j = layer // 2
        if layer % 2 == 0:
            x = x + attn_lru_mixer(h, ab_w_in[j], lru_conv_w[j], lru_conv_b[j], lru_w_a[j], lru_b_a[j],
                                   lru_w_x[j], lru_b_x[j], lru_lambda[j], ab_w_out[j])
        else:
            x = x + deltanet_mixer(h, dn_w_in[j], dn_conv_w[j], dn_a_log[j], dn_dt_bias[j],
                                   dn_o_norm[j], dn_w_out[j])
        x = x + cross_attention(rmsnorm(x, xa_norm[layer]), rmsnorm(mem, xa_mem_norm[layer]),
                                xa_wq[layer], xa_wkv[layer], xa_wo[layer])
        x = x + 0.5 * swiglu(rmsnorm(x, ffn2_norm[layer]), ffn2_w_in[layer], ffn2_w_out[layer])
    return rmsnorm(x, final_norm)


import jax as _jax
import jax.numpy as _jnp

TWIN_FORMAT = 'train_step'
FWD_PARAMS = ['x', 'mem', 'ffn1_norm', 'ffn1_w_in', 'ffn1_w_out', 'mix_norm', 'xa_norm', 'xa_mem_norm', 'xa_wq', 'xa_wkv', 'xa_wo', 'ffn2_norm', 'ffn2_w_in', 'ffn2_w_out', 'ab_w_in', 'lru_conv_w', 'lru_conv_b', 'lru_w_a', 'lru_b_a', 'lru_w_x', 'lru_b_x', 'lru_lambda', 'ab_w_out', 'dn_w_in', 'dn_conv_w', 'dn_a_log', 'dn_dt_bias', 'dn_o_norm', 'dn_w_out', 'final_norm']
TWIN_WEIGHTS = ['ffn1_norm', 'ffn1_w_in', 'ffn1_w_out', 'mix_norm', 'xa_norm', 'xa_mem_norm', 'xa_wq', 'xa_wkv', 'xa_wo', 'ffn2_norm', 'ffn2_w_in', 'ffn2_w_out', 'ab_w_in', 'lru_conv_w', 'lru_conv_b', 'lru_w_a', 'lru_b_a', 'lru_w_x', 'lru_b_x', 'lru_lambda', 'ab_w_out', 'dn_w_in', 'dn_conv_w', 'dn_a_log', 'dn_dt_bias', 'dn_o_norm', 'dn_w_out', 'final_norm']
TWIN_DIFF_INPUT = 'x'
TWIN_INPUTS = ['x', 'mem', 'ffn1_norm', 'ffn1_w_in', 'ffn1_w_out', 'mix_norm', 'xa_norm', 'xa_mem_norm', 'xa_wq', 'xa_wkv', 'xa_wo', 'ffn2_norm', 'ffn2_w_in', 'ffn2_w_out', 'ab_w_in', 'lru_conv_w', 'lru_conv_b', 'lru_w_a', 'lru_b_a', 'lru_w_x', 'lru_b_x', 'lru_lambda', 'ab_w_out', 'dn_w_in', 'dn_conv_w', 'dn_a_log', 'dn_dt_bias', 'dn_o_norm', 'dn_w_out', 'final_norm', 'loss_target', 'm_ffn1_norm', 'm_ffn1_w_in', 'm_ffn1_w_out', 'm_mix_norm', 'm_xa_norm', 'm_xa_mem_norm', 'm_xa_wq', 'm_xa_wkv', 'm_xa_wo', 'm_ffn2_norm', 'm_ffn2_w_in', 'm_ffn2_w_out', 'm_ab_w_in', 'm_lru_conv_w', 'm_lru_conv_b', 'm_lru_w_a', 'm_lru_b_a', 'm_lru_w_x', 'm_lru_b_x', 'm_lru_lambda', 'm_ab_w_out', 'm_dn_w_in', 'm_dn_conv_w', 'm_dn_a_log', 'm_dn_dt_bias', 'm_dn_o_norm', 'm_dn_w_out', 'm_final_norm', 'v_ffn1_norm', 'v_ffn1_w_in', 'v_ffn1_w_out', 'v_mix_norm', 'v_xa_norm', 'v_xa_mem_norm', 'v_xa_wq', 'v_xa_wkv', 'v_xa_wo', 'v_ffn2_norm', 'v_ffn2_w_in', 'v_ffn2_w_out', 'v_ab_w_in', 'v_lru_conv_w', 'v_lru_conv_b', 'v_lru_w_a', 'v_lru_b_a', 'v_lru_w_x', 'v_lru_b_x', 'v_lru_lambda', 'v_ab_w_out', 'v_dn_w_in', 'v_dn_conv_w', 'v_dn_a_log', 'v_dn_dt_bias', 'v_dn_o_norm', 'v_dn_w_out', 'v_final_norm']
TWIN_OUTPUTS = ['loss', 'grad_x', 'grad_ffn1_norm', 'grad_ffn1_w_in', 'grad_ffn1_w_out', 'grad_mix_norm', 'grad_xa_norm', 'grad_xa_mem_norm', 'grad_xa_wq', 'grad_xa_wkv', 'grad_xa_wo', 'grad_ffn2_norm', 'grad_ffn2_w_in', 'grad_ffn2_w_out', 'grad_ab_w_in', 'grad_lru_conv_w', 'grad_lru_conv_b', 'grad_lru_w_a', 'grad_lru_b_a', 'grad_lru_w_x', 'grad_lru_b_x', 'grad_lru_lambda', 'grad_ab_w_out', 'grad_dn_w_in', 'grad_dn_conv_w', 'grad_dn_a_log', 'grad_dn_dt_bias', 'grad_dn_o_norm', 'grad_dn_w_out', 'grad_final_norm', 'delta_ffn1_norm', 'delta_ffn1_w_in', 'delta_ffn1_w_out', 'delta_mix_norm', 'delta_xa_norm', 'delta_xa_mem_norm', 'delta_xa_wq', 'delta_xa_wkv', 'delta_xa_wo', 'delta_ffn2_norm', 'delta_ffn2_w_in', 'delta_ffn2_w_out', 'delta_ab_w_in', 'delta_lru_conv_w', 'delta_lru_conv_b', 'delta_lru_w_a', 'delta_lru_b_a', 'delta_lru_w_x', 'delta_lru_b_x', 'delta_lru_lambda', 'delta_ab_w_out', 'delta_dn_w_in', 'delta_dn_conv_w', 'delta_dn_a_log', 'delta_dn_dt_bias', 'delta_dn_o_norm', 'delta_dn_w_out', 'delta_final_norm', 'new_m_ffn1_norm', 'new_m_ffn1_w_in', 'new_m_ffn1_w_out', 'new_m_mix_norm', 'new_m_xa_norm', 'new_m_xa_mem_norm', 'new_m_xa_wq', 'new_m_xa_wkv', 'new_m_xa_wo', 'new_m_ffn2_norm', 'new_m_ffn2_w_in', 'new_m_ffn2_w_out', 'new_m_ab_w_in', 'new_m_lru_conv_w', 'new_m_lru_conv_b', 'new_m_lru_w_a', 'new_m_lru_b_a', 'new_m_lru_w_x', 'new_m_lru_b_x', 'new_m_lru_lambda', 'new_m_ab_w_out', 'new_m_dn_w_in', 'new_m_dn_conv_w', 'new_m_dn_a_log', 'new_m_dn_dt_bias', 'new_m_dn_o_norm', 'new_m_dn_w_out', 'new_m_final_norm', 'new_v_ffn1_norm', 'new_v_ffn1_w_in', 'new_v_ffn1_w_out', 'new_v_mix_norm', 'new_v_xa_norm', 'new_v_xa_mem_norm', 'new_v_xa_wq', 'new_v_xa_wkv', 'new_v_xa_wo', 'new_v_ffn2_norm', 'new_v_ffn2_w_in', 'new_v_ffn2_w_out', 'new_v_ab_w_in', 'new_v_lru_conv_w', 'new_v_lru_conv_b', 'new_v_lru_w_a', 'new_v_lru_b_a', 'new_v_lru_w_x', 'new_v_lru_b_x', 'new_v_lru_lambda', 'new_v_ab_w_out', 'new_v_dn_w_in', 'new_v_dn_conv_w', 'new_v_dn_a_log', 'new_v_dn_dt_bias', 'new_v_dn_o_norm', 'new_v_dn_w_out', 'new_v_final_norm']
TWIN_LEAF_KINDS = {'loss': 'loss', 'grad_x': 'grad_x', 'grad_ffn1_norm': 'grad_w', 'grad_ffn1_w_in': 'grad_w', 'grad_ffn1_w_out': 'grad_w', 'grad_mix_norm': 'grad_w', 'grad_xa_norm': 'grad_w', 'grad_xa_mem_norm': 'grad_w', 'grad_xa_wq': 'grad_w', 'grad_xa_wkv': 'grad_w', 'grad_xa_wo': 'grad_w', 'grad_ffn2_norm': 'grad_w', 'grad_ffn2_w_in': 'grad_w', 'grad_ffn2_w_out': 'grad_w', 'grad_ab_w_in': 'grad_w', 'grad_lru_conv_w': 'grad_w', 'grad_lru_conv_b': 'grad_w', 'grad_lru_w_a': 'grad_w', 'grad_lru_b_a': 'grad_w', 'grad_lru_w_x': 'grad_w', 'grad_lru_b_x': 'grad_w', 'grad_lru_lambda': 'grad_w', 'grad_ab_w_out': 'grad_w', 'grad_dn_w_in': 'grad_w', 'grad_dn_conv_w': 'grad_w', 'grad_dn_a_log': 'grad_w', 'grad_dn_dt_bias': 'grad_w', 'grad_dn_o_norm': 'grad_w', 'grad_dn_w_out': 'grad_w', 'grad_final_norm': 'grad_w', 'delta_ffn1_norm': 'delta_w', 'delta_ffn1_w_in': 'delta_w', 'delta_ffn1_w_out': 'delta_w', 'delta_mix_norm': 'delta_w', 'delta_xa_norm': 'delta_w', 'delta_xa_mem_norm': 'delta_w', 'delta_xa_wq': 'delta_w', 'delta_xa_wkv': 'delta_w', 'delta_xa_wo': 'delta_w', 'delta_ffn2_norm': 'delta_w', 'delta_ffn2_w_in': 'delta_w', 'delta_ffn2_w_out': 'delta_w', 'delta_ab_w_in': 'delta_w', 'delta_lru_conv_w': 'delta_w', 'delta_lru_conv_b': 'delta_w', 'delta_lru_w_a': 'delta_w', 'delta_lru_b_a': 'delta_w', 'delta_lru_w_x': 'delta_w', 'delta_lru_b_x': 'delta_w', 'delta_lru_lambda': 'delta_w', 'delta_ab_w_out': 'delta_w', 'delta_dn_w_in': 'delta_w', 'delta_dn_conv_w': 'delta_w', 'delta_dn_a_log': 'delta_w', 'delta_dn_dt_bias': 'delta_w', 'delta_dn_o_norm': 'delta_w', 'delta_dn_w_out': 'delta_w', 'delta_final_norm': 'delta_w', 'new_m_ffn1_norm': 'new_m', 'new_m_ffn1_w_in': 'new_m', 'new_m_ffn1_w_out': 'new_m', 'new_m_mix_norm': 'new_m', 'new_m_xa_norm': 'new_m', 'new_m_xa_mem_norm': 'new_m', 'new_m_xa_wq': 'new_m', 'new_m_xa_wkv': 'new_m', 'new_m_xa_wo': 'new_m', 'new_m_ffn2_norm': 'new_m', 'new_m_ffn2_w_in': 'new_m', 'new_m_ffn2_w_out': 'new_m', 'new_m_ab_w_in': 'new_m', 'new_m_lru_conv_w': 'new_m', 'new_m_lru_conv_b': 'new_m', 'new_m_lru_w_a': 'new_m', 'new_m_lru_b_a': 'new_m', 'new_m_lru_w_x': 'new_m', 'new_m_lru_b_x': 'new_m', 'new_m_lru_lambda': 'new_m', 'new_m_ab_w_out': 'new_m', 'new_m_dn_w_in': 'new_m', 'new_m_dn_conv_w': 'new_m', 'new_m_dn_a_log': 'new_m', 'new_m_dn_dt_bias': 'new_m', 'new_m_dn_o_norm': 'new_m', 'new_m_dn_w_out': 'new_m', 'new_m_final_norm': 'new_m', 'new_v_ffn1_norm': 'new_v', 'new_v_ffn1_w_in': 'new_v', 'new_v_ffn1_w_out': 'new_v', 'new_v_mix_norm': 'new_v', 'new_v_xa_norm': 'new_v', 'new_v_xa_mem_norm': 'new_v', 'new_v_xa_wq': 'new_v', 'new_v_xa_wkv': 'new_v', 'new_v_xa_wo': 'new_v', 'new_v_ffn2_norm': 'new_v', 'new_v_ffn2_w_in': 'new_v', 'new_v_ffn2_w_out': 'new_v', 'new_v_ab_w_in': 'new_v', 'new_v_lru_conv_w': 'new_v', 'new_v_lru_conv_b': 'new_v', 'new_v_lru_w_a': 'new_v', 'new_v_lru_b_a': 'new_v', 'new_v_lru_w_x': 'new_v', 'new_v_lru_b_x': 'new_v', 'new_v_lru_lambda': 'new_v', 'new_v_ab_w_out': 'new_v', 'new_v_dn_w_in': 'new_v', 'new_v_dn_conv_w': 'new_v', 'new_v_dn_a_log': 'new_v', 'new_v_dn_dt_bias': 'new_v', 'new_v_dn_o_norm': 'new_v', 'new_v_dn_w_out': 'new_v', 'new_v_final_norm': 'new_v'}


def _forward(args):
    return _fwd_reference(*[args[k] for k in FWD_PARAMS])


def _output_shape():
    def fwd():
        inp = _fwd_setup_inputs(0)
        return _fwd_reference(*[inp[k] for k in FWD_PARAMS])
    out = _jax.eval_shape(fwd)
    return out.shape, out.dtype

N_MICROBATCH = 1
ADAM_LR = 0.001
ADAM_B1 = 0.9
ADAM_B2 = 0.999
ADAM_EPS = 1e-08
ADAM_WD = 0.01
ADAM_STEP = 10
PER_EXAMPLE_BATCH_AXIS = {'x': 0, 'mem': 0, 'loss_target': 0}
SHARED_INPUTS = []
_WEIGHT_DTYPES = {'ffn1_norm': _jnp.float32, 'ffn1_w_in': _jnp.float32, 'ffn1_w_out': _jnp.float32, 'mix_norm': _jnp.float32, 'xa_norm': _jnp.float32, 'xa_mem_norm': _jnp.float32, 'xa_wq': _jnp.float32, 'xa_wkv': _jnp.float32, 'xa_wo': _jnp.float32, 'ffn2_norm': _jnp.float32, 'ffn2_w_in': _jnp.float32, 'ffn2_w_out': _jnp.float32, 'ab_w_in': _jnp.float32, 'lru_conv_w': _jnp.float32, 'lru_conv_b': _jnp.float32, 'lru_w_a': _jnp.float32, 'lru_b_a': _jnp.float32, 'lru_w_x': _jnp.float32, 'lru_b_x': _jnp.float32, 'lru_lambda': _jnp.float32, 'ab_w_out': _jnp.float32, 'dn_w_in': _jnp.float32, 'dn_conv_w': _jnp.float32, 'dn_a_log': _jnp.float32, 'dn_dt_bias': _jnp.float32, 'dn_o_norm': _jnp.float32, 'dn_w_out': _jnp.float32, 'final_norm': _jnp.float32}
MOMENT_SCALE = {'ffn1_norm': 1.781375e-01, 'ffn1_w_in': 7.527434e-02, 'ffn1_w_out': 1.227497e-01, 'mix_norm': 2.423385e-01, 'xa_norm': 3.802921e-02, 'xa_mem_norm': 5.526805e-02, 'xa_wq': 3.686826e-02, 'xa_wkv': 3.764760e-02, 'xa_wo': 3.825731e-02, 'ffn2_norm': 1.458146e-01, 'ffn2_w_in': 6.071379e-02, 'ffn2_w_out': 9.909988e-02, 'ab_w_in': 1.187326e-01, 'lru_conv_w': 2.289494e-01, 'lru_conv_b': 2.712229e+00, 'lru_w_a': 1.205520e-01, 'lru_b_a': 6.958436e-02, 'lru_w_x': 2.182212e-01, 'lru_b_x': 8.334252e-02, 'lru_lambda': 8.855800e-02, 'ab_w_out': 1.480623e-01, 'dn_w_in': 1.389709e-01, 'dn_conv_w': 1.338689e-01, 'dn_a_log': 7.336078e-01, 'dn_dt_bias': 6.776055e-01, 'dn_o_norm': 4.745402e-01, 'dn_w_out': 1.637326e-01, 'final_norm': 1.280013e+02}


def _to_microbatches(a, axis):
    t = _jnp.moveaxis(a, axis, 0)
    t = t.reshape((N_MICROBATCH, t.shape[0] // N_MICROBATCH) + t.shape[1:])
    return _jnp.moveaxis(t, 1, axis + 1)


def setup_inputs(seed: int = 0) -> dict:
    inp = _fwd_setup_inputs(seed)
    key = _jax.random.fold_in(_jax.random.key(seed), 7919)
    shape, _ = _output_shape()
    out = dict(inp)
    out["loss_target"] = _jax.random.normal(_jax.random.fold_in(key, 0), shape, _jnp.float32)
    for i, name in enumerate(TWIN_WEIGHTS):
        w = inp[name].astype(_jnp.float32)
        if MOMENT_SCALE is None:
            s = _jnp.sqrt(_jnp.mean(_jnp.square(w)) + 1e-30)
        else:
            s = MOMENT_SCALE[name]
        km, kv = _jax.random.split(_jax.random.fold_in(key, i + 1))
        out[name] = w
        out["m_" + name] = s * _jax.random.normal(km, w.shape, _jnp.float32)
        out["v_" + name] = (s * s) * _jax.random.uniform(kv, w.shape, _jnp.float32, 0.5, 1.5)
    if N_MICROBATCH > 1:
        for name, axis in PER_EXAMPLE_BATCH_AXIS.items():
            out[name] = _to_microbatches(out[name], axis)
    return {'x': out['x'], 'mem': out['mem'], 'ffn1_norm': out['ffn1_norm'], 'ffn1_w_in': out['ffn1_w_in'], 'ffn1_w_out': out['ffn1_w_out'], 'mix_norm': out['mix_norm'], 'xa_norm': out['xa_norm'], 'xa_mem_norm': out['xa_mem_norm'], 'xa_wq': out['xa_wq'], 'xa_wkv': out['xa_wkv'], 'xa_wo': out['xa_wo'], 'ffn2_norm': out['ffn2_norm'], 'ffn2_w_in': out['ffn2_w_in'], 'ffn2_w_out': out['ffn2_w_out'], 'ab_w_in': out['ab_w_in'], 'lru_conv_w': out['lru_conv_w'], 'lru_conv_b': out['lru_conv_b'], 'lru_w_a': out['lru_w_a'], 'lru_b_a': out['lru_b_a'], 'lru_w_x': out['lru_w_x'], 'lru_b_x': out['lru_b_x'], 'lru_lambda': out['lru_lambda'], 'ab_w_out': out['ab_w_out'], 'dn_w_in': out['dn_w_in'], 'dn_conv_w': out['dn_conv_w'], 'dn_a_log': out['dn_a_log'], 'dn_dt_bias': out['dn_dt_bias'], 'dn_o_norm': out['dn_o_norm'], 'dn_w_out': out['dn_w_out'], 'final_norm': out['final_norm'], 'loss_target': out['loss_target'], 'm_ffn1_norm': out['m_ffn1_norm'], 'm_ffn1_w_in': out['m_ffn1_w_in'], 'm_ffn1_w_out': out['m_ffn1_w_out'], 'm_mix_norm': out['m_mix_norm'], 'm_xa_norm': out['m_xa_norm'], 'm_xa_mem_norm': out['m_xa_mem_norm'], 'm_xa_wq': out['m_xa_wq'], 'm_xa_wkv': out['m_xa_wkv'], 'm_xa_wo': out['m_xa_wo'], 'm_ffn2_norm': out['m_ffn2_norm'], 'm_ffn2_w_in': out['m_ffn2_w_in'], 'm_ffn2_w_out': out['m_ffn2_w_out'], 'm_ab_w_in': out['m_ab_w_in'], 'm_lru_conv_w': out['m_lru_conv_w'], 'm_lru_conv_b': out['m_lru_conv_b'], 'm_lru_w_a': out['m_lru_w_a'], 'm_lru_b_a': out['m_lru_b_a'], 'm_lru_w_x': out['m_lru_w_x'], 'm_lru_b_x': out['m_lru_b_x'], 'm_lru_lambda': out['m_lru_lambda'], 'm_ab_w_out': out['m_ab_w_out'], 'm_dn_w_in': out['m_dn_w_in'], 'm_dn_conv_w': out['m_dn_conv_w'], 'm_dn_a_log': out['m_dn_a_log'], 'm_dn_dt_bias': out['m_dn_dt_bias'], 'm_dn_o_norm': out['m_dn_o_norm'], 'm_dn_w_out': out['m_dn_w_out'], 'm_final_norm': out['m_final_norm'], 'v_ffn1_norm': out['v_ffn1_norm'], 'v_ffn1_w_in': out['v_ffn1_w_in'], 'v_ffn1_w_out': out['v_ffn1_w_out'], 'v_mix_norm': out['v_mix_norm'], 'v_xa_norm': out['v_xa_norm'], 'v_xa_mem_norm': out['v_xa_mem_norm'], 'v_xa_wq': out['v_xa_wq'], 'v_xa_wkv': out['v_xa_wkv'], 'v_xa_wo': out['v_xa_wo'], 'v_ffn2_norm': out['v_ffn2_norm'], 'v_ffn2_w_in': out['v_ffn2_w_in'], 'v_ffn2_w_out': out['v_ffn2_w_out'], 'v_ab_w_in': out['v_ab_w_in'], 'v_lru_conv_w': out['v_lru_conv_w'], 'v_lru_conv_b': out['v_lru_conv_b'], 'v_lru_w_a': out['v_lru_w_a'], 'v_lru_b_a': out['v_lru_b_a'], 'v_lru_w_x': out['v_lru_w_x'], 'v_lru_b_x': out['v_lru_b_x'], 'v_lru_lambda': out['v_lru_lambda'], 'v_ab_w_out': out['v_ab_w_out'], 'v_dn_w_in': out['v_dn_w_in'], 'v_dn_conv_w': out['v_dn_conv_w'], 'v_dn_a_log': out['v_dn_a_log'], 'v_dn_dt_bias': out['v_dn_dt_bias'], 'v_dn_o_norm': out['v_dn_o_norm'], 'v_dn_w_out': out['v_dn_w_out'], 'v_final_norm': out['v_final_norm']}


def _loss(weights, diff, rest, loss_target):
    with _jax.named_scope("forward"):
        args = {**rest, TWIN_DIFF_INPUT: diff, **{k: w.astype(_WEIGHT_DTYPES[k]) for k, w in weights.items()}}
        y = _forward(args)
    with _jax.named_scope("loss_head"):
        err = _jnp.square(y.astype(_jnp.float32) - loss_target)
        return 0.5 * _jnp.sum(_jnp.mean(err, axis=-1)) if err.ndim else 0.5 * err


def _adamw(w, g, m, v):
    m = ADAM_B1 * m + (1.0 - ADAM_B1) * g
    v = ADAM_B2 * v + (1.0 - ADAM_B2) * _jnp.square(g)
    m_hat = m / (1.0 - ADAM_B1 ** ADAM_STEP)
    v_hat = v / (1.0 - ADAM_B2 ** ADAM_STEP)
    delta = -ADAM_LR * (m_hat / (_jnp.sqrt(v_hat) + ADAM_EPS) + ADAM_WD * w)
    return delta, m, v


def reference(x, mem, ffn1_norm, ffn1_w_in, ffn1_w_out, mix_norm, xa_norm, xa_mem_norm, xa_wq, xa_wkv, xa_wo, ffn2_norm, ffn2_w_in, ffn2_w_out, ab_w_in, lru_conv_w, lru_conv_b, lru_w_a, lru_b_a, lru_w_x, lru_b_x, lru_lambda, ab_w_out, dn_w_in, dn_conv_w, dn_a_log, dn_dt_bias, dn_o_norm, dn_w_out, final_norm, loss_target, m_ffn1_norm, m_ffn1_w_in, m_ffn1_w_out, m_mix_norm, m_xa_norm, m_xa_mem_norm, m_xa_wq, m_xa_wkv, m_xa_wo, m_ffn2_norm, m_ffn2_w_in, m_ffn2_w_out, m_ab_w_in, m_lru_conv_w, m_lru_conv_b, m_lru_w_a, m_lru_b_a, m_lru_w_x, m_lru_b_x, m_lru_lambda, m_ab_w_out, m_dn_w_in, m_dn_conv_w, m_dn_a_log, m_dn_dt_bias, m_dn_o_norm, m_dn_w_out, m_final_norm, v_ffn1_norm, v_ffn1_w_in, v_ffn1_w_out, v_mix_norm, v_xa_norm, v_xa_mem_norm, v_xa_wq, v_xa_wkv, v_xa_wo, v_ffn2_norm, v_ffn2_w_in, v_ffn2_w_out, v_ab_w_in, v_lru_conv_w, v_lru_conv_b, v_lru_w_a, v_lru_b_a, v_lru_w_x, v_lru_b_x, v_lru_lambda, v_ab_w_out, v_dn_w_in, v_dn_conv_w, v_dn_a_log, v_dn_dt_bias, v_dn_o_norm, v_dn_w_out, v_final_norm):
    given = dict(x=x, mem=mem, ffn1_norm=ffn1_norm, ffn1_w_in=ffn1_w_in, ffn1_w_out=ffn1_w_out, mix_norm=mix_norm, xa_norm=xa_norm, xa_mem_norm=xa_mem_norm, xa_wq=xa_wq, xa_wkv=xa_wkv, xa_wo=xa_wo, ffn2_norm=ffn2_norm, ffn2_w_in=ffn2_w_in, ffn2_w_out=ffn2_w_out, ab_w_in=ab_w_in, lru_conv_w=lru_conv_w, lru_conv_b=lru_conv_b, lru_w_a=lru_w_a, lru_b_a=lru_b_a, lru_w_x=lru_w_x, lru_b_x=lru_b_x, lru_lambda=lru_lambda, ab_w_out=ab_w_out, dn_w_in=dn_w_in, dn_conv_w=dn_conv_w, dn_a_log=dn_a_log, dn_dt_bias=dn_dt_bias, dn_o_norm=dn_o_norm, dn_w_out=dn_w_out, final_norm=final_norm, loss_target=loss_target, m_ffn1_norm=m_ffn1_norm, m_ffn1_w_in=m_ffn1_w_in, m_ffn1_w_out=m_ffn1_w_out, m_mix_norm=m_mix_norm, m_xa_norm=m_xa_norm, m_xa_mem_norm=m_xa_mem_norm, m_xa_wq=m_xa_wq, m_xa_wkv=m_xa_wkv, m_xa_wo=m_xa_wo, m_ffn2_norm=m_ffn2_norm, m_ffn2_w_in=m_ffn2_w_in, m_ffn2_w_out=m_ffn2_w_out, m_ab_w_in=m_ab_w_in, m_lru_conv_w=m_lru_conv_w, m_lru_conv_b=m_lru_conv_b, m_lru_w_a=m_lru_w_a, m_lru_b_a=m_lru_b_a, m_lru_w_x=m_lru_w_x, m_lru_b_x=m_lru_b_x, m_lru_lambda=m_lru_lambda, m_ab_w_out=m_ab_w_out, m_dn_w_in=m_dn_w_in, m_dn_conv_w=m_dn_conv_w, m_dn_a_log=m_dn_a_log, m_dn_dt_bias=m_dn_dt_bias, m_dn_o_norm=m_dn_o_norm, m_dn_w_out=m_dn_w_out, m_final_norm=m_final_norm, v_ffn1_norm=v_ffn1_norm, v_ffn1_w_in=v_ffn1_w_in, v_ffn1_w_out=v_ffn1_w_out, v_mix_norm=v_mix_norm, v_xa_norm=v_xa_norm, v_xa_mem_norm=v_xa_mem_norm, v_xa_wq=v_xa_wq, v_xa_wkv=v_xa_wkv, v_xa_wo=v_xa_wo, v_ffn2_norm=v_ffn2_norm, v_ffn2_w_in=v_ffn2_w_in, v_ffn2_w_out=v_ffn2_w_out, v_ab_w_in=v_ab_w_in, v_lru_conv_w=v_lru_conv_w, v_lru_conv_b=v_lru_conv_b, v_lru_w_a=v_lru_w_a, v_lru_b_a=v_lru_b_a, v_lru_w_x=v_lru_w_x, v_lru_b_x=v_lru_b_x, v_lru_lambda=v_lru_lambda, v_ab_w_out=v_ab_w_out, v_dn_w_in=v_dn_w_in, v_dn_conv_w=v_dn_conv_w, v_dn_a_log=v_dn_a_log, v_dn_dt_bias=v_dn_dt_bias, v_dn_o_norm=v_dn_o_norm, v_dn_w_out=v_dn_w_out, v_final_norm=v_final_norm)
    weights = {n: given[n] for n in TWIN_WEIGHTS}
    shared = {n: given[n] for n in SHARED_INPUTS}
    per_example = {n: given[n] for n in ['x', 'mem']}
    grad_fn = _jax.value_and_grad(_loss, argnums=(0, 1))

    def one_microbatch(ex, loss_target):
        ex = dict(ex)
        diff = ex.pop(TWIN_DIFF_INPUT)
        return grad_fn(weights, diff, {**shared, **ex}, loss_target)

    if N_MICROBATCH == 1:
        loss, (grad_w, grad_x) = one_microbatch(per_example, given["loss_target"])
    else:
        def body(carry, xs):
            loss_sum, grad_sum = carry
            l_k, (gw_k, gx_k) = one_microbatch(xs[0], xs[1])
            with _jax.named_scope("update"):
                return (loss_sum + l_k, _jax.tree.map(_jnp.add, grad_sum, gw_k)), gx_k

        init = (_jnp.zeros((), _jnp.float32), _jax.tree.map(_jnp.zeros_like, weights))
        (loss, grad_w), grad_x = _jax.lax.scan(body, init, (per_example, given["loss_target"]))
    with _jax.named_scope("update"):
        delta_w, new_m, new_v = {}, {}, {}
        for n in TWIN_WEIGHTS:
            delta_w[n], new_m[n], new_v[n] = _adamw(weights[n], grad_w[n], given["m_" + n], given["v_" + n])
    return (loss, grad_x, *[grad_w[n] for n in TWIN_WEIGHTS], *[delta_w[n] for n in TWIN_WEIGHTS],
            *[new_m[n] for n in TWIN_WEIGHTS], *[new_v[n] for n in TWIN_WEIGHTS])
```

```python
import functools
import math

import jax
import jax.numpy as jnp
from jax import lax
from jax.experimental import pallas as pl
from jax.experimental.pallas import tpu as pltpu

F32 = jnp.float32
MXU_DTYPE = jnp.bfloat16
VMEM_LIMIT_BYTES = 48 * 1024 * 1024
LANES = 128
SUBLANES = 8

NORM_EPS = 1e-6
CONV_K = 4
ATTN_PAIRS = 4
ATTN_HEAD_DIM = 64
ATTN_WIDTH = 512
ATTN_BLOCK = 128
DILATIONS = (1, 4, 16)
LRU_WIDTH = 512
LRU_BLOCKS = 8
LRU_C = 8.0
DN_HEADS = 8
DN_HEAD_DIM = 128
DN_WIDTH = 1024
DN_CHUNK = 64
XA_HEADS = 4
XA_HEAD_DIM = 256
D_FF = 2816
ADAM_LR, ADAM_B1, ADAM_B2, ADAM_EPS, ADAM_WD, ADAM_STEP = 0.001, 0.9, 0.999, 1e-08, 0.01, 10

MESH = pl.DeviceIdType.MESH


def _tile(n, prefs):
    for p in prefs:
        if n % p == 0:
            return p
    return n


def _params(*sem):
    return pltpu.CompilerParams(dimension_semantics=sem, vmem_limit_bytes=VMEM_LIMIT_BYTES)


def _dg(a, b, dims, hi=False):
    if hi:
        return lax.dot_general(a, b, (dims, ((), ())), precision=lax.Precision.HIGHEST, preferred_element_type=F32)
    return lax.dot_general(a.astype(MXU_DTYPE), b.astype(MXU_DTYPE), (dims, ((), ())), preferred_element_type=F32)


def _make_dot(hi):
    @jax.custom_vjp
    def dot(a, b):
        return _dg(a, b, ((1,), (0,)), hi)

    def fwd(a, b):
        return dot(a, b), (a, b)

    def bwd(r, g):
        a, b = r
        return _dg(g, b, ((1,), (1,)), hi).astype(a.dtype), _dg(a, g, ((0,), (0,)), hi).astype(b.dtype)

    dot.defvjp(fwd, bwd)

    @jax.custom_vjp
    def dot_nt(a, b):
        return _dg(a, b, ((1,), (1,)), hi)

    def fwd_nt(a, b):
        return dot_nt(a, b), (a, b)

    def bwd_nt(r, g):
        a, b = r
        return _dg(g, b, ((1,), (0,)), hi).astype(a.dtype), _dg(g, a, ((0,), (0,)), hi).astype(b.dtype)

    dot_nt.defvjp(fwd_nt, bwd_nt)

    @jax.custom_vjp
    def dot_tn(a, b):
        return _dg(a, b, ((0,), (0,)), hi)

    def fwd_tn(a, b):
        return dot_tn(a, b), (a, b)

    def bwd_tn(r, g):
        a, b = r
        return _dg(b, g, ((1,), (1,)), hi).astype(a.dtype), _dg(a, g, ((1,), (0,)), hi).astype(b.dtype)

    dot_tn.defvjp(fwd_tn, bwd_tn)
    return dot, dot_nt, dot_tn


_bdot, _bdot_nt, _bdot_tn = _make_dot(False)
_hdot, _hdot_nt, _hdot_tn = _make_dot(True)


def _log1p(t):
    return jnp.where(t < 0.01, t * (1.0 - t * (0.5 - t * (1.0 / 3.0))), jnp.log(1.0 + t))


def _neg_expm1(y):
    series = -y * (1.0 + 0.5 * y * (1.0 + (1.0 / 3.0) * y * (1.0 + 0.25 * y)))
    return jnp.where(y > -0.01, series, 1.0 - jnp.exp(y))


def _softplus(x):
    return jnp.maximum(x, 0.0) + _log1p(jnp.exp(-jnp.abs(x)))


def _sigmoid(x):
    return 1.0 / (1.0 + jnp.exp(-x))


def _silu(x):
    return x * _sigmoid(x)


def _gelu(x):
    return 0.5 * x * (1.0 + jnp.tanh(0.7978845608028654 * (x + 0.044715 * x * x * x)))


def _rows(shape):
    return lax.broadcasted_iota(jnp.int32, shape, 0)


def _cols(shape):
    return lax.broadcasted_iota(jnp.int32, shape, 1)


def _mm(a, b, *, mode="nn", out_dtype=F32, res=None, scale=1.0, name):
    if mode == "nn":
        (m, k), (k2, n) = a.shape, b.shape
    elif mode == "nt":
        (m, k), (n, k2) = a.shape, b.shape
    else:
        (k, m), (k2, n) = a.shape, b.shape
    assert k == k2, (a.shape, b.shape, mode)
    tm = _tile(m, (512, 256, 128))
    tn = _tile(n, (512, 256, 128))
    tk = _tile(k, (1024, 512, 256, 128))
    nk = k // tk
    dims = {"nn": ((1,), (0,)), "nt": ((1,), (1,)), "tn": ((0,), (0,))}[mode]

    def body(*refs):
        if res is None:
            a_ref, b_ref, o_ref, acc = refs
        else:
            a_ref, b_ref, r_ref, o_ref, acc = refs
        kk = pl.program_id(2)

        @pl.when(kk == 0)
        def _():
            acc[...] = jnp.zeros_like(acc)

        acc[...] += _dg(a_ref[...], b_ref[...], dims)

        @pl.when(kk == nk - 1)
        def _():
            r = acc[...]
            if scale != 1.0:
                r = r * scale
            if res is not None:
                r = r_ref[...] + r
            o_ref[...] = r.astype(out_dtype)

    a_spec = pl.BlockSpec((tk, tm), lambda i, j, kk: (kk, i)) if mode == "tn" else pl.BlockSpec((tm, tk), lambda i, j, kk: (i, kk))
    b_spec = pl.BlockSpec((tn, tk), lambda i, j, kk: (j, kk)) if mode == "nt" else pl.BlockSpec((tk, tn), lambda i, j, kk: (kk, j))
    o_spec = pl.BlockSpec((tm, tn), lambda i, j, kk: (i, j))
    in_specs = [a_spec, b_spec] + ([o_spec] if res is not None else [])
    args = (a, b) + ((res,) if res is not None else ())
    return pl.pallas_call(
        body, name=name, grid=(m // tm, n // tn, nk), in_specs=in_specs, out_specs=o_spec,
        out_shape=jax.ShapeDtypeStruct((m, n), out_dtype), scratch_shapes=[pltpu.VMEM((tm, tn), F32)],
        compiler_params=_params("parallel", "parallel", "arbitrary"),
    )(*args)


def _rms_fwd(x, g, *, name):
    s, d = x.shape
    tm = _tile(s, (512, 256))

    def body(x_ref, g_ref, o_ref):
        xv = x_ref[...]
        r = lax.rsqrt(jnp.mean(xv * xv, axis=-1, keepdims=True) + NORM_EPS)
        o_ref[...] = (xv * r * g_ref[...]).astype(o_ref.dtype)

    return pl.pallas_call(
        body, name=name, grid=(s // tm,),
        in_specs=[pl.BlockSpec((tm, d), lambda i: (i, 0)), pl.BlockSpec((1, d), lambda i: (0, 0))],
        out_specs=pl.BlockSpec((tm, d), lambda i: (i, 0)), out_shape=jax.ShapeDtypeStruct((s, d), MXU_DTYPE),
        compiler_params=_params("parallel"),
    )(x, g.reshape(1, d))


def _rms_bwd(x, g, dh, dres, *, name):
    s, d = x.shape
    tm = _tile(s, (512, 256))

    def body(x_ref, g_ref, dh_ref, dr_ref, dx_ref, dg_ref):
        xv = x_ref[...]
        r = lax.rsqrt(jnp.mean(xv * xv, axis=-1, keepdims=True) + NORM_EPS)
        xh = xv * r
        dhv = dh_ref[...].astype(F32)
        dxh = dhv * g_ref[...]
        dx = r * (dxh - xh * jnp.mean(dxh * xh, axis=-1, keepdims=True))
        dx_ref[...] = dr_ref[...] + dx

        @pl.when(pl.program_id(0) == 0)
        def _():
            dg_ref[...] = jnp.zeros_like(dg_ref)

        dg_ref[...] += jnp.sum(dhv * xh, axis=0, keepdims=True)

    row = pl.BlockSpec((tm, d), lambda i: (i, 0))
    vec = pl.BlockSpec((1, d), lambda i: (0, 0))
    return pl.pallas_call(
        body, name=name, grid=(s // tm,), in_specs=[row, vec, row, row], out_specs=[row, vec],
        out_shape=[jax.ShapeDtypeStruct((s, d), F32), jax.ShapeDtypeStruct((1, d), F32)],
        compiler_params=_params("arbitrary"),
    )(x, g.reshape(1, d), dh, dres)


def _swiglu_fwd(u, *, name):
    s, f2 = u.shape
    f = f2 // 2
    tm, tn = _tile(s, (512, 256)), _tile(f, (256, 128))
    nj = f // tn

    def body(g_ref, u_ref, o_ref):
        g, up = g_ref[...].astype(F32), u_ref[...].astype(F32)
        o_ref[...] = (_silu(g) * up).astype(o_ref.dtype)

    return pl.pallas_call(
        body, name=name, grid=(s // tm, nj),
        in_specs=[pl.BlockSpec((tm, tn), lambda i, j: (i, j)), pl.BlockSpec((tm, tn), lambda i, j: (i, j + nj))],
        out_specs=pl.BlockSpec((tm, tn), lambda i, j: (i, j)), out_shape=jax.ShapeDtypeStruct((s, f), MXU_DTYPE),
        compiler_params=_params("parallel", "parallel"),
    )(u, u)


def _swiglu_bwd(u, dact, *, name):
    s, f2 = u.shape
    f = f2 // 2
    tm, tn = _tile(s, (512, 256)), _tile(f, (256, 128))
    nj = f // tn

    def body(g_ref, u_ref, d_ref, o_ref):
        g, up, d = g_ref[...].astype(F32), u_ref[...].astype(F32), d_ref[...].astype(F32)
        sg = _sigmoid(g)
        part = pl.program_id(1)
        dgate = d * up * sg * (1.0 + g * (1.0 - sg))
        dup = d * g * sg
        o_ref[...] = jnp.where(part == 0, dgate, dup).astype(o_ref.dtype)

    return pl.pallas_call(
        body, name=name, grid=(s // tm, 2, nj),
        in_specs=[pl.BlockSpec((tm, tn), lambda i, p, j: (i, j)), pl.BlockSpec((tm, tn), lambda i, p, j: (i, j + nj)),
                  pl.BlockSpec((tm, tn), lambda i, p, j: (i, j))],
        out_specs=pl.BlockSpec((tm, tn), lambda i, p, j: (i, p * nj + j)), out_shape=jax.ShapeDtypeStruct((s, f2), MXU_DTYPE),
        compiler_params=_params("parallel", "parallel", "parallel"),
    )(u, u, dact)


def _ffn_fwd(x, g, w_in, w_out, tag):
    h = _rms_fwd(x, g, name=f"{tag}_norm")
    u = _mm(h, w_in, out_dtype=MXU_DTYPE, name=f"{tag}_in")
    act = _swiglu_fwd(u, name=f"{tag}_act")
    y = _mm(act, w_out, res=x, scale=0.5, name=f"{tag}_out")
    return y, (x, h, u, act)


def _ffn_bwd(saved, g, w_in, w_out, dy, tag):
    x, h, u, act = saved
    dact = _mm(dy, w_out, mode="nt", scale=0.5, out_dtype=MXU_DTYPE, name=f"{tag}_dact")
    dw_out = _mm(act, dy, mode="tn", scale=0.5, name=f"{tag}_dwout")
    du = _swiglu_bwd(u, dact, name=f"{tag}_dact2")
    dw_in = _mm(h, du, mode="tn", name=f"{tag}_dwin")
    dh = _mm(du, w_in, mode="nt", name=f"{tag}_dh")
    dx, dg = _rms_bwd(x, g, dh, dy, name=f"{tag}_dnorm")
    return dx, dg[0], dw_in, dw_out


ATTN_SCALE = ATTN_HEAD_DIM ** -0.5
NEG_BIG = -1e30
PROJ_AB_BLOCKS = 5


def _head_masks():
    lane = _cols((ATTN_BLOCK, LANES))
    return lane < ATTN_HEAD_DIM, lane >= ATTN_HEAD_DIM


def _band_masks(has_prev):
    qi, kj = _rows((ATTN_BLOCK, ATTN_BLOCK)), _cols((ATTN_BLOCK, ATTN_BLOCK))
    return (kj >= qi) & has_prev, kj <= qi


def _dattn_fwd(proj, state, d, *, last, name):
    s_len, width = proj.shape
    nblk = width // ATTN_WIDTH
    l_len = s_len // d
    nb = l_len // ATTN_BLOCK
    first = state is None

    def body(*refs):
        q_ref, kp_ref, kc_ref, vp_ref, vc_ref = refs[:5]
        if first:
            o1_ref, o2_ref, o3_ref = refs[5:5 + (2 if last else 3)] + ((None,) if last else ())
        else:
            m_ref, l_ref, a_ref = refs[5:8]
            outs = refs[8:]
            o1_ref, o2_ref, o3_ref = outs + ((None,) if last else ())
        has_prev = pl.program_id(1) > 0
        prev_ok, cur_ok = _band_masks(has_prev)
        for p in range(ATTN_PAIRS):
            sl = slice(p * LANES, (p + 1) * LANES)
            q, kp, kc, vp, vc = q_ref[:, sl], kp_ref[:, sl], kc_ref[:, sl], vp_ref[:, sl], vc_ref[:, sl]
            m_out, l_out, pv = [], [], []
            for hm, c0 in zip(_head_masks(), (0, ATTN_HEAD_DIM)):
                qh = jnp.where(hm, q, 0.0)
                sp = jnp.where(prev_ok, _dg(qh, kp, ((1,), (1,))) * ATTN_SCALE, NEG_BIG)
                sc = jnp.where(cur_ok, _dg(qh, kc, ((1,), (1,))) * ATTN_SCALE, NEG_BIG)
                mb = jnp.maximum(jnp.max(sp, axis=1, keepdims=True), jnp.max(sc, axis=1, keepdims=True))
                if first:
                    m_new, alpha, l_old = mb, None, None
                else:
                    m_old = m_ref[:, p * LANES + c0:p * LANES + c0 + 1]
                    l_old = l_ref[:, p * LANES + c0:p * LANES + c0 + 1]
                    m_new = jnp.maximum(m_old, mb)
                    alpha = jnp.exp(m_old - m_new)
                pp, pc = jnp.exp(sp - m_new), jnp.exp(sc - m_new)
                l_new = jnp.sum(pp, axis=1, keepdims=True) + jnp.sum(pc, axis=1, keepdims=True)
                if not first:
                    l_new = l_new + l_old * alpha
                pv.append((_dg(pp, jnp.where(hm, vp, 0.0), ((1,), (0,))) + _dg(pc, jnp.where(hm, vc, 0.0), ((1,), (0,))), alpha))
                m_out.append(m_new)
                l_out.append(l_new)
            hm_a = _head_masks()[0]
            acc = pv[0][0] + pv[1][0]
            if not first:
                acc = acc + a_ref[:, sl] * jnp.where(hm_a, pv[0][1], pv[1][1])
            m_pair = jnp.where(hm_a, m_out[0], m_out[1])
            l_pair = jnp.where(hm_a, l_out[0], l_out[1])
            if last:
                o1_ref[:, sl] = acc / l_pair
                o2_ref[:, sl] = m_pair + jnp.log(l_pair)
            else:
                o1_ref[:, sl] = m_pair
                o2_ref[:, sl] = l_pair
                o3_ref[:, sl] = acc

    blk = (ATTN_BLOCK, ATTN_WIDTH)
    cur = lambda c: pl.BlockSpec(blk, lambda r, n: (n, r * nblk + c))
    prev = lambda c: pl.BlockSpec(blk, lambda r, n: (jnp.maximum(n - 1, 0), r * nblk + c))
    st = pl.BlockSpec(blk, lambda r, n: (n, r))
    pv_ = proj.reshape(l_len, d * width)
    args = [pv_] * 5 + ([] if first else [t.reshape(l_len, d * ATTN_WIDTH) for t in state])
    n_out = 2 if last else 3
    outs = pl.pallas_call(
        body, name=name, grid=(d, nb),
        in_specs=[cur(0), prev(1), cur(1), prev(2), cur(2)] + ([] if first else [st] * 3),
        out_specs=[st] * n_out, out_shape=[jax.ShapeDtypeStruct((l_len, d * ATTN_WIDTH), F32)] * n_out,
        compiler_params=_params("parallel", "arbitrary"),
    )(*args)
    return tuple(t.reshape(s_len, ATTN_WIDTH) for t in outs)


def _dattn_delta(o, dcat, *, name):
    s_len = o.shape[0]
    tm = _tile(s_len, (512, 256))

    def body(o_ref, do_ref, out_ref):
        r, c = _rows((ATTN_WIDTH, ATTN_WIDTH)), _cols((ATTN_WIDTH, ATTN_WIDTH))
        ones_bd = (r // ATTN_HEAD_DIM == c // ATTN_HEAD_DIM).astype(F32)
        out_ref[...] = _dg(o_ref[...] * do_ref[...], ones_bd, ((1,), (0,)), hi=True)

    blk = pl.BlockSpec((tm, ATTN_WIDTH), lambda i: (i, 0))
    return pl.pallas_call(
        body, name=name, grid=(s_len // tm,), in_specs=[blk, blk], out_specs=blk,
        out_shape=jax.ShapeDtypeStruct((s_len, ATTN_WIDTH), F32), compiler_params=_params("parallel"),
    )(o, dcat)


def _dattn_dq(proj, dcat, lse, delta, dq_in, d, *, name):
    s_len, width = proj.shape
    nblk = width // ATTN_WIDTH
    ncat = dcat.shape[1] // ATTN_WIDTH
    l_len = s_len // d
    nb = l_len // ATTN_BLOCK
    first = dq_in is None

    def body(*refs):
        q_ref, kp_ref, kc_ref, vp_ref, vc_ref, do_ref, lse_ref, dl_ref = refs[:8]
        dq_ref = refs[-1]
        prev_ok, cur_ok = _band_masks(pl.program_id(1) > 0)
        for p in range(ATTN_PAIRS):
            sl = slice(p * LANES, (p + 1) * LANES)
            q, kp, kc, vp, vc, do = q_ref[:, sl], kp_ref[:, sl], kc_ref[:, sl], vp_ref[:, sl], vc_ref[:, sl], do_ref[:, sl]
            acc = jnp.zeros((ATTN_BLOCK, LANES), F32) if first else refs[8][:, sl]
            for hm, c0 in zip(_head_masks(), (0, ATTN_HEAD_DIM)):
                col = slice(p * LANES + c0, p * LANES + c0 + 1)
                lse_h, dl_h = lse_ref[:, col], dl_ref[:, col]
                qh, doh = jnp.where(hm, q, 0.0), jnp.where(hm, do, 0.0)
                for ok, k, v in ((prev_ok, kp, vp), (cur_ok, kc, vc)):
                    s = _dg(qh, k, ((1,), (1,))) * ATTN_SCALE
                    pr = jnp.where(ok, jnp.exp(jnp.where(ok, s, NEG_BIG) - lse_h), 0.0)
                    ds = pr * (_dg(doh, v, ((1,), (1,))) - dl_h)
                    acc = acc + _dg(ds, jnp.where(hm, k, 0.0), ((1,), (0,))) * ATTN_SCALE
            dq_ref[:, sl] = acc

    blk = (ATTN_BLOCK, ATTN_WIDTH)
    cur = lambda c: pl.BlockSpec(blk, lambda r, n: (n, r * nblk + c))
    prev = lambda c: pl.BlockSpec(blk, lambda r, n: (jnp.maximum(n - 1, 0), r * nblk + c))
    st = pl.BlockSpec(blk, lambda r, n: (n, r))
    do_spec = pl.BlockSpec(blk, lambda r, n: (n, r * ncat))
    view = lambda t: t.reshape(l_len, d * t.shape[1])
    args = [view(proj)] * 5 + [view(dcat), view(lse), view(delta)] + ([] if first else [view(dq_in)])
    out = pl.pallas_call(
        body, name=name, grid=(d, nb),
        in_specs=[cur(0), prev(1), cur(1), prev(2), cur(2), do_spec, st, st] + ([] if first else [st]),
        out_specs=st, out_shape=jax.ShapeDtypeStruct((l_len, d * ATTN_WIDTH), F32),
        compiler_params=_params("parallel", "arbitrary"),
    )(*args)
    return out.reshape(s_len, ATTN_WIDTH)


def _dattn_dkv(proj, dcat, lse, delta, dkv_in, d, *, name):
    s_len, width = proj.shape
    nblk = width // ATTN_WIDTH
    ncat = dcat.shape[1] // ATTN_WIDTH
    l_len = s_len // d
    nb = l_len // ATTN_BLOCK
    first = dkv_in is None

    def body(*refs):
        k_ref, v_ref = refs[:2]
        qs = (refs[2:6], refs[6:10])
        dk_ref, dv_ref = refs[-2:]
        has_next = pl.program_id(1) < nb - 1
        nxt_ok, cur_ok = _band_masks(has_next)
        for p in range(ATTN_PAIRS):
            sl = slice(p * LANES, (p + 1) * LANES)
            k, v = k_ref[:, sl], v_ref[:, sl]
            dk = jnp.zeros((ATTN_BLOCK, LANES), F32) if first else refs[10][:, sl]
            dv = jnp.zeros((ATTN_BLOCK, LANES), F32) if first else refs[11][:, sl]
            for hm, c0 in zip(_head_masks(), (0, ATTN_HEAD_DIM)):
                col = slice(p * LANES + c0, p * LANES + c0 + 1)
                for ok, (q_ref, do_ref, lse_ref, dl_ref) in ((cur_ok, qs[0]), (nxt_ok, qs[1])):
                    qh, doh = jnp.where(hm, q_ref[:, sl], 0.0), jnp.where(hm, do_ref[:, sl], 0.0)
                    s = _dg(qh, k, ((1,), (1,))) * ATTN_SCALE
                    pr = jnp.where(ok, jnp.exp(jnp.where(ok, s, NEG_BIG) - lse_ref[:, col]), 0.0)
                    dv = dv + _dg(pr, doh, ((0,), (0,)))
                    ds = pr * (_dg(doh, v, ((1,), (1,))) - dl_ref[:, col])
                    dk = dk + _dg(ds, qh, ((0,), (0,))) * ATTN_SCALE
            dk_ref[:, sl] = dk
            dv_ref[:, sl] = dv

    blk = (ATTN_BLOCK, ATTN_WIDTH)
    nxt_n = lambda n: jnp.minimum(n + 1, nb - 1)
    st = pl.BlockSpec(blk, lambda r, n: (n, r))
    st_n = pl.BlockSpec(blk, lambda r, n: (nxt_n(n), r))
    view = lambda t: t.reshape(l_len, d * t.shape[1])
    in_specs = [pl.BlockSpec(blk, lambda r, n: (n, r * nblk + 1)), pl.BlockSpec(blk, lambda r, n: (n, r * nblk + 2)),
                pl.BlockSpec(blk, lambda r, n: (n, r * nblk)), pl.BlockSpec(blk, lambda r, n: (n, r * ncat)), st, st,
                pl.BlockSpec(blk, lambda r, n: (nxt_n(n), r * nblk)), pl.BlockSpec(blk, lambda r, n: (nxt_n(n), r * ncat)), st_n, st_n]
    args = [view(proj)] * 2 + [view(proj), view(dcat), view(lse), view(delta)] * 2
    if not first:
        in_specs += [st, st]
        args += [view(dkv_in[0]), view(dkv_in[1])]
    dk, dv = pl.pallas_call(
        body, name=name, grid=(d, nb), in_specs=in_specs, out_specs=[st, st],
        out_shape=[jax.ShapeDtypeStruct((l_len, d * ATTN_WIDTH), F32)] * 2,
        compiler_params=_params("parallel", "arbitrary"),
    )(*args)
    return dk.reshape(s_len, ATTN_WIDTH), dv.reshape(s_len, ATTN_WIDTH)


def _dattn_forward(proj, tag):
    state = None
    for i, d in enumerate(DILATIONS):
        state = _dattn_fwd(proj, state, d, last=i == len(DILATIONS) - 1, name=f"{tag}_attn_d{d}")
    return state


def _dattn_backward(proj, o, lse, dcat, tag):
    delta = _dattn_delta(o, dcat, name=f"{tag}_attn_delta")
    dq, dkv = None, None
    for d in DILATIONS:
        dq = _dattn_dq(proj, dcat, lse, delta, dq, d, name=f"{tag}_attn_dq_d{d}")
        dkv = _dattn_dkv(proj, dcat, lse, delta, dkv, d, name=f"{tag}_attn_dkv_d{d}")
    return dq, dkv[0], dkv[1]


CONV_TC = 512
CONV_T = 256


def _shift_down(ext, k, t):
    return (pltpu.roll(ext, k, 0) if k else ext)[SUBLANES:SUBLANES + t]


def _conv_fwd(src, cb0, width, w8, *, name):
    s_len = src.shape[0]
    t, tc = _tile(s_len, (CONV_T,)), CONV_TC
    tpb = t // SUBLANES

    def body(x_ref, h_ref, w_ref, y_ref):
        halo = jnp.where(pl.program_id(0) > 0, h_ref[...], 0.0)
        ext = jnp.concatenate([halo, x_ref[...]], axis=0)
        w = w_ref[...]
        y = jnp.broadcast_to(w[CONV_K:CONV_K + 1], (t, tc))
        for k in range(CONV_K):
            y = y + w[k:k + 1] * _shift_down(ext, CONV_K - 1 - k, t)
        y_ref[...] = y

    return pl.pallas_call(
        body, name=name, grid=(s_len // t, width // tc),
        in_specs=[pl.BlockSpec((t, tc), lambda i, j: (i, cb0 + j)),
                  pl.BlockSpec((SUBLANES, tc), lambda i, j: (jnp.maximum(i * tpb - 1, 0), cb0 + j)),
                  pl.BlockSpec((SUBLANES, tc), lambda i, j: (0, j))],
        out_specs=pl.BlockSpec((t, tc), lambda i, j: (i, j)), out_shape=jax.ShapeDtypeStruct((s_len, width), F32),
        compiler_params=_params("parallel", "parallel"),
    )(src, src, w8)


def _conv_bwd(src, cb0, width, w8, dy, *, name):
    s_len = src.shape[0]
    t, tc = _tile(s_len, (CONV_T,)), CONV_TC
    tpb = t // SUBLANES
    ni = s_len // t

    def body(x_ref, h_ref, w_ref, dy_ref, dn_ref, dx_ref, dw_ref):
        i = pl.program_id(1)
        halo = jnp.where(i > 0, h_ref[...], 0.0)
        ext = jnp.concatenate([halo, x_ref[...]], axis=0)
        dyv = dy_ref[...]
        extn = jnp.concatenate([dyv, jnp.where(i < ni - 1, dn_ref[...], 0.0)], axis=0)
        w = w_ref[...]
        row = _rows((SUBLANES, tc))
        dx = jnp.zeros((t, tc), F32)
        dw = jnp.where(row == CONV_K, jnp.sum(dyv, axis=0, keepdims=True), 0.0)
        for k in range(CONV_K):
            up = CONV_K - 1 - k
            dx = dx + w[k:k + 1] * (pltpu.roll(extn, t + SUBLANES - up, 0) if up else extn)[:t]
            dw = dw + jnp.where(row == k, jnp.sum(dyv * _shift_down(ext, up, t), axis=0, keepdims=True), 0.0)
        dx_ref[...] = dx

        @pl.when(i == 0)
        def _():
            dw_ref[...] = jnp.zeros_like(dw_ref)

        dw_ref[...] += dw

    return pl.pallas_call(
        body, name=name, grid=(width // tc, ni),
        in_specs=[pl.BlockSpec((t, tc), lambda j, i: (i, cb0 + j)),
                  pl.BlockSpec((SUBLANES, tc), lambda j, i: (jnp.maximum(i * tpb - 1, 0), cb0 + j)),
                  pl.BlockSpec((SUBLANES, tc), lambda j, i: (0, j)),
                  pl.BlockSpec((t, tc), lambda j, i: (i, j)),
                  pl.BlockSpec((SUBLANES, tc), lambda j, i: (jnp.minimum((i + 1) * tpb, s_len // SUBLANES - 1), j))],
        out_specs=[pl.BlockSpec((t, tc), lambda j, i: (i, j)), pl.BlockSpec((SUBLANES, tc), lambda j, i: (0, j))],
        out_shape=[jax.ShapeDtypeStruct((s_len, width), F32), jax.ShapeDtypeStruct((SUBLANES, width), F32)],
        compiler_params=_params("parallel", "arbitrary"),
    )(src, src, w8, dy, dy)


LRU_T = 256


def _lru_gates(xc, wa, wx, ba, bx, lam):
    r = _sigmoid(_bdot(xc, wa) + ba)
    i = _sigmoid(_bdot(xc, wx) + bx)
    log_a = (-LRU_C) * r * _softplus(-lam)
    return jnp.exp(log_a), jnp.sqrt(_neg_expm1(2.0 * log_a)) * i * xc


def _block_scan(a, b, reverse):
    t = a.shape[0]
    row = _rows(a.shape)
    s = 1
    while s < t:
        shift, ok = (t - s, row < t - s) if reverse else (s, row >= s)
        b = jnp.where(ok, a * pltpu.roll(b, shift, 0) + b, b)
        a = jnp.where(ok, a * pltpu.roll(a, shift, 0), a)
        s *= 2
    return a, b


def _lru_fwd(xc, proj, wa, wx, ba, bx, lam, *, name):
    s_len, w = xc.shape
    t = _tile(s_len, (LRU_T,))

    def body(xc_ref, gr_ref, wa_ref, wx_ref, ba_ref, bx_ref, lam_ref, h_ref, y_ref, carry):
        @pl.when(pl.program_id(0) == 0)
        def _():
            carry[...] = jnp.zeros_like(carry)

        a, b = _lru_gates(xc_ref[...], wa_ref[...], wx_ref[...], ba_ref[...], bx_ref[...], lam_ref[...])
        a_cum, h0 = _block_scan(a, b, False)
        h = h0 + a_cum * carry[0:1, :]
        h_ref[...] = h
        y_ref[...] = (h * _gelu(gr_ref[...])).astype(y_ref.dtype)
        carry[0:1, :] = h[t - 1:t, :]

    row = pl.BlockSpec((t, w), lambda i: (i, 0))
    mat = pl.BlockSpec((w, w), lambda i: (0, 0))
    vec = pl.BlockSpec((1, w), lambda i: (0, 0))
    return pl.pallas_call(
        body, name=name, grid=(s_len // t,),
        in_specs=[row, pl.BlockSpec((t, w), lambda i: (i, PROJ_AB_BLOCKS - 1)), mat, mat, vec, vec, vec],
        out_specs=[row, row], out_shape=[jax.ShapeDtypeStruct((s_len, w), F32), jax.ShapeDtypeStruct((s_len, w), MXU_DTYPE)],
        scratch_shapes=[pltpu.VMEM((SUBLANES, w), F32)], compiler_params=_params("arbitrary"),
    )(xc, proj, wa, wx, ba, bx, lam)


def _lru_bwd(xc, proj, hs, dcat, wa, wx, ba, bx, lam, *, name):
    s_len, w = xc.shape
    t = _tile(s_len, (LRU_T,))
    nb = s_len // t
    tpb = t // SUBLANES

    def body(xc_ref, gr_ref, h_ref, hp_ref, dy_ref, wa_ref, wx_ref, ba_ref, bx_ref, lam_ref,
             dxc_ref, dgr_ref, dwa_ref, dwx_ref, dba_ref, dbx_ref, dlam_ref, carry):
        step = pl.program_id(0)
        params = (wa_ref[...], wx_ref[...], ba_ref[...], bx_ref[...], lam_ref[...])

        @pl.when(step == 0)
        def _():
            carry[...] = jnp.zeros_like(carry)
            for r in (dwa_ref, dwx_ref, dba_ref, dbx_ref, dlam_ref):
                r[...] = jnp.zeros_like(r)

        (a, _), vjp = jax.vjp(_lru_gates, xc_ref[...], *params)
        gr, h, dy = gr_ref[...], h_ref[...], dy_ref[...]
        gel, gel_vjp = jax.vjp(_gelu, gr)
        dgr_ref[...] = gel_vjp(dy * h)[0]
        dh = dy * gel
        a_cum, g0 = _block_scan(a, a * dh, True)
        big_g = g0 + a_cum * carry[0:1, :]
        row = _rows((t, w))
        g = dh + jnp.where(row == t - 1, carry[0:1, :], pltpu.roll(big_g, t - 1, 0))
        carry[0:1, :] = big_g[0:1, :]
        h_last = jnp.where(step < nb - 1, hp_ref[SUBLANES - 1:SUBLANES, :], 0.0)
        h_prev = jnp.where(row == 0, h_last, pltpu.roll(h, 1, 0))
        dxc, dwa, dwx, dba, dbx, dlam = vjp((g * h_prev, g))
        dxc_ref[...] = dxc
        dwa_ref[...] += dwa
        dwx_ref[...] += dwx
        dba_ref[...] += dba
        dbx_ref[...] += dbx
        dlam_ref[...] += dlam

    rev = lambda i: nb - 1 - i
    row = pl.BlockSpec((t, w), lambda i: (rev(i), 0))
    mat = pl.BlockSpec((w, w), lambda i: (0, 0))
    vec = pl.BlockSpec((1, w), lambda i: (0, 0))
    return pl.pallas_call(
        body, name=name, grid=(nb,),
        in_specs=[row, pl.BlockSpec((t, w), lambda i: (rev(i), PROJ_AB_BLOCKS - 1)), row,
                  pl.BlockSpec((SUBLANES, w), lambda i: (jnp.maximum(rev(i) * tpb - 1, 0), 0)),
                  pl.BlockSpec((t, w), lambda i: (rev(i), 1)), mat, mat, vec, vec, vec],
        out_specs=[row, row, mat, mat, vec, vec, vec],
        out_shape=[jax.ShapeDtypeStruct((s_len, w), F32)] * 2 + [jax.ShapeDtypeStruct((w, w), F32)] * 2
        + [jax.ShapeDtypeStruct((1, w), F32)] * 3,
        scratch_shapes=[pltpu.VMEM((SUBLANES, w), F32)], compiler_params=_params("arbitrary"),
    )(xc, proj, hs, hs, dcat, wa, wx, ba, bx, lam)


XA_T = 256
XA_SCALE = XA_HEAD_DIM ** -0.5


def _xa_head(q, k, v):
    s = _bdot_nt(q, k) * XA_SCALE
    e = jnp.exp(s - jnp.max(s, axis=-1, keepdims=True))
    return _bdot(e / jnp.sum(e, axis=-1, keepdims=True), v)


def _xa_fwd(q, kv, *, name):
    s_len, d = q.shape
    n_mem = kv.shape[0]
    t = _tile(s_len, (XA_T,))

    def body(q_ref, k_ref, v_ref, o_ref):
        for h in range(XA_HEADS):
            sl = slice(h * XA_HEAD_DIM, (h + 1) * XA_HEAD_DIM)
            o_ref[:, sl] = _xa_head(q_ref[:, sl].astype(F32), k_ref[:, sl], v_ref[:, sl]).astype(o_ref.dtype)

    return pl.pallas_call(
        body, name=name, grid=(s_len // t,),
        in_specs=[pl.BlockSpec((t, d), lambda i: (i, 0)), pl.BlockSpec((n_mem, d), lambda i: (0, 0)),
                  pl.BlockSpec((n_mem, d), lambda i: (0, 1))],
        out_specs=pl.BlockSpec((t, d), lambda i: (i, 0)), out_shape=jax.ShapeDtypeStruct((s_len, d), MXU_DTYPE),
        compiler_params=_params("parallel"),
    )(q, kv, kv)


def _xa_bwd(q, kv, do, *, name):
    s_len, d = q.shape
    n_mem = kv.shape[0]
    t = _tile(s_len, (XA_T,))

    def body(q_ref, k_ref, v_ref, do_ref, dq_ref, dk_ref, dv_ref):
        @pl.when(pl.program_id(0) == 0)
        def _():
            dk_ref[...] = jnp.zeros_like(dk_ref)
            dv_ref[...] = jnp.zeros_like(dv_ref)

        for h in range(XA_HEADS):
            sl = slice(h * XA_HEAD_DIM, (h + 1) * XA_HEAD_DIM)
            _, vjp = jax.vjp(_xa_head, q_ref[:, sl].astype(F32), k_ref[:, sl], v_ref[:, sl])
            dq, dk, dv = vjp(do_ref[:, sl].astype(F32))
            dq_ref[:, sl] = dq.astype(dq_ref.dtype)
            dk_ref[:, sl] += dk
            dv_ref[:, sl] += dv

    row = pl.BlockSpec((t, d), lambda i: (i, 0))
    dq, dk, dv = pl.pallas_call(
        body, name=name, grid=(s_len // t,),
        in_specs=[row, pl.BlockSpec((n_mem, d), lambda i: (0, 0)), pl.BlockSpec((n_mem, d), lambda i: (0, 1)), row],
        out_specs=[row, pl.BlockSpec((n_mem, d), lambda i: (0, 0)), pl.BlockSpec((n_mem, d), lambda i: (0, 0))],
        out_shape=[jax.ShapeDtypeStruct((s_len, d), MXU_DTYPE)] + [jax.ShapeDtypeStruct((n_mem, d), F32)] * 2,
        compiler_params=_params("arbitrary"),
    )(q, kv, kv, do)
    return dq, jnp.concatenate([dk, dv], axis=1)


DN_Q_SCALE = DN_HEAD_DIM ** -0.5
L2_EPS = 1e-6


def _dn_gates(ab, alog, dtb):
    return -jnp.exp(alog) * _softplus(ab + dtb), _sigmoid(ab)


def _dn_head(cq, ck, cv, z, g, beta, onorm, state):
    c = cq.shape[0]
    l2 = lambda t: t * lax.rsqrt(jnp.sum(t * t, axis=-1, keepdims=True) + L2_EPS)
    q, k, v = l2(_silu(cq)) * DN_Q_SCALE, l2(_silu(ck)), _silu(cv)
    r, cc = _rows((c, c)), _cols((c, c))
    tri, eye = r >= cc, r == cc
    g_row = jnp.sum(jnp.where(eye, g, 0.0), axis=0, keepdims=True)
    gcum_c = jnp.sum(jnp.where(tri, g_row, 0.0), axis=1, keepdims=True)
    gcum_r = jnp.sum(jnp.where(cc >= r, g, 0.0), axis=0, keepdims=True)
    decay = jnp.where(tri, jnp.exp(jnp.where(tri, gcum_c - gcum_r, 0.0)), 0.0)
    kb = k * beta
    n = jnp.where(r > cc, _bdot_nt(kb, k) * decay, 0.0)
    eye_f = eye.astype(F32)
    inv, p = eye_f - n, n
    for _ in range(5):
        p = _hdot(p, p)
        inv = _hdot(inv, eye_f + p)
    u, w = _hdot(inv, v * beta), _hdot(inv, kb * jnp.exp(gcum_c))
    v_new = u - _bdot(w, state)
    o = _bdot(q * jnp.exp(gcum_c), state) + _bdot(_bdot_nt(q, k) * decay, v_new)
    g_last = jnp.sum(g, axis=0, keepdims=True)
    new_state = state * jnp.exp(g_last) + _bdot_tn(k * jnp.exp(g_last - gcum_c), v_new)
    on = o * lax.rsqrt(jnp.mean(o * o, axis=-1, keepdims=True) + NORM_EPS) * onorm
    return on * _silu(z), new_state


def _dn_fwd(cqkv, proj, ab, alog, dtb, onorm, *, name):
    s_len = cqkv.shape[0]
    c, hd, w = DN_CHUNK, DN_HEAD_DIM, DN_WIDTH
    n_chunks = s_len // c

    def body(c_ref, z_ref, ab_ref, alog_ref, dtb_ref, on_ref, o_ref, st_ref, state):
        @pl.when(pl.program_id(0) == 0)
        def _():
            state[...] = jnp.zeros_like(state)

        g_all, beta_all = _dn_gates(ab_ref[...], alog_ref[...], dtb_ref[...])
        for h in range(DN_HEADS):
            sl = slice(h * hd, (h + 1) * hd)
            st = state[h]
            st_ref[0, h] = st
            out, new = _dn_head(c_ref[:, sl], c_ref[:, w + h * hd:w + (h + 1) * hd], c_ref[:, 2 * w + h * hd:2 * w + (h + 1) * hd],
                                z_ref[:, sl], g_all[:, h:h + 1], beta_all[:, DN_HEADS + h:DN_HEADS + h + 1], on_ref[...], st)
            o_ref[:, sl] = out.astype(o_ref.dtype)
            state[h] = new

    vec = pl.BlockSpec((1, LANES), lambda i: (0, 0))
    return pl.pallas_call(
        body, name=name, grid=(n_chunks,),
        in_specs=[pl.BlockSpec((c, 3 * w), lambda i: (i, 0)), pl.BlockSpec((c, w), lambda i: (i, 3)),
                  pl.BlockSpec((c, LANES), lambda i: (i, 0)), vec, vec, vec],
        out_specs=[pl.BlockSpec((c, w), lambda i: (i, 0)), pl.BlockSpec((1, DN_HEADS, hd, hd), lambda i: (i, 0, 0, 0))],
        out_shape=[jax.ShapeDtypeStruct((s_len, w), MXU_DTYPE), jax.ShapeDtypeStruct((n_chunks, DN_HEADS, hd, hd), F32)],
        scratch_shapes=[pltpu.VMEM((DN_HEADS, hd, hd), F32)], compiler_params=_params("arbitrary"),
    )(cqkv, proj, ab, alog, dtb, onorm)


def _dn_bwd(cqkv, proj, ab, alog, dtb, onorm, states, dout, *, name):
    s_len = cqkv.shape[0]
    c, hd, w = DN_CHUNK, DN_HEAD_DIM, DN_WIDTH
    n_chunks = s_len // c

    def body(c_ref, z_ref, ab_ref, alog_ref, dtb_ref, on_ref, st_ref, do_ref,
             dc_ref, dz_ref, dab_ref, dalog_ref, ddtb_ref, don_ref, dstate):
        @pl.when(pl.program_id(0) == 0)
        def _():
            dstate[...] = jnp.zeros_like(dstate)
            for r in (dalog_ref, ddtb_ref, don_ref):
                r[...] = jnp.zeros_like(r)

        (g_all, beta_all), gates_vjp = jax.vjp(_dn_gates, ab_ref[...], alog_ref[...], dtb_ref[...])
        col = _cols((c, LANES))
        dg_all, dbeta_all = jnp.zeros((c, LANES), F32), jnp.zeros((c, LANES), F32)
        don = jnp.zeros((1, hd), F32)
        for h in range(DN_HEADS):
            sl = slice(h * hd, (h + 1) * hd)
            slk, slv = slice(w + h * hd, w + (h + 1) * hd), slice(2 * w + h * hd, 2 * w + (h + 1) * hd)
            _, vjp = jax.vjp(_dn_head, c_ref[:, sl], c_ref[:, slk], c_ref[:, slv], z_ref[:, sl], g_all[:, h:h + 1],
                             beta_all[:, DN_HEADS + h:DN_HEADS + h + 1], on_ref[...], st_ref[0, h])
            dcq, dck, dcv, dz, dg, dbeta, don_h, dst = vjp((do_ref[:, sl].astype(F32), dstate[h]))
            dc_ref[:, sl], dc_ref[:, slk], dc_ref[:, slv] = dcq, dck, dcv
            dz_ref[:, sl] = dz
            dstate[h] = dst
            dg_all = dg_all + jnp.where(col == h, dg, 0.0)
            dbeta_all = dbeta_all + jnp.where(col == DN_HEADS + h, dbeta, 0.0)
            don = don + don_h
        dab, dalog, ddtb = gates_vjp((dg_all, dbeta_all))
        dab_ref[...] = dab
        dalog_ref[...] += dalog
        ddtb_ref[...] += ddtb
        don_ref[...] += don

    rev = lambda i: n_chunks - 1 - i
    vec = pl.BlockSpec((1, LANES), lambda i: (0, 0))
    return pl.pallas_call(
        body, name=name, grid=(n_chunks,),
        in_specs=[pl.BlockSpec((c, 3 * w), lambda i: (rev(i), 0)), pl.BlockSpec((c, w), lambda i: (rev(i), 3)),
                  pl.BlockSpec((c, LANES), lambda i: (rev(i), 0)), vec, vec, vec,
                  pl.BlockSpec((1, DN_HEADS, hd, hd), lambda i: (rev(i), 0, 0, 0)), pl.BlockSpec((c, w), lambda i: (rev(i), 0))],
        out_specs=[pl.BlockSpec((c, 3 * w), lambda i: (rev(i), 0)), pl.BlockSpec((c, w), lambda i: (rev(i), 0)),
                   pl.BlockSpec((c, LANES), lambda i: (rev(i), 0)), vec, vec, vec],
        out_shape=[jax.ShapeDtypeStruct((s_len, 3 * w), F32), jax.ShapeDtypeStruct((s_len, w), F32),
                   jax.ShapeDtypeStruct((s_len, LANES), F32)] + [jax.ShapeDtypeStruct((1, LANES), F32)] * 3,
        scratch_shapes=[pltpu.VMEM((DN_HEADS, hd, hd), F32)], compiler_params=_params("arbitrary"),
    )(cqkv, proj, ab, alog, dtb, onorm, states, dout)


def _final_loss(x, g, target, *, name):
    s, d = x.shape
    tm = _tile(s, (512, 256))

    def body(x_ref, g_ref, t_ref, loss_ref, dx_ref, dg_ref):
        @pl.when(pl.program_id(0) == 0)
        def _():
            loss_ref[...] = jnp.zeros_like(loss_ref)
            dg_ref[...] = jnp.zeros_like(dg_ref)

        xv, gv = x_ref[...], g_ref[...]
        r = lax.rsqrt(jnp.mean(xv * xv, axis=-1, keepdims=True) + NORM_EPS)
        xh = xv * r
        err = xh * gv - t_ref[...]
        loss_ref[...] += 0.5 * jnp.sum(jnp.mean(err * err, axis=-1, keepdims=True), axis=0, keepdims=True)
        dy = err * (1.0 / d)
        dxh = dy * gv
        dx_ref[...] = r * (dxh - xh * jnp.mean(dxh * xh, axis=-1, keepdims=True))
        dg_ref[...] += jnp.sum(dy * xh, axis=0, keepdims=True)

    row = pl.BlockSpec((tm, d), lambda i: (i, 0))
    vec = pl.BlockSpec((1, d), lambda i: (0, 0))
    return pl.pallas_call(
        body, name=name, grid=(s // tm,), in_specs=[row, vec, row],
        out_specs=[pl.BlockSpec((1, LANES), lambda i: (0, 0)), row, vec],
        out_shape=[jax.ShapeDtypeStruct((1, LANES), F32), jax.ShapeDtypeStruct((s, d), F32), jax.ShapeDtypeStruct((1, d), F32)],
        compiler_params=_params("arbitrary"),
    )(x, g.reshape(1, d), target)


def _adamw(w, g, m, v, *, name):
    shape = w.shape
    cols = shape[-1]
    rows = max(w.size // cols, 1)
    tr = _tile(rows, (512, 352, 256, 128, 64, 32, 16, 8))
    c1, c2 = 1.0 - ADAM_B1 ** ADAM_STEP, 1.0 - ADAM_B2 ** ADAM_STEP

    def body(w_ref, g_ref, m_ref, v_ref, d_ref, nm_ref, nv_ref):
        gv = g_ref[...]
        nm = ADAM_B1 * m_ref[...] + (1.0 - ADAM_B1) * gv
        nv = ADAM_B2 * v_ref[...] + (1.0 - ADAM_B2) * (gv * gv)
        d_ref[...] = -ADAM_LR * ((nm / c1) / (jnp.sqrt(nv / c2) + ADAM_EPS) + ADAM_WD * w_ref[...])
        nm_ref[...] = nm
        nv_ref[...] = nv

    blk = pl.BlockSpec((tr, cols), lambda i: (i, 0))
    outs = pl.pallas_call(
        body, name=name, grid=(rows // tr,), in_specs=[blk] * 4, out_specs=[blk] * 3,
        out_shape=[jax.ShapeDtypeStruct((rows, cols), F32)] * 3, compiler_params=_params("parallel"),
    )(*(t.reshape(rows, cols) for t in (w, g, m, v)))
    return tuple(t.reshape(shape) for t in outs)


def _block_diag(w):
    n, j, k = w.shape
    eye = jnp.eye(n, dtype=w.dtype)
    return (eye[:, None, :, None] * w[:, :, None, :]).reshape(n * j, n * k)


def _block_diag_part(m, n):
    j, k = m.shape[0] // n, m.shape[1] // n
    m4 = m.reshape(n, j, n, k)
    return jnp.stack([m4[i, :, i, :] for i in range(n)], axis=0)


DN_AB = 2 * DN_HEADS
DEPTH = 2


def _row(v, width=None):
    v = v.reshape(1, -1)
    return v if width is None else jnp.pad(v, ((0, 0), (0, width - v.shape[1])))


def _conv_w8(conv_w, bias=None):
    w8 = jnp.zeros((SUBLANES, conv_w.shape[1]), F32).at[:CONV_K].set(conv_w)
    return w8 if bias is None else w8.at[CONV_K].set(bias)


def _mixer_ab_fwd(x, w, tag):
    h = _rms_fwd(x, w["mix_norm"][0], name=f"{tag}_norm")
    proj = _mm(h, w["ab_w_in"][0], name=f"{tag}_in")
    o, lse = _dattn_forward(proj, tag)
    w8 = _conv_w8(w["lru_conv_w"][0], w["lru_conv_b"][0])
    xc = _conv_fwd(proj, PROJ_AB_BLOCKS - 2, LRU_WIDTH, w8, name=f"{tag}_conv")
    wa, wx = _block_diag(w["lru_w_a"][0]), _block_diag(w["lru_w_x"][0])
    vecs = (_row(w["lru_b_a"][0]), _row(w["lru_b_x"][0]), _row(w["lru_lambda"][0]))
    hs, y = _lru_fwd(xc, proj, wa, wx, *vecs, name=f"{tag}_lru")
    w_out = w["ab_w_out"][0]
    x2 = _mm(o, w_out[:ATTN_WIDTH], res=x, name=f"{tag}_out_attn")
    x2 = _mm(y, w_out[ATTN_WIDTH:], res=x2, name=f"{tag}_out_lru")
    return x2, (x, h, proj, o, lse, w8, xc, wa, wx, vecs, hs, y)


def _mixer_ab_bwd(saved, w, dy, tag):
    x, h, proj, o, lse, w8, xc, wa, wx, vecs, hs, y = saved
    w_out = w["ab_w_out"][0]
    dcat = _mm(dy, w_out, mode="nt", name=f"{tag}_dcat")
    dw_out = jnp.concatenate([_mm(o, dy, mode="tn", name=f"{tag}_dwout_attn"), _mm(y, dy, mode="tn", name=f"{tag}_dwout_lru")], axis=0)
    dq, dk, dv = _dattn_backward(proj, o, lse, dcat, tag)
    dxc, dgr, dwa, dwx, dba, dbx, dlam = _lru_bwd(xc, proj, hs, dcat, wa, wx, *vecs, name=f"{tag}_dlru")
    dxr, dw8 = _conv_bwd(proj, PROJ_AB_BLOCKS - 2, LRU_WIDTH, w8, dxc, name=f"{tag}_dconv")
    dproj = jnp.concatenate([t.astype(MXU_DTYPE) for t in (dq, dk, dv, dxr, dgr)], axis=1)
    dw_in = _mm(h, dproj, mode="tn", name=f"{tag}_dwin")
    dh = _mm(dproj, w["ab_w_in"][0], mode="nt", name=f"{tag}_dh")
    dx, dg = _rms_bwd(x, w["mix_norm"][0], dh, dy, name=f"{tag}_dnorm")
    grads = dict(mix_norm=dg[0], ab_w_in=dw_in, ab_w_out=dw_out, lru_conv_w=dw8[:CONV_K], lru_conv_b=dw8[CONV_K],
                 lru_w_a=_block_diag_part(dwa, LRU_BLOCKS), lru_b_a=dba[0], lru_w_x=_block_diag_part(dwx, LRU_BLOCKS),
                 lru_b_x=dbx[0], lru_lambda=dlam[0])
    return dx, grads


def _dn_split_w(w_in):
    return w_in[:, :4 * DN_WIDTH], jnp.pad(w_in[:, 4 * DN_WIDTH:], ((0, 0), (0, LANES - DN_AB)))


def _mixer_dn_fwd(x, w, tag):
    h = _rms_fwd(x, w["mix_norm"][1], name=f"{tag}_norm")
    w_qkvz, w_ab = _dn_split_w(w["dn_w_in"][0])
    proj = _mm(h, w_qkvz, name=f"{tag}_in")
    ab = _mm(h, w_ab, name=f"{tag}_in_ab")
    w8 = _conv_w8(w["dn_conv_w"][0])
    cqkv = _conv_fwd(proj, 0, 3 * DN_WIDTH, w8, name=f"{tag}_conv")
    vecs = (_row(w["dn_a_log"][0], LANES), _row(w["dn_dt_bias"][0], LANES), _row(w["dn_o_norm"][0]))
    og, states = _dn_fwd(cqkv, proj, ab, *vecs, name=f"{tag}_dn")
    x2 = _mm(og, w["dn_w_out"][0], res=x, name=f"{tag}_out")
    return x2, (x, h, w_qkvz, w_ab, proj, ab, w8, cqkv, vecs, og, states)


def _mixer_dn_bwd(saved, w, dy, tag):
    x, h, w_qkvz, w_ab, proj, ab, w8, cqkv, vecs, og, states = saved
    dout = _mm(dy, w["dn_w_out"][0], mode="nt", name=f"{tag}_dout")
    dw_out = _mm(og, dy, mode="tn", name=f"{tag}_dwout")
    dcqkv, dz, dab, dalog, ddtb, don = _dn_bwd(cqkv, proj, ab, *vecs, states, dout, name=f"{tag}_ddn")
    dqkv, dw8 = _conv_bwd(proj, 0, 3 * DN_WIDTH, w8, dcqkv, name=f"{tag}_dconv")
    dproj = jnp.concatenate([dqkv.astype(MXU_DTYPE), dz.astype(MXU_DTYPE)], axis=1)
    dw_in = jnp.concatenate([_mm(h, dproj, mode="tn", name=f"{tag}_dwin"),
                             _mm(h, dab, mode="tn", name=f"{tag}_dwin_ab")[:, :DN_AB]], axis=1)
    dh = _mm(dproj, w_qkvz, mode="nt", name=f"{tag}_dh")
    dh = _mm(dab, w_ab, mode="nt", res=dh, name=f"{tag}_dh_ab")
    dx, dg = _rms_bwd(x, w["mix_norm"][1], dh, dy, name=f"{tag}_dnorm")
    grads = dict(mix_norm=dg[0], dn_w_in=dw_in, dn_w_out=dw_out, dn_conv_w=dw8[:CONV_K], dn_a_log=dalog[0, :DN_HEADS],
                 dn_dt_bias=ddtb[0, :DN_HEADS], dn_o_norm=don[0])
    return dx, grads


def _xa_layer_fwd(x, mem, w, layer, tag):
    hq = _rms_fwd(x, w["xa_norm"][layer], name=f"{tag}_norm")
    q = _mm(hq, w["xa_wq"][layer], out_dtype=MXU_DTYPE, name=f"{tag}_q")
    hm = _rms_fwd(mem, w["xa_mem_norm"][layer], name=f"{tag}_mem_norm")
    kv = _mm(hm, w["xa_wkv"][layer], name=f"{tag}_kv")
    oa = _xa_fwd(q, kv, name=f"{tag}_core")
    x2 = _mm(oa, w["xa_wo"][layer], res=x, name=f"{tag}_out")
    return x2, (x, hq, q, hm, kv, oa)


def _xa_layer_bwd(saved, mem, w, layer, dy, tag):
    x, hq, q, hm, kv, oa = saved
    do = _mm(dy, w["xa_wo"][layer], mode="nt", name=f"{tag}_do")
    dwo = _mm(oa, dy, mode="tn", name=f"{tag}_dwo")
    dq, dkv = _xa_bwd(q, kv, do, name=f"{tag}_dcore")
    dwq = _mm(hq, dq, mode="tn", name=f"{tag}_dwq")
    dhq = _mm(dq, w["xa_wq"][layer], mode="nt", name=f"{tag}_dhq")
    dx, dg = _rms_bwd(x, w["xa_norm"][layer], dhq, dy, name=f"{tag}_dnorm")
    dwkv = _mm(hm, dkv, mode="tn", name=f"{tag}_dwkv")
    dhm = _mm(dkv, w["xa_wkv"][layer], mode="nt", name=f"{tag}_dhm")
    _, dgm = _rms_bwd(mem, w["xa_mem_norm"][layer], dhm, jnp.zeros_like(mem), name=f"{tag}_dmem_norm")
    return dx, dict(xa_norm=dg[0], xa_mem_norm=dgm[0], xa_wq=dwq, xa_wkv=dwkv, xa_wo=dwo)


def _local_step(x, mem, target, w):
    saved = []
    for layer in range(DEPTH):
        t = f"l{layer}"
        x, s1 = _ffn_fwd(x, w["ffn1_norm"][layer], w["ffn1_w_in"][layer], w["ffn1_w_out"][layer], f"{t}_ffn1")
        x, s2 = (_mixer_ab_fwd if layer % 2 == 0 else _mixer_dn_fwd)(x, w, f"{t}_mix")
        x, s3 = _xa_layer_fwd(x, mem, w, layer, f"{t}_xa")
        x, s4 = _ffn_fwd(x, w["ffn2_norm"][layer], w["ffn2_w_in"][layer], w["ffn2_w_out"][layer], f"{t}_ffn2")
        saved.append((s1, s2, s3, s4))
    loss, dx, dgf = _final_loss(x, w["final_norm"], target, name="final_loss")
    per_layer = []
    for layer in reversed(range(DEPTH)):
        t = f"l{layer}"
        s1, s2, s3, s4 = saved[layer]
        g = {}
        dx, g["ffn2_norm"], g["ffn2_w_in"], g["ffn2_w_out"] = _ffn_bwd(
            s4, w["ffn2_norm"][layer], w["ffn2_w_in"][layer], w["ffn2_w_out"][layer], dx, f"{t}_ffn2")
        dx, gx = _xa_layer_bwd(s3, mem, w, layer, dx, f"{t}_xa")
        dx, gm = (_mixer_ab_bwd if layer % 2 == 0 else _mixer_dn_bwd)(s2, w, dx, f"{t}_mix")
        dx, g["ffn1_norm"], g["ffn1_w_in"], g["ffn1_w_out"] = _ffn_bwd(
            s1, w["ffn1_norm"][layer], w["ffn1_w_in"][layer], w["ffn1_w_out"][layer], dx, f"{t}_ffn1")
        per_layer.append({**g, **gx, **gm})
    per_layer.reverse()
    grads = {"final_norm": dgf[0]}
    for name in per_layer[0]:
        if name in per_layer[1]:
            grads[name] = jnp.stack([per_layer[0][name], per_layer[1][name]], axis=0)
        else:
            grads[name] = per_layer[0][name][None]
    for name in per_layer[1]:
        if name not in per_layer[0]:
            grads[name] = per_layer[1][name][None]
    return loss, dx, grads


N_CHIPS = 4
HBM_SPEC = pl.BlockSpec(memory_space=pltpu.HBM)
PACK_COLS = 1024


def _place():
    x, y, c = lax.axis_index("x"), lax.axis_index("y"), lax.axis_index("c")
    return x, y, c, [(1 - x, y), (x, 1 - y), (1 - x, 1 - y)]


def _remote(src, dst, sems, k, to):
    return pltpu.make_async_remote_copy(src_ref=src, dst_ref=dst, send_sem=sems[0].at[k], recv_sem=sems[1].at[k],
                                        device_id=to, device_id_type=MESH)


def _gather_weights(p):
    rows, cols = p.shape
    half = rows // 2

    def body(p_ref, out_ref, send_sems, recv_sems, local_sem):
        x, y, c, chips = _place()
        sems = (send_sems, recv_sems)
        part = lambda cx, cy, h: out_ref.at[2 * cx + cy, pl.ds(h * half, half), :]
        mine = pltpu.make_async_copy(p_ref, out_ref.at[2 * x + y], local_sem)
        mine.start()
        first = [_remote(p_ref.at[pl.ds(c * half, half), :], part(x, y, c), sems, j, (*chip, c)) for j, chip in enumerate(chips)]
        for cp in first:
            cp.start()
        passed = [_remote(part(*chip, c), part(*chip, c), sems, 3 + j, (x, y, 1 - c)) for j, chip in enumerate(chips)]
        for j, chip in enumerate(chips):
            _remote(part(*chip, c), part(*chip, c), sems, j, (*chip, c)).wait_recv()
            passed[j].start()
        for j, chip in enumerate(chips):
            _remote(part(*chip, 1 - c), part(*chip, 1 - c), sems, 3 + j, (x, y, 1 - c)).wait_recv()
        for cp in first + passed:
            cp.wait_send()
        mine.wait()

    return pl.pallas_call(
        body, name="gather_weights", in_specs=[HBM_SPEC], out_specs=HBM_SPEC,
        out_shape=jax.ShapeDtypeStruct((N_CHIPS, rows, cols), p.dtype),
        scratch_shapes=[pltpu.SemaphoreType.DMA((6,)), pltpu.SemaphoreType.DMA((6,)), pltpu.SemaphoreType.DMA],
    )(p)


def _allreduce_small(v):
    rows, cols = v.shape
    n_dev = 2 * N_CHIPS

    def body(v_ref, out_ref, all_ref, send_sems, recv_sems, local_sem):
        x, y, c, chips = _place()
        sems = (send_sems, recv_sems)
        me, sibling = (x, y, c), (x, y, 1 - c)
        slot = lambda px, py, pc: all_ref.at[pl.ds((4 * px + 2 * py + pc) * rows, rows), :]
        mine = pltpu.make_async_copy(v_ref, slot(*me), local_sem)
        mine.start()
        first = [_remote(v_ref, slot(*me), sems, 0, sibling)]
        first += [_remote(v_ref, slot(*me), sems, 1 + j, (*chip, c)) for j, chip in enumerate(chips)]
        for cp in first:
            cp.start()
        passed = [_remote(slot(*chip, c), slot(*chip, c), sems, 4 + j, sibling) for j, chip in enumerate(chips)]
        for j, chip in enumerate(chips):
            _remote(slot(*chip, c), slot(*chip, c), sems, 1 + j, me).wait_recv()
            passed[j].start()
        _remote(slot(*sibling), slot(*sibling), sems, 0, me).wait_recv()
        for j, chip in enumerate(chips):
            _remote(slot(*chip, 1 - c), slot(*chip, 1 - c), sems, 4 + j, me).wait_recv()
        for cp in first + passed:
            cp.wait_send()
        mine.wait()
        acc = all_ref[pl.ds(0, rows), :]
        for k in range(1, n_dev):
            acc = acc + all_ref[pl.ds(k * rows, rows), :]
        out_ref[...] = acc

    vmem = pl.BlockSpec(memory_space=pltpu.VMEM)
    return pl.pallas_call(
        body, name="allreduce_small", in_specs=[vmem], out_specs=vmem, out_shape=jax.ShapeDtypeStruct((rows, cols), F32),
        scratch_shapes=[pltpu.VMEM((n_dev * rows, cols), F32), pltpu.SemaphoreType.DMA((7,)), pltpu.SemaphoreType.DMA((7,)),
                        pltpu.SemaphoreType.DMA],
    )(v)


def _swap_sibling(v, *, name):
    def body(v_ref, out_ref, send_sems, recv_sems):
        x, y, c, _ = _place()
        cp = _remote(v_ref, out_ref, (send_sems, recv_sems), 0, (x, y, 1 - c))
        cp.start()
        cp.wait()

    return pl.pallas_call(
        body, name=name, in_specs=[HBM_SPEC], out_specs=HBM_SPEC, out_shape=jax.ShapeDtypeStruct(v.shape, v.dtype),
        scratch_shapes=[pltpu.SemaphoreType.DMA((1,)), pltpu.SemaphoreType.DMA((1,))],
    )(v)


def _exchange_chips(v):
    def body(v_ref, out_ref, send_sems, recv_sems, local_sem):
        x, y, c, chips = _place()
        sems = (send_sems, recv_sems)
        me = 2 * x + y
        mine = pltpu.make_async_copy(v_ref.at[me], out_ref.at[me], local_sem)
        mine.start()
        sends = [_remote(v_ref.at[2 * cx + cy], out_ref.at[me], sems, j, (cx, cy, c)) for j, (cx, cy) in enumerate(chips)]
        for cp in sends:
            cp.start()
        for j, (cx, cy) in enumerate(chips):
            _remote(v_ref.at[me], out_ref.at[2 * cx + cy], sems, j, (cx, cy, c)).wait_recv()
        for cp in sends:
            cp.wait_send()
        mine.wait()

    return pl.pallas_call(
        body, name="exchange_chips", in_specs=[HBM_SPEC], out_specs=HBM_SPEC, out_shape=jax.ShapeDtypeStruct(v.shape, v.dtype),
        scratch_shapes=[pltpu.SemaphoreType.DMA((3,)), pltpu.SemaphoreType.DMA((3,)), pltpu.SemaphoreType.DMA],
    )(v)


def _share_halves(v):
    def body(v_ref, out_ref, send_sems, recv_sems, local_sem):
        x, y, c, _ = _place()
        mine = pltpu.make_async_copy(v_ref, out_ref.at[c], local_sem)
        mine.start()
        cp = _remote(v_ref, out_ref.at[c], (send_sems, recv_sems), 0, (x, y, 1 - c))
        cp.start()
        _remote(v_ref, out_ref.at[1 - c], (send_sems, recv_sems), 0, (x, y, 1 - c)).wait_recv()
        cp.wait_send()
        mine.wait()

    return pl.pallas_call(
        body, name="share_halves", in_specs=[HBM_SPEC], out_specs=HBM_SPEC, out_shape=jax.ShapeDtypeStruct((2,) + v.shape, v.dtype),
        scratch_shapes=[pltpu.SemaphoreType.DMA((1,)), pltpu.SemaphoreType.DMA((1,)), pltpu.SemaphoreType.DMA],
    )(v)


def _sum_blocks(terms, *, name):
    _, rows, cols = terms[0][0].shape
    tr = _tile(rows, (256, 128, 64, 32, 16, 8))

    def body(*refs):
        acc = refs[0][...]
        for r in refs[1:-1]:
            acc = acc + r[...]
        refs[-1][...] = acc

    spec = lambda k: pl.BlockSpec((None, tr, cols), lambda i: (k, i, 0))
    return pl.pallas_call(
        body, name=name, grid=(rows // tr,), in_specs=[spec(k) for _, k in terms],
        out_specs=pl.BlockSpec((tr, cols), lambda i: (i, 0)), out_shape=jax.ShapeDtypeStruct((rows, cols), F32),
        compiler_params=_params("parallel"),
    )(*(a for a, _ in terms))


def _reduce_grads(g4):
    c = lax.axis_index("c")
    keep = lax.dynamic_index_in_dim(g4, c, axis=1, keepdims=False)
    give = lax.dynamic_index_in_dim(g4, 1 - c, axis=1, keepdims=False)
    got = _swap_sibling(give, name="reduce_swap")
    n, rows, cols = keep.shape
    chip_sum = _sum_blocks([(keep.reshape(1, n * rows, cols), 0), (got.reshape(1, n * rows, cols), 0)], name="reduce_sum_cores")
    parts = _exchange_chips(chip_sum.reshape(n, rows, cols))
    half = _sum_blocks([(parts, k) for k in range(N_CHIPS)], name="reduce_sum_chips")
    return _share_halves(half)


BIG = (("ffn1_w_in", 2), ("ffn1_w_out", 1), ("xa_wq", 1), ("xa_wkv", 2), ("xa_wo", 1), ("ffn2_w_in", 2), ("ffn2_w_out", 1),
       ("ab_w_in", 2), ("ab_w_out", 1), ("dn_w_in", 2), ("dn_w_out", 1))
TINY_SHARDED = (("lru_conv_w", 2), ("dn_conv_w", 2))
REPLICATED = ("ffn1_norm", "mix_norm", "xa_norm", "xa_mem_norm", "ffn2_norm", "lru_conv_b", "lru_w_a", "lru_b_a", "lru_w_x",
              "lru_b_x", "lru_lambda", "dn_a_log", "dn_dt_bias", "dn_o_norm", "final_norm")
WEIGHTS = ("ffn1_norm", "ffn1_w_in", "ffn1_w_out", "mix_norm", "xa_norm", "xa_mem_norm", "xa_wq", "xa_wkv", "xa_wo", "ffn2_norm",
           "ffn2_w_in", "ffn2_w_out", "ab_w_in", "lru_conv_w", "lru_conv_b", "lru_w_a", "lru_b_a", "lru_w_x", "lru_b_x",
           "lru_lambda", "ab_w_out", "dn_w_in", "dn_conv_w", "dn_a_log", "dn_dt_bias", "dn_o_norm", "dn_w_out", "final_norm")


def _pad_rows(flat, row_multiple):
    n = flat.shape[-1]
    per = row_multiple * PACK_COLS
    total = -(-n // per) * per
    flat = jnp.pad(flat, [(0, 0)] * (flat.ndim - 1) + [(0, total - n)])
    return flat.reshape(flat.shape[:-1] + (total // PACK_COLS, PACK_COLS))


def _from_blocks(blocks, axis):
    k, l, r, c = blocks.shape
    if axis == 2:
        return blocks.transpose(1, 2, 0, 3).reshape(l, r, k * c)
    return blocks.transpose(1, 0, 2, 3).reshape(l, k * r, c)


def _to_blocks(full, axis):
    l, r, c = full.shape
    if axis == 2:
        return full.reshape(l, r, N_CHIPS, c // N_CHIPS).transpose(2, 0, 1, 3).reshape(N_CHIPS, -1)
    return full.reshape(l, N_CHIPS, r // N_CHIPS, c).transpose(1, 0, 2, 3).reshape(N_CHIPS, -1)


def _pack_weights(shards):
    parts = [shards[n].astype(MXU_DTYPE).reshape(-1) for n, _ in BIG]
    parts += [lax.bitcast_convert_type(shards[n].reshape(-1), jnp.bfloat16).astype(MXU_DTYPE).reshape(-1) for n, _ in TINY_SHARDED]
    return _pad_rows(jnp.concatenate(parts), 32)


def _unpack_weights(gathered, shards):
    flat = gathered.reshape(N_CHIPS, -1)
    out, off = {}, 0
    for n, axis in BIG:
        size = shards[n].size
        out[n] = _from_blocks(flat[:, off:off + size].reshape((N_CHIPS,) + shards[n].shape), axis)
        off += size
    for n, axis in TINY_SHARDED:
        size = shards[n].size
        words = flat[:, off:off + 2 * size].astype(jnp.bfloat16).reshape(N_CHIPS, size, 2)
        out[n] = _from_blocks(lax.bitcast_convert_type(words, F32).reshape((N_CHIPS,) + shards[n].shape), axis)
        off += 2 * size
    return out


def _pack_grads(grads):
    flat = jnp.concatenate([_to_blocks(grads[n], axis) for n, axis in BIG], axis=1)
    g = _pad_rows(flat, 2 * 256)
    return g.reshape(N_CHIPS, 2, g.shape[1] // 2, PACK_COLS)


def _unpack_grads(reduced, shards):
    flat = reduced.reshape(-1)
    out, off = {}, 0
    for n, _ in BIG:
        out[n] = flat[off:off + shards[n].size].reshape(shards[n].shape)
        off += shards[n].size
    return out


def _pack_small(grads, loss):
    parts = [grads[n].reshape(-1) for n in REPLICATED] + [grads[n].reshape(-1) for n, _ in TINY_SHARDED] + [loss[0, :1]]
    flat = jnp.concatenate(parts)
    total = -(-flat.shape[0] // (SUBLANES * LANES)) * SUBLANES * LANES
    return jnp.pad(flat, (0, total - flat.shape[0])).reshape(-1, LANES)


def _unpack_small(summed, grads, shards, chip):
    flat = summed.reshape(-1)
    out, off = {}, 0
    for n in REPLICATED:
        out[n] = flat[off:off + grads[n].size].reshape(grads[n].shape)
        off += grads[n].size
    for n, axis in TINY_SHARDED:
        full = flat[off:off + grads[n].size].reshape(grads[n].shape)
        width = shards[n].shape[axis]
        out[n] = lax.dynamic_slice_in_dim(full, chip * width, width, axis=axis)
        off += grads[n].size
    return out, flat[off]


def kernel(x, mem, ffn1_norm, ffn1_w_in, ffn1_w_out, mix_norm, xa_norm, xa_mem_norm, xa_wq, xa_wkv, xa_wo, ffn2_norm,
           ffn2_w_in, ffn2_w_out, ab_w_in, lru_conv_w, lru_conv_b, lru_w_a, lru_b_a, lru_w_x, lru_b_x, lru_lambda,
           ab_w_out, dn_w_in, dn_conv_w, dn_a_log, dn_dt_bias, dn_o_norm, dn_w_out, final_norm, loss_target,
           m_ffn1_norm, m_ffn1_w_in, m_ffn1_w_out, m_mix_norm, m_xa_norm, m_xa_mem_norm, m_xa_wq, m_xa_wkv, m_xa_wo,
           m_ffn2_norm, m_ffn2_w_in, m_ffn2_w_out, m_ab_w_in, m_lru_conv_w, m_lru_conv_b, m_lru_w_a, m_lru_b_a,
           m_lru_w_x, m_lru_b_x, m_lru_lambda, m_ab_w_out, m_dn_w_in, m_dn_conv_w, m_dn_a_log, m_dn_dt_bias,
           m_dn_o_norm, m_dn_w_out, m_final_norm, v_ffn1_norm, v_ffn1_w_in, v_ffn1_w_out, v_mix_norm, v_xa_norm,
           v_xa_mem_norm, v_xa_wq, v_xa_wkv, v_xa_wo, v_ffn2_norm, v_ffn2_w_in, v_ffn2_w_out, v_ab_w_in,
           v_lru_conv_w, v_lru_conv_b, v_lru_w_a, v_lru_b_a, v_lru_w_x, v_lru_b_x, v_lru_lambda, v_ab_w_out,
           v_dn_w_in, v_dn_conv_w, v_dn_a_log, v_dn_dt_bias, v_dn_o_norm, v_dn_w_out, v_final_norm):
    given = dict(locals())
    shards = {n: given[n] for n in WEIGHTS}
    chip = 2 * lax.axis_index("x") + lax.axis_index("y")

    full = {n: shards[n] for n in REPLICATED}
    full.update(_unpack_weights(_gather_weights(_pack_weights(shards)), shards))
    loss, grad_x, grads = _local_step(x[0], mem[0], loss_target[0], full)

    small, loss_sum = _unpack_small(_allreduce_small(_pack_small(grads, loss)), grads, shards, chip)
    grad = {**small, **_unpack_grads(_reduce_grads(_pack_grads(grads)), shards)}

    delta, new_m, new_v = {}, {}, {}
    for n in WEIGHTS:
        delta[n], new_m[n], new_v[n] = _adamw(shards[n], grad[n], given["m_" + n], given["v_" + n], name=f"adamw_{n}")
    return (loss_sum, grad_x[None], *[grad[n] for n in WEIGHTS], *[delta[n] for n in WEIGHTS],
            *[new_m[n] for n in WEIGHTS], *[new_v[n] for n in WEIGHTS])
```

```python
import functools
import math

import jax
import jax.numpy as jnp
from jax import lax
from jax.experimental import pallas as pl
from jax.experimental.pallas import tpu as pltpu

F32 = jnp.float32
MXU_DTYPE = jnp.bfloat16
VMEM_LIMIT_BYTES = 48 * 1024 * 1024
MM_BLOCK_BYTES = 6 * 1024 * 1024
LANES = 128
SUBLANES = 8

NORM_EPS = 1e-6
CONV_K = 4
ATTN_PAIRS = 4
ATTN_HEAD_DIM = 64
ATTN_WIDTH = 512
ATTN_BLOCK = 128
DILATIONS = (1, 4, 16)
LRU_WIDTH = 512
LRU_BLOCKS = 8
LRU_C = 8.0
DN_HEADS = 8
DN_HEAD_DIM = 128
DN_WIDTH = 1024
DN_CHUNK = 64
XA_HEADS = 4
XA_HEAD_DIM = 256
D_FF = 2816
ADAM_LR, ADAM_B1, ADAM_B2, ADAM_EPS, ADAM_WD, ADAM_STEP = 0.001, 0.9, 0.999, 1e-08, 0.01, 10

MESH = pl.DeviceIdType.MESH


def _tile(n, prefs):
    for p in prefs:
        if n % p == 0:
            return p
    return n


def _params(*sem):
    return pltpu.CompilerParams(dimension_semantics=sem, vmem_limit_bytes=VMEM_LIMIT_BYTES)


def _dg(a, b, dims, hi=False):
    if hi:
        return lax.dot_general(a, b, (dims, ((), ())), precision=lax.Precision.HIGHEST, preferred_element_type=F32)
    return lax.dot_general(a.astype(MXU_DTYPE), b.astype(MXU_DTYPE), (dims, ((), ())), preferred_element_type=F32)


def _make_dot(hi):
    @jax.custom_vjp
    def dot(a, b):
        return _dg(a, b, ((1,), (0,)), hi)

    def fwd(a, b):
        return dot(a, b), (a, b)

    def bwd(r, g):
        a, b = r
        return _dg(g, b, ((1,), (1,)), hi).astype(a.dtype), _dg(a, g, ((0,), (0,)), hi).astype(b.dtype)

    dot.defvjp(fwd, bwd)

    @jax.custom_vjp
    def dot_nt(a, b):
        return _dg(a, b, ((1,), (1,)), hi)

    def fwd_nt(a, b):
        return dot_nt(a, b), (a, b)

    def bwd_nt(r, g):
        a, b = r
        return _dg(g, b, ((1,), (0,)), hi).astype(a.dtype), _dg(g, a, ((0,), (0,)), hi).astype(b.dtype)

    dot_nt.defvjp(fwd_nt, bwd_nt)

    @jax.custom_vjp
    def dot_tn(a, b):
        return _dg(a, b, ((0,), (0,)), hi)

    def fwd_tn(a, b):
        return dot_tn(a, b), (a, b)

    def bwd_tn(r, g):
        a, b = r
        return _dg(b, g, ((1,), (1,)), hi).astype(a.dtype), _dg(a, g, ((1,), (0,)), hi).astype(b.dtype)

    dot_tn.defvjp(fwd_tn, bwd_tn)
    return dot, dot_nt, dot_tn


_bdot, _bdot_nt, _bdot_tn = _make_dot(False)
_hdot, _hdot_nt, _hdot_tn = _make_dot(True)


def _log1p(t):
    return jnp.where(t < 0.01, t * (1.0 - t * (0.5 - t * (1.0 / 3.0))), jnp.log(1.0 + t))


def _neg_expm1(y):
    series = -y * (1.0 + 0.5 * y * (1.0 + (1.0 / 3.0) * y * (1.0 + 0.25 * y)))
    return jnp.where(y > -0.01, series, 1.0 - jnp.exp(y))


def _softplus(x):
    return jnp.maximum(x, 0.0) + _log1p(jnp.exp(-jnp.abs(x)))


def _sigmoid(x):
    return 1.0 / (1.0 + jnp.exp(-x))


def _silu(x):
    return x * _sigmoid(x)


def _gelu(x):
    return 0.5 * x * (1.0 + jnp.tanh(0.7978845608028654 * (x + 0.044715 * x * x * x)))


def _rows(shape):
    return lax.broadcasted_iota(jnp.int32, shape, 0)


def _cols(shape):
    return lax.broadcasted_iota(jnp.int32, shape, 1)


def _mm(a, b, *, mode="nn", out_dtype=F32, res=None, scale=1.0, name):
    if mode == "nn":
        (m, k), (k2, n) = a.shape, b.shape
    elif mode == "nt":
        (m, k), (n, k2) = a.shape, b.shape
    else:
        (k, m), (k2, n) = a.shape, b.shape
    assert k == k2, (a.shape, b.shape, mode)
    if mode == "tn":
        tm, tn, tk = _tile(m, (1024, 512, 256, 128)), _tile(n, (1024, 512, 256, 128)), _tile(k, (1024, 512, 256))
    else:
        tm, tn = _tile(m, (512, 256, 128)), _tile(n, (512, 256, 128))
        tk = k if k * 512 * 2 <= MM_BLOCK_BYTES else _tile(k, (1024, 512, 256, 128))
    nk = k // tk
    dims = {"nn": ((1,), (0,)), "nt": ((1,), (1,)), "tn": ((0,), (0,))}[mode]

    def body(*refs):
        a_ref, b_ref = refs[:2]
        r_ref = refs[2] if res is not None else None
        o_ref = refs[3 if res is not None else 2]

        def finish(r):
            if scale != 1.0:
                r = r * scale
            if res is not None:
                r = r_ref[...] + r
            o_ref[...] = r.astype(out_dtype)

        if nk == 1:
            finish(_dg(a_ref[...], b_ref[...], dims))
            return
        acc = refs[-1]
        kk = pl.program_id(2)

        @pl.when(kk == 0)
        def _():
            acc[...] = jnp.zeros_like(acc)

        acc[...] += _dg(a_ref[...], b_ref[...], dims)

        @pl.when(kk == nk - 1)
        def _():
            finish(acc[...])

    a_spec = pl.BlockSpec((tk, tm), lambda i, j, kk: (kk, i)) if mode == "tn" else pl.BlockSpec((tm, tk), lambda i, j, kk: (i, kk))
    b_spec = pl.BlockSpec((tn, tk), lambda i, j, kk: (j, kk)) if mode == "nt" else pl.BlockSpec((tk, tn), lambda i, j, kk: (kk, j))
    o_spec = pl.BlockSpec((tm, tn), lambda i, j, kk: (i, j))
    in_specs = [a_spec, b_spec] + ([o_spec] if res is not None else [])
    args = (a, b) + ((res,) if res is not None else ())
    return pl.pallas_call(
        body, name=name, grid=(m // tm, n // tn, nk), in_specs=in_specs, out_specs=o_spec,
        out_shape=jax.ShapeDtypeStruct((m, n), out_dtype), scratch_shapes=[pltpu.VMEM((tm, tn), F32)] if nk > 1 else [],
        compiler_params=_params("parallel", "parallel", "arbitrary"),
    )(*args)


def _rms_fwd(x, g, *, name):
    s, d = x.shape
    tm = _tile(s, (512, 256))

    def body(x_ref, g_ref, o_ref):
        xv = x_ref[...]
        r = lax.rsqrt(jnp.mean(xv * xv, axis=-1, keepdims=True) + NORM_EPS)
        o_ref[...] = (xv * r * g_ref[...]).astype(o_ref.dtype)

    return pl.pallas_call(
        body, name=name, grid=(s // tm,),
        in_specs=[pl.BlockSpec((tm, d), lambda i: (i, 0)), pl.BlockSpec((1, d), lambda i: (0, 0))],
        out_specs=pl.BlockSpec((tm, d), lambda i: (i, 0)), out_shape=jax.ShapeDtypeStruct((s, d), MXU_DTYPE),
        compiler_params=_params("parallel"),
    )(x, g.reshape(1, d))


def _rms_bwd(x, g, dh, dres, *, name):
    s, d = x.shape
    tm = _tile(s, (512, 256))

    def body(x_ref, g_ref, dh_ref, dr_ref, dx_ref, dg_ref):
        xv = x_ref[...]
        r = lax.rsqrt(jnp.mean(xv * xv, axis=-1, keepdims=True) + NORM_EPS)
        xh = xv * r
        dhv = dh_ref[...].astype(F32)
        dxh = dhv * g_ref[...]
        dx = r * (dxh - xh * jnp.mean(dxh * xh, axis=-1, keepdims=True))
        dx_ref[...] = dr_ref[...] + dx

        @pl.when(pl.program_id(0) == 0)
        def _():
            dg_ref[...] = jnp.zeros_like(dg_ref)

        dg_ref[...] += jnp.sum(dhv * xh, axis=0, keepdims=True)

    row = pl.BlockSpec((tm, d), lambda i: (i, 0))
    vec = pl.BlockSpec((1, d), lambda i: (0, 0))
    return pl.pallas_call(
        body, name=name, grid=(s // tm,), in_specs=[row, vec, row, row], out_specs=[row, vec],
        out_shape=[jax.ShapeDtypeStruct((s, d), F32), jax.ShapeDtypeStruct((1, d), F32)],
        compiler_params=_params("arbitrary"),
    )(x, g.reshape(1, d), dh, dres)


FFN_CHUNK = 256
FFN_TM = 256
RESIDENT = pl.Buffered(1)


def _ffn_fwd_call(x, g, w_in, w_out, *, name):
    s, d = x.shape
    f = w_out.shape[0]
    tm = _tile(s, (FFN_TM,))

    def body(x_ref, g_ref, wi_ref, wo_ref, y_ref, u_ref, act_ref):
        xv = x_ref[...]
        r = lax.rsqrt(jnp.mean(xv * xv, axis=-1, keepdims=True) + NORM_EPS)
        h = (xv * r * g_ref[...]).astype(MXU_DTYPE)
        for j in range(f // FFN_CHUNK):
            lo, hi = j * FFN_CHUNK, (j + 1) * FFN_CHUNK
            gate = _dg(h, wi_ref[:, lo:hi], ((1,), (0,))).astype(MXU_DTYPE)
            up = _dg(h, wi_ref[:, f + lo:f + hi], ((1,), (0,))).astype(MXU_DTYPE)
            u_ref[:, lo:hi] = gate
            u_ref[:, f + lo:f + hi] = up
            act_ref[:, lo:hi] = (_silu(gate.astype(F32)) * up.astype(F32)).astype(MXU_DTYPE)
        y_ref[...] = xv + 0.5 * _dg(act_ref[...], wo_ref[...], ((1,), (0,)))

    row = lambda w: pl.BlockSpec((tm, w), lambda i: (i, 0))
    return pl.pallas_call(
        body, name=name, grid=(s // tm,),
        in_specs=[row(d), pl.BlockSpec((1, d), lambda i: (0, 0)),
                  pl.BlockSpec(w_in.shape, lambda i: (0, 0), pipeline_mode=RESIDENT),
                  pl.BlockSpec(w_out.shape, lambda i: (0, 0), pipeline_mode=RESIDENT)],
        out_specs=[row(d), row(2 * f)],
        out_shape=[jax.ShapeDtypeStruct((s, d), F32), jax.ShapeDtypeStruct((s, 2 * f), MXU_DTYPE)],
        scratch_shapes=[pltpu.VMEM((tm, f), MXU_DTYPE)], compiler_params=_params("parallel"),
    )(x, g.reshape(1, d), w_in, w_out)


def _ffn_bwd_call(x, g, u, dy, w_in, w_out, *, name):
    s, d = x.shape
    f = w_out.shape[0]
    tm = _tile(s, (FFN_TM,))

    def body(x_ref, g_ref, u_ref, dy_ref, wi_ref, wo_ref, du_ref, dx_ref, dg_ref, h_ref):
        dyv = dy_ref[...]
        dyh = (0.5 * dyv).astype(MXU_DTYPE)
        for j in range(f // FFN_CHUNK):
            lo, hi = j * FFN_CHUNK, (j + 1) * FFN_CHUNK
            dact = _dg(dyh, wo_ref[lo:hi, :], ((1,), (1,)))
            gate, up = u_ref[:, lo:hi].astype(F32), u_ref[:, f + lo:f + hi].astype(F32)
            sg = _sigmoid(gate)
            du_ref[:, lo:hi] = (dact * up * sg * (1.0 + gate * (1.0 - sg))).astype(MXU_DTYPE)
            du_ref[:, f + lo:f + hi] = (dact * gate * sg).astype(MXU_DTYPE)
        dh = _dg(du_ref[...], wi_ref[...], ((1,), (1,)))
        xv, gv = x_ref[...], g_ref[...]
        r = lax.rsqrt(jnp.mean(xv * xv, axis=-1, keepdims=True) + NORM_EPS)
        xh = xv * r
        h_ref[...] = (xh * gv).astype(MXU_DTYPE)
        dxh = dh * gv
        dx_ref[...] = dyv + r * (dxh - xh * jnp.mean(dxh * xh, axis=-1, keepdims=True))

        @pl.when(pl.program_id(0) == 0)
        def _():
            dg_ref[...] = jnp.zeros_like(dg_ref)

        dg_ref[...] += jnp.sum(dh * xh, axis=0, keepdims=True)

    row = lambda w: pl.BlockSpec((tm, w), lambda i: (i, 0))
    vec = pl.BlockSpec((1, d), lambda i: (0, 0))
    return pl.pallas_call(
        body, name=name, grid=(s // tm,),
        in_specs=[row(d), vec, row(2 * f), row(d),
                  pl.BlockSpec(w_in.shape, lambda i: (0, 0), pipeline_mode=RESIDENT),
                  pl.BlockSpec(w_out.shape, lambda i: (0, 0), pipeline_mode=RESIDENT)],
        out_specs=[row(2 * f), row(d), vec, row(d)],
        out_shape=[jax.ShapeDtypeStruct((s, 2 * f), MXU_DTYPE), jax.ShapeDtypeStruct((s, d), F32),
                   jax.ShapeDtypeStruct((1, d), F32), jax.ShapeDtypeStruct((s, d), MXU_DTYPE)],
        compiler_params=_params("arbitrary"),
    )(x, g.reshape(1, d), u, dy, w_in, w_out)


def _ffn_dw_out(u, dy, *, name):
    s, f2 = u.shape
    f, d = f2 // 2, dy.shape[1]
    tf, tk = _tile(f, (1408, 256, 128)), _tile(s, (512, 256))
    nj = f // tf

    def body(g_ref, u_ref, dy_ref, o_ref):
        @pl.when(pl.program_id(1) == 0)
        def _():
            o_ref[...] = jnp.zeros_like(o_ref)

        act = _silu(g_ref[...].astype(F32)) * u_ref[...].astype(F32)
        o_ref[...] += _dg(act, 0.5 * dy_ref[...], ((0,), (0,)))

    return pl.pallas_call(
        body, name=name, grid=(nj, s // tk),
        in_specs=[pl.BlockSpec((tk, tf), lambda j, k: (k, j)), pl.BlockSpec((tk, tf), lambda j, k: (k, j + nj)),
                  pl.BlockSpec((tk, d), lambda j, k: (k, 0))],
        out_specs=pl.BlockSpec((tf, d), lambda j, k: (j, 0)), out_shape=jax.ShapeDtypeStruct((f, d), F32),
        compiler_params=_params("parallel", "arbitrary"),
    )(u, u, dy)


def _ffn_fwd(x, g, w_in, w_out, tag):
    y, u = _ffn_fwd_call(x, g, w_in, w_out, name=f"{tag}_fwd")
    return y, (x, u)


def _ffn_bwd(saved, g, w_in, w_out, dy, tag):
    x, u = saved
    du, dx, dg, h = _ffn_bwd_call(x, g, u, dy, w_in, w_out, name=f"{tag}_bwd")
    dw_out = _ffn_dw_out(u, dy, name=f"{tag}_dwout")
    dw_in = _mm(h, du, mode="tn", name=f"{tag}_dwin")
    return dx, dg[0], dw_in, dw_out


ATTN_SCALE = ATTN_HEAD_DIM ** -0.5
NEG_BIG = -1e30
PROJ_AB_BLOCKS = 5


def _head_masks():
    lane = _cols((ATTN_BLOCK, LANES))
    return lane < ATTN_HEAD_DIM, lane >= ATTN_HEAD_DIM


def _band_masks(has_prev):
    qi, kj = _rows((ATTN_BLOCK, ATTN_BLOCK)), _cols((ATTN_BLOCK, ATTN_BLOCK))
    return (kj >= qi) & has_prev, kj <= qi


def _dattn_delta(o, dcat, *, name):
    s_len = o.shape[0]
    tm = _tile(s_len, (512, 256))

    def body(o_ref, do_ref, out_ref):
        r, c = _rows((ATTN_WIDTH, ATTN_WIDTH)), _cols((ATTN_WIDTH, ATTN_WIDTH))
        ones_bd = (r // ATTN_HEAD_DIM == c // ATTN_HEAD_DIM).astype(F32)
        out_ref[...] = _dg(o_ref[...] * do_ref[...], ones_bd, ((1,), (0,)), hi=True)

    blk = pl.BlockSpec((tm, ATTN_WIDTH), lambda i: (i, 0))
    return pl.pallas_call(
        body, name=name, grid=(s_len // tm,), in_specs=[blk, blk], out_specs=blk,
        out_shape=jax.ShapeDtypeStruct((s_len, ATTN_WIDTH), F32), compiler_params=_params("parallel"),
    )(o, dcat)


def _res_rows(r, d):
    return pl.ds(r, ATTN_BLOCK, stride=d) if d > 1 else pl.ds(0, ATTN_BLOCK)


def _pair_specs(d, n_of):
    return lambda c: pl.BlockSpec((ATTN_BLOCK * d, LANES), lambda n, p: (n_of(n), c * ATTN_PAIRS + p))


def _sattn_fwd(proj, state, d, *, last, name):
    s_len = proj.shape[0]
    nb = s_len // (ATTN_BLOCK * d)
    first = state is None
    n_out = 2 if last else 3

    def body(*refs):
        q_ref, kp_ref, kc_ref, vp_ref, vc_ref = refs[:5]
        st_refs = () if first else refs[5:8]
        out_refs = refs[-n_out:]
        prev_ok, cur_ok = _band_masks(pl.program_id(0) > 0)
        hm_a, hm_b = _head_masks()

        def residue(r, carry):
            rows = _res_rows(r, d)
            q, kp, kc, vp, vc = q_ref[rows, :], kp_ref[rows, :], kc_ref[rows, :], vp_ref[rows, :], vc_ref[rows, :]
            if not first:
                m_in, l_in, a_in = (t[rows, :] for t in st_refs)
            m_out, l_out, pv, alphas = [], [], [], []
            for hm, c0 in ((hm_a, 0), (hm_b, ATTN_HEAD_DIM)):
                qh = jnp.where(hm, q, 0.0)
                sp = jnp.where(prev_ok, _dg(qh, kp, ((1,), (1,))) * ATTN_SCALE, NEG_BIG)
                sc = jnp.where(cur_ok, _dg(qh, kc, ((1,), (1,))) * ATTN_SCALE, NEG_BIG)
                m_new = jnp.maximum(jnp.max(sp, axis=1, keepdims=True), jnp.max(sc, axis=1, keepdims=True))
                if not first:
                    m_old = m_in[:, c0:c0 + 1]
                    m_new = jnp.maximum(m_old, m_new)
                    alphas.append(jnp.exp(m_old - m_new))
                pp, pc = jnp.exp(sp - m_new), jnp.exp(sc - m_new)
                l_new = jnp.sum(pp, axis=1, keepdims=True) + jnp.sum(pc, axis=1, keepdims=True)
                if not first:
                    l_new = l_new + l_in[:, c0:c0 + 1] * alphas[-1]
                pv.append(_dg(pp, jnp.where(hm, vp, 0.0), ((1,), (0,))) + _dg(pc, jnp.where(hm, vc, 0.0), ((1,), (0,))))
                m_out.append(m_new)
                l_out.append(l_new)
            acc = pv[0] + pv[1]
            if not first:
                acc = acc + a_in * jnp.where(hm_a, alphas[0], alphas[1])
            m_pair = jnp.where(hm_a, m_out[0], m_out[1])
            l_pair = jnp.where(hm_a, l_out[0], l_out[1])
            if last:
                out_refs[0][rows, :] = acc / l_pair
                out_refs[1][rows, :] = m_pair + jnp.log(l_pair)
            else:
                out_refs[0][rows, :] = m_pair
                out_refs[1][rows, :] = l_pair
                out_refs[2][rows, :] = acc
            return carry

        lax.fori_loop(0, d, residue, 0)

    cur, prev = _pair_specs(d, lambda n: n), _pair_specs(d, lambda n: jnp.maximum(n - 1, 0))
    st = cur(0)
    return tuple(pl.pallas_call(
        body, name=name, grid=(nb, ATTN_PAIRS),
        in_specs=[cur(0), prev(1), cur(1), prev(2), cur(2)] + ([] if first else [st] * 3),
        out_specs=[st] * n_out, out_shape=[jax.ShapeDtypeStruct((s_len, ATTN_WIDTH), F32)] * n_out,
        compiler_params=_params("arbitrary", "parallel"),
    )(*([proj] * 5 + ([] if first else list(state)))))


def _sattn_dq(proj, dcat, lse, delta, dq_in, d, *, name):
    s_len = proj.shape[0]
    nb = s_len // (ATTN_BLOCK * d)
    first = dq_in is None

    def body(*refs):
        q_ref, kp_ref, kc_ref, vp_ref, vc_ref, do_ref, lse_ref, dl_ref = refs[:8]
        dq_ref = refs[-1]
        prev_ok, cur_ok = _band_masks(pl.program_id(0) > 0)

        def residue(r, carry):
            rows = _res_rows(r, d)
            q, kp, kc, vp, vc, do = (t[rows, :] for t in (q_ref, kp_ref, kc_ref, vp_ref, vc_ref, do_ref))
            lse_v, dl_v = lse_ref[rows, :], dl_ref[rows, :]
            acc = jnp.zeros((ATTN_BLOCK, LANES), F32) if first else refs[8][rows, :]
            for hm, c0 in zip(_head_masks(), (0, ATTN_HEAD_DIM)):
                lse_h, dl_h = lse_v[:, c0:c0 + 1], dl_v[:, c0:c0 + 1]
                qh, doh = jnp.where(hm, q, 0.0), jnp.where(hm, do, 0.0)
                for ok, k, v in ((prev_ok, kp, vp), (cur_ok, kc, vc)):
                    s = _dg(qh, k, ((1,), (1,))) * ATTN_SCALE
                    pr = jnp.where(ok, jnp.exp(jnp.where(ok, s, NEG_BIG) - lse_h), 0.0)
                    ds = pr * (_dg(doh, v, ((1,), (1,))) - dl_h)
                    acc = acc + _dg(ds, jnp.where(hm, k, 0.0), ((1,), (0,))) * ATTN_SCALE
            dq_ref[rows, :] = acc
            return carry

        lax.fori_loop(0, d, residue, 0)

    cur, prev = _pair_specs(d, lambda n: n), _pair_specs(d, lambda n: jnp.maximum(n - 1, 0))
    st = cur(0)
    return pl.pallas_call(
        body, name=name, grid=(nb, ATTN_PAIRS),
        in_specs=[cur(0), prev(1), cur(1), prev(2), cur(2), st, st, st] + ([] if first else [st]),
        out_specs=st, out_shape=jax.ShapeDtypeStruct((s_len, ATTN_WIDTH), F32),
        compiler_params=_params("arbitrary", "parallel"),
    )(*([proj] * 5 + [dcat, lse, delta] + ([] if first else [dq_in])))


def _sattn_dkv(proj, dcat, lse, delta, dkv_in, d, *, name):
    s_len = proj.shape[0]
    nb = s_len // (ATTN_BLOCK * d)
    first = dkv_in is None

    def body(*refs):
        k_ref, v_ref = refs[:2]
        qs = (refs[2:6], refs[6:10])
        dk_ref, dv_ref = refs[-2:]
        nxt_ok, cur_ok = _band_masks(pl.program_id(0) < nb - 1)

        def residue(r, carry):
            rows = _res_rows(r, d)
            k, v = k_ref[rows, :], v_ref[rows, :]
            dk = jnp.zeros((ATTN_BLOCK, LANES), F32) if first else refs[10][rows, :]
            dv = jnp.zeros((ATTN_BLOCK, LANES), F32) if first else refs[11][rows, :]
            for ok, (q_ref, do_ref, lse_ref, dl_ref) in ((cur_ok, qs[0]), (nxt_ok, qs[1])):
                q, do, lse_v, dl_v = q_ref[rows, :], do_ref[rows, :], lse_ref[rows, :], dl_ref[rows, :]
                for hm, c0 in zip(_head_masks(), (0, ATTN_HEAD_DIM)):
                    qh, doh = jnp.where(hm, q, 0.0), jnp.where(hm, do, 0.0)
                    s = _dg(qh, k, ((1,), (1,))) * ATTN_SCALE
                    pr = jnp.where(ok, jnp.exp(jnp.where(ok, s, NEG_BIG) - lse_v[:, c0:c0 + 1]), 0.0)
                    dv = dv + _dg(pr, doh, ((0,), (0,)))
                    ds = pr * (_dg(doh, v, ((1,), (1,))) - dl_v[:, c0:c0 + 1])
                    dk = dk + _dg(ds, qh, ((0,), (0,))) * ATTN_SCALE
            dk_ref[rows, :] = dk
            dv_ref[rows, :] = dv
            return carry

        lax.fori_loop(0, d, residue, 0)

    cur, nxt = _pair_specs(d, lambda n: n), _pair_specs(d, lambda n: jnp.minimum(n + 1, nb - 1))
    st, st_n = cur(0), nxt(0)
    in_specs = [cur(1), cur(2), cur(0), st, st, st, nxt(0), st_n, st_n, st_n] + ([] if first else [st, st])
    args = [proj] * 2 + [proj, dcat, lse, delta] * 2 + ([] if first else list(dkv_in))
    return tuple(pl.pallas_call(
        body, name=name, grid=(nb, ATTN_PAIRS), in_specs=in_specs, out_specs=[st, st],
        out_shape=[jax.ShapeDtypeStruct((s_len, ATTN_WIDTH), F32)] * 2,
        compiler_params=_params("arbitrary", "parallel"),
    )(*args))


def _dattn_forward(proj, tag):
    state = None
    for i, d in enumerate(DILATIONS):
        state = _sattn_fwd(proj, state, d, last=i == len(DILATIONS) - 1, name=f"{tag}_attn_d{d}")
    return state


def _dattn_backward(proj, o, lse, dcat, tag):
    delta = _dattn_delta(o, dcat, name=f"{tag}_attn_delta")
    dq, dkv = None, None
    for d in DILATIONS:
        dq = _sattn_dq(proj, dcat, lse, delta, dq, d, name=f"{tag}_attn_dq_d{d}")
        dkv = _sattn_dkv(proj, dcat, lse, delta, dkv, d, name=f"{tag}_attn_dkv_d{d}")
    return dq, dkv[0], dkv[1]


CONV_TC = 512
CONV_T = 256


def _shift_down(ext, k, t):
    return (pltpu.roll(ext, k, 0) if k else ext)[SUBLANES:SUBLANES + t]


def _conv_fwd(src, cb0, width, w8, *, name):
    s_len = src.shape[0]
    t, tc = _tile(s_len, (CONV_T,)), CONV_TC
    tpb = t // SUBLANES

    def body(x_ref, h_ref, w_ref, y_ref):
        halo = jnp.where(pl.program_id(0) > 0, h_ref[...], 0.0)
        ext = jnp.concatenate([halo, x_ref[...]], axis=0)
        w = w_ref[...]
        y = jnp.broadcast_to(w[CONV_K:CONV_K + 1], (t, tc))
        for k in range(CONV_K):
            y = y + w[k:k + 1] * _shift_down(ext, CONV_K - 1 - k, t)
        y_ref[...] = y

    return pl.pallas_call(
        body, name=name, grid=(s_len // t, width // tc),
        in_specs=[pl.BlockSpec((t, tc), lambda i, j: (i, cb0 + j)),
                  pl.BlockSpec((SUBLANES, tc), lambda i, j: (jnp.maximum(i * tpb - 1, 0), cb0 + j)),
                  pl.BlockSpec((SUBLANES, tc), lambda i, j: (0, j))],
        out_specs=pl.BlockSpec((t, tc), lambda i, j: (i, j)), out_shape=jax.ShapeDtypeStruct((s_len, width), F32),
        compiler_params=_params("parallel", "parallel"),
    )(src, src, w8)


def _conv_bwd(src, cb0, width, w8, dy, *, name):
    s_len = src.shape[0]
    t, tc = _tile(s_len, (CONV_T,)), CONV_TC
    tpb = t // SUBLANES
    ni = s_len // t

    def body(x_ref, h_ref, w_ref, dy_ref, dn_ref, dx_ref, dw_ref):
        i = pl.program_id(1)
        halo = jnp.where(i > 0, h_ref[...], 0.0)
        ext = jnp.concatenate([halo, x_ref[...]], axis=0)
        dyv = dy_ref[...]
        extn = jnp.concatenate([dyv, jnp.where(i < ni - 1, dn_ref[...], 0.0)], axis=0)
        w = w_ref[...]
        row = _rows((SUBLANES, tc))
        dx = jnp.zeros((t, tc), F32)
        dw = jnp.where(row == CONV_K, jnp.sum(dyv, axis=0, keepdims=True), 0.0)
        for k in range(CONV_K):
            up = CONV_K - 1 - k
            dx = dx + w[k:k + 1] * (pltpu.roll(extn, t + SUBLANES - up, 0) if up else extn)[:t]
            dw = dw + jnp.where(row == k, jnp.sum(dyv * _shift_down(ext, up, t), axis=0, keepdims=True), 0.0)
        dx_ref[...] = dx.astype(dx_ref.dtype)

        @pl.when(i == 0)
        def _():
            dw_ref[...] = jnp.zeros_like(dw_ref)

        dw_ref[...] += dw

    return pl.pallas_call(
        body, name=name, grid=(width // tc, ni),
        in_specs=[pl.BlockSpec((t, tc), lambda j, i: (i, cb0 + j)),
                  pl.BlockSpec((SUBLANES, tc), lambda j, i: (jnp.maximum(i * tpb - 1, 0), cb0 + j)),
                  pl.BlockSpec((SUBLANES, tc), lambda j, i: (0, j)),
                  pl.BlockSpec((t, tc), lambda j, i: (i, j)),
                  pl.BlockSpec((SUBLANES, tc), lambda j, i: (jnp.minimum((i + 1) * tpb, s_len // SUBLANES - 1), j))],
        out_specs=[pl.BlockSpec((t, tc), lambda j, i: (i, j)), pl.BlockSpec((SUBLANES, tc), lambda j, i: (0, j))],
        out_shape=[jax.ShapeDtypeStruct((s_len, width), MXU_DTYPE), jax.ShapeDtypeStruct((SUBLANES, width), F32)],
        compiler_params=_params("parallel", "arbitrary"),
    )(src, src, w8, dy, dy)


LRU_T = 256


def _lru_gates(xc, wa, wx, ba, bx, lam):
    r = _sigmoid(_bdot(xc, wa) + ba)
    i = _sigmoid(_bdot(xc, wx) + bx)
    log_a = (-LRU_C) * r * _softplus(-lam)
    return jnp.exp(log_a), jnp.sqrt(_neg_expm1(2.0 * log_a)) * i * xc


def _block_scan(a, b, reverse):
    t = a.shape[0]
    row = _rows(a.shape)
    s = 1
    while s < t:
        shift, ok = (t - s, row < t - s) if reverse else (s, row >= s)
        b = jnp.where(ok, a * pltpu.roll(b, shift, 0) + b, b)
        a = jnp.where(ok, a * pltpu.roll(a, shift, 0), a)
        s *= 2
    return a, b


def _lru_fwd(xc, proj, wa, wx, ba, bx, lam, *, name):
    s_len, w = xc.shape
    t = _tile(s_len, (LRU_T,))

    def body(xc_ref, gr_ref, wa_ref, wx_ref, ba_ref, bx_ref, lam_ref, h_ref, y_ref, carry):
        @pl.when(pl.program_id(0) == 0)
        def _():
            carry[...] = jnp.zeros_like(carry)

        a, b = _lru_gates(xc_ref[...], wa_ref[...], wx_ref[...], ba_ref[...], bx_ref[...], lam_ref[...])
        a_cum, h0 = _block_scan(a, b, False)
        h = h0 + a_cum * carry[0:1, :]
        h_ref[...] = h
        y_ref[...] = (h * _gelu(gr_ref[...])).astype(y_ref.dtype)
        carry[0:1, :] = h[t - 1:t, :]

    row = pl.BlockSpec((t, w), lambda i: (i, 0))
    mat = pl.BlockSpec((w, w), lambda i: (0, 0))
    vec = pl.BlockSpec((1, w), lambda i: (0, 0))
    return pl.pallas_call(
        body, name=name, grid=(s_len // t,),
        in_specs=[row, pl.BlockSpec((t, w), lambda i: (i, PROJ_AB_BLOCKS - 1)), mat, mat, vec, vec, vec],
        out_specs=[row, row], out_shape=[jax.ShapeDtypeStruct((s_len, w), F32), jax.ShapeDtypeStruct((s_len, w), MXU_DTYPE)],
        scratch_shapes=[pltpu.VMEM((SUBLANES, w), F32)], compiler_params=_params("arbitrary"),
    )(xc, proj, wa, wx, ba, bx, lam)


def _lru_bwd(xc, proj, hs, dcat, wa, wx, ba, bx, lam, *, name):
    s_len, w = xc.shape
    t = _tile(s_len, (LRU_T,))
    nb = s_len // t
    tpb = t // SUBLANES

    def body(xc_ref, gr_ref, h_ref, hp_ref, dy_ref, wa_ref, wx_ref, ba_ref, bx_ref, lam_ref,
             dxc_ref, dgr_ref, dwa_ref, dwx_ref, dba_ref, dbx_ref, dlam_ref, carry):
        step = pl.program_id(0)
        params = (wa_ref[...], wx_ref[...], ba_ref[...], bx_ref[...], lam_ref[...])

        @pl.when(step == 0)
        def _():
            carry[...] = jnp.zeros_like(carry)
            for r in (dwa_ref, dwx_ref, dba_ref, dbx_ref, dlam_ref):
                r[...] = jnp.zeros_like(r)

        (a, _), vjp = jax.vjp(_lru_gates, xc_ref[...], *params)
        gr, h, dy = gr_ref[...], h_ref[...], dy_ref[...]
        gel, gel_vjp = jax.vjp(_gelu, gr)
        dgr_ref[...] = gel_vjp(dy * h)[0].astype(dgr_ref.dtype)
        dh = dy * gel
        a_cum, g0 = _block_scan(a, a * dh, True)
        big_g = g0 + a_cum * carry[0:1, :]
        row = _rows((t, w))
        g = dh + jnp.where(row == t - 1, carry[0:1, :], pltpu.roll(big_g, t - 1, 0))
        carry[0:1, :] = big_g[0:1, :]
        h_last = jnp.where(step < nb - 1, hp_ref[SUBLANES - 1:SUBLANES, :], 0.0)
        h_prev = jnp.where(row == 0, h_last, pltpu.roll(h, 1, 0))
        dxc, dwa, dwx, dba, dbx, dlam = vjp((g * h_prev, g))
        dxc_ref[...] = dxc
        dwa_ref[...] += dwa
        dwx_ref[...] += dwx
        dba_ref[...] += dba
        dbx_ref[...] += dbx
        dlam_ref[...] += dlam

    rev = lambda i: nb - 1 - i
    row = pl.BlockSpec((t, w), lambda i: (rev(i), 0))
    mat = pl.BlockSpec((w, w), lambda i: (0, 0))
    vec = pl.BlockSpec((1, w), lambda i: (0, 0))
    return pl.pallas_call(
        body, name=name, grid=(nb,),
        in_specs=[row, pl.BlockSpec((t, w), lambda i: (rev(i), PROJ_AB_BLOCKS - 1)), row,
                  pl.BlockSpec((SUBLANES, w), lambda i: (jnp.maximum(rev(i) * tpb - 1, 0), 0)),
                  pl.BlockSpec((t, w), lambda i: (rev(i), 1)), mat, mat, vec, vec, vec],
        out_specs=[row, row, mat, mat, vec, vec, vec],
        out_shape=[jax.ShapeDtypeStruct((s_len, w), F32), jax.ShapeDtypeStruct((s_len, w), MXU_DTYPE)]
        + [jax.ShapeDtypeStruct((w, w), F32)] * 2 + [jax.ShapeDtypeStruct((1, w), F32)] * 3,
        scratch_shapes=[pltpu.VMEM((SUBLANES, w), F32)], compiler_params=_params("arbitrary"),
    )(xc, proj, hs, hs, dcat, wa, wx, ba, bx, lam)


XA_T = 256
XA_SCALE = XA_HEAD_DIM ** -0.5


def _xa_head(q, k, v):
    s = _bdot_nt(q, k) * XA_SCALE
    e = jnp.exp(s - jnp.max(s, axis=-1, keepdims=True))
    return _bdot(e / jnp.sum(e, axis=-1, keepdims=True), v)


def _xa_fwd(q, kv, *, name):
    s_len, d = q.shape
    n_mem = kv.shape[0]
    t = _tile(s_len, (XA_T,))

    def body(q_ref, k_ref, v_ref, o_ref):
        for h in range(XA_HEADS):
            sl = slice(h * XA_HEAD_DIM, (h + 1) * XA_HEAD_DIM)
            o_ref[:, sl] = _xa_head(q_ref[:, sl].astype(F32), k_ref[:, sl], v_ref[:, sl]).astype(o_ref.dtype)

    return pl.pallas_call(
        body, name=name, grid=(s_len // t,),
        in_specs=[pl.BlockSpec((t, d), lambda i: (i, 0)), pl.BlockSpec((n_mem, d), lambda i: (0, 0)),
                  pl.BlockSpec((n_mem, d), lambda i: (0, 1))],
        out_specs=pl.BlockSpec((t, d), lambda i: (i, 0)), out_shape=jax.ShapeDtypeStruct((s_len, d), MXU_DTYPE),
        compiler_params=_params("parallel"),
    )(q, kv, kv)


def _xa_bwd(q, kv, do, *, name):
    s_len, d = q.shape
    n_mem = kv.shape[0]
    t = _tile(s_len, (XA_T,))

    def body(q_ref, k_ref, v_ref, do_ref, dq_ref, dk_ref, dv_ref):
        @pl.when(pl.program_id(0) == 0)
        def _():
            dk_ref[...] = jnp.zeros_like(dk_ref)
            dv_ref[...] = jnp.zeros_like(dv_ref)

        for h in range(XA_HEADS):
            sl = slice(h * XA_HEAD_DIM, (h + 1) * XA_HEAD_DIM)
            _, vjp = jax.vjp(_xa_head, q_ref[:, sl].astype(F32), k_ref[:, sl], v_ref[:, sl])
            dq, dk, dv = vjp(do_ref[:, sl].astype(F32))
            dq_ref[:, sl] = dq.astype(dq_ref.dtype)
            dk_ref[:, sl] += dk
            dv_ref[:, sl] += dv

    row = pl.BlockSpec((t, d), lambda i: (i, 0))
    dq, dk, dv = pl.pallas_call(
        body, name=name, grid=(s_len // t,),
        in_specs=[row, pl.BlockSpec((n_mem, d), lambda i: (0, 0)), pl.BlockSpec((n_mem, d), lambda i: (0, 1)), row],
        out_specs=[row, pl.BlockSpec((n_mem, d), lambda i: (0, 0)), pl.BlockSpec((n_mem, d), lambda i: (0, 0))],
        out_shape=[jax.ShapeDtypeStruct((s_len, d), MXU_DTYPE)] + [jax.ShapeDtypeStruct((n_mem, d), F32)] * 2,
        compiler_params=_params("arbitrary"),
    )(q, kv, kv, do)
    return dq, jnp.concatenate([dk, dv], axis=1)


DN_Q_SCALE = DN_HEAD_DIM ** -0.5
L2_EPS = 1e-6


def _bdg(a, b, ca, cb):
    return lax.dot_general(a.astype(MXU_DTYPE), b.astype(MXU_DTYPE), (((ca,), (cb,)), ((0,), (0,))), preferred_element_type=F32)


@jax.custom_vjp
def _bmm(a, b):
    return _bdg(a, b, 2, 1)


_bmm.defvjp(lambda a, b: (_bdg(a, b, 2, 1), (a, b)), lambda r, g: (_bdg(g, r[1], 2, 2), _bdg(r[0], g, 1, 1)))


@jax.custom_vjp
def _bmm_nt(a, b):
    return _bdg(a, b, 2, 2)


_bmm_nt.defvjp(lambda a, b: (_bdg(a, b, 2, 2), (a, b)), lambda r, g: (_bdg(g, r[1], 2, 1), _bdg(g, r[0], 1, 1)))


@jax.custom_vjp
def _bmm_tn(a, b):
    return _bdg(a, b, 1, 1)


_bmm_tn.defvjp(lambda a, b: (_bdg(a, b, 1, 1), (a, b)), lambda r, g: (_bdg(r[1], g, 2, 2), _bdg(r[0], g, 2, 1)))


def _tri_inverse(n):
    eye = (lax.broadcasted_iota(jnp.int32, n.shape, 1) == lax.broadcasted_iota(jnp.int32, n.shape, 2)).astype(F32)
    inv, p = eye - n, n
    for _ in range(5):
        p = _bdg(p, p, 2, 1)
        inv = _bdg(inv, eye + p, 2, 1)
    return inv


@jax.custom_vjp
def _tri_solve2(n, r1, r2):
    t = _tri_inverse(n)
    return _bdg(t, r1, 2, 1), _bdg(t, r2, 2, 1)


def _tri_solve2_fwd(n, r1, r2):
    t = _tri_inverse(n)
    x1, x2 = _bdg(t, r1, 2, 1), _bdg(t, r2, 2, 1)
    return (x1, x2), (t, x1, x2)


def _tri_solve2_bwd(saved, cts):
    t, x1, x2 = saved
    d1, d2 = _bdg(t, cts[0], 1, 1), _bdg(t, cts[1], 1, 1)
    return -(_bdg(d1, x1, 2, 2) + _bdg(d2, x2, 2, 2)), d1, d2


_tri_solve2.defvjp(_tri_solve2_fwd, _tri_solve2_bwd)


def _dn_gates(ab, alog, dtb):
    return -jnp.exp(alog) * _softplus(ab + dtb), _sigmoid(ab)


def _dn_heads(cq, ck, cv, z, g, beta, onorm, state):
    h, c, _ = cq.shape
    l2 = lambda t: t * lax.rsqrt(jnp.sum(t * t, axis=-1, keepdims=True) + L2_EPS)
    q, k, v = l2(_silu(cq)) * DN_Q_SCALE, l2(_silu(ck)), _silu(cv)
    r, cc = lax.broadcasted_iota(jnp.int32, (h, c, c), 1), lax.broadcasted_iota(jnp.int32, (h, c, c), 2)
    tri, eye = r >= cc, r == cc
    g_row = jnp.sum(jnp.where(eye, g, 0.0), axis=1, keepdims=True)
    gcum_c = jnp.sum(jnp.where(tri, g_row, 0.0), axis=2, keepdims=True)
    gcum_r = jnp.sum(jnp.where(cc >= r, g, 0.0), axis=1, keepdims=True)
    decay = jnp.where(tri, jnp.exp(jnp.where(tri, gcum_c - gcum_r, 0.0)), 0.0)
    kb = k * beta
    n = jnp.where(r > cc, _bmm_nt(kb, k) * decay, 0.0)
    u, w = _tri_solve2(n, v * beta, kb * jnp.exp(gcum_c))
    v_new = u - _bmm(w, state)
    o = _bmm(q * jnp.exp(gcum_c), state) + _bmm(_bmm_nt(q, k) * decay, v_new)
    g_last = jnp.sum(g, axis=1, keepdims=True)
    new_state = state * jnp.exp(g_last) + _bmm_tn(k * jnp.exp(g_last - gcum_c), v_new)
    on = o * lax.rsqrt(jnp.mean(o * o, axis=-1, keepdims=True) + NORM_EPS) * onorm
    return on * _silu(z), new_state


def _dn_stack(ref, col0):
    return jnp.stack([ref[:, col0 + h * DN_HEAD_DIM:col0 + (h + 1) * DN_HEAD_DIM].astype(F32) for h in range(DN_HEADS)], axis=0)


def _dn_cols(block, col0):
    return jnp.stack([block[:, col0 + h:col0 + h + 1] for h in range(DN_HEADS)], axis=0)


def _dn_fwd(cqkv, proj, ab, alog, dtb, onorm, *, name):
    s_len = cqkv.shape[0]
    c, hd, w = DN_CHUNK, DN_HEAD_DIM, DN_WIDTH
    n_chunks = s_len // c

    def body(c_ref, z_ref, ab_ref, alog_ref, dtb_ref, on_ref, o_ref, st_ref, state):
        @pl.when(pl.program_id(0) == 0)
        def _():
            state[...] = jnp.zeros_like(state)

        g_all, beta_all = _dn_gates(ab_ref[...], alog_ref[...], dtb_ref[...])
        st = state[...]
        st_ref[0] = st
        out, new = _dn_heads(_dn_stack(c_ref, 0), _dn_stack(c_ref, w), _dn_stack(c_ref, 2 * w), _dn_stack(z_ref, 0),
                             _dn_cols(g_all, 0), _dn_cols(beta_all, DN_HEADS), on_ref[...], st)
        state[...] = new
        for h in range(DN_HEADS):
            o_ref[:, h * hd:(h + 1) * hd] = out[h].astype(o_ref.dtype)

    vec = pl.BlockSpec((1, LANES), lambda i: (0, 0))
    return pl.pallas_call(
        body, name=name, grid=(n_chunks,),
        in_specs=[pl.BlockSpec((c, 3 * w), lambda i: (i, 0)), pl.BlockSpec((c, w), lambda i: (i, 3)),
                  pl.BlockSpec((c, LANES), lambda i: (i, 0)), vec, vec, vec],
        out_specs=[pl.BlockSpec((c, w), lambda i: (i, 0)), pl.BlockSpec((1, DN_HEADS, hd, hd), lambda i: (i, 0, 0, 0))],
        out_shape=[jax.ShapeDtypeStruct((s_len, w), MXU_DTYPE), jax.ShapeDtypeStruct((n_chunks, DN_HEADS, hd, hd), F32)],
        scratch_shapes=[pltpu.VMEM((DN_HEADS, hd, hd), F32)], compiler_params=_params("arbitrary"),
    )(cqkv, proj, ab, alog, dtb, onorm)


def _dn_bwd(cqkv, proj, ab, alog, dtb, onorm, states, dout, *, name):
    s_len = cqkv.shape[0]
    c, hd, w = DN_CHUNK, DN_HEAD_DIM, DN_WIDTH
    n_chunks = s_len // c

    def body(c_ref, z_ref, ab_ref, alog_ref, dtb_ref, on_ref, st_ref, do_ref,
             dc_ref, dz_ref, dab_ref, dalog_ref, ddtb_ref, don_ref, dstate):
        @pl.when(pl.program_id(0) == 0)
        def _():
            dstate[...] = jnp.zeros_like(dstate)
            for r in (dalog_ref, ddtb_ref, don_ref):
                r[...] = jnp.zeros_like(r)

        (g_all, beta_all), gates_vjp = jax.vjp(_dn_gates, ab_ref[...], alog_ref[...], dtb_ref[...])
        _, vjp = jax.vjp(_dn_heads, _dn_stack(c_ref, 0), _dn_stack(c_ref, w), _dn_stack(c_ref, 2 * w), _dn_stack(z_ref, 0),
                         _dn_cols(g_all, 0), _dn_cols(beta_all, DN_HEADS), on_ref[...], st_ref[0])
        dcq, dck, dcv, dz, dg, dbeta, don, dst = vjp((_dn_stack(do_ref, 0), dstate[...]))
        dstate[...] = dst
        col = _cols((c, LANES))
        dg_all, dbeta_all = jnp.zeros((c, LANES), F32), jnp.zeros((c, LANES), F32)
        for h in range(DN_HEADS):
            sl = slice(h * hd, (h + 1) * hd)
            dc_ref[:, sl] = dcq[h]
            dc_ref[:, w + h * hd:w + (h + 1) * hd] = dck[h]
            dc_ref[:, 2 * w + h * hd:2 * w + (h + 1) * hd] = dcv[h]
            dz_ref[:, sl] = dz[h].astype(dz_ref.dtype)
            dg_all = dg_all + jnp.where(col == h, dg[h], 0.0)
            dbeta_all = dbeta_all + jnp.where(col == DN_HEADS + h, dbeta[h], 0.0)
        dab, dalog, ddtb = gates_vjp((dg_all, dbeta_all))
        dab_ref[...] = dab
        dalog_ref[...] += dalog
        ddtb_ref[...] += ddtb
        don_ref[...] += don

    rev = lambda i: n_chunks - 1 - i
    vec = pl.BlockSpec((1, LANES), lambda i: (0, 0))
    return pl.pallas_call(
        body, name=name, grid=(n_chunks,),
        in_specs=[pl.BlockSpec((c, 3 * w), lambda i: (rev(i), 0)), pl.BlockSpec((c, w), lambda i: (rev(i), 3)),
                  pl.BlockSpec((c, LANES), lambda i: (rev(i), 0)), vec, vec, vec,
                  pl.BlockSpec((1, DN_HEADS, hd, hd), lambda i: (rev(i), 0, 0, 0)), pl.BlockSpec((c, w), lambda i: (rev(i), 0))],
        out_specs=[pl.BlockSpec((c, 3 * w), lambda i: (rev(i), 0)), pl.BlockSpec((c, w), lambda i: (rev(i), 0)),
                   pl.BlockSpec((c, LANES), lambda i: (rev(i), 0)), vec, vec, vec],
        out_shape=[jax.ShapeDtypeStruct((s_len, 3 * w), F32), jax.ShapeDtypeStruct((s_len, w), MXU_DTYPE),
                   jax.ShapeDtypeStruct((s_len, LANES), F32)] + [jax.ShapeDtypeStruct((1, LANES), F32)] * 3,
        scratch_shapes=[pltpu.VMEM((DN_HEADS, hd, hd), F32)], compiler_params=_params("arbitrary"),
    )(cqkv, proj, ab, alog, dtb, onorm, states, dout)


def _final_loss(x, g, target, *, name):
    s, d = x.shape
    tm = _tile(s, (512, 256))

    def body(x_ref, g_ref, t_ref, loss_ref, dx_ref, dg_ref):
        @pl.when(pl.program_id(0) == 0)
        def _():
            loss_ref[...] = jnp.zeros_like(loss_ref)
            dg_ref[...] = jnp.zeros_like(dg_ref)

        xv, gv = x_ref[...], g_ref[...]
        r = lax.rsqrt(jnp.mean(xv * xv, axis=-1, keepdims=True) + NORM_EPS)
        xh = xv * r
        err = xh * gv - t_ref[...]
        loss_ref[...] += 0.5 * jnp.sum(jnp.mean(err * err, axis=-1, keepdims=True), axis=0, keepdims=True)
        dy = err * (1.0 / d)
        dxh = dy * gv
        dx_ref[...] = r * (dxh - xh * jnp.mean(dxh * xh, axis=-1, keepdims=True))
        dg_ref[...] += jnp.sum(dy * xh, axis=0, keepdims=True)

    row = pl.BlockSpec((tm, d), lambda i: (i, 0))
    vec = pl.BlockSpec((1, d), lambda i: (0, 0))
    return pl.pallas_call(
        body, name=name, grid=(s // tm,), in_specs=[row, vec, row],
        out_specs=[pl.BlockSpec((1, LANES), lambda i: (0, 0)), row, vec],
        out_shape=[jax.ShapeDtypeStruct((1, LANES), F32), jax.ShapeDtypeStruct((s, d), F32), jax.ShapeDtypeStruct((1, d), F32)],
        compiler_params=_params("arbitrary"),
    )(x, g.reshape(1, d), target)


def _adamw(w, g, m, v, *, name):
    shape = w.shape
    cols = shape[-1]
    rows = max(w.size // cols, 1)
    tr = _tile(rows, (512, 352, 256, 128, 64, 32, 16, 8))
    c1, c2 = 1.0 - ADAM_B1 ** ADAM_STEP, 1.0 - ADAM_B2 ** ADAM_STEP

    def body(w_ref, g_ref, m_ref, v_ref, d_ref, nm_ref, nv_ref):
        gv = g_ref[...]
        nm = ADAM_B1 * m_ref[...] + (1.0 - ADAM_B1) * gv
        nv = ADAM_B2 * v_ref[...] + (1.0 - ADAM_B2) * (gv * gv)
        d_ref[...] = -ADAM_LR * ((nm / c1) / (jnp.sqrt(nv / c2) + ADAM_EPS) + ADAM_WD * w_ref[...])
        nm_ref[...] = nm
        nv_ref[...] = nv

    blk = pl.BlockSpec((tr, cols), lambda i: (i, 0))
    outs = pl.pallas_call(
        body, name=name, grid=(rows // tr,), in_specs=[blk] * 4, out_specs=[blk] * 3,
        out_shape=[jax.ShapeDtypeStruct((rows, cols), F32)] * 3, compiler_params=_params("parallel"),
    )(*(t.reshape(rows, cols) for t in (w, g, m, v)))
    return tuple(t.reshape(shape) for t in outs)


def _block_diag(w):
    n, j, k = w.shape
    eye = jnp.eye(n, dtype=w.dtype)
    return (eye[:, None, :, None] * w[:, :, None, :]).reshape(n * j, n * k)


def _block_diag_part(m, n):
    j, k = m.shape[0] // n, m.shape[1] // n
    m4 = m.reshape(n, j, n, k)
    return jnp.stack([m4[i, :, i, :] for i in range(n)], axis=0)


DN_AB = 2 * DN_HEADS
DEPTH = 2


def _row(v, width=None):
    v = v.reshape(1, -1)
    return v if width is None else jnp.pad(v, ((0, 0), (0, width - v.shape[1])))


def _conv_w8(conv_w, bias=None):
    w8 = jnp.zeros((SUBLANES, conv_w.shape[1]), F32).at[:CONV_K].set(conv_w)
    return w8 if bias is None else w8.at[CONV_K].set(bias)


def _mixer_ab_fwd(x, w, tag):
    h = _rms_fwd(x, w["mix_norm"][0], name=f"{tag}_norm")
    proj = _mm(h, w["ab_w_in"][0], name=f"{tag}_in")
    o, lse = _dattn_forward(proj, tag)
    w8 = _conv_w8(w["lru_conv_w"][0], w["lru_conv_b"][0])
    xc = _conv_fwd(proj, PROJ_AB_BLOCKS - 2, LRU_WIDTH, w8, name=f"{tag}_conv")
    wa, wx = _block_diag(w["lru_w_a"][0]), _block_diag(w["lru_w_x"][0])
    vecs = (_row(w["lru_b_a"][0]), _row(w["lru_b_x"][0]), _row(w["lru_lambda"][0]))
    hs, y = _lru_fwd(xc, proj, wa, wx, *vecs, name=f"{tag}_lru")
    w_out = w["ab_w_out"][0]
    x2 = _mm(o, w_out[:ATTN_WIDTH], res=x, name=f"{tag}_out_attn")
    x2 = _mm(y, w_out[ATTN_WIDTH:], res=x2, name=f"{tag}_out_lru")
    return x2, (x, h, proj, o, lse, w8, xc, wa, wx, vecs, hs, y)


def _mixer_ab_bwd(saved, w, dy, tag):
    x, h, proj, o, lse, w8, xc, wa, wx, vecs, hs, y = saved
    w_out = w["ab_w_out"][0]
    dcat = _mm(dy, w_out, mode="nt", name=f"{tag}_dcat")
    dw_out = jnp.concatenate([_mm(o, dy, mode="tn", name=f"{tag}_dwout_attn"), _mm(y, dy, mode="tn", name=f"{tag}_dwout_lru")], axis=0)
    dq, dk, dv = _dattn_backward(proj, o, lse, dcat, tag)
    dxc, dgr, dwa, dwx, dba, dbx, dlam = _lru_bwd(xc, proj, hs, dcat, wa, wx, *vecs, name=f"{tag}_dlru")
    dxr, dw8 = _conv_bwd(proj, PROJ_AB_BLOCKS - 2, LRU_WIDTH, w8, dxc, name=f"{tag}_dconv")
    dproj = jnp.concatenate([t.astype(MXU_DTYPE) for t in (dq, dk, dv, dxr, dgr)], axis=1)
    dw_in = _mm(h, dproj, mode="tn", name=f"{tag}_dwin")
    dh = _mm(dproj, w["ab_w_in"][0], mode="nt", name=f"{tag}_dh")
    dx, dg = _rms_bwd(x, w["mix_norm"][0], dh, dy, name=f"{tag}_dnorm")
    grads = dict(mix_norm=dg[0], ab_w_in=dw_in, ab_w_out=dw_out, lru_conv_w=dw8[:CONV_K], lru_conv_b=dw8[CONV_K],
                 lru_w_a=_block_diag_part(dwa, LRU_BLOCKS), lru_b_a=dba[0], lru_w_x=_block_diag_part(dwx, LRU_BLOCKS),
                 lru_b_x=dbx[0], lru_lambda=dlam[0])
    return dx, grads


def _dn_split_w(w_in):
    return w_in[:, :4 * DN_WIDTH], jnp.pad(w_in[:, 4 * DN_WIDTH:], ((0, 0), (0, LANES - DN_AB)))


def _mixer_dn_fwd(x, w, tag):
    h = _rms_fwd(x, w["mix_norm"][1], name=f"{tag}_norm")
    w_qkvz, w_ab = _dn_split_w(w["dn_w_in"][0])
    proj = _mm(h, w_qkvz, name=f"{tag}_in")
    ab = _mm(h, w_ab, name=f"{tag}_in_ab")
    w8 = _conv_w8(w["dn_conv_w"][0])
    cqkv = _conv_fwd(proj, 0, 3 * DN_WIDTH, w8, name=f"{tag}_conv")
    vecs = (_row(w["dn_a_log"][0], LANES), _row(w["dn_dt_bias"][0], LANES), _row(w["dn_o_norm"][0]))
    og, states = _dn_fwd(cqkv, proj, ab, *vecs, name=f"{tag}_dn")
    x2 = _mm(og, w["dn_w_out"][0], res=x, name=f"{tag}_out")
    return x2, (x, h, w_qkvz, w_ab, proj, ab, w8, cqkv, vecs, og, states)


def _mixer_dn_bwd(saved, w, dy, tag):
    x, h, w_qkvz, w_ab, proj, ab, w8, cqkv, vecs, og, states = saved
    dout = _mm(dy, w["dn_w_out"][0], mode="nt", name=f"{tag}_dout")
    dw_out = _mm(og, dy, mode="tn", name=f"{tag}_dwout")
    dcqkv, dz, dab, dalog, ddtb, don = _dn_bwd(cqkv, proj, ab, *vecs, states, dout, name=f"{tag}_ddn")
    dqkv, dw8 = _conv_bwd(proj, 0, 3 * DN_WIDTH, w8, dcqkv, name=f"{tag}_dconv")
    dproj = jnp.concatenate([dqkv.astype(MXU_DTYPE), dz.astype(MXU_DTYPE)], axis=1)
    dw_in = jnp.concatenate([_mm(h, dproj, mode="tn", name=f"{tag}_dwin"),
                             _mm(h, dab, mode="tn", name=f"{tag}_dwin_ab")[:, :DN_AB]], axis=1)
    dh = _mm(dproj, w_qkvz, mode="nt", name=f"{tag}_dh")
    dh = _mm(dab, w_ab, mode="nt", res=dh, name=f"{tag}_dh_ab")
    dx, dg = _rms_bwd(x, w["mix_norm"][1], dh, dy, name=f"{tag}_dnorm")
    grads = dict(mix_norm=dg[0], dn_w_in=dw_in, dn_w_out=dw_out, dn_conv_w=dw8[:CONV_K], dn_a_log=dalog[0, :DN_HEADS],
                 dn_dt_bias=ddtb[0, :DN_HEADS], dn_o_norm=don[0])
    return dx, grads


def _xa_layer_fwd(x, mem, w, layer, tag):
    hq = _rms_fwd(x, w["xa_norm"][layer], name=f"{tag}_norm")
    q = _mm(hq, w["xa_wq"][layer], out_dtype=MXU_DTYPE, name=f"{tag}_q")
    hm = _rms_fwd(mem, w["xa_mem_norm"][layer], name=f"{tag}_mem_norm")
    kv = _mm(hm, w["xa_wkv"][layer], name=f"{tag}_kv")
    oa = _xa_fwd(q, kv, name=f"{tag}_core")
    x2 = _mm(oa, w["xa_wo"][layer], res=x, name=f"{tag}_out")
    return x2, (x, hq, q, hm, kv, oa)


def _xa_layer_bwd(saved, mem, w, layer, dy, tag):
    x, hq, q, hm, kv, oa = saved
    do = _mm(dy, w["xa_wo"][layer], mode="nt", name=f"{tag}_do")
    dwo = _mm(oa, dy, mode="tn", name=f"{tag}_dwo")
    dq, dkv = _xa_bwd(q, kv, do, name=f"{tag}_dcore")
    dwq = _mm(hq, dq, mode="tn", name=f"{tag}_dwq")
    dhq = _mm(dq, w["xa_wq"][layer], mode="nt", name=f"{tag}_dhq")
    dx, dg = _rms_bwd(x, w["xa_norm"][layer], dhq, dy, name=f"{tag}_dnorm")
    dwkv = _mm(hm, dkv, mode="tn", name=f"{tag}_dwkv")
    dhm = _mm(dkv, w["xa_wkv"][layer], mode="nt", name=f"{tag}_dhm")
    _, dgm = _rms_bwd(mem, w["xa_mem_norm"][layer], dhm, jnp.zeros_like(mem), name=f"{tag}_dmem_norm")
    return dx, dict(xa_norm=dg[0], xa_mem_norm=dgm[0], xa_wq=dwq, xa_wkv=dwkv, xa_wo=dwo)


def _local_step(x, mem, target, w):
    saved = []
    for layer in range(DEPTH):
        t = f"l{layer}"
        x, s1 = _ffn_fwd(x, w["ffn1_norm"][layer], w["ffn1_w_in"][layer], w["ffn1_w_out"][layer], f"{t}_ffn1")
        x, s2 = (_mixer_ab_fwd if layer % 2 == 0 else _mixer_dn_fwd)(x, w, f"{t}_mix")
        x, s3 = _xa_layer_fwd(x, mem, w, layer, f"{t}_xa")
        x, s4 = _ffn_fwd(x, w["ffn2_norm"][layer], w["ffn2_w_in"][layer], w["ffn2_w_out"][layer], f"{t}_ffn2")
        saved.append((s1, s2, s3, s4))
    loss, dx, dgf = _final_loss(x, w["final_norm"], target, name="final_loss")
    per_layer = []
    for layer in reversed(range(DEPTH)):
        t = f"l{layer}"
        s1, s2, s3, s4 = saved[layer]
        g = {}
        dx, g["ffn2_norm"], g["ffn2_w_in"], g["ffn2_w_out"] = _ffn_bwd(
            s4, w["ffn2_norm"][layer], w["ffn2_w_in"][layer], w["ffn2_w_out"][layer], dx, f"{t}_ffn2")
        dx, gx = _xa_layer_bwd(s3, mem, w, layer, dx, f"{t}_xa")
        dx, gm = (_mixer_ab_bwd if layer % 2 == 0 else _mixer_dn_bwd)(s2, w, dx, f"{t}_mix")
        dx, g["ffn1_norm"], g["ffn1_w_in"], g["ffn1_w_out"] = _ffn_bwd(
            s1, w["ffn1_norm"][layer], w["ffn1_w_in"][layer], w["ffn1_w_out"][layer], dx, f"{t}_ffn1")
        per_layer.append({**g, **gx, **gm})
    per_layer.reverse()
    grads = {"final_norm": dgf[0]}
    for name in per_layer[0]:
        if name in per_layer[1]:
            grads[name] = jnp.stack([per_layer[0][name], per_layer[1][name]], axis=0)
        else:
            grads[name] = per_layer[0][name][None]
    for name in per_layer[1]:
        if name not in per_layer[0]:
            grads[name] = per_layer[1][name][None]
    return loss, dx, grads


N_CHIPS = 4
WIRE_DTYPE = jnp.bfloat16
HBM_SPEC = pl.BlockSpec(memory_space=pltpu.HBM)
PACK_COLS = 1024


def _place():
    x, y, c = lax.axis_index("x"), lax.axis_index("y"), lax.axis_index("c")
    return x, y, c, [(1 - x, y), (x, 1 - y), (1 - x, 1 - y)]


def _remote(src, dst, sems, k, to):
    return pltpu.make_async_remote_copy(src_ref=src, dst_ref=dst, send_sem=sems[0].at[k], recv_sem=sems[1].at[k],
                                        device_id=to, device_id_type=MESH)


def _gather_weights(p):
    rows, cols = p.shape
    half = rows // 2

    def body(p_ref, out_ref, send_sems, recv_sems, local_sem):
        x, y, c, chips = _place()
        sems = (send_sems, recv_sems)
        part = lambda cx, cy, h: out_ref.at[2 * cx + cy, pl.ds(h * half, half), :]
        mine = pltpu.make_async_copy(p_ref, out_ref.at[2 * x + y], local_sem)
        mine.start()
        first = [_remote(p_ref.at[pl.ds(c * half, half), :], part(x, y, c), sems, j, (*chip, c)) for j, chip in enumerate(chips)]
        for cp in first:
            cp.start()
        passed = [_remote(part(*chip, c), part(*chip, c), sems, 3 + j, (x, y, 1 - c)) for j, chip in enumerate(chips)]
        for j, chip in enumerate(chips):
            _remote(part(*chip, c), part(*chip, c), sems, j, (*chip, c)).wait_recv()
            passed[j].start()
        for j, chip in enumerate(chips):
            _remote(part(*chip, 1 - c), part(*chip, 1 - c), sems, 3 + j, (x, y, 1 - c)).wait_recv()
        for cp in first + passed:
            cp.wait_send()
        mine.wait()

    return pl.pallas_call(
        body, name="gather_weights", in_specs=[HBM_SPEC], out_specs=HBM_SPEC,
        out_shape=jax.ShapeDtypeStruct((N_CHIPS, rows, cols), p.dtype),
        scratch_shapes=[pltpu.SemaphoreType.DMA((6,)), pltpu.SemaphoreType.DMA((6,)), pltpu.SemaphoreType.DMA],
    )(p)


def _allreduce_small(v):
    rows, cols = v.shape
    n_dev = 2 * N_CHIPS

    def body(v_ref, out_ref, all_ref, send_sems, recv_sems, local_sem):
        x, y, c, chips = _place()
        sems = (send_sems, recv_sems)
        me, sibling = (x, y, c), (x, y, 1 - c)
        slot = lambda px, py, pc: all_ref.at[pl.ds((4 * px + 2 * py + pc) * rows, rows), :]
        mine = pltpu.make_async_copy(v_ref, slot(*me), local_sem)
        mine.start()
        first = [_remote(v_ref, slot(*me), sems, 0, sibling)]
        first += [_remote(v_ref, slot(*me), sems, 1 + j, (*chip, c)) for j, chip in enumerate(chips)]
        for cp in first:
            cp.start()
        passed = [_remote(slot(*chip, c), slot(*chip, c), sems, 4 + j, sibling) for j, chip in enumerate(chips)]
        for j, chip in enumerate(chips):
            _remote(slot(*chip, c), slot(*chip, c), sems, 1 + j, me).wait_recv()
            passed[j].start()
        _remote(slot(*sibling), slot(*sibling), sems, 0, me).wait_recv()
        for j, chip in enumerate(chips):
            _remote(slot(*chip, 1 - c), slot(*chip, 1 - c), sems, 4 + j, me).wait_recv()
        for cp in first + passed:
            cp.wait_send()
        mine.wait()
        acc = all_ref[pl.ds(0, rows), :]
        for k in range(1, n_dev):
            acc = acc + all_ref[pl.ds(k * rows, rows), :]
        out_ref[...] = acc

    vmem = pl.BlockSpec(memory_space=pltpu.VMEM)
    return pl.pallas_call(
        body, name="allreduce_small", in_specs=[vmem], out_specs=vmem, out_shape=jax.ShapeDtypeStruct((rows, cols), F32),
        scratch_shapes=[pltpu.VMEM((n_dev * rows, cols), F32), pltpu.SemaphoreType.DMA((7,)), pltpu.SemaphoreType.DMA((7,)),
                        pltpu.SemaphoreType.DMA],
    )(v)


def _swap_sibling(v, *, name):
    def body(v_ref, out_ref, send_sems, recv_sems):
        x, y, c, _ = _place()
        cp = _remote(v_ref, out_ref, (send_sems, recv_sems), 0, (x, y, 1 - c))
        cp.start()
        cp.wait()

    return pl.pallas_call(
        body, name=name, in_specs=[HBM_SPEC], out_specs=HBM_SPEC, out_shape=jax.ShapeDtypeStruct(v.shape, v.dtype),
        scratch_shapes=[pltpu.SemaphoreType.DMA((1,)), pltpu.SemaphoreType.DMA((1,))],
    )(v)


def _exchange_chips(v):
    def body(v_ref, out_ref, send_sems, recv_sems, local_sem):
        x, y, c, chips = _place()
        sems = (send_sems, recv_sems)
        me = 2 * x + y
        mine = pltpu.make_async_copy(v_ref.at[me], out_ref.at[me], local_sem)
        mine.start()
        sends = [_remote(v_ref.at[2 * cx + cy], out_ref.at[me], sems, j, (cx, cy, c)) for j, (cx, cy) in enumerate(chips)]
        for cp in sends:
            cp.start()
        for j, (cx, cy) in enumerate(chips):
            _remote(v_ref.at[me], out_ref.at[2 * cx + cy], sems, j, (cx, cy, c)).wait_recv()
        for cp in sends:
            cp.wait_send()
        mine.wait()

    return pl.pallas_call(
        body, name="exchange_chips", in_specs=[HBM_SPEC], out_specs=HBM_SPEC, out_shape=jax.ShapeDtypeStruct(v.shape, v.dtype),
        scratch_shapes=[pltpu.SemaphoreType.DMA((3,)), pltpu.SemaphoreType.DMA((3,)), pltpu.SemaphoreType.DMA],
    )(v)


def _share_halves(v):
    def body(v_ref, out_ref, send_sems, recv_sems, local_sem):
        x, y, c, _ = _place()
        mine = pltpu.make_async_copy(v_ref, out_ref.at[c], local_sem)
        mine.start()
        cp = _remote(v_ref, out_ref.at[c], (send_sems, recv_sems), 0, (x, y, 1 - c))
        cp.start()
        _remote(v_ref, out_ref.at[1 - c], (send_sems, recv_sems), 0, (x, y, 1 - c)).wait_recv()
        cp.wait_send()
        mine.wait()

    return pl.pallas_call(
        body, name="share_halves", in_specs=[HBM_SPEC], out_specs=HBM_SPEC, out_shape=jax.ShapeDtypeStruct((2,) + v.shape, v.dtype),
        scratch_shapes=[pltpu.SemaphoreType.DMA((1,)), pltpu.SemaphoreType.DMA((1,)), pltpu.SemaphoreType.DMA],
    )(v)


def _sum_blocks(terms, *, out_dtype, name):
    _, rows, cols = terms[0][0].shape
    tr = _tile(rows, (256, 128, 64, 32, 16, 8))

    def body(*refs):
        acc = refs[0][...].astype(F32)
        for r in refs[1:-1]:
            acc = acc + r[...].astype(F32)
        refs[-1][...] = acc.astype(out_dtype)

    spec = lambda k: pl.BlockSpec((None, tr, cols), lambda i: (k, i, 0))
    return pl.pallas_call(
        body, name=name, grid=(rows // tr,), in_specs=[spec(k) for _, k in terms],
        out_specs=pl.BlockSpec((tr, cols), lambda i: (i, 0)), out_shape=jax.ShapeDtypeStruct((rows, cols), out_dtype),
        compiler_params=_params("parallel"),
    )(*(a for a, _ in terms))


def _reduce_grads(g4):
    c = lax.axis_index("c")
    keep = lax.dynamic_index_in_dim(g4, c, axis=1, keepdims=False)
    give = lax.dynamic_index_in_dim(g4, 1 - c, axis=1, keepdims=False)
    got = _swap_sibling(give, name="reduce_swap")
    n, rows, cols = keep.shape
    chip_sum = _sum_blocks([(keep.reshape(1, n * rows, cols), 0), (got.reshape(1, n * rows, cols), 0)],
                           out_dtype=WIRE_DTYPE, name="reduce_sum_cores")
    parts = _exchange_chips(chip_sum.reshape(n, rows, cols))
    half = _sum_blocks([(parts, k) for k in range(N_CHIPS)], out_dtype=F32, name="reduce_sum_chips")
    return _share_halves(half)


BIG = (("ffn1_w_in", 2), ("ffn1_w_out", 1), ("xa_wq", 1), ("xa_wkv", 2), ("xa_wo", 1), ("ffn2_w_in", 2), ("ffn2_w_out", 1),
       ("ab_w_in", 2), ("ab_w_out", 1), ("dn_w_in", 2), ("dn_w_out", 1))
TINY_SHARDED = (("lru_conv_w", 2), ("dn_conv_w", 2))
REPLICATED = ("ffn1_norm", "mix_norm", "xa_norm", "xa_mem_norm", "ffn2_norm", "lru_conv_b", "lru_w_a", "lru_b_a", "lru_w_x",
              "lru_b_x", "lru_lambda", "dn_a_log", "dn_dt_bias", "dn_o_norm", "final_norm")
WEIGHTS = ("ffn1_norm", "ffn1_w_in", "ffn1_w_out", "mix_norm", "xa_norm", "xa_mem_norm", "xa_wq", "xa_wkv", "xa_wo", "ffn2_norm",
           "ffn2_w_in", "ffn2_w_out", "ab_w_in", "lru_conv_w", "lru_conv_b", "lru_w_a", "lru_b_a", "lru_w_x", "lru_b_x",
           "lru_lambda", "ab_w_out", "dn_w_in", "dn_conv_w", "dn_a_log", "dn_dt_bias", "dn_o_norm", "dn_w_out", "final_norm")


def _pad_rows(flat, row_multiple):
    n = flat.shape[-1]
    per = row_multiple * PACK_COLS
    total = -(-n // per) * per
    flat = jnp.pad(flat, [(0, 0)] * (flat.ndim - 1) + [(0, total - n)])
    return flat.reshape(flat.shape[:-1] + (total // PACK_COLS, PACK_COLS))


def _lane_padded(shape):
    return shape[:-1] + (-(-shape[-1] // LANES) * LANES,)


def _pad_lanes(t):
    return jnp.pad(t, [(0, 0)] * (t.ndim - 1) + [(0, _lane_padded(t.shape)[-1] - t.shape[-1])])


def _from_blocks(blocks, axis):
    if axis == 2:
        return jnp.concatenate([blocks[k] for k in range(N_CHIPS)], axis=2)
    return jnp.concatenate([blocks[k] for k in range(N_CHIPS)], axis=1)


def _to_blocks(full, axis):
    width = full.shape[axis] // N_CHIPS
    blocks = [_pad_lanes(lax.slice_in_dim(full, k * width, (k + 1) * width, axis=axis)) for k in range(N_CHIPS)]
    return jnp.stack(blocks, axis=0).reshape(N_CHIPS, -1)


def _pack_weights(shards):
    parts = [_pad_lanes(shards[n]).astype(MXU_DTYPE).reshape(-1) for n, _ in BIG]
    parts += [lax.bitcast_convert_type(shards[n].reshape(-1), jnp.bfloat16).astype(MXU_DTYPE).reshape(-1) for n, _ in TINY_SHARDED]
    return _pad_rows(jnp.concatenate(parts), 32)


def _unpack_weights(gathered, shards):
    flat = gathered.reshape(N_CHIPS, -1)
    out, off = {}, 0
    for n, axis in BIG:
        shape = shards[n].shape
        size = math.prod(_lane_padded(shape))
        blocks = flat[:, off:off + size].reshape((N_CHIPS,) + _lane_padded(shape))[..., :shape[-1]]
        out[n] = _from_blocks(blocks, axis)
        off += size
    for n, axis in TINY_SHARDED:
        size = shards[n].size
        words = flat[:, off:off + 2 * size].astype(jnp.bfloat16).reshape(N_CHIPS, size, 2)
        out[n] = _from_blocks(lax.bitcast_convert_type(words, F32).reshape((N_CHIPS,) + shards[n].shape), axis)
        off += 2 * size
    return out


def _pack_grads(grads):
    flat = jnp.concatenate([_to_blocks(grads[n], axis) for n, axis in BIG], axis=1)
    g = _pad_rows(flat, 2 * 256)
    return g.reshape(N_CHIPS, 2, g.shape[1] // 2, PACK_COLS)


def _unpack_grads(reduced, shards):
    flat = reduced.reshape(-1)
    out, off = {}, 0
    for n, _ in BIG:
        shape = shards[n].shape
        size = math.prod(_lane_padded(shape))
        out[n] = flat[off:off + size].reshape(_lane_padded(shape))[..., :shape[-1]]
        off += size
    return out


def _pack_small(grads, loss):
    parts = [grads[n].reshape(-1) for n in REPLICATED] + [grads[n].reshape(-1) for n, _ in TINY_SHARDED] + [loss[0, :1]]
    flat = jnp.concatenate(parts)
    total = -(-flat.shape[0] // (SUBLANES * LANES)) * SUBLANES * LANES
    return jnp.pad(flat, (0, total - flat.shape[0])).reshape(-1, LANES)


def _unpack_small(summed, grads, shards, chip):
    flat = summed.reshape(-1)
    out, off = {}, 0
    for n in REPLICATED:
        out[n] = flat[off:off + grads[n].size].reshape(grads[n].shape)
        off += grads[n].size
    for n, axis in TINY_SHARDED:
        full = flat[off:off + grads[n].size].reshape(grads[n].shape)
        width = shards[n].shape[axis]
        out[n] = lax.dynamic_slice_in_dim(full, chip * width, width, axis=axis)
        off += grads[n].size
    return out, flat[off]


def kernel(x, mem, ffn1_norm, ffn1_w_in, ffn1_w_out, mix_norm, xa_norm, xa_mem_norm, xa_wq, xa_wkv, xa_wo, ffn2_norm,
           ffn2_w_in, ffn2_w_out, ab_w_in, lru_conv_w, lru_conv_b, lru_w_a, lru_b_a, lru_w_x, lru_b_x, lru_lambda,
           ab_w_out, dn_w_in, dn_conv_w, dn_a_log, dn_dt_bias, dn_o_norm, dn_w_out, final_norm, loss_target,
           m_ffn1_norm, m_ffn1_w_in, m_ffn1_w_out, m_mix_norm, m_xa_norm, m_xa_mem_norm, m_xa_wq, m_xa_wkv, m_xa_wo,
           m_ffn2_norm, m_ffn2_w_in, m_ffn2_w_out, m_ab_w_in, m_lru_conv_w, m_lru_conv_b, m_lru_w_a, m_lru_b_a,
           m_lru_w_x, m_lru_b_x, m_lru_lambda, m_ab_w_out, m_dn_w_in, m_dn_conv_w, m_dn_a_log, m_dn_dt_bias,
           m_dn_o_norm, m_dn_w_out, m_final_norm, v_ffn1_norm, v_ffn1_w_in, v_ffn1_w_out, v_mix_norm, v_xa_norm,
           v_xa_mem_norm, v_xa_wq, v_xa_wkv, v_xa_wo, v_ffn2_norm, v_ffn2_w_in, v_ffn2_w_out, v_ab_w_in,
           v_lru_conv_w, v_lru_conv_b, v_lru_w_a, v_lru_b_a, v_lru_w_x, v_lru_b_x, v_lru_lambda, v_ab_w_out,
           v_dn_w_in, v_dn_conv_w, v_dn_a_log, v_dn_dt_bias, v_dn_o_norm, v_dn_w_out, v_final_norm):
    given = dict(locals())
    shards = {n: given[n] for n in WEIGHTS}
    chip = 2 * lax.axis_index("x") + lax.axis_index("y")

    full = {n: shards[n] for n in REPLICATED}
    full.update(_unpack_weights(_gather_weights(_pack_weights(shards)), shards))
    loss, grad_x, grads = _local_step(x[0], mem[0], loss_target[0], full)

    small, loss_sum = _unpack_small(_allreduce_small(_pack_small(grads, loss)), grads, shards, chip)
    grad = {**small, **_unpack_grads(_reduce_grads(_pack_grads(grads)), shards)}

    delta, new_m, new_v = {}, {}, {}
    for n in WEIGHTS:
        delta[n], new_m[n], new_v[n] = _adamw(shards[n], grad[n], given["m_" + n], given["v_" + n], name=f"adamw_{n}")
    return (loss_sum, grad_x[None], *[grad[n] for n in WEIGHTS], *[delta[n] for n in WEIGHTS],
            *[new_m[n] for n in WEIGHTS], *[new_v[n] for n in WEIGHTS])
```

```python
import functools
import math

import jax
import jax.numpy as jnp
from jax import lax
from jax.experimental import pallas as pl
from jax.experimental.pallas import tpu as pltpu

F32 = jnp.float32
MXU_DTYPE = jnp.bfloat16
VMEM_LIMIT_BYTES = 48 * 1024 * 1024
MM_BLOCK_BYTES = 8 * 1024 * 1024
LANES = 128
SUBLANES = 8

NORM_EPS = 1e-6
CONV_K = 4
ATTN_PAIRS = 4
ATTN_HEAD_DIM = 64
ATTN_WIDTH = 512
ATTN_BLOCK = 128
DILATIONS = (1, 4, 16)
LRU_WIDTH = 512
LRU_BLOCKS = 8
LRU_C = 8.0
DN_HEADS = 8
DN_HEAD_DIM = 128
DN_WIDTH = 1024
DN_CHUNK = 64
XA_HEADS = 4
XA_HEAD_DIM = 256
D_FF = 2816
ADAM_LR, ADAM_B1, ADAM_B2, ADAM_EPS, ADAM_WD, ADAM_STEP = 0.001, 0.9, 0.999, 1e-08, 0.01, 10

MESH = pl.DeviceIdType.MESH


def _tile(n, prefs):
    for p in prefs:
        if n % p == 0:
            return p
    return n


def _params(*sem):
    return pltpu.CompilerParams(dimension_semantics=sem, vmem_limit_bytes=VMEM_LIMIT_BYTES)


def _dg(a, b, dims, hi=False):
    if hi:
        return lax.dot_general(a, b, (dims, ((), ())), precision=lax.Precision.HIGHEST, preferred_element_type=F32)
    return lax.dot_general(a.astype(MXU_DTYPE), b.astype(MXU_DTYPE), (dims, ((), ())), preferred_element_type=F32)


def _make_dot(hi):
    @jax.custom_vjp
    def dot(a, b):
        return _dg(a, b, ((1,), (0,)), hi)

    def fwd(a, b):
        return dot(a, b), (a, b)

    def bwd(r, g):
        a, b = r
        return _dg(g, b, ((1,), (1,)), hi).astype(a.dtype), _dg(a, g, ((0,), (0,)), hi).astype(b.dtype)

    dot.defvjp(fwd, bwd)

    @jax.custom_vjp
    def dot_nt(a, b):
        return _dg(a, b, ((1,), (1,)), hi)

    def fwd_nt(a, b):
        return dot_nt(a, b), (a, b)

    def bwd_nt(r, g):
        a, b = r
        return _dg(g, b, ((1,), (0,)), hi).astype(a.dtype), _dg(g, a, ((0,), (0,)), hi).astype(b.dtype)

    dot_nt.defvjp(fwd_nt, bwd_nt)

    @jax.custom_vjp
    def dot_tn(a, b):
        return _dg(a, b, ((0,), (0,)), hi)

    def fwd_tn(a, b):
        return dot_tn(a, b), (a, b)

    def bwd_tn(r, g):
        a, b = r
        return _dg(b, g, ((1,), (1,)), hi).astype(a.dtype), _dg(a, g, ((1,), (0,)), hi).astype(b.dtype)

    dot_tn.defvjp(fwd_tn, bwd_tn)
    return dot, dot_nt, dot_tn


_bdot, _bdot_nt, _bdot_tn = _make_dot(False)
_hdot, _hdot_nt, _hdot_tn = _make_dot(True)


def _log1p(t):
    return jnp.where(t < 0.01, t * (1.0 - t * (0.5 - t * (1.0 / 3.0))), jnp.log(1.0 + t))


def _neg_expm1(y):
    series = -y * (1.0 + 0.5 * y * (1.0 + (1.0 / 3.0) * y * (1.0 + 0.25 * y)))
    return jnp.where(y > -0.01, series, 1.0 - jnp.exp(y))


def _softplus(x):
    return jnp.maximum(x, 0.0) + _log1p(jnp.exp(-jnp.abs(x)))


def _sigmoid(x):
    return 1.0 / (1.0 + jnp.exp(-x))


def _silu(x):
    return x * _sigmoid(x)


def _gelu(x):
    return 0.5 * x * (1.0 + jnp.tanh(0.7978845608028654 * (x + 0.044715 * x * x * x)))


def _rows(shape):
    return lax.broadcasted_iota(jnp.int32, shape, 0)


def _cols(shape):
    return lax.broadcasted_iota(jnp.int32, shape, 1)


def _mm(a, b, *, mode="nn", out_dtype=F32, res=None, scale=1.0, name):
    if mode == "nn":
        (m, k), (k2, n) = a.shape, b.shape
    elif mode == "nt":
        (m, k), (n, k2) = a.shape, b.shape
    else:
        (k, m), (k2, n) = a.shape, b.shape
    assert k == k2, (a.shape, b.shape, mode)
    if mode == "tn":
        tm, tn, tk = _tile(m, (1024, 512, 256, 128)), _tile(n, (1024, 512, 256, 128)), _tile(k, (1024, 512, 256))
    else:
        tm, tn = _tile(m, (512, 256, 128)), _tile(n, (1024, 512, 256, 128))
        tk = k if k * tn * 2 <= MM_BLOCK_BYTES else _tile(k, (1024, 512, 256, 128))
    nk = k // tk
    dims = {"nn": ((1,), (0,)), "nt": ((1,), (1,)), "tn": ((0,), (0,))}[mode]

    def body(*refs):
        a_ref, b_ref = refs[:2]
        r_ref = refs[2] if res is not None else None
        o_ref = refs[3 if res is not None else 2]

        def finish(r):
            if scale != 1.0:
                r = r * scale
            if res is not None:
                r = r_ref[...] + r
            o_ref[...] = r.astype(out_dtype)

        if nk == 1:
            finish(_dg(a_ref[...], b_ref[...], dims))
            return
        acc = refs[-1]
        kk = pl.program_id(2)

        @pl.when(kk == 0)
        def _():
            acc[...] = jnp.zeros_like(acc)

        acc[...] += _dg(a_ref[...], b_ref[...], dims)

        @pl.when(kk == nk - 1)
        def _():
            finish(acc[...])

    a_spec = pl.BlockSpec((tk, tm), lambda i, j, kk: (kk, i)) if mode == "tn" else pl.BlockSpec((tm, tk), lambda i, j, kk: (i, kk))
    b_spec = pl.BlockSpec((tn, tk), lambda i, j, kk: (j, kk)) if mode == "nt" else pl.BlockSpec((tk, tn), lambda i, j, kk: (kk, j))
    o_spec = pl.BlockSpec((tm, tn), lambda i, j, kk: (i, j))
    in_specs = [a_spec, b_spec] + ([o_spec] if res is not None else [])
    args = (a, b) + ((res,) if res is not None else ())
    return pl.pallas_call(
        body, name=name, grid=(m // tm, n // tn, nk), in_specs=in_specs, out_specs=o_spec,
        out_shape=jax.ShapeDtypeStruct((m, n), out_dtype), scratch_shapes=[pltpu.VMEM((tm, tn), F32)] if nk > 1 else [],
        compiler_params=_params("parallel", "parallel", "arbitrary"),
    )(*args)


def _rms_fwd(x, g, *, name):
    s, d = x.shape
    tm = _tile(s, (512, 256))

    def body(x_ref, g_ref, o_ref):
        xv = x_ref[...]
        r = lax.rsqrt(jnp.mean(xv * xv, axis=-1, keepdims=True) + NORM_EPS)
        o_ref[...] = (xv * r * g_ref[...]).astype(o_ref.dtype)

    return pl.pallas_call(
        body, name=name, grid=(s // tm,),
        in_specs=[pl.BlockSpec((tm, d), lambda i: (i, 0)), pl.BlockSpec((1, d), lambda i: (0, 0))],
        out_specs=pl.BlockSpec((tm, d), lambda i: (i, 0)), out_shape=jax.ShapeDtypeStruct((s, d), MXU_DTYPE),
        compiler_params=_params("parallel"),
    )(x, g.reshape(1, d))


def _rms_bwd(x, g, dh, dres, *, name):
    s, d = x.shape
    tm = _tile(s, (512, 256))

    def body(x_ref, g_ref, dh_ref, dr_ref, dx_ref, dg_ref):
        xv = x_ref[...]
        r = lax.rsqrt(jnp.mean(xv * xv, axis=-1, keepdims=True) + NORM_EPS)
        xh = xv * r
        dhv = dh_ref[...].astype(F32)
        dxh = dhv * g_ref[...]
        dx = r * (dxh - xh * jnp.mean(dxh * xh, axis=-1, keepdims=True))
        dx_ref[...] = dr_ref[...] + dx

        @pl.when(pl.program_id(0) == 0)
        def _():
            dg_ref[...] = jnp.zeros_like(dg_ref)

        dg_ref[...] += jnp.sum(dhv * xh, axis=0, keepdims=True)

    row = pl.BlockSpec((tm, d), lambda i: (i, 0))
    vec = pl.BlockSpec((1, d), lambda i: (0, 0))
    return pl.pallas_call(
        body, name=name, grid=(s // tm,), in_specs=[row, vec, row, row], out_specs=[row, vec],
        out_shape=[jax.ShapeDtypeStruct((s, d), F32), jax.ShapeDtypeStruct((1, d), F32)],
        compiler_params=_params("arbitrary"),
    )(x, g.reshape(1, d), dh, dres)


FFN_CHUNK = 256
FFN_TM = 256
RESIDENT = pl.Buffered(1)


def _ffn_fwd_call(x, g, w_in, w_out, layer, *, name):
    s, d = x.shape
    f = w_out.shape[1]
    tm = _tile(s, (FFN_TM,))

    def body(x_ref, g_ref, wi_ref, wo_ref, y_ref, u_ref, act_ref):
        xv = x_ref[...]
        r = lax.rsqrt(jnp.mean(xv * xv, axis=-1, keepdims=True) + NORM_EPS)
        h = (xv * r * g_ref[...]).astype(MXU_DTYPE)
        for j in range(f // FFN_CHUNK):
            lo, hi = j * FFN_CHUNK, (j + 1) * FFN_CHUNK
            gate = _dg(h, wi_ref[:, lo:hi], ((1,), (0,))).astype(MXU_DTYPE)
            up = _dg(h, wi_ref[:, f + lo:f + hi], ((1,), (0,))).astype(MXU_DTYPE)
            u_ref[:, lo:hi] = gate
            u_ref[:, f + lo:f + hi] = up
            act_ref[:, lo:hi] = (_silu(gate.astype(F32)) * up.astype(F32)).astype(MXU_DTYPE)
        y_ref[...] = xv + 0.5 * _dg(act_ref[...], wo_ref[...], ((1,), (0,)))

    row = lambda w: pl.BlockSpec((tm, w), lambda i: (i, 0))
    return pl.pallas_call(
        body, name=name, grid=(s // tm,),
        in_specs=[row(d), pl.BlockSpec((1, d), lambda i: (0, 0)),
                  pl.BlockSpec((None,) + w_in.shape[1:], lambda i: (layer, 0, 0), pipeline_mode=RESIDENT),
                  pl.BlockSpec((None,) + w_out.shape[1:], lambda i: (layer, 0, 0), pipeline_mode=RESIDENT)],
        out_specs=[row(d), row(2 * f)],
        out_shape=[jax.ShapeDtypeStruct((s, d), F32), jax.ShapeDtypeStruct((s, 2 * f), MXU_DTYPE)],
        scratch_shapes=[pltpu.VMEM((tm, f), MXU_DTYPE)], compiler_params=_params("parallel"),
    )(x, g.reshape(1, d), w_in, w_out)


def _ffn_bwd_call(x, g, u, dy, w_in, w_out, layer, *, name):
    s, d = x.shape
    f = w_out.shape[1]
    tm = _tile(s, (FFN_TM,))

    def body(x_ref, g_ref, u_ref, dy_ref, wi_ref, wo_ref, du_ref, dx_ref, dg_ref, h_ref):
        dyv = dy_ref[...]
        dyh = (0.5 * dyv).astype(MXU_DTYPE)
        for j in range(f // FFN_CHUNK):
            lo, hi = j * FFN_CHUNK, (j + 1) * FFN_CHUNK
            dact = _dg(dyh, wo_ref[lo:hi, :], ((1,), (1,)))
            gate, up = u_ref[:, lo:hi].astype(F32), u_ref[:, f + lo:f + hi].astype(F32)
            sg = _sigmoid(gate)
            du_ref[:, lo:hi] = (dact * up * sg * (1.0 + gate * (1.0 - sg))).astype(MXU_DTYPE)
            du_ref[:, f + lo:f + hi] = (dact * gate * sg).astype(MXU_DTYPE)
        dh = _dg(du_ref[...], wi_ref[...], ((1,), (1,)))
        xv, gv = x_ref[...], g_ref[...]
        r = lax.rsqrt(jnp.mean(xv * xv, axis=-1, keepdims=True) + NORM_EPS)
        xh = xv * r
        h_ref[...] = (xh * gv).astype(MXU_DTYPE)
        dxh = dh * gv
        dx_ref[...] = dyv + r * (dxh - xh * jnp.mean(dxh * xh, axis=-1, keepdims=True))

        @pl.when(pl.program_id(0) == 0)
        def _():
            dg_ref[...] = jnp.zeros_like(dg_ref)

        dg_ref[...] += jnp.sum(dh * xh, axis=0, keepdims=True)

    row = lambda w: pl.BlockSpec((tm, w), lambda i: (i, 0))
    vec = pl.BlockSpec((1, d), lambda i: (0, 0))
    return pl.pallas_call(
        body, name=name, grid=(s // tm,),
        in_specs=[row(d), vec, row(2 * f), row(d),
                  pl.BlockSpec((None,) + w_in.shape[1:], lambda i: (layer, 0, 0), pipeline_mode=RESIDENT),
                  pl.BlockSpec((None,) + w_out.shape[1:], lambda i: (layer, 0, 0), pipeline_mode=RESIDENT)],
        out_specs=[row(2 * f), row(d), vec, row(d)],
        out_shape=[jax.ShapeDtypeStruct((s, 2 * f), MXU_DTYPE), jax.ShapeDtypeStruct((s, d), F32),
                   jax.ShapeDtypeStruct((1, d), F32), jax.ShapeDtypeStruct((s, d), MXU_DTYPE)],
        compiler_params=_params("arbitrary"),
    )(x, g.reshape(1, d), u, dy, w_in, w_out)


def _ffn_dw_out(u, dy, *, name):
    s, f2 = u.shape
    f, d = f2 // 2, dy.shape[1]
    tf, tk = _tile(f, (1408, 256, 128)), _tile(s, (512, 256))
    nj = f // tf

    def body(g_ref, u_ref, dy_ref, o_ref):
        @pl.when(pl.program_id(1) == 0)
        def _():
            o_ref[...] = jnp.zeros_like(o_ref)

        act = _silu(g_ref[...].astype(F32)) * u_ref[...].astype(F32)
        o_ref[...] += _dg(act, 0.5 * dy_ref[...], ((0,), (0,)))

    return pl.pallas_call(
        body, name=name, grid=(nj, s // tk),
        in_specs=[pl.BlockSpec((tk, tf), lambda j, k: (k, j)), pl.BlockSpec((tk, tf), lambda j, k: (k, j + nj)),
                  pl.BlockSpec((tk, d), lambda j, k: (k, 0))],
        out_specs=pl.BlockSpec((tf, d), lambda j, k: (j, 0)), out_shape=jax.ShapeDtypeStruct((f, d), F32),
        compiler_params=_params("parallel", "arbitrary"),
    )(u, u, dy)


def _ffn_fwd(x, g, w_in, w_out, layer, tag):
    y, u = _ffn_fwd_call(x, g[layer], w_in, w_out, layer, name=f"{tag}_fwd")
    return y, (x, u)


def _ffn_bwd(saved, g, w_in, w_out, layer, dy, tag):
    x, u = saved
    du, dx, dg, h = _ffn_bwd_call(x, g[layer], u, dy, w_in, w_out, layer, name=f"{tag}_bwd")
    dw_out = _ffn_dw_out(u, dy, name=f"{tag}_dwout")
    dw_in = _mm(h, du, mode="tn", name=f"{tag}_dwin")
    return dx, dg[0], dw_in, dw_out


ATTN_SCALE = ATTN_HEAD_DIM ** -0.5
NEG_BIG = -1e30
PROJ_AB_BLOCKS = 5


def _head_masks():
    lane = _cols((ATTN_BLOCK, LANES))
    return lane < ATTN_HEAD_DIM, lane >= ATTN_HEAD_DIM


def _per_head(t):
    shape = (2,) + t.shape
    keep = (lax.broadcasted_iota(jnp.int32, shape, 2) >= ATTN_HEAD_DIM) == (lax.broadcasted_iota(jnp.int32, shape, 0) == 1)
    return jnp.where(keep, t[None], 0.0)


def _both(t):
    return jnp.broadcast_to(t[None], (2,) + t.shape)


def _head_cols(t):
    return jnp.stack([t[:, 0:1], t[:, ATTN_HEAD_DIM:ATTN_HEAD_DIM + 1]], axis=0)


def _band_masks(has_prev):
    qi, kj = _rows((ATTN_BLOCK, ATTN_BLOCK)), _cols((ATTN_BLOCK, ATTN_BLOCK))
    return (kj >= qi) & has_prev, kj <= qi


def _dattn_delta(o, dcat, *, name):
    s_len = o.shape[0]
    tm = _tile(s_len, (512, 256))

    def body(o_ref, do_ref, out_ref):
        r, c = _rows((ATTN_WIDTH, ATTN_WIDTH)), _cols((ATTN_WIDTH, ATTN_WIDTH))
        ones_bd = (r // ATTN_HEAD_DIM == c // ATTN_HEAD_DIM).astype(F32)
        out_ref[...] = _dg(o_ref[...] * do_ref[...], ones_bd, ((1,), (0,)), hi=True)

    blk = pl.BlockSpec((tm, ATTN_WIDTH), lambda i: (i, 0))
    return pl.pallas_call(
        body, name=name, grid=(s_len // tm,), in_specs=[blk, blk], out_specs=blk,
        out_shape=jax.ShapeDtypeStruct((s_len, ATTN_WIDTH), F32), compiler_params=_params("parallel"),
    )(o, dcat)


def _res_rows(r, d):
    return pl.ds(r, ATTN_BLOCK, stride=d) if d > 1 else pl.ds(0, ATTN_BLOCK)


def _pair_specs(d, n_of):
    return lambda c: pl.BlockSpec((ATTN_BLOCK * d, LANES), lambda n, p: (n_of(n), c * ATTN_PAIRS + p))


def _sattn_fwd(proj, state, d, *, last, name):
    s_len = proj.shape[0]
    nb = s_len // (ATTN_BLOCK * d)
    first = state is None
    n_out = 2 if last else 3

    def body(*refs):
        q_ref, kp_ref, kc_ref, vp_ref, vc_ref = refs[:5]
        st_refs = () if first else refs[5:8]
        out_refs = refs[-n_out:]
        ok = jnp.concatenate(_band_masks(pl.program_id(0) > 0), axis=1)
        hm_a = _head_masks()[0]

        def residue(r, carry):
            rows = _res_rows(r, d)
            q = q_ref[rows, :]
            kcat = jnp.concatenate([kp_ref[rows, :], kc_ref[rows, :]], axis=0)
            vcat = jnp.concatenate([vp_ref[rows, :], vc_ref[rows, :]], axis=0)
            s = jnp.where(ok, _bdg(_per_head(q), _both(kcat), 2, 2) * ATTN_SCALE, NEG_BIG)
            m_new = jnp.max(s, axis=2, keepdims=True)
            if not first:
                m_in, l_in, a_in = (t[rows, :] for t in st_refs)
                m_old = _head_cols(m_in)
                m_new = jnp.maximum(m_old, m_new)
                alpha = jnp.exp(m_old - m_new)
            p = jnp.exp(s - m_new)
            l_new = jnp.sum(p, axis=2, keepdims=True)
            pv = _bdg(p, _per_head(vcat), 2, 1)
            acc = pv[0] + pv[1]
            if not first:
                l_new = l_new + _head_cols(l_in) * alpha
                acc = acc + a_in * jnp.where(hm_a, alpha[0], alpha[1])
            m_pair = jnp.where(hm_a, m_new[0], m_new[1])
            l_pair = jnp.where(hm_a, l_new[0], l_new[1])
            if last:
                out_refs[0][rows, :] = acc / l_pair
                out_refs[1][rows, :] = m_pair + jnp.log(l_pair)
            else:
                out_refs[0][rows, :] = m_pair
                out_refs[1][rows, :] = l_pair
                out_refs[2][rows, :] = acc
            return carry

        lax.fori_loop(0, d, residue, 0)

    cur, prev = _pair_specs(d, lambda n: n), _pair_specs(d, lambda n: jnp.maximum(n - 1, 0))
    st = cur(0)
    return tuple(pl.pallas_call(
        body, name=name, grid=(nb, ATTN_PAIRS),
        in_specs=[cur(0), prev(1), cur(1), prev(2), cur(2)] + ([] if first else [st] * 3),
        out_specs=[st] * n_out, out_shape=[jax.ShapeDtypeStruct((s_len, ATTN_WIDTH), F32)] * n_out,
        compiler_params=_params("arbitrary", "parallel"),
    )(*([proj] * 5 + ([] if first else list(state)))))


def _sattn_dq(proj, dcat, lse, delta, dq_in, d, *, name):
    s_len = proj.shape[0]
    nb = s_len // (ATTN_BLOCK * d)
    first = dq_in is None

    def body(*refs):
        q_ref, kp_ref, kc_ref, vp_ref, vc_ref, do_ref, lse_ref, dl_ref = refs[:8]
        dq_ref = refs[-1]
        ok = jnp.concatenate(_band_masks(pl.program_id(0) > 0), axis=1)

        def residue(r, carry):
            rows = _res_rows(r, d)
            kcat = jnp.concatenate([kp_ref[rows, :], kc_ref[rows, :]], axis=0)
            vcat = jnp.concatenate([vp_ref[rows, :], vc_ref[rows, :]], axis=0)
            s = _bdg(_per_head(q_ref[rows, :]), _both(kcat), 2, 2) * ATTN_SCALE
            pr = jnp.where(ok, jnp.exp(jnp.where(ok, s, NEG_BIG) - _head_cols(lse_ref[rows, :])), 0.0)
            ds = pr * (_bdg(_per_head(do_ref[rows, :]), _both(vcat), 2, 2) - _head_cols(dl_ref[rows, :]))
            dq = _bdg(ds, _per_head(kcat), 2, 1)
            acc = (dq[0] + dq[1]) * ATTN_SCALE
            dq_ref[rows, :] = acc if first else acc + refs[8][rows, :]
            return carry

        lax.fori_loop(0, d, residue, 0)

    cur, prev = _pair_specs(d, lambda n: n), _pair_specs(d, lambda n: jnp.maximum(n - 1, 0))
    st = cur(0)
    return pl.pallas_call(
        body, name=name, grid=(nb, ATTN_PAIRS),
        in_specs=[cur(0), prev(1), cur(1), prev(2), cur(2), st, st, st] + ([] if first else [st]),
        out_specs=st, out_shape=jax.ShapeDtypeStruct((s_len, ATTN_WIDTH), F32),
        compiler_params=_params("arbitrary", "parallel"),
    )(*([proj] * 5 + [dcat, lse, delta] + ([] if first else [dq_in])))


def _sattn_dkv(proj, dcat, lse, delta, dkv_in, d, *, name):
    s_len = proj.shape[0]
    nb = s_len // (ATTN_BLOCK * d)
    first = dkv_in is None

    def body(*refs):
        k_ref, v_ref = refs[:2]
        qs = (refs[2:6], refs[6:10])
        dk_ref, dv_ref = refs[-2:]
        nxt_ok, cur_ok = _band_masks(pl.program_id(0) < nb - 1)
        ok = jnp.concatenate([cur_ok, nxt_ok], axis=0)

        def residue(r, carry):
            rows = _res_rows(r, d)
            q, do, lse_v, dl_v = (jnp.concatenate([a[rows, :], b[rows, :]], axis=0) for a, b in zip(*qs))
            q2, do2 = _per_head(q), _per_head(do)
            s = _bdg(q2, _both(k_ref[rows, :]), 2, 2) * ATTN_SCALE
            pr = jnp.where(ok, jnp.exp(jnp.where(ok, s, NEG_BIG) - _head_cols(lse_v)), 0.0)
            dv = _bdg(pr, do2, 1, 1)
            ds = pr * (_bdg(do2, _both(v_ref[rows, :]), 2, 2) - _head_cols(dl_v))
            dk = _bdg(ds, q2, 1, 1)
            dk, dv = (dk[0] + dk[1]) * ATTN_SCALE, dv[0] + dv[1]
            dk_ref[rows, :] = dk if first else dk + refs[10][rows, :]
            dv_ref[rows, :] = dv if first else dv + refs[11][rows, :]
            return carry

        lax.fori_loop(0, d, residue, 0)

    cur, nxt = _pair_specs(d, lambda n: n), _pair_specs(d, lambda n: jnp.minimum(n + 1, nb - 1))
    st, st_n = cur(0), nxt(0)
    in_specs = [cur(1), cur(2), cur(0), st, st, st, nxt(0), st_n, st_n, st_n] + ([] if first else [st, st])
    args = [proj] * 2 + [proj, dcat, lse, delta] * 2 + ([] if first else list(dkv_in))
    return tuple(pl.pallas_call(
        body, name=name, grid=(nb, ATTN_PAIRS), in_specs=in_specs, out_specs=[st, st],
        out_shape=[jax.ShapeDtypeStruct((s_len, ATTN_WIDTH), F32)] * 2,
        compiler_params=_params("arbitrary", "parallel"),
    )(*args))


def _dattn_forward(proj, tag):
    state = None
    for i, d in enumerate(DILATIONS):
        state = _sattn_fwd(proj, state, d, last=i == len(DILATIONS) - 1, name=f"{tag}_attn_d{d}")
    return state


def _dattn_backward(proj, o, lse, dcat, tag):
    delta = _dattn_delta(o, dcat, name=f"{tag}_attn_delta")
    dq, dkv = None, None
    for d in DILATIONS:
        dq = _sattn_dq(proj, dcat, lse, delta, dq, d, name=f"{tag}_attn_dq_d{d}")
        dkv = _sattn_dkv(proj, dcat, lse, delta, dkv, d, name=f"{tag}_attn_dkv_d{d}")
    return dq, dkv[0], dkv[1]


CONV_TC = 512
CONV_T = 256


def _shift_down(ext, k, t):
    return (pltpu.roll(ext, k, 0) if k else ext)[SUBLANES:SUBLANES + t]


def _conv_fwd(src, cb0, width, w8, *, name):
    s_len = src.shape[0]
    t, tc = _tile(s_len, (CONV_T,)), CONV_TC
    tpb = t // SUBLANES

    def body(x_ref, h_ref, w_ref, y_ref):
        halo = jnp.where(pl.program_id(0) > 0, h_ref[...], 0.0)
        ext = jnp.concatenate([halo, x_ref[...]], axis=0)
        w = w_ref[...]
        y = jnp.broadcast_to(w[CONV_K:CONV_K + 1], (t, tc))
        for k in range(CONV_K):
            y = y + w[k:k + 1] * _shift_down(ext, CONV_K - 1 - k, t)
        y_ref[...] = y

    return pl.pallas_call(
        body, name=name, grid=(s_len // t, width // tc),
        in_specs=[pl.BlockSpec((t, tc), lambda i, j: (i, cb0 + j)),
                  pl.BlockSpec((SUBLANES, tc), lambda i, j: (jnp.maximum(i * tpb - 1, 0), cb0 + j)),
                  pl.BlockSpec((SUBLANES, tc), lambda i, j: (0, j))],
        out_specs=pl.BlockSpec((t, tc), lambda i, j: (i, j)), out_shape=jax.ShapeDtypeStruct((s_len, width), F32),
        compiler_params=_params("parallel", "parallel"),
    )(src, src, w8)


def _conv_bwd(src, cb0, width, w8, dy, *, name):
    s_len = src.shape[0]
    t, tc = _tile(s_len, (CONV_T,)), CONV_TC
    tpb = t // SUBLANES
    ni = s_len // t

    def body(x_ref, h_ref, w_ref, dy_ref, dn_ref, dx_ref, dw_ref):
        i = pl.program_id(1)
        halo = jnp.where(i > 0, h_ref[...], 0.0)
        ext = jnp.concatenate([halo, x_ref[...]], axis=0)
        dyv = dy_ref[...]
        extn = jnp.concatenate([dyv, jnp.where(i < ni - 1, dn_ref[...], 0.0)], axis=0)
        w = w_ref[...]
        row = _rows((SUBLANES, tc))
        dx = jnp.zeros((t, tc), F32)
        dw = jnp.where(row == CONV_K, jnp.sum(dyv, axis=0, keepdims=True), 0.0)
        for k in range(CONV_K):
            up = CONV_K - 1 - k
            dx = dx + w[k:k + 1] * (pltpu.roll(extn, t + SUBLANES - up, 0) if up else extn)[:t]
            dw = dw + jnp.where(row == k, jnp.sum(dyv * _shift_down(ext, up, t), axis=0, keepdims=True), 0.0)
        dx_ref[...] = dx.astype(dx_ref.dtype)

        @pl.when(i == 0)
        def _():
            dw_ref[...] = jnp.zeros_like(dw_ref)

        dw_ref[...] += dw

    return pl.pallas_call(
        body, name=name, grid=(width // tc, ni),
        in_specs=[pl.BlockSpec((t, tc), lambda j, i: (i, cb0 + j)),
                  pl.BlockSpec((SUBLANES, tc), lambda j, i: (jnp.maximum(i * tpb - 1, 0), cb0 + j)),
                  pl.BlockSpec((SUBLANES, tc), lambda j, i: (0, j)),
                  pl.BlockSpec((t, tc), lambda j, i: (i, j)),
                  pl.BlockSpec((SUBLANES, tc), lambda j, i: (jnp.minimum((i + 1) * tpb, s_len // SUBLANES - 1), j))],
        out_specs=[pl.BlockSpec((t, tc), lambda j, i: (i, j)), pl.BlockSpec((SUBLANES, tc), lambda j, i: (0, j))],
        out_shape=[jax.ShapeDtypeStruct((s_len, width), MXU_DTYPE), jax.ShapeDtypeStruct((SUBLANES, width), F32)],
        compiler_params=_params("parallel", "arbitrary"),
    )(src, src, w8, dy, dy)


LRU_T = 256


def _lru_gates(xc, wa, wx, ba, bx, lam):
    r = _sigmoid(_bdot(xc, wa) + ba)
    i = _sigmoid(_bdot(xc, wx) + bx)
    log_a = (-LRU_C) * r * _softplus(-lam)
    return jnp.exp(log_a), jnp.sqrt(_neg_expm1(2.0 * log_a)) * i * xc


def _block_scan(a, b, reverse):
    t = a.shape[0]
    row = _rows(a.shape)
    s = 1
    while s < t:
        shift, ok = (t - s, row < t - s) if reverse else (s, row >= s)
        b = jnp.where(ok, a * pltpu.roll(b, shift, 0) + b, b)
        a = jnp.where(ok, a * pltpu.roll(a, shift, 0), a)
        s *= 2
    return a, b


def _lru_fwd(xc, proj, wa, wx, ba, bx, lam, *, name):
    s_len, w = xc.shape
    t = _tile(s_len, (LRU_T,))

    def body(xc_ref, gr_ref, wa_ref, wx_ref, ba_ref, bx_ref, lam_ref, h_ref, y_ref, carry):
        @pl.when(pl.program_id(0) == 0)
        def _():
            carry[...] = jnp.zeros_like(carry)

        a, b = _lru_gates(xc_ref[...], wa_ref[...], wx_ref[...], ba_ref[...], bx_ref[...], lam_ref[...])
        a_cum, h0 = _block_scan(a, b, False)
        h = h0 + a_cum * carry[0:1, :]
        h_ref[...] = h
        y_ref[...] = (h * _gelu(gr_ref[...])).astype(y_ref.dtype)
        carry[0:1, :] = h[t - 1:t, :]

    row = pl.BlockSpec((t, w), lambda i: (i, 0))
    mat = pl.BlockSpec((w, w), lambda i: (0, 0))
    vec = pl.BlockSpec((1, w), lambda i: (0, 0))
    return pl.pallas_call(
        body, name=name, grid=(s_len // t,),
        in_specs=[row, pl.BlockSpec((t, w), lambda i: (i, PROJ_AB_BLOCKS - 1)), mat, mat, vec, vec, vec],
        out_specs=[row, row], out_shape=[jax.ShapeDtypeStruct((s_len, w), F32), jax.ShapeDtypeStruct((s_len, w), MXU_DTYPE)],
        scratch_shapes=[pltpu.VMEM((SUBLANES, w), F32)], compiler_params=_params("arbitrary"),
    )(xc, proj, wa, wx, ba, bx, lam)


def _lru_bwd(xc, proj, hs, dcat, wa, wx, ba, bx, lam, *, name):
    s_len, w = xc.shape
    t = _tile(s_len, (LRU_T,))
    nb = s_len // t
    tpb = t // SUBLANES

    def body(xc_ref, gr_ref, h_ref, hp_ref, dy_ref, wa_ref, wx_ref, ba_ref, bx_ref, lam_ref,
             dxc_ref, dgr_ref, dwa_ref, dwx_ref, dba_ref, dbx_ref, dlam_ref, carry):
        step = pl.program_id(0)
        params = (wa_ref[...], wx_ref[...], ba_ref[...], bx_ref[...], lam_ref[...])

        @pl.when(step == 0)
        def _():
            carry[...] = jnp.zeros_like(carry)
            for r in (dwa_ref, dwx_ref, dba_ref, dbx_ref, dlam_ref):
                r[...] = jnp.zeros_like(r)

        (a, _), vjp = jax.vjp(_lru_gates, xc_ref[...], *params)
        gr, h, dy = gr_ref[...], h_ref[...], dy_ref[...]
        gel, gel_vjp = jax.vjp(_gelu, gr)
        dgr_ref[...] = gel_vjp(dy * h)[0].astype(dgr_ref.dtype)
        dh = dy * gel
        a_cum, g0 = _block_scan(a, a * dh, True)
        big_g = g0 + a_cum * carry[0:1, :]
        row = _rows((t, w))
        g = dh + jnp.where(row == t - 1, carry[0:1, :], pltpu.roll(big_g, t - 1, 0))
        carry[0:1, :] = big_g[0:1, :]
        h_last = jnp.where(step < nb - 1, hp_ref[SUBLANES - 1:SUBLANES, :], 0.0)
        h_prev = jnp.where(row == 0, h_last, pltpu.roll(h, 1, 0))
        dxc, dwa, dwx, dba, dbx, dlam = vjp((g * h_prev, g))
        dxc_ref[...] = dxc
        dwa_ref[...] += dwa
        dwx_ref[...] += dwx
        dba_ref[...] += dba
        dbx_ref[...] += dbx
        dlam_ref[...] += dlam

    rev = lambda i: nb - 1 - i
    row = pl.BlockSpec((t, w), lambda i: (rev(i), 0))
    mat = pl.BlockSpec((w, w), lambda i: (0, 0))
    vec = pl.BlockSpec((1, w), lambda i: (0, 0))
    return pl.pallas_call(
        body, name=name, grid=(nb,),
        in_specs=[row, pl.BlockSpec((t, w), lambda i: (rev(i), PROJ_AB_BLOCKS - 1)), row,
                  pl.BlockSpec((SUBLANES, w), lambda i: (jnp.maximum(rev(i) * tpb - 1, 0), 0)),
                  pl.BlockSpec((t, w), lambda i: (rev(i), 1)), mat, mat, vec, vec, vec],
        out_specs=[row, row, mat, mat, vec, vec, vec],
        out_shape=[jax.ShapeDtypeStruct((s_len, w), F32), jax.ShapeDtypeStruct((s_len, w), MXU_DTYPE)]
        + [jax.ShapeDtypeStruct((w, w), F32)] * 2 + [jax.ShapeDtypeStruct((1, w), F32)] * 3,
        scratch_shapes=[pltpu.VMEM((SUBLANES, w), F32)], compiler_params=_params("arbitrary"),
    )(xc, proj, hs, hs, dcat, wa, wx, ba, bx, lam)


XA_T = 256
XA_SCALE = XA_HEAD_DIM ** -0.5


def _xa_head(q, k, v):
    s = _bdot_nt(q, k) * XA_SCALE
    e = jnp.exp(s - jnp.max(s, axis=-1, keepdims=True))
    return _bdot(e / jnp.sum(e, axis=-1, keepdims=True), v)


def _xa_fwd(q, kv, *, name):
    s_len, d = q.shape
    n_mem = kv.shape[0]
    t = _tile(s_len, (XA_T,))

    def body(q_ref, k_ref, v_ref, o_ref):
        for h in range(XA_HEADS):
            sl = slice(h * XA_HEAD_DIM, (h + 1) * XA_HEAD_DIM)
            o_ref[:, sl] = _xa_head(q_ref[:, sl].astype(F32), k_ref[:, sl], v_ref[:, sl]).astype(o_ref.dtype)

    return pl.pallas_call(
        body, name=name, grid=(s_len // t,),
        in_specs=[pl.BlockSpec((t, d), lambda i: (i, 0)), pl.BlockSpec((n_mem, d), lambda i: (0, 0)),
                  pl.BlockSpec((n_mem, d), lambda i: (0, 1))],
        out_specs=pl.BlockSpec((t, d), lambda i: (i, 0)), out_shape=jax.ShapeDtypeStruct((s_len, d), MXU_DTYPE),
        compiler_params=_params("parallel"),
    )(q, kv, kv)


def _xa_bwd(q, kv, do, *, name):
    s_len, d = q.shape
    n_mem = kv.shape[0]
    t = _tile(s_len, (XA_T,))

    def body(q_ref, k_ref, v_ref, do_ref, dq_ref, dk_ref, dv_ref):
        @pl.when(pl.program_id(0) == 0)
        def _():
            dk_ref[...] = jnp.zeros_like(dk_ref)
            dv_ref[...] = jnp.zeros_like(dv_ref)

        for h in range(XA_HEADS):
            sl = slice(h * XA_HEAD_DIM, (h + 1) * XA_HEAD_DIM)
            _, vjp = jax.vjp(_xa_head, q_ref[:, sl].astype(F32), k_ref[:, sl], v_ref[:, sl])
            dq, dk, dv = vjp(do_ref[:, sl].astype(F32))
            dq_ref[:, sl] = dq.astype(dq_ref.dtype)
            dk_ref[:, sl] += dk
            dv_ref[:, sl] += dv

    row = pl.BlockSpec((t, d), lambda i: (i, 0))
    dq, dk, dv = pl.pallas_call(
        body, name=name, grid=(s_len // t,),
        in_specs=[row, pl.BlockSpec((n_mem, d), lambda i: (0, 0)), pl.BlockSpec((n_mem, d), lambda i: (0, 1)), row],
        out_specs=[row, pl.BlockSpec((n_mem, d), lambda i: (0, 0)), pl.BlockSpec((n_mem, d), lambda i: (0, 0))],
        out_shape=[jax.ShapeDtypeStruct((s_len, d), MXU_DTYPE)] + [jax.ShapeDtypeStruct((n_mem, d), F32)] * 2,
        compiler_params=_params("arbitrary"),
    )(q, kv, kv, do)
    return dq, jnp.concatenate([dk, dv], axis=1)


DN_Q_SCALE = DN_HEAD_DIM ** -0.5
L2_EPS = 1e-6


def _bdg(a, b, ca, cb):
    return lax.dot_general(a.astype(MXU_DTYPE), b.astype(MXU_DTYPE), (((ca,), (cb,)), ((0,), (0,))), preferred_element_type=F32)


@jax.custom_vjp
def _bmm(a, b):
    return _bdg(a, b, 2, 1)


_bmm.defvjp(lambda a, b: (_bdg(a, b, 2, 1), (a, b)), lambda r, g: (_bdg(g, r[1], 2, 2), _bdg(r[0], g, 1, 1)))


@jax.custom_vjp
def _bmm_nt(a, b):
    return _bdg(a, b, 2, 2)


_bmm_nt.defvjp(lambda a, b: (_bdg(a, b, 2, 2), (a, b)), lambda r, g: (_bdg(g, r[1], 2, 1), _bdg(g, r[0], 1, 1)))


@jax.custom_vjp
def _bmm_tn(a, b):
    return _bdg(a, b, 1, 1)


_bmm_tn.defvjp(lambda a, b: (_bdg(a, b, 1, 1), (a, b)), lambda r, g: (_bdg(r[1], g, 2, 2), _bdg(r[0], g, 2, 1)))


def _tri_inverse(n):
    eye = (lax.broadcasted_iota(jnp.int32, n.shape, 1) == lax.broadcasted_iota(jnp.int32, n.shape, 2)).astype(F32)
    inv, p = eye - n, n
    for _ in range(5):
        p = _bdg(p, p, 2, 1)
        inv = _bdg(inv, eye + p, 2, 1)
    return inv


@jax.custom_vjp
def _tri_solve2(n, r1, r2):
    t = _tri_inverse(n)
    return _bdg(t, r1, 2, 1), _bdg(t, r2, 2, 1)


def _tri_solve2_fwd(n, r1, r2):
    t = _tri_inverse(n)
    x1, x2 = _bdg(t, r1, 2, 1), _bdg(t, r2, 2, 1)
    return (x1, x2), (t, x1, x2)


def _tri_solve2_bwd(saved, cts):
    t, x1, x2 = saved
    d1, d2 = _bdg(t, cts[0], 1, 1), _bdg(t, cts[1], 1, 1)
    return -(_bdg(d1, x1, 2, 2) + _bdg(d2, x2, 2, 2)), d1, d2


_tri_solve2.defvjp(_tri_solve2_fwd, _tri_solve2_bwd)


def _dn_gates(ab, alog, dtb):
    return -jnp.exp(alog) * _softplus(ab + dtb), _sigmoid(ab)


def _dn_heads(cq, ck, cv, z, g, beta, onorm, state):
    h, c, _ = cq.shape
    l2 = lambda t: t * lax.rsqrt(jnp.sum(t * t, axis=-1, keepdims=True) + L2_EPS)
    q, k, v = l2(_silu(cq)) * DN_Q_SCALE, l2(_silu(ck)), _silu(cv)
    r, cc = lax.broadcasted_iota(jnp.int32, (h, c, c), 1), lax.broadcasted_iota(jnp.int32, (h, c, c), 2)
    tri, eye = r >= cc, r == cc
    g_row = jnp.sum(jnp.where(eye, g, 0.0), axis=1, keepdims=True)
    gcum_c = jnp.sum(jnp.where(tri, g_row, 0.0), axis=2, keepdims=True)
    gcum_r = jnp.sum(jnp.where(cc >= r, g, 0.0), axis=1, keepdims=True)
    decay = jnp.where(tri, jnp.exp(jnp.where(tri, gcum_c - gcum_r, 0.0)), 0.0)
    kb = k * beta
    n = jnp.where(r > cc, _bmm_nt(kb, k) * decay, 0.0)
    u, w = _tri_solve2(n, v * beta, kb * jnp.exp(gcum_c))
    v_new = u - _bmm(w, state)
    o = _bmm(q * jnp.exp(gcum_c), state) + _bmm(_bmm_nt(q, k) * decay, v_new)
    g_last = jnp.sum(g, axis=1, keepdims=True)
    new_state = state * jnp.exp(g_last) + _bmm_tn(k * jnp.exp(g_last - gcum_c), v_new)
    on = o * lax.rsqrt(jnp.mean(o * o, axis=-1, keepdims=True) + NORM_EPS) * onorm
    return on * _silu(z), new_state


def _dn_stack(ref, col0):
    return jnp.stack([ref[:, col0 + h * DN_HEAD_DIM:col0 + (h + 1) * DN_HEAD_DIM].astype(F32) for h in range(DN_HEADS)], axis=0)


def _dn_cols(block, col0):
    return jnp.stack([block[:, col0 + h:col0 + h + 1] for h in range(DN_HEADS)], axis=0)


def _dn_fwd(cqkv, proj, ab, alog, dtb, onorm, *, name):
    s_len = cqkv.shape[0]
    c, hd, w = DN_CHUNK, DN_HEAD_DIM, DN_WIDTH
    n_chunks = s_len // c

    def body(c_ref, z_ref, ab_ref, alog_ref, dtb_ref, on_ref, o_ref, st_ref, state):
        @pl.when(pl.program_id(0) == 0)
        def _():
            state[...] = jnp.zeros_like(state)

        g_all, beta_all = _dn_gates(ab_ref[...], alog_ref[...], dtb_ref[...])
        st = state[...]
        st_ref[0] = st
        out, new = _dn_heads(_dn_stack(c_ref, 0), _dn_stack(c_ref, w), _dn_stack(c_ref, 2 * w), _dn_stack(z_ref, 0),
                             _dn_cols(g_all, 0), _dn_cols(beta_all, DN_HEADS), on_ref[...], st)
        state[...] = new
        for h in range(DN_HEADS):
            o_ref[:, h * hd:(h + 1) * hd] = out[h].astype(o_ref.dtype)

    vec = pl.BlockSpec((1, LANES), lambda i: (0, 0))
    return pl.pallas_call(
        body, name=name, grid=(n_chunks,),
        in_specs=[pl.BlockSpec((c, 3 * w), lambda i: (i, 0)), pl.BlockSpec((c, w), lambda i: (i, 3)),
                  pl.BlockSpec((c, LANES), lambda i: (i, 0)), vec, vec, vec],
        out_specs=[pl.BlockSpec((c, w), lambda i: (i, 0)), pl.BlockSpec((1, DN_HEADS, hd, hd), lambda i: (i, 0, 0, 0))],
        out_shape=[jax.ShapeDtypeStruct((s_len, w), MXU_DTYPE), jax.ShapeDtypeStruct((n_chunks, DN_HEADS, hd, hd), F32)],
        scratch_shapes=[pltpu.VMEM((DN_HEADS, hd, hd), F32)], compiler_params=_params("arbitrary"),
    )(cqkv, proj, ab, alog, dtb, onorm)


def _dn_bwd(cqkv, proj, ab, alog, dtb, onorm, states, dout, *, name):
    s_len = cqkv.shape[0]
    c, hd, w = DN_CHUNK, DN_HEAD_DIM, DN_WIDTH
    n_chunks = s_len // c

    def body(c_ref, z_ref, ab_ref, alog_ref, dtb_ref, on_ref, st_ref, do_ref,
             dc_ref, dz_ref, dab_ref, dalog_ref, ddtb_ref, don_ref, dstate):
        @pl.when(pl.program_id(0) == 0)
        def _():
            dstate[...] = jnp.zeros_like(dstate)
            for r in (dalog_ref, ddtb_ref, don_ref):
                r[...] = jnp.zeros_like(r)

        (g_all, beta_all), gates_vjp = jax.vjp(_dn_gates, ab_ref[...], alog_ref[...], dtb_ref[...])
        _, vjp = jax.vjp(_dn_heads, _dn_stack(c_ref, 0), _dn_stack(c_ref, w), _dn_stack(c_ref, 2 * w), _dn_stack(z_ref, 0),
                         _dn_cols(g_all, 0), _dn_cols(beta_all, DN_HEADS), on_ref[...], st_ref[0])
        dcq, dck, dcv, dz, dg, dbeta, don, dst = vjp((_dn_stack(do_ref, 0), dstate[...]))
        dstate[...] = dst
        col = _cols((c, LANES))
        dg_all, dbeta_all = jnp.zeros((c, LANES), F32), jnp.zeros((c, LANES), F32)
        for h in range(DN_HEADS):
            sl = slice(h * hd, (h + 1) * hd)
            dc_ref[:, sl] = dcq[h]
            dc_ref[:, w + h * hd:w + (h + 1) * hd] = dck[h]
            dc_ref[:, 2 * w + h * hd:2 * w + (h + 1) * hd] = dcv[h]
            dz_ref[:, sl] = dz[h].astype(dz_ref.dtype)
            dg_all = dg_all + jnp.where(col == h, dg[h], 0.0)
            dbeta_all = dbeta_all + jnp.where(col == DN_HEADS + h, dbeta[h], 0.0)
        dab, dalog, ddtb = gates_vjp((dg_all, dbeta_all))
        dab_ref[...] = dab
        dalog_ref[...] += dalog
        ddtb_ref[...] += ddtb
        don_ref[...] += don

    rev = lambda i: n_chunks - 1 - i
    vec = pl.BlockSpec((1, LANES), lambda i: (0, 0))
    return pl.pallas_call(
        body, name=name, grid=(n_chunks,),
        in_specs=[pl.BlockSpec((c, 3 * w), lambda i: (rev(i), 0)), pl.BlockSpec((c, w), lambda i: (rev(i), 3)),
                  pl.BlockSpec((c, LANES), lambda i: (rev(i), 0)), vec, vec, vec,
                  pl.BlockSpec((1, DN_HEADS, hd, hd), lambda i: (rev(i), 0, 0, 0)), pl.BlockSpec((c, w), lambda i: (rev(i), 0))],
        out_specs=[pl.BlockSpec((c, 3 * w), lambda i: (rev(i), 0)), pl.BlockSpec((c, w), lambda i: (rev(i), 0)),
                   pl.BlockSpec((c, LANES), lambda i: (rev(i), 0)), vec, vec, vec],
        out_shape=[jax.ShapeDtypeStruct((s_len, 3 * w), F32), jax.ShapeDtypeStruct((s_len, w), MXU_DTYPE),
                   jax.ShapeDtypeStruct((s_len, LANES), F32)] + [jax.ShapeDtypeStruct((1, LANES), F32)] * 3,
        scratch_shapes=[pltpu.VMEM((DN_HEADS, hd, hd), F32)], compiler_params=_params("arbitrary"),
    )(cqkv, proj, ab, alog, dtb, onorm, states, dout)


def _final_loss(x, g, target, *, name):
    s, d = x.shape
    tm = _tile(s, (512, 256))

    def body(x_ref, g_ref, t_ref, loss_ref, dx_ref, dg_ref):
        @pl.when(pl.program_id(0) == 0)
        def _():
            loss_ref[...] = jnp.zeros_like(loss_ref)
            dg_ref[...] = jnp.zeros_like(dg_ref)

        xv, gv = x_ref[...], g_ref[...]
        r = lax.rsqrt(jnp.mean(xv * xv, axis=-1, keepdims=True) + NORM_EPS)
        xh = xv * r
        err = xh * gv - t_ref[...]
        loss_ref[...] += 0.5 * jnp.sum(jnp.mean(err * err, axis=-1, keepdims=True), axis=0, keepdims=True)
        dy = err * (1.0 / d)
        dxh = dy * gv
        dx_ref[...] = r * (dxh - xh * jnp.mean(dxh * xh, axis=-1, keepdims=True))
        dg_ref[...] += jnp.sum(dy * xh, axis=0, keepdims=True)

    row = pl.BlockSpec((tm, d), lambda i: (i, 0))
    vec = pl.BlockSpec((1, d), lambda i: (0, 0))
    return pl.pallas_call(
        body, name=name, grid=(s // tm,), in_specs=[row, vec, row],
        out_specs=[pl.BlockSpec((1, LANES), lambda i: (0, 0)), row, vec],
        out_shape=[jax.ShapeDtypeStruct((1, LANES), F32), jax.ShapeDtypeStruct((s, d), F32), jax.ShapeDtypeStruct((1, d), F32)],
        compiler_params=_params("arbitrary"),
    )(x, g.reshape(1, d), target)


def _adamw(w, g, m, v, *, name):
    shape = w.shape
    cols = shape[-1]
    rows = max(w.size // cols, 1)
    tr = _tile(rows, (512, 352, 256, 128, 64, 32, 16, 8))
    c1, c2 = 1.0 - ADAM_B1 ** ADAM_STEP, 1.0 - ADAM_B2 ** ADAM_STEP

    def body(w_ref, g_ref, m_ref, v_ref, d_ref, nm_ref, nv_ref):
        gv = g_ref[...]
        nm = ADAM_B1 * m_ref[...] + (1.0 - ADAM_B1) * gv
        nv = ADAM_B2 * v_ref[...] + (1.0 - ADAM_B2) * (gv * gv)
        d_ref[...] = -ADAM_LR * ((nm / c1) / (jnp.sqrt(nv / c2) + ADAM_EPS) + ADAM_WD * w_ref[...])
        nm_ref[...] = nm
        nv_ref[...] = nv

    blk = pl.BlockSpec((tr, cols), lambda i: (i, 0))
    outs = pl.pallas_call(
        body, name=name, grid=(rows // tr,), in_specs=[blk] * 4, out_specs=[blk] * 3,
        out_shape=[jax.ShapeDtypeStruct((rows, cols), F32)] * 3, compiler_params=_params("parallel"),
    )(*(t.reshape(rows, cols) for t in (w, g, m, v)))
    return tuple(t.reshape(shape) for t in outs)


def _block_diag(w):
    n, j, k = w.shape
    eye = jnp.eye(n, dtype=w.dtype)
    return (eye[:, None, :, None] * w[:, :, None, :]).reshape(n * j, n * k)


def _block_diag_part(m, n):
    j, k = m.shape[0] // n, m.shape[1] // n
    m4 = m.reshape(n, j, n, k)
    return jnp.stack([m4[i, :, i, :] for i in range(n)], axis=0)


DN_AB = 2 * DN_HEADS
DEPTH = 2


def _row(v, width=None):
    v = v.reshape(1, -1)
    return v if width is None else jnp.pad(v, ((0, 0), (0, width - v.shape[1])))


def _conv_w8(conv_w, bias=None):
    w8 = jnp.zeros((SUBLANES, conv_w.shape[1]), F32).at[:CONV_K].set(conv_w)
    return w8 if bias is None else w8.at[CONV_K].set(bias)


def _mixer_ab_fwd(x, w, tag):
    h = _rms_fwd(x, w["mix_norm"][0], name=f"{tag}_norm")
    proj = _mm(h, w["ab_w_in"][0], name=f"{tag}_in")
    o, lse = _dattn_forward(proj, tag)
    w8 = _conv_w8(w["lru_conv_w"][0], w["lru_conv_b"][0])
    xc = _conv_fwd(proj, PROJ_AB_BLOCKS - 2, LRU_WIDTH, w8, name=f"{tag}_conv")
    wa, wx = _block_diag(w["lru_w_a"][0]), _block_diag(w["lru_w_x"][0])
    vecs = (_row(w["lru_b_a"][0]), _row(w["lru_b_x"][0]), _row(w["lru_lambda"][0]))
    hs, y = _lru_fwd(xc, proj, wa, wx, *vecs, name=f"{tag}_lru")
    w_out = w["ab_w_out"][0]
    x2 = _mm(o, w_out[:ATTN_WIDTH], res=x, name=f"{tag}_out_attn")
    x2 = _mm(y, w_out[ATTN_WIDTH:], res=x2, name=f"{tag}_out_lru")
    return x2, (x, h, proj, o, lse, w8, xc, wa, wx, vecs, hs, y)


def _mixer_ab_bwd(saved, w, dy, tag):
    x, h, proj, o, lse, w8, xc, wa, wx, vecs, hs, y = saved
    w_out = w["ab_w_out"][0]
    dcat = _mm(dy, w_out, mode="nt", name=f"{tag}_dcat")
    dw_out = jnp.concatenate([_mm(o, dy, mode="tn", name=f"{tag}_dwout_attn"), _mm(y, dy, mode="tn", name=f"{tag}_dwout_lru")], axis=0)
    dq, dk, dv = _dattn_backward(proj, o, lse, dcat, tag)
    dxc, dgr, dwa, dwx, dba, dbx, dlam = _lru_bwd(xc, proj, hs, dcat, wa, wx, *vecs, name=f"{tag}_dlru")
    dxr, dw8 = _conv_bwd(proj, PROJ_AB_BLOCKS - 2, LRU_WIDTH, w8, dxc, name=f"{tag}_dconv")
    dproj = jnp.concatenate([t.astype(MXU_DTYPE) for t in (dq, dk, dv, dxr, dgr)], axis=1)
    dw_in = _mm(h, dproj, mode="tn", name=f"{tag}_dwin")
    dh = _mm(dproj, w["ab_w_in"][0], mode="nt", name=f"{tag}_dh")
    dx, dg = _rms_bwd(x, w["mix_norm"][0], dh, dy, name=f"{tag}_dnorm")
    grads = dict(mix_norm=dg[0], ab_w_in=dw_in, ab_w_out=dw_out, lru_conv_w=dw8[:CONV_K], lru_conv_b=dw8[CONV_K],
                 lru_w_a=_block_diag_part(dwa, LRU_BLOCKS), lru_b_a=dba[0], lru_w_x=_block_diag_part(dwx, LRU_BLOCKS),
                 lru_b_x=dbx[0], lru_lambda=dlam[0])
    return dx, grads


def _dn_split_w(w_in):
    return w_in[:, :4 * DN_WIDTH], jnp.pad(w_in[:, 4 * DN_WIDTH:], ((0, 0), (0, LANES - DN_AB)))


def _mixer_dn_fwd(x, w, tag):
    h = _rms_fwd(x, w["mix_norm"][1], name=f"{tag}_norm")
    w_qkvz, w_ab = _dn_split_w(w["dn_w_in"][0])
    proj = _mm(h, w_qkvz, name=f"{tag}_in")
    ab = _mm(h, w_ab, name=f"{tag}_in_ab")
    w8 = _conv_w8(w["dn_conv_w"][0])
    cqkv = _conv_fwd(proj, 0, 3 * DN_WIDTH, w8, name=f"{tag}_conv")
    vecs = (_row(w["dn_a_log"][0], LANES), _row(w["dn_dt_bias"][0], LANES), _row(w["dn_o_norm"][0]))
    og, states = _dn_fwd(cqkv, proj, ab, *vecs, name=f"{tag}_dn")
    x2 = _mm(og, w["dn_w_out"][0], res=x, name=f"{tag}_out")
    return x2, (x, h, w_qkvz, w_ab, proj, ab, w8, cqkv, vecs, og, states)


def _mixer_dn_bwd(saved, w, dy, tag):
    x, h, w_qkvz, w_ab, proj, ab, w8, cqkv, vecs, og, states = saved
    dout = _mm(dy, w["dn_w_out"][0], mode="nt", name=f"{tag}_dout")
    dw_out = _mm(og, dy, mode="tn", name=f"{tag}_dwout")
    dcqkv, dz, dab, dalog, ddtb, don = _dn_bwd(cqkv, proj, ab, *vecs, states, dout, name=f"{tag}_ddn")
    dqkv, dw8 = _conv_bwd(proj, 0, 3 * DN_WIDTH, w8, dcqkv, name=f"{tag}_dconv")
    dproj = jnp.concatenate([dqkv.astype(MXU_DTYPE), dz.astype(MXU_DTYPE)], axis=1)
    dw_in = jnp.concatenate([_mm(h, dproj, mode="tn", name=f"{tag}_dwin"),
                             _mm(h, dab, mode="tn", name=f"{tag}_dwin_ab")[:, :DN_AB]], axis=1)
    dh = _mm(dproj, w_qkvz, mode="nt", name=f"{tag}_dh")
    dh = _mm(dab, w_ab, mode="nt", res=dh, name=f"{tag}_dh_ab")
    dx, dg = _rms_bwd(x, w["mix_norm"][1], dh, dy, name=f"{tag}_dnorm")
    grads = dict(mix_norm=dg[0], dn_w_in=dw_in, dn_w_out=dw_out, dn_conv_w=dw8[:CONV_K], dn_a_log=dalog[0, :DN_HEADS],
                 dn_dt_bias=ddtb[0, :DN_HEADS], dn_o_norm=don[0])
    return dx, grads


def _xa_layer_fwd(x, mem, w, layer, tag):
    hq = _rms_fwd(x, w["xa_norm"][layer], name=f"{tag}_norm")
    q = _mm(hq, w["xa_wq"][layer], out_dtype=MXU_DTYPE, name=f"{tag}_q")
    hm = _rms_fwd(mem, w["xa_mem_norm"][layer], name=f"{tag}_mem_norm")
    kv = _mm(hm, w["xa_wkv"][layer], name=f"{tag}_kv")
    oa = _xa_fwd(q, kv, name=f"{tag}_core")
    x2 = _mm(oa, w["xa_wo"][layer], res=x, name=f"{tag}_out")
    return x2, (x, hq, q, hm, kv, oa)


def _xa_layer_bwd(saved, mem, w, layer, dy, tag):
    x, hq, q, hm, kv, oa = saved
    do = _mm(dy, w["xa_wo"][layer], mode="nt", name=f"{tag}_do")
    dwo = _mm(oa, dy, mode="tn", name=f"{tag}_dwo")
    dq, dkv = _xa_bwd(q, kv, do, name=f"{tag}_dcore")
    dwq = _mm(hq, dq, mode="tn", name=f"{tag}_dwq")
    dhq = _mm(dq, w["xa_wq"][layer], mode="nt", name=f"{tag}_dhq")
    dx, dg = _rms_bwd(x, w["xa_norm"][layer], dhq, dy, name=f"{tag}_dnorm")
    dwkv = _mm(hm, dkv, mode="tn", name=f"{tag}_dwkv")
    dhm = _mm(dkv, w["xa_wkv"][layer], mode="nt", name=f"{tag}_dhm")
    _, dgm = _rms_bwd(mem, w["xa_mem_norm"][layer], dhm, jnp.zeros_like(mem), name=f"{tag}_dmem_norm")
    return dx, dict(xa_norm=dg[0], xa_mem_norm=dgm[0], xa_wq=dwq, xa_wkv=dwkv, xa_wo=dwo)


def _local_step(x, mem, target, w):
    saved = []
    for layer in range(DEPTH):
        t = f"l{layer}"
        x, s1 = _ffn_fwd(x, w["ffn1_norm"], w["ffn1_w_in"], w["ffn1_w_out"], layer, f"{t}_ffn1")
        x, s2 = (_mixer_ab_fwd if layer % 2 == 0 else _mixer_dn_fwd)(x, w, f"{t}_mix")
        x, s3 = _xa_layer_fwd(x, mem, w, layer, f"{t}_xa")
        x, s4 = _ffn_fwd(x, w["ffn2_norm"], w["ffn2_w_in"], w["ffn2_w_out"], layer, f"{t}_ffn2")
        saved.append((s1, s2, s3, s4))
    loss, dx, dgf = _final_loss(x, w["final_norm"], target, name="final_loss")
    per_layer = []
    for layer in reversed(range(DEPTH)):
        t = f"l{layer}"
        s1, s2, s3, s4 = saved[layer]
        g = {}
        dx, g["ffn2_norm"], g["ffn2_w_in"], g["ffn2_w_out"] = _ffn_bwd(
            s4, w["ffn2_norm"], w["ffn2_w_in"], w["ffn2_w_out"], layer, dx, f"{t}_ffn2")
        dx, gx = _xa_layer_bwd(s3, mem, w, layer, dx, f"{t}_xa")
        dx, gm = (_mixer_ab_bwd if layer % 2 == 0 else _mixer_dn_bwd)(s2, w, dx, f"{t}_mix")
        dx, g["ffn1_norm"], g["ffn1_w_in"], g["ffn1_w_out"] = _ffn_bwd(
            s1, w["ffn1_norm"], w["ffn1_w_in"], w["ffn1_w_out"], layer, dx, f"{t}_ffn1")
        per_layer.append({**g, **gx, **gm})
    per_layer.reverse()
    grads = {"final_norm": [dgf[0]]}
    for layer_grads in per_layer:
        for name, value in layer_grads.items():
            grads.setdefault(name, []).append(value)
    return loss, dx, grads


N_CHIPS = 4
WIRE_DTYPE = jnp.bfloat16
HBM_SPEC = pl.BlockSpec(memory_space=pltpu.HBM)
PACK_COLS = 1024


def _place():
    x, y, c = lax.axis_index("x"), lax.axis_index("y"), lax.axis_index("c")
    return x, y, c, [(1 - x, y), (x, 1 - y), (1 - x, 1 - y)]


def _remote(src, dst, sems, k, to):
    return pltpu.make_async_remote_copy(src_ref=src, dst_ref=dst, send_sem=sems[0].at[k], recv_sem=sems[1].at[k],
                                        device_id=to, device_id_type=MESH)


def _gather_weights(blocks, axes):
    n = len(blocks)
    split = [b.shape[1] % 32 == 0 for b in blocks]

    def full_shape(i):
        l, r, c = blocks[i].shape
        return (l, N_CHIPS * r, c) if axes[i] == 1 else (l, r, N_CHIPS * c)

    def body(*refs):
        ins, outs = refs[:n], refs[n:2 * n]
        send_sems, recv_sems, local_sems = refs[2 * n:]
        x, y, c, chips = _place()
        sems = (send_sems, recv_sems)
        sibling = (x, y, 1 - c)

        def window(i, k, h):
            l, r, cc = blocks[i].shape
            r0, nr = (0, r) if h is None else (h * (r // 2), r // 2)
            if axes[i] == 1:
                return outs[i].at[:, pl.ds(k * r + r0, nr), :]
            return outs[i].at[:, pl.ds(r0, nr), pl.ds(k * cc, cc)]

        def mine(i, h):
            r = blocks[i].shape[1]
            return ins[i] if h is None else ins[i].at[:, pl.ds(h * (r // 2), r // 2), :]

        me = 2 * x + y
        local = [pltpu.make_async_copy(ins[i], window(i, me, None), local_sems.at[i]) for i in range(n)]
        for cp in local:
            cp.start()
        first, passed = [], []
        for i in range(n):
            h = c if split[i] else None
            for j, chip in enumerate(chips):
                first.append(_remote(mine(i, h), window(i, me, h), sems, 3 * i + j, (*chip, c)))
        for cp in first:
            cp.start()
        for i in range(n):
            h = c if split[i] else None
            for j, (cx, cy) in enumerate(chips):
                got = window(i, 2 * cx + cy, h)
                _remote(got, got, sems, 3 * i + j, (cx, cy, c)).wait_recv()
                if split[i]:
                    passed.append(_remote(got, got, sems, 3 * (n + i) + j, sibling))
                    passed[-1].start()
        for i in range(n):
            if split[i]:
                for j, (cx, cy) in enumerate(chips):
                    got = window(i, 2 * cx + cy, 1 - c)
                    _remote(got, got, sems, 3 * (n + i) + j, sibling).wait_recv()
        for cp in first + passed:
            cp.wait_send()
        for cp in local:
            cp.wait()

    return pl.pallas_call(
        body, name="gather_weights", in_specs=[HBM_SPEC] * n, out_specs=[HBM_SPEC] * n,
        out_shape=[jax.ShapeDtypeStruct(full_shape(i), blocks[i].dtype) for i in range(n)],
        scratch_shapes=[pltpu.SemaphoreType.DMA((6 * n,)), pltpu.SemaphoreType.DMA((6 * n,)), pltpu.SemaphoreType.DMA((n,))],
    )(*blocks)


def _allreduce_small(v):
    rows, cols = v.shape
    n_dev = 2 * N_CHIPS

    def body(v_ref, out_ref, all_ref, send_sems, recv_sems, local_sem):
        x, y, c, chips = _place()
        sems = (send_sems, recv_sems)
        me, sibling = (x, y, c), (x, y, 1 - c)
        slot = lambda px, py, pc: all_ref.at[pl.ds((4 * px + 2 * py + pc) * rows, rows), :]
        mine = pltpu.make_async_copy(v_ref, slot(*me), local_sem)
        mine.start()
        first = [_remote(v_ref, slot(*me), sems, 0, sibling)]
        first += [_remote(v_ref, slot(*me), sems, 1 + j, (*chip, c)) for j, chip in enumerate(chips)]
        for cp in first:
            cp.start()
        passed = [_remote(slot(*chip, c), slot(*chip, c), sems, 4 + j, sibling) for j, chip in enumerate(chips)]
        for j, chip in enumerate(chips):
            _remote(slot(*chip, c), slot(*chip, c), sems, 1 + j, me).wait_recv()
            passed[j].start()
        _remote(slot(*sibling), slot(*sibling), sems, 0, me).wait_recv()
        for j, chip in enumerate(chips):
            _remote(slot(*chip, 1 - c), slot(*chip, 1 - c), sems, 4 + j, me).wait_recv()
        for cp in first + passed:
            cp.wait_send()
        mine.wait()
        acc = all_ref[pl.ds(0, rows), :]
        for k in range(1, n_dev):
            acc = acc + all_ref[pl.ds(k * rows, rows), :]
        out_ref[...] = acc

    vmem = pl.BlockSpec(memory_space=pltpu.VMEM)
    return pl.pallas_call(
        body, name="allreduce_small", in_specs=[vmem], out_specs=vmem, out_shape=jax.ShapeDtypeStruct((rows, cols), F32),
        scratch_shapes=[pltpu.VMEM((n_dev * rows, cols), F32), pltpu.SemaphoreType.DMA((7,)), pltpu.SemaphoreType.DMA((7,)),
                        pltpu.SemaphoreType.DMA],
    )(v)


def _swap_other_half(g4):
    n, _, rows, cols = g4.shape

    def body(v_ref, out_ref, send_sems, recv_sems):
        x, y, c, _ = _place()
        cp = _remote(v_ref.at[:, 1 - c], out_ref, (send_sems, recv_sems), 0, (x, y, 1 - c))
        cp.start()
        cp.wait()

    return pl.pallas_call(
        body, name="reduce_swap", in_specs=[HBM_SPEC], out_specs=HBM_SPEC, out_shape=jax.ShapeDtypeStruct((n, rows, cols), g4.dtype),
        scratch_shapes=[pltpu.SemaphoreType.DMA((1,)), pltpu.SemaphoreType.DMA((1,))],
    )(g4)


def _add_kept_half(g4, got):
    n, _, rows, cols = g4.shape
    tr = _tile(rows, (256, 128, 64, 32, 16))
    nb = rows // tr

    def body(c_ref, a_ref, b_ref, o_ref):
        o_ref[...] = (a_ref[...] + b_ref[...]).astype(o_ref.dtype)

    return pl.pallas_call(
        body, name="reduce_sum_cores",
        grid_spec=pltpu.PrefetchScalarGridSpec(
            num_scalar_prefetch=1, grid=(n, nb),
            in_specs=[pl.BlockSpec((None, None, tr, cols), lambda k, i, c_ref: (k, c_ref[0], i, 0)),
                      pl.BlockSpec((None, tr, cols), lambda k, i, c_ref: (k, i, 0))],
            out_specs=pl.BlockSpec((None, tr, cols), lambda k, i, c_ref: (k, i, 0))),
        out_shape=jax.ShapeDtypeStruct((n, rows, cols), WIRE_DTYPE), compiler_params=_params("parallel", "parallel"),
    )(lax.axis_index("c").astype(jnp.int32).reshape(1), g4, got)


def _exchange_chips(v):
    def body(v_ref, out_ref, send_sems, recv_sems, local_sem):
        x, y, c, chips = _place()
        sems = (send_sems, recv_sems)
        me = 2 * x + y
        mine = pltpu.make_async_copy(v_ref.at[me], out_ref.at[me], local_sem)
        mine.start()
        sends = [_remote(v_ref.at[2 * cx + cy], out_ref.at[me], sems, j, (cx, cy, c)) for j, (cx, cy) in enumerate(chips)]
        for cp in sends:
            cp.start()
        for j, (cx, cy) in enumerate(chips):
            _remote(v_ref.at[me], out_ref.at[2 * cx + cy], sems, j, (cx, cy, c)).wait_recv()
        for cp in sends:
            cp.wait_send()
        mine.wait()

    return pl.pallas_call(
        body, name="exchange_chips", in_specs=[HBM_SPEC], out_specs=HBM_SPEC, out_shape=jax.ShapeDtypeStruct(v.shape, v.dtype),
        scratch_shapes=[pltpu.SemaphoreType.DMA((3,)), pltpu.SemaphoreType.DMA((3,)), pltpu.SemaphoreType.DMA],
    )(v)


def _share_halves(v):
    def body(v_ref, out_ref, send_sems, recv_sems, local_sem):
        x, y, c, _ = _place()
        mine = pltpu.make_async_copy(v_ref, out_ref.at[c], local_sem)
        mine.start()
        cp = _remote(v_ref, out_ref.at[c], (send_sems, recv_sems), 0, (x, y, 1 - c))
        cp.start()
        _remote(v_ref, out_ref.at[1 - c], (send_sems, recv_sems), 0, (x, y, 1 - c)).wait_recv()
        cp.wait_send()
        mine.wait()

    return pl.pallas_call(
        body, name="share_halves", in_specs=[HBM_SPEC], out_specs=HBM_SPEC, out_shape=jax.ShapeDtypeStruct((2,) + v.shape, v.dtype),
        scratch_shapes=[pltpu.SemaphoreType.DMA((1,)), pltpu.SemaphoreType.DMA((1,)), pltpu.SemaphoreType.DMA],
    )(v)


def _sum_blocks(terms, *, out_dtype, name):
    _, rows, cols = terms[0][0].shape
    tr = _tile(rows, (256, 128, 64, 32, 16, 8))

    def body(*refs):
        acc = refs[0][...].astype(F32)
        for r in refs[1:-1]:
            acc = acc + r[...].astype(F32)
        refs[-1][...] = acc.astype(out_dtype)

    spec = lambda k: pl.BlockSpec((None, tr, cols), lambda i: (k, i, 0))
    return pl.pallas_call(
        body, name=name, grid=(rows // tr,), in_specs=[spec(k) for _, k in terms],
        out_specs=pl.BlockSpec((tr, cols), lambda i: (i, 0)), out_shape=jax.ShapeDtypeStruct((rows, cols), out_dtype),
        compiler_params=_params("parallel"),
    )(*(a for a, _ in terms))


def _reduce_grads(g4):
    parts = _exchange_chips(_add_kept_half(g4, _swap_other_half(g4)))
    half = _sum_blocks([(parts, k) for k in range(N_CHIPS)], out_dtype=F32, name="reduce_sum_chips")
    return _share_halves(half)


BIG = (("ffn1_w_in", 2), ("ffn1_w_out", 1), ("xa_wq", 1), ("xa_wkv", 2), ("xa_wo", 1), ("ffn2_w_in", 2), ("ffn2_w_out", 1),
       ("ab_w_in", 2), ("ab_w_out", 1), ("dn_w_in", 2), ("dn_w_out", 1))
TINY_SHARDED = (("lru_conv_w", 2), ("dn_conv_w", 2))
REPLICATED = ("ffn1_norm", "mix_norm", "xa_norm", "xa_mem_norm", "ffn2_norm", "lru_conv_b", "lru_w_a", "lru_b_a", "lru_w_x",
              "lru_b_x", "lru_lambda", "dn_a_log", "dn_dt_bias", "dn_o_norm", "final_norm")
WEIGHTS = ("ffn1_norm", "ffn1_w_in", "ffn1_w_out", "mix_norm", "xa_norm", "xa_mem_norm", "xa_wq", "xa_wkv", "xa_wo", "ffn2_norm",
           "ffn2_w_in", "ffn2_w_out", "ab_w_in", "lru_conv_w", "lru_conv_b", "lru_w_a", "lru_b_a", "lru_w_x", "lru_b_x",
           "lru_lambda", "ab_w_out", "dn_w_in", "dn_conv_w", "dn_a_log", "dn_dt_bias", "dn_o_norm", "dn_w_out", "final_norm")


def _pad_rows(flat, row_multiple):
    n = flat.shape[-1]
    per = row_multiple * PACK_COLS
    total = -(-n // per) * per
    flat = jnp.pad(flat, [(0, 0)] * (flat.ndim - 1) + [(0, total - n)])
    return flat.reshape(flat.shape[:-1] + (total // PACK_COLS, PACK_COLS))


def _lane_padded(shape):
    return shape[:-1] + (-(-shape[-1] // LANES) * LANES,)


def _pad_lanes(t):
    return jnp.pad(t, [(0, 0)] * (t.ndim - 1) + [(0, _lane_padded(t.shape)[-1] - t.shape[-1])])


def _gather_full(shards):
    named = BIG + TINY_SHARDED
    blocks = [_pad_lanes(shards[n]).astype(MXU_DTYPE) if (n, a) in BIG else shards[n] for n, a in named]
    out = dict(zip((n for n, _ in named), _gather_weights(blocks, [a for _, a in named])))
    for n, axis in BIG:
        width, padded = shards[n].shape[-1], _lane_padded(shards[n].shape)[-1]
        if padded != width:
            assert axis == 2
            out[n] = jnp.concatenate([out[n][..., k * padded:k * padded + width] for k in range(N_CHIPS)], axis=-1)
    return out


def _to_blocks(pieces, axis):
    width = pieces[0].shape[axis - 1] // N_CHIPS
    block = lambda p, k: _pad_lanes(lax.slice_in_dim(p, k * width, (k + 1) * width, axis=axis - 1)).reshape(-1)
    return jnp.stack([jnp.concatenate([block(p, k) for p in pieces]) for k in range(N_CHIPS)], axis=0)


def _pack_grads(grads):
    flat = jnp.concatenate([_to_blocks(grads[n], axis) for n, axis in BIG], axis=1)
    g = _pad_rows(flat, 2 * 256)
    return g.reshape(N_CHIPS, 2, g.shape[1] // 2, PACK_COLS)


def _unpack_grads(reduced, shards):
    flat = reduced.reshape(-1)
    out, off = {}, 0
    for n, _ in BIG:
        shape = shards[n].shape
        size = math.prod(_lane_padded(shape))
        out[n] = flat[off:off + size].reshape(_lane_padded(shape))[..., :shape[-1]]
        off += size
    return out


def _pack_small(grads, loss):
    parts = [p.reshape(-1) for n in REPLICATED + tuple(n for n, _ in TINY_SHARDED) for p in grads[n]] + [loss[0, :1]]
    flat = jnp.concatenate(parts)
    total = -(-flat.shape[0] // (SUBLANES * LANES)) * SUBLANES * LANES
    return jnp.pad(flat, (0, total - flat.shape[0])).reshape(-1, LANES)


def _unpack_small(summed, shards, chip):
    flat = summed.reshape(-1)
    out, off = {}, 0
    for n in REPLICATED:
        out[n] = flat[off:off + shards[n].size].reshape(shards[n].shape)
        off += shards[n].size
    for n, axis in TINY_SHARDED:
        width = shards[n].shape[axis]
        shape = shards[n].shape[:axis] + (N_CHIPS * width,) + shards[n].shape[axis + 1:]
        full = flat[off:off + N_CHIPS * shards[n].size].reshape(shape)
        out[n] = lax.dynamic_slice_in_dim(full, chip * width, width, axis=axis)
        off += N_CHIPS * shards[n].size
    return out, flat[off]


def kernel(x, mem, ffn1_norm, ffn1_w_in, ffn1_w_out, mix_norm, xa_norm, xa_mem_norm, xa_wq, xa_wkv, xa_wo, ffn2_norm,
           ffn2_w_in, ffn2_w_out, ab_w_in, lru_conv_w, lru_conv_b, lru_w_a, lru_b_a, lru_w_x, lru_b_x, lru_lambda,
           ab_w_out, dn_w_in, dn_conv_w, dn_a_log, dn_dt_bias, dn_o_norm, dn_w_out, final_norm, loss_target,
           m_ffn1_norm, m_ffn1_w_in, m_ffn1_w_out, m_mix_norm, m_xa_norm, m_xa_mem_norm, m_xa_wq, m_xa_wkv, m_xa_wo,
           m_ffn2_norm, m_ffn2_w_in, m_ffn2_w_out, m_ab_w_in, m_lru_conv_w, m_lru_conv_b, m_lru_w_a, m_lru_b_a,
           m_lru_w_x, m_lru_b_x, m_lru_lambda, m_ab_w_out, m_dn_w_in, m_dn_conv_w, m_dn_a_log, m_dn_dt_bias,
           m_dn_o_norm, m_dn_w_out, m_final_norm, v_ffn1_norm, v_ffn1_w_in, v_ffn1_w_out, v_mix_norm, v_xa_norm,
           v_xa_mem_norm, v_xa_wq, v_xa_wkv, v_xa_wo, v_ffn2_norm, v_ffn2_w_in, v_ffn2_w_out, v_ab_w_in,
           v_lru_conv_w, v_lru_conv_b, v_lru_w_a, v_lru_b_a, v_lru_w_x, v_lru_b_x, v_lru_lambda, v_ab_w_out,
           v_dn_w_in, v_dn_conv_w, v_dn_a_log, v_dn_dt_bias, v_dn_o_norm, v_dn_w_out, v_final_norm):
    given = dict(locals())
    shards = {n: given[n] for n in WEIGHTS}
    chip = 2 * lax.axis_index("x") + lax.axis_index("y")

    full = {n: shards[n] for n in REPLICATED}
    full.update(_gather_full(shards))
    loss, grad_x, grads = _local_step(x[0], mem[0], loss_target[0], full)

    small, loss_sum = _unpack_small(_allreduce_small(_pack_small(grads, loss)), shards, chip)
    grad = {**small, **_unpack_grads(_reduce_grads(_pack_grads(grads)), shards)}

    delta, new_m, new_v = {}, {}, {}
    for n in WEIGHTS:
        delta[n], new_m[n], new_v[n] = _adamw(shards[n], grad[n], given["m_" + n], given["v_" + n], name=f"adamw_{n}")
    return (loss_sum, grad_x[None], *[grad[n] for n in WEIGHTS], *[delta[n] for n in WEIGHTS],
            *[new_m[n] for n in WEIGHTS], *[new_v[n] for n in WEIGHTS])
```

```python
import functools
import math

import jax
import jax.numpy as jnp
from jax import lax
from jax.experimental import pallas as pl
from jax.experimental.pallas import tpu as pltpu

F32 = jnp.float32
MXU_DTYPE = jnp.bfloat16
VMEM_LIMIT_BYTES = 48 * 1024 * 1024
MM_BLOCK_BYTES = 8 * 1024 * 1024
LANES = 128
SUBLANES = 8

NORM_EPS = 1e-6
CONV_K = 4
ATTN_PAIRS = 4
ATTN_HEAD_DIM = 64
ATTN_WIDTH = 512
ATTN_BLOCK = 128
DILATIONS = (1, 4, 16)
LRU_WIDTH = 512
LRU_BLOCKS = 8
LRU_C = 8.0
DN_HEADS = 8
DN_HEAD_DIM = 128
DN_WIDTH = 1024
DN_CHUNK = 64
XA_HEADS = 4
XA_HEAD_DIM = 256
D_FF = 2816
ADAM_LR, ADAM_B1, ADAM_B2, ADAM_EPS, ADAM_WD, ADAM_STEP = 0.001, 0.9, 0.999, 1e-08, 0.01, 10

MESH = pl.DeviceIdType.MESH


def _tile(n, prefs):
    for p in prefs:
        if n % p == 0:
            return p
    return n


def _params(*sem):
    return pltpu.CompilerParams(dimension_semantics=sem, vmem_limit_bytes=VMEM_LIMIT_BYTES)


def _dg(a, b, dims, hi=False):
    if hi:
        return lax.dot_general(a, b, (dims, ((), ())), precision=lax.Precision.HIGHEST, preferred_element_type=F32)
    return lax.dot_general(a.astype(MXU_DTYPE), b.astype(MXU_DTYPE), (dims, ((), ())), preferred_element_type=F32)


def _make_dot(hi):
    @jax.custom_vjp
    def dot(a, b):
        return _dg(a, b, ((1,), (0,)), hi)

    def fwd(a, b):
        return dot(a, b), (a, b)

    def bwd(r, g):
        a, b = r
        return _dg(g, b, ((1,), (1,)), hi).astype(a.dtype), _dg(a, g, ((0,), (0,)), hi).astype(b.dtype)

    dot.defvjp(fwd, bwd)

    @jax.custom_vjp
    def dot_nt(a, b):
        return _dg(a, b, ((1,), (1,)), hi)

    def fwd_nt(a, b):
        return dot_nt(a, b), (a, b)

    def bwd_nt(r, g):
        a, b = r
        return _dg(g, b, ((1,), (0,)), hi).astype(a.dtype), _dg(g, a, ((0,), (0,)), hi).astype(b.dtype)

    dot_nt.defvjp(fwd_nt, bwd_nt)

    @jax.custom_vjp
    def dot_tn(a, b):
        return _dg(a, b, ((0,), (0,)), hi)

    def fwd_tn(a, b):
        return dot_tn(a, b), (a, b)

    def bwd_tn(r, g):
        a, b = r
        return _dg(b, g, ((1,), (1,)), hi).astype(a.dtype), _dg(a, g, ((1,), (0,)), hi).astype(b.dtype)

    dot_tn.defvjp(fwd_tn, bwd_tn)
    return dot, dot_nt, dot_tn


_bdot, _bdot_nt, _bdot_tn = _make_dot(False)
_hdot, _hdot_nt, _hdot_tn = _make_dot(True)


def _log1p(t):
    return jnp.where(t < 0.01, t * (1.0 - t * (0.5 - t * (1.0 / 3.0))), jnp.log(1.0 + t))


def _neg_expm1(y):
    series = -y * (1.0 + 0.5 * y * (1.0 + (1.0 / 3.0) * y * (1.0 + 0.25 * y)))
    return jnp.where(y > -0.01, series, 1.0 - jnp.exp(y))


def _softplus(x):
    return jnp.maximum(x, 0.0) + _log1p(jnp.exp(-jnp.abs(x)))


def _sigmoid(x):
    return 1.0 / (1.0 + jnp.exp(-x))


def _silu(x):
    return x * _sigmoid(x)


def _gelu(x):
    return 0.5 * x * (1.0 + jnp.tanh(0.7978845608028654 * (x + 0.044715 * x * x * x)))


def _rows(shape):
    return lax.broadcasted_iota(jnp.int32, shape, 0)


def _cols(shape):
    return lax.broadcasted_iota(jnp.int32, shape, 1)


def _mm(a, b, *, mode="nn", out_dtype=F32, res=None, scale=1.0, name):
    if mode == "nn":
        (m, k), (k2, n) = a.shape, b.shape
    elif mode == "nt":
        (m, k), (n, k2) = a.shape, b.shape
    else:
        (k, m), (k2, n) = a.shape, b.shape
    assert k == k2, (a.shape, b.shape, mode)
    if mode == "tn":
        tm, tn, tk = _tile(m, (1024, 512, 256, 128)), _tile(n, (1024, 512, 256, 128)), _tile(k, (2048, 1024, 512, 256))
    else:
        tm, tn = _tile(m, (512, 256, 128)), _tile(n, (1024, 512, 256, 128))
        tk = k if k * tn * 2 <= MM_BLOCK_BYTES else _tile(k, (1024, 512, 256, 128))
    nk = k // tk
    dims = {"nn": ((1,), (0,)), "nt": ((1,), (1,)), "tn": ((0,), (0,))}[mode]

    def body(*refs):
        a_ref, b_ref = refs[:2]
        r_ref = refs[2] if res is not None else None
        o_ref = refs[3 if res is not None else 2]

        def finish(r):
            if scale != 1.0:
                r = r * scale
            if res is not None:
                r = r_ref[...] + r
            o_ref[...] = r.astype(out_dtype)

        if nk == 1:
            finish(_dg(a_ref[...], b_ref[...], dims))
            return
        acc = refs[-1]
        kk = pl.program_id(2)

        @pl.when(kk == 0)
        def _():
            acc[...] = jnp.zeros_like(acc)

        acc[...] += _dg(a_ref[...], b_ref[...], dims)

        @pl.when(kk == nk - 1)
        def _():
            finish(acc[...])

    a_spec = pl.BlockSpec((tk, tm), lambda i, j, kk: (kk, i)) if mode == "tn" else pl.BlockSpec((tm, tk), lambda i, j, kk: (i, kk))
    b_spec = pl.BlockSpec((tn, tk), lambda i, j, kk: (j, kk)) if mode == "nt" else pl.BlockSpec((tk, tn), lambda i, j, kk: (kk, j))
    o_spec = pl.BlockSpec((tm, tn), lambda i, j, kk: (i, j))
    in_specs = [a_spec, b_spec] + ([o_spec] if res is not None else [])
    args = (a, b) + ((res,) if res is not None else ())
    return pl.pallas_call(
        body, name=name, grid=(m // tm, n // tn, nk), in_specs=in_specs, out_specs=o_spec,
        out_shape=jax.ShapeDtypeStruct((m, n), out_dtype), scratch_shapes=[pltpu.VMEM((tm, tn), F32)] if nk > 1 else [],
        compiler_params=_params("parallel", "parallel", "arbitrary"),
    )(*args)


def _rms_fwd(x, g, *, name):
    s, d = x.shape
    tm = _tile(s, (512, 256))

    def body(x_ref, g_ref, o_ref):
        xv = x_ref[...]
        r = lax.rsqrt(jnp.mean(xv * xv, axis=-1, keepdims=True) + NORM_EPS)
        o_ref[...] = (xv * r * g_ref[...]).astype(o_ref.dtype)

    return pl.pallas_call(
        body, name=name, grid=(s // tm,),
        in_specs=[pl.BlockSpec((tm, d), lambda i: (i, 0)), pl.BlockSpec((1, d), lambda i: (0, 0))],
        out_specs=pl.BlockSpec((tm, d), lambda i: (i, 0)), out_shape=jax.ShapeDtypeStruct((s, d), MXU_DTYPE),
        compiler_params=_params("parallel"),
    )(x, g.reshape(1, d))


def _norm_mm(x, g, ws, out_dtypes, *, name):
    s, d = x.shape
    tm = _tile(s, (512, 256))
    nw = len(ws)

    def body(*refs):
        x_ref, g_ref = refs[:2]
        h_ref = refs[2 + nw]
        xv = x_ref[...]
        r = lax.rsqrt(jnp.mean(xv * xv, axis=-1, keepdims=True) + NORM_EPS)
        h = (xv * r * g_ref[...]).astype(MXU_DTYPE)
        h_ref[...] = h
        for w_ref, o_ref in zip(refs[2:2 + nw], refs[3 + nw:]):
            o_ref[...] = _dg(h, w_ref[...], ((1,), (0,))).astype(o_ref.dtype)

    row = lambda w: pl.BlockSpec((tm, w), lambda i: (i, 0))
    outs = pl.pallas_call(
        body, name=name, grid=(s // tm,),
        in_specs=[row(d), pl.BlockSpec((1, d), lambda i: (0, 0))]
        + [pl.BlockSpec(w.shape, lambda i: (0, 0), pipeline_mode=RESIDENT) for w in ws],
        out_specs=[row(d)] + [row(w.shape[1]) for w in ws],
        out_shape=[jax.ShapeDtypeStruct((s, d), MXU_DTYPE)] + [jax.ShapeDtypeStruct((s, w.shape[1]), t) for w, t in zip(ws, out_dtypes)],
        compiler_params=_params("parallel"),
    )(x, g.reshape(1, d), *ws)
    return outs[0], outs[1:]


def _rms_bwd(x, g, dh, dres, *, name):
    s, d = x.shape
    tm = _tile(s, (512, 256))

    def body(x_ref, g_ref, dh_ref, dr_ref, dx_ref, dg_ref):
        xv = x_ref[...]
        r = lax.rsqrt(jnp.mean(xv * xv, axis=-1, keepdims=True) + NORM_EPS)
        xh = xv * r
        dhv = dh_ref[...].astype(F32)
        dxh = dhv * g_ref[...]
        dx = r * (dxh - xh * jnp.mean(dxh * xh, axis=-1, keepdims=True))
        dx_ref[...] = dr_ref[...] + dx

        @pl.when(pl.program_id(0) == 0)
        def _():
            dg_ref[...] = jnp.zeros_like(dg_ref)

        dg_ref[...] += jnp.sum(dhv * xh, axis=0, keepdims=True)

    row = pl.BlockSpec((tm, d), lambda i: (i, 0))
    vec = pl.BlockSpec((1, d), lambda i: (0, 0))
    return pl.pallas_call(
        body, name=name, grid=(s // tm,), in_specs=[row, vec, row, row], out_specs=[row, vec],
        out_shape=[jax.ShapeDtypeStruct((s, d), F32), jax.ShapeDtypeStruct((1, d), F32)],
        compiler_params=_params("arbitrary"),
    )(x, g.reshape(1, d), dh, dres)


FFN_CHUNK = 256
FFN_TM = 256
RESIDENT = pl.Buffered(1)


def _ffn_fwd_call(x, g, w_in, w_out, layer, *, name):
    s, d = x.shape
    f = w_out.shape[1]
    tm = _tile(s, (FFN_TM,))

    def body(x_ref, g_ref, wi_ref, wo_ref, y_ref, u_ref, act_ref):
        xv = x_ref[...]
        r = lax.rsqrt(jnp.mean(xv * xv, axis=-1, keepdims=True) + NORM_EPS)
        h = (xv * r * g_ref[...]).astype(MXU_DTYPE)
        for j in range(f // FFN_CHUNK):
            lo, hi = j * FFN_CHUNK, (j + 1) * FFN_CHUNK
            gate = _dg(h, wi_ref[:, lo:hi], ((1,), (0,))).astype(MXU_DTYPE)
            up = _dg(h, wi_ref[:, f + lo:f + hi], ((1,), (0,))).astype(MXU_DTYPE)
            u_ref[:, lo:hi] = gate
            u_ref[:, f + lo:f + hi] = up
            act_ref[:, lo:hi] = (_silu(gate.astype(F32)) * up.astype(F32)).astype(MXU_DTYPE)
        y_ref[...] = xv + 0.5 * _dg(act_ref[...], wo_ref[...], ((1,), (0,)))

    row = lambda w: pl.BlockSpec((tm, w), lambda i: (i, 0))
    return pl.pallas_call(
        body, name=name, grid=(s // tm,),
        in_specs=[row(d), pl.BlockSpec((1, d), lambda i: (0, 0)),
                  pl.BlockSpec((None,) + w_in.shape[1:], lambda i: (layer, 0, 0), pipeline_mode=RESIDENT),
                  pl.BlockSpec((None,) + w_out.shape[1:], lambda i: (layer, 0, 0), pipeline_mode=RESIDENT)],
        out_specs=[row(d), row(2 * f)],
        out_shape=[jax.ShapeDtypeStruct((s, d), F32), jax.ShapeDtypeStruct((s, 2 * f), MXU_DTYPE)],
        scratch_shapes=[pltpu.VMEM((tm, f), MXU_DTYPE)], compiler_params=_params("parallel"),
    )(x, g.reshape(1, d), w_in, w_out)


def _ffn_bwd_call(x, g, u, dy, w_in, w_out, layer, *, name):
    s, d = x.shape
    f = w_out.shape[1]
    tm = _tile(s, (FFN_TM,))

    def body(x_ref, g_ref, u_ref, dy_ref, wi_ref, wo_ref, du_ref, dx_ref, dg_ref, h_ref):
        dyv = dy_ref[...]
        dyh = (0.5 * dyv).astype(MXU_DTYPE)
        for j in range(f // FFN_CHUNK):
            lo, hi = j * FFN_CHUNK, (j + 1) * FFN_CHUNK
            dact = _dg(dyh, wo_ref[lo:hi, :], ((1,), (1,)))
            gate, up = u_ref[:, lo:hi].astype(F32), u_ref[:, f + lo:f + hi].astype(F32)
            sg = _sigmoid(gate)
            du_ref[:, lo:hi] = (dact * up * sg * (1.0 + gate * (1.0 - sg))).astype(MXU_DTYPE)
            du_ref[:, f + lo:f + hi] = (dact * gate * sg).astype(MXU_DTYPE)
        dh = _dg(du_ref[...], wi_ref[...], ((1,), (1,)))
        xv, gv = x_ref[...], g_ref[...]
        r = lax.rsqrt(jnp.mean(xv * xv, axis=-1, keepdims=True) + NORM_EPS)
        xh = xv * r
        h_ref[...] = (xh * gv).astype(MXU_DTYPE)
        dxh = dh * gv
        dx_ref[...] = dyv + r * (dxh - xh * jnp.mean(dxh * xh, axis=-1, keepdims=True))

        @pl.when(pl.program_id(0) == 0)
        def _():
            dg_ref[...] = jnp.zeros_like(dg_ref)

        dg_ref[...] += jnp.sum(dh * xh, axis=0, keepdims=True)

    row = lambda w: pl.BlockSpec((tm, w), lambda i: (i, 0))
    vec = pl.BlockSpec((1, d), lambda i: (0, 0))
    return pl.pallas_call(
        body, name=name, grid=(s // tm,),
        in_specs=[row(d), vec, row(2 * f), row(d),
                  pl.BlockSpec((None,) + w_in.shape[1:], lambda i: (layer, 0, 0), pipeline_mode=RESIDENT),
                  pl.BlockSpec((None,) + w_out.shape[1:], lambda i: (layer, 0, 0), pipeline_mode=RESIDENT)],
        out_specs=[row(2 * f), row(d), vec, row(d)],
        out_shape=[jax.ShapeDtypeStruct((s, 2 * f), MXU_DTYPE), jax.ShapeDtypeStruct((s, d), F32),
                   jax.ShapeDtypeStruct((1, d), F32), jax.ShapeDtypeStruct((s, d), MXU_DTYPE)],
        compiler_params=_params("arbitrary"),
    )(x, g.reshape(1, d), u, dy, w_in, w_out)


def _ffn_dw_out(u, dy, *, name):
    s, f2 = u.shape
    f, d = f2 // 2, dy.shape[1]
    tf, tk = _tile(f, (1408, 256, 128)), _tile(s, (512, 256))
    nj = f // tf

    def body(g_ref, u_ref, dy_ref, o_ref):
        @pl.when(pl.program_id(1) == 0)
        def _():
            o_ref[...] = jnp.zeros_like(o_ref)

        act = _silu(g_ref[...].astype(F32)) * u_ref[...].astype(F32)
        o_ref[...] += _dg(act, 0.5 * dy_ref[...], ((0,), (0,)))

    return pl.pallas_call(
        body, name=name, grid=(nj, s // tk),
        in_specs=[pl.BlockSpec((tk, tf), lambda j, k: (k, j)), pl.BlockSpec((tk, tf), lambda j, k: (k, j + nj)),
                  pl.BlockSpec((tk, d), lambda j, k: (k, 0))],
        out_specs=pl.BlockSpec((tf, d), lambda j, k: (j, 0)), out_shape=jax.ShapeDtypeStruct((f, d), F32),
        compiler_params=_params("parallel", "arbitrary"),
    )(u, u, dy)


def _ffn_fwd(x, g, w_in, w_out, layer, tag):
    y, u = _ffn_fwd_call(x, g[layer], w_in, w_out, layer, name=f"{tag}_fwd")
    return y, (x, u)


def _ffn_bwd(saved, g, w_in, w_out, layer, dy, tag):
    x, u = saved
    du, dx, dg, h = _ffn_bwd_call(x, g[layer], u, dy, w_in, w_out, layer, name=f"{tag}_bwd")
    dw_out = _ffn_dw_out(u, dy, name=f"{tag}_dwout")
    dw_in = _mm(h, du, mode="tn", name=f"{tag}_dwin")
    return dx, dg[0], dw_in, dw_out


ATTN_SCALE = ATTN_HEAD_DIM ** -0.5
NEG_BIG = -1e30
PROJ_AB_BLOCKS = 5


def _lane_tiles(t):
    return [t[:, p * LANES:(p + 1) * LANES] for p in range(t.shape[1] // LANES)]


def _first_head(n):
    return _cols((n, LANES)) < ATTN_HEAD_DIM


def _per_head(t):
    first = _first_head(t.shape[0])
    return jnp.stack([jnp.where(first == (h == 0), tile, 0.0) for tile in _lane_tiles(t) for h in (0, 1)], axis=0)


def _both(t):
    return jnp.stack([tile for tile in _lane_tiles(t) for _ in (0, 1)], axis=0)


def _head_cols(t):
    return jnp.stack([tile[:, c0:c0 + 1] for tile in _lane_tiles(t) for c0 in (0, ATTN_HEAD_DIM)], axis=0)


def _join_heads(v):
    return jnp.concatenate([v[2 * p] + v[2 * p + 1] for p in range(v.shape[0] // 2)], axis=1)


def _spread_heads(v):
    first = _first_head(v.shape[1])
    return jnp.concatenate([jnp.where(first, v[2 * p], v[2 * p + 1]) for p in range(v.shape[0] // 2)], axis=1)


def _band_masks(has_prev):
    qi, kj = _rows((ATTN_BLOCK, ATTN_BLOCK)), _cols((ATTN_BLOCK, ATTN_BLOCK))
    return (kj >= qi) & has_prev, kj <= qi


def _dattn_delta(o, dcat, *, name):
    s_len = o.shape[0]
    tm = _tile(s_len, (512, 256))

    def body(o_ref, do_ref, out_ref):
        r, c = _rows((ATTN_WIDTH, ATTN_WIDTH)), _cols((ATTN_WIDTH, ATTN_WIDTH))
        ones_bd = (r // ATTN_HEAD_DIM == c // ATTN_HEAD_DIM).astype(F32)
        out_ref[...] = _dg(o_ref[...] * do_ref[...], ones_bd, ((1,), (0,)), hi=True)

    blk = pl.BlockSpec((tm, ATTN_WIDTH), lambda i: (i, 0))
    return pl.pallas_call(
        body, name=name, grid=(s_len // tm,), in_specs=[blk, blk], out_specs=blk,
        out_shape=jax.ShapeDtypeStruct((s_len, ATTN_WIDTH), F32), compiler_params=_params("parallel"),
    )(o, dcat)


def _res_rows(r, d):
    return pl.ds(r, ATTN_BLOCK, stride=d) if d > 1 else pl.ds(0, ATTN_BLOCK)


ATTN_ALL_PAIRS_MAX_D = 1


def _pairs_per_step(d):
    return ATTN_PAIRS if d <= ATTN_ALL_PAIRS_MAX_D else 1


def _pair_specs(d, n_of):
    pairs = _pairs_per_step(d)
    groups = ATTN_PAIRS // pairs
    return lambda c: pl.BlockSpec((ATTN_BLOCK * d, LANES * pairs), lambda n, p: (n_of(n), c * groups + p))


def _sattn_fwd(proj, state, d, *, last, name):
    s_len = proj.shape[0]
    nb = s_len // (ATTN_BLOCK * d)
    first = state is None
    n_out = 2 if last else 3

    def body(*refs):
        q_ref, kp_ref, kc_ref, vp_ref, vc_ref = refs[:5]
        st_refs = () if first else refs[5:8]
        out_refs = refs[-n_out:]
        ok = jnp.concatenate(_band_masks(pl.program_id(0) > 0), axis=1)

        def residue(r, carry):
            rows = _res_rows(r, d)
            q = q_ref[rows, :]
            kcat = jnp.concatenate([kp_ref[rows, :], kc_ref[rows, :]], axis=0)
            vcat = jnp.concatenate([vp_ref[rows, :], vc_ref[rows, :]], axis=0)
            s = jnp.where(ok, _bdg(_per_head(q), _both(kcat), 2, 2) * ATTN_SCALE, NEG_BIG)
            m_new = jnp.max(s, axis=2, keepdims=True)
            if not first:
                m_in, l_in, a_in = (t[rows, :] for t in st_refs)
                m_old = _head_cols(m_in)
                m_new = jnp.maximum(m_old, m_new)
                alpha = jnp.exp(m_old - m_new)
            p = jnp.exp(s - m_new)
            l_new = jnp.sum(p, axis=2, keepdims=True)
            acc = _join_heads(_bdg(p, _per_head(vcat), 2, 1))
            if not first:
                l_new = l_new + _head_cols(l_in) * alpha
                acc = acc + a_in * _spread_heads(alpha)
            m_pair, l_pair = _spread_heads(m_new), _spread_heads(l_new)
            if last:
                out_refs[0][rows, :] = acc / l_pair
                out_refs[1][rows, :] = m_pair + jnp.log(l_pair)
            else:
                out_refs[0][rows, :] = m_pair
                out_refs[1][rows, :] = l_pair
                out_refs[2][rows, :] = acc
            return carry

        lax.fori_loop(0, d, residue, 0)

    cur, prev = _pair_specs(d, lambda n: n), _pair_specs(d, lambda n: jnp.maximum(n - 1, 0))
    st = cur(0)
    return tuple(pl.pallas_call(
        body, name=name, grid=(nb, ATTN_PAIRS // _pairs_per_step(d)),
        in_specs=[cur(0), prev(1), cur(1), prev(2), cur(2)] + ([] if first else [st] * 3),
        out_specs=[st] * n_out, out_shape=[jax.ShapeDtypeStruct((s_len, ATTN_WIDTH), F32)] * n_out,
        compiler_params=_params("arbitrary", "parallel"),
    )(*([proj] * 5 + ([] if first else list(state)))))


def _sattn_dq(proj, dcat, lse, delta, dq_in, d, *, name):
    s_len = proj.shape[0]
    nb = s_len // (ATTN_BLOCK * d)
    first = dq_in is None

    def body(*refs):
        q_ref, kp_ref, kc_ref, vp_ref, vc_ref, do_ref, lse_ref, dl_ref = refs[:8]
        dq_ref = refs[-1]
        ok = jnp.concatenate(_band_masks(pl.program_id(0) > 0), axis=1)

        def residue(r, carry):
            rows = _res_rows(r, d)
            kcat = jnp.concatenate([kp_ref[rows, :], kc_ref[rows, :]], axis=0)
            vcat = jnp.concatenate([vp_ref[rows, :], vc_ref[rows, :]], axis=0)
            s = _bdg(_per_head(q_ref[rows, :]), _both(kcat), 2, 2) * ATTN_SCALE
            pr = jnp.where(ok, jnp.exp(jnp.where(ok, s, NEG_BIG) - _head_cols(lse_ref[rows, :])), 0.0)
            ds = pr * (_bdg(_per_head(do_ref[rows, :]), _both(vcat), 2, 2) - _head_cols(dl_ref[rows, :]))
            acc = _join_heads(_bdg(ds, _per_head(kcat), 2, 1)) * ATTN_SCALE
            dq_ref[rows, :] = acc if first else acc + refs[8][rows, :]
            return carry

        lax.fori_loop(0, d, residue, 0)

    cur, prev = _pair_specs(d, lambda n: n), _pair_specs(d, lambda n: jnp.maximum(n - 1, 0))
    st = cur(0)
    return pl.pallas_call(
        body, name=name, grid=(nb, ATTN_PAIRS // _pairs_per_step(d)),
        in_specs=[cur(0), prev(1), cur(1), prev(2), cur(2), st, st, st] + ([] if first else [st]),
        out_specs=st, out_shape=jax.ShapeDtypeStruct((s_len, ATTN_WIDTH), F32),
        compiler_params=_params("arbitrary", "parallel"),
    )(*([proj] * 5 + [dcat, lse, delta] + ([] if first else [dq_in])))


def _sattn_dkv(proj, dcat, lse, delta, dkv_in, d, *, name):
    s_len = proj.shape[0]
    nb = s_len // (ATTN_BLOCK * d)
    first = dkv_in is None

    def body(*refs):
        k_ref, v_ref = refs[:2]
        qs = (refs[2:6], refs[6:10])
        dk_ref, dv_ref = refs[-2:]
        nxt_ok, cur_ok = _band_masks(pl.program_id(0) < nb - 1)
        ok = jnp.concatenate([cur_ok, nxt_ok], axis=0)

        def residue(r, carry):
            rows = _res_rows(r, d)
            q, do, lse_v, dl_v = (jnp.concatenate([a[rows, :], b[rows, :]], axis=0) for a, b in zip(*qs))
            q2, do2 = _per_head(q), _per_head(do)
            s = _bdg(q2, _both(k_ref[rows, :]), 2, 2) * ATTN_SCALE
            pr = jnp.where(ok, jnp.exp(jnp.where(ok, s, NEG_BIG) - _head_cols(lse_v)), 0.0)
            dv = _bdg(pr, do2, 1, 1)
            ds = pr * (_bdg(do2, _both(v_ref[rows, :]), 2, 2) - _head_cols(dl_v))
            dk, dv = _join_heads(_bdg(ds, q2, 1, 1)) * ATTN_SCALE, _join_heads(dv)
            dk_ref[rows, :] = dk if first else dk + refs[10][rows, :]
            dv_ref[rows, :] = dv if first else dv + refs[11][rows, :]
            return carry

        lax.fori_loop(0, d, residue, 0)

    cur, nxt = _pair_specs(d, lambda n: n), _pair_specs(d, lambda n: jnp.minimum(n + 1, nb - 1))
    st, st_n = cur(0), nxt(0)
    in_specs = [cur(1), cur(2), cur(0), st, st, st, nxt(0), st_n, st_n, st_n] + ([] if first else [st, st])
    args = [proj] * 2 + [proj, dcat, lse, delta] * 2 + ([] if first else list(dkv_in))
    return tuple(pl.pallas_call(
        body, name=name, grid=(nb, ATTN_PAIRS // _pairs_per_step(d)), in_specs=in_specs, out_specs=[st, st],
        out_shape=[jax.ShapeDtypeStruct((s_len, ATTN_WIDTH), F32)] * 2,
        compiler_params=_params("arbitrary", "parallel"),
    )(*args))


def _dattn_forward(proj, tag):
    state = None
    for i, d in enumerate(DILATIONS):
        state = _sattn_fwd(proj, state, d, last=i == len(DILATIONS) - 1, name=f"{tag}_attn_d{d}")
    return state


def _dattn_backward(proj, o, lse, dcat, tag):
    delta = _dattn_delta(o, dcat, name=f"{tag}_attn_delta")
    dq, dkv = None, None
    for d in DILATIONS:
        dq = _sattn_dq(proj, dcat, lse, delta, dq, d, name=f"{tag}_attn_dq_d{d}")
        dkv = _sattn_dkv(proj, dcat, lse, delta, dkv, d, name=f"{tag}_attn_dkv_d{d}")
    return dq, dkv[0], dkv[1]


CONV_TC = 512
CONV_T = 256


def _shift_down(ext, k, t):
    return (pltpu.roll(ext, k, 0) if k else ext)[SUBLANES:SUBLANES + t]


def _conv_fwd(src, cb0, width, w8, *, name):
    s_len = src.shape[0]
    t, tc = _tile(s_len, (CONV_T,)), CONV_TC
    tpb = t // SUBLANES

    def body(x_ref, h_ref, w_ref, y_ref):
        halo = jnp.where(pl.program_id(0) > 0, h_ref[...], 0.0)
        ext = jnp.concatenate([halo, x_ref[...]], axis=0)
        w = w_ref[...]
        y = jnp.broadcast_to(w[CONV_K:CONV_K + 1], (t, tc))
        for k in range(CONV_K):
            y = y + w[k:k + 1] * _shift_down(ext, CONV_K - 1 - k, t)
        y_ref[...] = y

    return pl.pallas_call(
        body, name=name, grid=(s_len // t, width // tc),
        in_specs=[pl.BlockSpec((t, tc), lambda i, j: (i, cb0 + j)),
                  pl.BlockSpec((SUBLANES, tc), lambda i, j: (jnp.maximum(i * tpb - 1, 0), cb0 + j)),
                  pl.BlockSpec((SUBLANES, tc), lambda i, j: (0, j))],
        out_specs=pl.BlockSpec((t, tc), lambda i, j: (i, j)), out_shape=jax.ShapeDtypeStruct((s_len, width), F32),
        compiler_params=_params("parallel", "parallel"),
    )(src, src, w8)


def _conv_bwd(src, cb0, width, w8, dy, *, name):
    s_len = src.shape[0]
    t, tc = _tile(s_len, (CONV_T,)), CONV_TC
    tpb = t // SUBLANES
    ni = s_len // t

    def body(x_ref, h_ref, w_ref, dy_ref, dn_ref, dx_ref, dw_ref):
        i = pl.program_id(1)
        halo = jnp.where(i > 0, h_ref[...], 0.0)
        ext = jnp.concatenate([halo, x_ref[...]], axis=0)
        dyv = dy_ref[...]
        extn = jnp.concatenate([dyv, jnp.where(i < ni - 1, dn_ref[...], 0.0)], axis=0)
        w = w_ref[...]
        row = _rows((SUBLANES, tc))
        dx = jnp.zeros((t, tc), F32)
        dw = jnp.where(row == CONV_K, jnp.sum(dyv, axis=0, keepdims=True), 0.0)
        for k in range(CONV_K):
            up = CONV_K - 1 - k
            dx = dx + w[k:k + 1] * (pltpu.roll(extn, t + SUBLANES - up, 0) if up else extn)[:t]
            dw = dw + jnp.where(row == k, jnp.sum(dyv * _shift_down(ext, up, t), axis=0, keepdims=True), 0.0)
        dx_ref[...] = dx.astype(dx_ref.dtype)

        @pl.when(i == 0)
        def _():
            dw_ref[...] = jnp.zeros_like(dw_ref)

        dw_ref[...] += dw

    return pl.pallas_call(
        body, name=name, grid=(width // tc, ni),
        in_specs=[pl.BlockSpec((t, tc), lambda j, i: (i, cb0 + j)),
                  pl.BlockSpec((SUBLANES, tc), lambda j, i: (jnp.maximum(i * tpb - 1, 0), cb0 + j)),
                  pl.BlockSpec((SUBLANES, tc), lambda j, i: (0, j)),
                  pl.BlockSpec((t, tc), lambda j, i: (i, j)),
                  pl.BlockSpec((SUBLANES, tc), lambda j, i: (jnp.minimum((i + 1) * tpb, s_len // SUBLANES - 1), j))],
        out_specs=[pl.BlockSpec((t, tc), lambda j, i: (i, j)), pl.BlockSpec((SUBLANES, tc), lambda j, i: (0, j))],
        out_shape=[jax.ShapeDtypeStruct((s_len, width), MXU_DTYPE), jax.ShapeDtypeStruct((SUBLANES, width), F32)],
        compiler_params=_params("parallel", "arbitrary"),
    )(src, src, w8, dy, dy)


LRU_T = 256


def _lru_gates(xc, wa, wx, ba, bx, lam):
    r = _sigmoid(_bdot(xc, wa) + ba)
    i = _sigmoid(_bdot(xc, wx) + bx)
    log_a = (-LRU_C) * r * _softplus(-lam)
    return jnp.exp(log_a), jnp.sqrt(_neg_expm1(2.0 * log_a)) * i * xc


def _block_scan(a, b, reverse):
    t = a.shape[0]
    row = _rows(a.shape)
    s = 1
    while s < t:
        shift, ok = (t - s, row < t - s) if reverse else (s, row >= s)
        b = jnp.where(ok, a * pltpu.roll(b, shift, 0) + b, b)
        a = jnp.where(ok, a * pltpu.roll(a, shift, 0), a)
        s *= 2
    return a, b


def _lru_fwd(xc, proj, wa, wx, ba, bx, lam, *, name):
    s_len, w = xc.shape
    t = _tile(s_len, (LRU_T,))

    def body(xc_ref, gr_ref, wa_ref, wx_ref, ba_ref, bx_ref, lam_ref, h_ref, y_ref, carry):
        @pl.when(pl.program_id(0) == 0)
        def _():
            carry[...] = jnp.zeros_like(carry)

        a, b = _lru_gates(xc_ref[...], wa_ref[...], wx_ref[...], ba_ref[...], bx_ref[...], lam_ref[...])
        a_cum, h0 = _block_scan(a, b, False)
        h = h0 + a_cum * carry[0:1, :]
        h_ref[...] = h
        y_ref[...] = (h * _gelu(gr_ref[...])).astype(y_ref.dtype)
        carry[0:1, :] = h[t - 1:t, :]

    row = pl.BlockSpec((t, w), lambda i: (i, 0))
    mat = pl.BlockSpec((w, w), lambda i: (0, 0))
    vec = pl.BlockSpec((1, w), lambda i: (0, 0))
    return pl.pallas_call(
        body, name=name, grid=(s_len // t,),
        in_specs=[row, pl.BlockSpec((t, w), lambda i: (i, PROJ_AB_BLOCKS - 1)), mat, mat, vec, vec, vec],
        out_specs=[row, row], out_shape=[jax.ShapeDtypeStruct((s_len, w), F32), jax.ShapeDtypeStruct((s_len, w), MXU_DTYPE)],
        scratch_shapes=[pltpu.VMEM((SUBLANES, w), F32)], compiler_params=_params("arbitrary"),
    )(xc, proj, wa, wx, ba, bx, lam)


def _lru_bwd(xc, proj, hs, dcat, wa, wx, ba, bx, lam, *, name):
    s_len, w = xc.shape
    t = _tile(s_len, (LRU_T,))
    nb = s_len // t
    tpb = t // SUBLANES

    def body(xc_ref, gr_ref, h_ref, hp_ref, dy_ref, wa_ref, wx_ref, ba_ref, bx_ref, lam_ref,
             dxc_ref, dgr_ref, dwa_ref, dwx_ref, dba_ref, dbx_ref, dlam_ref, carry):
        step = pl.program_id(0)
        params = (wa_ref[...], wx_ref[...], ba_ref[...], bx_ref[...], lam_ref[...])

        @pl.when(step == 0)
        def _():
            carry[...] = jnp.zeros_like(carry)
            for r in (dwa_ref, dwx_ref, dba_ref, dbx_ref, dlam_ref):
                r[...] = jnp.zeros_like(r)

        (a, _), vjp = jax.vjp(_lru_gates, xc_ref[...], *params)
        gr, h, dy = gr_ref[...], h_ref[...], dy_ref[...]
        gel, gel_vjp = jax.vjp(_gelu, gr)
        dgr_ref[...] = gel_vjp(dy * h)[0].astype(dgr_ref.dtype)
        dh = dy * gel
        a_cum, g0 = _block_scan(a, a * dh, True)
        big_g = g0 + a_cum * carry[0:1, :]
        row = _rows((t, w))
        g = dh + jnp.where(row == t - 1, carry[0:1, :], pltpu.roll(big_g, t - 1, 0))
        carry[0:1, :] = big_g[0:1, :]
        h_last = jnp.where(step < nb - 1, hp_ref[SUBLANES - 1:SUBLANES, :], 0.0)
        h_prev = jnp.where(row == 0, h_last, pltpu.roll(h, 1, 0))
        dxc, dwa, dwx, dba, dbx, dlam = vjp((g * h_prev, g))
        dxc_ref[...] = dxc
        dwa_ref[...] += dwa
        dwx_ref[...] += dwx
        dba_ref[...] += dba
        dbx_ref[...] += dbx
        dlam_ref[...] += dlam

    rev = lambda i: nb - 1 - i
    row = pl.BlockSpec((t, w), lambda i: (rev(i), 0))
    mat = pl.BlockSpec((w, w), lambda i: (0, 0))
    vec = pl.BlockSpec((1, w), lambda i: (0, 0))
    return pl.pallas_call(
        body, name=name, grid=(nb,),
        in_specs=[row, pl.BlockSpec((t, w), lambda i: (rev(i), PROJ_AB_BLOCKS - 1)), row,
                  pl.BlockSpec((SUBLANES, w), lambda i: (jnp.maximum(rev(i) * tpb - 1, 0), 0)),
                  pl.BlockSpec((t, w), lambda i: (rev(i), 1)), mat, mat, vec, vec, vec],
        out_specs=[row, row, mat, mat, vec, vec, vec],
        out_shape=[jax.ShapeDtypeStruct((s_len, w), F32), jax.ShapeDtypeStruct((s_len, w), MXU_DTYPE)]
        + [jax.ShapeDtypeStruct((w, w), F32)] * 2 + [jax.ShapeDtypeStruct((1, w), F32)] * 3,
        scratch_shapes=[pltpu.VMEM((SUBLANES, w), F32)], compiler_params=_params("arbitrary"),
    )(xc, proj, hs, hs, dcat, wa, wx, ba, bx, lam)


XA_T = 256
XA_SCALE = XA_HEAD_DIM ** -0.5


def _xa_head(q, k, v):
    s = _bdot_nt(q, k) * XA_SCALE
    e = jnp.exp(s - jnp.max(s, axis=-1, keepdims=True))
    return _bdot(e / jnp.sum(e, axis=-1, keepdims=True), v)


def _xa_fwd(q, kv, *, name):
    s_len, d = q.shape
    n_mem = kv.shape[0]
    t = _tile(s_len, (XA_T,))

    def body(q_ref, k_ref, v_ref, o_ref):
        for h in range(XA_HEADS):
            sl = slice(h * XA_HEAD_DIM, (h + 1) * XA_HEAD_DIM)
            o_ref[:, sl] = _xa_head(q_ref[:, sl].astype(F32), k_ref[:, sl], v_ref[:, sl]).astype(o_ref.dtype)

    return pl.pallas_call(
        body, name=name, grid=(s_len // t,),
        in_specs=[pl.BlockSpec((t, d), lambda i: (i, 0)), pl.BlockSpec((n_mem, d), lambda i: (0, 0)),
                  pl.BlockSpec((n_mem, d), lambda i: (0, 1))],
        out_specs=pl.BlockSpec((t, d), lambda i: (i, 0)), out_shape=jax.ShapeDtypeStruct((s_len, d), MXU_DTYPE),
        compiler_params=_params("parallel"),
    )(q, kv, kv)


def _xa_bwd(q, kv, do, *, name):
    s_len, d = q.shape
    n_mem = kv.shape[0]
    t = _tile(s_len, (XA_T,))

    def body(q_ref, k_ref, v_ref, do_ref, dq_ref, dk_ref, dv_ref):
        @pl.when(pl.program_id(0) == 0)
        def _():
            dk_ref[...] = jnp.zeros_like(dk_ref)
            dv_ref[...] = jnp.zeros_like(dv_ref)

        for h in range(XA_HEADS):
            sl = slice(h * XA_HEAD_DIM, (h + 1) * XA_HEAD_DIM)
            _, vjp = jax.vjp(_xa_head, q_ref[:, sl].astype(F32), k_ref[:, sl], v_ref[:, sl])
            dq, dk, dv = vjp(do_ref[:, sl].astype(F32))
            dq_ref[:, sl] = dq.astype(dq_ref.dtype)
            dk_ref[:, sl] += dk
            dv_ref[:, sl] += dv

    row = pl.BlockSpec((t, d), lambda i: (i, 0))
    dq, dk, dv = pl.pallas_call(
        body, name=name, grid=(s_len // t,),
        in_specs=[row, pl.BlockSpec((n_mem, d), lambda i: (0, 0)), pl.BlockSpec((n_mem, d), lambda i: (0, 1)), row],
        out_specs=[row, pl.BlockSpec((n_mem, d), lambda i: (0, 0)), pl.BlockSpec((n_mem, d), lambda i: (0, 0))],
        out_shape=[jax.ShapeDtypeStruct((s_len, d), MXU_DTYPE)] + [jax.ShapeDtypeStruct((n_mem, d), F32)] * 2,
        compiler_params=_params("arbitrary"),
    )(q, kv, kv, do)
    return dq, jnp.concatenate([dk, dv], axis=1)


DN_Q_SCALE = DN_HEAD_DIM ** -0.5
L2_EPS = 1e-6


def _bdg(a, b, ca, cb):
    return lax.dot_general(a.astype(MXU_DTYPE), b.astype(MXU_DTYPE), (((ca,), (cb,)), ((0,), (0,))), preferred_element_type=F32)


@jax.custom_vjp
def _bmm(a, b):
    return _bdg(a, b, 2, 1)


_bmm.defvjp(lambda a, b: (_bdg(a, b, 2, 1), (a, b)), lambda r, g: (_bdg(g, r[1], 2, 2), _bdg(r[0], g, 1, 1)))


@jax.custom_vjp
def _bmm_nt(a, b):
    return _bdg(a, b, 2, 2)


_bmm_nt.defvjp(lambda a, b: (_bdg(a, b, 2, 2), (a, b)), lambda r, g: (_bdg(g, r[1], 2, 1), _bdg(g, r[0], 1, 1)))


@jax.custom_vjp
def _bmm_tn(a, b):
    return _bdg(a, b, 1, 1)


_bmm_tn.defvjp(lambda a, b: (_bdg(a, b, 1, 1), (a, b)), lambda r, g: (_bdg(r[1], g, 2, 2), _bdg(r[0], g, 2, 1)))


def _tri_inverse(n):
    eye = (lax.broadcasted_iota(jnp.int32, n.shape, 1) == lax.broadcasted_iota(jnp.int32, n.shape, 2)).astype(F32)
    inv, p = eye - n, n
    for _ in range(5):
        p = _bdg(p, p, 2, 1)
        inv = _bdg(inv, eye + p, 2, 1)
    return inv


@jax.custom_vjp
def _tri_solve2(n, r1, r2):
    t = _tri_inverse(n)
    return _bdg(t, r1, 2, 1), _bdg(t, r2, 2, 1)


def _tri_solve2_fwd(n, r1, r2):
    t = _tri_inverse(n)
    x1, x2 = _bdg(t, r1, 2, 1), _bdg(t, r2, 2, 1)
    return (x1, x2), (t, x1, x2)


def _tri_solve2_bwd(saved, cts):
    t, x1, x2 = saved
    d1, d2 = _bdg(t, cts[0], 1, 1), _bdg(t, cts[1], 1, 1)
    return -(_bdg(d1, x1, 2, 2) + _bdg(d2, x2, 2, 2)), d1, d2


_tri_solve2.defvjp(_tri_solve2_fwd, _tri_solve2_bwd)


def _dn_gates(ab, alog, dtb):
    return -jnp.exp(alog) * _softplus(ab + dtb), _sigmoid(ab)


def _dn_heads(cq, ck, cv, z, g, beta, onorm, state):
    h, c, _ = cq.shape
    l2 = lambda t: t * lax.rsqrt(jnp.sum(t * t, axis=-1, keepdims=True) + L2_EPS)
    q, k, v = l2(_silu(cq)) * DN_Q_SCALE, l2(_silu(ck)), _silu(cv)
    r, cc = lax.broadcasted_iota(jnp.int32, (h, c, c), 1), lax.broadcasted_iota(jnp.int32, (h, c, c), 2)
    tri, eye = r >= cc, r == cc
    g_row = jnp.sum(jnp.where(eye, g, 0.0), axis=1, keepdims=True)
    gcum_c = jnp.sum(jnp.where(tri, g_row, 0.0), axis=2, keepdims=True)
    gcum_r = jnp.sum(jnp.where(cc >= r, g, 0.0), axis=1, keepdims=True)
    decay = jnp.where(tri, jnp.exp(jnp.where(tri, gcum_c - gcum_r, 0.0)), 0.0)
    kb = k * beta
    n = jnp.where(r > cc, _bmm_nt(kb, k) * decay, 0.0)
    u, w = _tri_solve2(n, v * beta, kb * jnp.exp(gcum_c))
    v_new = u - _bmm(w, state)
    o = _bmm(q * jnp.exp(gcum_c), state) + _bmm(_bmm_nt(q, k) * decay, v_new)
    g_last = jnp.sum(g, axis=1, keepdims=True)
    new_state = state * jnp.exp(g_last) + _bmm_tn(k * jnp.exp(g_last - gcum_c), v_new)
    on = o * lax.rsqrt(jnp.mean(o * o, axis=-1, keepdims=True) + NORM_EPS) * onorm
    return on * _silu(z), new_state


def _dn_stack(ref, col0):
    return jnp.stack([ref[:, col0 + h * DN_HEAD_DIM:col0 + (h + 1) * DN_HEAD_DIM].astype(F32) for h in range(DN_HEADS)], axis=0)


def _dn_cols(block, col0):
    return jnp.stack([block[:, col0 + h:col0 + h + 1] for h in range(DN_HEADS)], axis=0)


def _dn_fwd(cqkv, proj, ab, alog, dtb, onorm, *, name):
    s_len = cqkv.shape[0]
    c, hd, w = DN_CHUNK, DN_HEAD_DIM, DN_WIDTH
    n_chunks = s_len // c

    def body(c_ref, z_ref, ab_ref, alog_ref, dtb_ref, on_ref, o_ref, st_ref, state):
        @pl.when(pl.program_id(0) == 0)
        def _():
            state[...] = jnp.zeros_like(state)

        g_all, beta_all = _dn_gates(ab_ref[...], alog_ref[...], dtb_ref[...])
        st = state[...]
        st_ref[0] = st
        out, new = _dn_heads(_dn_stack(c_ref, 0), _dn_stack(c_ref, w), _dn_stack(c_ref, 2 * w), _dn_stack(z_ref, 0),
                             _dn_cols(g_all, 0), _dn_cols(beta_all, DN_HEADS), on_ref[...], st)
        state[...] = new
        for h in range(DN_HEADS):
            o_ref[:, h * hd:(h + 1) * hd] = out[h].astype(o_ref.dtype)

    vec = pl.BlockSpec((1, LANES), lambda i: (0, 0))
    return pl.pallas_call(
        body, name=name, grid=(n_chunks,),
        in_specs=[pl.BlockSpec((c, 3 * w), lambda i: (i, 0)), pl.BlockSpec((c, w), lambda i: (i, 3)),
                  pl.BlockSpec((c, LANES), lambda i: (i, 0)), vec, vec, vec],
        out_specs=[pl.BlockSpec((c, w), lambda i: (i, 0)), pl.BlockSpec((1, DN_HEADS, hd, hd), lambda i: (i, 0, 0, 0))],
        out_shape=[jax.ShapeDtypeStruct((s_len, w), MXU_DTYPE), jax.ShapeDtypeStruct((n_chunks, DN_HEADS, hd, hd), F32)],
        scratch_shapes=[pltpu.VMEM((DN_HEADS, hd, hd), F32)], compiler_params=_params("arbitrary"),
    )(cqkv, proj, ab, alog, dtb, onorm)


def _dn_bwd(cqkv, proj, ab, alog, dtb, onorm, states, dout, *, name):
    s_len = cqkv.shape[0]
    c, hd, w = DN_CHUNK, DN_HEAD_DIM, DN_WIDTH
    n_chunks = s_len // c

    def body(c_ref, z_ref, ab_ref, alog_ref, dtb_ref, on_ref, st_ref, do_ref,
             dc_ref, dz_ref, dab_ref, dalog_ref, ddtb_ref, don_ref, dstate):
        @pl.when(pl.program_id(0) == 0)
        def _():
            dstate[...] = jnp.zeros_like(dstate)
            for r in (dalog_ref, ddtb_ref, don_ref):
                r[...] = jnp.zeros_like(r)

        (g_all, beta_all), gates_vjp = jax.vjp(_dn_gates, ab_ref[...], alog_ref[...], dtb_ref[...])
        _, vjp = jax.vjp(_dn_heads, _dn_stack(c_ref, 0), _dn_stack(c_ref, w), _dn_stack(c_ref, 2 * w), _dn_stack(z_ref, 0),
                         _dn_cols(g_all, 0), _dn_cols(beta_all, DN_HEADS), on_ref[...], st_ref[0])
        dcq, dck, dcv, dz, dg, dbeta, don, dst = vjp((_dn_stack(do_ref, 0), dstate[...]))
        dstate[...] = dst
        col = _cols((c, LANES))
        dg_all, dbeta_all = jnp.zeros((c, LANES), F32), jnp.zeros((c, LANES), F32)
        for h in range(DN_HEADS):
            sl = slice(h * hd, (h + 1) * hd)
            dc_ref[:, sl] = dcq[h]
            dc_ref[:, w + h * hd:w + (h + 1) * hd] = dck[h]
            dc_ref[:, 2 * w + h * hd:2 * w + (h + 1) * hd] = dcv[h]
            dz_ref[:, sl] = dz[h].astype(dz_ref.dtype)
            dg_all = dg_all + jnp.where(col == h, dg[h], 0.0)
            dbeta_all = dbeta_all + jnp.where(col == DN_HEADS + h, dbeta[h], 0.0)
        dab, dalog, ddtb = gates_vjp((dg_all, dbeta_all))
        dab_ref[...] = dab
        dalog_ref[...] += dalog
        ddtb_ref[...] += ddtb
        don_ref[...] += don

    rev = lambda i: n_chunks - 1 - i
    vec = pl.BlockSpec((1, LANES), lambda i: (0, 0))
    return pl.pallas_call(
        body, name=name, grid=(n_chunks,),
        in_specs=[pl.BlockSpec((c, 3 * w), lambda i: (rev(i), 0)), pl.BlockSpec((c, w), lambda i: (rev(i), 3)),
                  pl.BlockSpec((c, LANES), lambda i: (rev(i), 0)), vec, vec, vec,
                  pl.BlockSpec((1, DN_HEADS, hd, hd), lambda i: (rev(i), 0, 0, 0)), pl.BlockSpec((c, w), lambda i: (rev(i), 0))],
        out_specs=[pl.BlockSpec((c, 3 * w), lambda i: (rev(i), 0)), pl.BlockSpec((c, w), lambda i: (rev(i), 0)),
                   pl.BlockSpec((c, LANES), lambda i: (rev(i), 0)), vec, vec, vec],
        out_shape=[jax.ShapeDtypeStruct((s_len, 3 * w), F32), jax.ShapeDtypeStruct((s_len, w), MXU_DTYPE),
                   jax.ShapeDtypeStruct((s_len, LANES), F32)] + [jax.ShapeDtypeStruct((1, LANES), F32)] * 3,
        scratch_shapes=[pltpu.VMEM((DN_HEADS, hd, hd), F32)], compiler_params=_params("arbitrary"),
    )(cqkv, proj, ab, alog, dtb, onorm, states, dout)


def _final_loss(x, g, target, *, name):
    s, d = x.shape
    tm = _tile(s, (512, 256))

    def body(x_ref, g_ref, t_ref, loss_ref, dx_ref, dg_ref):
        @pl.when(pl.program_id(0) == 0)
        def _():
            loss_ref[...] = jnp.zeros_like(loss_ref)
            dg_ref[...] = jnp.zeros_like(dg_ref)

        xv, gv = x_ref[...], g_ref[...]
        r = lax.rsqrt(jnp.mean(xv * xv, axis=-1, keepdims=True) + NORM_EPS)
        xh = xv * r
        err = xh * gv - t_ref[...]
        loss_ref[...] += 0.5 * jnp.sum(jnp.mean(err * err, axis=-1, keepdims=True), axis=0, keepdims=True)
        dy = err * (1.0 / d)
        dxh = dy * gv
        dx_ref[...] = r * (dxh - xh * jnp.mean(dxh * xh, axis=-1, keepdims=True))
        dg_ref[...] += jnp.sum(dy * xh, axis=0, keepdims=True)

    row = pl.BlockSpec((tm, d), lambda i: (i, 0))
    vec = pl.BlockSpec((1, d), lambda i: (0, 0))
    return pl.pallas_call(
        body, name=name, grid=(s // tm,), in_specs=[row, vec, row],
        out_specs=[pl.BlockSpec((1, LANES), lambda i: (0, 0)), row, vec],
        out_shape=[jax.ShapeDtypeStruct((1, LANES), F32), jax.ShapeDtypeStruct((s, d), F32), jax.ShapeDtypeStruct((1, d), F32)],
        compiler_params=_params("arbitrary"),
    )(x, g.reshape(1, d), target)


def _adamw(w, g, m, v, *, name):
    shape = w.shape
    cols = shape[-1]
    rows = max(w.size // cols, 1)
    tr = _tile(rows, (512, 352, 256, 128, 64, 32, 16, 8))
    c1, c2 = 1.0 - ADAM_B1 ** ADAM_STEP, 1.0 - ADAM_B2 ** ADAM_STEP

    def body(w_ref, g_ref, m_ref, v_ref, d_ref, nm_ref, nv_ref):
        gv = g_ref[...]
        nm = ADAM_B1 * m_ref[...] + (1.0 - ADAM_B1) * gv
        nv = ADAM_B2 * v_ref[...] + (1.0 - ADAM_B2) * (gv * gv)
        d_ref[...] = -ADAM_LR * ((nm / c1) / (jnp.sqrt(nv / c2) + ADAM_EPS) + ADAM_WD * w_ref[...])
        nm_ref[...] = nm
        nv_ref[...] = nv

    blk = pl.BlockSpec((tr, cols), lambda i: (i, 0))
    outs = pl.pallas_call(
        body, name=name, grid=(rows // tr,), in_specs=[blk] * 4, out_specs=[blk] * 3,
        out_shape=[jax.ShapeDtypeStruct((rows, cols), F32)] * 3, compiler_params=_params("parallel"),
    )(*(t.reshape(rows, cols) for t in (w, g, m, v)))
    return tuple(t.reshape(shape) for t in outs)


def _block_diag(w):
    n, j, k = w.shape
    eye = jnp.eye(n, dtype=w.dtype)
    return (eye[:, None, :, None] * w[:, :, None, :]).reshape(n * j, n * k)


def _block_diag_part(m, n):
    j, k = m.shape[0] // n, m.shape[1] // n
    m4 = m.reshape(n, j, n, k)
    return jnp.stack([m4[i, :, i, :] for i in range(n)], axis=0)


DN_AB = 2 * DN_HEADS
DEPTH = 2


def _row(v, width=None):
    v = v.reshape(1, -1)
    return v if width is None else jnp.pad(v, ((0, 0), (0, width - v.shape[1])))


def _conv_w8(conv_w, bias=None):
    w8 = jnp.zeros((SUBLANES, conv_w.shape[1]), F32).at[:CONV_K].set(conv_w)
    return w8 if bias is None else w8.at[CONV_K].set(bias)


def _mixer_ab_fwd(x, w, tag):
    h, (proj,) = _norm_mm(x, w["mix_norm"][0], [w["ab_w_in"][0]], [F32], name=f"{tag}_in")
    o, lse = _dattn_forward(proj, tag)
    w8 = _conv_w8(w["lru_conv_w"][0], w["lru_conv_b"][0])
    xc = _conv_fwd(proj, PROJ_AB_BLOCKS - 2, LRU_WIDTH, w8, name=f"{tag}_conv")
    wa, wx = _block_diag(w["lru_w_a"][0]), _block_diag(w["lru_w_x"][0])
    vecs = (_row(w["lru_b_a"][0]), _row(w["lru_b_x"][0]), _row(w["lru_lambda"][0]))
    hs, y = _lru_fwd(xc, proj, wa, wx, *vecs, name=f"{tag}_lru")
    w_out = w["ab_w_out"][0]
    x2 = _mm(o, w_out[:ATTN_WIDTH], res=x, name=f"{tag}_out_attn")
    x2 = _mm(y, w_out[ATTN_WIDTH:], res=x2, name=f"{tag}_out_lru")
    return x2, (x, h, proj, o, lse, w8, xc, wa, wx, vecs, hs, y)


def _mixer_ab_bwd(saved, w, dy, tag):
    x, h, proj, o, lse, w8, xc, wa, wx, vecs, hs, y = saved
    w_out = w["ab_w_out"][0]
    dcat = _mm(dy, w_out, mode="nt", name=f"{tag}_dcat")
    dw_out = jnp.concatenate([_mm(o, dy, mode="tn", name=f"{tag}_dwout_attn"), _mm(y, dy, mode="tn", name=f"{tag}_dwout_lru")], axis=0)
    dq, dk, dv = _dattn_backward(proj, o, lse, dcat, tag)
    dxc, dgr, dwa, dwx, dba, dbx, dlam = _lru_bwd(xc, proj, hs, dcat, wa, wx, *vecs, name=f"{tag}_dlru")
    dxr, dw8 = _conv_bwd(proj, PROJ_AB_BLOCKS - 2, LRU_WIDTH, w8, dxc, name=f"{tag}_dconv")
    dproj = jnp.concatenate([t.astype(MXU_DTYPE) for t in (dq, dk, dv, dxr, dgr)], axis=1)
    dw_in = _mm(h, dproj, mode="tn", name=f"{tag}_dwin")
    dh = _mm(dproj, w["ab_w_in"][0], mode="nt", name=f"{tag}_dh")
    dx, dg = _rms_bwd(x, w["mix_norm"][0], dh, dy, name=f"{tag}_dnorm")
    grads = dict(mix_norm=dg[0], ab_w_in=dw_in, ab_w_out=dw_out, lru_conv_w=dw8[:CONV_K], lru_conv_b=dw8[CONV_K],
                 lru_w_a=_block_diag_part(dwa, LRU_BLOCKS), lru_b_a=dba[0], lru_w_x=_block_diag_part(dwx, LRU_BLOCKS),
                 lru_b_x=dbx[0], lru_lambda=dlam[0])
    return dx, grads


def _dn_split_w(w_in):
    return w_in[:, :4 * DN_WIDTH], jnp.pad(w_in[:, 4 * DN_WIDTH:], ((0, 0), (0, LANES - DN_AB)))


def _mixer_dn_fwd(x, w, tag):
    w_qkvz, w_ab = _dn_split_w(w["dn_w_in"][0])
    h, (proj, ab) = _norm_mm(x, w["mix_norm"][1], [w_qkvz, w_ab], [F32, F32], name=f"{tag}_in")
    w8 = _conv_w8(w["dn_conv_w"][0])
    cqkv = _conv_fwd(proj, 0, 3 * DN_WIDTH, w8, name=f"{tag}_conv")
    vecs = (_row(w["dn_a_log"][0], LANES), _row(w["dn_dt_bias"][0], LANES), _row(w["dn_o_norm"][0]))
    og, states = _dn_fwd(cqkv, proj, ab, *vecs, name=f"{tag}_dn")
    x2 = _mm(og, w["dn_w_out"][0], res=x, name=f"{tag}_out")
    return x2, (x, h, w_qkvz, w_ab, proj, ab, w8, cqkv, vecs, og, states)


def _mixer_dn_bwd(saved, w, dy, tag):
    x, h, w_qkvz, w_ab, proj, ab, w8, cqkv, vecs, og, states = saved
    dout = _mm(dy, w["dn_w_out"][0], mode="nt", name=f"{tag}_dout")
    dw_out = _mm(og, dy, mode="tn", name=f"{tag}_dwout")
    dcqkv, dz, dab, dalog, ddtb, don = _dn_bwd(cqkv, proj, ab, *vecs, states, dout, name=f"{tag}_ddn")
    dqkv, dw8 = _conv_bwd(proj, 0, 3 * DN_WIDTH, w8, dcqkv, name=f"{tag}_dconv")
    dproj = jnp.concatenate([dqkv.astype(MXU_DTYPE), dz.astype(MXU_DTYPE)], axis=1)
    dw_in = jnp.concatenate([_mm(h, dproj, mode="tn", name=f"{tag}_dwin"),
                             _mm(h, dab, mode="tn", name=f"{tag}_dwin_ab")[:, :DN_AB]], axis=1)
    dh = _mm(dproj, w_qkvz, mode="nt", name=f"{tag}_dh")
    dh = _mm(dab, w_ab, mode="nt", res=dh, name=f"{tag}_dh_ab")
    dx, dg = _rms_bwd(x, w["mix_norm"][1], dh, dy, name=f"{tag}_dnorm")
    grads = dict(mix_norm=dg[0], dn_w_in=dw_in, dn_w_out=dw_out, dn_conv_w=dw8[:CONV_K], dn_a_log=dalog[0, :DN_HEADS],
                 dn_dt_bias=ddtb[0, :DN_HEADS], dn_o_norm=don[0])
    return dx, grads


def _xa_layer_fwd(x, mem, w, layer, tag):
    hq, (q,) = _norm_mm(x, w["xa_norm"][layer], [w["xa_wq"][layer]], [MXU_DTYPE], name=f"{tag}_q")
    hm = _rms_fwd(mem, w["xa_mem_norm"][layer], name=f"{tag}_mem_norm")
    kv = _mm(hm, w["xa_wkv"][layer], name=f"{tag}_kv")
    oa = _xa_fwd(q, kv, name=f"{tag}_core")
    x2 = _mm(oa, w["xa_wo"][layer], res=x, name=f"{tag}_out")
    return x2, (x, hq, q, hm, kv, oa)


def _xa_layer_bwd(saved, mem, w, layer, dy, tag):
    x, hq, q, hm, kv, oa = saved
    do = _mm(dy, w["xa_wo"][layer], mode="nt", name=f"{tag}_do")
    dwo = _mm(oa, dy, mode="tn", name=f"{tag}_dwo")
    dq, dkv = _xa_bwd(q, kv, do, name=f"{tag}_dcore")
    dwq = _mm(hq, dq, mode="tn", name=f"{tag}_dwq")
    dhq = _mm(dq, w["xa_wq"][layer], mode="nt", name=f"{tag}_dhq")
    dx, dg = _rms_bwd(x, w["xa_norm"][layer], dhq, dy, name=f"{tag}_dnorm")
    dwkv = _mm(hm, dkv, mode="tn", name=f"{tag}_dwkv")
    dhm = _mm(dkv, w["xa_wkv"][layer], mode="nt", name=f"{tag}_dhm")
    _, dgm = _rms_bwd(mem, w["xa_mem_norm"][layer], dhm, jnp.zeros_like(mem), name=f"{tag}_dmem_norm")
    return dx, dict(xa_norm=dg[0], xa_mem_norm=dgm[0], xa_wq=dwq, xa_wkv=dwkv, xa_wo=dwo)


def _local_step(x, mem, target, w):
    saved = []
    for layer in range(DEPTH):
        t = f"l{layer}"
        x, s1 = _ffn_fwd(x, w["ffn1_norm"], w["ffn1_w_in"], w["ffn1_w_out"], layer, f"{t}_ffn1")
        x, s2 = (_mixer_ab_fwd if layer % 2 == 0 else _mixer_dn_fwd)(x, w, f"{t}_mix")
        x, s3 = _xa_layer_fwd(x, mem, w, layer, f"{t}_xa")
        x, s4 = _ffn_fwd(x, w["ffn2_norm"], w["ffn2_w_in"], w["ffn2_w_out"], layer, f"{t}_ffn2")
        saved.append((s1, s2, s3, s4))
    loss, dx, dgf = _final_loss(x, w["final_norm"], target, name="final_loss")
    per_layer = []
    for layer in reversed(range(DEPTH)):
        t = f"l{layer}"
        s1, s2, s3, s4 = saved[layer]
        g = {}
        dx, g["ffn2_norm"], g["ffn2_w_in"], g["ffn2_w_out"] = _ffn_bwd(
            s4, w["ffn2_norm"], w["ffn2_w_in"], w["ffn2_w_out"], layer, dx, f"{t}_ffn2")
        dx, gx = _xa_layer_bwd(s3, mem, w, layer, dx, f"{t}_xa")
        dx, gm = (_mixer_ab_bwd if layer % 2 == 0 else _mixer_dn_bwd)(s2, w, dx, f"{t}_mix")
        dx, g["ffn1_norm"], g["ffn1_w_in"], g["ffn1_w_out"] = _ffn_bwd(
            s1, w["ffn1_norm"], w["ffn1_w_in"], w["ffn1_w_out"], layer, dx, f"{t}_ffn1")
        per_layer.append({**g, **gx, **gm})
    per_layer.reverse()
    grads = {"final_norm": [dgf[0]]}
    for layer_grads in per_layer:
        for name, value in layer_grads.items():
            grads.setdefault(name, []).append(value)
    return loss, dx, grads


N_CHIPS = 4
WIRE_DTYPE = jnp.bfloat16
HBM_SPEC = pl.BlockSpec(memory_space=pltpu.HBM)
PACK_COLS = 1024


def _place():
    x, y, c = lax.axis_index("x"), lax.axis_index("y"), lax.axis_index("c")
    return x, y, c, [(1 - x, y), (x, 1 - y), (1 - x, 1 - y)]


def _remote(src, dst, sems, k, to):
    return pltpu.make_async_remote_copy(src_ref=src, dst_ref=dst, send_sem=sems[0].at[k], recv_sem=sems[1].at[k],
                                        device_id=to, device_id_type=MESH)


def _gather_weights(blocks, axes):
    n = len(blocks)
    split = [b.shape[1] % 32 == 0 for b in blocks]

    def full_shape(i):
        l, r, c = blocks[i].shape
        return (l, N_CHIPS * r, c) if axes[i] == 1 else (l, r, N_CHIPS * c)

    def body(*refs):
        ins, outs = refs[:n], refs[n:2 * n]
        send_sems, recv_sems = refs[2 * n:]
        x, y, c, chips = _place()
        sems = (send_sems, recv_sems)
        sibling = (x, y, 1 - c)

        def window(i, k, h):
            l, r, cc = blocks[i].shape
            r0, nr = (0, r) if h is None else (h * (r // 2), r // 2)
            if axes[i] == 1:
                return outs[i].at[:, pl.ds(k * r + r0, nr), :]
            return outs[i].at[:, pl.ds(r0, nr), pl.ds(k * cc, cc)]

        def mine(i, h):
            r = blocks[i].shape[1]
            return ins[i] if h is None else ins[i].at[:, pl.ds(h * (r // 2), r // 2), :]

        me = 2 * x + y
        first, passed = [], []
        for i in range(n):
            h = c if split[i] else None
            for j, chip in enumerate(chips):
                first.append(_remote(mine(i, h), window(i, me, h), sems, 3 * i + j, (*chip, c)))
        for cp in first:
            cp.start()
        for i in range(n):
            h = c if split[i] else None
            for j, (cx, cy) in enumerate(chips):
                got = window(i, 2 * cx + cy, h)
                _remote(got, got, sems, 3 * i + j, (cx, cy, c)).wait_recv()
                if split[i]:
                    passed.append(_remote(got, got, sems, 3 * (n + i) + j, sibling))
                    passed[-1].start()
        for i in range(n):
            if split[i]:
                for j, (cx, cy) in enumerate(chips):
                    got = window(i, 2 * cx + cy, 1 - c)
                    _remote(got, got, sems, 3 * (n + i) + j, sibling).wait_recv()
        for cp in first + passed:
            cp.wait_send()

    return pl.pallas_call(
        body, name="gather_weights", in_specs=[HBM_SPEC] * n, out_specs=[HBM_SPEC] * n,
        out_shape=[jax.ShapeDtypeStruct(full_shape(i), blocks[i].dtype) for i in range(n)],
        scratch_shapes=[pltpu.SemaphoreType.DMA((6 * n,)), pltpu.SemaphoreType.DMA((6 * n,))],
    )(*blocks)


def _allreduce_small(v):
    rows, cols = v.shape
    n_dev = 2 * N_CHIPS

    def body(v_ref, out_ref, all_ref, send_sems, recv_sems, local_sem):
        x, y, c, chips = _place()
        sems = (send_sems, recv_sems)
        me, sibling = (x, y, c), (x, y, 1 - c)
        slot = lambda px, py, pc: all_ref.at[pl.ds((4 * px + 2 * py + pc) * rows, rows), :]
        mine = pltpu.make_async_copy(v_ref, slot(*me), local_sem)
        mine.start()
        first = [_remote(v_ref, slot(*me), sems, 0, sibling)]
        first += [_remote(v_ref, slot(*me), sems, 1 + j, (*chip, c)) for j, chip in enumerate(chips)]
        for cp in first:
            cp.start()
        passed = [_remote(slot(*chip, c), slot(*chip, c), sems, 4 + j, sibling) for j, chip in enumerate(chips)]
        for j, chip in enumerate(chips):
            _remote(slot(*chip, c), slot(*chip, c), sems, 1 + j, me).wait_recv()
            passed[j].start()
        _remote(slot(*sibling), slot(*sibling), sems, 0, me).wait_recv()
        for j, chip in enumerate(chips):
            _remote(slot(*chip, 1 - c), slot(*chip, 1 - c), sems, 4 + j, me).wait_recv()
        for cp in first + passed:
            cp.wait_send()
        mine.wait()
        acc = all_ref[pl.ds(0, rows), :]
        for k in range(1, n_dev):
            acc = acc + all_ref[pl.ds(k * rows, rows), :]
        out_ref[...] = acc

    vmem = pl.BlockSpec(memory_space=pltpu.VMEM)
    return pl.pallas_call(
        body, name="allreduce_small", in_specs=[vmem], out_specs=vmem, out_shape=jax.ShapeDtypeStruct((rows, cols), F32),
        scratch_shapes=[pltpu.VMEM((n_dev * rows, cols), F32), pltpu.SemaphoreType.DMA((7,)), pltpu.SemaphoreType.DMA((7,)),
                        pltpu.SemaphoreType.DMA],
    )(v)


def _swap_other_half(g4):
    n, _, rows, cols = g4.shape

    def body(v_ref, out_ref, send_sems, recv_sems):
        x, y, c, _ = _place()
        cp = _remote(v_ref.at[:, 1 - c], out_ref, (send_sems, recv_sems), 0, (x, y, 1 - c))
        cp.start()
        cp.wait()

    return pl.pallas_call(
        body, name="reduce_swap", in_specs=[HBM_SPEC], out_specs=HBM_SPEC, out_shape=jax.ShapeDtypeStruct((n, rows, cols), g4.dtype),
        scratch_shapes=[pltpu.SemaphoreType.DMA((1,)), pltpu.SemaphoreType.DMA((1,))],
    )(g4)


def _add_kept_half(g4, got):
    n, _, rows, cols = g4.shape
    tr = _tile(rows, (256, 128, 64, 32, 16))
    nb = rows // tr

    def body(c_ref, a_ref, b_ref, o_ref):
        o_ref[...] = (a_ref[...] + b_ref[...]).astype(o_ref.dtype)

    return pl.pallas_call(
        body, name="reduce_sum_cores",
        grid_spec=pltpu.PrefetchScalarGridSpec(
            num_scalar_prefetch=1, grid=(n, nb),
            in_specs=[pl.BlockSpec((None, None, tr, cols), lambda k, i, c_ref: (k, c_ref[0], i, 0)),
                      pl.BlockSpec((None, tr, cols), lambda k, i, c_ref: (k, i, 0))],
            out_specs=pl.BlockSpec((None, tr, cols), lambda k, i, c_ref: (k, i, 0))),
        out_shape=jax.ShapeDtypeStruct((n, rows, cols), WIRE_DTYPE), compiler_params=_params("parallel", "parallel"),
    )(lax.axis_index("c").astype(jnp.int32).reshape(1), g4, got)


def _exchange_chips(v):
    def body(v_ref, out_ref, send_sems, recv_sems):
        x, y, c, chips = _place()
        sems = (send_sems, recv_sems)
        sends = [_remote(v_ref.at[2 * cx + cy], out_ref.at[j], sems, j, (cx, cy, c)) for j, (cx, cy) in enumerate(chips)]
        for cp in sends:
            cp.start()
        for j, (cx, cy) in enumerate(chips):
            _remote(v_ref.at[0], out_ref.at[j], sems, j, (cx, cy, c)).wait_recv()
        for cp in sends:
            cp.wait_send()

    return pl.pallas_call(
        body, name="exchange_chips", in_specs=[HBM_SPEC], out_specs=HBM_SPEC,
        out_shape=jax.ShapeDtypeStruct((N_CHIPS - 1,) + v.shape[1:], v.dtype),
        scratch_shapes=[pltpu.SemaphoreType.DMA((3,)), pltpu.SemaphoreType.DMA((3,))],
    )(v)


def _swap_sibling(v):
    def body(v_ref, out_ref, send_sems, recv_sems):
        x, y, c, _ = _place()
        cp = _remote(v_ref, out_ref, (send_sems, recv_sems), 0, (x, y, 1 - c))
        cp.start()
        cp.wait()

    return pl.pallas_call(
        body, name="share_halves", in_specs=[HBM_SPEC], out_specs=HBM_SPEC, out_shape=jax.ShapeDtypeStruct(v.shape, v.dtype),
        scratch_shapes=[pltpu.SemaphoreType.DMA((1,)), pltpu.SemaphoreType.DMA((1,))],
    )(v)


def _sum_chips(own4, parts):
    _, rows, cols = own4.shape
    tr = _tile(rows, (256, 128, 64, 32, 16))

    def body(me_ref, own_ref, p0_ref, p1_ref, p2_ref, o_ref):
        acc = own_ref[...].astype(F32)
        for r in (p0_ref, p1_ref, p2_ref):
            acc = acc + r[...].astype(F32)
        o_ref[...] = acc

    part = lambda j: pl.BlockSpec((None, tr, cols), lambda i, me_ref: (j, i, 0))
    chip = (2 * lax.axis_index("x") + lax.axis_index("y")).astype(jnp.int32).reshape(1)
    return pl.pallas_call(
        body, name="reduce_sum_chips",
        grid_spec=pltpu.PrefetchScalarGridSpec(
            num_scalar_prefetch=1, grid=(rows // tr,),
            in_specs=[pl.BlockSpec((None, tr, cols), lambda i, me_ref: (me_ref[0], i, 0)), part(0), part(1), part(2)],
            out_specs=pl.BlockSpec((tr, cols), lambda i, me_ref: (i, 0))),
        out_shape=jax.ShapeDtypeStruct((rows, cols), F32), compiler_params=_params("parallel"),
    )(chip, own4, parts, parts, parts)


def _reduce_grads(g4):
    chip_sum = _add_kept_half(g4, _swap_other_half(g4))
    half = _sum_chips(chip_sum, _exchange_chips(chip_sum))
    other = _swap_sibling(half)
    return jnp.where(lax.axis_index("c") == 0, jnp.stack([half, other]), jnp.stack([other, half]))


BIG = (("ffn1_w_in", 2), ("ffn1_w_out", 1), ("xa_wq", 1), ("xa_wkv", 2), ("xa_wo", 1), ("ffn2_w_in", 2), ("ffn2_w_out", 1),
       ("ab_w_in", 2), ("ab_w_out", 1), ("dn_w_in", 2), ("dn_w_out", 1))
TINY_SHARDED = (("lru_conv_w", 2), ("dn_conv_w", 2))
REPLICATED = ("ffn1_norm", "mix_norm", "xa_norm", "xa_mem_norm", "ffn2_norm", "lru_conv_b", "lru_w_a", "lru_b_a", "lru_w_x",
              "lru_b_x", "lru_lambda", "dn_a_log", "dn_dt_bias", "dn_o_norm", "final_norm")
WEIGHTS = ("ffn1_norm", "ffn1_w_in", "ffn1_w_out", "mix_norm", "xa_norm", "xa_mem_norm", "xa_wq", "xa_wkv", "xa_wo", "ffn2_norm",
           "ffn2_w_in", "ffn2_w_out", "ab_w_in", "lru_conv_w", "lru_conv_b", "lru_w_a", "lru_b_a", "lru_w_x", "lru_b_x",
           "lru_lambda", "ab_w_out", "dn_w_in", "dn_conv_w", "dn_a_log", "dn_dt_bias", "dn_o_norm", "dn_w_out", "final_norm")


def _pad_rows(flat, row_multiple):
    n = flat.shape[-1]
    per = row_multiple * PACK_COLS
    total = -(-n // per) * per
    flat = jnp.pad(flat, [(0, 0)] * (flat.ndim - 1) + [(0, total - n)])
    return flat.reshape(flat.shape[:-1] + (total // PACK_COLS, PACK_COLS))


def _lane_padded(shape):
    return shape[:-1] + (-(-shape[-1] // LANES) * LANES,)


def _pad_lanes(t):
    return jnp.pad(t, [(0, 0)] * (t.ndim - 1) + [(0, _lane_padded(t.shape)[-1] - t.shape[-1])])


def _gather_full(shards):
    named = BIG + TINY_SHARDED
    blocks = [_pad_lanes(shards[n]).astype(MXU_DTYPE) if (n, a) in BIG else shards[n] for n, a in named]
    chip = 2 * lax.axis_index("x") + lax.axis_index("y")
    gathered = _gather_weights(blocks, [a for _, a in named])
    out = {n: lax.dynamic_update_slice_in_dim(g, b, chip * b.shape[a], axis=a) for (n, a), g, b in zip(named, gathered, blocks)}
    for n, axis in BIG:
        width, padded = shards[n].shape[-1], _lane_padded(shards[n].shape)[-1]
        if padded != width:
            assert axis == 2
            out[n] = jnp.concatenate([out[n][..., k * padded:k * padded + width] for k in range(N_CHIPS)], axis=-1)
    return out


def _to_blocks(pieces, axis):
    width = pieces[0].shape[axis - 1] // N_CHIPS
    block = lambda p, k: _pad_lanes(lax.slice_in_dim(p, k * width, (k + 1) * width, axis=axis - 1)).reshape(-1)
    return jnp.stack([jnp.concatenate([block(p, k) for p in pieces]) for k in range(N_CHIPS)], axis=0)


def _pack_grads(grads):
    flat = jnp.concatenate([_to_blocks(grads[n], axis) for n, axis in BIG], axis=1)
    g = _pad_rows(flat, 2 * 256)
    return g.reshape(N_CHIPS, 2, g.shape[1] // 2, PACK_COLS)


def _unpack_grads(reduced, shards):
    flat = reduced.reshape(-1)
    out, off = {}, 0
    for n, _ in BIG:
        shape = shards[n].shape
        size = math.prod(_lane_padded(shape))
        out[n] = flat[off:off + size].reshape(_lane_padded(shape))[..., :shape[-1]]
        off += size
    return out


def _pack_small(grads, loss):
    parts = [p.reshape(-1) for n in REPLICATED + tuple(n for n, _ in TINY_SHARDED) for p in grads[n]] + [loss[0, :1]]
    flat = jnp.concatenate(parts)
    total = -(-flat.shape[0] // (SUBLANES * LANES)) * SUBLANES * LANES
    return jnp.pad(flat, (0, total - flat.shape[0])).reshape(-1, LANES)


def _unpack_small(summed, shards, chip):
    flat = summed.reshape(-1)
    out, off = {}, 0
    for n in REPLICATED:
        out[n] = flat[off:off + shards[n].size].reshape(shards[n].shape)
        off += shards[n].size
    for n, axis in TINY_SHARDED:
        width = shards[n].shape[axis]
        shape = shards[n].shape[:axis] + (N_CHIPS * width,) + shards[n].shape[axis + 1:]
        full = flat[off:off + N_CHIPS * shards[n].size].reshape(shape)
        out[n] = lax.dynamic_slice_in_dim(full, chip * width, width, axis=axis)
        off += N_CHIPS * shards[n].size
    return out, flat[off]


def kernel(x, mem, ffn1_norm, ffn1_w_in, ffn1_w_out, mix_norm, xa_norm, xa_mem_norm, xa_wq, xa_wkv, xa_wo, ffn2_norm,
           ffn2_w_in, ffn2_w_out, ab_w_in, lru_conv_w, lru_conv_b, lru_w_a, lru_b_a, lru_w_x, lru_b_x, lru_lambda,
           ab_w_out, dn_w_in, dn_conv_w, dn_a_log, dn_dt_bias, dn_o_norm, dn_w_out, final_norm, loss_target,
           m_ffn1_norm, m_ffn1_w_in, m_ffn1_w_out, m_mix_norm, m_xa_norm, m_xa_mem_norm, m_xa_wq, m_xa_wkv, m_xa_wo,
           m_ffn2_norm, m_ffn2_w_in, m_ffn2_w_out, m_ab_w_in, m_lru_conv_w, m_lru_conv_b, m_lru_w_a, m_lru_b_a,
           m_lru_w_x, m_lru_b_x, m_lru_lambda, m_ab_w_out, m_dn_w_in, m_dn_conv_w, m_dn_a_log, m_dn_dt_bias,
           m_dn_o_norm, m_dn_w_out, m_final_norm, v_ffn1_norm, v_ffn1_w_in, v_ffn1_w_out, v_mix_norm, v_xa_norm,
           v_xa_mem_norm, v_xa_wq, v_xa_wkv, v_xa_wo, v_ffn2_norm, v_ffn2_w_in, v_ffn2_w_out, v_ab_w_in,
           v_lru_conv_w, v_lru_conv_b, v_lru_w_a, v_lru_b_a, v_lru_w_x, v_lru_b_x, v_lru_lambda, v_ab_w_out,
           v_dn_w_in, v_dn_conv_w, v_dn_a_log, v_dn_dt_bias, v_dn_o_norm, v_dn_w_out, v_final_norm):
    given = dict(locals())
    shards = {n: given[n] for n in WEIGHTS}
    chip = 2 * lax.axis_index("x") + lax.axis_index("y")

    full = {n: shards[n] for n in REPLICATED}
    full.update(_gather_full(shards))
    loss, grad_x, grads = _local_step(x[0], mem[0], loss_target[0], full)

    small, loss_sum = _unpack_small(_allreduce_small(_pack_small(grads, loss)), shards, chip)
    grad = {**small, **_unpack_grads(_reduce_grads(_pack_grads(grads)), shards)}

    delta, new_m, new_v = {}, {}, {}
    for n in WEIGHTS:
        delta[n], new_m[n], new_v[n] = _adamw(shards[n], grad[n], given["m_" + n], given["v_" + n], name=f"adamw_{n}")
    return (loss_sum, grad_x[None], *[grad[n] for n in WEIGHTS], *[delta[n] for n in WEIGHTS],
            *[new_m[n] for n in WEIGHTS], *[new_v[n] for n in WEIGHTS])
```

```python
import math

import jax
import jax.numpy as jnp
from jax import lax
from jax.experimental import pallas as pl
from jax.experimental.pallas import tpu as pltpu

F32 = jnp.float32
MXU_DTYPE = jnp.bfloat16
VMEM_LIMIT_BYTES = 48 * 1024 * 1024
MM_BLOCK_BYTES = 8 * 1024 * 1024
LANES = 128
SUBLANES = 8

NORM_EPS = 1e-6
CONV_K = 4
ATTN_PAIRS = 4
ATTN_HEAD_DIM = 64
ATTN_WIDTH = 512
ATTN_BLOCK = 128
DILATIONS = (1, 4, 16)
LRU_WIDTH = 512
LRU_BLOCKS = 8
LRU_C = 8.0
DN_HEADS = 8
DN_HEAD_DIM = 128
DN_WIDTH = 1024
DN_CHUNK = 64
XA_HEADS = 4
XA_HEAD_DIM = 256
D_FF = 2816
ADAM_LR, ADAM_B1, ADAM_B2, ADAM_EPS, ADAM_WD, ADAM_STEP = 0.001, 0.9, 0.999, 1e-08, 0.01, 10

MESH = pl.DeviceIdType.MESH


def _tile(n, prefs):
    for p in prefs:
        if n % p == 0:
            return p
    return n


def _params(*sem):
    return pltpu.CompilerParams(dimension_semantics=sem, vmem_limit_bytes=VMEM_LIMIT_BYTES)


def _dg(a, b, dims, hi=False):
    if hi:
        return lax.dot_general(a, b, (dims, ((), ())), precision=lax.Precision.HIGHEST, preferred_element_type=F32)
    return lax.dot_general(a.astype(MXU_DTYPE), b.astype(MXU_DTYPE), (dims, ((), ())), preferred_element_type=F32)


@jax.custom_vjp
def _bdot(a, b):
    return _dg(a, b, ((1,), (0,)))


def _bdot_fwd(a, b):
    return _bdot(a, b), (a, b)


def _bdot_bwd(r, g):
    a, b = r
    return _dg(g, b, ((1,), (1,))).astype(a.dtype), _dg(a, g, ((0,), (0,))).astype(b.dtype)


_bdot.defvjp(_bdot_fwd, _bdot_bwd)


def _log1p(t):
    return jnp.where(t < 0.01, t * (1.0 - t * (0.5 - t * (1.0 / 3.0))), jnp.log(1.0 + t))


def _neg_expm1(y):
    series = -y * (1.0 + 0.5 * y * (1.0 + (1.0 / 3.0) * y * (1.0 + 0.25 * y)))
    return jnp.where(y > -0.01, series, 1.0 - jnp.exp(y))


def _softplus(x):
    return jnp.maximum(x, 0.0) + _log1p(jnp.exp(-jnp.abs(x)))


def _sigmoid(x):
    return 1.0 / (1.0 + jnp.exp(-x))


def _silu(x):
    return x * _sigmoid(x)


def _gelu(x):
    return 0.5 * x * (1.0 + jnp.tanh(0.7978845608028654 * (x + 0.044715 * x * x * x)))


def _rows(shape):
    return lax.broadcasted_iota(jnp.int32, shape, 0)


def _cols(shape):
    return lax.broadcasted_iota(jnp.int32, shape, 1)


def _mm(a, b, *, mode="nn", out_dtype=F32, res=None, scale=1.0, name):
    if mode == "nn":
        (m, k), (k2, n) = a.shape, b.shape
    elif mode == "nt":
        (m, k), (n, k2) = a.shape, b.shape
    else:
        (k, m), (k2, n) = a.shape, b.shape
    assert k == k2, (a.shape, b.shape, mode)
    if mode == "tn":
        tm, tn, tk = _tile(m, (1024, 512, 256, 128)), _tile(n, (1024, 512, 256, 128)), _tile(k, (2048, 1024, 512, 256))
    else:
        tm, tn = _tile(m, (512, 256, 128)), _tile(n, (1024, 512, 256, 128))
        tk = k if k * tn * 2 <= MM_BLOCK_BYTES else _tile(k, (1024, 512, 256, 128))
    nk = k // tk
    dims = {"nn": ((1,), (0,)), "nt": ((1,), (1,)), "tn": ((0,), (0,))}[mode]

    def body(*refs):
        a_ref, b_ref = refs[:2]
        r_ref = refs[2] if res is not None else None
        o_ref = refs[3 if res is not None else 2]

        def finish(r):
            if scale != 1.0:
                r = r * scale
            if res is not None:
                r = r_ref[...] + r
            o_ref[...] = r.astype(out_dtype)

        if nk == 1:
            finish(_dg(a_ref[...], b_ref[...], dims))
            return
        acc = refs[-1]
        kk = pl.program_id(2)

        @pl.when(kk == 0)
        def _():
            acc[...] = jnp.zeros_like(acc)

        acc[...] += _dg(a_ref[...], b_ref[...], dims)

        @pl.when(kk == nk - 1)
        def _():
            finish(acc[...])

    a_spec = pl.BlockSpec((tk, tm), lambda i, j, kk: (kk, i)) if mode == "tn" else pl.BlockSpec((tm, tk), lambda i, j, kk: (i, kk))
    b_spec = pl.BlockSpec((tn, tk), lambda i, j, kk: (j, kk)) if mode == "nt" else pl.BlockSpec((tk, tn), lambda i, j, kk: (kk, j))
    o_spec = pl.BlockSpec((tm, tn), lambda i, j, kk: (i, j))
    in_specs = [a_spec, b_spec] + ([o_spec] if res is not None else [])
    args = (a, b) + ((res,) if res is not None else ())
    return pl.pallas_call(
        body, name=name, grid=(m // tm, n // tn, nk), in_specs=in_specs, out_specs=o_spec,
        out_shape=jax.ShapeDtypeStruct((m, n), out_dtype), scratch_shapes=[pltpu.VMEM((tm, tn), F32)] if nk > 1 else [],
        compiler_params=_params("parallel", "parallel", "arbitrary"),
    )(*args)


def _rms_fwd(x, g, *, name):
    s, d = x.shape
    tm = _tile(s, (512, 256))

    def body(x_ref, g_ref, o_ref):
        xv = x_ref[...]
        r = lax.rsqrt(jnp.mean(xv * xv, axis=-1, keepdims=True) + NORM_EPS)
        o_ref[...] = (xv * r * g_ref[...]).astype(o_ref.dtype)

    return pl.pallas_call(
        body, name=name, grid=(s // tm,),
        in_specs=[pl.BlockSpec((tm, d), lambda i: (i, 0)), pl.BlockSpec((1, d), lambda i: (0, 0))],
        out_specs=pl.BlockSpec((tm, d), lambda i: (i, 0)), out_shape=jax.ShapeDtypeStruct((s, d), MXU_DTYPE),
        compiler_params=_params("parallel"),
    )(x, g.reshape(1, d))


def _norm_mm(x, g, ws, out_dtypes, *, name):
    s, d = x.shape
    tm = _tile(s, (512, 256))
    nw = len(ws)

    def body(*refs):
        x_ref, g_ref = refs[:2]
        h_ref = refs[2 + nw]
        xv = x_ref[...]
        r = lax.rsqrt(jnp.mean(xv * xv, axis=-1, keepdims=True) + NORM_EPS)
        h = (xv * r * g_ref[...]).astype(MXU_DTYPE)
        h_ref[...] = h
        for w_ref, o_ref in zip(refs[2:2 + nw], refs[3 + nw:]):
            o_ref[...] = _dg(h, w_ref[...], ((1,), (0,))).astype(o_ref.dtype)

    row = lambda w: pl.BlockSpec((tm, w), lambda i: (i, 0))
    outs = pl.pallas_call(
        body, name=name, grid=(s // tm,),
        in_specs=[row(d), pl.BlockSpec((1, d), lambda i: (0, 0))]
        + [pl.BlockSpec(w.shape, lambda i: (0, 0), pipeline_mode=RESIDENT) for w in ws],
        out_specs=[row(d)] + [row(w.shape[1]) for w in ws],
        out_shape=[jax.ShapeDtypeStruct((s, d), MXU_DTYPE)] + [jax.ShapeDtypeStruct((s, w.shape[1]), t) for w, t in zip(ws, out_dtypes)],
        compiler_params=_params("parallel"),
    )(x, g.reshape(1, d), *ws)
    return outs[0], outs[1:]


def _mm_rms_bwd(pairs, x, g, dres, *, name):
    s, d = x.shape
    tm = _tile(s, (512, 256))
    n = len(pairs)

    def body(*refs):
        x_ref, g_ref, dr_ref = refs[2 * n:2 * n + 3]
        dx_ref, dg_ref = refs[2 * n + 3:]
        dh = _dg(refs[0][...], refs[n][...], ((1,), (1,)))
        for a_ref, w_ref in zip(refs[1:n], refs[n + 1:2 * n]):
            dh = dh + _dg(a_ref[...], w_ref[...], ((1,), (1,)))
        xv, gv = x_ref[...], g_ref[...]
        r = lax.rsqrt(jnp.mean(xv * xv, axis=-1, keepdims=True) + NORM_EPS)
        xh = xv * r
        dxh = dh * gv
        dx_ref[...] = dr_ref[...] + r * (dxh - xh * jnp.mean(dxh * xh, axis=-1, keepdims=True))

        @pl.when(pl.program_id(0) == 0)
        def _():
            dg_ref[...] = jnp.zeros_like(dg_ref)

        dg_ref[...] += jnp.sum(dh * xh, axis=0, keepdims=True)

    row = lambda w: pl.BlockSpec((tm, w), lambda i: (i, 0))
    vec = pl.BlockSpec((1, d), lambda i: (0, 0))
    return pl.pallas_call(
        body, name=name, grid=(s // tm,),
        in_specs=[row(a.shape[1]) for a, _ in pairs]
        + [pl.BlockSpec(w.shape, lambda i: (0, 0), pipeline_mode=RESIDENT) for _, w in pairs] + [row(d), vec, row(d)],
        out_specs=[row(d), vec], out_shape=[jax.ShapeDtypeStruct((s, d), F32), jax.ShapeDtypeStruct((1, d), F32)],
        compiler_params=_params("arbitrary"),
    )(*[a for a, _ in pairs], *[w for _, w in pairs], x, g.reshape(1, d), dres)


def _rms_bwd(x, g, dh, dres, *, name):
    s, d = x.shape
    tm = _tile(s, (512, 256))

    def body(x_ref, g_ref, dh_ref, dr_ref, dx_ref, dg_ref):
        xv = x_ref[...]
        r = lax.rsqrt(jnp.mean(xv * xv, axis=-1, keepdims=True) + NORM_EPS)
        xh = xv * r
        dhv = dh_ref[...].astype(F32)
        dxh = dhv * g_ref[...]
        dx = r * (dxh - xh * jnp.mean(dxh * xh, axis=-1, keepdims=True))
        dx_ref[...] = dr_ref[...] + dx

        @pl.when(pl.program_id(0) == 0)
        def _():
            dg_ref[...] = jnp.zeros_like(dg_ref)

        dg_ref[...] += jnp.sum(dhv * xh, axis=0, keepdims=True)

    row = pl.BlockSpec((tm, d), lambda i: (i, 0))
    vec = pl.BlockSpec((1, d), lambda i: (0, 0))
    return pl.pallas_call(
        body, name=name, grid=(s // tm,), in_specs=[row, vec, row, row], out_specs=[row, vec],
        out_shape=[jax.ShapeDtypeStruct((s, d), F32), jax.ShapeDtypeStruct((1, d), F32)],
        compiler_params=_params("arbitrary"),
    )(x, g.reshape(1, d), dh, dres)


FFN_CHUNK = 256
FFN_TM = 256
RESIDENT = pl.Buffered(1)


def _ffn_fwd_call(x, g, w_in, w_out, layer, *, name):
    s, d = x.shape
    f = w_out.shape[1]
    tm = _tile(s, (FFN_TM,))

    def body(x_ref, g_ref, wi_ref, wo_ref, y_ref, u_ref, act_ref):
        xv = x_ref[...]
        r = lax.rsqrt(jnp.mean(xv * xv, axis=-1, keepdims=True) + NORM_EPS)
        h = (xv * r * g_ref[...]).astype(MXU_DTYPE)
        for j in range(f // FFN_CHUNK):
            lo, hi = j * FFN_CHUNK, (j + 1) * FFN_CHUNK
            gate = _dg(h, wi_ref[:, lo:hi], ((1,), (0,))).astype(MXU_DTYPE)
            up = _dg(h, wi_ref[:, f + lo:f + hi], ((1,), (0,))).astype(MXU_DTYPE)
            u_ref[:, lo:hi] = gate
            u_ref[:, f + lo:f + hi] = up
            act_ref[:, lo:hi] = (_silu(gate.astype(F32)) * up.astype(F32)).astype(MXU_DTYPE)
        y_ref[...] = xv + 0.5 * _dg(act_ref[...], wo_ref[...], ((1,), (0,)))

    row = lambda w: pl.BlockSpec((tm, w), lambda i: (i, 0))
    return pl.pallas_call(
        body, name=name, grid=(s // tm,),
        in_specs=[row(d), pl.BlockSpec((1, d), lambda i: (0, 0)),
                  pl.BlockSpec((None,) + w_in.shape[1:], lambda i: (layer, 0, 0), pipeline_mode=RESIDENT),
                  pl.BlockSpec((None,) + w_out.shape[1:], lambda i: (layer, 0, 0), pipeline_mode=RESIDENT)],
        out_specs=[row(d), row(2 * f)],
        out_shape=[jax.ShapeDtypeStruct((s, d), F32), jax.ShapeDtypeStruct((s, 2 * f), MXU_DTYPE)],
        scratch_shapes=[pltpu.VMEM((tm, f), MXU_DTYPE)], compiler_params=_params("parallel"),
    )(x, g.reshape(1, d), w_in, w_out)


def _ffn_bwd_call(x, g, u, dy, w_in, w_out, layer, *, name):
    s, d = x.shape
    f = w_out.shape[1]
    tm = _tile(s, (FFN_TM,))

    def body(x_ref, g_ref, u_ref, dy_ref, wi_ref, wo_ref, du_ref, dx_ref, dg_ref, h_ref):
        dyv = dy_ref[...]
        dyh = (0.5 * dyv).astype(MXU_DTYPE)
        for j in range(f // FFN_CHUNK):
            lo, hi = j * FFN_CHUNK, (j + 1) * FFN_CHUNK
            dact = _dg(dyh, wo_ref[lo:hi, :], ((1,), (1,)))
            gate, up = u_ref[:, lo:hi].astype(F32), u_ref[:, f + lo:f + hi].astype(F32)
            sg = _sigmoid(gate)
            du_ref[:, lo:hi] = (dact * up * sg * (1.0 + gate * (1.0 - sg))).astype(MXU_DTYPE)
            du_ref[:, f + lo:f + hi] = (dact * gate * sg).astype(MXU_DTYPE)
        dh = _dg(du_ref[...], wi_ref[...], ((1,), (1,)))
        xv, gv = x_ref[...], g_ref[...]
        r = lax.rsqrt(jnp.mean(xv * xv, axis=-1, keepdims=True) + NORM_EPS)
        xh = xv * r
        h_ref[...] = (xh * gv).astype(MXU_DTYPE)
        dxh = dh * gv
        dx_ref[...] = dyv + r * (dxh - xh * jnp.mean(dxh * xh, axis=-1, keepdims=True))

        @pl.when(pl.program_id(0) == 0)
        def _():
            dg_ref[...] = jnp.zeros_like(dg_ref)

        dg_ref[...] += jnp.sum(dh * xh, axis=0, keepdims=True)

    row = lambda w: pl.BlockSpec((tm, w), lambda i: (i, 0))
    vec = pl.BlockSpec((1, d), lambda i: (0, 0))
    return pl.pallas_call(
        body, name=name, grid=(s // tm,),
        in_specs=[row(d), vec, row(2 * f), row(d),
                  pl.BlockSpec((None,) + w_in.shape[1:], lambda i: (layer, 0, 0), pipeline_mode=RESIDENT),
                  pl.BlockSpec((None,) + w_out.shape[1:], lambda i: (layer, 0, 0), pipeline_mode=RESIDENT)],
        out_specs=[row(2 * f), row(d), vec, row(d)],
        out_shape=[jax.ShapeDtypeStruct((s, 2 * f), MXU_DTYPE), jax.ShapeDtypeStruct((s, d), F32),
                   jax.ShapeDtypeStruct((1, d), F32), jax.ShapeDtypeStruct((s, d), MXU_DTYPE)],
        compiler_params=_params("arbitrary"),
    )(x, g.reshape(1, d), u, dy, w_in, w_out)


def _ffn_dw_out(u, dy, *, name):
    s, f2 = u.shape
    f, d = f2 // 2, dy.shape[1]
    tf, tk = _tile(f, (1408, 256, 128)), _tile(s, (512, 256))
    nj = f // tf

    def body(g_ref, u_ref, dy_ref, o_ref):
        @pl.when(pl.program_id(1) == 0)
        def _():
            o_ref[...] = jnp.zeros_like(o_ref)

        act = _silu(g_ref[...].astype(F32)) * u_ref[...].astype(F32)
        o_ref[...] += _dg(act, 0.5 * dy_ref[...], ((0,), (0,)))

    return pl.pallas_call(
        body, name=name, grid=(nj, s // tk),
        in_specs=[pl.BlockSpec((tk, tf), lambda j, k: (k, j)), pl.BlockSpec((tk, tf), lambda j, k: (k, j + nj)),
                  pl.BlockSpec((tk, d), lambda j, k: (k, 0))],
        out_specs=pl.BlockSpec((tf, d), lambda j, k: (j, 0)), out_shape=jax.ShapeDtypeStruct((f, d), F32),
        compiler_params=_params("parallel", "arbitrary"),
    )(u, u, dy)


def _ffn_fwd(x, g, w_in, w_out, layer, tag):
    y, u = _ffn_fwd_call(x, g[layer], w_in, w_out, layer, name=f"{tag}_fwd")
    return y, (x, u)


def _ffn_bwd(saved, g, w_in, w_out, layer, dy, tag):
    x, u = saved
    du, dx, dg, h = _ffn_bwd_call(x, g[layer], u, dy, w_in, w_out, layer, name=f"{tag}_bwd")
    dw_out = _ffn_dw_out(u, dy, name=f"{tag}_dwout")
    dw_in = _mm(h, du, mode="tn", name=f"{tag}_dwin")
    return dx, dg[0], dw_in, dw_out


ATTN_SCALE = ATTN_HEAD_DIM ** -0.5
NEG_BIG = -1e30
PROJ_AB_BLOCKS = 5


def _first_head(n):
    return _cols((n, LANES)) < ATTN_HEAD_DIM


def _per_head(tiles):
    first = _first_head(tiles[0].shape[0])
    return jnp.stack([jnp.where(first == (h == 0), t, 0.0) for t in tiles for h in (0, 1)], axis=0)


def _both(tiles):
    return jnp.stack([t for t in tiles for _ in (0, 1)], axis=0)


def _head_cols(tiles):
    return jnp.stack([t[:, c0:c0 + 1] for t in tiles for c0 in (0, ATTN_HEAD_DIM)], axis=0)


def _join_heads(v):
    return [v[2 * u] + v[2 * u + 1] for u in range(v.shape[0] // 2)]


def _spread_heads(v):
    first = _first_head(v.shape[1])
    return [jnp.where(first, v[2 * u], v[2 * u + 1]) for u in range(v.shape[0] // 2)]


def _band_masks(has_prev):
    qi, kj = _rows((ATTN_BLOCK, ATTN_BLOCK)), _cols((ATTN_BLOCK, ATTN_BLOCK))
    return (kj >= qi) & has_prev, kj <= qi


def _dattn_delta(o, dcat, *, name):
    s_len = o.shape[0]
    tm = _tile(s_len, (512, 256))

    def body(o_ref, do_ref, out_ref):
        r, c = _rows((ATTN_WIDTH, ATTN_WIDTH)), _cols((ATTN_WIDTH, ATTN_WIDTH))
        ones_bd = (r // ATTN_HEAD_DIM == c // ATTN_HEAD_DIM).astype(F32)
        out_ref[...] = _dg(o_ref[...] * do_ref[...], ones_bd, ((1,), (0,)), hi=True)

    blk = pl.BlockSpec((tm, ATTN_WIDTH), lambda i: (i, 0))
    return pl.pallas_call(
        body, name=name, grid=(s_len // tm,), in_specs=[blk, blk], out_specs=blk,
        out_shape=jax.ShapeDtypeStruct((s_len, ATTN_WIDTH), F32), compiler_params=_params("parallel"),
    )(o, dcat)


ATTN_UNITS = 4


def _units(it, d):
    if d == 1:
        return [(pl.ds(0, ATTN_BLOCK), pl.ds(p * LANES, LANES)) for p in range(ATTN_UNITS)]
    return [(pl.ds(it * ATTN_UNITS + u, ATTN_BLOCK, stride=d), pl.ds(0, LANES)) for u in range(ATTN_UNITS)]


def _tiles(ref, units):
    return [ref[rows, lanes] for rows, lanes in units]


def _store_tiles(ref, units, tiles):
    for (rows, lanes), t in zip(units, tiles):
        ref[rows, lanes] = t


def _stacked(a_tiles, b_tiles):
    return [jnp.concatenate([a, b], axis=0) for a, b in zip(a_tiles, b_tiles)]


def _passes(d):
    return max(d // ATTN_UNITS, 1)


def _pairs_per_step(d):
    return ATTN_PAIRS if d == 1 else 1


def _pair_specs(d, n_of):
    pairs = _pairs_per_step(d)
    groups = ATTN_PAIRS // pairs
    return lambda c: pl.BlockSpec((ATTN_BLOCK * d, LANES * pairs), lambda n, p: (n_of(n), c * groups + p))


def _sattn_fwd(proj, state, d, *, last, name):
    s_len = proj.shape[0]
    nb = s_len // (ATTN_BLOCK * d)
    first = state is None
    n_out = 2 if last else 3

    def body(*refs):
        q_ref, kp_ref, kc_ref, vp_ref, vc_ref = refs[:5]
        st_refs = () if first else refs[5:8]
        out_refs = refs[-n_out:]
        ok = jnp.concatenate(_band_masks(pl.program_id(0) > 0), axis=1)

        def one_pass(it, carry):
            units = _units(it, d)
            kcat = _stacked(_tiles(kp_ref, units), _tiles(kc_ref, units))
            vcat = _stacked(_tiles(vp_ref, units), _tiles(vc_ref, units))
            s = jnp.where(ok, _bdg(_per_head(_tiles(q_ref, units)), _both(kcat), 2, 2) * ATTN_SCALE, NEG_BIG)
            m_new = jnp.max(s, axis=2, keepdims=True)
            if not first:
                m_old = _head_cols(_tiles(st_refs[0], units))
                m_new = jnp.maximum(m_old, m_new)
                alpha = jnp.exp(m_old - m_new)
            p = jnp.exp(s - m_new)
            l_new = jnp.sum(p, axis=2, keepdims=True)
            acc = _join_heads(_bdg(p, _per_head(vcat), 2, 1))
            if not first:
                l_new = l_new + _head_cols(_tiles(st_refs[1], units)) * alpha
                acc = [a + a_in * sp for a, a_in, sp in zip(acc, _tiles(st_refs[2], units), _spread_heads(alpha))]
            m_pair, l_pair = _spread_heads(m_new), _spread_heads(l_new)
            if last:
                _store_tiles(out_refs[0], units, [a / l for a, l in zip(acc, l_pair)])
                _store_tiles(out_refs[1], units, [m + jnp.log(l) for m, l in zip(m_pair, l_pair)])
            else:
                _store_tiles(out_refs[0], units, m_pair)
                _store_tiles(out_refs[1], units, l_pair)
                _store_tiles(out_refs[2], units, acc)
            return carry

        lax.fori_loop(0, _passes(d), one_pass, 0)

    cur, prev = _pair_specs(d, lambda n: n), _pair_specs(d, lambda n: jnp.maximum(n - 1, 0))
    st = cur(0)
    return tuple(pl.pallas_call(
        body, name=name, grid=(nb, ATTN_PAIRS // _pairs_per_step(d)),
        in_specs=[cur(0), prev(1), cur(1), prev(2), cur(2)] + ([] if first else [st] * 3),
        out_specs=[st] * n_out, out_shape=[jax.ShapeDtypeStruct((s_len, ATTN_WIDTH), F32)] * n_out,
        compiler_params=_params("arbitrary", "parallel"),
    )(*([proj] * 5 + ([] if first else list(state)))))


def _sattn_dq(proj, dcat, lse, delta, dq_in, d, *, name):
    s_len = proj.shape[0]
    nb = s_len // (ATTN_BLOCK * d)
    first = dq_in is None

    def body(*refs):
        q_ref, kp_ref, kc_ref, vp_ref, vc_ref, do_ref, lse_ref, dl_ref = refs[:8]
        dq_ref = refs[-1]
        ok = jnp.concatenate(_band_masks(pl.program_id(0) > 0), axis=1)

        def one_pass(it, carry):
            units = _units(it, d)
            kcat = _stacked(_tiles(kp_ref, units), _tiles(kc_ref, units))
            vcat = _stacked(_tiles(vp_ref, units), _tiles(vc_ref, units))
            s = _bdg(_per_head(_tiles(q_ref, units)), _both(kcat), 2, 2) * ATTN_SCALE
            pr = jnp.where(ok, jnp.exp(jnp.where(ok, s, NEG_BIG) - _head_cols(_tiles(lse_ref, units))), 0.0)
            ds = pr * (_bdg(_per_head(_tiles(do_ref, units)), _both(vcat), 2, 2) - _head_cols(_tiles(dl_ref, units)))
            acc = [t * ATTN_SCALE for t in _join_heads(_bdg(ds, _per_head(kcat), 2, 1))]
            if not first:
                acc = [a + b for a, b in zip(acc, _tiles(refs[8], units))]
            _store_tiles(dq_ref, units, acc)
            return carry

        lax.fori_loop(0, _passes(d), one_pass, 0)

    cur, prev = _pair_specs(d, lambda n: n), _pair_specs(d, lambda n: jnp.maximum(n - 1, 0))
    st = cur(0)
    return pl.pallas_call(
        body, name=name, grid=(nb, ATTN_PAIRS // _pairs_per_step(d)),
        in_specs=[cur(0), prev(1), cur(1), prev(2), cur(2), st, st, st] + ([] if first else [st]),
        out_specs=st, out_shape=jax.ShapeDtypeStruct((s_len, ATTN_WIDTH), F32),
        compiler_params=_params("arbitrary", "parallel"),
    )(*([proj] * 5 + [dcat, lse, delta] + ([] if first else [dq_in])))


def _sattn_dkv(proj, dcat, lse, delta, dkv_in, d, *, name):
    s_len = proj.shape[0]
    nb = s_len // (ATTN_BLOCK * d)
    first = dkv_in is None

    def body(*refs):
        k_ref, v_ref = refs[:2]
        qs = (refs[2:6], refs[6:10])
        dk_ref, dv_ref = refs[-2:]
        nxt_ok, cur_ok = _band_masks(pl.program_id(0) < nb - 1)
        ok = jnp.concatenate([cur_ok, nxt_ok], axis=0)

        def one_pass(it, carry):
            units = _units(it, d)
            q, do, lse_v, dl_v = (_stacked(_tiles(a, units), _tiles(b, units)) for a, b in zip(*qs))
            q2, do2 = _per_head(q), _per_head(do)
            s = _bdg(q2, _both(_tiles(k_ref, units)), 2, 2) * ATTN_SCALE
            pr = jnp.where(ok, jnp.exp(jnp.where(ok, s, NEG_BIG) - _head_cols(lse_v)), 0.0)
            dv = _join_heads(_bdg(pr, do2, 1, 1))
            ds = pr * (_bdg(do2, _both(_tiles(v_ref, units)), 2, 2) - _head_cols(dl_v))
            dk = [t * ATTN_SCALE for t in _join_heads(_bdg(ds, q2, 1, 1))]
            if not first:
                dk = [a + b for a, b in zip(dk, _tiles(refs[10], units))]
                dv = [a + b for a, b in zip(dv, _tiles(refs[11], units))]
            _store_tiles(dk_ref, units, dk)
            _store_tiles(dv_ref, units, dv)
            return carry

        lax.fori_loop(0, _passes(d), one_pass, 0)

    cur, nxt = _pair_specs(d, lambda n: n), _pair_specs(d, lambda n: jnp.minimum(n + 1, nb - 1))
    st, st_n = cur(0), nxt(0)
    in_specs = [cur(1), cur(2), cur(0), st, st, st, nxt(0), st_n, st_n, st_n] + ([] if first else [st, st])
    args = [proj] * 2 + [proj, dcat, lse, delta] * 2 + ([] if first else list(dkv_in))
    return tuple(pl.pallas_call(
        body, name=name, grid=(nb, ATTN_PAIRS // _pairs_per_step(d)), in_specs=in_specs, out_specs=[st, st],
        out_shape=[jax.ShapeDtypeStruct((s_len, ATTN_WIDTH), F32)] * 2,
        compiler_params=_params("arbitrary", "parallel"),
    )(*args))


def _dattn_forward(proj, tag):
    state = None
    for i, d in enumerate(DILATIONS):
        state = _sattn_fwd(proj, state, d, last=i == len(DILATIONS) - 1, name=f"{tag}_attn_d{d}")
    return state


def _dattn_backward(proj, o, lse, dcat, tag):
    delta = _dattn_delta(o, dcat, name=f"{tag}_attn_delta")
    dq, dkv = None, None
    for d in DILATIONS:
        dq = _sattn_dq(proj, dcat, lse, delta, dq, d, name=f"{tag}_attn_dq_d{d}")
        dkv = _sattn_dkv(proj, dcat, lse, delta, dkv, d, name=f"{tag}_attn_dkv_d{d}")
    return dq, dkv[0], dkv[1]


CONV_TC = 512
CONV_T = 256


def _shift_down(ext, k, t):
    return (pltpu.roll(ext, k, 0) if k else ext)[SUBLANES:SUBLANES + t]


def _conv_fwd(src, cb0, width, w8, *, name):
    s_len = src.shape[0]
    t, tc = _tile(s_len, (CONV_T,)), CONV_TC
    tpb = t // SUBLANES

    def body(x_ref, h_ref, w_ref, y_ref):
        halo = jnp.where(pl.program_id(0) > 0, h_ref[...], 0.0)
        ext = jnp.concatenate([halo, x_ref[...]], axis=0)
        w = w_ref[...]
        y = jnp.broadcast_to(w[CONV_K:CONV_K + 1], (t, tc))
        for k in range(CONV_K):
            y = y + w[k:k + 1] * _shift_down(ext, CONV_K - 1 - k, t)
        y_ref[...] = y

    return pl.pallas_call(
        body, name=name, grid=(s_len // t, width // tc),
        in_specs=[pl.BlockSpec((t, tc), lambda i, j: (i, cb0 + j)),
                  pl.BlockSpec((SUBLANES, tc), lambda i, j: (jnp.maximum(i * tpb - 1, 0), cb0 + j)),
                  pl.BlockSpec((SUBLANES, tc), lambda i, j: (0, j))],
        out_specs=pl.BlockSpec((t, tc), lambda i, j: (i, j)), out_shape=jax.ShapeDtypeStruct((s_len, width), F32),
        compiler_params=_params("parallel", "parallel"),
    )(src, src, w8)


def _conv_bwd(src, cb0, width, w8, dy, *, name):
    s_len = src.shape[0]
    t, tc = _tile(s_len, (CONV_T,)), CONV_TC
    tpb = t // SUBLANES
    ni = s_len // t

    def body(x_ref, h_ref, w_ref, dy_ref, dn_ref, dx_ref, dw_ref):
        i = pl.program_id(1)
        halo = jnp.where(i > 0, h_ref[...], 0.0)
        ext = jnp.concatenate([halo, x_ref[...]], axis=0)
        dyv = dy_ref[...]
        extn = jnp.concatenate([dyv, jnp.where(i < ni - 1, dn_ref[...], 0.0)], axis=0)
        w = w_ref[...]
        row = _rows((SUBLANES, tc))
        dx = jnp.zeros((t, tc), F32)
        dw = jnp.where(row == CONV_K, jnp.sum(dyv, axis=0, keepdims=True), 0.0)
        for k in range(CONV_K):
            up = CONV_K - 1 - k
            dx = dx + w[k:k + 1] * (pltpu.roll(extn, t + SUBLANES - up, 0) if up else extn)[:t]
            dw = dw + jnp.where(row == k, jnp.sum(dyv * _shift_down(ext, up, t), axis=0, keepdims=True), 0.0)
        dx_ref[...] = dx.astype(dx_ref.dtype)

        @pl.when(i == 0)
        def _():
            dw_ref[...] = jnp.zeros_like(dw_ref)

        dw_ref[...] += dw

    return pl.pallas_call(
        body, name=name, grid=(width // tc, ni),
        in_specs=[pl.BlockSpec((t, tc), lambda j, i: (i, cb0 + j)),
                  pl.BlockSpec((SUBLANES, tc), lambda j, i: (jnp.maximum(i * tpb - 1, 0), cb0 + j)),
                  pl.BlockSpec((SUBLANES, tc), lambda j, i: (0, j)),
                  pl.BlockSpec((t, tc), lambda j, i: (i, j)),
                  pl.BlockSpec((SUBLANES, tc), lambda j, i: (jnp.minimum((i + 1) * tpb, s_len // SUBLANES - 1), j))],
        out_specs=[pl.BlockSpec((t, tc), lambda j, i: (i, j)), pl.BlockSpec((SUBLANES, tc), lambda j, i: (0, j))],
        out_shape=[jax.ShapeDtypeStruct((s_len, width), MXU_DTYPE), jax.ShapeDtypeStruct((SUBLANES, width), F32)],
        compiler_params=_params("parallel", "arbitrary"),
    )(src, src, w8, dy, dy)


LRU_T = 256


def _lru_gates(xc, wa, wx, ba, bx, lam):
    r = _sigmoid(_bdot(xc, wa) + ba)
    i = _sigmoid(_bdot(xc, wx) + bx)
    log_a = (-LRU_C) * r * _softplus(-lam)
    return jnp.exp(log_a), jnp.sqrt(_neg_expm1(2.0 * log_a)) * i * xc


def _block_scan(a, b, reverse):
    t = a.shape[0]
    row = _rows(a.shape)
    s = 1
    while s < t:
        shift, ok = (t - s, row < t - s) if reverse else (s, row >= s)
        b = jnp.where(ok, a * pltpu.roll(b, shift, 0) + b, b)
        a = jnp.where(ok, a * pltpu.roll(a, shift, 0), a)
        s *= 2
    return a, b


def _lru_fwd(xc, proj, wa, wx, ba, bx, lam, *, name):
    s_len, w = xc.shape
    t = _tile(s_len, (LRU_T,))

    def body(xc_ref, gr_ref, wa_ref, wx_ref, ba_ref, bx_ref, lam_ref, h_ref, y_ref, carry):
        @pl.when(pl.program_id(0) == 0)
        def _():
            carry[...] = jnp.zeros_like(carry)

        a, b = _lru_gates(xc_ref[...], wa_ref[...], wx_ref[...], ba_ref[...], bx_ref[...], lam_ref[...])
        a_cum, h0 = _block_scan(a, b, False)
        h = h0 + a_cum * carry[0:1, :]
        h_ref[...] = h
        y_ref[...] = (h * _gelu(gr_ref[...])).astype(y_ref.dtype)
        carry[0:1, :] = h[t - 1:t, :]

    row = pl.BlockSpec((t, w), lambda i: (i, 0))
    mat = pl.BlockSpec((w, w), lambda i: (0, 0))
    vec = pl.BlockSpec((1, w), lambda i: (0, 0))
    return pl.pallas_call(
        body, name=name, grid=(s_len // t,),
        in_specs=[row, pl.BlockSpec((t, w), lambda i: (i, PROJ_AB_BLOCKS - 1)), mat, mat, vec, vec, vec],
        out_specs=[row, row], out_shape=[jax.ShapeDtypeStruct((s_len, w), F32), jax.ShapeDtypeStruct((s_len, w), MXU_DTYPE)],
        scratch_shapes=[pltpu.VMEM((SUBLANES, w), F32)], compiler_params=_params("arbitrary"),
    )(xc, proj, wa, wx, ba, bx, lam)


def _lru_bwd(xc, proj, hs, dcat, wa, wx, ba, bx, lam, *, name):
    s_len, w = xc.shape
    t = _tile(s_len, (LRU_T,))
    nb = s_len // t
    tpb = t // SUBLANES

    def body(xc_ref, gr_ref, h_ref, hp_ref, dy_ref, wa_ref, wx_ref, ba_ref, bx_ref, lam_ref,
             dxc_ref, dgr_ref, dwa_ref, dwx_ref, dba_ref, dbx_ref, dlam_ref, carry):
        step = pl.program_id(0)
        params = (wa_ref[...], wx_ref[...], ba_ref[...], bx_ref[...], lam_ref[...])

        @pl.when(step == 0)
        def _():
            carry[...] = jnp.zeros_like(carry)
            for r in (dwa_ref, dwx_ref, dba_ref, dbx_ref, dlam_ref):
                r[...] = jnp.zeros_like(r)

        (a, _), vjp = jax.vjp(_lru_gates, xc_ref[...], *params)
        gr, h, dy = gr_ref[...], h_ref[...], dy_ref[...]
        gel, gel_vjp = jax.vjp(_gelu, gr)
        dgr_ref[...] = gel_vjp(dy * h)[0].astype(dgr_ref.dtype)
        dh = dy * gel
        a_cum, g0 = _block_scan(a, a * dh, True)
        big_g = g0 + a_cum * carry[0:1, :]
        row = _rows((t, w))
        g = dh + jnp.where(row == t - 1, carry[0:1, :], pltpu.roll(big_g, t - 1, 0))
        carry[0:1, :] = big_g[0:1, :]
        h_last = jnp.where(step < nb - 1, hp_ref[SUBLANES - 1:SUBLANES, :], 0.0)
        h_prev = jnp.where(row == 0, h_last, pltpu.roll(h, 1, 0))
        dxc, dwa, dwx, dba, dbx, dlam = vjp((g * h_prev, g))
        dxc_ref[...] = dxc
        dwa_ref[...] += dwa
        dwx_ref[...] += dwx
        dba_ref[...] += dba
        dbx_ref[...] += dbx
        dlam_ref[...] += dlam

    rev = lambda i: nb - 1 - i
    row = pl.BlockSpec((t, w), lambda i: (rev(i), 0))
    mat = pl.BlockSpec((w, w), lambda i: (0, 0))
    vec = pl.BlockSpec((1, w), lambda i: (0, 0))
    return pl.pallas_call(
        body, name=name, grid=(nb,),
        in_specs=[row, pl.BlockSpec((t, w), lambda i: (rev(i), PROJ_AB_BLOCKS - 1)), row,
                  pl.BlockSpec((SUBLANES, w), lambda i: (jnp.maximum(rev(i) * tpb - 1, 0), 0)),
                  pl.BlockSpec((t, w), lambda i: (rev(i), 1)), mat, mat, vec, vec, vec],
        out_specs=[row, row, mat, mat, vec, vec, vec],
        out_shape=[jax.ShapeDtypeStruct((s_len, w), F32), jax.ShapeDtypeStruct((s_len, w), MXU_DTYPE)]
        + [jax.ShapeDtypeStruct((w, w), F32)] * 2 + [jax.ShapeDtypeStruct((1, w), F32)] * 3,
        scratch_shapes=[pltpu.VMEM((SUBLANES, w), F32)], compiler_params=_params("arbitrary"),
    )(xc, proj, hs, hs, dcat, wa, wx, ba, bx, lam)


XA_T = 256
XA_SCALE = XA_HEAD_DIM ** -0.5


def _xa_heads(q, k, v):
    s = _bmm_nt(q, k) * XA_SCALE
    e = jnp.exp(s - jnp.max(s, axis=-1, keepdims=True))
    return _bmm(e / jnp.sum(e, axis=-1, keepdims=True), v)


def _xa_stack(ref):
    return jnp.stack([ref[:, h * XA_HEAD_DIM:(h + 1) * XA_HEAD_DIM].astype(F32) for h in range(XA_HEADS)], axis=0)


def _xa_fwd(q, kv, *, name):
    s_len, d = q.shape
    n_mem = kv.shape[0]
    t = _tile(s_len, (XA_T,))

    def body(q_ref, k_ref, v_ref, o_ref):
        o = _xa_heads(_xa_stack(q_ref), _xa_stack(k_ref), _xa_stack(v_ref))
        for h in range(XA_HEADS):
            o_ref[:, h * XA_HEAD_DIM:(h + 1) * XA_HEAD_DIM] = o[h].astype(o_ref.dtype)

    return pl.pallas_call(
        body, name=name, grid=(s_len // t,),
        in_specs=[pl.BlockSpec((t, d), lambda i: (i, 0)), pl.BlockSpec((n_mem, d), lambda i: (0, 0)),
                  pl.BlockSpec((n_mem, d), lambda i: (0, 1))],
        out_specs=pl.BlockSpec((t, d), lambda i: (i, 0)), out_shape=jax.ShapeDtypeStruct((s_len, d), MXU_DTYPE),
        compiler_params=_params("parallel"),
    )(q, kv, kv)


def _xa_bwd(q, kv, do, *, name):
    s_len, d = q.shape
    n_mem = kv.shape[0]
    t = _tile(s_len, (XA_T,))

    def body(q_ref, k_ref, v_ref, do_ref, dq_ref, dk_ref, dv_ref):
        @pl.when(pl.program_id(0) == 0)
        def _():
            dk_ref[...] = jnp.zeros_like(dk_ref)
            dv_ref[...] = jnp.zeros_like(dv_ref)

        _, vjp = jax.vjp(_xa_heads, _xa_stack(q_ref), _xa_stack(k_ref), _xa_stack(v_ref))
        dq, dk, dv = vjp(_xa_stack(do_ref))
        for h in range(XA_HEADS):
            sl = slice(h * XA_HEAD_DIM, (h + 1) * XA_HEAD_DIM)
            dq_ref[:, sl] = dq[h].astype(dq_ref.dtype)
            dk_ref[:, sl] += dk[h]
            dv_ref[:, sl] += dv[h]

    row = pl.BlockSpec((t, d), lambda i: (i, 0))
    dq, dk, dv = pl.pallas_call(
        body, name=name, grid=(s_len // t,),
        in_specs=[row, pl.BlockSpec((n_mem, d), lambda i: (0, 0)), pl.BlockSpec((n_mem, d), lambda i: (0, 1)), row],
        out_specs=[row, pl.BlockSpec((n_mem, d), lambda i: (0, 0)), pl.BlockSpec((n_mem, d), lambda i: (0, 0))],
        out_shape=[jax.ShapeDtypeStruct((s_len, d), MXU_DTYPE)] + [jax.ShapeDtypeStruct((n_mem, d), F32)] * 2,
        compiler_params=_params("arbitrary"),
    )(q, kv, kv, do)
    return dq, jnp.concatenate([dk, dv], axis=1)


DN_Q_SCALE = DN_HEAD_DIM ** -0.5
L2_EPS = 1e-6


def _bdg(a, b, ca, cb):
    return lax.dot_general(a.astype(MXU_DTYPE), b.astype(MXU_DTYPE), (((ca,), (cb,)), ((0,), (0,))), preferred_element_type=F32)


@jax.custom_vjp
def _bmm(a, b):
    return _bdg(a, b, 2, 1)


_bmm.defvjp(lambda a, b: (_bdg(a, b, 2, 1), (a, b)), lambda r, g: (_bdg(g, r[1], 2, 2), _bdg(r[0], g, 1, 1)))


@jax.custom_vjp
def _bmm_nt(a, b):
    return _bdg(a, b, 2, 2)


_bmm_nt.defvjp(lambda a, b: (_bdg(a, b, 2, 2), (a, b)), lambda r, g: (_bdg(g, r[1], 2, 1), _bdg(g, r[0], 1, 1)))


@jax.custom_vjp
def _bmm_tn(a, b):
    return _bdg(a, b, 1, 1)


_bmm_tn.defvjp(lambda a, b: (_bdg(a, b, 1, 1), (a, b)), lambda r, g: (_bdg(r[1], g, 2, 2), _bdg(r[0], g, 2, 1)))


def _tri_inverse(n):
    eye = (lax.broadcasted_iota(jnp.int32, n.shape, 1) == lax.broadcasted_iota(jnp.int32, n.shape, 2)).astype(F32)
    inv, p = eye - n, n
    for _ in range(5):
        p = _bdg(p, p, 2, 1)
        inv = _bdg(inv, eye + p, 2, 1)
    return inv


@jax.custom_vjp
def _tri_solve2(n, r1, r2):
    t = _tri_inverse(n)
    return _bdg(t, r1, 2, 1), _bdg(t, r2, 2, 1)


def _tri_solve2_fwd(n, r1, r2):
    t = _tri_inverse(n)
    x1, x2 = _bdg(t, r1, 2, 1), _bdg(t, r2, 2, 1)
    return (x1, x2), (t, x1, x2)


def _tri_solve2_bwd(saved, cts):
    t, x1, x2 = saved
    d1, d2 = _bdg(t, cts[0], 1, 1), _bdg(t, cts[1], 1, 1)
    return -(_bdg(d1, x1, 2, 2) + _bdg(d2, x2, 2, 2)), d1, d2


_tri_solve2.defvjp(_tri_solve2_fwd, _tri_solve2_bwd)


def _dn_gates(ab, alog, dtb):
    return -jnp.exp(alog) * _softplus(ab + dtb), _sigmoid(ab)


def _dn_heads(cq, ck, cv, z, g, beta, onorm, state):
    h, c, _ = cq.shape
    l2 = lambda t: t * lax.rsqrt(jnp.sum(t * t, axis=-1, keepdims=True) + L2_EPS)
    q, k, v = l2(_silu(cq)) * DN_Q_SCALE, l2(_silu(ck)), _silu(cv)
    r, cc = lax.broadcasted_iota(jnp.int32, (h, c, c), 1), lax.broadcasted_iota(jnp.int32, (h, c, c), 2)
    tri, eye = r >= cc, r == cc
    g_row = jnp.sum(jnp.where(eye, g, 0.0), axis=1, keepdims=True)
    gcum_c = jnp.sum(jnp.where(tri, g_row, 0.0), axis=2, keepdims=True)
    gcum_r = jnp.sum(jnp.where(cc >= r, g, 0.0), axis=1, keepdims=True)
    decay = jnp.where(tri, jnp.exp(jnp.where(tri, gcum_c - gcum_r, 0.0)), 0.0)
    kb = k * beta
    n = jnp.where(r > cc, _bmm_nt(kb, k) * decay, 0.0)
    u, w = _tri_solve2(n, v * beta, kb * jnp.exp(gcum_c))
    v_new = u - _bmm(w, state)
    o = _bmm(q * jnp.exp(gcum_c), state) + _bmm(_bmm_nt(q, k) * decay, v_new)
    g_last = jnp.sum(g, axis=1, keepdims=True)
    new_state = state * jnp.exp(g_last) + _bmm_tn(k * jnp.exp(g_last - gcum_c), v_new)
    on = o * lax.rsqrt(jnp.mean(o * o, axis=-1, keepdims=True) + NORM_EPS) * onorm
    return on * _silu(z), new_state


def _dn_stack(ref, col0):
    return jnp.stack([ref[:, col0 + h * DN_HEAD_DIM:col0 + (h + 1) * DN_HEAD_DIM].astype(F32) for h in range(DN_HEADS)], axis=0)


def _dn_cols(block, col0):
    return jnp.stack([block[:, col0 + h:col0 + h + 1] for h in range(DN_HEADS)], axis=0)


def _dn_fwd(cqkv, proj, ab, alog, dtb, onorm, *, name):
    s_len = cqkv.shape[0]
    c, hd, w = DN_CHUNK, DN_HEAD_DIM, DN_WIDTH
    n_chunks = s_len // c

    def body(c_ref, z_ref, ab_ref, alog_ref, dtb_ref, on_ref, o_ref, st_ref, state):
        @pl.when(pl.program_id(0) == 0)
        def _():
            state[...] = jnp.zeros_like(state)

        g_all, beta_all = _dn_gates(ab_ref[...], alog_ref[...], dtb_ref[...])
        st = state[...]
        st_ref[0] = st
        out, new = _dn_heads(_dn_stack(c_ref, 0), _dn_stack(c_ref, w), _dn_stack(c_ref, 2 * w), _dn_stack(z_ref, 0),
                             _dn_cols(g_all, 0), _dn_cols(beta_all, DN_HEADS), on_ref[...], st)
        state[...] = new
        for h in range(DN_HEADS):
            o_ref[:, h * hd:(h + 1) * hd] = out[h].astype(o_ref.dtype)

    vec = pl.BlockSpec((1, LANES), lambda i: (0, 0))
    return pl.pallas_call(
        body, name=name, grid=(n_chunks,),
        in_specs=[pl.BlockSpec((c, 3 * w), lambda i: (i, 0)), pl.BlockSpec((c, w), lambda i: (i, 3)),
                  pl.BlockSpec((c, LANES), lambda i: (i, 0)), vec, vec, vec],
        out_specs=[pl.BlockSpec((c, w), lambda i: (i, 0)), pl.BlockSpec((1, DN_HEADS, hd, hd), lambda i: (i, 0, 0, 0))],
        out_shape=[jax.ShapeDtypeStruct((s_len, w), MXU_DTYPE), jax.ShapeDtypeStruct((n_chunks, DN_HEADS, hd, hd), F32)],
        scratch_shapes=[pltpu.VMEM((DN_HEADS, hd, hd), F32)], compiler_params=_params("arbitrary"),
    )(cqkv, proj, ab, alog, dtb, onorm)


def _dn_bwd(cqkv, proj, ab, alog, dtb, onorm, states, dout, *, name):
    s_len = cqkv.shape[0]
    c, hd, w = DN_CHUNK, DN_HEAD_DIM, DN_WIDTH
    n_chunks = s_len // c

    def body(c_ref, z_ref, ab_ref, alog_ref, dtb_ref, on_ref, st_ref, do_ref,
             dc_ref, dz_ref, dab_ref, dalog_ref, ddtb_ref, don_ref, dstate):
        @pl.when(pl.program_id(0) == 0)
        def _():
            dstate[...] = jnp.zeros_like(dstate)
            for r in (dalog_ref, ddtb_ref, don_ref):
                r[...] = jnp.zeros_like(r)

        (g_all, beta_all), gates_vjp = jax.vjp(_dn_gates, ab_ref[...], alog_ref[...], dtb_ref[...])
        _, vjp = jax.vjp(_dn_heads, _dn_stack(c_ref, 0), _dn_stack(c_ref, w), _dn_stack(c_ref, 2 * w), _dn_stack(z_ref, 0),
                         _dn_cols(g_all, 0), _dn_cols(beta_all, DN_HEADS), on_ref[...], st_ref[0])
        dcq, dck, dcv, dz, dg, dbeta, don, dst = vjp((_dn_stack(do_ref, 0), dstate[...]))
        dstate[...] = dst
        col = _cols((c, LANES))
        dg_all, dbeta_all = jnp.zeros((c, LANES), F32), jnp.zeros((c, LANES), F32)
        for h in range(DN_HEADS):
            sl = slice(h * hd, (h + 1) * hd)
            dc_ref[:, sl] = dcq[h]
            dc_ref[:, w + h * hd:w + (h + 1) * hd] = dck[h]
            dc_ref[:, 2 * w + h * hd:2 * w + (h + 1) * hd] = dcv[h]
            dz_ref[:, sl] = dz[h].astype(dz_ref.dtype)
            dg_all = dg_all + jnp.where(col == h, dg[h], 0.0)
            dbeta_all = dbeta_all + jnp.where(col == DN_HEADS + h, dbeta[h], 0.0)
        dab, dalog, ddtb = gates_vjp((dg_all, dbeta_all))
        dab_ref[...] = dab
        dalog_ref[...] += dalog
        ddtb_ref[...] += ddtb
        don_ref[...] += don

    rev = lambda i: n_chunks - 1 - i
    vec = pl.BlockSpec((1, LANES), lambda i: (0, 0))
    return pl.pallas_call(
        body, name=name, grid=(n_chunks,),
        in_specs=[pl.BlockSpec((c, 3 * w), lambda i: (rev(i), 0)), pl.BlockSpec((c, w), lambda i: (rev(i), 3)),
                  pl.BlockSpec((c, LANES), lambda i: (rev(i), 0)), vec, vec, vec,
                  pl.BlockSpec((1, DN_HEADS, hd, hd), lambda i: (rev(i), 0, 0, 0)), pl.BlockSpec((c, w), lambda i: (rev(i), 0))],
        out_specs=[pl.BlockSpec((c, 3 * w), lambda i: (rev(i), 0)), pl.BlockSpec((c, w), lambda i: (rev(i), 0)),
                   pl.BlockSpec((c, LANES), lambda i: (rev(i), 0)), vec, vec, vec],
        out_shape=[jax.ShapeDtypeStruct((s_len, 3 * w), F32), jax.ShapeDtypeStruct((s_len, w), MXU_DTYPE),
                   jax.ShapeDtypeStruct((s_len, LANES), F32)] + [jax.ShapeDtypeStruct((1, LANES), F32)] * 3,
        scratch_shapes=[pltpu.VMEM((DN_HEADS, hd, hd), F32)], compiler_params=_params("arbitrary"),
    )(cqkv, proj, ab, alog, dtb, onorm, states, dout)


def _final_loss(x, g, target, *, name):
    s, d = x.shape
    tm = _tile(s, (512, 256))

    def body(x_ref, g_ref, t_ref, loss_ref, dx_ref, dg_ref):
        @pl.when(pl.program_id(0) == 0)
        def _():
            loss_ref[...] = jnp.zeros_like(loss_ref)
            dg_ref[...] = jnp.zeros_like(dg_ref)

        xv, gv = x_ref[...], g_ref[...]
        r = lax.rsqrt(jnp.mean(xv * xv, axis=-1, keepdims=True) + NORM_EPS)
        xh = xv * r
        err = xh * gv - t_ref[...]
        loss_ref[...] += 0.5 * jnp.sum(jnp.mean(err * err, axis=-1, keepdims=True), axis=0, keepdims=True)
        dy = err * (1.0 / d)
        dxh = dy * gv
        dx_ref[...] = r * (dxh - xh * jnp.mean(dxh * xh, axis=-1, keepdims=True))
        dg_ref[...] += jnp.sum(dy * xh, axis=0, keepdims=True)

    row = pl.BlockSpec((tm, d), lambda i: (i, 0))
    vec = pl.BlockSpec((1, d), lambda i: (0, 0))
    return pl.pallas_call(
        body, name=name, grid=(s // tm,), in_specs=[row, vec, row],
        out_specs=[pl.BlockSpec((1, LANES), lambda i: (0, 0)), row, vec],
        out_shape=[jax.ShapeDtypeStruct((1, LANES), F32), jax.ShapeDtypeStruct((s, d), F32), jax.ShapeDtypeStruct((1, d), F32)],
        compiler_params=_params("arbitrary"),
    )(x, g.reshape(1, d), target)


def _adamw(w, g, m, v, *, name):
    shape = w.shape
    cols = shape[-1]
    rows = max(w.size // cols, 1)
    tr = _tile(rows, (512, 352, 256, 128, 64, 32, 16, 8))
    c1, c2 = 1.0 - ADAM_B1 ** ADAM_STEP, 1.0 - ADAM_B2 ** ADAM_STEP

    def body(w_ref, g_ref, m_ref, v_ref, d_ref, nm_ref, nv_ref):
        gv = g_ref[...]
        nm = ADAM_B1 * m_ref[...] + (1.0 - ADAM_B1) * gv
        nv = ADAM_B2 * v_ref[...] + (1.0 - ADAM_B2) * (gv * gv)
        d_ref[...] = -ADAM_LR * ((nm / c1) / (jnp.sqrt(nv / c2) + ADAM_EPS) + ADAM_WD * w_ref[...])
        nm_ref[...] = nm
        nv_ref[...] = nv

    blk = pl.BlockSpec((tr, cols), lambda i: (i, 0))
    outs = pl.pallas_call(
        body, name=name, grid=(rows // tr,), in_specs=[blk] * 4, out_specs=[blk] * 3,
        out_shape=[jax.ShapeDtypeStruct((rows, cols), F32)] * 3, compiler_params=_params("parallel"),
    )(*(t.reshape(rows, cols) for t in (w, g, m, v)))
    return tuple(t.reshape(shape) for t in outs)


def _block_diag(w):
    n, j, k = w.shape
    eye = jnp.eye(n, dtype=w.dtype)
    return (eye[:, None, :, None] * w[:, :, None, :]).reshape(n * j, n * k)


def _block_diag_part(m, n):
    j, k = m.shape[0] // n, m.shape[1] // n
    m4 = m.reshape(n, j, n, k)
    return jnp.stack([m4[i, :, i, :] for i in range(n)], axis=0)


DN_AB = 2 * DN_HEADS
DEPTH = 2


def _row(v, width=None):
    v = v.reshape(1, -1)
    return v if width is None else jnp.pad(v, ((0, 0), (0, width - v.shape[1])))


def _conv_w8(conv_w, bias=None):
    w8 = jnp.zeros((SUBLANES, conv_w.shape[1]), F32).at[:CONV_K].set(conv_w)
    return w8 if bias is None else w8.at[CONV_K].set(bias)


def _mixer_ab_fwd(x, w, tag):
    h, (proj,) = _norm_mm(x, w["mix_norm"][0], [w["ab_w_in"][0]], [F32], name=f"{tag}_in")
    o, lse = _dattn_forward(proj, tag)
    w8 = _conv_w8(w["lru_conv_w"][0], w["lru_conv_b"][0])
    xc = _conv_fwd(proj, PROJ_AB_BLOCKS - 2, LRU_WIDTH, w8, name=f"{tag}_conv")
    wa, wx = _block_diag(w["lru_w_a"][0]), _block_diag(w["lru_w_x"][0])
    vecs = (_row(w["lru_b_a"][0]), _row(w["lru_b_x"][0]), _row(w["lru_lambda"][0]))
    hs, y = _lru_fwd(xc, proj, wa, wx, *vecs, name=f"{tag}_lru")
    w_out = w["ab_w_out"][0]
    x2 = _mm(o, w_out[:ATTN_WIDTH], res=x, name=f"{tag}_out_attn")
    x2 = _mm(y, w_out[ATTN_WIDTH:], res=x2, name=f"{tag}_out_lru")
    return x2, (x, h, proj, o, lse, w8, xc, wa, wx, vecs, hs, y)


def _mixer_ab_bwd(saved, w, dy, tag):
    x, h, proj, o, lse, w8, xc, wa, wx, vecs, hs, y = saved
    w_out = w["ab_w_out"][0]
    dcat = _mm(dy, w_out, mode="nt", name=f"{tag}_dcat")
    dw_out = jnp.concatenate([_mm(o, dy, mode="tn", name=f"{tag}_dwout_attn"), _mm(y, dy, mode="tn", name=f"{tag}_dwout_lru")], axis=0)
    dq, dk, dv = _dattn_backward(proj, o, lse, dcat, tag)
    dxc, dgr, dwa, dwx, dba, dbx, dlam = _lru_bwd(xc, proj, hs, dcat, wa, wx, *vecs, name=f"{tag}_dlru")
    dxr, dw8 = _conv_bwd(proj, PROJ_AB_BLOCKS - 2, LRU_WIDTH, w8, dxc, name=f"{tag}_dconv")
    dproj = jnp.concatenate([t.astype(MXU_DTYPE) for t in (dq, dk, dv, dxr, dgr)], axis=1)
    dw_in = _mm(h, dproj, mode="tn", name=f"{tag}_dwin")
    dx, dg = _mm_rms_bwd([(dproj, w["ab_w_in"][0])], x, w["mix_norm"][0], dy, name=f"{tag}_dh")
    grads = dict(mix_norm=dg[0], ab_w_in=dw_in, ab_w_out=dw_out, lru_conv_w=dw8[:CONV_K], lru_conv_b=dw8[CONV_K],
                 lru_w_a=_block_diag_part(dwa, LRU_BLOCKS), lru_b_a=dba[0], lru_w_x=_block_diag_part(dwx, LRU_BLOCKS),
                 lru_b_x=dbx[0], lru_lambda=dlam[0])
    return dx, grads


def _dn_split_w(w_in):
    return w_in[:, :4 * DN_WIDTH], jnp.pad(w_in[:, 4 * DN_WIDTH:], ((0, 0), (0, LANES - DN_AB)))


def _mixer_dn_fwd(x, w, tag):
    w_qkvz, w_ab = _dn_split_w(w["dn_w_in"][0])
    h, (proj, ab) = _norm_mm(x, w["mix_norm"][1], [w_qkvz, w_ab], [F32, F32], name=f"{tag}_in")
    w8 = _conv_w8(w["dn_conv_w"][0])
    cqkv = _conv_fwd(proj, 0, 3 * DN_WIDTH, w8, name=f"{tag}_conv")
    vecs = (_row(w["dn_a_log"][0], LANES), _row(w["dn_dt_bias"][0], LANES), _row(w["dn_o_norm"][0]))
    og, states = _dn_fwd(cqkv, proj, ab, *vecs, name=f"{tag}_dn")
    x2 = _mm(og, w["dn_w_out"][0], res=x, name=f"{tag}_out")
    return x2, (x, h, w_qkvz, w_ab, proj, ab, w8, cqkv, vecs, og, states)


def _mixer_dn_bwd(saved, w, dy, tag):
    x, h, w_qkvz, w_ab, proj, ab, w8, cqkv, vecs, og, states = saved
    dout = _mm(dy, w["dn_w_out"][0], mode="nt", name=f"{tag}_dout")
    dw_out = _mm(og, dy, mode="tn", name=f"{tag}_dwout")
    dcqkv, dz, dab, dalog, ddtb, don = _dn_bwd(cqkv, proj, ab, *vecs, states, dout, name=f"{tag}_ddn")
    dqkv, dw8 = _conv_bwd(proj, 0, 3 * DN_WIDTH, w8, dcqkv, name=f"{tag}_dconv")
    dproj = jnp.concatenate([dqkv.astype(MXU_DTYPE), dz.astype(MXU_DTYPE)], axis=1)
    dw_in = jnp.concatenate([_mm(h, dproj, mode="tn", name=f"{tag}_dwin"),
                             _mm(h, dab, mode="tn", name=f"{tag}_dwin_ab")[:, :DN_AB]], axis=1)
    dx, dg = _mm_rms_bwd([(dproj, w_qkvz), (dab, w_ab)], x, w["mix_norm"][1], dy, name=f"{tag}_dh")
    grads = dict(mix_norm=dg[0], dn_w_in=dw_in, dn_w_out=dw_out, dn_conv_w=dw8[:CONV_K], dn_a_log=dalog[0, :DN_HEADS],
                 dn_dt_bias=ddtb[0, :DN_HEADS], dn_o_norm=don[0])
    return dx, grads


def _xa_layer_fwd(x, mem, w, layer, tag):
    hq, (q,) = _norm_mm(x, w["xa_norm"][layer], [w["xa_wq"][layer]], [MXU_DTYPE], name=f"{tag}_q")
    hm = _rms_fwd(mem, w["xa_mem_norm"][layer], name=f"{tag}_mem_norm")
    kv = _mm(hm, w["xa_wkv"][layer], name=f"{tag}_kv")
    oa = _xa_fwd(q, kv, name=f"{tag}_core")
    x2 = _mm(oa, w["xa_wo"][layer], res=x, name=f"{tag}_out")
    return x2, (x, hq, q, hm, kv, oa)


def _xa_layer_bwd(saved, mem, w, layer, dy, tag):
    x, hq, q, hm, kv, oa = saved
    do = _mm(dy, w["xa_wo"][layer], mode="nt", name=f"{tag}_do")
    dwo = _mm(oa, dy, mode="tn", name=f"{tag}_dwo")
    dq, dkv = _xa_bwd(q, kv, do, name=f"{tag}_dcore")
    dwq = _mm(hq, dq, mode="tn", name=f"{tag}_dwq")
    dx, dg = _mm_rms_bwd([(dq, w["xa_wq"][layer])], x, w["xa_norm"][layer], dy, name=f"{tag}_dhq")
    dwkv = _mm(hm, dkv, mode="tn", name=f"{tag}_dwkv")
    dhm = _mm(dkv, w["xa_wkv"][layer], mode="nt", name=f"{tag}_dhm")
    _, dgm = _rms_bwd(mem, w["xa_mem_norm"][layer], dhm, jnp.zeros_like(mem), name=f"{tag}_dmem_norm")
    return dx, dict(xa_norm=dg[0], xa_mem_norm=dgm[0], xa_wq=dwq, xa_wkv=dwkv, xa_wo=dwo)


def _local_step(x, mem, target, w):
    saved = []
    for layer in range(DEPTH):
        t = f"l{layer}"
        x, s1 = _ffn_fwd(x, w["ffn1_norm"], w["ffn1_w_in"], w["ffn1_w_out"], layer, f"{t}_ffn1")
        x, s2 = (_mixer_ab_fwd if layer % 2 == 0 else _mixer_dn_fwd)(x, w, f"{t}_mix")
        x, s3 = _xa_layer_fwd(x, mem, w, layer, f"{t}_xa")
        x, s4 = _ffn_fwd(x, w["ffn2_norm"], w["ffn2_w_in"], w["ffn2_w_out"], layer, f"{t}_ffn2")
        saved.append((s1, s2, s3, s4))
    loss, dx, dgf = _final_loss(x, w["final_norm"], target, name="final_loss")
    per_layer = []
    for layer in reversed(range(DEPTH)):
        t = f"l{layer}"
        s1, s2, s3, s4 = saved[layer]
        g = {}
        dx, g["ffn2_norm"], g["ffn2_w_in"], g["ffn2_w_out"] = _ffn_bwd(
            s4, w["ffn2_norm"], w["ffn2_w_in"], w["ffn2_w_out"], layer, dx, f"{t}_ffn2")
        dx, gx = _xa_layer_bwd(s3, mem, w, layer, dx, f"{t}_xa")
        dx, gm = (_mixer_ab_bwd if layer % 2 == 0 else _mixer_dn_bwd)(s2, w, dx, f"{t}_mix")
        dx, g["ffn1_norm"], g["ffn1_w_in"], g["ffn1_w_out"] = _ffn_bwd(
            s1, w["ffn1_norm"], w["ffn1_w_in"], w["ffn1_w_out"], layer, dx, f"{t}_ffn1")
        per_layer.append({**g, **gx, **gm})
    per_layer.reverse()
    grads = {"final_norm": [dgf[0]]}
    for layer_grads in per_layer:
        for name, value in layer_grads.items():
            grads.setdefault(name, []).append(value)
    return loss, dx, grads


N_CHIPS = 4
WIRE_DTYPE = jnp.bfloat16
HBM_SPEC = pl.BlockSpec(memory_space=pltpu.HBM)
PACK_COLS = 1024


def _place():
    x, y, c = lax.axis_index("x"), lax.axis_index("y"), lax.axis_index("c")
    return x, y, c, [(1 - x, y), (x, 1 - y), (1 - x, 1 - y)]


def _remote(src, dst, sems, k, to):
    return pltpu.make_async_remote_copy(src_ref=src, dst_ref=dst, send_sem=sems[0].at[k], recv_sem=sems[1].at[k],
                                        device_id=to, device_id_type=MESH)


def _gather_weights(blocks, axes):
    n = len(blocks)
    split = [b.shape[1] % 32 == 0 for b in blocks]

    def full_shape(i):
        l, r, c = blocks[i].shape
        return (l, N_CHIPS * r, c) if axes[i] == 1 else (l, r, N_CHIPS * c)

    def body(*refs):
        ins, outs = refs[:n], refs[n:2 * n]
        send_sems, recv_sems = refs[2 * n:]
        x, y, c, chips = _place()
        sems = (send_sems, recv_sems)
        sibling = (x, y, 1 - c)

        def window(i, k, h):
            l, r, cc = blocks[i].shape
            r0, nr = (0, r) if h is None else (h * (r // 2), r // 2)
            if axes[i] == 1:
                return outs[i].at[:, pl.ds(k * r + r0, nr), :]
            return outs[i].at[:, pl.ds(r0, nr), pl.ds(k * cc, cc)]

        def mine(i, h):
            r = blocks[i].shape[1]
            return ins[i] if h is None else ins[i].at[:, pl.ds(h * (r // 2), r // 2), :]

        me = 2 * x + y
        first, passed = [], []
        for i in range(n):
            h = c if split[i] else None
            for j, chip in enumerate(chips):
                first.append(_remote(mine(i, h), window(i, me, h), sems, 3 * i + j, (*chip, c)))
        first += [_remote(ins[i], window(i, me, None), sems, 6 * n + i, sibling) for i in range(n)]
        for cp in first:
            cp.start()
        for i in range(n):
            h = c if split[i] else None
            for j, (cx, cy) in enumerate(chips):
                got = window(i, 2 * cx + cy, h)
                _remote(got, got, sems, 3 * i + j, (cx, cy, c)).wait_recv()
                if split[i]:
                    passed.append(_remote(got, got, sems, 3 * (n + i) + j, sibling))
                    passed[-1].start()
        for i in range(n):
            if split[i]:
                for j, (cx, cy) in enumerate(chips):
                    got = window(i, 2 * cx + cy, 1 - c)
                    _remote(got, got, sems, 3 * (n + i) + j, sibling).wait_recv()
        for i in range(n):
            _remote(ins[i], window(i, me, None), sems, 6 * n + i, sibling).wait_recv()
        for cp in first + passed:
            cp.wait_send()

    return pl.pallas_call(
        body, name="gather_weights", in_specs=[HBM_SPEC] * n, out_specs=[HBM_SPEC] * n,
        out_shape=[jax.ShapeDtypeStruct(full_shape(i), blocks[i].dtype) for i in range(n)],
        scratch_shapes=[pltpu.SemaphoreType.DMA((7 * n,)), pltpu.SemaphoreType.DMA((7 * n,))],
    )(*blocks)


def _allreduce_small(v):
    rows, cols = v.shape
    n_dev = 2 * N_CHIPS

    def body(v_ref, out_ref, all_ref, send_sems, recv_sems, local_sem):
        x, y, c, chips = _place()
        sems = (send_sems, recv_sems)
        me, sibling = (x, y, c), (x, y, 1 - c)
        slot = lambda px, py, pc: all_ref.at[pl.ds((4 * px + 2 * py + pc) * rows, rows), :]
        mine = pltpu.make_async_copy(v_ref, slot(*me), local_sem)
        mine.start()
        first = [_remote(v_ref, slot(*me), sems, 0, sibling)]
        first += [_remote(v_ref, slot(*me), sems, 1 + j, (*chip, c)) for j, chip in enumerate(chips)]
        for cp in first:
            cp.start()
        passed = [_remote(slot(*chip, c), slot(*chip, c), sems, 4 + j, sibling) for j, chip in enumerate(chips)]
        for j, chip in enumerate(chips):
            _remote(slot(*chip, c), slot(*chip, c), sems, 1 + j, me).wait_recv()
            passed[j].start()
        _remote(slot(*sibling), slot(*sibling), sems, 0, me).wait_recv()
        for j, chip in enumerate(chips):
            _remote(slot(*chip, 1 - c), slot(*chip, 1 - c), sems, 4 + j, me).wait_recv()
        for cp in first + passed:
            cp.wait_send()
        mine.wait()
        acc = all_ref[pl.ds(0, rows), :]
        for k in range(1, n_dev):
            acc = acc + all_ref[pl.ds(k * rows, rows), :]
        out_ref[...] = acc

    vmem = pl.BlockSpec(memory_space=pltpu.VMEM)
    return pl.pallas_call(
        body, name="allreduce_small", in_specs=[vmem], out_specs=vmem, out_shape=jax.ShapeDtypeStruct((rows, cols), F32),
        scratch_shapes=[pltpu.VMEM((n_dev * rows, cols), F32), pltpu.SemaphoreType.DMA((7,)), pltpu.SemaphoreType.DMA((7,)),
                        pltpu.SemaphoreType.DMA],
    )(v)


def _swap_other_half(g4):
    n, _, rows, cols = g4.shape

    def body(v_ref, out_ref, send_sems, recv_sems):
        x, y, c, _ = _place()
        cp = _remote(v_ref.at[:, 1 - c], out_ref, (send_sems, recv_sems), 0, (x, y, 1 - c))
        cp.start()
        cp.wait()

    return pl.pallas_call(
        body, name="reduce_swap", in_specs=[HBM_SPEC], out_specs=HBM_SPEC, out_shape=jax.ShapeDtypeStruct((n, rows, cols), g4.dtype),
        scratch_shapes=[pltpu.SemaphoreType.DMA((1,)), pltpu.SemaphoreType.DMA((1,))],
    )(g4)


def _add_kept_half(g4, got):
    n, _, rows, cols = g4.shape
    tr = _tile(rows, (256, 128, 64, 32, 16))
    nb = rows // tr

    def body(c_ref, a_ref, b_ref, o_ref):
        o_ref[...] = (a_ref[...] + b_ref[...]).astype(o_ref.dtype)

    return pl.pallas_call(
        body, name="reduce_sum_cores",
        grid_spec=pltpu.PrefetchScalarGridSpec(
            num_scalar_prefetch=1, grid=(n, nb),
            in_specs=[pl.BlockSpec((None, None, tr, cols), lambda k, i, c_ref: (k, c_ref[0], i, 0)),
                      pl.BlockSpec((None, tr, cols), lambda k, i, c_ref: (k, i, 0))],
            out_specs=pl.BlockSpec((None, tr, cols), lambda k, i, c_ref: (k, i, 0))),
        out_shape=jax.ShapeDtypeStruct((n, rows, cols), WIRE_DTYPE), compiler_params=_params("parallel", "parallel"),
    )(lax.axis_index("c").astype(jnp.int32).reshape(1), g4, got)


def _exchange_chips(v):
    def body(v_ref, out_ref, send_sems, recv_sems):
        x, y, c, chips = _place()
        sems = (send_sems, recv_sems)
        sends = [_remote(v_ref.at[2 * cx + cy], out_ref.at[j], sems, j, (cx, cy, c)) for j, (cx, cy) in enumerate(chips)]
        for cp in sends:
            cp.start()
        for j, (cx, cy) in enumerate(chips):
            _remote(v_ref.at[0], out_ref.at[j], sems, j, (cx, cy, c)).wait_recv()
        for cp in sends:
            cp.wait_send()

    return pl.pallas_call(
        body, name="exchange_chips", in_specs=[HBM_SPEC], out_specs=HBM_SPEC,
        out_shape=jax.ShapeDtypeStruct((N_CHIPS - 1,) + v.shape[1:], v.dtype),
        scratch_shapes=[pltpu.SemaphoreType.DMA((3,)), pltpu.SemaphoreType.DMA((3,))],
    )(v)


def _swap_sibling(v):
    def body(v_ref, out_ref, send_sems, recv_sems):
        x, y, c, _ = _place()
        cp = _remote(v_ref, out_ref, (send_sems, recv_sems), 0, (x, y, 1 - c))
        cp.start()
        cp.wait()

    return pl.pallas_call(
        body, name="share_halves", in_specs=[HBM_SPEC], out_specs=HBM_SPEC, out_shape=jax.ShapeDtypeStruct(v.shape, v.dtype),
        scratch_shapes=[pltpu.SemaphoreType.DMA((1,)), pltpu.SemaphoreType.DMA((1,))],
    )(v)


def _sum_chips(own4, parts):
    _, rows, cols = own4.shape
    tr = _tile(rows, (256, 128, 64, 32, 16))

    def body(me_ref, own_ref, p0_ref, p1_ref, p2_ref, o_ref):
        acc = own_ref[...].astype(F32)
        for r in (p0_ref, p1_ref, p2_ref):
            acc = acc + r[...].astype(F32)
        o_ref[...] = acc

    part = lambda j: pl.BlockSpec((None, tr, cols), lambda i, me_ref: (j, i, 0))
    chip = (2 * lax.axis_index("x") + lax.axis_index("y")).astype(jnp.int32).reshape(1)
    return pl.pallas_call(
        body, name="reduce_sum_chips",
        grid_spec=pltpu.PrefetchScalarGridSpec(
            num_scalar_prefetch=1, grid=(rows // tr,),
            in_specs=[pl.BlockSpec((None, tr, cols), lambda i, me_ref: (me_ref[0], i, 0)), part(0), part(1), part(2)],
            out_specs=pl.BlockSpec((tr, cols), lambda i, me_ref: (i, 0))),
        out_shape=jax.ShapeDtypeStruct((rows, cols), F32), compiler_params=_params("parallel"),
    )(chip, own4, parts, parts, parts)


def _reduce_grads(g4):
    chip_sum = _add_kept_half(g4, _swap_other_half(g4))
    half = _sum_chips(chip_sum, _exchange_chips(chip_sum))
    other = _swap_sibling(half)
    return jnp.where(lax.axis_index("c") == 0, jnp.stack([half, other]), jnp.stack([other, half]))


BIG = (("ffn1_w_in", 2), ("ffn1_w_out", 1), ("xa_wq", 1), ("xa_wkv", 2), ("xa_wo", 1), ("ffn2_w_in", 2), ("ffn2_w_out", 1),
       ("ab_w_in", 2), ("ab_w_out", 1), ("dn_w_in", 2), ("dn_w_out", 1))
TINY_SHARDED = (("lru_conv_w", 2), ("dn_conv_w", 2))
REPLICATED = ("ffn1_norm", "mix_norm", "xa_norm", "xa_mem_norm", "ffn2_norm", "lru_conv_b", "lru_w_a", "lru_b_a", "lru_w_x",
              "lru_b_x", "lru_lambda", "dn_a_log", "dn_dt_bias", "dn_o_norm", "final_norm")
WEIGHTS = ("ffn1_norm", "ffn1_w_in", "ffn1_w_out", "mix_norm", "xa_norm", "xa_mem_norm", "xa_wq", "xa_wkv", "xa_wo", "ffn2_norm",
           "ffn2_w_in", "ffn2_w_out", "ab_w_in", "lru_conv_w", "lru_conv_b", "lru_w_a", "lru_b_a", "lru_w_x", "lru_b_x",
           "lru_lambda", "ab_w_out", "dn_w_in", "dn_conv_w", "dn_a_log", "dn_dt_bias", "dn_o_norm", "dn_w_out", "final_norm")


def _pad_rows(flat, row_multiple):
    n = flat.shape[-1]
    per = row_multiple * PACK_COLS
    total = -(-n // per) * per
    flat = jnp.pad(flat, [(0, 0)] * (flat.ndim - 1) + [(0, total - n)])
    return flat.reshape(flat.shape[:-1] + (total // PACK_COLS, PACK_COLS))


def _lane_padded(shape):
    return shape[:-1] + (-(-shape[-1] // LANES) * LANES,)


def _pad_lanes(t):
    return jnp.pad(t, [(0, 0)] * (t.ndim - 1) + [(0, _lane_padded(t.shape)[-1] - t.shape[-1])])


def _gather_full(shards):
    named = BIG + TINY_SHARDED
    blocks = [_pad_lanes(shards[n]).astype(MXU_DTYPE) if (n, a) in BIG else shards[n] for n, a in named]
    out = dict(zip((n for n, _ in named), _gather_weights(blocks, [a for _, a in named])))
    for n, axis in BIG:
        width, padded = shards[n].shape[-1], _lane_padded(shards[n].shape)[-1]
        if padded != width:
            assert axis == 2
            out[n] = jnp.concatenate([out[n][..., k * padded:k * padded + width] for k in range(N_CHIPS)], axis=-1)
    return out


def _to_blocks(pieces, axis):
    width = pieces[0].shape[axis - 1] // N_CHIPS
    block = lambda p, k: _pad_lanes(lax.slice_in_dim(p, k * width, (k + 1) * width, axis=axis - 1)).reshape(-1)
    return jnp.stack([jnp.concatenate([block(p, k) for p in pieces]) for k in range(N_CHIPS)], axis=0)


def _pack_grads(grads):
    flat = jnp.concatenate([_to_blocks(grads[n], axis) for n, axis in BIG], axis=1)
    g = _pad_rows(flat, 2 * 256)
    return g.reshape(N_CHIPS, 2, g.shape[1] // 2, PACK_COLS)


def _unpack_grads(reduced, shards):
    flat = reduced.reshape(-1)
    out, off = {}, 0
    for n, _ in BIG:
        shape = shards[n].shape
        size = math.prod(_lane_padded(shape))
        out[n] = flat[off:off + size].reshape(_lane_padded(shape))[..., :shape[-1]]
        off += size
    return out


def _pack_small(grads, loss):
    parts = [p.reshape(-1) for n in REPLICATED + tuple(n for n, _ in TINY_SHARDED) for p in grads[n]] + [loss[0, :1]]
    flat = jnp.concatenate(parts)
    total = -(-flat.shape[0] // (SUBLANES * LANES)) * SUBLANES * LANES
    return jnp.pad(flat, (0, total - flat.shape[0])).reshape(-1, LANES)


def _unpack_small(summed, shards, chip):
    flat = summed.reshape(-1)
    out, off = {}, 0
    for n in REPLICATED:
        out[n] = flat[off:off + shards[n].size].reshape(shards[n].shape)
        off += shards[n].size
    for n, axis in TINY_SHARDED:
        width = shards[n].shape[axis]
        shape = shards[n].shape[:axis] + (N_CHIPS * width,) + shards[n].shape[axis + 1:]
        full = flat[off:off + N_CHIPS * shards[n].size].reshape(shape)
        out[n] = lax.dynamic_slice_in_dim(full, chip * width, width, axis=axis)
        off += N_CHIPS * shards[n].size
    return out, flat[off]


def kernel(x, mem, ffn1_norm, ffn1_w_in, ffn1_w_out, mix_norm, xa_norm, xa_mem_norm, xa_wq, xa_wkv, xa_wo, ffn2_norm,
           ffn2_w_in, ffn2_w_out, ab_w_in, lru_conv_w, lru_conv_b, lru_w_a, lru_b_a, lru_w_x, lru_b_x, lru_lambda,
           ab_w_out, dn_w_in, dn_conv_w, dn_a_log, dn_dt_bias, dn_o_norm, dn_w_out, final_norm, loss_target,
           m_ffn1_norm, m_ffn1_w_in, m_ffn1_w_out, m_mix_norm, m_xa_norm, m_xa_mem_norm, m_xa_wq, m_xa_wkv, m_xa_wo,
           m_ffn2_norm, m_ffn2_w_in, m_ffn2_w_out, m_ab_w_in, m_lru_conv_w, m_lru_conv_b, m_lru_w_a, m_lru_b_a,
           m_lru_w_x, m_lru_b_x, m_lru_lambda, m_ab_w_out, m_dn_w_in, m_dn_conv_w, m_dn_a_log, m_dn_dt_bias,
           m_dn_o_norm, m_dn_w_out, m_final_norm, v_ffn1_norm, v_ffn1_w_in, v_ffn1_w_out, v_mix_norm, v_xa_norm,
           v_xa_mem_norm, v_xa_wq, v_xa_wkv, v_xa_wo, v_ffn2_norm, v_ffn2_w_in, v_ffn2_w_out, v_ab_w_in,
           v_lru_conv_w, v_lru_conv_b, v_lru_w_a, v_lru_b_a, v_lru_w_x, v_lru_b_x, v_lru_lambda, v_ab_w_out,
           v_dn_w_in, v_dn_conv_w, v_dn_a_log, v_dn_dt_bias, v_dn_o_norm, v_dn_w_out, v_final_norm):
    given = dict(locals())
    shards = {n: given[n] for n in WEIGHTS}
    chip = 2 * lax.axis_index("x") + lax.axis_index("y")

    full = {n: shards[n] for n in REPLICATED}
    full.update(_gather_full(shards))
    loss, grad_x, grads = _local_step(x[0], mem[0], loss_target[0], full)

    small, loss_sum = _unpack_small(_allreduce_small(_pack_small(grads, loss)), shards, chip)
    grad = {**small, **_unpack_grads(_reduce_grads(_pack_grads(grads)), shards)}

    delta, new_m, new_v = {}, {}, {}
    for n in WEIGHTS:
        delta[n], new_m[n], new_v[n] = _adamw(shards[n], grad[n], given["m_" + n], given["v_" + n], name=f"adamw_{n}")
    return (loss_sum, grad_x[None], *[grad[n] for n in WEIGHTS], *[delta[n] for n in WEIGHTS],
            *[new_m[n] for n in WEIGHTS], *[new_v[n] for n in WEIGHTS])
```

```python
import math

import jax
import jax.numpy as jnp
from jax import lax
from jax.experimental import pallas as pl
from jax.experimental.pallas import tpu as pltpu

F32 = jnp.float32
MXU_DTYPE = jnp.bfloat16
VMEM_LIMIT_BYTES = 48 * 1024 * 1024
MM_BLOCK_BYTES = 8 * 1024 * 1024
LANES = 128
SUBLANES = 8

NORM_EPS = 1e-6
CONV_K = 4
ATTN_PAIRS = 4
ATTN_HEAD_DIM = 64
ATTN_WIDTH = 512
ATTN_BLOCK = 128
DILATIONS = (1, 4, 16)
LRU_WIDTH = 512
LRU_BLOCKS = 8
LRU_C = 8.0
DN_HEADS = 8
DN_HEAD_DIM = 128
DN_WIDTH = 1024
DN_CHUNK = 64
XA_HEADS = 4
XA_HEAD_DIM = 256
D_FF = 2816
ADAM_LR, ADAM_B1, ADAM_B2, ADAM_EPS, ADAM_WD, ADAM_STEP = 0.001, 0.9, 0.999, 1e-08, 0.01, 10

MESH = pl.DeviceIdType.MESH


def _tile(n, prefs):
    for p in prefs:
        if n % p == 0:
            return p
    return n


def _params(*sem):
    return pltpu.CompilerParams(dimension_semantics=sem, vmem_limit_bytes=VMEM_LIMIT_BYTES)


def _dg(a, b, dims, hi=False):
    if hi:
        return lax.dot_general(a, b, (dims, ((), ())), precision=lax.Precision.HIGHEST, preferred_element_type=F32)
    return lax.dot_general(a.astype(MXU_DTYPE), b.astype(MXU_DTYPE), (dims, ((), ())), preferred_element_type=F32)


@jax.custom_vjp
def _bdot(a, b):
    return _dg(a, b, ((1,), (0,)))


def _bdot_fwd(a, b):
    return _bdot(a, b), (a, b)


def _bdot_bwd(r, g):
    a, b = r
    return _dg(g, b, ((1,), (1,))).astype(a.dtype), _dg(a, g, ((0,), (0,))).astype(b.dtype)


_bdot.defvjp(_bdot_fwd, _bdot_bwd)


def _log1p(t):
    return jnp.where(t < 0.01, t * (1.0 - t * (0.5 - t * (1.0 / 3.0))), jnp.log(1.0 + t))


def _neg_expm1(y):
    series = -y * (1.0 + 0.5 * y * (1.0 + (1.0 / 3.0) * y * (1.0 + 0.25 * y)))
    return jnp.where(y > -0.01, series, 1.0 - jnp.exp(y))


def _softplus(x):
    return jnp.maximum(x, 0.0) + _log1p(jnp.exp(-jnp.abs(x)))


def _sigmoid(x):
    return 1.0 / (1.0 + jnp.exp(-x))


def _silu(x):
    return x * _sigmoid(x)


def _gelu(x):
    return 0.5 * x * (1.0 + jnp.tanh(0.7978845608028654 * (x + 0.044715 * x * x * x)))


def _rows(shape):
    return lax.broadcasted_iota(jnp.int32, shape, 0)


def _cols(shape):
    return lax.broadcasted_iota(jnp.int32, shape, 1)


def _mm(a, b, *, mode="nn", out_dtype=F32, res=None, scale=1.0, name):
    if mode == "nn":
        (m, k), (k2, n) = a.shape, b.shape
    elif mode == "nt":
        (m, k), (n, k2) = a.shape, b.shape
    else:
        (k, m), (k2, n) = a.shape, b.shape
    assert k == k2, (a.shape, b.shape, mode)
    if mode == "tn":
        tm, tn, tk = _tile(m, (1024, 512, 256, 128)), _tile(n, (1024, 512, 256, 128)), _tile(k, (2048, 1024, 512, 256))
    else:
        tm, tn = _tile(m, (512, 256, 128)), _tile(n, (1024, 512, 256, 128))
        tk = k if k * tn * 2 <= MM_BLOCK_BYTES else _tile(k, (1024, 512, 256, 128))
    nk = k // tk
    dims = {"nn": ((1,), (0,)), "nt": ((1,), (1,)), "tn": ((0,), (0,))}[mode]

    def body(*refs):
        a_ref, b_ref = refs[:2]
        r_ref = refs[2] if res is not None else None
        o_ref = refs[3 if res is not None else 2]

        def finish(r):
            if scale != 1.0:
                r = r * scale
            if res is not None:
                r = r_ref[...] + r
            o_ref[...] = r.astype(out_dtype)

        if nk == 1:
            finish(_dg(a_ref[...], b_ref[...], dims))
            return
        acc = refs[-1]
        kk = pl.program_id(2)

        @pl.when(kk == 0)
        def _():
            acc[...] = jnp.zeros_like(acc)

        acc[...] += _dg(a_ref[...], b_ref[...], dims)

        @pl.when(kk == nk - 1)
        def _():
            finish(acc[...])

    a_spec = pl.BlockSpec((tk, tm), lambda i, j, kk: (kk, i)) if mode == "tn" else pl.BlockSpec((tm, tk), lambda i, j, kk: (i, kk))
    b_spec = pl.BlockSpec((tn, tk), lambda i, j, kk: (j, kk)) if mode == "nt" else pl.BlockSpec((tk, tn), lambda i, j, kk: (kk, j))
    o_spec = pl.BlockSpec((tm, tn), lambda i, j, kk: (i, j))
    in_specs = [a_spec, b_spec] + ([o_spec] if res is not None else [])
    args = (a, b) + ((res,) if res is not None else ())
    return pl.pallas_call(
        body, name=name, grid=(m // tm, n // tn, nk), in_specs=in_specs, out_specs=o_spec,
        out_shape=jax.ShapeDtypeStruct((m, n), out_dtype), scratch_shapes=[pltpu.VMEM((tm, tn), F32)] if nk > 1 else [],
        compiler_params=_params("parallel", "parallel", "arbitrary"),
    )(*args)


def _rms_fwd(x, g, *, name):
    s, d = x.shape
    tm = _tile(s, (512, 256))

    def body(x_ref, g_ref, o_ref):
        xv = x_ref[...]
        r = lax.rsqrt(jnp.mean(xv * xv, axis=-1, keepdims=True) + NORM_EPS)
        o_ref[...] = (xv * r * g_ref[...]).astype(o_ref.dtype)

    return pl.pallas_call(
        body, name=name, grid=(s // tm,),
        in_specs=[pl.BlockSpec((tm, d), lambda i: (i, 0)), pl.BlockSpec((1, d), lambda i: (0, 0))],
        out_specs=pl.BlockSpec((tm, d), lambda i: (i, 0)), out_shape=jax.ShapeDtypeStruct((s, d), MXU_DTYPE),
        compiler_params=_params("parallel"),
    )(x, g.reshape(1, d))


def _norm_mm(x, g, ws, out_dtypes, *, name):
    s, d = x.shape
    tm = _tile(s, (512, 256))
    nw = len(ws)

    def body(*refs):
        x_ref, g_ref = refs[:2]
        h_ref = refs[2 + nw]
        xv = x_ref[...]
        r = lax.rsqrt(jnp.mean(xv * xv, axis=-1, keepdims=True) + NORM_EPS)
        h = (xv * r * g_ref[...]).astype(MXU_DTYPE)
        h_ref[...] = h
        for w_ref, o_ref in zip(refs[2:2 + nw], refs[3 + nw:]):
            o_ref[...] = _dg(h, w_ref[...], ((1,), (0,))).astype(o_ref.dtype)

    row = lambda w: pl.BlockSpec((tm, w), lambda i: (i, 0))
    outs = pl.pallas_call(
        body, name=name, grid=(s // tm,),
        in_specs=[row(d), pl.BlockSpec((1, d), lambda i: (0, 0))]
        + [pl.BlockSpec(w.shape, lambda i: (0, 0), pipeline_mode=RESIDENT) for w in ws],
        out_specs=[row(d)] + [row(w.shape[1]) for w in ws],
        out_shape=[jax.ShapeDtypeStruct((s, d), MXU_DTYPE)] + [jax.ShapeDtypeStruct((s, w.shape[1]), t) for w, t in zip(ws, out_dtypes)],
        compiler_params=_params("parallel"),
    )(x, g.reshape(1, d), *ws)
    return outs[0], outs[1:]


def _mm_rms_bwd(pairs, x, g, dres, *, name):
    s, d = x.shape
    tm = _tile(s, (512, 256))
    n = len(pairs)

    def body(*refs):
        x_ref, g_ref, dr_ref = refs[2 * n:2 * n + 3]
        dx_ref, dg_ref = refs[2 * n + 3:]
        dh = _dg(refs[0][...], refs[n][...], ((1,), (1,)))
        for a_ref, w_ref in zip(refs[1:n], refs[n + 1:2 * n]):
            dh = dh + _dg(a_ref[...], w_ref[...], ((1,), (1,)))
        xv, gv = x_ref[...], g_ref[...]
        r = lax.rsqrt(jnp.mean(xv * xv, axis=-1, keepdims=True) + NORM_EPS)
        xh = xv * r
        dxh = dh * gv
        dx_ref[...] = dr_ref[...] + r * (dxh - xh * jnp.mean(dxh * xh, axis=-1, keepdims=True))

        @pl.when(pl.program_id(0) == 0)
        def _():
            dg_ref[...] = jnp.zeros_like(dg_ref)

        dg_ref[...] += jnp.sum(dh * xh, axis=0, keepdims=True)

    row = lambda w: pl.BlockSpec((tm, w), lambda i: (i, 0))
    vec = pl.BlockSpec((1, d), lambda i: (0, 0))
    return pl.pallas_call(
        body, name=name, grid=(s // tm,),
        in_specs=[row(a.shape[1]) for a, _ in pairs]
        + [pl.BlockSpec(w.shape, lambda i: (0, 0), pipeline_mode=RESIDENT) for _, w in pairs] + [row(d), vec, row(d)],
        out_specs=[row(d), vec], out_shape=[jax.ShapeDtypeStruct((s, d), F32), jax.ShapeDtypeStruct((1, d), F32)],
        compiler_params=_params("arbitrary"),
    )(*[a for a, _ in pairs], *[w for _, w in pairs], x, g.reshape(1, d), dres)


def _rms_bwd(x, g, dh, dres, *, name):
    s, d = x.shape
    tm = _tile(s, (512, 256))

    def body(x_ref, g_ref, dh_ref, dr_ref, dx_ref, dg_ref):
        xv = x_ref[...]
        r = lax.rsqrt(jnp.mean(xv * xv, axis=-1, keepdims=True) + NORM_EPS)
        xh = xv * r
        dhv = dh_ref[...].astype(F32)
        dxh = dhv * g_ref[...]
        dx = r * (dxh - xh * jnp.mean(dxh * xh, axis=-1, keepdims=True))
        dx_ref[...] = dr_ref[...] + dx

        @pl.when(pl.program_id(0) == 0)
        def _():
            dg_ref[...] = jnp.zeros_like(dg_ref)

        dg_ref[...] += jnp.sum(dhv * xh, axis=0, keepdims=True)

    row = pl.BlockSpec((tm, d), lambda i: (i, 0))
    vec = pl.BlockSpec((1, d), lambda i: (0, 0))
    return pl.pallas_call(
        body, name=name, grid=(s // tm,), in_specs=[row, vec, row, row], out_specs=[row, vec],
        out_shape=[jax.ShapeDtypeStruct((s, d), F32), jax.ShapeDtypeStruct((1, d), F32)],
        compiler_params=_params("arbitrary"),
    )(x, g.reshape(1, d), dh, dres)


FFN_CHUNK = 256
FFN_TM = 256
RESIDENT = pl.Buffered(1)


def _ffn_fwd_call(x, g, w_in, w_out, layer, *, name):
    s, d = x.shape
    f = w_out.shape[1]
    tm = _tile(s, (2 * FFN_TM, FFN_TM))

    def body(x_ref, g_ref, wi_ref, wo_ref, y_ref, u_ref, act_ref):
        xv = x_ref[...]
        r = lax.rsqrt(jnp.mean(xv * xv, axis=-1, keepdims=True) + NORM_EPS)
        h = (xv * r * g_ref[...]).astype(MXU_DTYPE)
        for j in range(f // FFN_CHUNK):
            lo, hi = j * FFN_CHUNK, (j + 1) * FFN_CHUNK
            gate = _dg(h, wi_ref[:, lo:hi], ((1,), (0,))).astype(MXU_DTYPE)
            up = _dg(h, wi_ref[:, f + lo:f + hi], ((1,), (0,))).astype(MXU_DTYPE)
            u_ref[:, lo:hi] = gate
            u_ref[:, f + lo:f + hi] = up
            act_ref[:, lo:hi] = (_silu(gate.astype(F32)) * up.astype(F32)).astype(MXU_DTYPE)
        y_ref[...] = xv + 0.5 * _dg(act_ref[...], wo_ref[...], ((1,), (0,)))

    row = lambda w: pl.BlockSpec((tm, w), lambda i: (i, 0))
    return pl.pallas_call(
        body, name=name, grid=(s // tm,),
        in_specs=[row(d), pl.BlockSpec((1, d), lambda i: (0, 0)),
                  pl.BlockSpec((None,) + w_in.shape[1:], lambda i: (layer, 0, 0), pipeline_mode=RESIDENT),
                  pl.BlockSpec((None,) + w_out.shape[1:], lambda i: (layer, 0, 0), pipeline_mode=RESIDENT)],
        out_specs=[row(d), row(2 * f)],
        out_shape=[jax.ShapeDtypeStruct((s, d), F32), jax.ShapeDtypeStruct((s, 2 * f), MXU_DTYPE)],
        scratch_shapes=[pltpu.VMEM((tm, f), MXU_DTYPE)], compiler_params=_params("parallel"),
    )(x, g.reshape(1, d), w_in, w_out)


def _ffn_bwd_call(x, g, u, dy, w_in, w_out, layer, *, name):
    s, d = x.shape
    f = w_out.shape[1]
    tm = _tile(s, (FFN_TM,))

    def body(x_ref, g_ref, u_ref, dy_ref, wi_ref, wo_ref, du_ref, dx_ref, dg_ref, h_ref):
        dyv = dy_ref[...]
        dyh = (0.5 * dyv).astype(MXU_DTYPE)
        for j in range(f // FFN_CHUNK):
            lo, hi = j * FFN_CHUNK, (j + 1) * FFN_CHUNK
            dact = _dg(dyh, wo_ref[lo:hi, :], ((1,), (1,)))
            gate, up = u_ref[:, lo:hi].astype(F32), u_ref[:, f + lo:f + hi].astype(F32)
            sg = _sigmoid(gate)
            du_ref[:, lo:hi] = (dact * up * sg * (1.0 + gate * (1.0 - sg))).astype(MXU_DTYPE)
            du_ref[:, f + lo:f + hi] = (dact * gate * sg).astype(MXU_DTYPE)
        dh = _dg(du_ref[...], wi_ref[...], ((1,), (1,)))
        xv, gv = x_ref[...], g_ref[...]
        r = lax.rsqrt(jnp.mean(xv * xv, axis=-1, keepdims=True) + NORM_EPS)
        xh = xv * r
        h_ref[...] = (xh * gv).astype(MXU_DTYPE)
        dxh = dh * gv
        dx_ref[...] = dyv + r * (dxh - xh * jnp.mean(dxh * xh, axis=-1, keepdims=True))

        @pl.when(pl.program_id(0) == 0)
        def _():
            dg_ref[...] = jnp.zeros_like(dg_ref)

        dg_ref[...] += jnp.sum(dh * xh, axis=0, keepdims=True)

    row = lambda w: pl.BlockSpec((tm, w), lambda i: (i, 0))
    vec = pl.BlockSpec((1, d), lambda i: (0, 0))
    return pl.pallas_call(
        body, name=name, grid=(s // tm,),
        in_specs=[row(d), vec, row(2 * f), row(d),
                  pl.BlockSpec((None,) + w_in.shape[1:], lambda i: (layer, 0, 0), pipeline_mode=RESIDENT),
                  pl.BlockSpec((None,) + w_out.shape[1:], lambda i: (layer, 0, 0), pipeline_mode=RESIDENT)],
        out_specs=[row(2 * f), row(d), vec, row(d)],
        out_shape=[jax.ShapeDtypeStruct((s, 2 * f), MXU_DTYPE), jax.ShapeDtypeStruct((s, d), F32),
                   jax.ShapeDtypeStruct((1, d), F32), jax.ShapeDtypeStruct((s, d), MXU_DTYPE)],
        compiler_params=_params("arbitrary"),
    )(x, g.reshape(1, d), u, dy, w_in, w_out)


def _ffn_dw_out(u, dy, *, name):
    s, f2 = u.shape
    f, d = f2 // 2, dy.shape[1]
    tf, tk = _tile(f, (1408, 256, 128)), _tile(s, (1024, 512, 256))
    nj = f // tf

    def body(g_ref, u_ref, dy_ref, o_ref):
        @pl.when(pl.program_id(1) == 0)
        def _():
            o_ref[...] = jnp.zeros_like(o_ref)

        act = _silu(g_ref[...].astype(F32)) * u_ref[...].astype(F32)
        o_ref[...] += _dg(act, 0.5 * dy_ref[...], ((0,), (0,)))

    return pl.pallas_call(
        body, name=name, grid=(nj, s // tk),
        in_specs=[pl.BlockSpec((tk, tf), lambda j, k: (k, j)), pl.BlockSpec((tk, tf), lambda j, k: (k, j + nj)),
                  pl.BlockSpec((tk, d), lambda j, k: (k, 0))],
        out_specs=pl.BlockSpec((tf, d), lambda j, k: (j, 0)), out_shape=jax.ShapeDtypeStruct((f, d), F32),
        compiler_params=_params("parallel", "arbitrary"),
    )(u, u, dy)


def _ffn_fwd(x, g, w_in, w_out, layer, tag):
    y, u = _ffn_fwd_call(x, g[layer], w_in, w_out, layer, name=f"{tag}_fwd")
    return y, (x, u)


def _ffn_bwd(saved, g, w_in, w_out, layer, dy, tag):
    x, u = saved
    du, dx, dg, h = _ffn_bwd_call(x, g[layer], u, dy, w_in, w_out, layer, name=f"{tag}_bwd")
    dw_out = _ffn_dw_out(u, dy, name=f"{tag}_dwout")
    dw_in = _mm(h, du, mode="tn", name=f"{tag}_dwin")
    return dx, dg[0], dw_in, dw_out


ATTN_SCALE = ATTN_HEAD_DIM ** -0.5
NEG_BIG = -1e30
PROJ_AB_BLOCKS = 5


def _first_head(n):
    return _cols((n, LANES)) < ATTN_HEAD_DIM


def _per_head(tiles):
    first = _first_head(tiles[0].shape[0])
    return jnp.stack([jnp.where(first == (h == 0), t, 0.0) for t in tiles for h in (0, 1)], axis=0)


def _both(tiles):
    return jnp.stack([t for t in tiles for _ in (0, 1)], axis=0)


def _head_cols(tiles):
    return jnp.stack([t[:, c0:c0 + 1] for t in tiles for c0 in (0, ATTN_HEAD_DIM)], axis=0)


def _join_heads(v):
    return [v[2 * u] + v[2 * u + 1] for u in range(v.shape[0] // 2)]


def _spread_heads(v):
    first = _first_head(v.shape[1])
    return [jnp.where(first, v[2 * u], v[2 * u + 1]) for u in range(v.shape[0] // 2)]


def _band_masks(has_prev):
    qi, kj = _rows((ATTN_BLOCK, ATTN_BLOCK)), _cols((ATTN_BLOCK, ATTN_BLOCK))
    return (kj >= qi) & has_prev, kj <= qi


def _dattn_delta(o, dcat, *, name):
    s_len = o.shape[0]
    tm = _tile(s_len, (512, 256))

    def body(o_ref, do_ref, out_ref):
        r, c = _rows((ATTN_WIDTH, ATTN_WIDTH)), _cols((ATTN_WIDTH, ATTN_WIDTH))
        ones_bd = (r // ATTN_HEAD_DIM == c // ATTN_HEAD_DIM).astype(F32)
        out_ref[...] = _dg(o_ref[...] * do_ref[...], ones_bd, ((1,), (0,)), hi=True)

    blk = pl.BlockSpec((tm, ATTN_WIDTH), lambda i: (i, 0))
    return pl.pallas_call(
        body, name=name, grid=(s_len // tm,), in_specs=[blk, blk], out_specs=blk,
        out_shape=jax.ShapeDtypeStruct((s_len, ATTN_WIDTH), F32), compiler_params=_params("parallel"),
    )(o, dcat)


ATTN_UNITS = 4


def _units(it, d):
    if d == 1:
        return [(pl.ds(0, ATTN_BLOCK), pl.ds(p * LANES, LANES)) for p in range(ATTN_UNITS)]
    return [(pl.ds(it * ATTN_UNITS + u, ATTN_BLOCK, stride=d), pl.ds(0, LANES)) for u in range(ATTN_UNITS)]


def _tiles(ref, units):
    return [ref[rows, lanes] for rows, lanes in units]


def _store_tiles(ref, units, tiles):
    for (rows, lanes), t in zip(units, tiles):
        ref[rows, lanes] = t


def _stacked(a_tiles, b_tiles):
    return [jnp.concatenate([a, b], axis=0) for a, b in zip(a_tiles, b_tiles)]


def _passes(d):
    return max(d // ATTN_UNITS, 1)


def _pairs_per_step(d):
    return ATTN_PAIRS if d == 1 else 1


def _pair_specs(d, n_of):
    pairs = _pairs_per_step(d)
    groups = ATTN_PAIRS // pairs
    return lambda c: pl.BlockSpec((ATTN_BLOCK * d, LANES * pairs), lambda n, p: (n_of(n), c * groups + p))


def _sattn_fwd(proj, state, d, *, last, name):
    s_len = proj.shape[0]
    nb = s_len // (ATTN_BLOCK * d)
    first = state is None
    n_out = 2 if last else 3

    def body(*refs):
        q_ref, kp_ref, kc_ref, vp_ref, vc_ref = refs[:5]
        st_refs = () if first else refs[5:8]
        out_refs = refs[-n_out:]
        ok = jnp.concatenate(_band_masks(pl.program_id(0) > 0), axis=1)

        def one_pass(it, carry):
            units = _units(it, d)
            kcat = _stacked(_tiles(kp_ref, units), _tiles(kc_ref, units))
            vcat = _stacked(_tiles(vp_ref, units), _tiles(vc_ref, units))
            s = jnp.where(ok, _bdg(_per_head(_tiles(q_ref, units)), _both(kcat), 2, 2) * ATTN_SCALE, NEG_BIG)
            m_new = jnp.max(s, axis=2, keepdims=True)
            if not first:
                m_old = _head_cols(_tiles(st_refs[0], units))
                m_new = jnp.maximum(m_old, m_new)
                alpha = jnp.exp(m_old - m_new)
            p = jnp.exp(s - m_new)
            l_new = jnp.sum(p, axis=2, keepdims=True)
            acc = _join_heads(_bdg(p, _per_head(vcat), 2, 1))
            if not first:
                l_new = l_new + _head_cols(_tiles(st_refs[1], units)) * alpha
                acc = [a + a_in * sp for a, a_in, sp in zip(acc, _tiles(st_refs[2], units), _spread_heads(alpha))]
            m_pair, l_pair = _spread_heads(m_new), _spread_heads(l_new)
            if last:
                _store_tiles(out_refs[0], units, [a / l for a, l in zip(acc, l_pair)])
                _store_tiles(out_refs[1], units, [m + jnp.log(l) for m, l in zip(m_pair, l_pair)])
            else:
                _store_tiles(out_refs[0], units, m_pair)
                _store_tiles(out_refs[1], units, l_pair)
                _store_tiles(out_refs[2], units, acc)
            return carry

        lax.fori_loop(0, _passes(d), one_pass, 0)

    cur, prev = _pair_specs(d, lambda n: n), _pair_specs(d, lambda n: jnp.maximum(n - 1, 0))
    st = cur(0)
    return tuple(pl.pallas_call(
        body, name=name, grid=(nb, ATTN_PAIRS // _pairs_per_step(d)),
        in_specs=[cur(0), prev(1), cur(1), prev(2), cur(2)] + ([] if first else [st] * 3),
        out_specs=[st] * n_out, out_shape=[jax.ShapeDtypeStruct((s_len, ATTN_WIDTH), F32)] * n_out,
        compiler_params=_params("arbitrary", "parallel"),
    )(*([proj] * 5 + ([] if first else list(state)))))


def _dattn_forward(proj, tag):
    state = None
    for i, d in enumerate(DILATIONS):
        state = _sattn_fwd(proj, state, d, last=i == len(DILATIONS) - 1, name=f"{tag}_attn_d{d}")
    return state


def _sattn_bwd(proj, dcat, lse, delta, grads_in, d, *, name):
    s_len = proj.shape[0]
    nb = s_len // (ATTN_BLOCK * d)
    first = grads_in is None
    groups = ATTN_PAIRS // _pairs_per_step(d)

    def body(*refs):
        q_ref, kp_ref, kc_ref, vp_ref, vc_ref, do_ref, lse_ref, dl_ref = refs[:8]
        dq_in, dk_in, dv_in = (None, None, None) if first else refs[8:11]
        dq_ref, dk_ref, dv_ref, carry_k, carry_v = refs[-5:]
        n = pl.program_id(1)
        ok = jnp.concatenate(_band_masks(n > 0), axis=1)

        @pl.when(n == 0)
        def _():
            carry_k[...] = jnp.zeros_like(carry_k)
            carry_v[...] = jnp.zeros_like(carry_v)

        def leave(ref, carry, units, extra, into):
            out = [c + e for c, e in zip(_tiles(carry, units), extra)] if extra else _tiles(carry, units)
            if into is not None:
                out = [a + b for a, b in zip(out, _tiles(into, units))]
            _store_tiles(ref, units, out)

        def one_pass(it, carry):
            units = _units(it, d)
            kcat = _stacked(_tiles(kp_ref, units), _tiles(kc_ref, units))
            vcat = _stacked(_tiles(vp_ref, units), _tiles(vc_ref, units))
            q2, do2 = _per_head(_tiles(q_ref, units)), _per_head(_tiles(do_ref, units))
            s = _bdg(q2, _both(kcat), 2, 2) * ATTN_SCALE
            pr = jnp.where(ok, jnp.exp(jnp.where(ok, s, NEG_BIG) - _head_cols(_tiles(lse_ref, units))), 0.0)
            ds = pr * (_bdg(do2, _both(vcat), 2, 2) - _head_cols(_tiles(dl_ref, units)))
            dq = [t * ATTN_SCALE for t in _join_heads(_bdg(ds, _per_head(kcat), 2, 1))]
            if not first:
                dq = [a + b for a, b in zip(dq, _tiles(dq_in, units))]
            _store_tiles(dq_ref, units, dq)
            dk = [t * ATTN_SCALE for t in _join_heads(_bdg(ds, q2, 1, 1))]
            dv = _join_heads(_bdg(pr, do2, 1, 1))
            leave(dk_ref, carry_k, units, [t[:ATTN_BLOCK] for t in dk], dk_in)
            leave(dv_ref, carry_v, units, [t[:ATTN_BLOCK] for t in dv], dv_in)
            _store_tiles(carry_k, units, [t[ATTN_BLOCK:] for t in dk])
            _store_tiles(carry_v, units, [t[ATTN_BLOCK:] for t in dv])
            return carry

        def last_pass(it, carry):
            units = _units(it, d)
            leave(dk_ref, carry_k, units, None, dk_in)
            leave(dv_ref, carry_v, units, None, dv_in)
            return carry

        @pl.when(n < nb)
        def _():
            lax.fori_loop(0, _passes(d), one_pass, 0)

        @pl.when(n == nb)
        def _():
            lax.fori_loop(0, _passes(d), last_pass, 0)

    pairs = _pairs_per_step(d)
    blk = (ATTN_BLOCK * d, LANES * pairs)
    at = lambda n_of: (lambda c: pl.BlockSpec(blk, lambda p, n: (n_of(n), c * groups + p)))
    here = lambda n: jnp.minimum(n, nb - 1)
    cur, prev, lag = at(here), at(lambda n: jnp.maximum(here(n) - 1, 0)), at(lambda n: jnp.maximum(n - 1, 0))
    st, st_lag = cur(0), lag(0)
    return tuple(pl.pallas_call(
        body, name=name, grid=(groups, nb + 1),
        in_specs=[cur(0), prev(1), cur(1), prev(2), cur(2), st, st, st] + ([] if first else [st, st_lag, st_lag]),
        out_specs=[st, st_lag, st_lag], out_shape=[jax.ShapeDtypeStruct((s_len, ATTN_WIDTH), F32)] * 3,
        scratch_shapes=[pltpu.VMEM(blk, F32)] * 2, compiler_params=_params("parallel", "arbitrary"),
    )(*([proj] * 5 + [dcat, lse, delta] + ([] if first else list(grads_in)))))


def _dattn_backward(proj, o, lse, dcat, tag):
    delta = _dattn_delta(o, dcat, name=f"{tag}_attn_delta")
    grads = None
    for d in DILATIONS:
        grads = _sattn_bwd(proj, dcat, lse, delta, grads, d, name=f"{tag}_attn_bwd_d{d}")
    return grads


CONV_TC = 512
CONV_T = 512


def _conv_tiles(s_len, cb0, width):
    wide = 2 * CONV_TC
    tc = wide if width % wide == 0 and (cb0 * CONV_TC) % wide == 0 else CONV_TC
    return _tile(s_len, (CONV_T, CONV_T // 2)), tc, cb0 * CONV_TC // tc


def _shift_down(ext, k, t):
    return (pltpu.roll(ext, k, 0) if k else ext)[SUBLANES:SUBLANES + t]


def _conv_fwd(src, cb0, width, w8, *, name):
    s_len = src.shape[0]
    t, tc, cb = _conv_tiles(s_len, cb0, width)
    tpb = t // SUBLANES

    def body(x_ref, h_ref, w_ref, y_ref):
        halo = jnp.where(pl.program_id(0) > 0, h_ref[...], 0.0)
        ext = jnp.concatenate([halo, x_ref[...]], axis=0)
        w = w_ref[...]
        y = jnp.broadcast_to(w[CONV_K:CONV_K + 1], (t, tc))
        for k in range(CONV_K):
            y = y + w[k:k + 1] * _shift_down(ext, CONV_K - 1 - k, t)
        y_ref[...] = y

    return pl.pallas_call(
        body, name=name, grid=(s_len // t, width // tc),
        in_specs=[pl.BlockSpec((t, tc), lambda i, j: (i, cb + j)),
                  pl.BlockSpec((SUBLANES, tc), lambda i, j: (jnp.maximum(i * tpb - 1, 0), cb + j)),
                  pl.BlockSpec((SUBLANES, tc), lambda i, j: (0, j))],
        out_specs=pl.BlockSpec((t, tc), lambda i, j: (i, j)), out_shape=jax.ShapeDtypeStruct((s_len, width), F32),
        compiler_params=_params("parallel", "parallel"),
    )(src, src, w8)


def _conv_bwd(src, cb0, width, w8, dy, *, name):
    s_len = src.shape[0]
    t, tc, cb = _conv_tiles(s_len, cb0, width)
    tpb = t // SUBLANES
    ni = s_len // t

    def body(x_ref, h_ref, w_ref, dy_ref, dn_ref, dx_ref, dw_ref):
        i = pl.program_id(1)
        halo = jnp.where(i > 0, h_ref[...], 0.0)
        ext = jnp.concatenate([halo, x_ref[...]], axis=0)
        dyv = dy_ref[...]
        extn = jnp.concatenate([dyv, jnp.where(i < ni - 1, dn_ref[...], 0.0)], axis=0)
        w = w_ref[...]
        row = _rows((SUBLANES, tc))
        dx = jnp.zeros((t, tc), F32)
        dw = jnp.where(row == CONV_K, jnp.sum(dyv, axis=0, keepdims=True), 0.0)
        for k in range(CONV_K):
            up = CONV_K - 1 - k
            dx = dx + w[k:k + 1] * (pltpu.roll(extn, t + SUBLANES - up, 0) if up else extn)[:t]
            dw = dw + jnp.where(row == k, jnp.sum(dyv * _shift_down(ext, up, t), axis=0, keepdims=True), 0.0)
        dx_ref[...] = dx.astype(dx_ref.dtype)

        @pl.when(i == 0)
        def _():
            dw_ref[...] = jnp.zeros_like(dw_ref)

        dw_ref[...] += dw

    return pl.pallas_call(
        body, name=name, grid=(width // tc, ni),
        in_specs=[pl.BlockSpec((t, tc), lambda j, i: (i, cb + j)),
                  pl.BlockSpec((SUBLANES, tc), lambda j, i: (jnp.maximum(i * tpb - 1, 0), cb + j)),
                  pl.BlockSpec((SUBLANES, tc), lambda j, i: (0, j)),
                  pl.BlockSpec((t, tc), lambda j, i: (i, j)),
                  pl.BlockSpec((SUBLANES, tc), lambda j, i: (jnp.minimum((i + 1) * tpb, s_len // SUBLANES - 1), j))],
        out_specs=[pl.BlockSpec((t, tc), lambda j, i: (i, j)), pl.BlockSpec((SUBLANES, tc), lambda j, i: (0, j))],
        out_shape=[jax.ShapeDtypeStruct((s_len, width), MXU_DTYPE), jax.ShapeDtypeStruct((SUBLANES, width), F32)],
        compiler_params=_params("parallel", "arbitrary"),
    )(src, src, w8, dy, dy)


LRU_T = 256


def _lru_gates(xc, wa, wx, ba, bx, lam):
    r = _sigmoid(_bdot(xc, wa) + ba)
    i = _sigmoid(_bdot(xc, wx) + bx)
    log_a = (-LRU_C) * r * _softplus(-lam)
    return jnp.exp(log_a), jnp.sqrt(_neg_expm1(2.0 * log_a)) * i * xc


def _block_scan(a, b, reverse):
    t = a.shape[0]
    row = _rows(a.shape)
    s = 1
    while s < t:
        shift, ok = (t - s, row < t - s) if reverse else (s, row >= s)
        b = jnp.where(ok, a * pltpu.roll(b, shift, 0) + b, b)
        a = jnp.where(ok, a * pltpu.roll(a, shift, 0), a)
        s *= 2
    return a, b


def _lru_fwd(xc, proj, wa, wx, ba, bx, lam, *, name):
    s_len, w = xc.shape
    t = _tile(s_len, (LRU_T,))

    def body(xc_ref, gr_ref, wa_ref, wx_ref, ba_ref, bx_ref, lam_ref, h_ref, y_ref, carry):
        @pl.when(pl.program_id(0) == 0)
        def _():
            carry[...] = jnp.zeros_like(carry)

        a, b = _lru_gates(xc_ref[...], wa_ref[...], wx_ref[...], ba_ref[...], bx_ref[...], lam_ref[...])
        a_cum, h0 = _block_scan(a, b, False)
        h = h0 + a_cum * carry[0:1, :]
        h_ref[...] = h
        y_ref[...] = (h * _gelu(gr_ref[...])).astype(y_ref.dtype)
        carry[0:1, :] = h[t - 1:t, :]

    row = pl.BlockSpec((t, w), lambda i: (i, 0))
    mat = pl.BlockSpec((w, w), lambda i: (0, 0))
    vec = pl.BlockSpec((1, w), lambda i: (0, 0))
    return pl.pallas_call(
        body, name=name, grid=(s_len // t,),
        in_specs=[row, pl.BlockSpec((t, w), lambda i: (i, PROJ_AB_BLOCKS - 1)), mat, mat, vec, vec, vec],
        out_specs=[row, row], out_shape=[jax.ShapeDtypeStruct((s_len, w), F32), jax.ShapeDtypeStruct((s_len, w), MXU_DTYPE)],
        scratch_shapes=[pltpu.VMEM((SUBLANES, w), F32)], compiler_params=_params("arbitrary"),
    )(xc, proj, wa, wx, ba, bx, lam)


def _lru_bwd(xc, proj, hs, dcat, wa, wx, ba, bx, lam, *, name):
    s_len, w = xc.shape
    t = _tile(s_len, (LRU_T,))
    nb = s_len // t
    tpb = t // SUBLANES

    def body(xc_ref, gr_ref, h_ref, hp_ref, dy_ref, wa_ref, wx_ref, ba_ref, bx_ref, lam_ref,
             dxc_ref, dgr_ref, dwa_ref, dwx_ref, dba_ref, dbx_ref, dlam_ref, carry):
        step = pl.program_id(0)
        params = (wa_ref[...], wx_ref[...], ba_ref[...], bx_ref[...], lam_ref[...])

        @pl.when(step == 0)
        def _():
            carry[...] = jnp.zeros_like(carry)
            for r in (dwa_ref, dwx_ref, dba_ref, dbx_ref, dlam_ref):
                r[...] = jnp.zeros_like(r)

        (a, _), vjp = jax.vjp(_lru_gates, xc_ref[...], *params)
        gr, h, dy = gr_ref[...], h_ref[...], dy_ref[...]
        gel, gel_vjp = jax.vjp(_gelu, gr)
        dgr_ref[...] = gel_vjp(dy * h)[0].astype(dgr_ref.dtype)
        dh = dy * gel
        a_cum, g0 = _block_scan(a, a * dh, True)
        big_g = g0 + a_cum * carry[0:1, :]
        row = _rows((t, w))
        g = dh + jnp.where(row == t - 1, carry[0:1, :], pltpu.roll(big_g, t - 1, 0))
        carry[0:1, :] = big_g[0:1, :]
        h_last = jnp.where(step < nb - 1, hp_ref[SUBLANES - 1:SUBLANES, :], 0.0)
        h_prev = jnp.where(row == 0, h_last, pltpu.roll(h, 1, 0))
        dxc, dwa, dwx, dba, dbx, dlam = vjp((g * h_prev, g))
        dxc_ref[...] = dxc
        dwa_ref[...] += dwa
        dwx_ref[...] += dwx
        dba_ref[...] += dba
        dbx_ref[...] += dbx
        dlam_ref[...] += dlam

    rev = lambda i: nb - 1 - i
    row = pl.BlockSpec((t, w), lambda i: (rev(i), 0))
    mat = pl.BlockSpec((w, w), lambda i: (0, 0))
    vec = pl.BlockSpec((1, w), lambda i: (0, 0))
    return pl.pallas_call(
        body, name=name, grid=(nb,),
        in_specs=[row, pl.BlockSpec((t, w), lambda i: (rev(i), PROJ_AB_BLOCKS - 1)), row,
                  pl.BlockSpec((SUBLANES, w), lambda i: (jnp.maximum(rev(i) * tpb - 1, 0), 0)),
                  pl.BlockSpec((t, w), lambda i: (rev(i), 1)), mat, mat, vec, vec, vec],
        out_specs=[row, row, mat, mat, vec, vec, vec],
        out_shape=[jax.ShapeDtypeStruct((s_len, w), F32), jax.ShapeDtypeStruct((s_len, w), MXU_DTYPE)]
        + [jax.ShapeDtypeStruct((w, w), F32)] * 2 + [jax.ShapeDtypeStruct((1, w), F32)] * 3,
        scratch_shapes=[pltpu.VMEM((SUBLANES, w), F32)], compiler_params=_params("arbitrary"),
    )(xc, proj, hs, hs, dcat, wa, wx, ba, bx, lam)


XA_T = 256
XA_SCALE = XA_HEAD_DIM ** -0.5


def _xa_heads(q, k, v):
    s = _bmm_nt(q, k) * XA_SCALE
    e = jnp.exp(s - jnp.max(s, axis=-1, keepdims=True))
    return _bmm(e / jnp.sum(e, axis=-1, keepdims=True), v)


def _xa_stack(ref):
    return jnp.stack([ref[:, h * XA_HEAD_DIM:(h + 1) * XA_HEAD_DIM].astype(F32) for h in range(XA_HEADS)], axis=0)


def _xa_fwd(q, kv, *, name):
    s_len, d = q.shape
    n_mem = kv.shape[0]
    t = _tile(s_len, (XA_T,))

    def body(q_ref, k_ref, v_ref, o_ref):
        o = _xa_heads(_xa_stack(q_ref), _xa_stack(k_ref), _xa_stack(v_ref))
        for h in range(XA_HEADS):
            o_ref[:, h * XA_HEAD_DIM:(h + 1) * XA_HEAD_DIM] = o[h].astype(o_ref.dtype)

    return pl.pallas_call(
        body, name=name, grid=(s_len // t,),
        in_specs=[pl.BlockSpec((t, d), lambda i: (i, 0)), pl.BlockSpec((n_mem, d), lambda i: (0, 0)),
                  pl.BlockSpec((n_mem, d), lambda i: (0, 1))],
        out_specs=pl.BlockSpec((t, d), lambda i: (i, 0)), out_shape=jax.ShapeDtypeStruct((s_len, d), MXU_DTYPE),
        compiler_params=_params("parallel"),
    )(q, kv, kv)


def _xa_bwd(q, kv, dy, wo, *, name):
    s_len, d = q.shape
    n_mem = kv.shape[0]
    t = _tile(s_len, (XA_T,))

    def body(q_ref, k_ref, v_ref, dy_ref, wo_ref, dq_ref, dk_ref, dv_ref):
        @pl.when(pl.program_id(0) == 0)
        def _():
            dk_ref[...] = jnp.zeros_like(dk_ref)
            dv_ref[...] = jnp.zeros_like(dv_ref)

        do = _dg(dy_ref[...], wo_ref[...], ((1,), (1,)))
        do = jnp.stack([do[:, h * XA_HEAD_DIM:(h + 1) * XA_HEAD_DIM] for h in range(XA_HEADS)], axis=0)
        _, vjp = jax.vjp(_xa_heads, _xa_stack(q_ref), _xa_stack(k_ref), _xa_stack(v_ref))
        dq, dk, dv = vjp(do)
        for h in range(XA_HEADS):
            sl = slice(h * XA_HEAD_DIM, (h + 1) * XA_HEAD_DIM)
            dq_ref[:, sl] = dq[h].astype(dq_ref.dtype)
            dk_ref[:, sl] += dk[h]
            dv_ref[:, sl] += dv[h]

    row = pl.BlockSpec((t, d), lambda i: (i, 0))
    dq, dk, dv = pl.pallas_call(
        body, name=name, grid=(s_len // t,),
        in_specs=[row, pl.BlockSpec((n_mem, d), lambda i: (0, 0)), pl.BlockSpec((n_mem, d), lambda i: (0, 1)), row,
                  pl.BlockSpec(wo.shape, lambda i: (0, 0), pipeline_mode=RESIDENT)],
        out_specs=[row, pl.BlockSpec((n_mem, d), lambda i: (0, 0)), pl.BlockSpec((n_mem, d), lambda i: (0, 0))],
        out_shape=[jax.ShapeDtypeStruct((s_len, d), MXU_DTYPE)] + [jax.ShapeDtypeStruct((n_mem, d), F32)] * 2,
        compiler_params=_params("arbitrary"),
    )(q, kv, kv, dy, wo)
    return dq, jnp.concatenate([dk, dv], axis=1)


DN_Q_SCALE = DN_HEAD_DIM ** -0.5
L2_EPS = 1e-6


def _bdg(a, b, ca, cb):
    return lax.dot_general(a.astype(MXU_DTYPE), b.astype(MXU_DTYPE), (((ca,), (cb,)), ((0,), (0,))), preferred_element_type=F32)


@jax.custom_vjp
def _bmm(a, b):
    return _bdg(a, b, 2, 1)


_bmm.defvjp(lambda a, b: (_bdg(a, b, 2, 1), (a, b)), lambda r, g: (_bdg(g, r[1], 2, 2), _bdg(r[0], g, 1, 1)))


@jax.custom_vjp
def _bmm_nt(a, b):
    return _bdg(a, b, 2, 2)


_bmm_nt.defvjp(lambda a, b: (_bdg(a, b, 2, 2), (a, b)), lambda r, g: (_bdg(g, r[1], 2, 1), _bdg(g, r[0], 1, 1)))


@jax.custom_vjp
def _bmm_tn(a, b):
    return _bdg(a, b, 1, 1)


_bmm_tn.defvjp(lambda a, b: (_bdg(a, b, 1, 1), (a, b)), lambda r, g: (_bdg(r[1], g, 2, 2), _bdg(r[0], g, 2, 1)))


def _tri_inverse(n):
    eye = (lax.broadcasted_iota(jnp.int32, n.shape, 1) == lax.broadcasted_iota(jnp.int32, n.shape, 2)).astype(F32)
    inv, p = eye - n, n
    for _ in range(5):
        p = _bdg(p, p, 2, 1)
        inv = _bdg(inv, eye + p, 2, 1)
    return inv


@jax.custom_vjp
def _tri_solve2(n, r1, r2):
    t = _tri_inverse(n)
    return _bdg(t, r1, 2, 1), _bdg(t, r2, 2, 1)


def _tri_solve2_fwd(n, r1, r2):
    t = _tri_inverse(n)
    x1, x2 = _bdg(t, r1, 2, 1), _bdg(t, r2, 2, 1)
    return (x1, x2), (t, x1, x2)


def _tri_solve2_bwd(saved, cts):
    t, x1, x2 = saved
    d1, d2 = _bdg(t, cts[0], 1, 1), _bdg(t, cts[1], 1, 1)
    return -(_bdg(d1, x1, 2, 2) + _bdg(d2, x2, 2, 2)), d1, d2


_tri_solve2.defvjp(_tri_solve2_fwd, _tri_solve2_bwd)


def _dn_gates(ab, alog, dtb):
    return -jnp.exp(alog) * _softplus(ab + dtb), _sigmoid(ab)


def _dn_heads(cq, ck, cv, z, g, beta, onorm, state):
    h, c, _ = cq.shape
    l2 = lambda t: t * lax.rsqrt(jnp.sum(t * t, axis=-1, keepdims=True) + L2_EPS)
    q, k, v = l2(_silu(cq)) * DN_Q_SCALE, l2(_silu(ck)), _silu(cv)
    r, cc = lax.broadcasted_iota(jnp.int32, (h, c, c), 1), lax.broadcasted_iota(jnp.int32, (h, c, c), 2)
    tri, eye = r >= cc, r == cc
    g_row = jnp.sum(jnp.where(eye, g, 0.0), axis=1, keepdims=True)
    gcum_c = jnp.sum(jnp.where(tri, g_row, 0.0), axis=2, keepdims=True)
    gcum_r = jnp.sum(jnp.where(cc >= r, g, 0.0), axis=1, keepdims=True)
    decay = jnp.where(tri, jnp.exp(jnp.where(tri, gcum_c - gcum_r, 0.0)), 0.0)
    kb = k * beta
    n = jnp.where(r > cc, _bmm_nt(kb, k) * decay, 0.0)
    u, w = _tri_solve2(n, v * beta, kb * jnp.exp(gcum_c))
    v_new = u - _bmm(w, state)
    o = _bmm(q * jnp.exp(gcum_c), state) + _bmm(_bmm_nt(q, k) * decay, v_new)
    g_last = jnp.sum(g, axis=1, keepdims=True)
    new_state = state * jnp.exp(g_last) + _bmm_tn(k * jnp.exp(g_last - gcum_c), v_new)
    on = o * lax.rsqrt(jnp.mean(o * o, axis=-1, keepdims=True) + NORM_EPS) * onorm
    return on * _silu(z), new_state


def _dn_stack(ref, col0):
    return jnp.stack([ref[:, col0 + h * DN_HEAD_DIM:col0 + (h + 1) * DN_HEAD_DIM].astype(F32) for h in range(DN_HEADS)], axis=0)


def _dn_cols(block, col0):
    return jnp.stack([block[:, col0 + h:col0 + h + 1] for h in range(DN_HEADS)], axis=0)


def _dn_fwd(cqkv, proj, ab, alog, dtb, onorm, *, name):
    s_len = cqkv.shape[0]
    c, hd, w = DN_CHUNK, DN_HEAD_DIM, DN_WIDTH
    n_chunks = s_len // c

    def body(c_ref, z_ref, ab_ref, alog_ref, dtb_ref, on_ref, o_ref, st_ref, state):
        @pl.when(pl.program_id(0) == 0)
        def _():
            state[...] = jnp.zeros_like(state)

        g_all, beta_all = _dn_gates(ab_ref[...], alog_ref[...], dtb_ref[...])
        st = state[...]
        st_ref[0] = st
        out, new = _dn_heads(_dn_stack(c_ref, 0), _dn_stack(c_ref, w), _dn_stack(c_ref, 2 * w), _dn_stack(z_ref, 0),
                             _dn_cols(g_all, 0), _dn_cols(beta_all, DN_HEADS), on_ref[...], st)
        state[...] = new
        for h in range(DN_HEADS):
            o_ref[:, h * hd:(h + 1) * hd] = out[h].astype(o_ref.dtype)

    vec = pl.BlockSpec((1, LANES), lambda i: (0, 0))
    return pl.pallas_call(
        body, name=name, grid=(n_chunks,),
        in_specs=[pl.BlockSpec((c, 3 * w), lambda i: (i, 0)), pl.BlockSpec((c, w), lambda i: (i, 3)),
                  pl.BlockSpec((c, LANES), lambda i: (i, 0)), vec, vec, vec],
        out_specs=[pl.BlockSpec((c, w), lambda i: (i, 0)), pl.BlockSpec((1, DN_HEADS, hd, hd), lambda i: (i, 0, 0, 0))],
        out_shape=[jax.ShapeDtypeStruct((s_len, w), MXU_DTYPE), jax.ShapeDtypeStruct((n_chunks, DN_HEADS, hd, hd), F32)],
        scratch_shapes=[pltpu.VMEM((DN_HEADS, hd, hd), F32)], compiler_params=_params("arbitrary"),
    )(cqkv, proj, ab, alog, dtb, onorm)


def _dn_bwd(cqkv, proj, ab, alog, dtb, onorm, states, dout, *, name):
    s_len = cqkv.shape[0]
    c, hd, w = DN_CHUNK, DN_HEAD_DIM, DN_WIDTH
    n_chunks = s_len // c

    def body(c_ref, z_ref, ab_ref, alog_ref, dtb_ref, on_ref, st_ref, do_ref,
             dc_ref, dz_ref, dab_ref, dalog_ref, ddtb_ref, don_ref, dstate):
        @pl.when(pl.program_id(0) == 0)
        def _():
            dstate[...] = jnp.zeros_like(dstate)
            for r in (dalog_ref, ddtb_ref, don_ref):
                r[...] = jnp.zeros_like(r)

        (g_all, beta_all), gates_vjp = jax.vjp(_dn_gates, ab_ref[...], alog_ref[...], dtb_ref[...])
        _, vjp = jax.vjp(_dn_heads, _dn_stack(c_ref, 0), _dn_stack(c_ref, w), _dn_stack(c_ref, 2 * w), _dn_stack(z_ref, 0),
                         _dn_cols(g_all, 0), _dn_cols(beta_all, DN_HEADS), on_ref[...], st_ref[0])
        dcq, dck, dcv, dz, dg, dbeta, don, dst = vjp((_dn_stack(do_ref, 0), dstate[...]))
        dstate[...] = dst
        col = _cols((c, LANES))
        dg_all, dbeta_all = jnp.zeros((c, LANES), F32), jnp.zeros((c, LANES), F32)
        for h in range(DN_HEADS):
            sl = slice(h * hd, (h + 1) * hd)
            dc_ref[:, sl] = dcq[h]
            dc_ref[:, w + h * hd:w + (h + 1) * hd] = dck[h]
            dc_ref[:, 2 * w + h * hd:2 * w + (h + 1) * hd] = dcv[h]
            dz_ref[:, sl] = dz[h].astype(dz_ref.dtype)
            dg_all = dg_all + jnp.where(col == h, dg[h], 0.0)
            dbeta_all = dbeta_all + jnp.where(col == DN_HEADS + h, dbeta[h], 0.0)
        dab, dalog, ddtb = gates_vjp((dg_all, dbeta_all))
        dab_ref[...] = dab
        dalog_ref[...] += dalog
        ddtb_ref[...] += ddtb
        don_ref[...] += don

    rev = lambda i: n_chunks - 1 - i
    vec = pl.BlockSpec((1, LANES), lambda i: (0, 0))
    return pl.pallas_call(
        body, name=name, grid=(n_chunks,),
        in_specs=[pl.BlockSpec((c, 3 * w), lambda i: (rev(i), 0)), pl.BlockSpec((c, w), lambda i: (rev(i), 3)),
                  pl.BlockSpec((c, LANES), lambda i: (rev(i), 0)), vec, vec, vec,
                  pl.BlockSpec((1, DN_HEADS, hd, hd), lambda i: (rev(i), 0, 0, 0)), pl.BlockSpec((c, w), lambda i: (rev(i), 0))],
        out_specs=[pl.BlockSpec((c, 3 * w), lambda i: (rev(i), 0)), pl.BlockSpec((c, w), lambda i: (rev(i), 0)),
                   pl.BlockSpec((c, LANES), lambda i: (rev(i), 0)), vec, vec, vec],
        out_shape=[jax.ShapeDtypeStruct((s_len, 3 * w), F32), jax.ShapeDtypeStruct((s_len, w), MXU_DTYPE),
                   jax.ShapeDtypeStruct((s_len, LANES), F32)] + [jax.ShapeDtypeStruct((1, LANES), F32)] * 3,
        scratch_shapes=[pltpu.VMEM((DN_HEADS, hd, hd), F32)], compiler_params=_params("arbitrary"),
    )(cqkv, proj, ab, alog, dtb, onorm, states, dout)


def _final_loss(x, g, target, *, name):
    s, d = x.shape
    tm = _tile(s, (512, 256))

    def body(x_ref, g_ref, t_ref, loss_ref, dx_ref, dg_ref):
        @pl.when(pl.program_id(0) == 0)
        def _():
            loss_ref[...] = jnp.zeros_like(loss_ref)
            dg_ref[...] = jnp.zeros_like(dg_ref)

        xv, gv = x_ref[...], g_ref[...]
        r = lax.rsqrt(jnp.mean(xv * xv, axis=-1, keepdims=True) + NORM_EPS)
        xh = xv * r
        err = xh * gv - t_ref[...]
        loss_ref[...] += 0.5 * jnp.sum(jnp.mean(err * err, axis=-1, keepdims=True), axis=0, keepdims=True)
        dy = err * (1.0 / d)
        dxh = dy * gv
        dx_ref[...] = r * (dxh - xh * jnp.mean(dxh * xh, axis=-1, keepdims=True))
        dg_ref[...] += jnp.sum(dy * xh, axis=0, keepdims=True)

    row = pl.BlockSpec((tm, d), lambda i: (i, 0))
    vec = pl.BlockSpec((1, d), lambda i: (0, 0))
    return pl.pallas_call(
        body, name=name, grid=(s // tm,), in_specs=[row, vec, row],
        out_specs=[pl.BlockSpec((1, LANES), lambda i: (0, 0)), row, vec],
        out_shape=[jax.ShapeDtypeStruct((1, LANES), F32), jax.ShapeDtypeStruct((s, d), F32), jax.ShapeDtypeStruct((1, d), F32)],
        compiler_params=_params("arbitrary"),
    )(x, g.reshape(1, d), target)


def _adamw(w, g, m, v, *, name):
    shape = w.shape
    cols = shape[-1]
    rows = max(w.size // cols, 1)
    tr = _tile(rows, (512, 352, 256, 128, 64, 32, 16, 8))
    c1, c2 = 1.0 - ADAM_B1 ** ADAM_STEP, 1.0 - ADAM_B2 ** ADAM_STEP

    def body(w_ref, g_ref, m_ref, v_ref, d_ref, nm_ref, nv_ref):
        gv = g_ref[...]
        nm = ADAM_B1 * m_ref[...] + (1.0 - ADAM_B1) * gv
        nv = ADAM_B2 * v_ref[...] + (1.0 - ADAM_B2) * (gv * gv)
        d_ref[...] = -ADAM_LR * ((nm / c1) / (jnp.sqrt(nv / c2) + ADAM_EPS) + ADAM_WD * w_ref[...])
        nm_ref[...] = nm
        nv_ref[...] = nv

    blk = pl.BlockSpec((tr, cols), lambda i: (i, 0))
    outs = pl.pallas_call(
        body, name=name, grid=(rows // tr,), in_specs=[blk] * 4, out_specs=[blk] * 3,
        out_shape=[jax.ShapeDtypeStruct((rows, cols), F32)] * 3, compiler_params=_params("parallel"),
    )(*(t.reshape(rows, cols) for t in (w, g, m, v)))
    return tuple(t.reshape(shape) for t in outs)


def _block_diag(w):
    n, j, k = w.shape
    eye = jnp.eye(n, dtype=w.dtype)
    return (eye[:, None, :, None] * w[:, :, None, :]).reshape(n * j, n * k)


def _block_diag_part(m, n):
    j, k = m.shape[0] // n, m.shape[1] // n
    m4 = m.reshape(n, j, n, k)
    return jnp.stack([m4[i, :, i, :] for i in range(n)], axis=0)


DN_AB = 2 * DN_HEADS
DEPTH = 2


def _row(v, width=None):
    v = v.reshape(1, -1)
    return v if width is None else jnp.pad(v, ((0, 0), (0, width - v.shape[1])))


def _conv_w8(conv_w, bias=None):
    w8 = jnp.zeros((SUBLANES, conv_w.shape[1]), F32).at[:CONV_K].set(conv_w)
    return w8 if bias is None else w8.at[CONV_K].set(bias)


def _mixer_ab_fwd(x, w, tag):
    h, (proj,) = _norm_mm(x, w["mix_norm"][0], [w["ab_w_in"][0]], [F32], name=f"{tag}_in")
    o, lse = _dattn_forward(proj, tag)
    w8 = _conv_w8(w["lru_conv_w"][0], w["lru_conv_b"][0])
    xc = _conv_fwd(proj, PROJ_AB_BLOCKS - 2, LRU_WIDTH, w8, name=f"{tag}_conv")
    wa, wx = _block_diag(w["lru_w_a"][0]), _block_diag(w["lru_w_x"][0])
    vecs = (_row(w["lru_b_a"][0]), _row(w["lru_b_x"][0]), _row(w["lru_lambda"][0]))
    hs, y = _lru_fwd(xc, proj, wa, wx, *vecs, name=f"{tag}_lru")
    w_out = w["ab_w_out"][0]
    x2 = _mm(o, w_out[:ATTN_WIDTH], res=x, name=f"{tag}_out_attn")
    x2 = _mm(y, w_out[ATTN_WIDTH:], res=x2, name=f"{tag}_out_lru")
    return x2, (x, h, proj, o, lse, w8, xc, wa, wx, vecs, hs, y)


def _mixer_ab_bwd(saved, w, dy, tag):
    x, h, proj, o, lse, w8, xc, wa, wx, vecs, hs, y = saved
    w_out = w["ab_w_out"][0]
    dcat = _mm(dy, w_out, mode="nt", name=f"{tag}_dcat")
    dw_out = jnp.concatenate([_mm(o, dy, mode="tn", name=f"{tag}_dwout_attn"), _mm(y, dy, mode="tn", name=f"{tag}_dwout_lru")], axis=0)
    dq, dk, dv = _dattn_backward(proj, o, lse, dcat, tag)
    dxc, dgr, dwa, dwx, dba, dbx, dlam = _lru_bwd(xc, proj, hs, dcat, wa, wx, *vecs, name=f"{tag}_dlru")
    dxr, dw8 = _conv_bwd(proj, PROJ_AB_BLOCKS - 2, LRU_WIDTH, w8, dxc, name=f"{tag}_dconv")
    dproj = jnp.concatenate([t.astype(MXU_DTYPE) for t in (dq, dk, dv, dxr, dgr)], axis=1)
    dw_in = _mm(h, dproj, mode="tn", name=f"{tag}_dwin")
    dx, dg = _mm_rms_bwd([(dproj, w["ab_w_in"][0])], x, w["mix_norm"][0], dy, name=f"{tag}_dh")
    grads = dict(mix_norm=dg[0], ab_w_in=dw_in, ab_w_out=dw_out, lru_conv_w=dw8[:CONV_K], lru_conv_b=dw8[CONV_K],
                 lru_w_a=_block_diag_part(dwa, LRU_BLOCKS), lru_b_a=dba[0], lru_w_x=_block_diag_part(dwx, LRU_BLOCKS),
                 lru_b_x=dbx[0], lru_lambda=dlam[0])
    return dx, grads


def _dn_split_w(w_in):
    return w_in[:, :4 * DN_WIDTH], jnp.pad(w_in[:, 4 * DN_WIDTH:], ((0, 0), (0, LANES - DN_AB)))


def _mixer_dn_fwd(x, w, tag):
    w_qkvz, w_ab = _dn_split_w(w["dn_w_in"][0])
    h, (proj, ab) = _norm_mm(x, w["mix_norm"][1], [w_qkvz, w_ab], [F32, F32], name=f"{tag}_in")
    w8 = _conv_w8(w["dn_conv_w"][0])
    cqkv = _conv_fwd(proj, 0, 3 * DN_WIDTH, w8, name=f"{tag}_conv")
    vecs = (_row(w["dn_a_log"][0], LANES), _row(w["dn_dt_bias"][0], LANES), _row(w["dn_o_norm"][0]))
    og, states = _dn_fwd(cqkv, proj, ab, *vecs, name=f"{tag}_dn")
    x2 = _mm(og, w["dn_w_out"][0], res=x, name=f"{tag}_out")
    return x2, (x, h, w_qkvz, w_ab, proj, ab, w8, cqkv, vecs, og, states)


def _mixer_dn_bwd(saved, w, dy, tag):
    x, h, w_qkvz, w_ab, proj, ab, w8, cqkv, vecs, og, states = saved
    dout = _mm(dy, w["dn_w_out"][0], mode="nt", name=f"{tag}_dout")
    dw_out = _mm(og, dy, mode="tn", name=f"{tag}_dwout")
    dcqkv, dz, dab, dalog, ddtb, don = _dn_bwd(cqkv, proj, ab, *vecs, states, dout, name=f"{tag}_ddn")
    dqkv, dw8 = _conv_bwd(proj, 0, 3 * DN_WIDTH, w8, dcqkv, name=f"{tag}_dconv")
    dproj = jnp.concatenate([dqkv.astype(MXU_DTYPE), dz.astype(MXU_DTYPE)], axis=1)
    dw_in = jnp.concatenate([_mm(h, dproj, mode="tn", name=f"{tag}_dwin"),
                             _mm(h, dab, mode="tn", name=f"{tag}_dwin_ab")[:, :DN_AB]], axis=1)
    dx, dg = _mm_rms_bwd([(dproj, w_qkvz), (dab, w_ab)], x, w["mix_norm"][1], dy, name=f"{tag}_dh")
    grads = dict(mix_norm=dg[0], dn_w_in=dw_in, dn_w_out=dw_out, dn_conv_w=dw8[:CONV_K], dn_a_log=dalog[0, :DN_HEADS],
                 dn_dt_bias=ddtb[0, :DN_HEADS], dn_o_norm=don[0])
    return dx, grads


def _xa_layer_fwd(x, mem, w, layer, tag):
    hq, (q,) = _norm_mm(x, w["xa_norm"][layer], [w["xa_wq"][layer]], [MXU_DTYPE], name=f"{tag}_q")
    hm = _rms_fwd(mem, w["xa_mem_norm"][layer], name=f"{tag}_mem_norm")
    kv = _mm(hm, w["xa_wkv"][layer], name=f"{tag}_kv")
    oa = _xa_fwd(q, kv, name=f"{tag}_core")
    x2 = _mm(oa, w["xa_wo"][layer], res=x, name=f"{tag}_out")
    return x2, (x, hq, q, hm, kv, oa)


def _xa_layer_bwd(saved, mem, w, layer, dy, tag):
    x, hq, q, hm, kv, oa = saved
    dwo = _mm(oa, dy, mode="tn", name=f"{tag}_dwo")
    dq, dkv = _xa_bwd(q, kv, dy, w["xa_wo"][layer], name=f"{tag}_dcore")
    dwq = _mm(hq, dq, mode="tn", name=f"{tag}_dwq")
    dx, dg = _mm_rms_bwd([(dq, w["xa_wq"][layer])], x, w["xa_norm"][layer], dy, name=f"{tag}_dhq")
    dwkv = _mm(hm, dkv, mode="tn", name=f"{tag}_dwkv")
    dhm = _mm(dkv, w["xa_wkv"][layer], mode="nt", name=f"{tag}_dhm")
    _, dgm = _rms_bwd(mem, w["xa_mem_norm"][layer], dhm, jnp.zeros_like(mem), name=f"{tag}_dmem_norm")
    return dx, dict(xa_norm=dg[0], xa_mem_norm=dgm[0], xa_wq=dwq, xa_wkv=dwkv, xa_wo=dwo)


def _local_step(x, mem, target, w):
    saved = []
    for layer in range(DEPTH):
        t = f"l{layer}"
        x, s1 = _ffn_fwd(x, w["ffn1_norm"], w["ffn1_w_in"], w["ffn1_w_out"], layer, f"{t}_ffn1")
        x, s2 = (_mixer_ab_fwd if layer % 2 == 0 else _mixer_dn_fwd)(x, w, f"{t}_mix")
        x, s3 = _xa_layer_fwd(x, mem, w, layer, f"{t}_xa")
        x, s4 = _ffn_fwd(x, w["ffn2_norm"], w["ffn2_w_in"], w["ffn2_w_out"], layer, f"{t}_ffn2")
        saved.append((s1, s2, s3, s4))
    loss, dx, dgf = _final_loss(x, w["final_norm"], target, name="final_loss")
    per_layer = []
    for layer in reversed(range(DEPTH)):
        t = f"l{layer}"
        s1, s2, s3, s4 = saved[layer]
        g = {}
        dx, g["ffn2_norm"], g["ffn2_w_in"], g["ffn2_w_out"] = _ffn_bwd(
            s4, w["ffn2_norm"], w["ffn2_w_in"], w["ffn2_w_out"], layer, dx, f"{t}_ffn2")
        dx, gx = _xa_layer_bwd(s3, mem, w, layer, dx, f"{t}_xa")
        dx, gm = (_mixer_ab_bwd if layer % 2 == 0 else _mixer_dn_bwd)(s2, w, dx, f"{t}_mix")
        dx, g["ffn1_norm"], g["ffn1_w_in"], g["ffn1_w_out"] = _ffn_bwd(
            s1, w["ffn1_norm"], w["ffn1_w_in"], w["ffn1_w_out"], layer, dx, f"{t}_ffn1")
        per_layer.append({**g, **gx, **gm})
    per_layer.reverse()
    grads = {"final_norm": [dgf[0]]}
    for layer_grads in per_layer:
        for name, value in layer_grads.items():
            grads.setdefault(name, []).append(value)
    return loss, dx, grads


N_CHIPS = 4
WIRE_DTYPE = jnp.bfloat16
HBM_SPEC = pl.BlockSpec(memory_space=pltpu.HBM)
PACK_COLS = 1024


def _place():
    x, y, c = lax.axis_index("x"), lax.axis_index("y"), lax.axis_index("c")
    return x, y, c, [(1 - x, y), (x, 1 - y), (1 - x, 1 - y)]


def _remote(src, dst, sems, k, to):
    return pltpu.make_async_remote_copy(src_ref=src, dst_ref=dst, send_sem=sems[0].at[k], recv_sem=sems[1].at[k],
                                        device_id=to, device_id_type=MESH)


def _gather_weights(blocks, axes):
    n = len(blocks)
    split = [b.shape[1] % 32 == 0 for b in blocks]

    def full_shape(i):
        l, r, c = blocks[i].shape
        return (l, N_CHIPS * r, c) if axes[i] == 1 else (l, r, N_CHIPS * c)

    def body(*refs):
        ins, outs = refs[:n], refs[n:2 * n]
        send_sems, recv_sems = refs[2 * n:]
        x, y, c, chips = _place()
        sems = (send_sems, recv_sems)
        sibling = (x, y, 1 - c)

        def window(i, k, h):
            l, r, cc = blocks[i].shape
            r0, nr = (0, r) if h is None else (h * (r // 2), r // 2)
            if axes[i] == 1:
                return outs[i].at[:, pl.ds(k * r + r0, nr), :]
            return outs[i].at[:, pl.ds(r0, nr), pl.ds(k * cc, cc)]

        def mine(i, h):
            r = blocks[i].shape[1]
            return ins[i] if h is None else ins[i].at[:, pl.ds(h * (r // 2), r // 2), :]

        me = 2 * x + y
        first, passed = [], []
        for i in range(n):
            h = c if split[i] else None
            for j, chip in enumerate(chips):
                first.append(_remote(mine(i, h), window(i, me, h), sems, 3 * i + j, (*chip, c)))
        first += [_remote(ins[i], window(i, me, None), sems, 6 * n + i, sibling) for i in range(n)]
        for cp in first:
            cp.start()
        for i in range(n):
            h = c if split[i] else None
            for j, (cx, cy) in enumerate(chips):
                got = window(i, 2 * cx + cy, h)
                _remote(got, got, sems, 3 * i + j, (cx, cy, c)).wait_recv()
                if split[i]:
                    passed.append(_remote(got, got, sems, 3 * (n + i) + j, sibling))
                    passed[-1].start()
        for i in range(n):
            if split[i]:
                for j, (cx, cy) in enumerate(chips):
                    got = window(i, 2 * cx + cy, 1 - c)
                    _remote(got, got, sems, 3 * (n + i) + j, sibling).wait_recv()
        for i in range(n):
            _remote(ins[i], window(i, me, None), sems, 6 * n + i, sibling).wait_recv()
        for cp in first + passed:
            cp.wait_send()

    return pl.pallas_call(
        body, name="gather_weights", in_specs=[HBM_SPEC] * n, out_specs=[HBM_SPEC] * n,
        out_shape=[jax.ShapeDtypeStruct(full_shape(i), blocks[i].dtype) for i in range(n)],
        scratch_shapes=[pltpu.SemaphoreType.DMA((7 * n,)), pltpu.SemaphoreType.DMA((7 * n,))],
    )(*blocks)


def _allreduce_small(v):
    rows, cols = v.shape
    n_dev = 2 * N_CHIPS

    def body(v_ref, out_ref, all_ref, send_sems, recv_sems, local_sem):
        x, y, c, chips = _place()
        sems = (send_sems, recv_sems)
        me, sibling = (x, y, c), (x, y, 1 - c)
        slot = lambda px, py, pc: all_ref.at[pl.ds((4 * px + 2 * py + pc) * rows, rows), :]
        mine = pltpu.make_async_copy(v_ref, slot(*me), local_sem)
        mine.start()
        first = [_remote(v_ref, slot(*me), sems, 0, sibling)]
        first += [_remote(v_ref, slot(*me), sems, 1 + j, (*chip, c)) for j, chip in enumerate(chips)]
        for cp in first:
            cp.start()
        passed = [_remote(slot(*chip, c), slot(*chip, c), sems, 4 + j, sibling) for j, chip in enumerate(chips)]
        for j, chip in enumerate(chips):
            _remote(slot(*chip, c), slot(*chip, c), sems, 1 + j, me).wait_recv()
            passed[j].start()
        _remote(slot(*sibling), slot(*sibling), sems, 0, me).wait_recv()
        for j, chip in enumerate(chips):
            _remote(slot(*chip, 1 - c), slot(*chip, 1 - c), sems, 4 + j, me).wait_recv()
        for cp in first + passed:
            cp.wait_send()
        mine.wait()
        acc = all_ref[pl.ds(0, rows), :]
        for k in range(1, n_dev):
            acc = acc + all_ref[pl.ds(k * rows, rows), :]
        out_ref[...] = acc

    vmem = pl.BlockSpec(memory_space=pltpu.VMEM)
    return pl.pallas_call(
        body, name="allreduce_small", in_specs=[vmem], out_specs=vmem, out_shape=jax.ShapeDtypeStruct((rows, cols), F32),
        scratch_shapes=[pltpu.VMEM((n_dev * rows, cols), F32), pltpu.SemaphoreType.DMA((7,)), pltpu.SemaphoreType.DMA((7,)),
                        pltpu.SemaphoreType.DMA],
    )(v)


def _swap_other_half(g4):
    n, _, rows, cols = g4.shape

    def body(v_ref, out_ref, send_sems, recv_sems):
        x, y, c, _ = _place()
        cp = _remote(v_ref.at[:, 1 - c], out_ref, (send_sems, recv_sems), 0, (x, y, 1 - c))
        cp.start()
        cp.wait()

    return pl.pallas_call(
        body, name="reduce_swap", in_specs=[HBM_SPEC], out_specs=HBM_SPEC, out_shape=jax.ShapeDtypeStruct((n, rows, cols), g4.dtype),
        scratch_shapes=[pltpu.SemaphoreType.DMA((1,)), pltpu.SemaphoreType.DMA((1,))],
    )(g4)


def _add_kept_half(g4, got):
    n, _, rows, cols = g4.shape
    tr = _tile(rows, (256, 128, 64, 32, 16))
    nb = rows // tr

    def body(c_ref, a_ref, b_ref, o_ref):
        o_ref[...] = (a_ref[...] + b_ref[...]).astype(o_ref.dtype)

    return pl.pallas_call(
        body, name="reduce_sum_cores",
        grid_spec=pltpu.PrefetchScalarGridSpec(
            num_scalar_prefetch=1, grid=(n, nb),
            in_specs=[pl.BlockSpec((None, None, tr, cols), lambda k, i, c_ref: (k, c_ref[0], i, 0)),
                      pl.BlockSpec((None, tr, cols), lambda k, i, c_ref: (k, i, 0))],
            out_specs=pl.BlockSpec((None, tr, cols), lambda k, i, c_ref: (k, i, 0))),
        out_shape=jax.ShapeDtypeStruct((n, rows, cols), WIRE_DTYPE), compiler_params=_params("parallel", "parallel"),
    )(lax.axis_index("c").astype(jnp.int32).reshape(1), g4, got)


def _exchange_chips(v):
    def body(v_ref, out_ref, send_sems, recv_sems):
        x, y, c, chips = _place()
        sems = (send_sems, recv_sems)
        sends = [_remote(v_ref.at[2 * cx + cy], out_ref.at[j], sems, j, (cx, cy, c)) for j, (cx, cy) in enumerate(chips)]
        for cp in sends:
            cp.start()
        for j, (cx, cy) in enumerate(chips):
            _remote(v_ref.at[0], out_ref.at[j], sems, j, (cx, cy, c)).wait_recv()
        for cp in sends:
            cp.wait_send()

    return pl.pallas_call(
        body, name="exchange_chips", in_specs=[HBM_SPEC], out_specs=HBM_SPEC,
        out_shape=jax.ShapeDtypeStruct((N_CHIPS - 1,) + v.shape[1:], v.dtype),
        scratch_shapes=[pltpu.SemaphoreType.DMA((3,)), pltpu.SemaphoreType.DMA((3,))],
    )(v)


def _swap_sibling(v):
    def body(v_ref, out_ref, send_sems, recv_sems):
        x, y, c, _ = _place()
        cp = _remote(v_ref, out_ref, (send_sems, recv_sems), 0, (x, y, 1 - c))
        cp.start()
        cp.wait()

    return pl.pallas_call(
        body, name="share_halves", in_specs=[HBM_SPEC], out_specs=HBM_SPEC, out_shape=jax.ShapeDtypeStruct(v.shape, v.dtype),
        scratch_shapes=[pltpu.SemaphoreType.DMA((1,)), pltpu.SemaphoreType.DMA((1,))],
    )(v)


def _sum_chips(own4, parts):
    _, rows, cols = own4.shape
    tr = _tile(rows, (256, 128, 64, 32, 16))

    def body(me_ref, own_ref, p0_ref, p1_ref, p2_ref, o_ref):
        acc = own_ref[...].astype(F32)
        for r in (p0_ref, p1_ref, p2_ref):
            acc = acc + r[...].astype(F32)
        o_ref[...] = acc

    part = lambda j: pl.BlockSpec((None, tr, cols), lambda i, me_ref: (j, i, 0))
    chip = (2 * lax.axis_index("x") + lax.axis_index("y")).astype(jnp.int32).reshape(1)
    return pl.pallas_call(
        body, name="reduce_sum_chips",
        grid_spec=pltpu.PrefetchScalarGridSpec(
            num_scalar_prefetch=1, grid=(rows // tr,),
            in_specs=[pl.BlockSpec((None, tr, cols), lambda i, me_ref: (me_ref[0], i, 0)), part(0), part(1), part(2)],
            out_specs=pl.BlockSpec((tr, cols), lambda i, me_ref: (i, 0))),
        out_shape=jax.ShapeDtypeStruct((rows, cols), F32), compiler_params=_params("parallel"),
    )(chip, own4, parts, parts, parts)


def _reduce_grads(g4):
    chip_sum = _add_kept_half(g4, _swap_other_half(g4))
    half = _sum_chips(chip_sum, _exchange_chips(chip_sum))
    other = _swap_sibling(half)
    return jnp.where(lax.axis_index("c") == 0, jnp.stack([half, other]), jnp.stack([other, half]))


BIG = (("ffn1_w_in", 2), ("ffn1_w_out", 1), ("xa_wq", 1), ("xa_wkv", 2), ("xa_wo", 1), ("ffn2_w_in", 2), ("ffn2_w_out", 1),
       ("ab_w_in", 2), ("ab_w_out", 1), ("dn_w_in", 2), ("dn_w_out", 1))
TINY_SHARDED = (("lru_conv_w", 2), ("dn_conv_w", 2))
REPLICATED = ("ffn1_norm", "mix_norm", "xa_norm", "xa_mem_norm", "ffn2_norm", "lru_conv_b", "lru_w_a", "lru_b_a", "lru_w_x",
              "lru_b_x", "lru_lambda", "dn_a_log", "dn_dt_bias", "dn_o_norm", "final_norm")
WEIGHTS = ("ffn1_norm", "ffn1_w_in", "ffn1_w_out", "mix_norm", "xa_norm", "xa_mem_norm", "xa_wq", "xa_wkv", "xa_wo", "ffn2_norm",
           "ffn2_w_in", "ffn2_w_out", "ab_w_in", "lru_conv_w", "lru_conv_b", "lru_w_a", "lru_b_a", "lru_w_x", "lru_b_x",
           "lru_lambda", "ab_w_out", "dn_w_in", "dn_conv_w", "dn_a_log", "dn_dt_bias", "dn_o_norm", "dn_w_out", "final_norm")


def _pad_rows(flat, row_multiple):
    n = flat.shape[-1]
    per = row_multiple * PACK_COLS
    total = -(-n // per) * per
    flat = jnp.pad(flat, [(0, 0)] * (flat.ndim - 1) + [(0, total - n)])
    return flat.reshape(flat.shape[:-1] + (total // PACK_COLS, PACK_COLS))


def _lane_padded(shape):
    return shape[:-1] + (-(-shape[-1] // LANES) * LANES,)


def _pad_lanes(t):
    return jnp.pad(t, [(0, 0)] * (t.ndim - 1) + [(0, _lane_padded(t.shape)[-1] - t.shape[-1])])


def _gather_full(shards):
    named = BIG + TINY_SHARDED
    blocks = [_pad_lanes(shards[n]).astype(MXU_DTYPE) if (n, a) in BIG else shards[n] for n, a in named]
    out = dict(zip((n for n, _ in named), _gather_weights(blocks, [a for _, a in named])))
    for n, axis in BIG:
        width, padded = shards[n].shape[-1], _lane_padded(shards[n].shape)[-1]
        if padded != width:
            assert axis == 2
            out[n] = jnp.concatenate([out[n][..., k * padded:k * padded + width] for k in range(N_CHIPS)], axis=-1)
    return out


def _to_blocks(pieces, axis):
    width = pieces[0].shape[axis - 1] // N_CHIPS
    block = lambda p, k: _pad_lanes(lax.slice_in_dim(p, k * width, (k + 1) * width, axis=axis - 1)).reshape(-1)
    return jnp.stack([jnp.concatenate([block(p, k) for p in pieces]) for k in range(N_CHIPS)], axis=0)


def _pack_grads(grads):
    flat = jnp.concatenate([_to_blocks(grads[n], axis) for n, axis in BIG], axis=1)
    g = _pad_rows(flat, 2 * 256)
    return g.reshape(N_CHIPS, 2, g.shape[1] // 2, PACK_COLS)


def _unpack_grads(reduced, shards):
    flat = reduced.reshape(-1)
    out, off = {}, 0
    for n, _ in BIG:
        shape = shards[n].shape
        size = math.prod(_lane_padded(shape))
        out[n] = flat[off:off + size].reshape(_lane_padded(shape))[..., :shape[-1]]
        off += size
    return out


def _pack_small(grads, loss):
    parts = [p.reshape(-1) for n in REPLICATED + tuple(n for n, _ in TINY_SHARDED) for p in grads[n]] + [loss[0, :1]]
    flat = jnp.concatenate(parts)
    total = -(-flat.shape[0] // (SUBLANES * LANES)) * SUBLANES * LANES
    return jnp.pad(flat, (0, total - flat.shape[0])).reshape(-1, LANES)


def _unpack_small(summed, shards, chip):
    flat = summed.reshape(-1)
    out, off = {}, 0
    for n in REPLICATED:
        out[n] = flat[off:off + shards[n].size].reshape(shards[n].shape)
        off += shards[n].size
    for n, axis in TINY_SHARDED:
        width = shards[n].shape[axis]
        shape = shards[n].shape[:axis] + (N_CHIPS * width,) + shards[n].shape[axis + 1:]
        full = flat[off:off + N_CHIPS * shards[n].size].reshape(shape)
        out[n] = lax.dynamic_slice_in_dim(full, chip * width, width, axis=axis)
        off += N_CHIPS * shards[n].size
    return out, flat[off]


def kernel(x, mem, ffn1_norm, ffn1_w_in, ffn1_w_out, mix_norm, xa_norm, xa_mem_norm, xa_wq, xa_wkv, xa_wo, ffn2_norm,
           ffn2_w_in, ffn2_w_out, ab_w_in, lru_conv_w, lru_conv_b, lru_w_a, lru_b_a, lru_w_x, lru_b_x, lru_lambda,
           ab_w_out, dn_w_in, dn_conv_w, dn_a_log, dn_dt_bias, dn_o_norm, dn_w_out, final_norm, loss_target,
           m_ffn1_norm, m_ffn1_w_in, m_ffn1_w_out, m_mix_norm, m_xa_norm, m_xa_mem_norm, m_xa_wq, m_xa_wkv, m_xa_wo,
           m_ffn2_norm, m_ffn2_w_in, m_ffn2_w_out, m_ab_w_in, m_lru_conv_w, m_lru_conv_b, m_lru_w_a, m_lru_b_a,
           m_lru_w_x, m_lru_b_x, m_lru_lambda, m_ab_w_out, m_dn_w_in, m_dn_conv_w, m_dn_a_log, m_dn_dt_bias,
           m_dn_o_norm, m_dn_w_out, m_final_norm, v_ffn1_norm, v_ffn1_w_in, v_ffn1_w_out, v_mix_norm, v_xa_norm,
           v_xa_mem_norm, v_xa_wq, v_xa_wkv, v_xa_wo, v_ffn2_norm, v_ffn2_w_in, v_ffn2_w_out, v_ab_w_in,
           v_lru_conv_w, v_lru_conv_b, v_lru_w_a, v_lru_b_a, v_lru_w_x, v_lru_b_x, v_lru_lambda, v_ab_w_out,
           v_dn_w_in, v_dn_conv_w, v_dn_a_log, v_dn_dt_bias, v_dn_o_norm, v_dn_w_out, v_final_norm):
    given = dict(locals())
    shards = {n: given[n] for n in WEIGHTS}
    chip = 2 * lax.axis_index("x") + lax.axis_index("y")

    full = {n: shards[n] for n in REPLICATED}
    full.update(_gather_full(shards))
    loss, grad_x, grads = _local_step(x[0], mem[0], loss_target[0], full)

    small, loss_sum = _unpack_small(_allreduce_small(_pack_small(grads, loss)), shards, chip)
    grad = {**small, **_unpack_grads(_reduce_grads(_pack_grads(grads)), shards)}

    delta, new_m, new_v = {}, {}, {}
    for n in WEIGHTS:
        delta[n], new_m[n], new_v[n] = _adamw(shards[n], grad[n], given["m_" + n], given["v_" + n], name=f"adamw_{n}")
    return (loss_sum, grad_x[None], *[grad[n] for n in WEIGHTS], *[delta[n] for n in WEIGHTS],
            *[new_m[n] for n in WEIGHTS], *[new_v[n] for n in WEIGHTS])
```

```python
import math

import jax
import jax.numpy as jnp
from jax import lax
from jax.experimental import pallas as pl
from jax.experimental.pallas import tpu as pltpu

F32 = jnp.float32
MXU_DTYPE = jnp.bfloat16
VMEM_LIMIT_BYTES = 48 * 1024 * 1024
MM_BLOCK_BYTES = 8 * 1024 * 1024
LANES = 128
SUBLANES = 8

NORM_EPS = 1e-6
CONV_K = 4
ATTN_PAIRS = 4
ATTN_HEAD_DIM = 64
ATTN_WIDTH = 512
ATTN_BLOCK = 128
DILATIONS = (1, 4, 16)
LRU_WIDTH = 512
LRU_BLOCKS = 8
LRU_C = 8.0
DN_HEADS = 8
DN_HEAD_DIM = 128
DN_WIDTH = 1024
DN_CHUNK = 64
XA_HEADS = 4
XA_HEAD_DIM = 256
D_FF = 2816
ADAM_LR, ADAM_B1, ADAM_B2, ADAM_EPS, ADAM_WD, ADAM_STEP = 0.001, 0.9, 0.999, 1e-08, 0.01, 10

MESH = pl.DeviceIdType.MESH


def _tile(n, prefs):
    for p in prefs:
        if n % p == 0:
            return p
    return n


def _params(*sem):
    return pltpu.CompilerParams(dimension_semantics=sem, vmem_limit_bytes=VMEM_LIMIT_BYTES)


def _dg(a, b, dims, hi=False):
    if hi:
        return lax.dot_general(a, b, (dims, ((), ())), precision=lax.Precision.HIGHEST, preferred_element_type=F32)
    return lax.dot_general(a.astype(MXU_DTYPE), b.astype(MXU_DTYPE), (dims, ((), ())), preferred_element_type=F32)


@jax.custom_vjp
def _bdot(a, b):
    return _dg(a, b, ((1,), (0,)))


def _bdot_fwd(a, b):
    return _bdot(a, b), (a, b)


def _bdot_bwd(r, g):
    a, b = r
    return _dg(g, b, ((1,), (1,))).astype(a.dtype), _dg(a, g, ((0,), (0,))).astype(b.dtype)


_bdot.defvjp(_bdot_fwd, _bdot_bwd)


def _log1p(t):
    return jnp.where(t < 0.01, t * (1.0 - t * (0.5 - t * (1.0 / 3.0))), jnp.log(1.0 + t))


def _neg_expm1(y):
    series = -y * (1.0 + 0.5 * y * (1.0 + (1.0 / 3.0) * y * (1.0 + 0.25 * y)))
    return jnp.where(y > -0.01, series, 1.0 - jnp.exp(y))


def _softplus(x):
    return jnp.maximum(x, 0.0) + _log1p(jnp.exp(-jnp.abs(x)))


def _sigmoid(x):
    return 0.5 * jnp.tanh(0.5 * x) + 0.5


def _silu(x):
    return x * _sigmoid(x)


def _gelu(x):
    return 0.5 * x * (1.0 + jnp.tanh(0.7978845608028654 * (x + 0.044715 * x * x * x)))


def _rows(shape):
    return lax.broadcasted_iota(jnp.int32, shape, 0)


def _cols(shape):
    return lax.broadcasted_iota(jnp.int32, shape, 1)


def _mm(a, b, *, mode="nn", out_dtype=F32, res=None, scale=1.0, name):
    if mode == "nn":
        (m, k), (k2, n) = a.shape, b.shape
    elif mode == "nt":
        (m, k), (n, k2) = a.shape, b.shape
    else:
        (k, m), (k2, n) = a.shape, b.shape
    assert k == k2, (a.shape, b.shape, mode)
    if mode == "tn":
        tm, tn, tk = _tile(m, (1024, 512, 256, 128)), _tile(n, (1024, 512, 256, 128)), _tile(k, (2048, 1024, 512, 256))
    else:
        tm, tn = _tile(m, (512, 256, 128)), _tile(n, (1024, 512, 256, 128))
        tk = k if k * tn * 2 <= MM_BLOCK_BYTES else _tile(k, (1024, 512, 256, 128))
    nk = k // tk
    dims = {"nn": ((1,), (0,)), "nt": ((1,), (1,)), "tn": ((0,), (0,))}[mode]

    def body(*refs):
        a_ref, b_ref = refs[:2]
        r_ref = refs[2] if res is not None else None
        o_ref = refs[3 if res is not None else 2]

        def finish(r):
            if scale != 1.0:
                r = r * scale
            if res is not None:
                r = r_ref[...] + r
            o_ref[...] = r.astype(out_dtype)

        if nk == 1:
            finish(_dg(a_ref[...], b_ref[...], dims))
            return
        acc = refs[-1]
        kk = pl.program_id(2)

        @pl.when(kk == 0)
        def _():
            acc[...] = jnp.zeros_like(acc)

        acc[...] += _dg(a_ref[...], b_ref[...], dims)

        @pl.when(kk == nk - 1)
        def _():
            finish(acc[...])

    a_spec = pl.BlockSpec((tk, tm), lambda i, j, kk: (kk, i)) if mode == "tn" else pl.BlockSpec((tm, tk), lambda i, j, kk: (i, kk))
    b_spec = pl.BlockSpec((tn, tk), lambda i, j, kk: (j, kk)) if mode == "nt" else pl.BlockSpec((tk, tn), lambda i, j, kk: (kk, j))
    o_spec = pl.BlockSpec((tm, tn), lambda i, j, kk: (i, j))
    in_specs = [a_spec, b_spec] + ([o_spec] if res is not None else [])
    args = (a, b) + ((res,) if res is not None else ())
    return pl.pallas_call(
        body, name=name, grid=(m // tm, n // tn, nk), in_specs=in_specs, out_specs=o_spec,
        out_shape=jax.ShapeDtypeStruct((m, n), out_dtype), scratch_shapes=[pltpu.VMEM((tm, tn), F32)] if nk > 1 else [],
        compiler_params=_params("parallel", "parallel", "arbitrary"),
    )(*args)


def _rms_fwd(x, g, *, name):
    s, d = x.shape
    tm = _tile(s, (512, 256))

    def body(x_ref, g_ref, o_ref):
        xv = x_ref[...]
        r = lax.rsqrt(jnp.mean(xv * xv, axis=-1, keepdims=True) + NORM_EPS)
        o_ref[...] = (xv * r * g_ref[...]).astype(o_ref.dtype)

    return pl.pallas_call(
        body, name=name, grid=(s // tm,),
        in_specs=[pl.BlockSpec((tm, d), lambda i: (i, 0)), pl.BlockSpec((1, d), lambda i: (0, 0))],
        out_specs=pl.BlockSpec((tm, d), lambda i: (i, 0)), out_shape=jax.ShapeDtypeStruct((s, d), MXU_DTYPE),
        compiler_params=_params("parallel"),
    )(x, g.reshape(1, d))


def _norm_mm(x, g, ws, out_dtypes, *, name):
    s, d = x.shape
    tm = _tile(s, (512, 256))
    nw = len(ws)

    def body(*refs):
        x_ref, g_ref = refs[:2]
        h_ref = refs[2 + nw]
        xv = x_ref[...]
        r = lax.rsqrt(jnp.mean(xv * xv, axis=-1, keepdims=True) + NORM_EPS)
        h = (xv * r * g_ref[...]).astype(MXU_DTYPE)
        h_ref[...] = h
        for w_ref, o_ref in zip(refs[2:2 + nw], refs[3 + nw:]):
            o_ref[...] = _dg(h, w_ref[...], ((1,), (0,))).astype(o_ref.dtype)

    row = lambda w: pl.BlockSpec((tm, w), lambda i: (i, 0))
    outs = pl.pallas_call(
        body, name=name, grid=(s // tm,),
        in_specs=[row(d), pl.BlockSpec((1, d), lambda i: (0, 0))]
        + [pl.BlockSpec(w.shape, lambda i: (0, 0), pipeline_mode=RESIDENT) for w in ws],
        out_specs=[row(d)] + [row(w.shape[1]) for w in ws],
        out_shape=[jax.ShapeDtypeStruct((s, d), MXU_DTYPE)] + [jax.ShapeDtypeStruct((s, w.shape[1]), t) for w, t in zip(ws, out_dtypes)],
        compiler_params=_params("parallel"),
    )(x, g.reshape(1, d), *ws)
    return outs[0], outs[1:]


def _mm_rms_bwd(pairs, x, g, dres, *, name):
    s, d = x.shape
    tm = _tile(s, (512, 256))
    n = len(pairs)

    def body(*refs):
        x_ref, g_ref, dr_ref = refs[2 * n:2 * n + 3]
        dx_ref, dg_ref = refs[2 * n + 3:]
        dh = _dg(refs[0][...], refs[n][...], ((1,), (1,)))
        for a_ref, w_ref in zip(refs[1:n], refs[n + 1:2 * n]):
            dh = dh + _dg(a_ref[...], w_ref[...], ((1,), (1,)))
        xv, gv = x_ref[...], g_ref[...]
        r = lax.rsqrt(jnp.mean(xv * xv, axis=-1, keepdims=True) + NORM_EPS)
        xh = xv * r
        dxh = dh * gv
        dx_ref[...] = dr_ref[...] + r * (dxh - xh * jnp.mean(dxh * xh, axis=-1, keepdims=True))

        @pl.when(pl.program_id(0) == 0)
        def _():
            dg_ref[...] = jnp.zeros_like(dg_ref)

        dg_ref[...] += jnp.sum(dh * xh, axis=0, keepdims=True)

    row = lambda w: pl.BlockSpec((tm, w), lambda i: (i, 0))
    vec = pl.BlockSpec((1, d), lambda i: (0, 0))
    return pl.pallas_call(
        body, name=name, grid=(s // tm,),
        in_specs=[row(a.shape[1]) for a, _ in pairs]
        + [pl.BlockSpec(w.shape, lambda i: (0, 0), pipeline_mode=RESIDENT) for _, w in pairs] + [row(d), vec, row(d)],
        out_specs=[row(d), vec], out_shape=[jax.ShapeDtypeStruct((s, d), F32), jax.ShapeDtypeStruct((1, d), F32)],
        compiler_params=_params("arbitrary"),
    )(*[a for a, _ in pairs], *[w for _, w in pairs], x, g.reshape(1, d), dres)


def _rms_bwd(x, g, dh, dres, *, name):
    s, d = x.shape
    tm = _tile(s, (512, 256))

    def body(x_ref, g_ref, dh_ref, dr_ref, dx_ref, dg_ref):
        xv = x_ref[...]
        r = lax.rsqrt(jnp.mean(xv * xv, axis=-1, keepdims=True) + NORM_EPS)
        xh = xv * r
        dhv = dh_ref[...].astype(F32)
        dxh = dhv * g_ref[...]
        dx = r * (dxh - xh * jnp.mean(dxh * xh, axis=-1, keepdims=True))
        dx_ref[...] = dr_ref[...] + dx

        @pl.when(pl.program_id(0) == 0)
        def _():
            dg_ref[...] = jnp.zeros_like(dg_ref)

        dg_ref[...] += jnp.sum(dhv * xh, axis=0, keepdims=True)

    row = pl.BlockSpec((tm, d), lambda i: (i, 0))
    vec = pl.BlockSpec((1, d), lambda i: (0, 0))
    return pl.pallas_call(
        body, name=name, grid=(s // tm,), in_specs=[row, vec, row, row], out_specs=[row, vec],
        out_shape=[jax.ShapeDtypeStruct((s, d), F32), jax.ShapeDtypeStruct((1, d), F32)],
        compiler_params=_params("arbitrary"),
    )(x, g.reshape(1, d), dh, dres)


FFN_CHUNK = 256
FFN_TM = 256
RESIDENT = pl.Buffered(1)


def _ffn_fwd_call(x, g, w_in, w_out, layer, *, name, gather=None):
    s, d = x.shape
    f = w_out.shape[1]
    tm = _tile(s, (2 * FFN_TM, FFN_TM))
    steps = s // tm
    n_g = len(gather[0]) if gather else 0
    g_shapes, g_sems, g_start, g_finish = _gather_plan(*gather) if gather else ([], [], None, None)

    def body(*refs):
        x_ref, g_ref, wi_ref, wo_ref = refs[:4]
        y_ref, u_ref = refs[4 + n_g:6 + n_g]
        act_ref = refs[6 + 2 * n_g]
        if gather:
            comm = (refs[4:4 + n_g], refs[6 + n_g:6 + 2 * n_g], *refs[7 + 2 * n_g:])
            pl.when(pl.program_id(0) == 0)(lambda: g_start(*comm))
        xv = x_ref[...]
        r = lax.rsqrt(jnp.mean(xv * xv, axis=-1, keepdims=True) + NORM_EPS)
        h = (xv * r * g_ref[...]).astype(MXU_DTYPE)
        for j in range(f // FFN_CHUNK):
            lo, hi = j * FFN_CHUNK, (j + 1) * FFN_CHUNK
            gate = _dg(h, wi_ref[:, lo:hi], ((1,), (0,))).astype(MXU_DTYPE)
            up = _dg(h, wi_ref[:, f + lo:f + hi], ((1,), (0,))).astype(MXU_DTYPE)
            u_ref[:, lo:hi] = gate
            u_ref[:, f + lo:f + hi] = up
            act_ref[:, lo:hi] = (_silu(gate.astype(F32)) * up.astype(F32)).astype(MXU_DTYPE)
        y_ref[...] = xv + 0.5 * _dg(act_ref[...], wo_ref[...], ((1,), (0,)))
        if gather:
            pl.when(pl.program_id(0) == steps - 1)(lambda: g_finish(*comm))

    row = lambda w: pl.BlockSpec((tm, w), lambda i: (i, 0))
    return pl.pallas_call(
        body, name=name, grid=(steps,),
        in_specs=[row(d), pl.BlockSpec((1, d), lambda i: (0, 0)),
                  pl.BlockSpec((None,) + w_in.shape[1:], lambda i: (layer, 0, 0), pipeline_mode=RESIDENT),
                  pl.BlockSpec((None,) + w_out.shape[1:], lambda i: (layer, 0, 0), pipeline_mode=RESIDENT)] + [HBM_SPEC] * n_g,
        out_specs=[row(d), row(2 * f)] + [HBM_SPEC] * n_g,
        out_shape=[jax.ShapeDtypeStruct((s, d), F32), jax.ShapeDtypeStruct((s, 2 * f), MXU_DTYPE)] + g_shapes,
        scratch_shapes=[pltpu.VMEM((tm, f), MXU_DTYPE)] + g_sems,
        compiler_params=_params("arbitrary" if gather else "parallel"),
    )(x, g.reshape(1, d), w_in, w_out, *(gather[0] if gather else ()))


def _ffn_bwd_call(x, g, u, dy, w_in, w_out, layer, *, name):
    s, d = x.shape
    f = w_out.shape[1]
    tm = _tile(s, (FFN_TM,))

    def body(x_ref, g_ref, u_ref, dy_ref, wi_ref, wo_ref, du_ref, dx_ref, dg_ref, h_ref):
        dyv = dy_ref[...]
        dyh = (0.5 * dyv).astype(MXU_DTYPE)
        for j in range(f // FFN_CHUNK):
            lo, hi = j * FFN_CHUNK, (j + 1) * FFN_CHUNK
            dact = _dg(dyh, wo_ref[lo:hi, :], ((1,), (1,)))
            gate, up = u_ref[:, lo:hi].astype(F32), u_ref[:, f + lo:f + hi].astype(F32)
            sg = _sigmoid(gate)
            du_ref[:, lo:hi] = (dact * up * sg * (1.0 + gate * (1.0 - sg))).astype(MXU_DTYPE)
            du_ref[:, f + lo:f + hi] = (dact * gate * sg).astype(MXU_DTYPE)
        dh = _dg(du_ref[...], wi_ref[...], ((1,), (1,)))
        xv, gv = x_ref[...], g_ref[...]
        r = lax.rsqrt(jnp.mean(xv * xv, axis=-1, keepdims=True) + NORM_EPS)
        xh = xv * r
        h_ref[...] = (xh * gv).astype(MXU_DTYPE)
        dxh = dh * gv
        dx_ref[...] = dyv + r * (dxh - xh * jnp.mean(dxh * xh, axis=-1, keepdims=True))

        @pl.when(pl.program_id(0) == 0)
        def _():
            dg_ref[...] = jnp.zeros_like(dg_ref)

        dg_ref[...] += jnp.sum(dh * xh, axis=0, keepdims=True)

    row = lambda w: pl.BlockSpec((tm, w), lambda i: (i, 0))
    vec = pl.BlockSpec((1, d), lambda i: (0, 0))
    return pl.pallas_call(
        body, name=name, grid=(s // tm,),
        in_specs=[row(d), vec, row(2 * f), row(d),
                  pl.BlockSpec((None,) + w_in.shape[1:], lambda i: (layer, 0, 0), pipeline_mode=RESIDENT),
                  pl.BlockSpec((None,) + w_out.shape[1:], lambda i: (layer, 0, 0), pipeline_mode=RESIDENT)],
        out_specs=[row(2 * f), row(d), vec, row(d)],
        out_shape=[jax.ShapeDtypeStruct((s, 2 * f), MXU_DTYPE), jax.ShapeDtypeStruct((s, d), F32),
                   jax.ShapeDtypeStruct((1, d), F32), jax.ShapeDtypeStruct((s, d), MXU_DTYPE)],
        compiler_params=_params("arbitrary"),
    )(x, g.reshape(1, d), u, dy, w_in, w_out)


def _ffn_dw_out(u, dy, *, name):
    s, f2 = u.shape
    f, d = f2 // 2, dy.shape[1]
    tf, tk = _tile(f, (1408, 256, 128)), _tile(s, (1024, 512, 256))
    nj = f // tf

    def body(g_ref, u_ref, dy_ref, o_ref):
        @pl.when(pl.program_id(1) == 0)
        def _():
            o_ref[...] = jnp.zeros_like(o_ref)

        act = _silu(g_ref[...].astype(F32)) * u_ref[...].astype(F32)
        o_ref[...] += _dg(act, 0.5 * dy_ref[...], ((0,), (0,)))

    return pl.pallas_call(
        body, name=name, grid=(nj, s // tk),
        in_specs=[pl.BlockSpec((tk, tf), lambda j, k: (k, j)), pl.BlockSpec((tk, tf), lambda j, k: (k, j + nj)),
                  pl.BlockSpec((tk, d), lambda j, k: (k, 0))],
        out_specs=pl.BlockSpec((tf, d), lambda j, k: (j, 0)), out_shape=jax.ShapeDtypeStruct((f, d), F32),
        compiler_params=_params("parallel", "arbitrary"),
    )(u, u, dy)


def _of_layer(w, layer):
    return (w[layer], 0) if isinstance(w, (list, tuple)) else (w, layer)


def _ffn_fwd(x, g, w_in, w_out, layer, tag, gather=None):
    (w_in, at), (w_out, _) = _of_layer(w_in, layer), _of_layer(w_out, layer)
    y, u, *gathered = _ffn_fwd_call(x, g[layer], w_in, w_out, at, name=f"{tag}_fwd", gather=gather)
    return y, (x, u), gathered


def _ffn_bwd(saved, g, w_in, w_out, layer, dy, tag):
    x, u = saved
    (w_in, at), (w_out, _) = _of_layer(w_in, layer), _of_layer(w_out, layer)
    du, dx, dg, h = _ffn_bwd_call(x, g[layer], u, dy, w_in, w_out, at, name=f"{tag}_bwd")
    dw_out = _ffn_dw_out(u, dy, name=f"{tag}_dwout")
    dw_in = _mm(h, du, mode="tn", name=f"{tag}_dwin")
    return dx, dg[0], dw_in, dw_out


ATTN_SCALE = ATTN_HEAD_DIM ** -0.5
NEG_BIG = -1e30
PROJ_AB_BLOCKS = 5


def _first_head(n):
    return _cols((n, LANES)) < ATTN_HEAD_DIM


def _per_head(tiles):
    first = _first_head(tiles[0].shape[0])
    return jnp.stack([jnp.where(first == (h == 0), t, 0.0) for t in tiles for h in (0, 1)], axis=0)


def _both(tiles):
    return jnp.stack([t for t in tiles for _ in (0, 1)], axis=0)


def _head_cols(tiles):
    return jnp.stack([t[:, c0:c0 + 1] for t in tiles for c0 in (0, ATTN_HEAD_DIM)], axis=0)


def _join_heads(v):
    return [v[2 * u] + v[2 * u + 1] for u in range(v.shape[0] // 2)]


def _spread_heads(v):
    first = _first_head(v.shape[1])
    return [jnp.where(first, v[2 * u], v[2 * u + 1]) for u in range(v.shape[0] // 2)]


def _band_masks(has_prev):
    qi, kj = _rows((ATTN_BLOCK, ATTN_BLOCK)), _cols((ATTN_BLOCK, ATTN_BLOCK))
    return (kj >= qi) & has_prev, kj <= qi


def _dattn_delta(o, dcat, *, name):
    s_len = o.shape[0]
    tm = _tile(s_len, (512, 256))

    def body(o_ref, do_ref, out_ref):
        r, c = _rows((ATTN_WIDTH, ATTN_WIDTH)), _cols((ATTN_WIDTH, ATTN_WIDTH))
        ones_bd = (r // ATTN_HEAD_DIM == c // ATTN_HEAD_DIM).astype(F32)
        out_ref[...] = _dg(o_ref[...] * do_ref[...], ones_bd, ((1,), (0,)), hi=True)

    blk = pl.BlockSpec((tm, ATTN_WIDTH), lambda i: (i, 0))
    return pl.pallas_call(
        body, name=name, grid=(s_len // tm,), in_specs=[blk, blk], out_specs=blk,
        out_shape=jax.ShapeDtypeStruct((s_len, ATTN_WIDTH), F32), compiler_params=_params("parallel"),
    )(o, dcat)


ATTN_UNITS = 4


def _units(it, d):
    if d == 1:
        return [(pl.ds(0, ATTN_BLOCK), pl.ds(p * LANES, LANES)) for p in range(ATTN_UNITS)]
    return [(pl.ds(it * ATTN_UNITS + u, ATTN_BLOCK, stride=d), pl.ds(0, LANES)) for u in range(ATTN_UNITS)]


def _tiles(ref, units):
    return [ref[rows, lanes] for rows, lanes in units]


def _store_tiles(ref, units, tiles):
    for (rows, lanes), t in zip(units, tiles):
        ref[rows, lanes] = t


def _stacked(a_tiles, b_tiles):
    return [jnp.concatenate([a, b], axis=0) for a, b in zip(a_tiles, b_tiles)]


def _passes(d):
    return max(d // ATTN_UNITS, 1)


def _pairs_per_step(d):
    return ATTN_PAIRS if d == 1 else 1


def _pair_specs(d, n_of):
    pairs = _pairs_per_step(d)
    groups = ATTN_PAIRS // pairs
    return lambda c: pl.BlockSpec((ATTN_BLOCK * d, LANES * pairs), lambda n, p: (n_of(n), c * groups + p))


def _sattn_fwd(proj, state, d, *, last, name):
    s_len = proj.shape[0]
    nb = s_len // (ATTN_BLOCK * d)
    first = state is None
    n_out = 2 if last else 3

    def body(*refs):
        q_ref, kp_ref, kc_ref, vp_ref, vc_ref = refs[:5]
        st_refs = () if first else refs[5:8]
        out_refs = refs[-n_out:]
        ok = jnp.concatenate(_band_masks(pl.program_id(0) > 0), axis=1)

        def one_pass(it, carry):
            units = _units(it, d)
            kcat = _stacked(_tiles(kp_ref, units), _tiles(kc_ref, units))
            vcat = _stacked(_tiles(vp_ref, units), _tiles(vc_ref, units))
            s = jnp.where(ok, _bdg(_per_head(_tiles(q_ref, units)), _both(kcat), 2, 2) * ATTN_SCALE, NEG_BIG)
            m_new = jnp.max(s, axis=2, keepdims=True)
            if not first:
                m_old = _head_cols(_tiles(st_refs[0], units))
                m_new = jnp.maximum(m_old, m_new)
                alpha = jnp.exp(m_old - m_new)
            p = jnp.exp(s - m_new)
            l_new = jnp.sum(p, axis=2, keepdims=True)
            acc = _join_heads(_bdg(p, _per_head(vcat), 2, 1))
            if not first:
                l_new = l_new + _head_cols(_tiles(st_refs[1], units)) * alpha
                acc = [a + a_in * sp for a, a_in, sp in zip(acc, _tiles(st_refs[2], units), _spread_heads(alpha))]
            m_pair, l_pair = _spread_heads(m_new), _spread_heads(l_new)
            if last:
                _store_tiles(out_refs[0], units, [a / l for a, l in zip(acc, l_pair)])
                _store_tiles(out_refs[1], units, [m + jnp.log(l) for m, l in zip(m_pair, l_pair)])
            else:
                _store_tiles(out_refs[0], units, m_pair)
                _store_tiles(out_refs[1], units, l_pair)
                _store_tiles(out_refs[2], units, acc)
            return carry

        lax.fori_loop(0, _passes(d), one_pass, 0)

    cur, prev = _pair_specs(d, lambda n: n), _pair_specs(d, lambda n: jnp.maximum(n - 1, 0))
    st = cur(0)
    return tuple(pl.pallas_call(
        body, name=name, grid=(nb, ATTN_PAIRS // _pairs_per_step(d)),
        in_specs=[cur(0), prev(1), cur(1), prev(2), cur(2)] + ([] if first else [st] * 3),
        out_specs=[st] * n_out, out_shape=[jax.ShapeDtypeStruct((s_len, ATTN_WIDTH), F32)] * n_out,
        compiler_params=_params("arbitrary", "parallel"),
    )(*([proj] * 5 + ([] if first else list(state)))))


def _dattn_forward(proj, tag):
    state = None
    for i, d in enumerate(DILATIONS):
        state = _sattn_fwd(proj, state, d, last=i == len(DILATIONS) - 1, name=f"{tag}_attn_d{d}")
    return state


def _sattn_bwd(proj, dcat, lse, delta, grads_in, d, *, name):
    s_len = proj.shape[0]
    nb = s_len // (ATTN_BLOCK * d)
    first = grads_in is None
    groups = ATTN_PAIRS // _pairs_per_step(d)

    def body(*refs):
        q_ref, kp_ref, kc_ref, vp_ref, vc_ref, do_ref, lse_ref, dl_ref = refs[:8]
        dq_in, dk_in, dv_in = (None, None, None) if first else refs[8:11]
        dq_ref, dk_ref, dv_ref, carry_k, carry_v = refs[-5:]
        n = pl.program_id(1)
        ok = jnp.concatenate(_band_masks(n > 0), axis=1)

        @pl.when(n == 0)
        def _():
            carry_k[...] = jnp.zeros_like(carry_k)
            carry_v[...] = jnp.zeros_like(carry_v)

        def leave(ref, carry, units, extra, into):
            out = [c + e for c, e in zip(_tiles(carry, units), extra)] if extra else _tiles(carry, units)
            if into is not None:
                out = [a + b for a, b in zip(out, _tiles(into, units))]
            _store_tiles(ref, units, out)

        def one_pass(it, carry):
            units = _units(it, d)
            kcat = _stacked(_tiles(kp_ref, units), _tiles(kc_ref, units))
            vcat = _stacked(_tiles(vp_ref, units), _tiles(vc_ref, units))
            q2, do2 = _per_head(_tiles(q_ref, units)), _per_head(_tiles(do_ref, units))
            s = _bdg(q2, _both(kcat), 2, 2) * ATTN_SCALE
            pr = jnp.where(ok, jnp.exp(jnp.where(ok, s, NEG_BIG) - _head_cols(_tiles(lse_ref, units))), 0.0)
            ds = pr * (_bdg(do2, _both(vcat), 2, 2) - _head_cols(_tiles(dl_ref, units)))
            dq = [t * ATTN_SCALE for t in _join_heads(_bdg(ds, _per_head(kcat), 2, 1))]
            if not first:
                dq = [a + b for a, b in zip(dq, _tiles(dq_in, units))]
            _store_tiles(dq_ref, units, dq)
            dk = [t * ATTN_SCALE for t in _join_heads(_bdg(ds, q2, 1, 1))]
            dv = _join_heads(_bdg(pr, do2, 1, 1))
            leave(dk_ref, carry_k, units, [t[:ATTN_BLOCK] for t in dk], dk_in)
            leave(dv_ref, carry_v, units, [t[:ATTN_BLOCK] for t in dv], dv_in)
            _store_tiles(carry_k, units, [t[ATTN_BLOCK:] for t in dk])
            _store_tiles(carry_v, units, [t[ATTN_BLOCK:] for t in dv])
            return carry

        def last_pass(it, carry):
            units = _units(it, d)
            leave(dk_ref, carry_k, units, None, dk_in)
            leave(dv_ref, carry_v, units, None, dv_in)
            return carry

        @pl.when(n < nb)
        def _():
            lax.fori_loop(0, _passes(d), one_pass, 0)

        @pl.when(n == nb)
        def _():
            lax.fori_loop(0, _passes(d), last_pass, 0)

    pairs = _pairs_per_step(d)
    blk = (ATTN_BLOCK * d, LANES * pairs)
    at = lambda n_of: (lambda c: pl.BlockSpec(blk, lambda p, n: (n_of(n), c * groups + p)))
    here = lambda n: jnp.minimum(n, nb - 1)
    cur, prev, lag = at(here), at(lambda n: jnp.maximum(here(n) - 1, 0)), at(lambda n: jnp.maximum(n - 1, 0))
    st, st_lag = cur(0), lag(0)
    return tuple(pl.pallas_call(
        body, name=name, grid=(groups, nb + 1),
        in_specs=[cur(0), prev(1), cur(1), prev(2), cur(2), st, st, st] + ([] if first else [st, st_lag, st_lag]),
        out_specs=[st, st_lag, st_lag], out_shape=[jax.ShapeDtypeStruct((s_len, ATTN_WIDTH), F32)] * 3,
        scratch_shapes=[pltpu.VMEM(blk, F32)] * 2, compiler_params=_params("parallel", "arbitrary"),
    )(*([proj] * 5 + [dcat, lse, delta] + ([] if first else list(grads_in)))))


def _dattn_backward(proj, o, lse, dcat, tag):
    delta = _dattn_delta(o, dcat, name=f"{tag}_attn_delta")
    grads = None
    for d in DILATIONS:
        grads = _sattn_bwd(proj, dcat, lse, delta, grads, d, name=f"{tag}_attn_bwd_d{d}")
    return grads


CONV_TC = 512
CONV_T = 512


def _conv_tiles(s_len, cb0, width):
    wide = 2 * CONV_TC
    tc = wide if width % wide == 0 and (cb0 * CONV_TC) % wide == 0 else CONV_TC
    return _tile(s_len, (CONV_T, CONV_T // 2)), tc, cb0 * CONV_TC // tc


def _shift_down(ext, k, t):
    return (pltpu.roll(ext, k, 0) if k else ext)[SUBLANES:SUBLANES + t]


def _conv_fwd(src, cb0, width, w8, *, name):
    s_len = src.shape[0]
    t, tc, cb = _conv_tiles(s_len, cb0, width)
    tpb = t // SUBLANES

    def body(x_ref, h_ref, w_ref, y_ref):
        halo = jnp.where(pl.program_id(0) > 0, h_ref[...], 0.0)
        ext = jnp.concatenate([halo, x_ref[...]], axis=0)
        w = w_ref[...]
        y = jnp.broadcast_to(w[CONV_K:CONV_K + 1], (t, tc))
        for k in range(CONV_K):
            y = y + w[k:k + 1] * _shift_down(ext, CONV_K - 1 - k, t)
        y_ref[...] = y

    return pl.pallas_call(
        body, name=name, grid=(s_len // t, width // tc),
        in_specs=[pl.BlockSpec((t, tc), lambda i, j: (i, cb + j)),
                  pl.BlockSpec((SUBLANES, tc), lambda i, j: (jnp.maximum(i * tpb - 1, 0), cb + j)),
                  pl.BlockSpec((SUBLANES, tc), lambda i, j: (0, j))],
        out_specs=pl.BlockSpec((t, tc), lambda i, j: (i, j)), out_shape=jax.ShapeDtypeStruct((s_len, width), F32),
        compiler_params=_params("parallel", "parallel"),
    )(src, src, w8)


def _conv_bwd(src, cb0, width, w8, dy, *, name):
    s_len = src.shape[0]
    t, tc, cb = _conv_tiles(s_len, cb0, width)
    tpb = t // SUBLANES
    ni = s_len // t

    def body(x_ref, h_ref, w_ref, dy_ref, dn_ref, dx_ref, dw_ref):
        i = pl.program_id(1)
        halo = jnp.where(i > 0, h_ref[...], 0.0)
        ext = jnp.concatenate([halo, x_ref[...]], axis=0)
        dyv = dy_ref[...]
        extn = jnp.concatenate([dyv, jnp.where(i < ni - 1, dn_ref[...], 0.0)], axis=0)
        w = w_ref[...]
        row = _rows((SUBLANES, tc))
        dx = jnp.zeros((t, tc), F32)
        dw = jnp.where(row == CONV_K, jnp.sum(dyv, axis=0, keepdims=True), 0.0)
        for k in range(CONV_K):
            up = CONV_K - 1 - k
            dx = dx + w[k:k + 1] * (pltpu.roll(extn, t + SUBLANES - up, 0) if up else extn)[:t]
            dw = dw + jnp.where(row == k, jnp.sum(dyv * _shift_down(ext, up, t), axis=0, keepdims=True), 0.0)
        dx_ref[...] = dx.astype(dx_ref.dtype)

        @pl.when(i == 0)
        def _():
            dw_ref[...] = jnp.zeros_like(dw_ref)

        dw_ref[...] += dw

    return pl.pallas_call(
        body, name=name, grid=(width // tc, ni),
        in_specs=[pl.BlockSpec((t, tc), lambda j, i: (i, cb + j)),
                  pl.BlockSpec((SUBLANES, tc), lambda j, i: (jnp.maximum(i * tpb - 1, 0), cb + j)),
                  pl.BlockSpec((SUBLANES, tc), lambda j, i: (0, j)),
                  pl.BlockSpec((t, tc), lambda j, i: (i, j)),
                  pl.BlockSpec((SUBLANES, tc), lambda j, i: (jnp.minimum((i + 1) * tpb, s_len // SUBLANES - 1), j))],
        out_specs=[pl.BlockSpec((t, tc), lambda j, i: (i, j)), pl.BlockSpec((SUBLANES, tc), lambda j, i: (0, j))],
        out_shape=[jax.ShapeDtypeStruct((s_len, width), MXU_DTYPE), jax.ShapeDtypeStruct((SUBLANES, width), F32)],
        compiler_params=_params("parallel", "arbitrary"),
    )(src, src, w8, dy, dy)


LRU_T = 256


def _lru_gates(xc, wa, wx, ba, bx, lam):
    r = _sigmoid(_bdot(xc, wa) + ba)
    i = _sigmoid(_bdot(xc, wx) + bx)
    log_a = (-LRU_C) * r * _softplus(-lam)
    return jnp.exp(log_a), jnp.sqrt(_neg_expm1(2.0 * log_a)) * i * xc


def _block_scan(a, b, state, reverse):
    t = a.shape[0]
    row = _rows(a.shape) % SUBLANES
    s = 1
    while s < SUBLANES:
        shift, ok = (t - s, row < SUBLANES - s) if reverse else (s, row >= s)
        b = jnp.where(ok, a * pltpu.roll(b, shift, 0) + b, b)
        a = jnp.where(ok, a * pltpu.roll(a, shift, 0), a)
        s *= 2
    groups = range(t // SUBLANES)
    out = [None] * len(groups)
    for g in (reversed(groups) if reverse else groups):
        rows = slice(g * SUBLANES, (g + 1) * SUBLANES)
        out[g] = b[rows] + a[rows] * state
        state = out[g][0:1] if reverse else out[g][SUBLANES - 1:SUBLANES]
    return jnp.concatenate(out, axis=0)


def _lru_fwd(xc, proj, wa, wx, ba, bx, lam, *, name):
    s_len, w = xc.shape
    t = _tile(s_len, (LRU_T,))

    def body(xc_ref, gr_ref, wa_ref, wx_ref, ba_ref, bx_ref, lam_ref, h_ref, y_ref, carry):
        @pl.when(pl.program_id(0) == 0)
        def _():
            carry[...] = jnp.zeros_like(carry)

        a, b = _lru_gates(xc_ref[...], wa_ref[...], wx_ref[...], ba_ref[...], bx_ref[...], lam_ref[...])
        h = _block_scan(a, b, carry[0:1, :], False)
        h_ref[...] = h
        y_ref[...] = (h * _gelu(gr_ref[...])).astype(y_ref.dtype)
        carry[0:1, :] = h[t - 1:t, :]

    row = pl.BlockSpec((t, w), lambda i: (i, 0))
    mat = pl.BlockSpec((w, w), lambda i: (0, 0))
    vec = pl.BlockSpec((1, w), lambda i: (0, 0))
    return pl.pallas_call(
        body, name=name, grid=(s_len // t,),
        in_specs=[row, pl.BlockSpec((t, w), lambda i: (i, PROJ_AB_BLOCKS - 1)), mat, mat, vec, vec, vec],
        out_specs=[row, row], out_shape=[jax.ShapeDtypeStruct((s_len, w), F32), jax.ShapeDtypeStruct((s_len, w), MXU_DTYPE)],
        scratch_shapes=[pltpu.VMEM((SUBLANES, w), F32)], compiler_params=_params("arbitrary"),
    )(xc, proj, wa, wx, ba, bx, lam)


def _lru_bwd(xc, proj, hs, dcat, wa, wx, ba, bx, lam, *, name):
    s_len, w = xc.shape
    t = _tile(s_len, (LRU_T,))
    nb = s_len // t
    tpb = t // SUBLANES

    def body(xc_ref, gr_ref, h_ref, hp_ref, dy_ref, wa_ref, wx_ref, ba_ref, bx_ref, lam_ref,
             dxc_ref, dgr_ref, dwa_ref, dwx_ref, dba_ref, dbx_ref, dlam_ref, carry):
        step = pl.program_id(0)
        params = (wa_ref[...], wx_ref[...], ba_ref[...], bx_ref[...], lam_ref[...])

        @pl.when(step == 0)
        def _():
            carry[...] = jnp.zeros_like(carry)
            for r in (dwa_ref, dwx_ref, dba_ref, dbx_ref, dlam_ref):
                r[...] = jnp.zeros_like(r)

        (a, _), vjp = jax.vjp(_lru_gates, xc_ref[...], *params)
        gr, h, dy = gr_ref[...], h_ref[...], dy_ref[...]
        gel, gel_vjp = jax.vjp(_gelu, gr)
        dgr_ref[...] = gel_vjp(dy * h)[0].astype(dgr_ref.dtype)
        dh = dy * gel
        big_g = _block_scan(a, a * dh, carry[0:1, :], True)
        row = _rows((t, w))
        g = dh + jnp.where(row == t - 1, carry[0:1, :], pltpu.roll(big_g, t - 1, 0))
        carry[0:1, :] = big_g[0:1, :]
        h_last = jnp.where(step < nb - 1, hp_ref[SUBLANES - 1:SUBLANES, :], 0.0)
        h_prev = jnp.where(row == 0, h_last, pltpu.roll(h, 1, 0))
        dxc, dwa, dwx, dba, dbx, dlam = vjp((g * h_prev, g))
        dxc_ref[...] = dxc
        dwa_ref[...] += dwa
        dwx_ref[...] += dwx
        dba_ref[...] += dba
        dbx_ref[...] += dbx
        dlam_ref[...] += dlam

    rev = lambda i: nb - 1 - i
    row = pl.BlockSpec((t, w), lambda i: (rev(i), 0))
    mat = pl.BlockSpec((w, w), lambda i: (0, 0))
    vec = pl.BlockSpec((1, w), lambda i: (0, 0))
    return pl.pallas_call(
        body, name=name, grid=(nb,),
        in_specs=[row, pl.BlockSpec((t, w), lambda i: (rev(i), PROJ_AB_BLOCKS - 1)), row,
                  pl.BlockSpec((SUBLANES, w), lambda i: (jnp.maximum(rev(i) * tpb - 1, 0), 0)),
                  pl.BlockSpec((t, w), lambda i: (rev(i), 1)), mat, mat, vec, vec, vec],
        out_specs=[row, row, mat, mat, vec, vec, vec],
        out_shape=[jax.ShapeDtypeStruct((s_len, w), F32), jax.ShapeDtypeStruct((s_len, w), MXU_DTYPE)]
        + [jax.ShapeDtypeStruct((w, w), F32)] * 2 + [jax.ShapeDtypeStruct((1, w), F32)] * 3,
        scratch_shapes=[pltpu.VMEM((SUBLANES, w), F32)], compiler_params=_params("arbitrary"),
    )(xc, proj, hs, hs, dcat, wa, wx, ba, bx, lam)


XA_T = 256
XA_SCALE = XA_HEAD_DIM ** -0.5


def _xa_heads(q, k, v):
    s = _bmm_nt(q, k) * XA_SCALE
    e = jnp.exp(s - jnp.max(s, axis=-1, keepdims=True))
    return _bmm(e / jnp.sum(e, axis=-1, keepdims=True), v)


def _xa_stack(ref):
    return jnp.stack([ref[:, h * XA_HEAD_DIM:(h + 1) * XA_HEAD_DIM].astype(F32) for h in range(XA_HEADS)], axis=0)


def _xa_fwd(q, kv, *, name):
    s_len, d = q.shape
    n_mem = kv.shape[0]
    t = _tile(s_len, (XA_T,))

    def body(q_ref, k_ref, v_ref, o_ref):
        o = _xa_heads(_xa_stack(q_ref), _xa_stack(k_ref), _xa_stack(v_ref))
        for h in range(XA_HEADS):
            o_ref[:, h * XA_HEAD_DIM:(h + 1) * XA_HEAD_DIM] = o[h].astype(o_ref.dtype)

    return pl.pallas_call(
        body, name=name, grid=(s_len // t,),
        in_specs=[pl.BlockSpec((t, d), lambda i: (i, 0)), pl.BlockSpec((n_mem, d), lambda i: (0, 0)),
                  pl.BlockSpec((n_mem, d), lambda i: (0, 1))],
        out_specs=pl.BlockSpec((t, d), lambda i: (i, 0)), out_shape=jax.ShapeDtypeStruct((s_len, d), MXU_DTYPE),
        compiler_params=_params("parallel"),
    )(q, kv, kv)


def _xa_bwd(q, kv, dy, wo, *, name):
    s_len, d = q.shape
    n_mem = kv.shape[0]
    t = _tile(s_len, (XA_T,))

    def body(q_ref, k_ref, v_ref, dy_ref, wo_ref, dq_ref, dk_ref, dv_ref):
        @pl.when(pl.program_id(0) == 0)
        def _():
            dk_ref[...] = jnp.zeros_like(dk_ref)
            dv_ref[...] = jnp.zeros_like(dv_ref)

        do = _dg(dy_ref[...], wo_ref[...], ((1,), (1,)))
        do = jnp.stack([do[:, h * XA_HEAD_DIM:(h + 1) * XA_HEAD_DIM] for h in range(XA_HEADS)], axis=0)
        _, vjp = jax.vjp(_xa_heads, _xa_stack(q_ref), _xa_stack(k_ref), _xa_stack(v_ref))
        dq, dk, dv = vjp(do)
        for h in range(XA_HEADS):
            sl = slice(h * XA_HEAD_DIM, (h + 1) * XA_HEAD_DIM)
            dq_ref[:, sl] = dq[h].astype(dq_ref.dtype)
            dk_ref[:, sl] += dk[h]
            dv_ref[:, sl] += dv[h]

    row = pl.BlockSpec((t, d), lambda i: (i, 0))
    dq, dk, dv = pl.pallas_call(
        body, name=name, grid=(s_len // t,),
        in_specs=[row, pl.BlockSpec((n_mem, d), lambda i: (0, 0)), pl.BlockSpec((n_mem, d), lambda i: (0, 1)), row,
                  pl.BlockSpec(wo.shape, lambda i: (0, 0), pipeline_mode=RESIDENT)],
        out_specs=[row, pl.BlockSpec((n_mem, d), lambda i: (0, 0)), pl.BlockSpec((n_mem, d), lambda i: (0, 0))],
        out_shape=[jax.ShapeDtypeStruct((s_len, d), MXU_DTYPE)] + [jax.ShapeDtypeStruct((n_mem, d), F32)] * 2,
        compiler_params=_params("arbitrary"),
    )(q, kv, kv, dy, wo)
    return dq, jnp.concatenate([dk, dv], axis=1)


DN_Q_SCALE = DN_HEAD_DIM ** -0.5
L2_EPS = 1e-6


def _bdg(a, b, ca, cb):
    return lax.dot_general(a.astype(MXU_DTYPE), b.astype(MXU_DTYPE), (((ca,), (cb,)), ((0,), (0,))), preferred_element_type=F32)


@jax.custom_vjp
def _bmm(a, b):
    return _bdg(a, b, 2, 1)


_bmm.defvjp(lambda a, b: (_bdg(a, b, 2, 1), (a, b)), lambda r, g: (_bdg(g, r[1], 2, 2), _bdg(r[0], g, 1, 1)))


@jax.custom_vjp
def _bmm_nt(a, b):
    return _bdg(a, b, 2, 2)


_bmm_nt.defvjp(lambda a, b: (_bdg(a, b, 2, 2), (a, b)), lambda r, g: (_bdg(g, r[1], 2, 1), _bdg(g, r[0], 1, 1)))


@jax.custom_vjp
def _bmm_tn(a, b):
    return _bdg(a, b, 1, 1)


_bmm_tn.defvjp(lambda a, b: (_bdg(a, b, 1, 1), (a, b)), lambda r, g: (_bdg(r[1], g, 2, 2), _bdg(r[0], g, 2, 1)))


def _tri_inverse(n):
    eye = (lax.broadcasted_iota(jnp.int32, n.shape, 1) == lax.broadcasted_iota(jnp.int32, n.shape, 2)).astype(F32)
    inv, p = eye - n, n
    for _ in range(5):
        p = _bdg(p, p, 2, 1)
        inv = _bdg(inv, eye + p, 2, 1)
    return inv


@jax.custom_vjp
def _tri_solve2(n, r1, r2):
    t = _tri_inverse(n)
    return _bdg(t, r1, 2, 1), _bdg(t, r2, 2, 1)


def _tri_solve2_fwd(n, r1, r2):
    t = _tri_inverse(n)
    x1, x2 = _bdg(t, r1, 2, 1), _bdg(t, r2, 2, 1)
    return (x1, x2), (t, x1, x2)


def _tri_solve2_bwd(saved, cts):
    t, x1, x2 = saved
    d1, d2 = _bdg(t, cts[0], 1, 1), _bdg(t, cts[1], 1, 1)
    return -(_bdg(d1, x1, 2, 2) + _bdg(d2, x2, 2, 2)), d1, d2


_tri_solve2.defvjp(_tri_solve2_fwd, _tri_solve2_bwd)


def _dn_gates(ab, alog, dtb):
    return -jnp.exp(alog) * _softplus(ab + dtb), _sigmoid(ab)


def _dn_heads(cq, ck, cv, z, g, beta, onorm, state):
    h, c, _ = cq.shape
    l2 = lambda t: t * lax.rsqrt(jnp.sum(t * t, axis=-1, keepdims=True) + L2_EPS)
    q, k, v = l2(_silu(cq)) * DN_Q_SCALE, l2(_silu(ck)), _silu(cv)
    r, cc = lax.broadcasted_iota(jnp.int32, (h, c, c), 1), lax.broadcasted_iota(jnp.int32, (h, c, c), 2)
    tri, eye = r >= cc, r == cc
    g_row = jnp.sum(jnp.where(eye, g, 0.0), axis=1, keepdims=True)
    gcum_c = jnp.sum(jnp.where(tri, g_row, 0.0), axis=2, keepdims=True)
    gcum_r = jnp.sum(jnp.where(cc >= r, g, 0.0), axis=1, keepdims=True)
    decay = jnp.where(tri, jnp.exp(jnp.where(tri, gcum_c - gcum_r, 0.0)), 0.0)
    kb = k * beta
    n = jnp.where(r > cc, _bmm_nt(kb, k) * decay, 0.0)
    u, w = _tri_solve2(n, v * beta, kb * jnp.exp(gcum_c))
    v_new = u - _bmm(w, state)
    o = _bmm(q * jnp.exp(gcum_c), state) + _bmm(_bmm_nt(q, k) * decay, v_new)
    g_last = jnp.sum(g, axis=1, keepdims=True)
    new_state = state * jnp.exp(g_last) + _bmm_tn(k * jnp.exp(g_last - gcum_c), v_new)
    on = o * lax.rsqrt(jnp.mean(o * o, axis=-1, keepdims=True) + NORM_EPS) * onorm
    return on * _silu(z), new_state


def _dn_stack(ref, col0):
    return jnp.stack([ref[:, col0 + h * DN_HEAD_DIM:col0 + (h + 1) * DN_HEAD_DIM].astype(F32) for h in range(DN_HEADS)], axis=0)


def _dn_cols(block, col0):
    return jnp.stack([block[:, col0 + h:col0 + h + 1] for h in range(DN_HEADS)], axis=0)


def _dn_fwd(cqkv, proj, ab, alog, dtb, onorm, *, name):
    s_len = cqkv.shape[0]
    c, hd, w = DN_CHUNK, DN_HEAD_DIM, DN_WIDTH
    n_chunks = s_len // c

    def body(c_ref, z_ref, ab_ref, alog_ref, dtb_ref, on_ref, o_ref, st_ref, state):
        @pl.when(pl.program_id(0) == 0)
        def _():
            state[...] = jnp.zeros_like(state)

        g_all, beta_all = _dn_gates(ab_ref[...], alog_ref[...], dtb_ref[...])
        st = state[...]
        st_ref[0] = st
        out, new = _dn_heads(_dn_stack(c_ref, 0), _dn_stack(c_ref, w), _dn_stack(c_ref, 2 * w), _dn_stack(z_ref, 0),
                             _dn_cols(g_all, 0), _dn_cols(beta_all, DN_HEADS), on_ref[...], st)
        state[...] = new
        for h in range(DN_HEADS):
            o_ref[:, h * hd:(h + 1) * hd] = out[h].astype(o_ref.dtype)

    vec = pl.BlockSpec((1, LANES), lambda i: (0, 0))
    return pl.pallas_call(
        body, name=name, grid=(n_chunks,),
        in_specs=[pl.BlockSpec((c, 3 * w), lambda i: (i, 0)), pl.BlockSpec((c, w), lambda i: (i, 3)),
                  pl.BlockSpec((c, LANES), lambda i: (i, 0)), vec, vec, vec],
        out_specs=[pl.BlockSpec((c, w), lambda i: (i, 0)), pl.BlockSpec((1, DN_HEADS, hd, hd), lambda i: (i, 0, 0, 0))],
        out_shape=[jax.ShapeDtypeStruct((s_len, w), MXU_DTYPE), jax.ShapeDtypeStruct((n_chunks, DN_HEADS, hd, hd), F32)],
        scratch_shapes=[pltpu.VMEM((DN_HEADS, hd, hd), F32)], compiler_params=_params("arbitrary"),
    )(cqkv, proj, ab, alog, dtb, onorm)


def _dn_bwd(cqkv, proj, ab, alog, dtb, onorm, states, dout, *, name):
    s_len = cqkv.shape[0]
    c, hd, w = DN_CHUNK, DN_HEAD_DIM, DN_WIDTH
    n_chunks = s_len // c

    def body(c_ref, z_ref, ab_ref, alog_ref, dtb_ref, on_ref, st_ref, do_ref,
             dc_ref, dz_ref, dab_ref, dalog_ref, ddtb_ref, don_ref, dstate):
        @pl.when(pl.program_id(0) == 0)
        def _():
            dstate[...] = jnp.zeros_like(dstate)
            for r in (dalog_ref, ddtb_ref, don_ref):
                r[...] = jnp.zeros_like(r)

        (g_all, beta_all), gates_vjp = jax.vjp(_dn_gates, ab_ref[...], alog_ref[...], dtb_ref[...])
        _, vjp = jax.vjp(_dn_heads, _dn_stack(c_ref, 0), _dn_stack(c_ref, w), _dn_stack(c_ref, 2 * w), _dn_stack(z_ref, 0),
                         _dn_cols(g_all, 0), _dn_cols(beta_all, DN_HEADS), on_ref[...], st_ref[0])
        dcq, dck, dcv, dz, dg, dbeta, don, dst = vjp((_dn_stack(do_ref, 0), dstate[...]))
        dstate[...] = dst
        col = _cols((c, LANES))
        dg_all, dbeta_all = jnp.zeros((c, LANES), F32), jnp.zeros((c, LANES), F32)
        for h in range(DN_HEADS):
            sl = slice(h * hd, (h + 1) * hd)
            dc_ref[:, sl] = dcq[h]
            dc_ref[:, w + h * hd:w + (h + 1) * hd] = dck[h]
            dc_ref[:, 2 * w + h * hd:2 * w + (h + 1) * hd] = dcv[h]
            dz_ref[:, sl] = dz[h].astype(dz_ref.dtype)
            dg_all = dg_all + jnp.where(col == h, dg[h], 0.0)
            dbeta_all = dbeta_all + jnp.where(col == DN_HEADS + h, dbeta[h], 0.0)
        dab, dalog, ddtb = gates_vjp((dg_all, dbeta_all))
        dab_ref[...] = dab
        dalog_ref[...] += dalog
        ddtb_ref[...] += ddtb
        don_ref[...] += don

    rev = lambda i: n_chunks - 1 - i
    vec = pl.BlockSpec((1, LANES), lambda i: (0, 0))
    return pl.pallas_call(
        body, name=name, grid=(n_chunks,),
        in_specs=[pl.BlockSpec((c, 3 * w), lambda i: (rev(i), 0)), pl.BlockSpec((c, w), lambda i: (rev(i), 3)),
                  pl.BlockSpec((c, LANES), lambda i: (rev(i), 0)), vec, vec, vec,
                  pl.BlockSpec((1, DN_HEADS, hd, hd), lambda i: (rev(i), 0, 0, 0)), pl.BlockSpec((c, w), lambda i: (rev(i), 0))],
        out_specs=[pl.BlockSpec((c, 3 * w), lambda i: (rev(i), 0)), pl.BlockSpec((c, w), lambda i: (rev(i), 0)),
                   pl.BlockSpec((c, LANES), lambda i: (rev(i), 0)), vec, vec, vec],
        out_shape=[jax.ShapeDtypeStruct((s_len, 3 * w), F32), jax.ShapeDtypeStruct((s_len, w), MXU_DTYPE),
                   jax.ShapeDtypeStruct((s_len, LANES), F32)] + [jax.ShapeDtypeStruct((1, LANES), F32)] * 3,
        scratch_shapes=[pltpu.VMEM((DN_HEADS, hd, hd), F32)], compiler_params=_params("arbitrary"),
    )(cqkv, proj, ab, alog, dtb, onorm, states, dout)


def _final_loss(x, g, target, *, name):
    s, d = x.shape
    tm = _tile(s, (512, 256))

    def body(x_ref, g_ref, t_ref, loss_ref, dx_ref, dg_ref):
        @pl.when(pl.program_id(0) == 0)
        def _():
            loss_ref[...] = jnp.zeros_like(loss_ref)
            dg_ref[...] = jnp.zeros_like(dg_ref)

        xv, gv = x_ref[...], g_ref[...]
        r = lax.rsqrt(jnp.mean(xv * xv, axis=-1, keepdims=True) + NORM_EPS)
        xh = xv * r
        err = xh * gv - t_ref[...]
        loss_ref[...] += 0.5 * jnp.sum(jnp.mean(err * err, axis=-1, keepdims=True), axis=0, keepdims=True)
        dy = err * (1.0 / d)
        dxh = dy * gv
        dx_ref[...] = r * (dxh - xh * jnp.mean(dxh * xh, axis=-1, keepdims=True))
        dg_ref[...] += jnp.sum(dy * xh, axis=0, keepdims=True)

    row = pl.BlockSpec((tm, d), lambda i: (i, 0))
    vec = pl.BlockSpec((1, d), lambda i: (0, 0))
    return pl.pallas_call(
        body, name=name, grid=(s // tm,), in_specs=[row, vec, row],
        out_specs=[pl.BlockSpec((1, LANES), lambda i: (0, 0)), row, vec],
        out_shape=[jax.ShapeDtypeStruct((1, LANES), F32), jax.ShapeDtypeStruct((s, d), F32), jax.ShapeDtypeStruct((1, d), F32)],
        compiler_params=_params("arbitrary"),
    )(x, g.reshape(1, d), target)


def _adamw(w, g, m, v, *, name):
    shape = w.shape
    cols = shape[-1]
    rows = max(w.size // cols, 1)
    tr = _tile(rows, (512, 352, 256, 128, 64, 32, 16, 8))
    c1, c2 = 1.0 - ADAM_B1 ** ADAM_STEP, 1.0 - ADAM_B2 ** ADAM_STEP

    def body(w_ref, g_ref, m_ref, v_ref, d_ref, nm_ref, nv_ref):
        gv = g_ref[...]
        nm = ADAM_B1 * m_ref[...] + (1.0 - ADAM_B1) * gv
        nv = ADAM_B2 * v_ref[...] + (1.0 - ADAM_B2) * (gv * gv)
        d_ref[...] = -ADAM_LR * ((nm / c1) / (jnp.sqrt(nv / c2) + ADAM_EPS) + ADAM_WD * w_ref[...])
        nm_ref[...] = nm
        nv_ref[...] = nv

    blk = pl.BlockSpec((tr, cols), lambda i: (i, 0))
    outs = pl.pallas_call(
        body, name=name, grid=(rows // tr,), in_specs=[blk] * 4, out_specs=[blk] * 3,
        out_shape=[jax.ShapeDtypeStruct((rows, cols), F32)] * 3, compiler_params=_params("parallel"),
    )(*(t.reshape(rows, cols) for t in (w, g, m, v)))
    return tuple(t.reshape(shape) for t in outs)


def _block_diag(w):
    n, j, k = w.shape
    eye = jnp.eye(n, dtype=w.dtype)
    return (eye[:, None, :, None] * w[:, :, None, :]).reshape(n * j, n * k)


def _block_diag_part(m, n):
    j, k = m.shape[0] // n, m.shape[1] // n
    m4 = m.reshape(n, j, n, k)
    return jnp.stack([m4[i, :, i, :] for i in range(n)], axis=0)


DN_AB = 2 * DN_HEADS
DEPTH = 2


def _row(v, width=None):
    v = v.reshape(1, -1)
    return v if width is None else jnp.pad(v, ((0, 0), (0, width - v.shape[1])))


def _conv_w8(conv_w, bias=None):
    w8 = jnp.zeros((SUBLANES, conv_w.shape[1]), F32).at[:CONV_K].set(conv_w)
    return w8 if bias is None else w8.at[CONV_K].set(bias)


def _mixer_ab_fwd(x, w, tag):
    h, (proj,) = _norm_mm(x, w["mix_norm"][0], [w["ab_w_in"][0]], [F32], name=f"{tag}_in")
    o, lse = _dattn_forward(proj, tag)
    w8 = _conv_w8(w["lru_conv_w"][0], w["lru_conv_b"][0])
    xc = _conv_fwd(proj, PROJ_AB_BLOCKS - 2, LRU_WIDTH, w8, name=f"{tag}_conv")
    wa, wx = _block_diag(w["lru_w_a"][0]), _block_diag(w["lru_w_x"][0])
    vecs = (_row(w["lru_b_a"][0]), _row(w["lru_b_x"][0]), _row(w["lru_lambda"][0]))
    hs, y = _lru_fwd(xc, proj, wa, wx, *vecs, name=f"{tag}_lru")
    w_out = w["ab_w_out"][0]
    x2 = _mm(o, w_out[:ATTN_WIDTH], res=x, name=f"{tag}_out_attn")
    x2 = _mm(y, w_out[ATTN_WIDTH:], res=x2, name=f"{tag}_out_lru")
    return x2, (x, h, proj, o, lse, w8, xc, wa, wx, vecs, hs, y)


def _mixer_ab_bwd(saved, w, dy, tag):
    x, h, proj, o, lse, w8, xc, wa, wx, vecs, hs, y = saved
    w_out = w["ab_w_out"][0]
    dcat = _mm(dy, w_out, mode="nt", name=f"{tag}_dcat")
    dw_out = jnp.concatenate([_mm(o, dy, mode="tn", name=f"{tag}_dwout_attn"), _mm(y, dy, mode="tn", name=f"{tag}_dwout_lru")], axis=0)
    dq, dk, dv = _dattn_backward(proj, o, lse, dcat, tag)
    dxc, dgr, dwa, dwx, dba, dbx, dlam = _lru_bwd(xc, proj, hs, dcat, wa, wx, *vecs, name=f"{tag}_dlru")
    dxr, dw8 = _conv_bwd(proj, PROJ_AB_BLOCKS - 2, LRU_WIDTH, w8, dxc, name=f"{tag}_dconv")
    dproj = jnp.concatenate([t.astype(MXU_DTYPE) for t in (dq, dk, dv, dxr, dgr)], axis=1)
    dw_in = _mm(h, dproj, mode="tn", name=f"{tag}_dwin")
    dx, dg = _mm_rms_bwd([(dproj, w["ab_w_in"][0])], x, w["mix_norm"][0], dy, name=f"{tag}_dh")
    grads = dict(mix_norm=dg[0], ab_w_in=dw_in, ab_w_out=dw_out, lru_conv_w=dw8[:CONV_K], lru_conv_b=dw8[CONV_K],
                 lru_w_a=_block_diag_part(dwa, LRU_BLOCKS), lru_b_a=dba[0], lru_w_x=_block_diag_part(dwx, LRU_BLOCKS),
                 lru_b_x=dbx[0], lru_lambda=dlam[0])
    return dx, grads


def _dn_split_w(w_in):
    return w_in[:, :4 * DN_WIDTH], jnp.pad(w_in[:, 4 * DN_WIDTH:], ((0, 0), (0, LANES - DN_AB)))


def _mixer_dn_fwd(x, w, tag):
    w_qkvz, w_ab = _dn_split_w(w["dn_w_in"][0])
    h, (proj, ab) = _norm_mm(x, w["mix_norm"][1], [w_qkvz, w_ab], [F32, F32], name=f"{tag}_in")
    w8 = _conv_w8(w["dn_conv_w"][0])
    cqkv = _conv_fwd(proj, 0, 3 * DN_WIDTH, w8, name=f"{tag}_conv")
    vecs = (_row(w["dn_a_log"][0], LANES), _row(w["dn_dt_bias"][0], LANES), _row(w["dn_o_norm"][0]))
    og, states = _dn_fwd(cqkv, proj, ab, *vecs, name=f"{tag}_dn")
    x2 = _mm(og, w["dn_w_out"][0], res=x, name=f"{tag}_out")
    return x2, (x, h, w_qkvz, w_ab, proj, ab, w8, cqkv, vecs, og, states)


def _mixer_dn_bwd(saved, w, dy, tag):
    x, h, w_qkvz, w_ab, proj, ab, w8, cqkv, vecs, og, states = saved
    dout = _mm(dy, w["dn_w_out"][0], mode="nt", name=f"{tag}_dout")
    dw_out = _mm(og, dy, mode="tn", name=f"{tag}_dwout")
    dcqkv, dz, dab, dalog, ddtb, don = _dn_bwd(cqkv, proj, ab, *vecs, states, dout, name=f"{tag}_ddn")
    dqkv, dw8 = _conv_bwd(proj, 0, 3 * DN_WIDTH, w8, dcqkv, name=f"{tag}_dconv")
    dproj = jnp.concatenate([dqkv.astype(MXU_DTYPE), dz.astype(MXU_DTYPE)], axis=1)
    dw_in = jnp.concatenate([_mm(h, dproj, mode="tn", name=f"{tag}_dwin"),
                             _mm(h, dab, mode="tn", name=f"{tag}_dwin_ab")[:, :DN_AB]], axis=1)
    dx, dg = _mm_rms_bwd([(dproj, w_qkvz), (dab, w_ab)], x, w["mix_norm"][1], dy, name=f"{tag}_dh")
    grads = dict(mix_norm=dg[0], dn_w_in=dw_in, dn_w_out=dw_out, dn_conv_w=dw8[:CONV_K], dn_a_log=dalog[0, :DN_HEADS],
                 dn_dt_bias=ddtb[0, :DN_HEADS], dn_o_norm=don[0])
    return dx, grads


def _xa_layer_fwd(x, mem, w, layer, tag):
    hq, (q,) = _norm_mm(x, w["xa_norm"][layer], [w["xa_wq"][layer]], [MXU_DTYPE], name=f"{tag}_q")
    hm = _rms_fwd(mem, w["xa_mem_norm"][layer], name=f"{tag}_mem_norm")
    kv = _mm(hm, w["xa_wkv"][layer], name=f"{tag}_kv")
    oa = _xa_fwd(q, kv, name=f"{tag}_core")
    x2 = _mm(oa, w["xa_wo"][layer], res=x, name=f"{tag}_out")
    return x2, (x, hq, q, hm, kv, oa)


def _xa_layer_bwd(saved, mem, w, layer, dy, tag):
    x, hq, q, hm, kv, oa = saved
    dwo = _mm(oa, dy, mode="tn", name=f"{tag}_dwo")
    dq, dkv = _xa_bwd(q, kv, dy, w["xa_wo"][layer], name=f"{tag}_dcore")
    dwq = _mm(hq, dq, mode="tn", name=f"{tag}_dwq")
    dx, dg = _mm_rms_bwd([(dq, w["xa_wq"][layer])], x, w["xa_norm"][layer], dy, name=f"{tag}_dhq")
    dwkv = _mm(hm, dkv, mode="tn", name=f"{tag}_dwkv")
    dhm = _mm(dkv, w["xa_wkv"][layer], mode="nt", name=f"{tag}_dhm")
    _, dgm = _rms_bwd(mem, w["xa_mem_norm"][layer], dhm, jnp.zeros_like(mem), name=f"{tag}_dmem_norm")
    return dx, dict(xa_norm=dg[0], xa_mem_norm=dgm[0], xa_wq=dwq, xa_wkv=dwkv, xa_wo=dwo)


def _local_step(x, mem, target, w, pending=None):
    saved = []
    for layer in range(DEPTH):
        t = f"l{layer}"
        hosted = ([b for *_, b in pending[0]], [a for _, _, a, _ in pending[0]]) if pending and layer == 0 else None
        x, s1, gathered = _ffn_fwd(x, w["ffn1_norm"], w["ffn1_w_in"], w["ffn1_w_out"], layer, f"{t}_ffn1", gather=hosted)
        if hosted:
            _gathered(pending[0], gathered, pending[1], w)
        x, s2 = (_mixer_ab_fwd if layer % 2 == 0 else _mixer_dn_fwd)(x, w, f"{t}_mix")
        x, s3 = _xa_layer_fwd(x, mem, w, layer, f"{t}_xa")
        x, s4, _ = _ffn_fwd(x, w["ffn2_norm"], w["ffn2_w_in"], w["ffn2_w_out"], layer, f"{t}_ffn2")
        saved.append((s1, s2, s3, s4))
    loss, dx, dgf = _final_loss(x, w["final_norm"], target, name="final_loss")
    per_layer = []
    for layer in reversed(range(DEPTH)):
        t = f"l{layer}"
        s1, s2, s3, s4 = saved[layer]
        g = {}
        dx, g["ffn2_norm"], g["ffn2_w_in"], g["ffn2_w_out"] = _ffn_bwd(
            s4, w["ffn2_norm"], w["ffn2_w_in"], w["ffn2_w_out"], layer, dx, f"{t}_ffn2")
        dx, gx = _xa_layer_bwd(s3, mem, w, layer, dx, f"{t}_xa")
        dx, gm = (_mixer_ab_bwd if layer % 2 == 0 else _mixer_dn_bwd)(s2, w, dx, f"{t}_mix")
        dx, g["ffn1_norm"], g["ffn1_w_in"], g["ffn1_w_out"] = _ffn_bwd(
            s1, w["ffn1_norm"], w["ffn1_w_in"], w["ffn1_w_out"], layer, dx, f"{t}_ffn1")
        per_layer.append({**g, **gx, **gm})
    per_layer.reverse()
    grads = {"final_norm": [dgf[0]]}
    for layer_grads in per_layer:
        for name, value in layer_grads.items():
            grads.setdefault(name, []).append(value)
    return loss, dx, grads


N_CHIPS = 4
WIRE_DTYPE = jnp.bfloat16
HBM_SPEC = pl.BlockSpec(memory_space=pltpu.HBM)
PACK_COLS = 1024


def _place():
    x, y, c = lax.axis_index("x"), lax.axis_index("y"), lax.axis_index("c")
    return x, y, c, [(1 - x, y), (x, 1 - y), (1 - x, 1 - y)]


def _remote(src, dst, sems, k, to):
    return pltpu.make_async_remote_copy(src_ref=src, dst_ref=dst, send_sem=sems[0].at[k], recv_sem=sems[1].at[k],
                                        device_id=to, device_id_type=MESH)


def _gather_weights(blocks, axes):
    n = len(blocks)
    out_shapes, sem_shapes, start, finish = _gather_plan(blocks, axes)

    def body(*refs):
        start(refs[:n], refs[n:2 * n], *refs[2 * n:])
        finish(refs[:n], refs[n:2 * n], *refs[2 * n:])

    return pl.pallas_call(
        body, name="gather_weights", in_specs=[HBM_SPEC] * n, out_specs=[HBM_SPEC] * n,
        out_shape=out_shapes, scratch_shapes=sem_shapes,
    )(*blocks)


def _gather_plan(blocks, axes):
    n = len(blocks)
    split = [b.shape[1] % 32 == 0 for b in blocks]

    def full_shape(i):
        l, r, c = blocks[i].shape
        return (l, N_CHIPS * r, c) if axes[i] == 1 else (l, r, N_CHIPS * c)

    def copies(ins, outs, send_sems, recv_sems):
        x, y, c, chips = _place()
        sems = (send_sems, recv_sems)
        sibling = (x, y, 1 - c)
        me = 2 * x + y

        def window(i, k, h):
            l, r, cc = blocks[i].shape
            r0, nr = (0, r) if h is None else (h * (r // 2), r // 2)
            if axes[i] == 1:
                return outs[i].at[:, pl.ds(k * r + r0, nr), :]
            return outs[i].at[:, pl.ds(r0, nr), pl.ds(k * cc, cc)]

        def mine(i, h):
            r = blocks[i].shape[1]
            return ins[i] if h is None else ins[i].at[:, pl.ds(h * (r // 2), r // 2), :]

        half = lambda i: c if split[i] else None
        first = [_remote(mine(i, half(i)), window(i, me, half(i)), sems, 3 * i + j, (*chip, c))
                 for i in range(n) for j, chip in enumerate(chips)]
        first += [_remote(ins[i], window(i, me, None), sems, 6 * n + i, sibling) for i in range(n)]
        arrive = lambda i, j, h, k, frm: _remote(window(i, 2 * chips[j][0] + chips[j][1], h), window(i, 2 * chips[j][0] + chips[j][1], h),
                                                 sems, k, frm)
        return first, arrive, chips, c, sibling

    def start(ins, outs, send_sems, recv_sems):
        for cp in copies(ins, outs, send_sems, recv_sems)[0]:
            cp.start()

    def finish(ins, outs, send_sems, recv_sems):
        first, arrive, chips, c, sibling = copies(ins, outs, send_sems, recv_sems)
        passed = []
        for i in range(n):
            for j, (cx, cy) in enumerate(chips):
                arrive(i, j, c if split[i] else None, 3 * i + j, (cx, cy, c)).wait_recv()
                if split[i]:
                    passed.append(arrive(i, j, c, 3 * (n + i) + j, sibling))
                    passed[-1].start()
        for i in range(n):
            if split[i]:
                for j in range(len(chips)):
                    arrive(i, j, 1 - c, 3 * (n + i) + j, sibling).wait_recv()
        for cp in first[3 * n:]:
            cp.wait_recv()
        for cp in first + passed:
            cp.wait_send()

    sem_shapes = [pltpu.SemaphoreType.DMA((7 * n,)), pltpu.SemaphoreType.DMA((7 * n,))]
    return [jax.ShapeDtypeStruct(full_shape(i), blocks[i].dtype) for i in range(n)], sem_shapes, start, finish


def _allreduce_small(v):
    rows, cols = v.shape
    n_dev = 2 * N_CHIPS

    def body(v_ref, out_ref, all_ref, send_sems, recv_sems, local_sem):
        x, y, c, chips = _place()
        sems = (send_sems, recv_sems)
        me, sibling = (x, y, c), (x, y, 1 - c)
        slot = lambda px, py, pc: all_ref.at[pl.ds((4 * px + 2 * py + pc) * rows, rows), :]
        mine = pltpu.make_async_copy(v_ref, slot(*me), local_sem)
        mine.start()
        first = [_remote(v_ref, slot(*me), sems, 0, sibling)]
        first += [_remote(v_ref, slot(*me), sems, 1 + j, (*chip, c)) for j, chip in enumerate(chips)]
        for cp in first:
            cp.start()
        passed = [_remote(slot(*chip, c), slot(*chip, c), sems, 4 + j, sibling) for j, chip in enumerate(chips)]
        for j, chip in enumerate(chips):
            _remote(slot(*chip, c), slot(*chip, c), sems, 1 + j, me).wait_recv()
            passed[j].start()
        _remote(slot(*sibling), slot(*sibling), sems, 0, me).wait_recv()
        for j, chip in enumerate(chips):
            _remote(slot(*chip, 1 - c), slot(*chip, 1 - c), sems, 4 + j, me).wait_recv()
        for cp in first + passed:
            cp.wait_send()
        mine.wait()
        acc = all_ref[pl.ds(0, rows), :]
        for k in range(1, n_dev):
            acc = acc + all_ref[pl.ds(k * rows, rows), :]
        out_ref[...] = acc

    vmem = pl.BlockSpec(memory_space=pltpu.VMEM)
    return pl.pallas_call(
        body, name="allreduce_small", in_specs=[vmem], out_specs=vmem, out_shape=jax.ShapeDtypeStruct((rows, cols), F32),
        scratch_shapes=[pltpu.VMEM((n_dev * rows, cols), F32), pltpu.SemaphoreType.DMA((7,)), pltpu.SemaphoreType.DMA((7,)),
                        pltpu.SemaphoreType.DMA],
    )(v)


def _swap_other_half(g4):
    n, _, rows, cols = g4.shape

    def body(v_ref, out_ref, send_sems, recv_sems):
        x, y, c, _ = _place()
        cp = _remote(v_ref.at[:, 1 - c], out_ref, (send_sems, recv_sems), 0, (x, y, 1 - c))
        cp.start()
        cp.wait()

    return pl.pallas_call(
        body, name="reduce_swap", in_specs=[HBM_SPEC], out_specs=HBM_SPEC, out_shape=jax.ShapeDtypeStruct((n, rows, cols), g4.dtype),
        scratch_shapes=[pltpu.SemaphoreType.DMA((1,)), pltpu.SemaphoreType.DMA((1,))],
    )(g4)


def _add_kept_half(g4, got):
    n, _, rows, cols = g4.shape
    tr = _tile(rows, (256, 128, 64, 32, 16))
    nb = rows // tr

    def body(c_ref, a_ref, b_ref, o_ref):
        o_ref[...] = (a_ref[...] + b_ref[...]).astype(o_ref.dtype)

    return pl.pallas_call(
        body, name="reduce_sum_cores",
        grid_spec=pltpu.PrefetchScalarGridSpec(
            num_scalar_prefetch=1, grid=(n, nb),
            in_specs=[pl.BlockSpec((None, None, tr, cols), lambda k, i, c_ref: (k, c_ref[0], i, 0)),
                      pl.BlockSpec((None, tr, cols), lambda k, i, c_ref: (k, i, 0))],
            out_specs=pl.BlockSpec((None, tr, cols), lambda k, i, c_ref: (k, i, 0))),
        out_shape=jax.ShapeDtypeStruct((n, rows, cols), WIRE_DTYPE), compiler_params=_params("parallel", "parallel"),
    )(lax.axis_index("c").astype(jnp.int32).reshape(1), g4, got)


def _exchange_chips(v):
    def body(v_ref, out_ref, send_sems, recv_sems):
        x, y, c, chips = _place()
        sems = (send_sems, recv_sems)
        sends = [_remote(v_ref.at[2 * cx + cy], out_ref.at[j], sems, j, (cx, cy, c)) for j, (cx, cy) in enumerate(chips)]
        for cp in sends:
            cp.start()
        for j, (cx, cy) in enumerate(chips):
            _remote(v_ref.at[0], out_ref.at[j], sems, j, (cx, cy, c)).wait_recv()
        for cp in sends:
            cp.wait_send()

    return pl.pallas_call(
        body, name="exchange_chips", in_specs=[HBM_SPEC], out_specs=HBM_SPEC,
        out_shape=jax.ShapeDtypeStruct((N_CHIPS - 1,) + v.shape[1:], v.dtype),
        scratch_shapes=[pltpu.SemaphoreType.DMA((3,)), pltpu.SemaphoreType.DMA((3,))],
    )(v)


def _swap_sibling(v):
    def body(v_ref, out_ref, send_sems, recv_sems):
        x, y, c, _ = _place()
        cp = _remote(v_ref, out_ref, (send_sems, recv_sems), 0, (x, y, 1 - c))
        cp.start()
        cp.wait()

    return pl.pallas_call(
        body, name="share_halves", in_specs=[HBM_SPEC], out_specs=HBM_SPEC, out_shape=jax.ShapeDtypeStruct(v.shape, v.dtype),
        scratch_shapes=[pltpu.SemaphoreType.DMA((1,)), pltpu.SemaphoreType.DMA((1,))],
    )(v)


def _sum_chips(own4, parts):
    _, rows, cols = own4.shape
    tr = _tile(rows, (256, 128, 64, 32, 16))

    def body(me_ref, own_ref, p0_ref, p1_ref, p2_ref, o_ref):
        acc = own_ref[...].astype(F32)
        for r in (p0_ref, p1_ref, p2_ref):
            acc = acc + r[...].astype(F32)
        o_ref[...] = acc

    part = lambda j: pl.BlockSpec((None, tr, cols), lambda i, me_ref: (j, i, 0))
    chip = (2 * lax.axis_index("x") + lax.axis_index("y")).astype(jnp.int32).reshape(1)
    return pl.pallas_call(
        body, name="reduce_sum_chips",
        grid_spec=pltpu.PrefetchScalarGridSpec(
            num_scalar_prefetch=1, grid=(rows // tr,),
            in_specs=[pl.BlockSpec((None, tr, cols), lambda i, me_ref: (me_ref[0], i, 0)), part(0), part(1), part(2)],
            out_specs=pl.BlockSpec((tr, cols), lambda i, me_ref: (i, 0))),
        out_shape=jax.ShapeDtypeStruct((rows, cols), F32), compiler_params=_params("parallel"),
    )(chip, own4, parts, parts, parts)


def _reduce_grads(g4):
    chip_sum = _add_kept_half(g4, _swap_other_half(g4))
    half = _sum_chips(chip_sum, _exchange_chips(chip_sum))
    other = _swap_sibling(half)
    return jnp.where(lax.axis_index("c") == 0, jnp.stack([half, other]), jnp.stack([other, half]))


BIG = (("ffn1_w_in", 2), ("ffn1_w_out", 1), ("xa_wq", 1), ("xa_wkv", 2), ("xa_wo", 1), ("ffn2_w_in", 2), ("ffn2_w_out", 1),
       ("ab_w_in", 2), ("ab_w_out", 1), ("dn_w_in", 2), ("dn_w_out", 1))
TINY_SHARDED = (("lru_conv_w", 2), ("dn_conv_w", 2))
REPLICATED = ("ffn1_norm", "mix_norm", "xa_norm", "xa_mem_norm", "ffn2_norm", "lru_conv_b", "lru_w_a", "lru_b_a", "lru_w_x",
              "lru_b_x", "lru_lambda", "dn_a_log", "dn_dt_bias", "dn_o_norm", "final_norm")
WEIGHTS = ("ffn1_norm", "ffn1_w_in", "ffn1_w_out", "mix_norm", "xa_norm", "xa_mem_norm", "xa_wq", "xa_wkv", "xa_wo", "ffn2_norm",
           "ffn2_w_in", "ffn2_w_out", "ab_w_in", "lru_conv_w", "lru_conv_b", "lru_w_a", "lru_b_a", "lru_w_x", "lru_b_x",
           "lru_lambda", "ab_w_out", "dn_w_in", "dn_conv_w", "dn_a_log", "dn_dt_bias", "dn_o_norm", "dn_w_out", "final_norm")


def _pad_rows(flat, row_multiple):
    n = flat.shape[-1]
    per = row_multiple * PACK_COLS
    total = -(-n // per) * per
    flat = jnp.pad(flat, [(0, 0)] * (flat.ndim - 1) + [(0, total - n)])
    return flat.reshape(flat.shape[:-1] + (total // PACK_COLS, PACK_COLS))


def _lane_padded(shape):
    return shape[:-1] + (-(-shape[-1] // LANES) * LANES,)


def _pad_lanes(t):
    return jnp.pad(t, [(0, 0)] * (t.ndim - 1) + [(0, _lane_padded(t.shape)[-1] - t.shape[-1])])


FIRST_USED = ("ffn1_w_in", "ffn1_w_out")


def _gather_blocks(shards):
    first, rest = [], []
    for n, a in BIG + TINY_SHARDED:
        block = _pad_lanes(shards[n]).astype(MXU_DTYPE) if (n, a) in BIG else shards[n]
        if n in FIRST_USED:
            first.append((n, 0, a, block[0:1]))
            rest += [(n, layer, a, block[layer:layer + 1]) for layer in range(1, block.shape[0])]
        else:
            rest.append((n, None, a, block))
    return first, rest


def _gathered(plan, arrays, shards, into):
    for (n, layer, axis, _), full in zip(plan, arrays):
        width, padded = shards[n].shape[-1], _lane_padded(shards[n].shape)[-1]
        if padded != width:
            assert axis == 2
            full = jnp.concatenate([full[..., k * padded:k * padded + width] for k in range(N_CHIPS)], axis=-1)
        if layer is None:
            into[n] = full
        else:
            into.setdefault(n, [None] * shards[n].shape[0])[layer] = full
    return into


def _to_blocks(pieces, axis):
    width = pieces[0].shape[axis - 1] // N_CHIPS
    block = lambda p, k: _pad_lanes(lax.slice_in_dim(p, k * width, (k + 1) * width, axis=axis - 1)).reshape(-1)
    return jnp.stack([jnp.concatenate([block(p, k) for p in pieces]) for k in range(N_CHIPS)], axis=0)


def _pack_grads(grads):
    flat = jnp.concatenate([_to_blocks(grads[n], axis) for n, axis in BIG], axis=1)
    g = _pad_rows(flat, 2 * 256)
    return g.reshape(N_CHIPS, 2, g.shape[1] // 2, PACK_COLS)


def _unpack_grads(reduced, shards):
    flat = reduced.reshape(-1)
    out, off = {}, 0
    for n, _ in BIG:
        shape = shards[n].shape
        size = math.prod(_lane_padded(shape))
        out[n] = flat[off:off + size].reshape(_lane_padded(shape))[..., :shape[-1]]
        off += size
    return out


def _pack_small(grads, loss):
    parts = [p.reshape(-1) for n in REPLICATED + tuple(n for n, _ in TINY_SHARDED) for p in grads[n]] + [loss[0, :1]]
    flat = jnp.concatenate(parts)
    total = -(-flat.shape[0] // (SUBLANES * LANES)) * SUBLANES * LANES
    return jnp.pad(flat, (0, total - flat.shape[0])).reshape(-1, LANES)


def _unpack_small(summed, shards, chip):
    flat = summed.reshape(-1)
    out, off = {}, 0
    for n in REPLICATED:
        out[n] = flat[off:off + shards[n].size].reshape(shards[n].shape)
        off += shards[n].size
    for n, axis in TINY_SHARDED:
        width = shards[n].shape[axis]
        shape = shards[n].shape[:axis] + (N_CHIPS * width,) + shards[n].shape[axis + 1:]
        full = flat[off:off + N_CHIPS * shards[n].size].reshape(shape)
        out[n] = lax.dynamic_slice_in_dim(full, chip * width, width, axis=axis)
        off += N_CHIPS * shards[n].size
    return out, flat[off]


def kernel(x, mem, ffn1_norm, ffn1_w_in, ffn1_w_out, mix_norm, xa_norm, xa_mem_norm, xa_wq, xa_wkv, xa_wo, ffn2_norm,
           ffn2_w_in, ffn2_w_out, ab_w_in, lru_conv_w, lru_conv_b, lru_w_a, lru_b_a, lru_w_x, lru_b_x, lru_lambda,
           ab_w_out, dn_w_in, dn_conv_w, dn_a_log, dn_dt_bias, dn_o_norm, dn_w_out, final_norm, loss_target,
           m_ffn1_norm, m_ffn1_w_in, m_ffn1_w_out, m_mix_norm, m_xa_norm, m_xa_mem_norm, m_xa_wq, m_xa_wkv, m_xa_wo,
           m_ffn2_norm, m_ffn2_w_in, m_ffn2_w_out, m_ab_w_in, m_lru_conv_w, m_lru_conv_b, m_lru_w_a, m_lru_b_a,
           m_lru_w_x, m_lru_b_x, m_lru_lambda, m_ab_w_out, m_dn_w_in, m_dn_conv_w, m_dn_a_log, m_dn_dt_bias,
           m_dn_o_norm, m_dn_w_out, m_final_norm, v_ffn1_norm, v_ffn1_w_in, v_ffn1_w_out, v_mix_norm, v_xa_norm,
           v_xa_mem_norm, v_xa_wq, v_xa_wkv, v_xa_wo, v_ffn2_norm, v_ffn2_w_in, v_ffn2_w_out, v_ab_w_in,
           v_lru_conv_w, v_lru_conv_b, v_lru_w_a, v_lru_b_a, v_lru_w_x, v_lru_b_x, v_lru_lambda, v_ab_w_out,
           v_dn_w_in, v_dn_conv_w, v_dn_a_log, v_dn_dt_bias, v_dn_o_norm, v_dn_w_out, v_final_norm):
    given = dict(locals())
    shards = {n: given[n] for n in WEIGHTS}
    chip = 2 * lax.axis_index("x") + lax.axis_index("y")

    full = {n: shards[n] for n in REPLICATED}
    first, rest = _gather_blocks(shards)
    _gathered(first, _gather_weights([b for *_, b in first], [a for _, _, a, _ in first]), shards, full)
    loss, grad_x, grads = _local_step(x[0], mem[0], loss_target[0], full, pending=(rest, shards))

    small, loss_sum = _unpack_small(_allreduce_small(_pack_small(grads, loss)), shards, chip)
    grad = {**small, **_unpack_grads(_reduce_grads(_pack_grads(grads)), shards)}

    delta, new_m, new_v = {}, {}, {}
    for n in WEIGHTS:
        delta[n], new_m[n], new_v[n] = _adamw(shards[n], grad[n], given["m_" + n], given["v_" + n], name=f"adamw_{n}")
    return (loss_sum, grad_x[None], *[grad[n] for n in WEIGHTS], *[delta[n] for n in WEIGHTS],
            *[new_m[n] for n in WEIGHTS], *[new_v[n] for n in WEIGHTS])
```

```python
import math

import jax
import jax.numpy as jnp
from jax import lax
from jax.experimental import pallas as pl
from jax.experimental.pallas import tpu as pltpu

F32 = jnp.float32
MXU_DTYPE = jnp.bfloat16
VMEM_LIMIT_BYTES = 48 * 1024 * 1024
MM_BLOCK_BYTES = 8 * 1024 * 1024
LANES = 128
SUBLANES = 8

NORM_EPS = 1e-6
CONV_K = 4
ATTN_PAIRS = 4
ATTN_HEAD_DIM = 64
ATTN_WIDTH = 512
ATTN_BLOCK = 128
DILATIONS = (1, 4, 16)
LRU_WIDTH = 512
LRU_BLOCKS = 8
LRU_C = 8.0
DN_HEADS = 8
DN_HEAD_DIM = 128
DN_WIDTH = 1024
DN_CHUNK = 64
XA_HEADS = 4
XA_HEAD_DIM = 256
D_FF = 2816
ADAM_LR, ADAM_B1, ADAM_B2, ADAM_EPS, ADAM_WD, ADAM_STEP = 0.001, 0.9, 0.999, 1e-08, 0.01, 10

MESH = pl.DeviceIdType.MESH


def _tile(n, prefs):
    for p in prefs:
        if n % p == 0:
            return p
    return n


def _params(*sem):
    return pltpu.CompilerParams(dimension_semantics=sem, vmem_limit_bytes=VMEM_LIMIT_BYTES)


def _dg(a, b, dims, hi=False):
    if hi:
        return lax.dot_general(a, b, (dims, ((), ())), precision=lax.Precision.HIGHEST, preferred_element_type=F32)
    return lax.dot_general(a.astype(MXU_DTYPE), b.astype(MXU_DTYPE), (dims, ((), ())), preferred_element_type=F32)


@jax.custom_vjp
def _bdot(a, b):
    return _dg(a, b, ((1,), (0,)))


def _bdot_fwd(a, b):
    return _bdot(a, b), (a, b)


def _bdot_bwd(r, g):
    a, b = r
    return _dg(g, b, ((1,), (1,))).astype(a.dtype), _dg(a, g, ((0,), (0,))).astype(b.dtype)


_bdot.defvjp(_bdot_fwd, _bdot_bwd)


def _log1p(t):
    return jnp.where(t < 0.01, t * (1.0 - t * (0.5 - t * (1.0 / 3.0))), jnp.log(1.0 + t))


def _neg_expm1(y):
    series = -y * (1.0 + 0.5 * y * (1.0 + (1.0 / 3.0) * y * (1.0 + 0.25 * y)))
    return jnp.where(y > -0.01, series, 1.0 - jnp.exp(y))


def _softplus(x):
    return jnp.maximum(x, 0.0) + _log1p(jnp.exp(-jnp.abs(x)))


def _sigmoid(x):
    return 0.5 * jnp.tanh(0.5 * x) + 0.5


def _silu(x):
    return x * _sigmoid(x)


def _gelu(x):
    return 0.5 * x * (1.0 + jnp.tanh(0.7978845608028654 * (x + 0.044715 * x * x * x)))


def _rows(shape):
    return lax.broadcasted_iota(jnp.int32, shape, 0)


def _cols(shape):
    return lax.broadcasted_iota(jnp.int32, shape, 1)


def _mm(a, b, *, mode="nn", out_dtype=F32, res=None, scale=1.0, name):
    if mode == "nn":
        (m, k), (k2, n) = a.shape, b.shape
    elif mode == "nt":
        (m, k), (n, k2) = a.shape, b.shape
    else:
        (k, m), (k2, n) = a.shape, b.shape
    assert k == k2, (a.shape, b.shape, mode)
    if mode == "tn":
        tm, tn, tk = _tile(m, (1024, 512, 256, 128)), _tile(n, (1024, 512, 256, 128)), _tile(k, (2048, 1024, 512, 256))
    else:
        tm, tn = _tile(m, (512, 256, 128)), _tile(n, (1024, 512, 256, 128))
        tk = k if k * tn * 2 <= MM_BLOCK_BYTES else _tile(k, (1024, 512, 256, 128))
    nk = k // tk
    dims = {"nn": ((1,), (0,)), "nt": ((1,), (1,)), "tn": ((0,), (0,))}[mode]

    def body(*refs):
        a_ref, b_ref = refs[:2]
        r_ref = refs[2] if res is not None else None
        o_ref = refs[3 if res is not None else 2]

        def finish(r):
            if scale != 1.0:
                r = r * scale
            if res is not None:
                r = r_ref[...] + r
            o_ref[...] = r.astype(out_dtype)

        if nk == 1:
            finish(_dg(a_ref[...], b_ref[...], dims))
            return
        acc = refs[-1]
        kk = pl.program_id(2)

        @pl.when(kk == 0)
        def _():
            acc[...] = jnp.zeros_like(acc)

        acc[...] += _dg(a_ref[...], b_ref[...], dims)

        @pl.when(kk == nk - 1)
        def _():
            finish(acc[...])

    a_spec = pl.BlockSpec((tk, tm), lambda i, j, kk: (kk, i)) if mode == "tn" else pl.BlockSpec((tm, tk), lambda i, j, kk: (i, kk))
    b_spec = pl.BlockSpec((tn, tk), lambda i, j, kk: (j, kk)) if mode == "nt" else pl.BlockSpec((tk, tn), lambda i, j, kk: (kk, j))
    o_spec = pl.BlockSpec((tm, tn), lambda i, j, kk: (i, j))
    in_specs = [a_spec, b_spec] + ([o_spec] if res is not None else [])
    args = (a, b) + ((res,) if res is not None else ())
    return pl.pallas_call(
        body, name=name, grid=(m // tm, n // tn, nk), in_specs=in_specs, out_specs=o_spec,
        out_shape=jax.ShapeDtypeStruct((m, n), out_dtype), scratch_shapes=[pltpu.VMEM((tm, tn), F32)] if nk > 1 else [],
        compiler_params=_params("parallel", "parallel", "arbitrary"),
    )(*args)


def _rms_fwd(x, g, *, name):
    s, d = x.shape
    tm = _tile(s, (512, 256))

    def body(x_ref, g_ref, o_ref):
        xv = x_ref[...]
        r = lax.rsqrt(jnp.mean(xv * xv, axis=-1, keepdims=True) + NORM_EPS)
        o_ref[...] = (xv * r * g_ref[...]).astype(o_ref.dtype)

    return pl.pallas_call(
        body, name=name, grid=(s // tm,),
        in_specs=[pl.BlockSpec((tm, d), lambda i: (i, 0)), pl.BlockSpec((1, d), lambda i: (0, 0))],
        out_specs=pl.BlockSpec((tm, d), lambda i: (i, 0)), out_shape=jax.ShapeDtypeStruct((s, d), MXU_DTYPE),
        compiler_params=_params("parallel"),
    )(x, g.reshape(1, d))


def _norm_mm(x, g, ws, out_dtypes, *, name):
    s, d = x.shape
    tm = _tile(s, (512, 256))
    nw = len(ws)

    def body(*refs):
        x_ref, g_ref = refs[:2]
        h_ref = refs[2 + nw]
        xv = x_ref[...]
        r = lax.rsqrt(jnp.mean(xv * xv, axis=-1, keepdims=True) + NORM_EPS)
        h = (xv * r * g_ref[...]).astype(MXU_DTYPE)
        h_ref[...] = h
        for w_ref, o_ref in zip(refs[2:2 + nw], refs[3 + nw:]):
            o_ref[...] = _dg(h, w_ref[...], ((1,), (0,))).astype(o_ref.dtype)

    row = lambda w: pl.BlockSpec((tm, w), lambda i: (i, 0))
    outs = pl.pallas_call(
        body, name=name, grid=(s // tm,),
        in_specs=[row(d), pl.BlockSpec((1, d), lambda i: (0, 0))]
        + [pl.BlockSpec(w.shape, lambda i: (0, 0), pipeline_mode=RESIDENT) for w in ws],
        out_specs=[row(d)] + [row(w.shape[1]) for w in ws],
        out_shape=[jax.ShapeDtypeStruct((s, d), MXU_DTYPE)] + [jax.ShapeDtypeStruct((s, w.shape[1]), t) for w, t in zip(ws, out_dtypes)],
        compiler_params=_params("parallel"),
    )(x, g.reshape(1, d), *ws)
    return outs[0], outs[1:]


def _mm_rms_bwd(pairs, x, g, dres, *, name):
    s, d = x.shape
    tm = _tile(s, (512, 256))
    n = len(pairs)

    def body(*refs):
        x_ref, g_ref, dr_ref = refs[2 * n:2 * n + 3]
        dx_ref, dg_ref = refs[2 * n + 3:]
        dh = _dg(refs[0][...], refs[n][...], ((1,), (1,)))
        for a_ref, w_ref in zip(refs[1:n], refs[n + 1:2 * n]):
            dh = dh + _dg(a_ref[...], w_ref[...], ((1,), (1,)))
        xv, gv = x_ref[...], g_ref[...]
        r = lax.rsqrt(jnp.mean(xv * xv, axis=-1, keepdims=True) + NORM_EPS)
        xh = xv * r
        dxh = dh * gv
        dx_ref[...] = dr_ref[...] + r * (dxh - xh * jnp.mean(dxh * xh, axis=-1, keepdims=True))

        @pl.when(pl.program_id(0) == 0)
        def _():
            dg_ref[...] = jnp.zeros_like(dg_ref)

        dg_ref[...] += jnp.sum(dh * xh, axis=0, keepdims=True)

    row = lambda w: pl.BlockSpec((tm, w), lambda i: (i, 0))
    vec = pl.BlockSpec((1, d), lambda i: (0, 0))
    return pl.pallas_call(
        body, name=name, grid=(s // tm,),
        in_specs=[row(a.shape[1]) for a, _ in pairs]
        + [pl.BlockSpec(w.shape, lambda i: (0, 0), pipeline_mode=RESIDENT) for _, w in pairs] + [row(d), vec, row(d)],
        out_specs=[row(d), vec], out_shape=[jax.ShapeDtypeStruct((s, d), F32), jax.ShapeDtypeStruct((1, d), F32)],
        compiler_params=_params("arbitrary"),
    )(*[a for a, _ in pairs], *[w for _, w in pairs], x, g.reshape(1, d), dres)


def _rms_bwd(x, g, dh, dres, *, name):
    s, d = x.shape
    tm = _tile(s, (512, 256))

    def body(x_ref, g_ref, dh_ref, dr_ref, dx_ref, dg_ref):
        xv = x_ref[...]
        r = lax.rsqrt(jnp.mean(xv * xv, axis=-1, keepdims=True) + NORM_EPS)
        xh = xv * r
        dhv = dh_ref[...].astype(F32)
        dxh = dhv * g_ref[...]
        dx = r * (dxh - xh * jnp.mean(dxh * xh, axis=-1, keepdims=True))
        dx_ref[...] = dr_ref[...] + dx

        @pl.when(pl.program_id(0) == 0)
        def _():
            dg_ref[...] = jnp.zeros_like(dg_ref)

        dg_ref[...] += jnp.sum(dhv * xh, axis=0, keepdims=True)

    row = pl.BlockSpec((tm, d), lambda i: (i, 0))
    vec = pl.BlockSpec((1, d), lambda i: (0, 0))
    return pl.pallas_call(
        body, name=name, grid=(s // tm,), in_specs=[row, vec, row, row], out_specs=[row, vec],
        out_shape=[jax.ShapeDtypeStruct((s, d), F32), jax.ShapeDtypeStruct((1, d), F32)],
        compiler_params=_params("arbitrary"),
    )(x, g.reshape(1, d), dh, dres)


FFN_CHUNK = 256
FFN_TM = 256
RESIDENT = pl.Buffered(1)


def _ffn_fwd_call(x, g, w_in, w_out, layer, *, name, gather=None):
    s, d = x.shape
    f = w_out.shape[1]
    tm = _tile(s, (2 * FFN_TM, FFN_TM))
    steps = s // tm
    n_g = len(gather[0]) if gather else 0
    g_shapes, g_sems, g_start, g_finish = _gather_plan(*gather) if gather else ([], [], None, None)

    def body(*refs):
        x_ref, g_ref, wi_ref, wo_ref = refs[:4]
        y_ref, u_ref = refs[4 + n_g:6 + n_g]
        act_ref = refs[6 + 2 * n_g]
        if gather:
            comm = (refs[4:4 + n_g], refs[6 + n_g:6 + 2 * n_g], *refs[7 + 2 * n_g:])
            pl.when(pl.program_id(0) == 0)(lambda: g_start(*comm))
        xv = x_ref[...]
        r = lax.rsqrt(jnp.mean(xv * xv, axis=-1, keepdims=True) + NORM_EPS)
        h = (xv * r * g_ref[...]).astype(MXU_DTYPE)
        for j in range(f // FFN_CHUNK):
            lo, hi = j * FFN_CHUNK, (j + 1) * FFN_CHUNK
            gate = _dg(h, wi_ref[:, lo:hi], ((1,), (0,))).astype(MXU_DTYPE)
            up = _dg(h, wi_ref[:, f + lo:f + hi], ((1,), (0,))).astype(MXU_DTYPE)
            u_ref[:, lo:hi] = gate
            u_ref[:, f + lo:f + hi] = up
            act_ref[:, lo:hi] = (_silu(gate.astype(F32)) * up.astype(F32)).astype(MXU_DTYPE)
        y_ref[...] = xv + 0.5 * _dg(act_ref[...], wo_ref[...], ((1,), (0,)))
        if gather:
            pl.when(pl.program_id(0) == steps - 1)(lambda: g_finish(*comm))

    row = lambda w: pl.BlockSpec((tm, w), lambda i: (i, 0))
    return pl.pallas_call(
        body, name=name, grid=(steps,),
        in_specs=[row(d), pl.BlockSpec((1, d), lambda i: (0, 0)),
                  pl.BlockSpec((None,) + w_in.shape[1:], lambda i: (layer, 0, 0), pipeline_mode=RESIDENT),
                  pl.BlockSpec((None,) + w_out.shape[1:], lambda i: (layer, 0, 0), pipeline_mode=RESIDENT)] + [HBM_SPEC] * n_g,
        out_specs=[row(d), row(2 * f)] + [HBM_SPEC] * n_g,
        out_shape=[jax.ShapeDtypeStruct((s, d), F32), jax.ShapeDtypeStruct((s, 2 * f), MXU_DTYPE)] + g_shapes,
        scratch_shapes=[pltpu.VMEM((tm, f), MXU_DTYPE)] + g_sems,
        compiler_params=_params("arbitrary" if gather else "parallel"),
    )(x, g.reshape(1, d), w_in, w_out, *(gather[0] if gather else ()))


def _ffn_bwd_call(x, g, u, dy, w_in, w_out, layer, *, name, exchange=None):
    s, d = x.shape
    f = w_out.shape[1]
    tm = _tile(s, (FFN_TM,))
    steps = s // tm
    hosted = exchange is not None
    e_shape, e_sems, e_start, e_finish = _exchange_plan(exchange) if hosted else (None, [], None, None)

    def body(*refs):
        x_ref, g_ref, u_ref, dy_ref, wi_ref, wo_ref = refs[:6]
        du_ref, dx_ref, dg_ref, h_ref = refs[6 + hosted:10 + hosted]
        if hosted:
            comm = (refs[6], *refs[10 + hosted:])
            pl.when(pl.program_id(0) == 0)(lambda: e_start(*comm))
        dyv = dy_ref[...]
        dyh = (0.5 * dyv).astype(MXU_DTYPE)
        for j in range(f // FFN_CHUNK):
            lo, hi = j * FFN_CHUNK, (j + 1) * FFN_CHUNK
            dact = _dg(dyh, wo_ref[lo:hi, :], ((1,), (1,)))
            gate, up = u_ref[:, lo:hi].astype(F32), u_ref[:, f + lo:f + hi].astype(F32)
            sg = _sigmoid(gate)
            du_ref[:, lo:hi] = (dact * up * sg * (1.0 + gate * (1.0 - sg))).astype(MXU_DTYPE)
            du_ref[:, f + lo:f + hi] = (dact * gate * sg).astype(MXU_DTYPE)
        dh = _dg(du_ref[...], wi_ref[...], ((1,), (1,)))
        xv, gv = x_ref[...], g_ref[...]
        r = lax.rsqrt(jnp.mean(xv * xv, axis=-1, keepdims=True) + NORM_EPS)
        xh = xv * r
        h_ref[...] = (xh * gv).astype(MXU_DTYPE)
        dxh = dh * gv
        dx_ref[...] = dyv + r * (dxh - xh * jnp.mean(dxh * xh, axis=-1, keepdims=True))

        @pl.when(pl.program_id(0) == 0)
        def _():
            dg_ref[...] = jnp.zeros_like(dg_ref)

        dg_ref[...] += jnp.sum(dh * xh, axis=0, keepdims=True)
        if hosted:
            pl.when(pl.program_id(0) == steps - 1)(lambda: e_finish(*comm))

    row = lambda w: pl.BlockSpec((tm, w), lambda i: (i, 0))
    vec = pl.BlockSpec((1, d), lambda i: (0, 0))
    return pl.pallas_call(
        body, name=name, grid=(steps,),
        in_specs=[row(d), vec, row(2 * f), row(d),
                  pl.BlockSpec((None,) + w_in.shape[1:], lambda i: (layer, 0, 0), pipeline_mode=RESIDENT),
                  pl.BlockSpec((None,) + w_out.shape[1:], lambda i: (layer, 0, 0), pipeline_mode=RESIDENT)] + [HBM_SPEC] * hosted,
        out_specs=[row(2 * f), row(d), vec, row(d)] + [HBM_SPEC] * hosted,
        out_shape=[jax.ShapeDtypeStruct((s, 2 * f), MXU_DTYPE), jax.ShapeDtypeStruct((s, d), F32),
                   jax.ShapeDtypeStruct((1, d), F32), jax.ShapeDtypeStruct((s, d), MXU_DTYPE)] + [e_shape] * hosted,
        scratch_shapes=e_sems, compiler_params=_params("arbitrary"),
    )(x, g.reshape(1, d), u, dy, w_in, w_out, *([exchange] if hosted else []))


def _ffn_dw_out(u, dy, *, name):
    s, f2 = u.shape
    f, d = f2 // 2, dy.shape[1]
    tf, tk = _tile(f, (1408, 256, 128)), _tile(s, (1024, 512, 256))
    nj = f // tf

    def body(g_ref, u_ref, dy_ref, o_ref):
        @pl.when(pl.program_id(1) == 0)
        def _():
            o_ref[...] = jnp.zeros_like(o_ref)

        act = _silu(g_ref[...].astype(F32)) * u_ref[...].astype(F32)
        o_ref[...] += _dg(act, 0.5 * dy_ref[...], ((0,), (0,)))

    return pl.pallas_call(
        body, name=name, grid=(nj, s // tk),
        in_specs=[pl.BlockSpec((tk, tf), lambda j, k: (k, j)), pl.BlockSpec((tk, tf), lambda j, k: (k, j + nj)),
                  pl.BlockSpec((tk, d), lambda j, k: (k, 0))],
        out_specs=pl.BlockSpec((tf, d), lambda j, k: (j, 0)), out_shape=jax.ShapeDtypeStruct((f, d), F32),
        compiler_params=_params("parallel", "arbitrary"),
    )(u, u, dy)


def _of_layer(w, layer):
    return (w[layer], 0) if isinstance(w, (list, tuple)) else (w, layer)


def _ffn_fwd(x, g, w_in, w_out, layer, tag, gather=None):
    (w_in, at), (w_out, _) = _of_layer(w_in, layer), _of_layer(w_out, layer)
    y, u, *gathered = _ffn_fwd_call(x, g[layer], w_in, w_out, at, name=f"{tag}_fwd", gather=gather)
    return y, (x, u), gathered


def _ffn_bwd(saved, g, w_in, w_out, layer, dy, tag, exchange=None):
    x, u = saved
    (w_in, at), (w_out, _) = _of_layer(w_in, layer), _of_layer(w_out, layer)
    du, dx, dg, h, *parts = _ffn_bwd_call(x, g[layer], u, dy, w_in, w_out, at, name=f"{tag}_bwd", exchange=exchange)
    dw_out = _ffn_dw_out(u, dy, name=f"{tag}_dwout")
    dw_in = _mm(h, du, mode="tn", name=f"{tag}_dwin")
    return dx, dg[0], dw_in, dw_out, (parts[0] if parts else None)


ATTN_SCALE = ATTN_HEAD_DIM ** -0.5
NEG_BIG = -1e30
PROJ_AB_BLOCKS = 5


def _first_head(n):
    return _cols((n, LANES)) < ATTN_HEAD_DIM


def _per_head(tiles):
    first = _first_head(tiles[0].shape[0])
    return jnp.stack([jnp.where(first == (h == 0), t, 0.0) for t in tiles for h in (0, 1)], axis=0)


def _both(tiles):
    return jnp.stack([t for t in tiles for _ in (0, 1)], axis=0)


def _head_cols(tiles):
    return jnp.stack([t[:, c0:c0 + 1] for t in tiles for c0 in (0, ATTN_HEAD_DIM)], axis=0)


def _join_heads(v):
    return [v[2 * u] + v[2 * u + 1] for u in range(v.shape[0] // 2)]


def _spread_heads(v):
    first = _first_head(v.shape[1])
    return [jnp.where(first, v[2 * u], v[2 * u + 1]) for u in range(v.shape[0] // 2)]


def _band_masks(has_prev):
    qi, kj = _rows((ATTN_BLOCK, ATTN_BLOCK)), _cols((ATTN_BLOCK, ATTN_BLOCK))
    return (kj >= qi) & has_prev, kj <= qi


def _dattn_delta(o, dcat, *, name):
    s_len = o.shape[0]
    tm = _tile(s_len, (512, 256))

    def body(o_ref, do_ref, out_ref):
        r, c = _rows((ATTN_WIDTH, ATTN_WIDTH)), _cols((ATTN_WIDTH, ATTN_WIDTH))
        ones_bd = (r // ATTN_HEAD_DIM == c // ATTN_HEAD_DIM).astype(F32)
        out_ref[...] = _dg(o_ref[...] * do_ref[...], ones_bd, ((1,), (0,)), hi=True)

    blk = pl.BlockSpec((tm, ATTN_WIDTH), lambda i: (i, 0))
    return pl.pallas_call(
        body, name=name, grid=(s_len // tm,), in_specs=[blk, blk], out_specs=blk,
        out_shape=jax.ShapeDtypeStruct((s_len, ATTN_WIDTH), F32), compiler_params=_params("parallel"),
    )(o, dcat)


ATTN_UNITS = 4


def _units(it, d):
    if d == 1:
        return [(pl.ds(0, ATTN_BLOCK), pl.ds(p * LANES, LANES)) for p in range(ATTN_UNITS)]
    return [(pl.ds(it * ATTN_UNITS + u, ATTN_BLOCK, stride=d), pl.ds(0, LANES)) for u in range(ATTN_UNITS)]


def _tiles(ref, units):
    return [ref[rows, lanes] for rows, lanes in units]


def _store_tiles(ref, units, tiles):
    for (rows, lanes), t in zip(units, tiles):
        ref[rows, lanes] = t


def _stacked(a_tiles, b_tiles):
    return [jnp.concatenate([a, b], axis=0) for a, b in zip(a_tiles, b_tiles)]


def _passes(d):
    return max(d // ATTN_UNITS, 1)


def _pairs_per_step(d):
    return ATTN_PAIRS if d == 1 else 1


def _pair_specs(d, n_of):
    pairs = _pairs_per_step(d)
    groups = ATTN_PAIRS // pairs
    return lambda c: pl.BlockSpec((ATTN_BLOCK * d, LANES * pairs), lambda n, p: (n_of(n), c * groups + p))


def _sattn_fwd(proj, state, d, *, last, name):
    s_len = proj.shape[0]
    nb = s_len // (ATTN_BLOCK * d)
    first = state is None
    n_out = 2 if last else 3

    def body(*refs):
        q_ref, kp_ref, kc_ref, vp_ref, vc_ref = refs[:5]
        st_refs = () if first else refs[5:8]
        out_refs = refs[-n_out:]
        ok = jnp.concatenate(_band_masks(pl.program_id(0) > 0), axis=1)

        def one_pass(it, carry):
            units = _units(it, d)
            kcat = _stacked(_tiles(kp_ref, units), _tiles(kc_ref, units))
            vcat = _stacked(_tiles(vp_ref, units), _tiles(vc_ref, units))
            s = jnp.where(ok, _bdg(_per_head(_tiles(q_ref, units)), _both(kcat), 2, 2) * ATTN_SCALE, NEG_BIG)
            m_new = jnp.max(s, axis=2, keepdims=True)
            if not first:
                m_old = _head_cols(_tiles(st_refs[0], units))
                m_new = jnp.maximum(m_old, m_new)
                alpha = jnp.exp(m_old - m_new)
            p = jnp.exp(s - m_new)
            l_new = jnp.sum(p, axis=2, keepdims=True)
            acc = _join_heads(_bdg(p, _per_head(vcat), 2, 1))
            if not first:
                l_new = l_new + _head_cols(_tiles(st_refs[1], units)) * alpha
                acc = [a + a_in * sp for a, a_in, sp in zip(acc, _tiles(st_refs[2], units), _spread_heads(alpha))]
            m_pair, l_pair = _spread_heads(m_new), _spread_heads(l_new)
            if last:
                _store_tiles(out_refs[0], units, [a / l for a, l in zip(acc, l_pair)])
                _store_tiles(out_refs[1], units, [m + jnp.log(l) for m, l in zip(m_pair, l_pair)])
            else:
                _store_tiles(out_refs[0], units, m_pair)
                _store_tiles(out_refs[1], units, l_pair)
                _store_tiles(out_refs[2], units, acc)
            return carry

        lax.fori_loop(0, _passes(d), one_pass, 0)

    cur, prev = _pair_specs(d, lambda n: n), _pair_specs(d, lambda n: jnp.maximum(n - 1, 0))
    st = cur(0)
    return tuple(pl.pallas_call(
        body, name=name, grid=(nb, ATTN_PAIRS // _pairs_per_step(d)),
        in_specs=[cur(0), prev(1), cur(1), prev(2), cur(2)] + ([] if first else [st] * 3),
        out_specs=[st] * n_out, out_shape=[jax.ShapeDtypeStruct((s_len, ATTN_WIDTH), F32)] * n_out,
        compiler_params=_params("arbitrary", "parallel"),
    )(*([proj] * 5 + ([] if first else list(state)))))


def _dattn_forward(proj, tag):
    state = None
    for i, d in enumerate(DILATIONS):
        state = _sattn_fwd(proj, state, d, last=i == len(DILATIONS) - 1, name=f"{tag}_attn_d{d}")
    return state


def _sattn_bwd(proj, dcat, lse, delta, grads_in, d, *, name):
    s_len = proj.shape[0]
    nb = s_len // (ATTN_BLOCK * d)
    first = grads_in is None
    groups = ATTN_PAIRS // _pairs_per_step(d)

    def body(*refs):
        q_ref, kp_ref, kc_ref, vp_ref, vc_ref, do_ref, lse_ref, dl_ref = refs[:8]
        dq_in, dk_in, dv_in = (None, None, None) if first else refs[8:11]
        dq_ref, dk_ref, dv_ref, carry_k, carry_v = refs[-5:]
        n = pl.program_id(1)
        ok = jnp.concatenate(_band_masks(n > 0), axis=1)

        @pl.when(n == 0)
        def _():
            carry_k[...] = jnp.zeros_like(carry_k)
            carry_v[...] = jnp.zeros_like(carry_v)

        def leave(ref, carry, units, extra, into):
            out = [c + e for c, e in zip(_tiles(carry, units), extra)] if extra else _tiles(carry, units)
            if into is not None:
                out = [a + b for a, b in zip(out, _tiles(into, units))]
            _store_tiles(ref, units, out)

        def one_pass(it, carry):
            units = _units(it, d)
            kcat = _stacked(_tiles(kp_ref, units), _tiles(kc_ref, units))
            vcat = _stacked(_tiles(vp_ref, units), _tiles(vc_ref, units))
            q2, do2 = _per_head(_tiles(q_ref, units)), _per_head(_tiles(do_ref, units))
            s = _bdg(q2, _both(kcat), 2, 2) * ATTN_SCALE
            pr = jnp.where(ok, jnp.exp(jnp.where(ok, s, NEG_BIG) - _head_cols(_tiles(lse_ref, units))), 0.0)
            ds = pr * (_bdg(do2, _both(vcat), 2, 2) - _head_cols(_tiles(dl_ref, units)))
            dq = [t * ATTN_SCALE for t in _join_heads(_bdg(ds, _per_head(kcat), 2, 1))]
            if not first:
                dq = [a + b for a, b in zip(dq, _tiles(dq_in, units))]
            _store_tiles(dq_ref, units, dq)
            dk = [t * ATTN_SCALE for t in _join_heads(_bdg(ds, q2, 1, 1))]
            dv = _join_heads(_bdg(pr, do2, 1, 1))
            leave(dk_ref, carry_k, units, [t[:ATTN_BLOCK] for t in dk], dk_in)
            leave(dv_ref, carry_v, units, [t[:ATTN_BLOCK] for t in dv], dv_in)
            _store_tiles(carry_k, units, [t[ATTN_BLOCK:] for t in dk])
            _store_tiles(carry_v, units, [t[ATTN_BLOCK:] for t in dv])
            return carry

        def last_pass(it, carry):
            units = _units(it, d)
            leave(dk_ref, carry_k, units, None, dk_in)
            leave(dv_ref, carry_v, units, None, dv_in)
            return carry

        @pl.when(n < nb)
        def _():
            lax.fori_loop(0, _passes(d), one_pass, 0)

        @pl.when(n == nb)
        def _():
            lax.fori_loop(0, _passes(d), last_pass, 0)

    pairs = _pairs_per_step(d)
    blk = (ATTN_BLOCK * d, LANES * pairs)
    at = lambda n_of: (lambda c: pl.BlockSpec(blk, lambda p, n: (n_of(n), c * groups + p)))
    here = lambda n: jnp.minimum(n, nb - 1)
    cur, prev, lag = at(here), at(lambda n: jnp.maximum(here(n) - 1, 0)), at(lambda n: jnp.maximum(n - 1, 0))
    st, st_lag = cur(0), lag(0)
    return tuple(pl.pallas_call(
        body, name=name, grid=(groups, nb + 1),
        in_specs=[cur(0), prev(1), cur(1), prev(2), cur(2), st, st, st] + ([] if first else [st, st_lag, st_lag]),
        out_specs=[st, st_lag, st_lag], out_shape=[jax.ShapeDtypeStruct((s_len, ATTN_WIDTH), F32)] * 3,
        scratch_shapes=[pltpu.VMEM(blk, F32)] * 2, compiler_params=_params("parallel", "arbitrary"),
    )(*([proj] * 5 + [dcat, lse, delta] + ([] if first else list(grads_in)))))


def _dattn_backward(proj, o, lse, dcat, tag):
    delta = _dattn_delta(o, dcat, name=f"{tag}_attn_delta")
    grads = None
    for d in DILATIONS:
        grads = _sattn_bwd(proj, dcat, lse, delta, grads, d, name=f"{tag}_attn_bwd_d{d}")
    return grads


CONV_TC = 512
CONV_T = 512


def _conv_tiles(s_len, cb0, width):
    wide = 2 * CONV_TC
    tc = wide if width % wide == 0 and (cb0 * CONV_TC) % wide == 0 else CONV_TC
    return _tile(s_len, (CONV_T, CONV_T // 2)), tc, cb0 * CONV_TC // tc


def _shift_down(ext, k, t):
    return (pltpu.roll(ext, k, 0) if k else ext)[SUBLANES:SUBLANES + t]


def _conv_fwd(src, cb0, width, w8, *, name):
    s_len = src.shape[0]
    t, tc, cb = _conv_tiles(s_len, cb0, width)
    tpb = t // SUBLANES

    def body(x_ref, h_ref, w_ref, y_ref):
        halo = jnp.where(pl.program_id(0) > 0, h_ref[...], 0.0)
        ext = jnp.concatenate([halo, x_ref[...]], axis=0)
        w = w_ref[...]
        y = jnp.broadcast_to(w[CONV_K:CONV_K + 1], (t, tc))
        for k in range(CONV_K):
            y = y + w[k:k + 1] * _shift_down(ext, CONV_K - 1 - k, t)
        y_ref[...] = y

    return pl.pallas_call(
        body, name=name, grid=(s_len // t, width // tc),
        in_specs=[pl.BlockSpec((t, tc), lambda i, j: (i, cb + j)),
                  pl.BlockSpec((SUBLANES, tc), lambda i, j: (jnp.maximum(i * tpb - 1, 0), cb + j)),
                  pl.BlockSpec((SUBLANES, tc), lambda i, j: (0, j))],
        out_specs=pl.BlockSpec((t, tc), lambda i, j: (i, j)), out_shape=jax.ShapeDtypeStruct((s_len, width), F32),
        compiler_params=_params("parallel", "parallel"),
    )(src, src, w8)


def _conv_bwd(src, cb0, width, w8, dy, *, name):
    s_len = src.shape[0]
    t, tc, cb = _conv_tiles(s_len, cb0, width)
    tpb = t // SUBLANES
    ni = s_len // t

    def body(x_ref, h_ref, w_ref, dy_ref, dn_ref, dx_ref, dw_ref):
        i = pl.program_id(1)
        halo = jnp.where(i > 0, h_ref[...], 0.0)
        ext = jnp.concatenate([halo, x_ref[...]], axis=0)
        dyv = dy_ref[...]
        extn = jnp.concatenate([dyv, jnp.where(i < ni - 1, dn_ref[...], 0.0)], axis=0)
        w = w_ref[...]
        row = _rows((SUBLANES, tc))
        dx = jnp.zeros((t, tc), F32)
        dw = jnp.where(row == CONV_K, jnp.sum(dyv, axis=0, keepdims=True), 0.0)
        for k in range(CONV_K):
            up = CONV_K - 1 - k
            dx = dx + w[k:k + 1] * (pltpu.roll(extn, t + SUBLANES - up, 0) if up else extn)[:t]
            dw = dw + jnp.where(row == k, jnp.sum(dyv * _shift_down(ext, up, t), axis=0, keepdims=True), 0.0)
        dx_ref[...] = dx.astype(dx_ref.dtype)

        @pl.when(i == 0)
        def _():
            dw_ref[...] = jnp.zeros_like(dw_ref)

        dw_ref[...] += dw

    return pl.pallas_call(
        body, name=name, grid=(width // tc, ni),
        in_specs=[pl.BlockSpec((t, tc), lambda j, i: (i, cb + j)),
                  pl.BlockSpec((SUBLANES, tc), lambda j, i: (jnp.maximum(i * tpb - 1, 0), cb + j)),
                  pl.BlockSpec((SUBLANES, tc), lambda j, i: (0, j)),
                  pl.BlockSpec((t, tc), lambda j, i: (i, j)),
                  pl.BlockSpec((SUBLANES, tc), lambda j, i: (jnp.minimum((i + 1) * tpb, s_len // SUBLANES - 1), j))],
        out_specs=[pl.BlockSpec((t, tc), lambda j, i: (i, j)), pl.BlockSpec((SUBLANES, tc), lambda j, i: (0, j))],
        out_shape=[jax.ShapeDtypeStruct((s_len, width), MXU_DTYPE), jax.ShapeDtypeStruct((SUBLANES, width), F32)],
        compiler_params=_params("parallel", "arbitrary"),
    )(src, src, w8, dy, dy)


LRU_T = 256


def _lru_gates(xc, wa, wx, ba, bx, lam):
    r = _sigmoid(_bdot(xc, wa) + ba)
    i = _sigmoid(_bdot(xc, wx) + bx)
    log_a = (-LRU_C) * r * _softplus(-lam)
    return jnp.exp(log_a), jnp.sqrt(_neg_expm1(2.0 * log_a)) * i * xc


def _block_scan(a, b, state, reverse):
    t = a.shape[0]
    row = _rows(a.shape) % SUBLANES
    s = 1
    while s < SUBLANES:
        shift, ok = (t - s, row < SUBLANES - s) if reverse else (s, row >= s)
        b = jnp.where(ok, a * pltpu.roll(b, shift, 0) + b, b)
        a = jnp.where(ok, a * pltpu.roll(a, shift, 0), a)
        s *= 2
    groups = range(t // SUBLANES)
    out = [None] * len(groups)
    for g in (reversed(groups) if reverse else groups):
        rows = slice(g * SUBLANES, (g + 1) * SUBLANES)
        out[g] = b[rows] + a[rows] * state
        state = out[g][0:1] if reverse else out[g][SUBLANES - 1:SUBLANES]
    return jnp.concatenate(out, axis=0)


def _lru_fwd(xc, proj, wa, wx, ba, bx, lam, *, name):
    s_len, w = xc.shape
    t = _tile(s_len, (LRU_T,))

    def body(xc_ref, gr_ref, wa_ref, wx_ref, ba_ref, bx_ref, lam_ref, h_ref, y_ref, carry):
        @pl.when(pl.program_id(0) == 0)
        def _():
            carry[...] = jnp.zeros_like(carry)

        a, b = _lru_gates(xc_ref[...], wa_ref[...], wx_ref[...], ba_ref[...], bx_ref[...], lam_ref[...])
        h = _block_scan(a, b, carry[0:1, :], False)
        h_ref[...] = h
        y_ref[...] = (h * _gelu(gr_ref[...])).astype(y_ref.dtype)
        carry[0:1, :] = h[t - 1:t, :]

    row = pl.BlockSpec((t, w), lambda i: (i, 0))
    mat = pl.BlockSpec((w, w), lambda i: (0, 0))
    vec = pl.BlockSpec((1, w), lambda i: (0, 0))
    return pl.pallas_call(
        body, name=name, grid=(s_len // t,),
        in_specs=[row, pl.BlockSpec((t, w), lambda i: (i, PROJ_AB_BLOCKS - 1)), mat, mat, vec, vec, vec],
        out_specs=[row, row], out_shape=[jax.ShapeDtypeStruct((s_len, w), F32), jax.ShapeDtypeStruct((s_len, w), MXU_DTYPE)],
        scratch_shapes=[pltpu.VMEM((SUBLANES, w), F32)], compiler_params=_params("arbitrary"),
    )(xc, proj, wa, wx, ba, bx, lam)


def _lru_bwd(xc, proj, hs, dcat, wa, wx, ba, bx, lam, *, name):
    s_len, w = xc.shape
    t = _tile(s_len, (LRU_T,))
    nb = s_len // t
    tpb = t // SUBLANES

    def body(xc_ref, gr_ref, h_ref, hp_ref, dy_ref, wa_ref, wx_ref, ba_ref, bx_ref, lam_ref,
             dxc_ref, dgr_ref, dwa_ref, dwx_ref, dba_ref, dbx_ref, dlam_ref, carry):
        step = pl.program_id(0)
        params = (wa_ref[...], wx_ref[...], ba_ref[...], bx_ref[...], lam_ref[...])

        @pl.when(step == 0)
        def _():
            carry[...] = jnp.zeros_like(carry)
            for r in (dwa_ref, dwx_ref, dba_ref, dbx_ref, dlam_ref):
                r[...] = jnp.zeros_like(r)

        (a, _), vjp = jax.vjp(_lru_gates, xc_ref[...], *params)
        gr, h, dy = gr_ref[...], h_ref[...], dy_ref[...]
        gel, gel_vjp = jax.vjp(_gelu, gr)
        dgr_ref[...] = gel_vjp(dy * h)[0].astype(dgr_ref.dtype)
        dh = dy * gel
        big_g = _block_scan(a, a * dh, carry[0:1, :], True)
        row = _rows((t, w))
        g = dh + jnp.where(row == t - 1, carry[0:1, :], pltpu.roll(big_g, t - 1, 0))
        carry[0:1, :] = big_g[0:1, :]
        h_last = jnp.where(step < nb - 1, hp_ref[SUBLANES - 1:SUBLANES, :], 0.0)
        h_prev = jnp.where(row == 0, h_last, pltpu.roll(h, 1, 0))
        dxc, dwa, dwx, dba, dbx, dlam = vjp((g * h_prev, g))
        dxc_ref[...] = dxc
        dwa_ref[...] += dwa
        dwx_ref[...] += dwx
        dba_ref[...] += dba
        dbx_ref[...] += dbx
        dlam_ref[...] += dlam

    rev = lambda i: nb - 1 - i
    row = pl.BlockSpec((t, w), lambda i: (rev(i), 0))
    mat = pl.BlockSpec((w, w), lambda i: (0, 0))
    vec = pl.BlockSpec((1, w), lambda i: (0, 0))
    return pl.pallas_call(
        body, name=name, grid=(nb,),
        in_specs=[row, pl.BlockSpec((t, w), lambda i: (rev(i), PROJ_AB_BLOCKS - 1)), row,
                  pl.BlockSpec((SUBLANES, w), lambda i: (jnp.maximum(rev(i) * tpb - 1, 0), 0)),
                  pl.BlockSpec((t, w), lambda i: (rev(i), 1)), mat, mat, vec, vec, vec],
        out_specs=[row, row, mat, mat, vec, vec, vec],
        out_shape=[jax.ShapeDtypeStruct((s_len, w), F32), jax.ShapeDtypeStruct((s_len, w), MXU_DTYPE)]
        + [jax.ShapeDtypeStruct((w, w), F32)] * 2 + [jax.ShapeDtypeStruct((1, w), F32)] * 3,
        scratch_shapes=[pltpu.VMEM((SUBLANES, w), F32)], compiler_params=_params("arbitrary"),
    )(xc, proj, hs, hs, dcat, wa, wx, ba, bx, lam)


XA_T = 256
XA_SCALE = XA_HEAD_DIM ** -0.5


def _xa_heads(q, k, v):
    s = _bmm_nt(q, k) * XA_SCALE
    e = jnp.exp(s - jnp.max(s, axis=-1, keepdims=True))
    return _bmm(e / jnp.sum(e, axis=-1, keepdims=True), v)


def _xa_stack(ref):
    return jnp.stack([ref[:, h * XA_HEAD_DIM:(h + 1) * XA_HEAD_DIM].astype(F32) for h in range(XA_HEADS)], axis=0)


def _xa_fwd(q, kv, *, name):
    s_len, d = q.shape
    n_mem = kv.shape[0]
    t = _tile(s_len, (XA_T,))

    def body(q_ref, k_ref, v_ref, o_ref):
        o = _xa_heads(_xa_stack(q_ref), _xa_stack(k_ref), _xa_stack(v_ref))
        for h in range(XA_HEADS):
            o_ref[:, h * XA_HEAD_DIM:(h + 1) * XA_HEAD_DIM] = o[h].astype(o_ref.dtype)

    return pl.pallas_call(
        body, name=name, grid=(s_len // t,),
        in_specs=[pl.BlockSpec((t, d), lambda i: (i, 0)), pl.BlockSpec((n_mem, d), lambda i: (0, 0)),
                  pl.BlockSpec((n_mem, d), lambda i: (0, 1))],
        out_specs=pl.BlockSpec((t, d), lambda i: (i, 0)), out_shape=jax.ShapeDtypeStruct((s_len, d), MXU_DTYPE),
        compiler_params=_params("parallel"),
    )(q, kv, kv)


def _xa_bwd(q, kv, dy, wo, *, name):
    s_len, d = q.shape
    n_mem = kv.shape[0]
    t = _tile(s_len, (XA_T,))

    def body(q_ref, k_ref, v_ref, dy_ref, wo_ref, dq_ref, dk_ref, dv_ref):
        @pl.when(pl.program_id(0) == 0)
        def _():
            dk_ref[...] = jnp.zeros_like(dk_ref)
            dv_ref[...] = jnp.zeros_like(dv_ref)

        do = _dg(dy_ref[...], wo_ref[...], ((1,), (1,)))
        do = jnp.stack([do[:, h * XA_HEAD_DIM:(h + 1) * XA_HEAD_DIM] for h in range(XA_HEADS)], axis=0)
        _, vjp = jax.vjp(_xa_heads, _xa_stack(q_ref), _xa_stack(k_ref), _xa_stack(v_ref))
        dq, dk, dv = vjp(do)
        for h in range(XA_HEADS):
            sl = slice(h * XA_HEAD_DIM, (h + 1) * XA_HEAD_DIM)
            dq_ref[:, sl] = dq[h].astype(dq_ref.dtype)
            dk_ref[:, sl] += dk[h]
            dv_ref[:, sl] += dv[h]

    row = pl.BlockSpec((t, d), lambda i: (i, 0))
    dq, dk, dv = pl.pallas_call(
        body, name=name, grid=(s_len // t,),
        in_specs=[row, pl.BlockSpec((n_mem, d), lambda i: (0, 0)), pl.BlockSpec((n_mem, d), lambda i: (0, 1)), row,
                  pl.BlockSpec(wo.shape, lambda i: (0, 0), pipeline_mode=RESIDENT)],
        out_specs=[row, pl.BlockSpec((n_mem, d), lambda i: (0, 0)), pl.BlockSpec((n_mem, d), lambda i: (0, 0))],
        out_shape=[jax.ShapeDtypeStruct((s_len, d), MXU_DTYPE)] + [jax.ShapeDtypeStruct((n_mem, d), F32)] * 2,
        compiler_params=_params("arbitrary"),
    )(q, kv, kv, dy, wo)
    return dq, jnp.concatenate([dk, dv], axis=1)


DN_Q_SCALE = DN_HEAD_DIM ** -0.5
L2_EPS = 1e-6


def _bdg(a, b, ca, cb):
    return lax.dot_general(a.astype(MXU_DTYPE), b.astype(MXU_DTYPE), (((ca,), (cb,)), ((0,), (0,))), preferred_element_type=F32)


@jax.custom_vjp
def _bmm(a, b):
    return _bdg(a, b, 2, 1)


_bmm.defvjp(lambda a, b: (_bdg(a, b, 2, 1), (a, b)), lambda r, g: (_bdg(g, r[1], 2, 2), _bdg(r[0], g, 1, 1)))


@jax.custom_vjp
def _bmm_nt(a, b):
    return _bdg(a, b, 2, 2)


_bmm_nt.defvjp(lambda a, b: (_bdg(a, b, 2, 2), (a, b)), lambda r, g: (_bdg(g, r[1], 2, 1), _bdg(g, r[0], 1, 1)))


@jax.custom_vjp
def _bmm_tn(a, b):
    return _bdg(a, b, 1, 1)


_bmm_tn.defvjp(lambda a, b: (_bdg(a, b, 1, 1), (a, b)), lambda r, g: (_bdg(r[1], g, 2, 2), _bdg(r[0], g, 2, 1)))


def _tri_inverse(n):
    eye = (lax.broadcasted_iota(jnp.int32, n.shape, 1) == lax.broadcasted_iota(jnp.int32, n.shape, 2)).astype(F32)
    inv, p = eye - n, n
    for _ in range(5):
        p = _bdg(p, p, 2, 1)
        inv = _bdg(inv, eye + p, 2, 1)
    return inv


@jax.custom_vjp
def _tri_solve2(n, r1, r2):
    t = _tri_inverse(n)
    return _bdg(t, r1, 2, 1), _bdg(t, r2, 2, 1)


def _tri_solve2_fwd(n, r1, r2):
    t = _tri_inverse(n)
    x1, x2 = _bdg(t, r1, 2, 1), _bdg(t, r2, 2, 1)
    return (x1, x2), (t, x1, x2)


def _tri_solve2_bwd(saved, cts):
    t, x1, x2 = saved
    d1, d2 = _bdg(t, cts[0], 1, 1), _bdg(t, cts[1], 1, 1)
    return -(_bdg(d1, x1, 2, 2) + _bdg(d2, x2, 2, 2)), d1, d2


_tri_solve2.defvjp(_tri_solve2_fwd, _tri_solve2_bwd)


def _dn_gates(ab, alog, dtb):
    return -jnp.exp(alog) * _softplus(ab + dtb), _sigmoid(ab)


def _dn_heads(cq, ck, cv, z, g, beta, onorm, state):
    h, c, _ = cq.shape
    l2 = lambda t: t * lax.rsqrt(jnp.sum(t * t, axis=-1, keepdims=True) + L2_EPS)
    q, k, v = l2(_silu(cq)) * DN_Q_SCALE, l2(_silu(ck)), _silu(cv)
    r, cc = lax.broadcasted_iota(jnp.int32, (h, c, c), 1), lax.broadcasted_iota(jnp.int32, (h, c, c), 2)
    tri, eye = r >= cc, r == cc
    g_row = jnp.sum(jnp.where(eye, g, 0.0), axis=1, keepdims=True)
    gcum_c = jnp.sum(jnp.where(tri, g_row, 0.0), axis=2, keepdims=True)
    gcum_r = jnp.sum(jnp.where(cc >= r, g, 0.0), axis=1, keepdims=True)
    decay = jnp.where(tri, jnp.exp(jnp.where(tri, gcum_c - gcum_r, 0.0)), 0.0)
    kb = k * beta
    n = jnp.where(r > cc, _bmm_nt(kb, k) * decay, 0.0)
    u, w = _tri_solve2(n, v * beta, kb * jnp.exp(gcum_c))
    v_new = u - _bmm(w, state)
    o = _bmm(q * jnp.exp(gcum_c), state) + _bmm(_bmm_nt(q, k) * decay, v_new)
    g_last = jnp.sum(g, axis=1, keepdims=True)
    new_state = state * jnp.exp(g_last) + _bmm_tn(k * jnp.exp(g_last - gcum_c), v_new)
    on = o * lax.rsqrt(jnp.mean(o * o, axis=-1, keepdims=True) + NORM_EPS) * onorm
    return on * _silu(z), new_state


def _dn_stack(ref, col0):
    return jnp.stack([ref[:, col0 + h * DN_HEAD_DIM:col0 + (h + 1) * DN_HEAD_DIM].astype(F32) for h in range(DN_HEADS)], axis=0)


def _dn_cols(block, col0):
    return jnp.stack([block[:, col0 + h:col0 + h + 1] for h in range(DN_HEADS)], axis=0)


def _dn_fwd(cqkv, proj, ab, alog, dtb, onorm, *, name):
    s_len = cqkv.shape[0]
    c, hd, w = DN_CHUNK, DN_HEAD_DIM, DN_WIDTH
    n_chunks = s_len // c

    def body(c_ref, z_ref, ab_ref, alog_ref, dtb_ref, on_ref, o_ref, st_ref, state):
        @pl.when(pl.program_id(0) == 0)
        def _():
            state[...] = jnp.zeros_like(state)

        g_all, beta_all = _dn_gates(ab_ref[...], alog_ref[...], dtb_ref[...])
        st = state[...]
        st_ref[0] = st
        out, new = _dn_heads(_dn_stack(c_ref, 0), _dn_stack(c_ref, w), _dn_stack(c_ref, 2 * w), _dn_stack(z_ref, 0),
                             _dn_cols(g_all, 0), _dn_cols(beta_all, DN_HEADS), on_ref[...], st)
        state[...] = new
        for h in range(DN_HEADS):
            o_ref[:, h * hd:(h + 1) * hd] = out[h].astype(o_ref.dtype)

    vec = pl.BlockSpec((1, LANES), lambda i: (0, 0))
    return pl.pallas_call(
        body, name=name, grid=(n_chunks,),
        in_specs=[pl.BlockSpec((c, 3 * w), lambda i: (i, 0)), pl.BlockSpec((c, w), lambda i: (i, 3)),
                  pl.BlockSpec((c, LANES), lambda i: (i, 0)), vec, vec, vec],
        out_specs=[pl.BlockSpec((c, w), lambda i: (i, 0)), pl.BlockSpec((1, DN_HEADS, hd, hd), lambda i: (i, 0, 0, 0))],
        out_shape=[jax.ShapeDtypeStruct((s_len, w), MXU_DTYPE), jax.ShapeDtypeStruct((n_chunks, DN_HEADS, hd, hd), F32)],
        scratch_shapes=[pltpu.VMEM((DN_HEADS, hd, hd), F32)], compiler_params=_params("arbitrary"),
    )(cqkv, proj, ab, alog, dtb, onorm)


def _dn_bwd(cqkv, proj, ab, alog, dtb, onorm, states, dout, *, name):
    s_len = cqkv.shape[0]
    c, hd, w = DN_CHUNK, DN_HEAD_DIM, DN_WIDTH
    n_chunks = s_len // c

    def body(c_ref, z_ref, ab_ref, alog_ref, dtb_ref, on_ref, st_ref, do_ref,
             dc_ref, dz_ref, dab_ref, dalog_ref, ddtb_ref, don_ref, dstate):
        @pl.when(pl.program_id(0) == 0)
        def _():
            dstate[...] = jnp.zeros_like(dstate)
            for r in (dalog_ref, ddtb_ref, don_ref):
                r[...] = jnp.zeros_like(r)

        (g_all, beta_all), gates_vjp = jax.vjp(_dn_gates, ab_ref[...], alog_ref[...], dtb_ref[...])
        _, vjp = jax.vjp(_dn_heads, _dn_stack(c_ref, 0), _dn_stack(c_ref, w), _dn_stack(c_ref, 2 * w), _dn_stack(z_ref, 0),
                         _dn_cols(g_all, 0), _dn_cols(beta_all, DN_HEADS), on_ref[...], st_ref[0])
        dcq, dck, dcv, dz, dg, dbeta, don, dst = vjp((_dn_stack(do_ref, 0), dstate[...]))
        dstate[...] = dst
        col = _cols((c, LANES))
        dg_all, dbeta_all = jnp.zeros((c, LANES), F32), jnp.zeros((c, LANES), F32)
        for h in range(DN_HEADS):
            sl = slice(h * hd, (h + 1) * hd)
            dc_ref[:, sl] = dcq[h]
            dc_ref[:, w + h * hd:w + (h + 1) * hd] = dck[h]
            dc_ref[:, 2 * w + h * hd:2 * w + (h + 1) * hd] = dcv[h]
            dz_ref[:, sl] = dz[h].astype(dz_ref.dtype)
            dg_all = dg_all + jnp.where(col == h, dg[h], 0.0)
            dbeta_all = dbeta_all + jnp.where(col == DN_HEADS + h, dbeta[h], 0.0)
        dab, dalog, ddtb = gates_vjp((dg_all, dbeta_all))
        dab_ref[...] = dab
        dalog_ref[...] += dalog
        ddtb_ref[...] += ddtb
        don_ref[...] += don

    rev = lambda i: n_chunks - 1 - i
    vec = pl.BlockSpec((1, LANES), lambda i: (0, 0))
    return pl.pallas_call(
        body, name=name, grid=(n_chunks,),
        in_specs=[pl.BlockSpec((c, 3 * w), lambda i: (rev(i), 0)), pl.BlockSpec((c, w), lambda i: (rev(i), 3)),
                  pl.BlockSpec((c, LANES), lambda i: (rev(i), 0)), vec, vec, vec,
                  pl.BlockSpec((1, DN_HEADS, hd, hd), lambda i: (rev(i), 0, 0, 0)), pl.BlockSpec((c, w), lambda i: (rev(i), 0))],
        out_specs=[pl.BlockSpec((c, 3 * w), lambda i: (rev(i), 0)), pl.BlockSpec((c, w), lambda i: (rev(i), 0)),
                   pl.BlockSpec((c, LANES), lambda i: (rev(i), 0)), vec, vec, vec],
        out_shape=[jax.ShapeDtypeStruct((s_len, 3 * w), F32), jax.ShapeDtypeStruct((s_len, w), MXU_DTYPE),
                   jax.ShapeDtypeStruct((s_len, LANES), F32)] + [jax.ShapeDtypeStruct((1, LANES), F32)] * 3,
        scratch_shapes=[pltpu.VMEM((DN_HEADS, hd, hd), F32)], compiler_params=_params("arbitrary"),
    )(cqkv, proj, ab, alog, dtb, onorm, states, dout)


def _final_loss(x, g, target, *, name):
    s, d = x.shape
    tm = _tile(s, (512, 256))

    def body(x_ref, g_ref, t_ref, loss_ref, dx_ref, dg_ref):
        @pl.when(pl.program_id(0) == 0)
        def _():
            loss_ref[...] = jnp.zeros_like(loss_ref)
            dg_ref[...] = jnp.zeros_like(dg_ref)

        xv, gv = x_ref[...], g_ref[...]
        r = lax.rsqrt(jnp.mean(xv * xv, axis=-1, keepdims=True) + NORM_EPS)
        xh = xv * r
        err = xh * gv - t_ref[...]
        loss_ref[...] += 0.5 * jnp.sum(jnp.mean(err * err, axis=-1, keepdims=True), axis=0, keepdims=True)
        dy = err * (1.0 / d)
        dxh = dy * gv
        dx_ref[...] = r * (dxh - xh * jnp.mean(dxh * xh, axis=-1, keepdims=True))
        dg_ref[...] += jnp.sum(dy * xh, axis=0, keepdims=True)

    row = pl.BlockSpec((tm, d), lambda i: (i, 0))
    vec = pl.BlockSpec((1, d), lambda i: (0, 0))
    return pl.pallas_call(
        body, name=name, grid=(s // tm,), in_specs=[row, vec, row],
        out_specs=[pl.BlockSpec((1, LANES), lambda i: (0, 0)), row, vec],
        out_shape=[jax.ShapeDtypeStruct((1, LANES), F32), jax.ShapeDtypeStruct((s, d), F32), jax.ShapeDtypeStruct((1, d), F32)],
        compiler_params=_params("arbitrary"),
    )(x, g.reshape(1, d), target)


def _adamw(w, g, m, v, *, name):
    shape = w.shape
    cols = shape[-1]
    rows = max(w.size // cols, 1)
    tr = _tile(rows, (512, 352, 256, 128, 64, 32, 16, 8))
    c1, c2 = 1.0 - ADAM_B1 ** ADAM_STEP, 1.0 - ADAM_B2 ** ADAM_STEP

    def body(w_ref, g_ref, m_ref, v_ref, d_ref, nm_ref, nv_ref):
        gv = g_ref[...]
        nm = ADAM_B1 * m_ref[...] + (1.0 - ADAM_B1) * gv
        nv = ADAM_B2 * v_ref[...] + (1.0 - ADAM_B2) * (gv * gv)
        d_ref[...] = -ADAM_LR * ((nm / c1) / (jnp.sqrt(nv / c2) + ADAM_EPS) + ADAM_WD * w_ref[...])
        nm_ref[...] = nm
        nv_ref[...] = nv

    blk = pl.BlockSpec((tr, cols), lambda i: (i, 0))
    outs = pl.pallas_call(
        body, name=name, grid=(rows // tr,), in_specs=[blk] * 4, out_specs=[blk] * 3,
        out_shape=[jax.ShapeDtypeStruct((rows, cols), F32)] * 3, compiler_params=_params("parallel"),
    )(*(t.reshape(rows, cols) for t in (w, g, m, v)))
    return tuple(t.reshape(shape) for t in outs)


def _block_diag(w):
    n, j, k = w.shape
    eye = jnp.eye(n, dtype=w.dtype)
    return (eye[:, None, :, None] * w[:, :, None, :]).reshape(n * j, n * k)


def _block_diag_part(m, n):
    j, k = m.shape[0] // n, m.shape[1] // n
    m4 = m.reshape(n, j, n, k)
    return jnp.stack([m4[i, :, i, :] for i in range(n)], axis=0)


DN_AB = 2 * DN_HEADS
DEPTH = 2


def _row(v, width=None):
    v = v.reshape(1, -1)
    return v if width is None else jnp.pad(v, ((0, 0), (0, width - v.shape[1])))


def _conv_w8(conv_w, bias=None):
    w8 = jnp.zeros((SUBLANES, conv_w.shape[1]), F32).at[:CONV_K].set(conv_w)
    return w8 if bias is None else w8.at[CONV_K].set(bias)


def _mixer_ab_fwd(x, w, tag):
    h, (proj,) = _norm_mm(x, w["mix_norm"][0], [w["ab_w_in"][0]], [F32], name=f"{tag}_in")
    o, lse = _dattn_forward(proj, tag)
    w8 = _conv_w8(w["lru_conv_w"][0], w["lru_conv_b"][0])
    xc = _conv_fwd(proj, PROJ_AB_BLOCKS - 2, LRU_WIDTH, w8, name=f"{tag}_conv")
    wa, wx = _block_diag(w["lru_w_a"][0]), _block_diag(w["lru_w_x"][0])
    vecs = (_row(w["lru_b_a"][0]), _row(w["lru_b_x"][0]), _row(w["lru_lambda"][0]))
    hs, y = _lru_fwd(xc, proj, wa, wx, *vecs, name=f"{tag}_lru")
    w_out = w["ab_w_out"][0]
    x2 = _mm(o, w_out[:ATTN_WIDTH], res=x, name=f"{tag}_out_attn")
    x2 = _mm(y, w_out[ATTN_WIDTH:], res=x2, name=f"{tag}_out_lru")
    return x2, (x, h, proj, o, lse, w8, xc, wa, wx, vecs, hs, y)


def _mixer_ab_bwd(saved, w, dy, tag):
    x, h, proj, o, lse, w8, xc, wa, wx, vecs, hs, y = saved
    w_out = w["ab_w_out"][0]
    dcat = _mm(dy, w_out, mode="nt", name=f"{tag}_dcat")
    dw_out = jnp.concatenate([_mm(o, dy, mode="tn", name=f"{tag}_dwout_attn"), _mm(y, dy, mode="tn", name=f"{tag}_dwout_lru")], axis=0)
    dq, dk, dv = _dattn_backward(proj, o, lse, dcat, tag)
    dxc, dgr, dwa, dwx, dba, dbx, dlam = _lru_bwd(xc, proj, hs, dcat, wa, wx, *vecs, name=f"{tag}_dlru")
    dxr, dw8 = _conv_bwd(proj, PROJ_AB_BLOCKS - 2, LRU_WIDTH, w8, dxc, name=f"{tag}_dconv")
    dproj = jnp.concatenate([t.astype(MXU_DTYPE) for t in (dq, dk, dv, dxr, dgr)], axis=1)
    dw_in = _mm(h, dproj, mode="tn", name=f"{tag}_dwin")
    dx, dg = _mm_rms_bwd([(dproj, w["ab_w_in"][0])], x, w["mix_norm"][0], dy, name=f"{tag}_dh")
    grads = dict(mix_norm=dg[0], ab_w_in=dw_in, ab_w_out=dw_out, lru_conv_w=dw8[:CONV_K], lru_conv_b=dw8[CONV_K],
                 lru_w_a=_block_diag_part(dwa, LRU_BLOCKS), lru_b_a=dba[0], lru_w_x=_block_diag_part(dwx, LRU_BLOCKS),
                 lru_b_x=dbx[0], lru_lambda=dlam[0])
    return dx, grads


def _dn_split_w(w_in):
    return w_in[:, :4 * DN_WIDTH], jnp.pad(w_in[:, 4 * DN_WIDTH:], ((0, 0), (0, LANES - DN_AB)))


def _mixer_dn_fwd(x, w, tag):
    w_qkvz, w_ab = _dn_split_w(w["dn_w_in"][0])
    h, (proj, ab) = _norm_mm(x, w["mix_norm"][1], [w_qkvz, w_ab], [F32, F32], name=f"{tag}_in")
    w8 = _conv_w8(w["dn_conv_w"][0])
    cqkv = _conv_fwd(proj, 0, 3 * DN_WIDTH, w8, name=f"{tag}_conv")
    vecs = (_row(w["dn_a_log"][0], LANES), _row(w["dn_dt_bias"][0], LANES), _row(w["dn_o_norm"][0]))
    og, states = _dn_fwd(cqkv, proj, ab, *vecs, name=f"{tag}_dn")
    x2 = _mm(og, w["dn_w_out"][0], res=x, name=f"{tag}_out")
    return x2, (x, h, w_qkvz, w_ab, proj, ab, w8, cqkv, vecs, og, states)


def _mixer_dn_bwd(saved, w, dy, tag):
    x, h, w_qkvz, w_ab, proj, ab, w8, cqkv, vecs, og, states = saved
    dout = _mm(dy, w["dn_w_out"][0], mode="nt", name=f"{tag}_dout")
    dw_out = _mm(og, dy, mode="tn", name=f"{tag}_dwout")
    dcqkv, dz, dab, dalog, ddtb, don = _dn_bwd(cqkv, proj, ab, *vecs, states, dout, name=f"{tag}_ddn")
    dqkv, dw8 = _conv_bwd(proj, 0, 3 * DN_WIDTH, w8, dcqkv, name=f"{tag}_dconv")
    dproj = jnp.concatenate([dqkv.astype(MXU_DTYPE), dz.astype(MXU_DTYPE)], axis=1)
    dw_in = jnp.concatenate([_mm(h, dproj, mode="tn", name=f"{tag}_dwin"),
                             _mm(h, dab, mode="tn", name=f"{tag}_dwin_ab")[:, :DN_AB]], axis=1)
    dx, dg = _mm_rms_bwd([(dproj, w_qkvz), (dab, w_ab)], x, w["mix_norm"][1], dy, name=f"{tag}_dh")
    grads = dict(mix_norm=dg[0], dn_w_in=dw_in, dn_w_out=dw_out, dn_conv_w=dw8[:CONV_K], dn_a_log=dalog[0, :DN_HEADS],
                 dn_dt_bias=ddtb[0, :DN_HEADS], dn_o_norm=don[0])
    return dx, grads


def _xa_layer_fwd(x, mem, w, layer, tag):
    hq, (q,) = _norm_mm(x, w["xa_norm"][layer], [w["xa_wq"][layer]], [MXU_DTYPE], name=f"{tag}_q")
    hm = _rms_fwd(mem, w["xa_mem_norm"][layer], name=f"{tag}_mem_norm")
    kv = _mm(hm, w["xa_wkv"][layer], name=f"{tag}_kv")
    oa = _xa_fwd(q, kv, name=f"{tag}_core")
    x2 = _mm(oa, w["xa_wo"][layer], res=x, name=f"{tag}_out")
    return x2, (x, hq, q, hm, kv, oa)


def _xa_layer_bwd(saved, mem, w, layer, dy, tag):
    x, hq, q, hm, kv, oa = saved
    dwo = _mm(oa, dy, mode="tn", name=f"{tag}_dwo")
    dq, dkv = _xa_bwd(q, kv, dy, w["xa_wo"][layer], name=f"{tag}_dcore")
    dwq = _mm(hq, dq, mode="tn", name=f"{tag}_dwq")
    dx, dg = _mm_rms_bwd([(dq, w["xa_wq"][layer])], x, w["xa_norm"][layer], dy, name=f"{tag}_dhq")
    dwkv = _mm(hm, dkv, mode="tn", name=f"{tag}_dwkv")
    dhm = _mm(dkv, w["xa_wkv"][layer], mode="nt", name=f"{tag}_dhm")
    _, dgm = _rms_bwd(mem, w["xa_mem_norm"][layer], dhm, jnp.zeros_like(mem), name=f"{tag}_dmem_norm")
    return dx, dict(xa_norm=dg[0], xa_mem_norm=dgm[0], xa_wq=dwq, xa_wkv=dwkv, xa_wo=dwo)


def _local_step(x, mem, target, w, pending=None, reduce_big=False):
    saved = []
    for layer in range(DEPTH):
        t = f"l{layer}"
        hosted = ([b for *_, b in pending[0]], [a for _, _, a, _ in pending[0]]) if pending and layer == 0 else None
        x, s1, gathered = _ffn_fwd(x, w["ffn1_norm"], w["ffn1_w_in"], w["ffn1_w_out"], layer, f"{t}_ffn1", gather=hosted)
        if hosted:
            _gathered(pending[0], gathered, pending[1], w)
        x, s2 = (_mixer_ab_fwd if layer % 2 == 0 else _mixer_dn_fwd)(x, w, f"{t}_mix")
        x, s3 = _xa_layer_fwd(x, mem, w, layer, f"{t}_xa")
        x, s4, _ = _ffn_fwd(x, w["ffn2_norm"], w["ffn2_w_in"], w["ffn2_w_out"], layer, f"{t}_ffn2")
        saved.append((s1, s2, s3, s4))
    loss, dx, dgf = _final_loss(x, w["final_norm"], target, name="final_loss")
    per_layer, reduced = [None] * DEPTH, [None] * DEPTH
    travelling = None
    for layer in reversed(range(DEPTH)):
        t = f"l{layer}"
        s1, s2, s3, s4 = saved[layer]
        g = {}
        dx, g["ffn2_norm"], g["ffn2_w_in"], g["ffn2_w_out"], parts = _ffn_bwd(
            s4, w["ffn2_norm"], w["ffn2_w_in"], w["ffn2_w_out"], layer, dx, f"{t}_ffn2",
            exchange=travelling[1] if travelling else None)
        if travelling:
            reduced[travelling[0]] = _reduce_end(travelling[1], parts, f"l{travelling[0]}")
        dx, gx = _xa_layer_bwd(s3, mem, w, layer, dx, f"{t}_xa")
        dx, gm = (_mixer_ab_bwd if layer % 2 == 0 else _mixer_dn_bwd)(s2, w, dx, f"{t}_mix")
        dx, g["ffn1_norm"], g["ffn1_w_in"], g["ffn1_w_out"], _ = _ffn_bwd(
            s1, w["ffn1_norm"], w["ffn1_w_in"], w["ffn1_w_out"], layer, dx, f"{t}_ffn1")
        per_layer[layer] = {**g, **gx, **gm}
        if reduce_big:
            chip_sum = _reduce_begin(_pack_grads(per_layer[layer]), t)
            if layer > 0:
                travelling = (layer, chip_sum)
            else:
                reduced[layer] = _reduce_end(chip_sum, _exchange_chips(chip_sum), t)
    grads = {"final_norm": [dgf[0]]}
    for layer_grads in per_layer:
        for name, value in layer_grads.items():
            grads.setdefault(name, []).append(value)
    return loss, dx, grads, list(zip(reduced, per_layer))


N_CHIPS = 4
WIRE_DTYPE = jnp.bfloat16
HBM_SPEC = pl.BlockSpec(memory_space=pltpu.HBM)
PACK_COLS = 1024


def _place():
    x, y, c = lax.axis_index("x"), lax.axis_index("y"), lax.axis_index("c")
    return x, y, c, [(1 - x, y), (x, 1 - y), (1 - x, 1 - y)]


def _remote(src, dst, sems, k, to):
    return pltpu.make_async_remote_copy(src_ref=src, dst_ref=dst, send_sem=sems[0].at[k], recv_sem=sems[1].at[k],
                                        device_id=to, device_id_type=MESH)


def _gather_weights(blocks, axes):
    n = len(blocks)
    out_shapes, sem_shapes, start, finish = _gather_plan(blocks, axes)

    def body(*refs):
        start(refs[:n], refs[n:2 * n], *refs[2 * n:])
        finish(refs[:n], refs[n:2 * n], *refs[2 * n:])

    return pl.pallas_call(
        body, name="gather_weights", in_specs=[HBM_SPEC] * n, out_specs=[HBM_SPEC] * n,
        out_shape=out_shapes, scratch_shapes=sem_shapes,
    )(*blocks)


def _gather_plan(blocks, axes):
    n = len(blocks)
    split = [b.shape[1] % 32 == 0 for b in blocks]

    def full_shape(i):
        l, r, c = blocks[i].shape
        return (l, N_CHIPS * r, c) if axes[i] == 1 else (l, r, N_CHIPS * c)

    def copies(ins, outs, send_sems, recv_sems):
        x, y, c, chips = _place()
        sems = (send_sems, recv_sems)
        sibling = (x, y, 1 - c)
        me = 2 * x + y

        def window(i, k, h):
            l, r, cc = blocks[i].shape
            r0, nr = (0, r) if h is None else (h * (r // 2), r // 2)
            if axes[i] == 1:
                return outs[i].at[:, pl.ds(k * r + r0, nr), :]
            return outs[i].at[:, pl.ds(r0, nr), pl.ds(k * cc, cc)]

        def mine(i, h):
            r = blocks[i].shape[1]
            return ins[i] if h is None else ins[i].at[:, pl.ds(h * (r // 2), r // 2), :]

        half = lambda i: c if split[i] else None
        first = [_remote(mine(i, half(i)), window(i, me, half(i)), sems, 3 * i + j, (*chip, c))
                 for i in range(n) for j, chip in enumerate(chips)]
        first += [_remote(ins[i], window(i, me, None), sems, 6 * n + i, sibling) for i in range(n)]
        arrive = lambda i, j, h, k, frm: _remote(window(i, 2 * chips[j][0] + chips[j][1], h), window(i, 2 * chips[j][0] + chips[j][1], h),
                                                 sems, k, frm)
        return first, arrive, chips, c, sibling

    def start(ins, outs, send_sems, recv_sems):
        for cp in copies(ins, outs, send_sems, recv_sems)[0]:
            cp.start()

    def finish(ins, outs, send_sems, recv_sems):
        first, arrive, chips, c, sibling = copies(ins, outs, send_sems, recv_sems)
        passed = []
        for i in range(n):
            for j, (cx, cy) in enumerate(chips):
                arrive(i, j, c if split[i] else None, 3 * i + j, (cx, cy, c)).wait_recv()
                if split[i]:
                    passed.append(arrive(i, j, c, 3 * (n + i) + j, sibling))
                    passed[-1].start()
        for i in range(n):
            if split[i]:
                for j in range(len(chips)):
                    arrive(i, j, 1 - c, 3 * (n + i) + j, sibling).wait_recv()
        for cp in first[3 * n:]:
            cp.wait_recv()
        for cp in first + passed:
            cp.wait_send()

    sem_shapes = [pltpu.SemaphoreType.DMA((7 * n,)), pltpu.SemaphoreType.DMA((7 * n,))]
    return [jax.ShapeDtypeStruct(full_shape(i), blocks[i].dtype) for i in range(n)], sem_shapes, start, finish


def _allreduce_small(v):
    rows, cols = v.shape
    n_dev = 2 * N_CHIPS

    def body(v_ref, out_ref, all_ref, send_sems, recv_sems, local_sem):
        x, y, c, chips = _place()
        sems = (send_sems, recv_sems)
        me, sibling = (x, y, c), (x, y, 1 - c)
        slot = lambda px, py, pc: all_ref.at[pl.ds((4 * px + 2 * py + pc) * rows, rows), :]
        mine = pltpu.make_async_copy(v_ref, slot(*me), local_sem)
        mine.start()
        first = [_remote(v_ref, slot(*me), sems, 0, sibling)]
        first += [_remote(v_ref, slot(*me), sems, 1 + j, (*chip, c)) for j, chip in enumerate(chips)]
        for cp in first:
            cp.start()
        passed = [_remote(slot(*chip, c), slot(*chip, c), sems, 4 + j, sibling) for j, chip in enumerate(chips)]
        for j, chip in enumerate(chips):
            _remote(slot(*chip, c), slot(*chip, c), sems, 1 + j, me).wait_recv()
            passed[j].start()
        _remote(slot(*sibling), slot(*sibling), sems, 0, me).wait_recv()
        for j, chip in enumerate(chips):
            _remote(slot(*chip, 1 - c), slot(*chip, 1 - c), sems, 4 + j, me).wait_recv()
        for cp in first + passed:
            cp.wait_send()
        mine.wait()
        acc = all_ref[pl.ds(0, rows), :]
        for k in range(1, n_dev):
            acc = acc + all_ref[pl.ds(k * rows, rows), :]
        out_ref[...] = acc

    vmem = pl.BlockSpec(memory_space=pltpu.VMEM)
    return pl.pallas_call(
        body, name="allreduce_small", in_specs=[vmem], out_specs=vmem, out_shape=jax.ShapeDtypeStruct((rows, cols), F32),
        scratch_shapes=[pltpu.VMEM((n_dev * rows, cols), F32), pltpu.SemaphoreType.DMA((7,)), pltpu.SemaphoreType.DMA((7,)),
                        pltpu.SemaphoreType.DMA],
    )(v)


def _swap_other_half(g4, tag):
    n, _, rows, cols = g4.shape

    def body(v_ref, out_ref, send_sems, recv_sems):
        x, y, c, _ = _place()
        cp = _remote(v_ref.at[:, 1 - c], out_ref, (send_sems, recv_sems), 0, (x, y, 1 - c))
        cp.start()
        cp.wait()

    return pl.pallas_call(
        body, name=f"{tag}_reduce_swap", in_specs=[HBM_SPEC], out_specs=HBM_SPEC, out_shape=jax.ShapeDtypeStruct((n, rows, cols), g4.dtype),
        scratch_shapes=[pltpu.SemaphoreType.DMA((1,)), pltpu.SemaphoreType.DMA((1,))],
    )(g4)


def _add_kept_half(g4, got, tag):
    n, _, rows, cols = g4.shape
    tr = _tile(rows, (256, 128, 64, 32, 16))
    nb = rows // tr

    def body(c_ref, a_ref, b_ref, o_ref):
        o_ref[...] = (a_ref[...] + b_ref[...]).astype(o_ref.dtype)

    return pl.pallas_call(
        body, name=f"{tag}_reduce_sum_cores",
        grid_spec=pltpu.PrefetchScalarGridSpec(
            num_scalar_prefetch=1, grid=(n, nb),
            in_specs=[pl.BlockSpec((None, None, tr, cols), lambda k, i, c_ref: (k, c_ref[0], i, 0)),
                      pl.BlockSpec((None, tr, cols), lambda k, i, c_ref: (k, i, 0))],
            out_specs=pl.BlockSpec((None, tr, cols), lambda k, i, c_ref: (k, i, 0))),
        out_shape=jax.ShapeDtypeStruct((n, rows, cols), WIRE_DTYPE), compiler_params=_params("parallel", "parallel"),
    )(lax.axis_index("c").astype(jnp.int32).reshape(1), g4, got)


def _exchange_plan(v):
    def copies(v_ref, out_ref, send_sems, recv_sems):
        x, y, c, chips = _place()
        return [_remote(v_ref.at[2 * cx + cy], out_ref.at[j], (send_sems, recv_sems), j, (cx, cy, c)) for j, (cx, cy) in enumerate(chips)]

    def start(*refs):
        for cp in copies(*refs):
            cp.start()

    def finish(*refs):
        for cp in copies(*refs):
            cp.wait_recv()
        for cp in copies(*refs):
            cp.wait_send()

    sem_shapes = [pltpu.SemaphoreType.DMA((N_CHIPS - 1,)), pltpu.SemaphoreType.DMA((N_CHIPS - 1,))]
    return jax.ShapeDtypeStruct((N_CHIPS - 1,) + v.shape[1:], v.dtype), sem_shapes, start, finish


def _exchange_chips(v):
    out_shape, sem_shapes, start, finish = _exchange_plan(v)

    def body(*refs):
        start(*refs)
        finish(*refs)

    return pl.pallas_call(body, name="exchange_chips", in_specs=[HBM_SPEC], out_specs=HBM_SPEC, out_shape=out_shape,
                          scratch_shapes=sem_shapes)(v)


def _swap_sibling(v, tag):
    def body(v_ref, out_ref, send_sems, recv_sems):
        x, y, c, _ = _place()
        cp = _remote(v_ref, out_ref, (send_sems, recv_sems), 0, (x, y, 1 - c))
        cp.start()
        cp.wait()

    return pl.pallas_call(
        body, name=f"{tag}_share_halves", in_specs=[HBM_SPEC], out_specs=HBM_SPEC, out_shape=jax.ShapeDtypeStruct(v.shape, v.dtype),
        scratch_shapes=[pltpu.SemaphoreType.DMA((1,)), pltpu.SemaphoreType.DMA((1,))],
    )(v)


def _sum_chips(own4, parts, tag):
    _, rows, cols = own4.shape
    tr = _tile(rows, (256, 128, 64, 32, 16))

    def body(me_ref, own_ref, p0_ref, p1_ref, p2_ref, o_ref):
        acc = own_ref[...].astype(F32)
        for r in (p0_ref, p1_ref, p2_ref):
            acc = acc + r[...].astype(F32)
        o_ref[...] = acc

    part = lambda j: pl.BlockSpec((None, tr, cols), lambda i, me_ref: (j, i, 0))
    chip = (2 * lax.axis_index("x") + lax.axis_index("y")).astype(jnp.int32).reshape(1)
    return pl.pallas_call(
        body, name=f"{tag}_reduce_sum_chips",
        grid_spec=pltpu.PrefetchScalarGridSpec(
            num_scalar_prefetch=1, grid=(rows // tr,),
            in_specs=[pl.BlockSpec((None, tr, cols), lambda i, me_ref: (me_ref[0], i, 0)), part(0), part(1), part(2)],
            out_specs=pl.BlockSpec((tr, cols), lambda i, me_ref: (i, 0))),
        out_shape=jax.ShapeDtypeStruct((rows, cols), F32), compiler_params=_params("parallel"),
    )(chip, own4, parts, parts, parts)


def _reduce_begin(g4, tag):
    return _add_kept_half(g4, _swap_other_half(g4, tag), tag)


def _reduce_end(chip_sum, parts, tag):
    half = _sum_chips(chip_sum, parts, tag)
    other = _swap_sibling(half, tag)
    return jnp.where(lax.axis_index("c") == 0, jnp.stack([half, other]), jnp.stack([other, half]))


def _reduce_grads(g4):
    chip_sum = _reduce_begin(g4, "all")
    return _reduce_end(chip_sum, _exchange_chips(chip_sum), "all")


BIG = (("ffn1_w_in", 2), ("ffn1_w_out", 1), ("xa_wq", 1), ("xa_wkv", 2), ("xa_wo", 1), ("ffn2_w_in", 2), ("ffn2_w_out", 1),
       ("ab_w_in", 2), ("ab_w_out", 1), ("dn_w_in", 2), ("dn_w_out", 1))
TINY_SHARDED = (("lru_conv_w", 2), ("dn_conv_w", 2))
REPLICATED = ("ffn1_norm", "mix_norm", "xa_norm", "xa_mem_norm", "ffn2_norm", "lru_conv_b", "lru_w_a", "lru_b_a", "lru_w_x",
              "lru_b_x", "lru_lambda", "dn_a_log", "dn_dt_bias", "dn_o_norm", "final_norm")
WEIGHTS = ("ffn1_norm", "ffn1_w_in", "ffn1_w_out", "mix_norm", "xa_norm", "xa_mem_norm", "xa_wq", "xa_wkv", "xa_wo", "ffn2_norm",
           "ffn2_w_in", "ffn2_w_out", "ab_w_in", "lru_conv_w", "lru_conv_b", "lru_w_a", "lru_b_a", "lru_w_x", "lru_b_x",
           "lru_lambda", "ab_w_out", "dn_w_in", "dn_conv_w", "dn_a_log", "dn_dt_bias", "dn_o_norm", "dn_w_out", "final_norm")


def _pad_rows(flat, row_multiple):
    n = flat.shape[-1]
    per = row_multiple * PACK_COLS
    total = -(-n // per) * per
    flat = jnp.pad(flat, [(0, 0)] * (flat.ndim - 1) + [(0, total - n)])
    return flat.reshape(flat.shape[:-1] + (total // PACK_COLS, PACK_COLS))


def _lane_padded(shape):
    return shape[:-1] + (-(-shape[-1] // LANES) * LANES,)


def _pad_lanes(t):
    return jnp.pad(t, [(0, 0)] * (t.ndim - 1) + [(0, _lane_padded(t.shape)[-1] - t.shape[-1])])


FIRST_USED = ("ffn1_w_in", "ffn1_w_out")


def _gather_blocks(shards):
    first, rest = [], []
    for n, a in BIG + TINY_SHARDED:
        block = _pad_lanes(shards[n]).astype(MXU_DTYPE) if (n, a) in BIG else shards[n]
        if n in FIRST_USED:
            first.append((n, 0, a, block[0:1]))
            rest += [(n, layer, a, block[layer:layer + 1]) for layer in range(1, block.shape[0])]
        else:
            rest.append((n, None, a, block))
    return first, rest


def _gathered(plan, arrays, shards, into):
    for (n, layer, axis, _), full in zip(plan, arrays):
        width, padded = shards[n].shape[-1], _lane_padded(shards[n].shape)[-1]
        if padded != width:
            assert axis == 2
            full = jnp.concatenate([full[..., k * padded:k * padded + width] for k in range(N_CHIPS)], axis=-1)
        if layer is None:
            into[n] = full
        else:
            into.setdefault(n, [None] * shards[n].shape[0])[layer] = full
    return into


def _to_blocks(pieces, axis):
    width = pieces[0].shape[axis - 1] // N_CHIPS
    block = lambda p, k: _pad_lanes(lax.slice_in_dim(p, k * width, (k + 1) * width, axis=axis - 1)).reshape(-1)
    return jnp.stack([jnp.concatenate([block(p, k) for p in pieces]) for k in range(N_CHIPS)], axis=0)


def _pack_grads(layer_grads):
    flat = jnp.concatenate([_to_blocks([layer_grads[n]], axis) for n, axis in BIG if n in layer_grads], axis=1)
    g = _pad_rows(flat, 2 * 256)
    return g.reshape(N_CHIPS, 2, g.shape[1] // 2, PACK_COLS)


def _unpack_grads(reduced, layer_grads, shards):
    flat = reduced.reshape(-1)
    out, off = {}, 0
    for n, _ in BIG:
        if n in layer_grads:
            shape = (1,) + shards[n].shape[1:]
            size = math.prod(_lane_padded(shape))
            out[n] = flat[off:off + size].reshape(_lane_padded(shape))[..., :shape[-1]]
            off += size
    return out


def _pack_small(grads, loss):
    parts = [p.reshape(-1) for n in REPLICATED + tuple(n for n, _ in TINY_SHARDED) for p in grads[n]] + [loss[0, :1]]
    flat = jnp.concatenate(parts)
    total = -(-flat.shape[0] // (SUBLANES * LANES)) * SUBLANES * LANES
    return jnp.pad(flat, (0, total - flat.shape[0])).reshape(-1, LANES)


def _unpack_small(summed, shards, chip):
    flat = summed.reshape(-1)
    out, off = {}, 0
    for n in REPLICATED:
        out[n] = flat[off:off + shards[n].size].reshape(shards[n].shape)
        off += shards[n].size
    for n, axis in TINY_SHARDED:
        width = shards[n].shape[axis]
        shape = shards[n].shape[:axis] + (N_CHIPS * width,) + shards[n].shape[axis + 1:]
        full = flat[off:off + N_CHIPS * shards[n].size].reshape(shape)
        out[n] = lax.dynamic_slice_in_dim(full, chip * width, width, axis=axis)
        off += N_CHIPS * shards[n].size
    return out, flat[off]


def kernel(x, mem, ffn1_norm, ffn1_w_in, ffn1_w_out, mix_norm, xa_norm, xa_mem_norm, xa_wq, xa_wkv, xa_wo, ffn2_norm,
           ffn2_w_in, ffn2_w_out, ab_w_in, lru_conv_w, lru_conv_b, lru_w_a, lru_b_a, lru_w_x, lru_b_x, lru_lambda,
           ab_w_out, dn_w_in, dn_conv_w, dn_a_log, dn_dt_bias, dn_o_norm, dn_w_out, final_norm, loss_target,
           m_ffn1_norm, m_ffn1_w_in, m_ffn1_w_out, m_mix_norm, m_xa_norm, m_xa_mem_norm, m_xa_wq, m_xa_wkv, m_xa_wo,
           m_ffn2_norm, m_ffn2_w_in, m_ffn2_w_out, m_ab_w_in, m_lru_conv_w, m_lru_conv_b, m_lru_w_a, m_lru_b_a,
           m_lru_w_x, m_lru_b_x, m_lru_lambda, m_ab_w_out, m_dn_w_in, m_dn_conv_w, m_dn_a_log, m_dn_dt_bias,
           m_dn_o_norm, m_dn_w_out, m_final_norm, v_ffn1_norm, v_ffn1_w_in, v_ffn1_w_out, v_mix_norm, v_xa_norm,
           v_xa_mem_norm, v_xa_wq, v_xa_wkv, v_xa_wo, v_ffn2_norm, v_ffn2_w_in, v_ffn2_w_out, v_ab_w_in,
           v_lru_conv_w, v_lru_conv_b, v_lru_w_a, v_lru_b_a, v_lru_w_x, v_lru_b_x, v_lru_lambda, v_ab_w_out,
           v_dn_w_in, v_dn_conv_w, v_dn_a_log, v_dn_dt_bias, v_dn_o_norm, v_dn_w_out, v_final_norm):
    given = dict(locals())
    shards = {n: given[n] for n in WEIGHTS}
    chip = 2 * lax.axis_index("x") + lax.axis_index("y")

    full = {n: shards[n] for n in REPLICATED}
    first, rest = _gather_blocks(shards)
    _gathered(first, _gather_weights([b for *_, b in first], [a for _, _, a, _ in first]), shards, full)
    loss, grad_x, grads, reduced = _local_step(x[0], mem[0], loss_target[0], full, pending=(rest, shards), reduce_big=True)

    grad, loss_sum = _unpack_small(_allreduce_small(_pack_small(grads, loss)), shards, chip)
    per_layer = [_unpack_grads(r, layer_grads, shards) for r, layer_grads in reduced]
    for n, _ in BIG:
        grad[n] = jnp.concatenate([p[n] for p in per_layer if n in p], axis=0)

    delta, new_m, new_v = {}, {}, {}
    for n in WEIGHTS:
        delta[n], new_m[n], new_v[n] = _adamw(shards[n], grad[n], given["m_" + n], given["v_" + n], name=f"adamw_{n}")
    return (loss_sum, grad_x[None], *[grad[n] for n in WEIGHTS], *[delta[n] for n in WEIGHTS],
            *[new_m[n] for n in WEIGHTS], *[new_v[n] for n in WEIGHTS])
```

```python
import math

import jax
import jax.numpy as jnp
from jax import lax
from jax.experimental import pallas as pl
from jax.experimental.pallas import tpu as pltpu

F32 = jnp.float32
MXU_DTYPE = jnp.bfloat16
VMEM_LIMIT_BYTES = 48 * 1024 * 1024
MM_BLOCK_BYTES = 8 * 1024 * 1024
ADAMW_BLOCK_BYTES = 1024 * 1024
LANES = 128
SUBLANES = 8

NORM_EPS = 1e-6
CONV_K = 4
ATTN_PAIRS = 4
ATTN_HEAD_DIM = 64
ATTN_WIDTH = 512
ATTN_BLOCK = 128
DILATIONS = (1, 4, 16)
LRU_WIDTH = 512
LRU_BLOCKS = 8
LRU_C = 8.0
DN_HEADS = 8
DN_HEAD_DIM = 128
DN_WIDTH = 1024
DN_CHUNK = 64
XA_HEADS = 4
XA_HEAD_DIM = 256
D_FF = 2816
ADAM_LR, ADAM_B1, ADAM_B2, ADAM_EPS, ADAM_WD, ADAM_STEP = 0.001, 0.9, 0.999, 1e-08, 0.01, 10

MESH = pl.DeviceIdType.MESH


def _tile(n, prefs):
    for p in prefs:
        if n % p == 0:
            return p
    return n


def _params(*sem):
    return pltpu.CompilerParams(dimension_semantics=sem, vmem_limit_bytes=VMEM_LIMIT_BYTES)


def _dg(a, b, dims, hi=False):
    if hi:
        return lax.dot_general(a, b, (dims, ((), ())), precision=lax.Precision.HIGHEST, preferred_element_type=F32)
    return lax.dot_general(a.astype(MXU_DTYPE), b.astype(MXU_DTYPE), (dims, ((), ())), preferred_element_type=F32)


@jax.custom_vjp
def _bdot(a, b):
    return _dg(a, b, ((1,), (0,)))


def _bdot_fwd(a, b):
    return _bdot(a, b), (a, b)


def _bdot_bwd(r, g):
    a, b = r
    return _dg(g, b, ((1,), (1,))).astype(a.dtype), _dg(a, g, ((0,), (0,))).astype(b.dtype)


_bdot.defvjp(_bdot_fwd, _bdot_bwd)


def _log1p(t):
    return jnp.where(t < 0.01, t * (1.0 - t * (0.5 - t * (1.0 / 3.0))), jnp.log(1.0 + t))


def _neg_expm1(y):
    series = -y * (1.0 + 0.5 * y * (1.0 + (1.0 / 3.0) * y * (1.0 + 0.25 * y)))
    return jnp.where(y > -0.01, series, 1.0 - jnp.exp(y))


def _softplus(x):
    return jnp.maximum(x, 0.0) + _log1p(jnp.exp(-jnp.abs(x)))


def _sigmoid(x):
    return 0.5 * jnp.tanh(0.5 * x) + 0.5


def _silu(x):
    return x * _sigmoid(x)


def _gelu(x):
    return 0.5 * x * (1.0 + jnp.tanh(0.7978845608028654 * (x + 0.044715 * x * x * x)))


def _rows(shape):
    return lax.broadcasted_iota(jnp.int32, shape, 0)


def _cols(shape):
    return lax.broadcasted_iota(jnp.int32, shape, 1)


def _mm(a, b, *, mode="nn", out_dtype=F32, res=None, scale=1.0, name):
    if mode == "nn":
        (m, k), (k2, n) = a.shape, b.shape
    elif mode == "nt":
        (m, k), (n, k2) = a.shape, b.shape
    else:
        (k, m), (k2, n) = a.shape, b.shape
    assert k == k2, (a.shape, b.shape, mode)
    if mode == "tn":
        tm, tn, tk = _tile(m, (1024, 512, 256, 128)), _tile(n, (1024, 512, 256, 128)), _tile(k, (2048, 1024, 512, 256))
    else:
        tm, tn = _tile(m, (512, 256, 128)), _tile(n, (1024, 512, 256, 128))
        tk = k if k * tn * 2 <= MM_BLOCK_BYTES else _tile(k, (1024, 512, 256, 128))
    nk = k // tk
    dims = {"nn": ((1,), (0,)), "nt": ((1,), (1,)), "tn": ((0,), (0,))}[mode]

    def body(*refs):
        a_ref, b_ref = refs[:2]
        r_ref = refs[2] if res is not None else None
        o_ref = refs[3 if res is not None else 2]

        def finish(r):
            if scale != 1.0:
                r = r * scale
            if res is not None:
                r = r_ref[...] + r
            o_ref[...] = r.astype(out_dtype)

        if nk == 1:
            finish(_dg(a_ref[...], b_ref[...], dims))
            return
        acc = refs[-1]
        kk = pl.program_id(2)

        @pl.when(kk == 0)
        def _():
            acc[...] = jnp.zeros_like(acc)

        acc[...] += _dg(a_ref[...], b_ref[...], dims)

        @pl.when(kk == nk - 1)
        def _():
            finish(acc[...])

    a_spec = pl.BlockSpec((tk, tm), lambda i, j, kk: (kk, i)) if mode == "tn" else pl.BlockSpec((tm, tk), lambda i, j, kk: (i, kk))
    b_spec = pl.BlockSpec((tn, tk), lambda i, j, kk: (j, kk)) if mode == "nt" else pl.BlockSpec((tk, tn), lambda i, j, kk: (kk, j))
    o_spec = pl.BlockSpec((tm, tn), lambda i, j, kk: (i, j))
    in_specs = [a_spec, b_spec] + ([o_spec] if res is not None else [])
    args = (a, b) + ((res,) if res is not None else ())
    return pl.pallas_call(
        body, name=name, grid=(m // tm, n // tn, nk), in_specs=in_specs, out_specs=o_spec,
        out_shape=jax.ShapeDtypeStruct((m, n), out_dtype), scratch_shapes=[pltpu.VMEM((tm, tn), F32)] if nk > 1 else [],
        compiler_params=_params("parallel", "parallel", "arbitrary"),
    )(*args)


def _rms_fwd(x, g, *, name):
    s, d = x.shape
    tm = _tile(s, (512, 256))

    def body(x_ref, g_ref, o_ref):
        xv = x_ref[...]
        r = lax.rsqrt(jnp.mean(xv * xv, axis=-1, keepdims=True) + NORM_EPS)
        o_ref[...] = (xv * r * g_ref[...]).astype(o_ref.dtype)

    return pl.pallas_call(
        body, name=name, grid=(s // tm,),
        in_specs=[pl.BlockSpec((tm, d), lambda i: (i, 0)), pl.BlockSpec((1, d), lambda i: (0, 0))],
        out_specs=pl.BlockSpec((tm, d), lambda i: (i, 0)), out_shape=jax.ShapeDtypeStruct((s, d), MXU_DTYPE),
        compiler_params=_params("parallel"),
    )(x, g.reshape(1, d))


def _norm_mm(x, g, ws, out_dtypes, *, name):
    s, d = x.shape
    tm = _tile(s, (512, 256))
    nw = len(ws)

    def body(*refs):
        x_ref, g_ref = refs[:2]
        h_ref = refs[2 + nw]
        xv = x_ref[...]
        r = lax.rsqrt(jnp.mean(xv * xv, axis=-1, keepdims=True) + NORM_EPS)
        h = (xv * r * g_ref[...]).astype(MXU_DTYPE)
        h_ref[...] = h
        for w_ref, o_ref in zip(refs[2:2 + nw], refs[3 + nw:]):
            o_ref[...] = _dg(h, w_ref[...], ((1,), (0,))).astype(o_ref.dtype)

    row = lambda w: pl.BlockSpec((tm, w), lambda i: (i, 0))
    outs = pl.pallas_call(
        body, name=name, grid=(s // tm,),
        in_specs=[row(d), pl.BlockSpec((1, d), lambda i: (0, 0))]
        + [pl.BlockSpec(w.shape, lambda i: (0, 0), pipeline_mode=RESIDENT) for w in ws],
        out_specs=[row(d)] + [row(w.shape[1]) for w in ws],
        out_shape=[jax.ShapeDtypeStruct((s, d), MXU_DTYPE)] + [jax.ShapeDtypeStruct((s, w.shape[1]), t) for w, t in zip(ws, out_dtypes)],
        compiler_params=_params("parallel"),
    )(x, g.reshape(1, d), *ws)
    return outs[0], outs[1:]


def _mm_rms_bwd(pairs, x, g, dres, *, name):
    s, d = x.shape
    tm = _tile(s, (512, 256))
    n = len(pairs)

    def body(*refs):
        x_ref, g_ref, dr_ref = refs[2 * n:2 * n + 3]
        dx_ref, dg_ref = refs[2 * n + 3:]
        dh = _dg(refs[0][...], refs[n][...], ((1,), (1,)))
        for a_ref, w_ref in zip(refs[1:n], refs[n + 1:2 * n]):
            dh = dh + _dg(a_ref[...], w_ref[...], ((1,), (1,)))
        xv, gv = x_ref[...], g_ref[...]
        r = lax.rsqrt(jnp.mean(xv * xv, axis=-1, keepdims=True) + NORM_EPS)
        xh = xv * r
        dxh = dh * gv
        dx_ref[...] = dr_ref[...] + r * (dxh - xh * jnp.mean(dxh * xh, axis=-1, keepdims=True))

        @pl.when(pl.program_id(0) == 0)
        def _():
            dg_ref[...] = jnp.zeros_like(dg_ref)

        dg_ref[...] += jnp.sum(dh * xh, axis=0, keepdims=True)

    row = lambda w: pl.BlockSpec((tm, w), lambda i: (i, 0))
    vec = pl.BlockSpec((1, d), lambda i: (0, 0))
    return pl.pallas_call(
        body, name=name, grid=(s // tm,),
        in_specs=[row(a.shape[1]) for a, _ in pairs]
        + [pl.BlockSpec(w.shape, lambda i: (0, 0), pipeline_mode=RESIDENT) for _, w in pairs] + [row(d), vec, row(d)],
        out_specs=[row(d), vec], out_shape=[jax.ShapeDtypeStruct((s, d), F32), jax.ShapeDtypeStruct((1, d), F32)],
        compiler_params=_params("arbitrary"),
    )(*[a for a, _ in pairs], *[w for _, w in pairs], x, g.reshape(1, d), dres)


def _rms_bwd(x, g, dh, dres, *, name):
    s, d = x.shape
    tm = _tile(s, (512, 256))

    def body(x_ref, g_ref, dh_ref, dr_ref, dx_ref, dg_ref):
        xv = x_ref[...]
        r = lax.rsqrt(jnp.mean(xv * xv, axis=-1, keepdims=True) + NORM_EPS)
        xh = xv * r
        dhv = dh_ref[...].astype(F32)
        dxh = dhv * g_ref[...]
        dx = r * (dxh - xh * jnp.mean(dxh * xh, axis=-1, keepdims=True))
        dx_ref[...] = dr_ref[...] + dx

        @pl.when(pl.program_id(0) == 0)
        def _():
            dg_ref[...] = jnp.zeros_like(dg_ref)

        dg_ref[...] += jnp.sum(dhv * xh, axis=0, keepdims=True)

    row = pl.BlockSpec((tm, d), lambda i: (i, 0))
    vec = pl.BlockSpec((1, d), lambda i: (0, 0))
    return pl.pallas_call(
        body, name=name, grid=(s // tm,), in_specs=[row, vec, row, row], out_specs=[row, vec],
        out_shape=[jax.ShapeDtypeStruct((s, d), F32), jax.ShapeDtypeStruct((1, d), F32)],
        compiler_params=_params("arbitrary"),
    )(x, g.reshape(1, d), dh, dres)


FFN_CHUNK = 256
FFN_TM = 256
RESIDENT = pl.Buffered(1)


def _ffn_fwd_call(x, g, w_in, w_out, layer, *, name, gather=None):
    s, d = x.shape
    f = w_out.shape[1]
    tm = _tile(s, (2 * FFN_TM, FFN_TM))
    steps = s // tm
    n_g = len(gather[0]) if gather else 0
    g_shapes, g_sems, g_start, g_finish = _gather_plan(*gather) if gather else ([], [], None, None)

    def body(*refs):
        x_ref, g_ref, wi_ref, wo_ref = refs[:4]
        y_ref, u_ref = refs[4 + n_g:6 + n_g]
        act_ref = refs[6 + 2 * n_g]
        if gather:
            comm = (refs[4:4 + n_g], refs[6 + n_g:6 + 2 * n_g], *refs[7 + 2 * n_g:])
            pl.when(pl.program_id(0) == 0)(lambda: g_start(*comm))
        xv = x_ref[...]
        r = lax.rsqrt(jnp.mean(xv * xv, axis=-1, keepdims=True) + NORM_EPS)
        h = (xv * r * g_ref[...]).astype(MXU_DTYPE)
        for j in range(f // FFN_CHUNK):
            lo, hi = j * FFN_CHUNK, (j + 1) * FFN_CHUNK
            gate = _dg(h, wi_ref[:, lo:hi], ((1,), (0,))).astype(MXU_DTYPE)
            up = _dg(h, wi_ref[:, f + lo:f + hi], ((1,), (0,))).astype(MXU_DTYPE)
            u_ref[:, lo:hi] = gate
            u_ref[:, f + lo:f + hi] = up
            act_ref[:, lo:hi] = (_silu(gate.astype(F32)) * up.astype(F32)).astype(MXU_DTYPE)
        y_ref[...] = xv + 0.5 * _dg(act_ref[...], wo_ref[...], ((1,), (0,)))
        if gather:
            pl.when(pl.program_id(0) == steps - 1)(lambda: g_finish(*comm))

    row = lambda w: pl.BlockSpec((tm, w), lambda i: (i, 0))
    return pl.pallas_call(
        body, name=name, grid=(steps,),
        in_specs=[row(d), pl.BlockSpec((1, d), lambda i: (0, 0)),
                  pl.BlockSpec((None,) + w_in.shape[1:], lambda i: (layer, 0, 0), pipeline_mode=RESIDENT),
                  pl.BlockSpec((None,) + w_out.shape[1:], lambda i: (layer, 0, 0), pipeline_mode=RESIDENT)] + [HBM_SPEC] * n_g,
        out_specs=[row(d), row(2 * f)] + [HBM_SPEC] * n_g,
        out_shape=[jax.ShapeDtypeStruct((s, d), F32), jax.ShapeDtypeStruct((s, 2 * f), MXU_DTYPE)] + g_shapes,
        scratch_shapes=[pltpu.VMEM((tm, f), MXU_DTYPE)] + g_sems,
        compiler_params=_params("arbitrary" if gather else "parallel"),
    )(x, g.reshape(1, d), w_in, w_out, *(gather[0] if gather else ()))


def _ffn_bwd_call(x, g, u, dy, w_in, w_out, layer, *, name, exchange=None):
    s, d = x.shape
    f = w_out.shape[1]
    tm = _tile(s, (FFN_TM,))
    steps = s // tm
    hosted = exchange is not None
    e_shape, e_sems, e_start, e_finish = _exchange_plan(exchange) if hosted else (None, [], None, None)

    def body(*refs):
        x_ref, g_ref, u_ref, dy_ref, wi_ref, wo_ref = refs[:6]
        du_ref, dx_ref, dg_ref, h_ref = refs[6 + hosted:10 + hosted]
        if hosted:
            comm = (refs[6], *refs[10 + hosted:])
            pl.when(pl.program_id(0) == 0)(lambda: e_start(*comm))
        dyv = dy_ref[...]
        dyh = (0.5 * dyv).astype(MXU_DTYPE)
        for j in range(f // FFN_CHUNK):
            lo, hi = j * FFN_CHUNK, (j + 1) * FFN_CHUNK
            dact = _dg(dyh, wo_ref[lo:hi, :], ((1,), (1,)))
            gate, up = u_ref[:, lo:hi].astype(F32), u_ref[:, f + lo:f + hi].astype(F32)
            sg = _sigmoid(gate)
            du_ref[:, lo:hi] = (dact * up * sg * (1.0 + gate * (1.0 - sg))).astype(MXU_DTYPE)
            du_ref[:, f + lo:f + hi] = (dact * gate * sg).astype(MXU_DTYPE)
        dh = _dg(du_ref[...], wi_ref[...], ((1,), (1,)))
        xv, gv = x_ref[...], g_ref[...]
        r = lax.rsqrt(jnp.mean(xv * xv, axis=-1, keepdims=True) + NORM_EPS)
        xh = xv * r
        h_ref[...] = (xh * gv).astype(MXU_DTYPE)
        dxh = dh * gv
        dx_ref[...] = dyv + r * (dxh - xh * jnp.mean(dxh * xh, axis=-1, keepdims=True))

        @pl.when(pl.program_id(0) == 0)
        def _():
            dg_ref[...] = jnp.zeros_like(dg_ref)

        dg_ref[...] += jnp.sum(dh * xh, axis=0, keepdims=True)
        if hosted:
            pl.when(pl.program_id(0) == steps - 1)(lambda: e_finish(*comm))

    row = lambda w: pl.BlockSpec((tm, w), lambda i: (i, 0))
    vec = pl.BlockSpec((1, d), lambda i: (0, 0))
    return pl.pallas_call(
        body, name=name, grid=(steps,),
        in_specs=[row(d), vec, row(2 * f), row(d),
                  pl.BlockSpec((None,) + w_in.shape[1:], lambda i: (layer, 0, 0), pipeline_mode=RESIDENT),
                  pl.BlockSpec((None,) + w_out.shape[1:], lambda i: (layer, 0, 0), pipeline_mode=RESIDENT)] + [HBM_SPEC] * hosted,
        out_specs=[row(2 * f), row(d), vec, row(d)] + [HBM_SPEC] * hosted,
        out_shape=[jax.ShapeDtypeStruct((s, 2 * f), MXU_DTYPE), jax.ShapeDtypeStruct((s, d), F32),
                   jax.ShapeDtypeStruct((1, d), F32), jax.ShapeDtypeStruct((s, d), MXU_DTYPE)] + [e_shape] * hosted,
        scratch_shapes=e_sems, compiler_params=_params("arbitrary"),
    )(x, g.reshape(1, d), u, dy, w_in, w_out, *([exchange] if hosted else []))


def _ffn_dw_out(u, dy, *, name):
    s, f2 = u.shape
    f, d = f2 // 2, dy.shape[1]
    tf, tk = _tile(f, (1408, 256, 128)), _tile(s, (1024, 512, 256))
    nj = f // tf

    def body(g_ref, u_ref, dy_ref, o_ref):
        @pl.when(pl.program_id(1) == 0)
        def _():
            o_ref[...] = jnp.zeros_like(o_ref)

        act = _silu(g_ref[...].astype(F32)) * u_ref[...].astype(F32)
        o_ref[...] += _dg(act, 0.5 * dy_ref[...], ((0,), (0,)))

    return pl.pallas_call(
        body, name=name, grid=(nj, s // tk),
        in_specs=[pl.BlockSpec((tk, tf), lambda j, k: (k, j)), pl.BlockSpec((tk, tf), lambda j, k: (k, j + nj)),
                  pl.BlockSpec((tk, d), lambda j, k: (k, 0))],
        out_specs=pl.BlockSpec((tf, d), lambda j, k: (j, 0)), out_shape=jax.ShapeDtypeStruct((f, d), F32),
        compiler_params=_params("parallel", "arbitrary"),
    )(u, u, dy)


def _of_layer(w, layer):
    return (w[layer], 0) if isinstance(w, (list, tuple)) else (w, layer)


def _layer_matrix(w, layer):
    w, at = _of_layer(w, layer)
    return w[at]


def _ffn_fwd(x, g, w_in, w_out, layer, tag, gather=None):
    (w_in, at), (w_out, _) = _of_layer(w_in, layer), _of_layer(w_out, layer)
    y, u, *gathered = _ffn_fwd_call(x, g[layer], w_in, w_out, at, name=f"{tag}_fwd", gather=gather)
    return y, (x, u), gathered


def _ffn_bwd(saved, g, w_in, w_out, layer, dy, tag, exchange=None):
    x, u = saved
    (w_in, at), (w_out, _) = _of_layer(w_in, layer), _of_layer(w_out, layer)
    du, dx, dg, h, *parts = _ffn_bwd_call(x, g[layer], u, dy, w_in, w_out, at, name=f"{tag}_bwd", exchange=exchange)
    dw_out = _ffn_dw_out(u, dy, name=f"{tag}_dwout")
    dw_in = _mm(h, du, mode="tn", name=f"{tag}_dwin")
    return dx, dg[0], dw_in, dw_out, (parts[0] if parts else None)


ATTN_SCALE = ATTN_HEAD_DIM ** -0.5
NEG_BIG = -1e30
PROJ_AB_BLOCKS = 5


def _first_head(n):
    return _cols((n, LANES)) < ATTN_HEAD_DIM


def _per_head(tiles):
    first = _first_head(tiles[0].shape[0])
    return jnp.stack([jnp.where(first == (h == 0), t, 0.0) for t in tiles for h in (0, 1)], axis=0)


def _both(tiles):
    return jnp.stack([t for t in tiles for _ in (0, 1)], axis=0)


def _head_cols(tiles):
    return jnp.stack([t[:, c0:c0 + 1] for t in tiles for c0 in (0, ATTN_HEAD_DIM)], axis=0)


def _join_heads(v):
    return [v[2 * u] + v[2 * u + 1] for u in range(v.shape[0] // 2)]


def _spread_heads(v):
    first = _first_head(v.shape[1])
    return [jnp.where(first, v[2 * u], v[2 * u + 1]) for u in range(v.shape[0] // 2)]


def _band_masks(has_prev):
    qi, kj = _rows((ATTN_BLOCK, ATTN_BLOCK)), _cols((ATTN_BLOCK, ATTN_BLOCK))
    return (kj >= qi) & has_prev, kj <= qi


def _dattn_delta(o, dcat, *, name):
    s_len = o.shape[0]
    tm = _tile(s_len, (512, 256))

    def body(o_ref, do_ref, out_ref):
        r, c = _rows((ATTN_WIDTH, ATTN_WIDTH)), _cols((ATTN_WIDTH, ATTN_WIDTH))
        ones_bd = (r // ATTN_HEAD_DIM == c // ATTN_HEAD_DIM).astype(F32)
        out_ref[...] = _dg(o_ref[...] * do_ref[...], ones_bd, ((1,), (0,)), hi=True)

    blk = pl.BlockSpec((tm, ATTN_WIDTH), lambda i: (i, 0))
    return pl.pallas_call(
        body, name=name, grid=(s_len // tm,), in_specs=[blk, blk], out_specs=blk,
        out_shape=jax.ShapeDtypeStruct((s_len, ATTN_WIDTH), F32), compiler_params=_params("parallel"),
    )(o, dcat)


ATTN_UNITS = 4


def _units(it, d):
    if d == 1:
        return [(pl.ds(0, ATTN_BLOCK), pl.ds(p * LANES, LANES)) for p in range(ATTN_UNITS)]
    return [(pl.ds(it * ATTN_UNITS + u, ATTN_BLOCK, stride=d), pl.ds(0, LANES)) for u in range(ATTN_UNITS)]


def _tiles(ref, units):
    return [ref[rows, lanes] for rows, lanes in units]


def _store_tiles(ref, units, tiles):
    for (rows, lanes), t in zip(units, tiles):
        ref[rows, lanes] = t


def _stacked(a_tiles, b_tiles):
    return [jnp.concatenate([a, b], axis=0) for a, b in zip(a_tiles, b_tiles)]


def _passes(d):
    return max(d // ATTN_UNITS, 1)


def _pairs_per_step(d):
    return ATTN_PAIRS if d == 1 else 1


def _pair_specs(d, n_of):
    pairs = _pairs_per_step(d)
    groups = ATTN_PAIRS // pairs
    return lambda c: pl.BlockSpec((ATTN_BLOCK * d, LANES * pairs), lambda n, p: (n_of(n), c * groups + p))


def _sattn_fwd(proj, state, d, *, last, name):
    s_len = proj.shape[0]
    nb = s_len // (ATTN_BLOCK * d)
    first = state is None
    n_out = 2 if last else 3

    def body(*refs):
        q_ref, kp_ref, kc_ref, vp_ref, vc_ref = refs[:5]
        st_refs = () if first else refs[5:8]
        out_refs = refs[-n_out:]
        ok = jnp.concatenate(_band_masks(pl.program_id(0) > 0), axis=1)

        def one_pass(it, carry):
            units = _units(it, d)
            kcat = _stacked(_tiles(kp_ref, units), _tiles(kc_ref, units))
            vcat = _stacked(_tiles(vp_ref, units), _tiles(vc_ref, units))
            s = jnp.where(ok, _bdg(_per_head(_tiles(q_ref, units)), _both(kcat), 2, 2) * ATTN_SCALE, NEG_BIG)
            m_new = jnp.max(s, axis=2, keepdims=True)
            if not first:
                m_old = _head_cols(_tiles(st_refs[0], units))
                m_new = jnp.maximum(m_old, m_new)
                alpha = jnp.exp(m_old - m_new)
            p = jnp.exp(s - m_new)
            l_new = jnp.sum(p, axis=2, keepdims=True)
            acc = _join_heads(_bdg(p, _per_head(vcat), 2, 1))
            if not first:
                l_new = l_new + _head_cols(_tiles(st_refs[1], units)) * alpha
                acc = [a + a_in * sp for a, a_in, sp in zip(acc, _tiles(st_refs[2], units), _spread_heads(alpha))]
            m_pair, l_pair = _spread_heads(m_new), _spread_heads(l_new)
            if last:
                _store_tiles(out_refs[0], units, [a / l for a, l in zip(acc, l_pair)])
                _store_tiles(out_refs[1], units, [m + jnp.log(l) for m, l in zip(m_pair, l_pair)])
            else:
                _store_tiles(out_refs[0], units, m_pair)
                _store_tiles(out_refs[1], units, l_pair)
                _store_tiles(out_refs[2], units, acc)
            return carry

        lax.fori_loop(0, _passes(d), one_pass, 0)

    cur, prev = _pair_specs(d, lambda n: n), _pair_specs(d, lambda n: jnp.maximum(n - 1, 0))
    st = cur(0)
    return tuple(pl.pallas_call(
        body, name=name, grid=(nb, ATTN_PAIRS // _pairs_per_step(d)),
        in_specs=[cur(0), prev(1), cur(1), prev(2), cur(2)] + ([] if first else [st] * 3),
        out_specs=[st] * n_out, out_shape=[jax.ShapeDtypeStruct((s_len, ATTN_WIDTH), F32)] * n_out,
        compiler_params=_params("arbitrary", "parallel"),
    )(*([proj] * 5 + ([] if first else list(state)))))


def _dattn_forward(proj, tag):
    state = None
    for i, d in enumerate(DILATIONS):
        state = _sattn_fwd(proj, state, d, last=i == len(DILATIONS) - 1, name=f"{tag}_attn_d{d}")
    return state


def _sattn_bwd(proj, dcat, lse, delta, grads_in, d, *, name):
    s_len = proj.shape[0]
    nb = s_len // (ATTN_BLOCK * d)
    first = grads_in is None
    groups = ATTN_PAIRS // _pairs_per_step(d)

    def body(*refs):
        q_ref, kp_ref, kc_ref, vp_ref, vc_ref, do_ref, lse_ref, dl_ref = refs[:8]
        dq_in, dk_in, dv_in = (None, None, None) if first else refs[8:11]
        dq_ref, dk_ref, dv_ref, carry_k, carry_v = refs[-5:]
        n = pl.program_id(1)
        ok = jnp.concatenate(_band_masks(n > 0), axis=1)

        @pl.when(n == 0)
        def _():
            carry_k[...] = jnp.zeros_like(carry_k)
            carry_v[...] = jnp.zeros_like(carry_v)

        def leave(ref, carry, units, extra, into):
            out = [c + e for c, e in zip(_tiles(carry, units), extra)] if extra else _tiles(carry, units)
            if into is not None:
                out = [a + b for a, b in zip(out, _tiles(into, units))]
            _store_tiles(ref, units, out)

        def one_pass(it, carry):
            units = _units(it, d)
            kcat = _stacked(_tiles(kp_ref, units), _tiles(kc_ref, units))
            vcat = _stacked(_tiles(vp_ref, units), _tiles(vc_ref, units))
            q2, do2 = _per_head(_tiles(q_ref, units)), _per_head(_tiles(do_ref, units))
            s = _bdg(q2, _both(kcat), 2, 2) * ATTN_SCALE
            pr = jnp.where(ok, jnp.exp(jnp.where(ok, s, NEG_BIG) - _head_cols(_tiles(lse_ref, units))), 0.0)
            ds = pr * (_bdg(do2, _both(vcat), 2, 2) - _head_cols(_tiles(dl_ref, units)))
            dq = [t * ATTN_SCALE for t in _join_heads(_bdg(ds, _per_head(kcat), 2, 1))]
            if not first:
                dq = [a + b for a, b in zip(dq, _tiles(dq_in, units))]
            _store_tiles(dq_ref, units, dq)
            dk = [t * ATTN_SCALE for t in _join_heads(_bdg(ds, q2, 1, 1))]
            dv = _join_heads(_bdg(pr, do2, 1, 1))
            leave(dk_ref, carry_k, units, [t[:ATTN_BLOCK] for t in dk], dk_in)
            leave(dv_ref, carry_v, units, [t[:ATTN_BLOCK] for t in dv], dv_in)
            _store_tiles(carry_k, units, [t[ATTN_BLOCK:] for t in dk])
            _store_tiles(carry_v, units, [t[ATTN_BLOCK:] for t in dv])
            return carry

        def last_pass(it, carry):
            units = _units(it, d)
            leave(dk_ref, carry_k, units, None, dk_in)
            leave(dv_ref, carry_v, units, None, dv_in)
            return carry

        @pl.when(n < nb)
        def _():
            lax.fori_loop(0, _passes(d), one_pass, 0)

        @pl.when(n == nb)
        def _():
            lax.fori_loop(0, _passes(d), last_pass, 0)

    pairs = _pairs_per_step(d)
    blk = (ATTN_BLOCK * d, LANES * pairs)
    at = lambda n_of: (lambda c: pl.BlockSpec(blk, lambda p, n: (n_of(n), c * groups + p)))
    here = lambda n: jnp.minimum(n, nb - 1)
    cur, prev, lag = at(here), at(lambda n: jnp.maximum(here(n) - 1, 0)), at(lambda n: jnp.maximum(n - 1, 0))
    st, st_lag = cur(0), lag(0)
    return tuple(pl.pallas_call(
        body, name=name, grid=(groups, nb + 1),
        in_specs=[cur(0), prev(1), cur(1), prev(2), cur(2), st, st, st] + ([] if first else [st, st_lag, st_lag]),
        out_specs=[st, st_lag, st_lag], out_shape=[jax.ShapeDtypeStruct((s_len, ATTN_WIDTH), F32)] * 3,
        scratch_shapes=[pltpu.VMEM(blk, F32)] * 2, compiler_params=_params("parallel", "arbitrary"),
    )(*([proj] * 5 + [dcat, lse, delta] + ([] if first else list(grads_in)))))


def _dattn_backward(proj, o, lse, dcat, tag):
    delta = _dattn_delta(o, dcat, name=f"{tag}_attn_delta")
    grads = None
    for d in DILATIONS:
        grads = _sattn_bwd(proj, dcat, lse, delta, grads, d, name=f"{tag}_attn_bwd_d{d}")
    return grads


CONV_TC = 512
CONV_T = 512


def _conv_tiles(s_len, cb0, width):
    wide = 2 * CONV_TC
    tc = wide if width % wide == 0 and (cb0 * CONV_TC) % wide == 0 else CONV_TC
    return _tile(s_len, (CONV_T, CONV_T // 2)), tc, cb0 * CONV_TC // tc


def _shift_down(ext, k, t):
    return (pltpu.roll(ext, k, 0) if k else ext)[SUBLANES:SUBLANES + t]


def _conv_fwd(src, cb0, width, w8, *, name):
    s_len = src.shape[0]
    t, tc, cb = _conv_tiles(s_len, cb0, width)
    tpb = t // SUBLANES

    def body(x_ref, h_ref, w_ref, y_ref):
        halo = jnp.where(pl.program_id(0) > 0, h_ref[...], 0.0)
        ext = jnp.concatenate([halo, x_ref[...]], axis=0)
        w = w_ref[...]
        y = jnp.broadcast_to(w[CONV_K:CONV_K + 1], (t, tc))
        for k in range(CONV_K):
            y = y + w[k:k + 1] * _shift_down(ext, CONV_K - 1 - k, t)
        y_ref[...] = y

    return pl.pallas_call(
        body, name=name, grid=(s_len // t, width // tc),
        in_specs=[pl.BlockSpec((t, tc), lambda i, j: (i, cb + j)),
                  pl.BlockSpec((SUBLANES, tc), lambda i, j: (jnp.maximum(i * tpb - 1, 0), cb + j)),
                  pl.BlockSpec((SUBLANES, tc), lambda i, j: (0, j))],
        out_specs=pl.BlockSpec((t, tc), lambda i, j: (i, j)), out_shape=jax.ShapeDtypeStruct((s_len, width), F32),
        compiler_params=_params("parallel", "parallel"),
    )(src, src, w8)


def _conv_bwd(src, cb0, width, w8, dy, *, name):
    s_len = src.shape[0]
    t, tc, cb = _conv_tiles(s_len, cb0, width)
    tpb = t // SUBLANES
    ni = s_len // t

    def body(x_ref, h_ref, w_ref, dy_ref, dn_ref, dx_ref, dw_ref):
        i = pl.program_id(1)
        halo = jnp.where(i > 0, h_ref[...], 0.0)
        ext = jnp.concatenate([halo, x_ref[...]], axis=0)
        dyv = dy_ref[...]
        extn = jnp.concatenate([dyv, jnp.where(i < ni - 1, dn_ref[...], 0.0)], axis=0)
        w = w_ref[...]
        row = _rows((SUBLANES, tc))
        dx = jnp.zeros((t, tc), F32)
        dw = jnp.where(row == CONV_K, jnp.sum(dyv, axis=0, keepdims=True), 0.0)
        for k in range(CONV_K):
            up = CONV_K - 1 - k
            dx = dx + w[k:k + 1] * (pltpu.roll(extn, t + SUBLANES - up, 0) if up else extn)[:t]
            dw = dw + jnp.where(row == k, jnp.sum(dyv * _shift_down(ext, up, t), axis=0, keepdims=True), 0.0)
        dx_ref[...] = dx.astype(dx_ref.dtype)

        @pl.when(i == 0)
        def _():
            dw_ref[...] = jnp.zeros_like(dw_ref)

        dw_ref[...] += dw

    return pl.pallas_call(
        body, name=name, grid=(width // tc, ni),
        in_specs=[pl.BlockSpec((t, tc), lambda j, i: (i, cb + j)),
                  pl.BlockSpec((SUBLANES, tc), lambda j, i: (jnp.maximum(i * tpb - 1, 0), cb + j)),
                  pl.BlockSpec((SUBLANES, tc), lambda j, i: (0, j)),
                  pl.BlockSpec((t, tc), lambda j, i: (i, j)),
                  pl.BlockSpec((SUBLANES, tc), lambda j, i: (jnp.minimum((i + 1) * tpb, s_len // SUBLANES - 1), j))],
        out_specs=[pl.BlockSpec((t, tc), lambda j, i: (i, j)), pl.BlockSpec((SUBLANES, tc), lambda j, i: (0, j))],
        out_shape=[jax.ShapeDtypeStruct((s_len, width), MXU_DTYPE), jax.ShapeDtypeStruct((SUBLANES, width), F32)],
        compiler_params=_params("parallel", "arbitrary"),
    )(src, src, w8, dy, dy)


LRU_T = 256


def _lru_gates(xc, wa, wx, ba, bx, lam):
    r = _sigmoid(_bdot(xc, wa) + ba)
    i = _sigmoid(_bdot(xc, wx) + bx)
    log_a = (-LRU_C) * r * _softplus(-lam)
    return jnp.exp(log_a), jnp.sqrt(_neg_expm1(2.0 * log_a)) * i * xc


def _block_scan(a, b, state, reverse):
    t = a.shape[0]
    row = _rows(a.shape) % SUBLANES
    s = 1
    while s < SUBLANES:
        shift, ok = (t - s, row < SUBLANES - s) if reverse else (s, row >= s)
        b = jnp.where(ok, a * pltpu.roll(b, shift, 0) + b, b)
        a = jnp.where(ok, a * pltpu.roll(a, shift, 0), a)
        s *= 2
    groups = range(t // SUBLANES)
    out = [None] * len(groups)
    for g in (reversed(groups) if reverse else groups):
        rows = slice(g * SUBLANES, (g + 1) * SUBLANES)
        out[g] = b[rows] + a[rows] * state
        state = out[g][0:1] if reverse else out[g][SUBLANES - 1:SUBLANES]
    return jnp.concatenate(out, axis=0)


def _lru_fwd(xc, proj, wa, wx, ba, bx, lam, *, name):
    s_len, w = xc.shape
    t = _tile(s_len, (LRU_T,))

    def body(xc_ref, gr_ref, wa_ref, wx_ref, ba_ref, bx_ref, lam_ref, h_ref, y_ref, carry):
        @pl.when(pl.program_id(0) == 0)
        def _():
            carry[...] = jnp.zeros_like(carry)

        a, b = _lru_gates(xc_ref[...], wa_ref[...], wx_ref[...], ba_ref[...], bx_ref[...], lam_ref[...])
        h = _block_scan(a, b, carry[0:1, :], False)
        h_ref[...] = h
        y_ref[...] = (h * _gelu(gr_ref[...])).astype(y_ref.dtype)
        carry[0:1, :] = h[t - 1:t, :]

    row = pl.BlockSpec((t, w), lambda i: (i, 0))
    mat = pl.BlockSpec((w, w), lambda i: (0, 0))
    vec = pl.BlockSpec((1, w), lambda i: (0, 0))
    return pl.pallas_call(
        body, name=name, grid=(s_len // t,),
        in_specs=[row, pl.BlockSpec((t, w), lambda i: (i, PROJ_AB_BLOCKS - 1)), mat, mat, vec, vec, vec],
        out_specs=[row, row], out_shape=[jax.ShapeDtypeStruct((s_len, w), F32), jax.ShapeDtypeStruct((s_len, w), MXU_DTYPE)],
        scratch_shapes=[pltpu.VMEM((SUBLANES, w), F32)], compiler_params=_params("arbitrary"),
    )(xc, proj, wa, wx, ba, bx, lam)


def _lru_bwd(xc, proj, hs, dcat, wa, wx, ba, bx, lam, *, name):
    s_len, w = xc.shape
    t = _tile(s_len, (LRU_T,))
    nb = s_len // t
    tpb = t // SUBLANES

    def body(xc_ref, gr_ref, h_ref, hp_ref, dy_ref, wa_ref, wx_ref, ba_ref, bx_ref, lam_ref,
             dxc_ref, dgr_ref, dwa_ref, dwx_ref, dba_ref, dbx_ref, dlam_ref, carry):
        step = pl.program_id(0)
        params = (wa_ref[...], wx_ref[...], ba_ref[...], bx_ref[...], lam_ref[...])

        @pl.when(step == 0)
        def _():
            carry[...] = jnp.zeros_like(carry)
            for r in (dwa_ref, dwx_ref, dba_ref, dbx_ref, dlam_ref):
                r[...] = jnp.zeros_like(r)

        (a, _), vjp = jax.vjp(_lru_gates, xc_ref[...], *params)
        gr, h, dy = gr_ref[...], h_ref[...], dy_ref[...]
        gel, gel_vjp = jax.vjp(_gelu, gr)
        dgr_ref[...] = gel_vjp(dy * h)[0].astype(dgr_ref.dtype)
        dh = dy * gel
        big_g = _block_scan(a, a * dh, carry[0:1, :], True)
        row = _rows((t, w))
        g = dh + jnp.where(row == t - 1, carry[0:1, :], pltpu.roll(big_g, t - 1, 0))
        carry[0:1, :] = big_g[0:1, :]
        h_last = jnp.where(step < nb - 1, hp_ref[SUBLANES - 1:SUBLANES, :], 0.0)
        h_prev = jnp.where(row == 0, h_last, pltpu.roll(h, 1, 0))
        dxc, dwa, dwx, dba, dbx, dlam = vjp((g * h_prev, g))
        dxc_ref[...] = dxc
        dwa_ref[...] += dwa
        dwx_ref[...] += dwx
        dba_ref[...] += dba
        dbx_ref[...] += dbx
        dlam_ref[...] += dlam

    rev = lambda i: nb - 1 - i
    row = pl.BlockSpec((t, w), lambda i: (rev(i), 0))
    mat = pl.BlockSpec((w, w), lambda i: (0, 0))
    vec = pl.BlockSpec((1, w), lambda i: (0, 0))
    return pl.pallas_call(
        body, name=name, grid=(nb,),
        in_specs=[row, pl.BlockSpec((t, w), lambda i: (rev(i), PROJ_AB_BLOCKS - 1)), row,
                  pl.BlockSpec((SUBLANES, w), lambda i: (jnp.maximum(rev(i) * tpb - 1, 0), 0)),
                  pl.BlockSpec((t, w), lambda i: (rev(i), 1)), mat, mat, vec, vec, vec],
        out_specs=[row, row, mat, mat, vec, vec, vec],
        out_shape=[jax.ShapeDtypeStruct((s_len, w), F32), jax.ShapeDtypeStruct((s_len, w), MXU_DTYPE)]
        + [jax.ShapeDtypeStruct((w, w), F32)] * 2 + [jax.ShapeDtypeStruct((1, w), F32)] * 3,
        scratch_shapes=[pltpu.VMEM((SUBLANES, w), F32)], compiler_params=_params("arbitrary"),
    )(xc, proj, hs, hs, dcat, wa, wx, ba, bx, lam)


XA_T = 256
XA_SCALE = XA_HEAD_DIM ** -0.5


def _xa_heads(q, k, v):
    s = _bmm_nt(q, k) * XA_SCALE
    e = jnp.exp(s - jnp.max(s, axis=-1, keepdims=True))
    return _bmm(e / jnp.sum(e, axis=-1, keepdims=True), v)


def _xa_stack(ref):
    return jnp.stack([ref[:, h * XA_HEAD_DIM:(h + 1) * XA_HEAD_DIM].astype(F32) for h in range(XA_HEADS)], axis=0)


def _xa_fwd(q, kv, *, name):
    s_len, d = q.shape
    n_mem = kv.shape[0]
    t = _tile(s_len, (XA_T,))

    def body(q_ref, k_ref, v_ref, o_ref):
        o = _xa_heads(_xa_stack(q_ref), _xa_stack(k_ref), _xa_stack(v_ref))
        for h in range(XA_HEADS):
            o_ref[:, h * XA_HEAD_DIM:(h + 1) * XA_HEAD_DIM] = o[h].astype(o_ref.dtype)

    return pl.pallas_call(
        body, name=name, grid=(s_len // t,),
        in_specs=[pl.BlockSpec((t, d), lambda i: (i, 0)), pl.BlockSpec((n_mem, d), lambda i: (0, 0)),
                  pl.BlockSpec((n_mem, d), lambda i: (0, 1))],
        out_specs=pl.BlockSpec((t, d), lambda i: (i, 0)), out_shape=jax.ShapeDtypeStruct((s_len, d), MXU_DTYPE),
        compiler_params=_params("parallel"),
    )(q, kv, kv)


def _xa_bwd(q, kv, dy, wo, *, name):
    s_len, d = q.shape
    n_mem = kv.shape[0]
    t = _tile(s_len, (XA_T,))

    def body(q_ref, k_ref, v_ref, dy_ref, wo_ref, dq_ref, dk_ref, dv_ref):
        @pl.when(pl.program_id(0) == 0)
        def _():
            dk_ref[...] = jnp.zeros_like(dk_ref)
            dv_ref[...] = jnp.zeros_like(dv_ref)

        do = _dg(dy_ref[...], wo_ref[...], ((1,), (1,)))
        do = jnp.stack([do[:, h * XA_HEAD_DIM:(h + 1) * XA_HEAD_DIM] for h in range(XA_HEADS)], axis=0)
        _, vjp = jax.vjp(_xa_heads, _xa_stack(q_ref), _xa_stack(k_ref), _xa_stack(v_ref))
        dq, dk, dv = vjp(do)
        for h in range(XA_HEADS):
            sl = slice(h * XA_HEAD_DIM, (h + 1) * XA_HEAD_DIM)
            dq_ref[:, sl] = dq[h].astype(dq_ref.dtype)
            dk_ref[:, sl] += dk[h]
            dv_ref[:, sl] += dv[h]

    row = pl.BlockSpec((t, d), lambda i: (i, 0))
    dq, dk, dv = pl.pallas_call(
        body, name=name, grid=(s_len // t,),
        in_specs=[row, pl.BlockSpec((n_mem, d), lambda i: (0, 0)), pl.BlockSpec((n_mem, d), lambda i: (0, 1)), row,
                  pl.BlockSpec(wo.shape, lambda i: (0, 0), pipeline_mode=RESIDENT)],
        out_specs=[row, pl.BlockSpec((n_mem, d), lambda i: (0, 0)), pl.BlockSpec((n_mem, d), lambda i: (0, 0))],
        out_shape=[jax.ShapeDtypeStruct((s_len, d), MXU_DTYPE)] + [jax.ShapeDtypeStruct((n_mem, d), F32)] * 2,
        compiler_params=_params("arbitrary"),
    )(q, kv, kv, dy, wo)
    return dq, jnp.concatenate([dk, dv], axis=1)


DN_Q_SCALE = DN_HEAD_DIM ** -0.5
L2_EPS = 1e-6


def _bdg(a, b, ca, cb):
    return lax.dot_general(a.astype(MXU_DTYPE), b.astype(MXU_DTYPE), (((ca,), (cb,)), ((0,), (0,))), preferred_element_type=F32)


@jax.custom_vjp
def _bmm(a, b):
    return _bdg(a, b, 2, 1)


_bmm.defvjp(lambda a, b: (_bdg(a, b, 2, 1), (a, b)), lambda r, g: (_bdg(g, r[1], 2, 2), _bdg(r[0], g, 1, 1)))


@jax.custom_vjp
def _bmm_nt(a, b):
    return _bdg(a, b, 2, 2)


_bmm_nt.defvjp(lambda a, b: (_bdg(a, b, 2, 2), (a, b)), lambda r, g: (_bdg(g, r[1], 2, 1), _bdg(g, r[0], 1, 1)))


@jax.custom_vjp
def _bmm_tn(a, b):
    return _bdg(a, b, 1, 1)


_bmm_tn.defvjp(lambda a, b: (_bdg(a, b, 1, 1), (a, b)), lambda r, g: (_bdg(r[1], g, 2, 2), _bdg(r[0], g, 2, 1)))


def _tri_inverse(n):
    eye = (lax.broadcasted_iota(jnp.int32, n.shape, 1) == lax.broadcasted_iota(jnp.int32, n.shape, 2)).astype(F32)
    inv, p = eye - n, n
    for _ in range(5):
        p = _bdg(p, p, 2, 1)
        inv = _bdg(inv, eye + p, 2, 1)
    return inv


@jax.custom_vjp
def _tri_solve2(n, r1, r2):
    t = _tri_inverse(n)
    return _bdg(t, r1, 2, 1), _bdg(t, r2, 2, 1)


def _tri_solve2_fwd(n, r1, r2):
    t = _tri_inverse(n)
    x1, x2 = _bdg(t, r1, 2, 1), _bdg(t, r2, 2, 1)
    return (x1, x2), (t, x1, x2)


def _tri_solve2_bwd(saved, cts):
    t, x1, x2 = saved
    d1, d2 = _bdg(t, cts[0], 1, 1), _bdg(t, cts[1], 1, 1)
    return -(_bdg(d1, x1, 2, 2) + _bdg(d2, x2, 2, 2)), d1, d2


_tri_solve2.defvjp(_tri_solve2_fwd, _tri_solve2_bwd)


def _dn_gates(ab, alog, dtb):
    return -jnp.exp(alog) * _softplus(ab + dtb), _sigmoid(ab)


def _dn_heads(cq, ck, cv, z, g, beta, onorm, state):
    h, c, _ = cq.shape
    l2 = lambda t: t * lax.rsqrt(jnp.sum(t * t, axis=-1, keepdims=True) + L2_EPS)
    q, k, v = l2(_silu(cq)) * DN_Q_SCALE, l2(_silu(ck)), _silu(cv)
    r, cc = lax.broadcasted_iota(jnp.int32, (h, c, c), 1), lax.broadcasted_iota(jnp.int32, (h, c, c), 2)
    tri, eye = r >= cc, r == cc
    g_row = jnp.sum(jnp.where(eye, g, 0.0), axis=1, keepdims=True)
    gcum_c = jnp.sum(jnp.where(tri, g_row, 0.0), axis=2, keepdims=True)
    gcum_r = jnp.sum(jnp.where(cc >= r, g, 0.0), axis=1, keepdims=True)
    decay = jnp.where(tri, jnp.exp(jnp.where(tri, gcum_c - gcum_r, 0.0)), 0.0)
    kb = k * beta
    n = jnp.where(r > cc, _bmm_nt(kb, k) * decay, 0.0)
    u, w = _tri_solve2(n, v * beta, kb * jnp.exp(gcum_c))
    v_new = u - _bmm(w, state)
    o = _bmm(q * jnp.exp(gcum_c), state) + _bmm(_bmm_nt(q, k) * decay, v_new)
    g_last = jnp.sum(g, axis=1, keepdims=True)
    new_state = state * jnp.exp(g_last) + _bmm_tn(k * jnp.exp(g_last - gcum_c), v_new)
    on = o * lax.rsqrt(jnp.mean(o * o, axis=-1, keepdims=True) + NORM_EPS) * onorm
    return on * _silu(z), new_state


def _dn_stack(ref, col0):
    return jnp.stack([ref[:, col0 + h * DN_HEAD_DIM:col0 + (h + 1) * DN_HEAD_DIM].astype(F32) for h in range(DN_HEADS)], axis=0)


def _dn_cols(block, col0):
    return jnp.stack([block[:, col0 + h:col0 + h + 1] for h in range(DN_HEADS)], axis=0)


def _dn_fwd(cqkv, proj, ab, alog, dtb, onorm, *, name):
    s_len = cqkv.shape[0]
    c, hd, w = DN_CHUNK, DN_HEAD_DIM, DN_WIDTH
    n_chunks = s_len // c

    def body(c_ref, z_ref, ab_ref, alog_ref, dtb_ref, on_ref, o_ref, st_ref, state):
        @pl.when(pl.program_id(0) == 0)
        def _():
            state[...] = jnp.zeros_like(state)

        g_all, beta_all = _dn_gates(ab_ref[...], alog_ref[...], dtb_ref[...])
        st = state[...]
        st_ref[0] = st
        out, new = _dn_heads(_dn_stack(c_ref, 0), _dn_stack(c_ref, w), _dn_stack(c_ref, 2 * w), _dn_stack(z_ref, 0),
                             _dn_cols(g_all, 0), _dn_cols(beta_all, DN_HEADS), on_ref[...], st)
        state[...] = new
        for h in range(DN_HEADS):
            o_ref[:, h * hd:(h + 1) * hd] = out[h].astype(o_ref.dtype)

    vec = pl.BlockSpec((1, LANES), lambda i: (0, 0))
    return pl.pallas_call(
        body, name=name, grid=(n_chunks,),
        in_specs=[pl.BlockSpec((c, 3 * w), lambda i: (i, 0)), pl.BlockSpec((c, w), lambda i: (i, 3)),
                  pl.BlockSpec((c, LANES), lambda i: (i, 0)), vec, vec, vec],
        out_specs=[pl.BlockSpec((c, w), lambda i: (i, 0)), pl.BlockSpec((1, DN_HEADS, hd, hd), lambda i: (i, 0, 0, 0))],
        out_shape=[jax.ShapeDtypeStruct((s_len, w), MXU_DTYPE), jax.ShapeDtypeStruct((n_chunks, DN_HEADS, hd, hd), F32)],
        scratch_shapes=[pltpu.VMEM((DN_HEADS, hd, hd), F32)], compiler_params=_params("arbitrary"),
    )(cqkv, proj, ab, alog, dtb, onorm)


def _dn_bwd(cqkv, proj, ab, alog, dtb, onorm, states, dout, *, name):
    s_len = cqkv.shape[0]
    c, hd, w = DN_CHUNK, DN_HEAD_DIM, DN_WIDTH
    n_chunks = s_len // c

    def body(c_ref, z_ref, ab_ref, alog_ref, dtb_ref, on_ref, st_ref, do_ref,
             dc_ref, dz_ref, dab_ref, dalog_ref, ddtb_ref, don_ref, dstate):
        @pl.when(pl.program_id(0) == 0)
        def _():
            dstate[...] = jnp.zeros_like(dstate)
            for r in (dalog_ref, ddtb_ref, don_ref):
                r[...] = jnp.zeros_like(r)

        (g_all, beta_all), gates_vjp = jax.vjp(_dn_gates, ab_ref[...], alog_ref[...], dtb_ref[...])
        _, vjp = jax.vjp(_dn_heads, _dn_stack(c_ref, 0), _dn_stack(c_ref, w), _dn_stack(c_ref, 2 * w), _dn_stack(z_ref, 0),
                         _dn_cols(g_all, 0), _dn_cols(beta_all, DN_HEADS), on_ref[...], st_ref[0])
        dcq, dck, dcv, dz, dg, dbeta, don, dst = vjp((_dn_stack(do_ref, 0), dstate[...]))
        dstate[...] = dst
        col = _cols((c, LANES))
        dg_all, dbeta_all = jnp.zeros((c, LANES), F32), jnp.zeros((c, LANES), F32)
        for h in range(DN_HEADS):
            sl = slice(h * hd, (h + 1) * hd)
            dc_ref[:, sl] = dcq[h]
            dc_ref[:, w + h * hd:w + (h + 1) * hd] = dck[h]
            dc_ref[:, 2 * w + h * hd:2 * w + (h + 1) * hd] = dcv[h]
            dz_ref[:, sl] = dz[h].astype(dz_ref.dtype)
            dg_all = dg_all + jnp.where(col == h, dg[h], 0.0)
            dbeta_all = dbeta_all + jnp.where(col == DN_HEADS + h, dbeta[h], 0.0)
        dab, dalog, ddtb = gates_vjp((dg_all, dbeta_all))
        dab_ref[...] = dab
        dalog_ref[...] += dalog
        ddtb_ref[...] += ddtb
        don_ref[...] += don

    rev = lambda i: n_chunks - 1 - i
    vec = pl.BlockSpec((1, LANES), lambda i: (0, 0))
    return pl.pallas_call(
        body, name=name, grid=(n_chunks,),
        in_specs=[pl.BlockSpec((c, 3 * w), lambda i: (rev(i), 0)), pl.BlockSpec((c, w), lambda i: (rev(i), 3)),
                  pl.BlockSpec((c, LANES), lambda i: (rev(i), 0)), vec, vec, vec,
                  pl.BlockSpec((1, DN_HEADS, hd, hd), lambda i: (rev(i), 0, 0, 0)), pl.BlockSpec((c, w), lambda i: (rev(i), 0))],
        out_specs=[pl.BlockSpec((c, 3 * w), lambda i: (rev(i), 0)), pl.BlockSpec((c, w), lambda i: (rev(i), 0)),
                   pl.BlockSpec((c, LANES), lambda i: (rev(i), 0)), vec, vec, vec],
        out_shape=[jax.ShapeDtypeStruct((s_len, 3 * w), F32), jax.ShapeDtypeStruct((s_len, w), MXU_DTYPE),
                   jax.ShapeDtypeStruct((s_len, LANES), F32)] + [jax.ShapeDtypeStruct((1, LANES), F32)] * 3,
        scratch_shapes=[pltpu.VMEM((DN_HEADS, hd, hd), F32)], compiler_params=_params("arbitrary"),
    )(cqkv, proj, ab, alog, dtb, onorm, states, dout)


def _final_loss(x, g, target, *, name):
    s, d = x.shape
    tm = _tile(s, (512, 256))

    def body(x_ref, g_ref, t_ref, loss_ref, dx_ref, dg_ref):
        @pl.when(pl.program_id(0) == 0)
        def _():
            loss_ref[...] = jnp.zeros_like(loss_ref)
            dg_ref[...] = jnp.zeros_like(dg_ref)

        xv, gv = x_ref[...], g_ref[...]
        r = lax.rsqrt(jnp.mean(xv * xv, axis=-1, keepdims=True) + NORM_EPS)
        xh = xv * r
        err = xh * gv - t_ref[...]
        loss_ref[...] += 0.5 * jnp.sum(jnp.mean(err * err, axis=-1, keepdims=True), axis=0, keepdims=True)
        dy = err * (1.0 / d)
        dxh = dy * gv
        dx_ref[...] = r * (dxh - xh * jnp.mean(dxh * xh, axis=-1, keepdims=True))
        dg_ref[...] += jnp.sum(dy * xh, axis=0, keepdims=True)

    row = pl.BlockSpec((tm, d), lambda i: (i, 0))
    vec = pl.BlockSpec((1, d), lambda i: (0, 0))
    return pl.pallas_call(
        body, name=name, grid=(s // tm,), in_specs=[row, vec, row],
        out_specs=[pl.BlockSpec((1, LANES), lambda i: (0, 0)), row, vec],
        out_shape=[jax.ShapeDtypeStruct((1, LANES), F32), jax.ShapeDtypeStruct((s, d), F32), jax.ShapeDtypeStruct((1, d), F32)],
        compiler_params=_params("arbitrary"),
    )(x, g.reshape(1, d), target)


def _adamw(w, g_layers, m, v, *, name):
    shape = w.shape
    cols = shape[-1]
    n_l = len(g_layers)
    rows = max(w.size // cols, 1) // n_l
    tr = _tile(rows, [t for t in (512, 352, 256, 128, 64, 32, 16, 8) if t * cols * 4 <= ADAMW_BLOCK_BYTES])
    c1, c2 = 1.0 - ADAM_B1 ** ADAM_STEP, 1.0 - ADAM_B2 ** ADAM_STEP

    def body(*refs):
        w_ref, m_ref, v_ref = refs[:3]
        d_ref, nm_ref, nv_ref, g_ref = refs[3 + n_l:]
        gv = refs[3][...]
        for k in range(1, n_l):
            gv = jnp.where(pl.program_id(0) == k, refs[3 + k][...], gv)
        nm = ADAM_B1 * m_ref[...] + (1.0 - ADAM_B1) * gv
        nv = ADAM_B2 * v_ref[...] + (1.0 - ADAM_B2) * (gv * gv)
        d_ref[...] = -ADAM_LR * ((nm / c1) / (jnp.sqrt(nv / c2) + ADAM_EPS) + ADAM_WD * w_ref[...])
        nm_ref[...] = nm
        nv_ref[...] = nv
        g_ref[...] = gv

    blk = pl.BlockSpec((None, tr, cols), lambda l, i: (l, i, 0))
    slab = pl.BlockSpec((tr, cols), lambda l, i: (i, 0))
    outs = pl.pallas_call(
        body, name=name, grid=(n_l, rows // tr), in_specs=[blk] * 3 + [slab] * n_l, out_specs=[blk] * 4,
        out_shape=[jax.ShapeDtypeStruct((n_l, rows, cols), F32)] * 4, compiler_params=_params("parallel", "parallel"),
    )(*(t.reshape(n_l, rows, cols) for t in (w, m, v)), *(t.reshape(rows, cols) for t in g_layers))
    return tuple(t.reshape(shape) for t in outs)


def _block_diag(w):
    n, j, k = w.shape
    eye = jnp.eye(n, dtype=w.dtype)
    return (eye[:, None, :, None] * w[:, :, None, :]).reshape(n * j, n * k)


def _block_diag_part(m, n):
    j, k = m.shape[0] // n, m.shape[1] // n
    m4 = m.reshape(n, j, n, k)
    return jnp.stack([m4[i, :, i, :] for i in range(n)], axis=0)


DN_AB = 2 * DN_HEADS
DEPTH = 2


def _row(v, width=None):
    v = v.reshape(1, -1)
    return v if width is None else jnp.pad(v, ((0, 0), (0, width - v.shape[1])))


def _conv_w8(conv_w, bias=None):
    w8 = jnp.zeros((SUBLANES, conv_w.shape[1]), F32).at[:CONV_K].set(conv_w)
    return w8 if bias is None else w8.at[CONV_K].set(bias)


def _mixer_ab_fwd(x, w, tag):
    h, (proj,) = _norm_mm(x, w["mix_norm"][0], [w["ab_w_in"][0]], [F32], name=f"{tag}_in")
    o, lse = _dattn_forward(proj, tag)
    w8 = _conv_w8(w["lru_conv_w"][0], w["lru_conv_b"][0])
    xc = _conv_fwd(proj, PROJ_AB_BLOCKS - 2, LRU_WIDTH, w8, name=f"{tag}_conv")
    wa, wx = _block_diag(w["lru_w_a"][0]), _block_diag(w["lru_w_x"][0])
    vecs = (_row(w["lru_b_a"][0]), _row(w["lru_b_x"][0]), _row(w["lru_lambda"][0]))
    hs, y = _lru_fwd(xc, proj, wa, wx, *vecs, name=f"{tag}_lru")
    w_out = w["ab_w_out"][0]
    x2 = _mm(o, w_out[:ATTN_WIDTH], res=x, name=f"{tag}_out_attn")
    x2 = _mm(y, w_out[ATTN_WIDTH:], res=x2, name=f"{tag}_out_lru")
    return x2, (x, h, proj, o, lse, w8, xc, wa, wx, vecs, hs, y)


def _mixer_ab_bwd(saved, w, dy, tag):
    x, h, proj, o, lse, w8, xc, wa, wx, vecs, hs, y = saved
    w_out = w["ab_w_out"][0]
    dcat = _mm(dy, w_out, mode="nt", name=f"{tag}_dcat")
    dw_out = jnp.concatenate([_mm(o, dy, mode="tn", name=f"{tag}_dwout_attn"), _mm(y, dy, mode="tn", name=f"{tag}_dwout_lru")], axis=0)
    dq, dk, dv = _dattn_backward(proj, o, lse, dcat, tag)
    dxc, dgr, dwa, dwx, dba, dbx, dlam = _lru_bwd(xc, proj, hs, dcat, wa, wx, *vecs, name=f"{tag}_dlru")
    dxr, dw8 = _conv_bwd(proj, PROJ_AB_BLOCKS - 2, LRU_WIDTH, w8, dxc, name=f"{tag}_dconv")
    dproj = jnp.concatenate([t.astype(MXU_DTYPE) for t in (dq, dk, dv, dxr, dgr)], axis=1)
    dw_in = _mm(h, dproj, mode="tn", name=f"{tag}_dwin")
    dx, dg = _mm_rms_bwd([(dproj, w["ab_w_in"][0])], x, w["mix_norm"][0], dy, name=f"{tag}_dh")
    grads = dict(mix_norm=dg[0], ab_w_in=dw_in, ab_w_out=dw_out, lru_conv_w=dw8[:CONV_K], lru_conv_b=dw8[CONV_K],
                 lru_w_a=_block_diag_part(dwa, LRU_BLOCKS), lru_b_a=dba[0], lru_w_x=_block_diag_part(dwx, LRU_BLOCKS),
                 lru_b_x=dbx[0], lru_lambda=dlam[0])
    return dx, grads


def _dn_split_w(w_in):
    return w_in[:, :4 * DN_WIDTH], jnp.pad(w_in[:, 4 * DN_WIDTH:], ((0, 0), (0, LANES - DN_AB)))


def _mixer_dn_fwd(x, w, tag):
    w_qkvz, w_ab = _dn_split_w(w["dn_w_in"][0])
    h, (proj, ab) = _norm_mm(x, w["mix_norm"][1], [w_qkvz, w_ab], [F32, F32], name=f"{tag}_in")
    w8 = _conv_w8(w["dn_conv_w"][0])
    cqkv = _conv_fwd(proj, 0, 3 * DN_WIDTH, w8, name=f"{tag}_conv")
    vecs = (_row(w["dn_a_log"][0], LANES), _row(w["dn_dt_bias"][0], LANES), _row(w["dn_o_norm"][0]))
    og, states = _dn_fwd(cqkv, proj, ab, *vecs, name=f"{tag}_dn")
    x2 = _mm(og, w["dn_w_out"][0], res=x, name=f"{tag}_out")
    return x2, (x, h, w_qkvz, w_ab, proj, ab, w8, cqkv, vecs, og, states)


def _mixer_dn_bwd(saved, w, dy, tag):
    x, h, w_qkvz, w_ab, proj, ab, w8, cqkv, vecs, og, states = saved
    dout = _mm(dy, w["dn_w_out"][0], mode="nt", name=f"{tag}_dout")
    dw_out = _mm(og, dy, mode="tn", name=f"{tag}_dwout")
    dcqkv, dz, dab, dalog, ddtb, don = _dn_bwd(cqkv, proj, ab, *vecs, states, dout, name=f"{tag}_ddn")
    dqkv, dw8 = _conv_bwd(proj, 0, 3 * DN_WIDTH, w8, dcqkv, name=f"{tag}_dconv")
    dproj = jnp.concatenate([dqkv.astype(MXU_DTYPE), dz.astype(MXU_DTYPE)], axis=1)
    dw_in = jnp.concatenate([_mm(h, dproj, mode="tn", name=f"{tag}_dwin"),
                             _mm(h, dab, mode="tn", name=f"{tag}_dwin_ab")[:, :DN_AB]], axis=1)
    dx, dg = _mm_rms_bwd([(dproj, w_qkvz), (dab, w_ab)], x, w["mix_norm"][1], dy, name=f"{tag}_dh")
    grads = dict(mix_norm=dg[0], dn_w_in=dw_in, dn_w_out=dw_out, dn_conv_w=dw8[:CONV_K], dn_a_log=dalog[0, :DN_HEADS],
                 dn_dt_bias=ddtb[0, :DN_HEADS], dn_o_norm=don[0])
    return dx, grads


def _xa_layer_fwd(x, mem, w, layer, tag):
    hq, (q,) = _norm_mm(x, w["xa_norm"][layer], [_layer_matrix(w["xa_wq"], layer)], [MXU_DTYPE], name=f"{tag}_q")
    hm = _rms_fwd(mem, w["xa_mem_norm"][layer], name=f"{tag}_mem_norm")
    kv = _mm(hm, _layer_matrix(w["xa_wkv"], layer), name=f"{tag}_kv")
    oa = _xa_fwd(q, kv, name=f"{tag}_core")
    x2 = _mm(oa, _layer_matrix(w["xa_wo"], layer), res=x, name=f"{tag}_out")
    return x2, (x, hq, q, hm, kv, oa)


def _xa_layer_bwd(saved, mem, w, layer, dy, tag):
    x, hq, q, hm, kv, oa = saved
    dwo = _mm(oa, dy, mode="tn", name=f"{tag}_dwo")
    dq, dkv = _xa_bwd(q, kv, dy, _layer_matrix(w["xa_wo"], layer), name=f"{tag}_dcore")
    dwq = _mm(hq, dq, mode="tn", name=f"{tag}_dwq")
    dx, dg = _mm_rms_bwd([(dq, _layer_matrix(w["xa_wq"], layer))], x, w["xa_norm"][layer], dy, name=f"{tag}_dhq")
    dwkv = _mm(hm, dkv, mode="tn", name=f"{tag}_dwkv")
    dhm = _mm(dkv, _layer_matrix(w["xa_wkv"], layer), mode="nt", name=f"{tag}_dhm")
    _, dgm = _rms_bwd(mem, w["xa_mem_norm"][layer], dhm, jnp.zeros_like(mem), name=f"{tag}_dmem_norm")
    return dx, dict(xa_norm=dg[0], xa_mem_norm=dgm[0], xa_wq=dwq, xa_wkv=dwkv, xa_wo=dwo)


def _local_step(x, mem, target, w, pending=None, reduce_big=False):
    saved = []
    plans, shards = pending if pending else ([], None)
    hosts = {("ffn1", 0): plans[0], ("ffn2", 0): plans[1]} if plans else {}

    def ffn(which, layer, x):
        plan = hosts.get((which, layer))
        hosted = ([b for *_, b in plan], [a for _, _, a, _ in plan]) if plan else None
        x, s, gathered = _ffn_fwd(x, w[f"{which}_norm"], w[f"{which}_w_in"], w[f"{which}_w_out"], layer, f"l{layer}_{which}", gather=hosted)
        if plan:
            _gathered(plan, gathered, shards, w)
        return x, s

    for layer in range(DEPTH):
        t = f"l{layer}"
        x, s1 = ffn("ffn1", layer, x)
        x, s2 = (_mixer_ab_fwd if layer % 2 == 0 else _mixer_dn_fwd)(x, w, f"{t}_mix")
        x, s3 = _xa_layer_fwd(x, mem, w, layer, f"{t}_xa")
        x, s4 = ffn("ffn2", layer, x)
        saved.append((s1, s2, s3, s4))
    loss, dx, dgf = _final_loss(x, w["final_norm"], target, name="final_loss")
    per_layer, reduced = [None] * DEPTH, [None] * DEPTH
    travelling = None
    for layer in reversed(range(DEPTH)):
        t = f"l{layer}"
        s1, s2, s3, s4 = saved[layer]
        g = {}
        dx, g["ffn2_norm"], g["ffn2_w_in"], g["ffn2_w_out"], parts = _ffn_bwd(
            s4, w["ffn2_norm"], w["ffn2_w_in"], w["ffn2_w_out"], layer, dx, f"{t}_ffn2",
            exchange=travelling[1] if travelling else None)
        if travelling:
            reduced[travelling[0]] = _reduce_end(travelling[1], parts, f"l{travelling[0]}")
        dx, gx = _xa_layer_bwd(s3, mem, w, layer, dx, f"{t}_xa")
        dx, gm = (_mixer_ab_bwd if layer % 2 == 0 else _mixer_dn_bwd)(s2, w, dx, f"{t}_mix")
        dx, g["ffn1_norm"], g["ffn1_w_in"], g["ffn1_w_out"], _ = _ffn_bwd(
            s1, w["ffn1_norm"], w["ffn1_w_in"], w["ffn1_w_out"], layer, dx, f"{t}_ffn1")
        per_layer[layer] = {**g, **gx, **gm}
        if reduce_big:
            chip_sum = _reduce_begin(_pack_grads(per_layer[layer]), t)
            if layer > 0:
                travelling = (layer, chip_sum)
            else:
                reduced[layer] = _reduce_end(chip_sum, _exchange_chips(chip_sum), t)
    grads = {"final_norm": [dgf[0]]}
    for layer_grads in per_layer:
        for name, value in layer_grads.items():
            grads.setdefault(name, []).append(value)
    return loss, dx, grads, list(zip(reduced, per_layer))


N_CHIPS = 4
WIRE_DTYPE = jnp.bfloat16
HBM_SPEC = pl.BlockSpec(memory_space=pltpu.HBM)
PACK_COLS = 1024


def _place():
    x, y, c = lax.axis_index("x"), lax.axis_index("y"), lax.axis_index("c")
    return x, y, c, [(1 - x, y), (x, 1 - y), (1 - x, 1 - y)]


def _remote(src, dst, sems, k, to):
    return pltpu.make_async_remote_copy(src_ref=src, dst_ref=dst, send_sem=sems[0].at[k], recv_sem=sems[1].at[k],
                                        device_id=to, device_id_type=MESH)


def _gather_weights(blocks, axes):
    n = len(blocks)
    out_shapes, sem_shapes, start, finish = _gather_plan(blocks, axes)

    def body(*refs):
        start(refs[:n], refs[n:2 * n], *refs[2 * n:])
        finish(refs[:n], refs[n:2 * n], *refs[2 * n:])

    return pl.pallas_call(
        body, name="gather_weights", in_specs=[HBM_SPEC] * n, out_specs=[HBM_SPEC] * n,
        out_shape=out_shapes, scratch_shapes=sem_shapes,
    )(*blocks)


def _gather_plan(blocks, axes):
    n = len(blocks)
    split = [b.shape[1] % 32 == 0 for b in blocks]

    def full_shape(i):
        l, r, c = blocks[i].shape
        return (l, N_CHIPS * r, c) if axes[i] == 1 else (l, r, N_CHIPS * c)

    def copies(ins, outs, send_sems, recv_sems):
        x, y, c, chips = _place()
        sems = (send_sems, recv_sems)
        sibling = (x, y, 1 - c)
        me = 2 * x + y

        def window(i, k, h):
            l, r, cc = blocks[i].shape
            r0, nr = (0, r) if h is None else (h * (r // 2), r // 2)
            if axes[i] == 1:
                return outs[i].at[:, pl.ds(k * r + r0, nr), :]
            return outs[i].at[:, pl.ds(r0, nr), pl.ds(k * cc, cc)]

        def mine(i, h):
            r = blocks[i].shape[1]
            return ins[i] if h is None else ins[i].at[:, pl.ds(h * (r // 2), r // 2), :]

        half = lambda i: c if split[i] else None
        first = [_remote(mine(i, half(i)), window(i, me, half(i)), sems, 3 * i + j, (*chip, c))
                 for i in range(n) for j, chip in enumerate(chips)]
        first += [_remote(ins[i], window(i, me, None), sems, 6 * n + i, sibling) for i in range(n)]
        arrive = lambda i, j, h, k, frm: _remote(window(i, 2 * chips[j][0] + chips[j][1], h), window(i, 2 * chips[j][0] + chips[j][1], h),
                                                 sems, k, frm)
        return first, arrive, chips, c, sibling

    def start(ins, outs, send_sems, recv_sems):
        for cp in copies(ins, outs, send_sems, recv_sems)[0]:
            cp.start()

    def finish(ins, outs, send_sems, recv_sems):
        first, arrive, chips, c, sibling = copies(ins, outs, send_sems, recv_sems)
        passed = []
        for i in range(n):
            for j, (cx, cy) in enumerate(chips):
                arrive(i, j, c if split[i] else None, 3 * i + j, (cx, cy, c)).wait_recv()
                if split[i]:
                    passed.append(arrive(i, j, c, 3 * (n + i) + j, sibling))
                    passed[-1].start()
        for i in range(n):
            if split[i]:
                for j in range(len(chips)):
                    arrive(i, j, 1 - c, 3 * (n + i) + j, sibling).wait_recv()
        for cp in first[3 * n:]:
            cp.wait_recv()
        for cp in first + passed:
            cp.wait_send()

    sem_shapes = [pltpu.SemaphoreType.DMA((7 * n,)), pltpu.SemaphoreType.DMA((7 * n,))]
    return [jax.ShapeDtypeStruct(full_shape(i), blocks[i].dtype) for i in range(n)], sem_shapes, start, finish


def _allreduce_small(v):
    rows, cols = v.shape
    n_dev = 2 * N_CHIPS

    def body(v_ref, out_ref, all_ref, send_sems, recv_sems, local_sem):
        x, y, c, chips = _place()
        sems = (send_sems, recv_sems)
        me, sibling = (x, y, c), (x, y, 1 - c)
        slot = lambda px, py, pc: all_ref.at[pl.ds((4 * px + 2 * py + pc) * rows, rows), :]
        mine = pltpu.make_async_copy(v_ref, slot(*me), local_sem)
        mine.start()
        first = [_remote(v_ref, slot(*me), sems, 0, sibling)]
        first += [_remote(v_ref, slot(*me), sems, 1 + j, (*chip, c)) for j, chip in enumerate(chips)]
        for cp in first:
            cp.start()
        passed = [_remote(slot(*chip, c), slot(*chip, c), sems, 4 + j, sibling) for j, chip in enumerate(chips)]
        for j, chip in enumerate(chips):
            _remote(slot(*chip, c), slot(*chip, c), sems, 1 + j, me).wait_recv()
            passed[j].start()
        _remote(slot(*sibling), slot(*sibling), sems, 0, me).wait_recv()
        for j, chip in enumerate(chips):
            _remote(slot(*chip, 1 - c), slot(*chip, 1 - c), sems, 4 + j, me).wait_recv()
        for cp in first + passed:
            cp.wait_send()
        mine.wait()
        acc = all_ref[pl.ds(0, rows), :]
        for k in range(1, n_dev):
            acc = acc + all_ref[pl.ds(k * rows, rows), :]
        out_ref[...] = acc

    vmem = pl.BlockSpec(memory_space=pltpu.VMEM)
    return pl.pallas_call(
        body, name="allreduce_small", in_specs=[vmem], out_specs=vmem, out_shape=jax.ShapeDtypeStruct((rows, cols), F32),
        scratch_shapes=[pltpu.VMEM((n_dev * rows, cols), F32), pltpu.SemaphoreType.DMA((7,)), pltpu.SemaphoreType.DMA((7,)),
                        pltpu.SemaphoreType.DMA],
    )(v)


def _swap_other_half(g4, tag):
    n, _, rows, cols = g4.shape

    def body(v_ref, out_ref, send_sems, recv_sems):
        x, y, c, _ = _place()
        cp = _remote(v_ref.at[:, 1 - c], out_ref, (send_sems, recv_sems), 0, (x, y, 1 - c))
        cp.start()
        cp.wait()

    return pl.pallas_call(
        body, name=f"{tag}_reduce_swap", in_specs=[HBM_SPEC], out_specs=HBM_SPEC, out_shape=jax.ShapeDtypeStruct((n, rows, cols), g4.dtype),
        scratch_shapes=[pltpu.SemaphoreType.DMA((1,)), pltpu.SemaphoreType.DMA((1,))],
    )(g4)


def _add_kept_half(g4, got, tag):
    n, _, rows, cols = g4.shape
    tr = _tile(rows, (256, 128, 64, 32, 16))
    nb = rows // tr

    def body(c_ref, a_ref, b_ref, o_ref):
        o_ref[...] = (a_ref[...] + b_ref[...]).astype(o_ref.dtype)

    return pl.pallas_call(
        body, name=f"{tag}_reduce_sum_cores",
        grid_spec=pltpu.PrefetchScalarGridSpec(
            num_scalar_prefetch=1, grid=(n, nb),
            in_specs=[pl.BlockSpec((None, None, tr, cols), lambda k, i, c_ref: (k, c_ref[0], i, 0)),
                      pl.BlockSpec((None, tr, cols), lambda k, i, c_ref: (k, i, 0))],
            out_specs=pl.BlockSpec((None, tr, cols), lambda k, i, c_ref: (k, i, 0))),
        out_shape=jax.ShapeDtypeStruct((n, rows, cols), WIRE_DTYPE), compiler_params=_params("parallel", "parallel"),
    )(lax.axis_index("c").astype(jnp.int32).reshape(1), g4, got)


def _exchange_plan(v):
    def copies(v_ref, out_ref, send_sems, recv_sems):
        x, y, c, chips = _place()
        return [_remote(v_ref.at[2 * cx + cy], out_ref.at[j], (send_sems, recv_sems), j, (cx, cy, c)) for j, (cx, cy) in enumerate(chips)]

    def start(*refs):
        for cp in copies(*refs):
            cp.start()

    def finish(*refs):
        for cp in copies(*refs):
            cp.wait_recv()
        for cp in copies(*refs):
            cp.wait_send()

    sem_shapes = [pltpu.SemaphoreType.DMA((N_CHIPS - 1,)), pltpu.SemaphoreType.DMA((N_CHIPS - 1,))]
    return jax.ShapeDtypeStruct((N_CHIPS - 1,) + v.shape[1:], v.dtype), sem_shapes, start, finish


def _exchange_chips(v):
    out_shape, sem_shapes, start, finish = _exchange_plan(v)

    def body(*refs):
        start(*refs)
        finish(*refs)

    return pl.pallas_call(body, name="exchange_chips", in_specs=[HBM_SPEC], out_specs=HBM_SPEC, out_shape=out_shape,
                          scratch_shapes=sem_shapes)(v)


def _swap_sibling(v, tag):
    def body(v_ref, out_ref, send_sems, recv_sems):
        x, y, c, _ = _place()
        cp = _remote(v_ref, out_ref, (send_sems, recv_sems), 0, (x, y, 1 - c))
        cp.start()
        cp.wait()

    return pl.pallas_call(
        body, name=f"{tag}_share_halves", in_specs=[HBM_SPEC], out_specs=HBM_SPEC, out_shape=jax.ShapeDtypeStruct(v.shape, v.dtype),
        scratch_shapes=[pltpu.SemaphoreType.DMA((1,)), pltpu.SemaphoreType.DMA((1,))],
    )(v)


def _sum_chips(own4, parts, tag):
    _, rows, cols = own4.shape
    tr = _tile(rows, (256, 128, 64, 32, 16))

    def body(me_ref, own_ref, p0_ref, p1_ref, p2_ref, o_ref):
        acc = own_ref[...].astype(F32)
        for r in (p0_ref, p1_ref, p2_ref):
            acc = acc + r[...].astype(F32)
        o_ref[...] = acc

    part = lambda j: pl.BlockSpec((None, tr, cols), lambda i, me_ref: (j, i, 0))
    chip = (2 * lax.axis_index("x") + lax.axis_index("y")).astype(jnp.int32).reshape(1)
    return pl.pallas_call(
        body, name=f"{tag}_reduce_sum_chips",
        grid_spec=pltpu.PrefetchScalarGridSpec(
            num_scalar_prefetch=1, grid=(rows // tr,),
            in_specs=[pl.BlockSpec((None, tr, cols), lambda i, me_ref: (me_ref[0], i, 0)), part(0), part(1), part(2)],
            out_specs=pl.BlockSpec((tr, cols), lambda i, me_ref: (i, 0))),
        out_shape=jax.ShapeDtypeStruct((rows, cols), F32), compiler_params=_params("parallel"),
    )(chip, own4, parts, parts, parts)


def _reduce_begin(g4, tag):
    return _add_kept_half(g4, _swap_other_half(g4, tag), tag)


def _reduce_end(chip_sum, parts, tag):
    half = _sum_chips(chip_sum, parts, tag)
    other = _swap_sibling(half, tag)
    return jnp.where(lax.axis_index("c") == 0, jnp.stack([half, other]), jnp.stack([other, half]))


def _reduce_grads(g4):
    chip_sum = _reduce_begin(g4, "all")
    return _reduce_end(chip_sum, _exchange_chips(chip_sum), "all")


BIG = (("ffn1_w_in", 2), ("ffn1_w_out", 1), ("xa_wq", 1), ("xa_wkv", 2), ("xa_wo", 1), ("ffn2_w_in", 2), ("ffn2_w_out", 1),
       ("ab_w_in", 2), ("ab_w_out", 1), ("dn_w_in", 2), ("dn_w_out", 1))
TINY_SHARDED = (("lru_conv_w", 2), ("dn_conv_w", 2))
REPLICATED = ("ffn1_norm", "mix_norm", "xa_norm", "xa_mem_norm", "ffn2_norm", "lru_conv_b", "lru_w_a", "lru_b_a", "lru_w_x",
              "lru_b_x", "lru_lambda", "dn_a_log", "dn_dt_bias", "dn_o_norm", "final_norm")
WEIGHTS = ("ffn1_norm", "ffn1_w_in", "ffn1_w_out", "mix_norm", "xa_norm", "xa_mem_norm", "xa_wq", "xa_wkv", "xa_wo", "ffn2_norm",
           "ffn2_w_in", "ffn2_w_out", "ab_w_in", "lru_conv_w", "lru_conv_b", "lru_w_a", "lru_b_a", "lru_w_x", "lru_b_x",
           "lru_lambda", "ab_w_out", "dn_w_in", "dn_conv_w", "dn_a_log", "dn_dt_bias", "dn_o_norm", "dn_w_out", "final_norm")


def _pad_rows(flat, row_multiple):
    n = flat.shape[-1]
    per = row_multiple * PACK_COLS
    total = -(-n // per) * per
    flat = jnp.pad(flat, [(0, 0)] * (flat.ndim - 1) + [(0, total - n)])
    return flat.reshape(flat.shape[:-1] + (total // PACK_COLS, PACK_COLS))


def _lane_padded(shape):
    return shape[:-1] + (-(-shape[-1] // LANES) * LANES,)


def _pad_lanes(t):
    return jnp.pad(t, [(0, 0)] * (t.ndim - 1) + [(0, _lane_padded(t.shape)[-1] - t.shape[-1])])


FIRST_USED = ("ffn1_w_in", "ffn1_w_out")
LAYER_1_ONLY = ("dn_w_in", "dn_w_out", "dn_conv_w")


def _gather_blocks(shards):
    groups = [], [], []
    for n, a in BIG + TINY_SHARDED:
        block = _pad_lanes(shards[n]).astype(MXU_DTYPE) if (n, a) in BIG else shards[n]
        if block.shape[0] == 1:
            groups[2 if n in LAYER_1_ONLY else 1].append((n, None, a, block))
        else:
            for layer in range(block.shape[0]):
                group = 2 if layer > 0 else 0 if n in FIRST_USED else 1
                groups[group].append((n, layer, a, block[layer:layer + 1]))
    return groups


def _gathered(plan, arrays, shards, into):
    for (n, layer, axis, _), full in zip(plan, arrays):
        width, padded = shards[n].shape[-1], _lane_padded(shards[n].shape)[-1]
        if padded != width:
            assert axis == 2
            full = jnp.concatenate([full[..., k * padded:k * padded + width] for k in range(N_CHIPS)], axis=-1)
        if layer is None:
            into[n] = full
        else:
            into.setdefault(n, [None] * shards[n].shape[0])[layer] = full
    return into


def _to_blocks(pieces, axis):
    width = pieces[0].shape[axis - 1] // N_CHIPS
    block = lambda p, k: _pad_lanes(lax.slice_in_dim(p, k * width, (k + 1) * width, axis=axis - 1)).reshape(-1)
    return jnp.stack([jnp.concatenate([block(p, k) for p in pieces]) for k in range(N_CHIPS)], axis=0)


def _pack_grads(layer_grads):
    flat = jnp.concatenate([_to_blocks([layer_grads[n]], axis) for n, axis in BIG if n in layer_grads], axis=1)
    g = _pad_rows(flat, 2 * 256)
    return g.reshape(N_CHIPS, 2, g.shape[1] // 2, PACK_COLS)


def _unpack_grads(reduced, layer_grads, shards):
    flat = reduced.reshape(-1)
    out, off = {}, 0
    for n, _ in BIG:
        if n in layer_grads:
            shape = (1,) + shards[n].shape[1:]
            size = math.prod(_lane_padded(shape))
            out[n] = flat[off:off + size].reshape(_lane_padded(shape))[..., :shape[-1]]
            off += size
    return out


def _pack_small(grads, loss):
    parts = [p.reshape(-1) for n in REPLICATED + tuple(n for n, _ in TINY_SHARDED) for p in grads[n]] + [loss[0, :1]]
    flat = jnp.concatenate(parts)
    total = -(-flat.shape[0] // (SUBLANES * LANES)) * SUBLANES * LANES
    return jnp.pad(flat, (0, total - flat.shape[0])).reshape(-1, LANES)


def _unpack_small(summed, shards, chip):
    flat = summed.reshape(-1)
    out, off = {}, 0
    for n in REPLICATED:
        out[n] = flat[off:off + shards[n].size].reshape(shards[n].shape)
        off += shards[n].size
    for n, axis in TINY_SHARDED:
        width = shards[n].shape[axis]
        shape = shards[n].shape[:axis] + (N_CHIPS * width,) + shards[n].shape[axis + 1:]
        full = flat[off:off + N_CHIPS * shards[n].size].reshape(shape)
        out[n] = lax.dynamic_slice_in_dim(full, chip * width, width, axis=axis)
        off += N_CHIPS * shards[n].size
    return out, flat[off]


def kernel(x, mem, ffn1_norm, ffn1_w_in, ffn1_w_out, mix_norm, xa_norm, xa_mem_norm, xa_wq, xa_wkv, xa_wo, ffn2_norm,
           ffn2_w_in, ffn2_w_out, ab_w_in, lru_conv_w, lru_conv_b, lru_w_a, lru_b_a, lru_w_x, lru_b_x, lru_lambda,
           ab_w_out, dn_w_in, dn_conv_w, dn_a_log, dn_dt_bias, dn_o_norm, dn_w_out, final_norm, loss_target,
           m_ffn1_norm, m_ffn1_w_in, m_ffn1_w_out, m_mix_norm, m_xa_norm, m_xa_mem_norm, m_xa_wq, m_xa_wkv, m_xa_wo,
           m_ffn2_norm, m_ffn2_w_in, m_ffn2_w_out, m_ab_w_in, m_lru_conv_w, m_lru_conv_b, m_lru_w_a, m_lru_b_a,
           m_lru_w_x, m_lru_b_x, m_lru_lambda, m_ab_w_out, m_dn_w_in, m_dn_conv_w, m_dn_a_log, m_dn_dt_bias,
           m_dn_o_norm, m_dn_w_out, m_final_norm, v_ffn1_norm, v_ffn1_w_in, v_ffn1_w_out, v_mix_norm, v_xa_norm,
           v_xa_mem_norm, v_xa_wq, v_xa_wkv, v_xa_wo, v_ffn2_norm, v_ffn2_w_in, v_ffn2_w_out, v_ab_w_in,
           v_lru_conv_w, v_lru_conv_b, v_lru_w_a, v_lru_b_a, v_lru_w_x, v_lru_b_x, v_lru_lambda, v_ab_w_out,
           v_dn_w_in, v_dn_conv_w, v_dn_a_log, v_dn_dt_bias, v_dn_o_norm, v_dn_w_out, v_final_norm):
    given = dict(locals())
    shards = {n: given[n] for n in WEIGHTS}
    chip = 2 * lax.axis_index("x") + lax.axis_index("y")

    full = {n: shards[n] for n in REPLICATED}
    first, *later = _gather_blocks(shards)
    _gathered(first, _gather_weights([b for *_, b in first], [a for _, _, a, _ in first]), shards, full)
    loss, grad_x, grads, reduced = _local_step(x[0], mem[0], loss_target[0], full, pending=(later, shards), reduce_big=True)

    small, loss_sum = _unpack_small(_allreduce_small(_pack_small(grads, loss)), shards, chip)
    per_layer = [_unpack_grads(r, layer_grads, shards) for r, layer_grads in reduced]
    slabs = {n: [g] for n, g in small.items()}
    slabs.update({n: [p[n] for p in per_layer if n in p] for n, _ in BIG})

    grad, delta, new_m, new_v = {}, {}, {}, {}
    for n in WEIGHTS:
        delta[n], new_m[n], new_v[n], grad[n] = _adamw(shards[n], slabs[n], given["m_" + n], given["v_" + n], name=f"adamw_{n}")
    return (loss_sum, grad_x[None], *[grad[n] for n in WEIGHTS], *[delta[n] for n in WEIGHTS],
            *[new_m[n] for n in WEIGHTS], *[new_v[n] for n in WEIGHTS])
```

```python
import math

import jax
import jax.numpy as jnp
from jax import lax
from jax.experimental import pallas as pl
from jax.experimental.pallas import tpu as pltpu

F32 = jnp.float32
MXU_DTYPE = jnp.bfloat16
VMEM_LIMIT_BYTES = 48 * 1024 * 1024
MM_BLOCK_BYTES = 8 * 1024 * 1024
ADAMW_BLOCK_BYTES = 1024 * 1024
LANES = 128
SUBLANES = 8

NORM_EPS = 1e-6
CONV_K = 4
ATTN_PAIRS = 4
ATTN_HEAD_DIM = 64
ATTN_WIDTH = 512
ATTN_BLOCK = 128
DILATIONS = (1, 4, 16)
LRU_WIDTH = 512
LRU_BLOCKS = 8
LRU_C = 8.0
DN_HEADS = 8
DN_HEAD_DIM = 128
DN_WIDTH = 1024
DN_CHUNK = 64
XA_HEADS = 4
XA_HEAD_DIM = 256
D_FF = 2816
ADAM_LR, ADAM_B1, ADAM_B2, ADAM_EPS, ADAM_WD, ADAM_STEP = 0.001, 0.9, 0.999, 1e-08, 0.01, 10

MESH = pl.DeviceIdType.MESH


def _tile(n, prefs):
    for p in prefs:
        if n % p == 0:
            return p
    return n


def _params(*sem):
    return pltpu.CompilerParams(dimension_semantics=sem, vmem_limit_bytes=VMEM_LIMIT_BYTES)


def _dg(a, b, dims, hi=False):
    if hi:
        return lax.dot_general(a, b, (dims, ((), ())), precision=lax.Precision.HIGHEST, preferred_element_type=F32)
    return lax.dot_general(a.astype(MXU_DTYPE), b.astype(MXU_DTYPE), (dims, ((), ())), preferred_element_type=F32)


@jax.custom_vjp
def _bdot(a, b):
    return _dg(a, b, ((1,), (0,)))


def _bdot_fwd(a, b):
    return _bdot(a, b), (a, b)


def _bdot_bwd(r, g):
    a, b = r
    return _dg(g, b, ((1,), (1,))).astype(a.dtype), _dg(a, g, ((0,), (0,))).astype(b.dtype)


_bdot.defvjp(_bdot_fwd, _bdot_bwd)


def _log1p(t):
    return jnp.where(t < 0.01, t * (1.0 - t * (0.5 - t * (1.0 / 3.0))), jnp.log(1.0 + t))


def _neg_expm1(y):
    series = -y * (1.0 + 0.5 * y * (1.0 + (1.0 / 3.0) * y * (1.0 + 0.25 * y)))
    return jnp.where(y > -0.01, series, 1.0 - jnp.exp(y))


def _softplus(x):
    return jnp.maximum(x, 0.0) + _log1p(jnp.exp(-jnp.abs(x)))


def _sigmoid(x):
    return 0.5 * jnp.tanh(0.5 * x) + 0.5


def _silu(x):
    return x * _sigmoid(x)


def _gelu(x):
    return 0.5 * x * (1.0 + jnp.tanh(0.7978845608028654 * (x + 0.044715 * x * x * x)))


def _rows(shape):
    return lax.broadcasted_iota(jnp.int32, shape, 0)


def _cols(shape):
    return lax.broadcasted_iota(jnp.int32, shape, 1)


def _mm(a, b, *, mode="nn", out_dtype=F32, res=None, scale=1.0, name):
    if mode == "nn":
        (m, k), (k2, n) = a.shape, b.shape
    elif mode == "nt":
        (m, k), (n, k2) = a.shape, b.shape
    else:
        (k, m), (k2, n) = a.shape, b.shape
    assert k == k2, (a.shape, b.shape, mode)
    if mode == "tn":
        tm, tn, tk = _tile(m, (1024, 512, 256, 128)), _tile(n, (1024, 512, 256, 128)), _tile(k, (2048, 1024, 512, 256))
    else:
        tm, tn = _tile(m, (512, 256, 128)), _tile(n, (1024, 512, 256, 128))
        tk = k if k * tn * 2 <= MM_BLOCK_BYTES else _tile(k, (1024, 512, 256, 128))
    nk = k // tk
    dims = {"nn": ((1,), (0,)), "nt": ((1,), (1,)), "tn": ((0,), (0,))}[mode]

    def body(*refs):
        a_ref, b_ref = refs[:2]
        r_ref = refs[2] if res is not None else None
        o_ref = refs[3 if res is not None else 2]

        def finish(r):
            if scale != 1.0:
                r = r * scale
            if res is not None:
                r = r_ref[...] + r
            o_ref[...] = r.astype(out_dtype)

        if nk == 1:
            finish(_dg(a_ref[...], b_ref[...], dims))
            return
        acc = refs[-1]
        kk = pl.program_id(2)

        @pl.when(kk == 0)
        def _():
            acc[...] = jnp.zeros_like(acc)

        acc[...] += _dg(a_ref[...], b_ref[...], dims)

        @pl.when(kk == nk - 1)
        def _():
            finish(acc[...])

    a_spec = pl.BlockSpec((tk, tm), lambda i, j, kk: (kk, i)) if mode == "tn" else pl.BlockSpec((tm, tk), lambda i, j, kk: (i, kk))
    b_spec = pl.BlockSpec((tn, tk), lambda i, j, kk: (j, kk)) if mode == "nt" else pl.BlockSpec((tk, tn), lambda i, j, kk: (kk, j))
    o_spec = pl.BlockSpec((tm, tn), lambda i, j, kk: (i, j))
    in_specs = [a_spec, b_spec] + ([o_spec] if res is not None else [])
    args = (a, b) + ((res,) if res is not None else ())
    return pl.pallas_call(
        body, name=name, grid=(m // tm, n // tn, nk), in_specs=in_specs, out_specs=o_spec,
        out_shape=jax.ShapeDtypeStruct((m, n), out_dtype), scratch_shapes=[pltpu.VMEM((tm, tn), F32)] if nk > 1 else [],
        compiler_params=_params("parallel", "parallel", "arbitrary"),
    )(*args)


def _rms_fwd(x, g, *, name):
    s, d = x.shape
    tm = _tile(s, (512, 256))

    def body(x_ref, g_ref, o_ref):
        xv = x_ref[...]
        r = lax.rsqrt(jnp.mean(xv * xv, axis=-1, keepdims=True) + NORM_EPS)
        o_ref[...] = (xv * r * g_ref[...]).astype(o_ref.dtype)

    return pl.pallas_call(
        body, name=name, grid=(s // tm,),
        in_specs=[pl.BlockSpec((tm, d), lambda i: (i, 0)), pl.BlockSpec((1, d), lambda i: (0, 0))],
        out_specs=pl.BlockSpec((tm, d), lambda i: (i, 0)), out_shape=jax.ShapeDtypeStruct((s, d), MXU_DTYPE),
        compiler_params=_params("parallel"),
    )(x, g.reshape(1, d))


def _norm_mm(x, g, ws, out_dtypes, *, name):
    s, d = x.shape
    tm = _tile(s, (512, 256))
    nw = len(ws)

    def body(*refs):
        x_ref, g_ref = refs[:2]
        h_ref = refs[2 + nw]
        xv = x_ref[...]
        r = lax.rsqrt(jnp.mean(xv * xv, axis=-1, keepdims=True) + NORM_EPS)
        h = (xv * r * g_ref[...]).astype(MXU_DTYPE)
        h_ref[...] = h
        for w_ref, o_ref in zip(refs[2:2 + nw], refs[3 + nw:]):
            o_ref[...] = _dg(h, w_ref[...], ((1,), (0,))).astype(o_ref.dtype)

    row = lambda w: pl.BlockSpec((tm, w), lambda i: (i, 0))
    outs = pl.pallas_call(
        body, name=name, grid=(s // tm,),
        in_specs=[row(d), pl.BlockSpec((1, d), lambda i: (0, 0))]
        + [pl.BlockSpec(w.shape, lambda i: (0, 0), pipeline_mode=RESIDENT) for w in ws],
        out_specs=[row(d)] + [row(w.shape[1]) for w in ws],
        out_shape=[jax.ShapeDtypeStruct((s, d), MXU_DTYPE)] + [jax.ShapeDtypeStruct((s, w.shape[1]), t) for w, t in zip(ws, out_dtypes)],
        compiler_params=_params("parallel"),
    )(x, g.reshape(1, d), *ws)
    return outs[0], outs[1:]


def _mm_rms_bwd(pairs, x, g, dres, *, name):
    s, d = x.shape
    tm = _tile(s, (512, 256))
    n = len(pairs)

    def body(*refs):
        x_ref, g_ref, dr_ref = refs[2 * n:2 * n + 3]
        dx_ref, dg_ref = refs[2 * n + 3:]
        dh = _dg(refs[0][...], refs[n][...], ((1,), (1,)))
        for a_ref, w_ref in zip(refs[1:n], refs[n + 1:2 * n]):
            dh = dh + _dg(a_ref[...], w_ref[...], ((1,), (1,)))
        xv, gv = x_ref[...], g_ref[...]
        r = lax.rsqrt(jnp.mean(xv * xv, axis=-1, keepdims=True) + NORM_EPS)
        xh = xv * r
        dxh = dh * gv
        dx_ref[...] = dr_ref[...] + r * (dxh - xh * jnp.mean(dxh * xh, axis=-1, keepdims=True))

        @pl.when(pl.program_id(0) == 0)
        def _():
            dg_ref[...] = jnp.zeros_like(dg_ref)

        dg_ref[...] += jnp.sum(dh * xh, axis=0, keepdims=True)

    row = lambda w: pl.BlockSpec((tm, w), lambda i: (i, 0))
    vec = pl.BlockSpec((1, d), lambda i: (0, 0))
    return pl.pallas_call(
        body, name=name, grid=(s // tm,),
        in_specs=[row(a.shape[1]) for a, _ in pairs]
        + [pl.BlockSpec(w.shape, lambda i: (0, 0), pipeline_mode=RESIDENT) for _, w in pairs] + [row(d), vec, row(d)],
        out_specs=[row(d), vec], out_shape=[jax.ShapeDtypeStruct((s, d), F32), jax.ShapeDtypeStruct((1, d), F32)],
        compiler_params=_params("arbitrary"),
    )(*[a for a, _ in pairs], *[w for _, w in pairs], x, g.reshape(1, d), dres)


def _rms_bwd(x, g, dh, dres, *, name):
    s, d = x.shape
    tm = _tile(s, (512, 256))

    def body(x_ref, g_ref, dh_ref, dr_ref, dx_ref, dg_ref):
        xv = x_ref[...]
        r = lax.rsqrt(jnp.mean(xv * xv, axis=-1, keepdims=True) + NORM_EPS)
        xh = xv * r
        dhv = dh_ref[...].astype(F32)
        dxh = dhv * g_ref[...]
        dx = r * (dxh - xh * jnp.mean(dxh * xh, axis=-1, keepdims=True))
        dx_ref[...] = dr_ref[...] + dx

        @pl.when(pl.program_id(0) == 0)
        def _():
            dg_ref[...] = jnp.zeros_like(dg_ref)

        dg_ref[...] += jnp.sum(dhv * xh, axis=0, keepdims=True)

    row = pl.BlockSpec((tm, d), lambda i: (i, 0))
    vec = pl.BlockSpec((1, d), lambda i: (0, 0))
    return pl.pallas_call(
        body, name=name, grid=(s // tm,), in_specs=[row, vec, row, row], out_specs=[row, vec],
        out_shape=[jax.ShapeDtypeStruct((s, d), F32), jax.ShapeDtypeStruct((1, d), F32)],
        compiler_params=_params("arbitrary"),
    )(x, g.reshape(1, d), dh, dres)


FFN_CHUNK = 256
FFN_TM = 256
RESIDENT = pl.Buffered(1)


def _ffn_fwd_call(x, g, w_in, w_out, layer, *, name, gather=None):
    s, d = x.shape
    f = w_out.shape[1]
    tm = _tile(s, (2 * FFN_TM, FFN_TM))
    steps = s // tm
    n_g = len(gather[0]) if gather else 0
    g_shapes, g_sems, g_start, g_finish = _gather_plan(*gather) if gather else ([], [], None, None)

    def body(*refs):
        x_ref, g_ref, wi_ref, wo_ref = refs[:4]
        y_ref, u_ref = refs[4 + n_g:6 + n_g]
        act_ref = refs[6 + 2 * n_g]
        if gather:
            comm = (refs[4:4 + n_g], refs[6 + n_g:6 + 2 * n_g], *refs[7 + 2 * n_g:])
            pl.when(pl.program_id(0) == 0)(lambda: g_start(*comm))
        xv = x_ref[...]
        r = lax.rsqrt(jnp.mean(xv * xv, axis=-1, keepdims=True) + NORM_EPS)
        h = (xv * r * g_ref[...]).astype(MXU_DTYPE)
        for j in range(f // FFN_CHUNK):
            lo, hi = j * FFN_CHUNK, (j + 1) * FFN_CHUNK
            gate = _dg(h, wi_ref[:, lo:hi], ((1,), (0,))).astype(MXU_DTYPE)
            up = _dg(h, wi_ref[:, f + lo:f + hi], ((1,), (0,))).astype(MXU_DTYPE)
            u_ref[:, lo:hi] = gate
            u_ref[:, f + lo:f + hi] = up
            act_ref[:, lo:hi] = (_silu(gate.astype(F32)) * up.astype(F32)).astype(MXU_DTYPE)
        y_ref[...] = xv + 0.5 * _dg(act_ref[...], wo_ref[...], ((1,), (0,)))
        if gather:
            pl.when(pl.program_id(0) == steps - 1)(lambda: g_finish(*comm))

    row = lambda w: pl.BlockSpec((tm, w), lambda i: (i, 0))
    return pl.pallas_call(
        body, name=name, grid=(steps,),
        in_specs=[row(d), pl.BlockSpec((1, d), lambda i: (0, 0)),
                  pl.BlockSpec((None,) + w_in.shape[1:], lambda i: (layer, 0, 0), pipeline_mode=RESIDENT),
                  pl.BlockSpec((None,) + w_out.shape[1:], lambda i: (layer, 0, 0), pipeline_mode=RESIDENT)] + [HBM_SPEC] * n_g,
        out_specs=[row(d), row(2 * f)] + [HBM_SPEC] * n_g,
        out_shape=[jax.ShapeDtypeStruct((s, d), F32), jax.ShapeDtypeStruct((s, 2 * f), MXU_DTYPE)] + g_shapes,
        scratch_shapes=[pltpu.VMEM((tm, f), MXU_DTYPE)] + g_sems,
        compiler_params=_params("arbitrary" if gather else "parallel"),
    )(x, g.reshape(1, d), w_in, w_out, *(gather[0] if gather else ()))


def _ffn_bwd_call(x, g, u, dy, w_in, w_out, layer, *, name, exchange=None):
    s, d = x.shape
    f = w_out.shape[1]
    tm = _tile(s, (FFN_TM,))
    steps = s // tm
    hosted = exchange is not None
    e_shape, e_sems, e_start, e_finish = _exchange_plan(exchange) if hosted else (None, [], None, None)

    def body(*refs):
        x_ref, g_ref, u_ref, dy_ref, wi_ref, wo_ref = refs[:6]
        du_ref, dx_ref, dg_ref, h_ref = refs[6 + hosted:10 + hosted]
        if hosted:
            comm = (refs[6], *refs[10 + hosted:])
            pl.when(pl.program_id(0) == 0)(lambda: e_start(*comm))
        dyv = dy_ref[...]
        dyh = (0.5 * dyv).astype(MXU_DTYPE)
        for j in range(f // FFN_CHUNK):
            lo, hi = j * FFN_CHUNK, (j + 1) * FFN_CHUNK
            dact = _dg(dyh, wo_ref[lo:hi, :], ((1,), (1,)))
            gate, up = u_ref[:, lo:hi].astype(F32), u_ref[:, f + lo:f + hi].astype(F32)
            sg = _sigmoid(gate)
            du_ref[:, lo:hi] = (dact * up * sg * (1.0 + gate * (1.0 - sg))).astype(MXU_DTYPE)
            du_ref[:, f + lo:f + hi] = (dact * gate * sg).astype(MXU_DTYPE)
        dh = _dg(du_ref[...], wi_ref[...], ((1,), (1,)))
        xv, gv = x_ref[...], g_ref[...]
        r = lax.rsqrt(jnp.mean(xv * xv, axis=-1, keepdims=True) + NORM_EPS)
        xh = xv * r
        h_ref[...] = (xh * gv).astype(MXU_DTYPE)
        dxh = dh * gv
        dx_ref[...] = dyv + r * (dxh - xh * jnp.mean(dxh * xh, axis=-1, keepdims=True))

        @pl.when(pl.program_id(0) == 0)
        def _():
            dg_ref[...] = jnp.zeros_like(dg_ref)

        dg_ref[...] += jnp.sum(dh * xh, axis=0, keepdims=True)
        if hosted:
            pl.when(pl.program_id(0) == steps - 1)(lambda: e_finish(*comm))

    row = lambda w: pl.BlockSpec((tm, w), lambda i: (i, 0))
    vec = pl.BlockSpec((1, d), lambda i: (0, 0))
    return pl.pallas_call(
        body, name=name, grid=(steps,),
        in_specs=[row(d), vec, row(2 * f), row(d),
                  pl.BlockSpec((None,) + w_in.shape[1:], lambda i: (layer, 0, 0), pipeline_mode=RESIDENT),
                  pl.BlockSpec((None,) + w_out.shape[1:], lambda i: (layer, 0, 0), pipeline_mode=RESIDENT)] + [HBM_SPEC] * hosted,
        out_specs=[row(2 * f), row(d), vec, row(d)] + [HBM_SPEC] * hosted,
        out_shape=[jax.ShapeDtypeStruct((s, 2 * f), MXU_DTYPE), jax.ShapeDtypeStruct((s, d), F32),
                   jax.ShapeDtypeStruct((1, d), F32), jax.ShapeDtypeStruct((s, d), MXU_DTYPE)] + [e_shape] * hosted,
        scratch_shapes=e_sems, compiler_params=_params("arbitrary"),
    )(x, g.reshape(1, d), u, dy, w_in, w_out, *([exchange] if hosted else []))


def _ffn_dw_out(u, dy, *, name):
    s, f2 = u.shape
    f, d = f2 // 2, dy.shape[1]
    tf, tk = _tile(f, (1408, 256, 128)), _tile(s, (1024, 512, 256))
    nj = f // tf

    def body(g_ref, u_ref, dy_ref, o_ref):
        @pl.when(pl.program_id(1) == 0)
        def _():
            o_ref[...] = jnp.zeros_like(o_ref)

        act = _silu(g_ref[...].astype(F32)) * u_ref[...].astype(F32)
        o_ref[...] += _dg(act, 0.5 * dy_ref[...], ((0,), (0,)))

    return pl.pallas_call(
        body, name=name, grid=(nj, s // tk),
        in_specs=[pl.BlockSpec((tk, tf), lambda j, k: (k, j)), pl.BlockSpec((tk, tf), lambda j, k: (k, j + nj)),
                  pl.BlockSpec((tk, d), lambda j, k: (k, 0))],
        out_specs=pl.BlockSpec((tf, d), lambda j, k: (j, 0)), out_shape=jax.ShapeDtypeStruct((f, d), F32),
        compiler_params=_params("parallel", "arbitrary"),
    )(u, u, dy)


def _of_layer(w, layer):
    return (w[layer], 0) if isinstance(w, (list, tuple)) else (w, layer)


def _layer_matrix(w, layer):
    w, at = _of_layer(w, layer)
    return w[at]


def _ffn_fwd(x, g, w_in, w_out, layer, tag, gather=None):
    (w_in, at), (w_out, _) = _of_layer(w_in, layer), _of_layer(w_out, layer)
    y, u, *gathered = _ffn_fwd_call(x, g[layer], w_in, w_out, at, name=f"{tag}_fwd", gather=gather)
    return y, (x, u), gathered


def _ffn_bwd(saved, g, w_in, w_out, layer, dy, tag, exchange=None):
    x, u = saved
    (w_in, at), (w_out, _) = _of_layer(w_in, layer), _of_layer(w_out, layer)
    du, dx, dg, h, *parts = _ffn_bwd_call(x, g[layer], u, dy, w_in, w_out, at, name=f"{tag}_bwd", exchange=exchange)
    dw_out = _ffn_dw_out(u, dy, name=f"{tag}_dwout")
    dw_in = _mm(h, du, mode="tn", name=f"{tag}_dwin")
    return dx, dg[0], dw_in, dw_out, (parts[0] if parts else None)


ATTN_SCALE = ATTN_HEAD_DIM ** -0.5
NEG_BIG = -1e30
PROJ_AB_BLOCKS = 5


def _first_head(n):
    return _cols((n, LANES)) < ATTN_HEAD_DIM


def _per_head(tiles):
    first = _first_head(tiles[0].shape[0])
    return jnp.stack([jnp.where(first == (h == 0), t, 0.0) for t in tiles for h in (0, 1)], axis=0)


def _both(tiles):
    return jnp.stack([t for t in tiles for _ in (0, 1)], axis=0)


def _head_cols(tiles):
    return jnp.stack([t[:, c0:c0 + 1] for t in tiles for c0 in (0, ATTN_HEAD_DIM)], axis=0)


def _join_heads(v):
    return [v[2 * u] + v[2 * u + 1] for u in range(v.shape[0] // 2)]


def _spread_heads(v):
    first = _first_head(v.shape[1])
    return [jnp.where(first, v[2 * u], v[2 * u + 1]) for u in range(v.shape[0] // 2)]


def _band_masks(has_prev):
    qi, kj = _rows((ATTN_BLOCK, ATTN_BLOCK)), _cols((ATTN_BLOCK, ATTN_BLOCK))
    return (kj >= qi) & has_prev, kj <= qi


def _dattn_delta(o, dcat, *, name):
    s_len = o.shape[0]
    tm = _tile(s_len, (512, 256))

    def body(o_ref, do_ref, out_ref):
        r, c = _rows((ATTN_WIDTH, ATTN_WIDTH)), _cols((ATTN_WIDTH, ATTN_WIDTH))
        ones_bd = (r // ATTN_HEAD_DIM == c // ATTN_HEAD_DIM).astype(F32)
        out_ref[...] = _dg(o_ref[...] * do_ref[...], ones_bd, ((1,), (0,)), hi=True)

    blk = pl.BlockSpec((tm, ATTN_WIDTH), lambda i: (i, 0))
    return pl.pallas_call(
        body, name=name, grid=(s_len // tm,), in_specs=[blk, blk], out_specs=blk,
        out_shape=jax.ShapeDtypeStruct((s_len, ATTN_WIDTH), F32), compiler_params=_params("parallel"),
    )(o, dcat)


ATTN_UNITS = 4


def _units(it, d):
    if d == 1:
        return [(pl.ds(0, ATTN_BLOCK), pl.ds(p * LANES, LANES)) for p in range(ATTN_UNITS)]
    return [(pl.ds(it * ATTN_UNITS + u, ATTN_BLOCK, stride=d), pl.ds(0, LANES)) for u in range(ATTN_UNITS)]


def _tiles(ref, units):
    return [ref[rows, lanes] for rows, lanes in units]


def _store_tiles(ref, units, tiles):
    for (rows, lanes), t in zip(units, tiles):
        ref[rows, lanes] = t


def _stacked(a_tiles, b_tiles):
    return [jnp.concatenate([a, b], axis=0) for a, b in zip(a_tiles, b_tiles)]


def _passes(d):
    return max(d // ATTN_UNITS, 1)


def _pairs_per_step(d):
    return ATTN_PAIRS if d == 1 else 1


def _pair_specs(d, n_of):
    pairs = _pairs_per_step(d)
    groups = ATTN_PAIRS // pairs
    return lambda c: pl.BlockSpec((ATTN_BLOCK * d, LANES * pairs), lambda n, p: (n_of(n), c * groups + p))


def _sattn_fwd(proj, state, d, *, last, name):
    s_len = proj.shape[0]
    nb = s_len // (ATTN_BLOCK * d)
    first = state is None
    n_out = 2 if last else 3

    def body(*refs):
        q_ref, kp_ref, kc_ref, vp_ref, vc_ref = refs[:5]
        st_refs = () if first else refs[5:8]
        out_refs = refs[-n_out:]
        ok = jnp.concatenate(_band_masks(pl.program_id(0) > 0), axis=1)

        def one_pass(it, carry):
            units = _units(it, d)
            kcat = _stacked(_tiles(kp_ref, units), _tiles(kc_ref, units))
            vcat = _stacked(_tiles(vp_ref, units), _tiles(vc_ref, units))
            s = jnp.where(ok, _bdg(_per_head(_tiles(q_ref, units)), _both(kcat), 2, 2) * ATTN_SCALE, NEG_BIG)
            m_new = jnp.max(s, axis=2, keepdims=True)
            if not first:
                m_old = _head_cols(_tiles(st_refs[0], units))
                m_new = jnp.maximum(m_old, m_new)
                alpha = jnp.exp(m_old - m_new)
            p = jnp.exp(s - m_new)
            l_new = jnp.sum(p, axis=2, keepdims=True)
            acc = _join_heads(_bdg(p, _per_head(vcat), 2, 1))
            if not first:
                l_new = l_new + _head_cols(_tiles(st_refs[1], units)) * alpha
                acc = [a + a_in * sp for a, a_in, sp in zip(acc, _tiles(st_refs[2], units), _spread_heads(alpha))]
            m_pair, l_pair = _spread_heads(m_new), _spread_heads(l_new)
            if last:
                _store_tiles(out_refs[0], units, [a / l for a, l in zip(acc, l_pair)])
                _store_tiles(out_refs[1], units, [m + jnp.log(l) for m, l in zip(m_pair, l_pair)])
            else:
                _store_tiles(out_refs[0], units, m_pair)
                _store_tiles(out_refs[1], units, l_pair)
                _store_tiles(out_refs[2], units, acc)
            return carry

        lax.fori_loop(0, _passes(d), one_pass, 0)

    cur, prev = _pair_specs(d, lambda n: n), _pair_specs(d, lambda n: jnp.maximum(n - 1, 0))
    st = cur(0)
    return tuple(pl.pallas_call(
        body, name=name, grid=(nb, ATTN_PAIRS // _pairs_per_step(d)),
        in_specs=[cur(0), prev(1), cur(1), prev(2), cur(2)] + ([] if first else [st] * 3),
        out_specs=[st] * n_out, out_shape=[jax.ShapeDtypeStruct((s_len, ATTN_WIDTH), F32)] * n_out,
        compiler_params=_params("arbitrary", "parallel"),
    )(*([proj] * 5 + ([] if first else list(state)))))


def _dattn_forward(proj, tag):
    state = None
    for i, d in enumerate(DILATIONS):
        state = _sattn_fwd(proj, state, d, last=i == len(DILATIONS) - 1, name=f"{tag}_attn_d{d}")
    return state


def _sattn_bwd(proj, dcat, lse, delta, grads_in, d, *, name):
    s_len = proj.shape[0]
    nb = s_len // (ATTN_BLOCK * d)
    first = grads_in is None
    groups = ATTN_PAIRS // _pairs_per_step(d)

    def body(*refs):
        q_ref, kp_ref, kc_ref, vp_ref, vc_ref, do_ref, lse_ref, dl_ref = refs[:8]
        dq_in, dk_in, dv_in = (None, None, None) if first else refs[8:11]
        dq_ref, dk_ref, dv_ref, carry_k, carry_v = refs[-5:]
        n = pl.program_id(1)
        ok = jnp.concatenate(_band_masks(n > 0), axis=1)

        @pl.when(n == 0)
        def _():
            carry_k[...] = jnp.zeros_like(carry_k)
            carry_v[...] = jnp.zeros_like(carry_v)

        def leave(ref, carry, units, extra, into):
            out = [c + e for c, e in zip(_tiles(carry, units), extra)] if extra else _tiles(carry, units)
            if into is not None:
                out = [a + b for a, b in zip(out, _tiles(into, units))]
            _store_tiles(ref, units, out)

        def one_pass(it, carry):
            units = _units(it, d)
            kcat = _stacked(_tiles(kp_ref, units), _tiles(kc_ref, units))
            vcat = _stacked(_tiles(vp_ref, units), _tiles(vc_ref, units))
            q2, do2 = _per_head(_tiles(q_ref, units)), _per_head(_tiles(do_ref, units))
            s = _bdg(q2, _both(kcat), 2, 2) * ATTN_SCALE
            pr = jnp.where(ok, jnp.exp(jnp.where(ok, s, NEG_BIG) - _head_cols(_tiles(lse_ref, units))), 0.0)
            ds = pr * (_bdg(do2, _both(vcat), 2, 2) - _head_cols(_tiles(dl_ref, units)))
            dq = [t * ATTN_SCALE for t in _join_heads(_bdg(ds, _per_head(kcat), 2, 1))]
            if not first:
                dq = [a + b for a, b in zip(dq, _tiles(dq_in, units))]
            _store_tiles(dq_ref, units, dq)
            dk = [t * ATTN_SCALE for t in _join_heads(_bdg(ds, q2, 1, 1))]
            dv = _join_heads(_bdg(pr, do2, 1, 1))
            leave(dk_ref, carry_k, units, [t[:ATTN_BLOCK] for t in dk], dk_in)
            leave(dv_ref, carry_v, units, [t[:ATTN_BLOCK] for t in dv], dv_in)
            _store_tiles(carry_k, units, [t[ATTN_BLOCK:] for t in dk])
            _store_tiles(carry_v, units, [t[ATTN_BLOCK:] for t in dv])
            return carry

        def last_pass(it, carry):
            units = _units(it, d)
            leave(dk_ref, carry_k, units, None, dk_in)
            leave(dv_ref, carry_v, units, None, dv_in)
            return carry

        @pl.when(n < nb)
        def _():
            lax.fori_loop(0, _passes(d), one_pass, 0)

        @pl.when(n == nb)
        def _():
            lax.fori_loop(0, _passes(d), last_pass, 0)

    pairs = _pairs_per_step(d)
    blk = (ATTN_BLOCK * d, LANES * pairs)
    at = lambda n_of: (lambda c: pl.BlockSpec(blk, lambda p, n: (n_of(n), c * groups + p)))
    here = lambda n: jnp.minimum(n, nb - 1)
    cur, prev, lag = at(here), at(lambda n: jnp.maximum(here(n) - 1, 0)), at(lambda n: jnp.maximum(n - 1, 0))
    st, st_lag = cur(0), lag(0)
    return tuple(pl.pallas_call(
        body, name=name, grid=(groups, nb + 1),
        in_specs=[cur(0), prev(1), cur(1), prev(2), cur(2), st, st, st] + ([] if first else [st, st_lag, st_lag]),
        out_specs=[st, st_lag, st_lag], out_shape=[jax.ShapeDtypeStruct((s_len, ATTN_WIDTH), F32)] * 3,
        scratch_shapes=[pltpu.VMEM(blk, F32)] * 2, compiler_params=_params("parallel", "arbitrary"),
    )(*([proj] * 5 + [dcat, lse, delta] + ([] if first else list(grads_in)))))


def _dattn_backward(proj, o, lse, dcat, tag):
    delta = _dattn_delta(o, dcat, name=f"{tag}_attn_delta")
    grads = None
    for d in DILATIONS:
        grads = _sattn_bwd(proj, dcat, lse, delta, grads, d, name=f"{tag}_attn_bwd_d{d}")
    return grads


CONV_TC = 512
CONV_T = 512


def _conv_tiles(s_len, cb0, width):
    wide = 2 * CONV_TC
    tc = wide if width % wide == 0 and (cb0 * CONV_TC) % wide == 0 else CONV_TC
    return _tile(s_len, (CONV_T, CONV_T // 2)), tc, cb0 * CONV_TC // tc


def _shift_down(ext, k, t):
    return (pltpu.roll(ext, k, 0) if k else ext)[SUBLANES:SUBLANES + t]


def _conv_fwd(src, cb0, width, w8, *, name):
    s_len = src.shape[0]
    t, tc, cb = _conv_tiles(s_len, cb0, width)
    tpb = t // SUBLANES

    def body(x_ref, h_ref, w_ref, y_ref):
        halo = jnp.where(pl.program_id(0) > 0, h_ref[...], 0.0)
        ext = jnp.concatenate([halo, x_ref[...]], axis=0)
        w = w_ref[...]
        y = jnp.broadcast_to(w[CONV_K:CONV_K + 1], (t, tc))
        for k in range(CONV_K):
            y = y + w[k:k + 1] * _shift_down(ext, CONV_K - 1 - k, t)
        y_ref[...] = y

    return pl.pallas_call(
        body, name=name, grid=(s_len // t, width // tc),
        in_specs=[pl.BlockSpec((t, tc), lambda i, j: (i, cb + j)),
                  pl.BlockSpec((SUBLANES, tc), lambda i, j: (jnp.maximum(i * tpb - 1, 0), cb + j)),
                  pl.BlockSpec((SUBLANES, tc), lambda i, j: (0, j))],
        out_specs=pl.BlockSpec((t, tc), lambda i, j: (i, j)), out_shape=jax.ShapeDtypeStruct((s_len, width), F32),
        compiler_params=_params("parallel", "parallel"),
    )(src, src, w8)


def _conv_bwd(src, cb0, width, w8, dy, *, name):
    s_len = src.shape[0]
    t, tc, cb = _conv_tiles(s_len, cb0, width)
    tpb = t // SUBLANES
    ni = s_len // t

    def body(x_ref, h_ref, w_ref, dy_ref, dn_ref, dx_ref, dw_ref):
        i = pl.program_id(1)
        halo = jnp.where(i > 0, h_ref[...], 0.0)
        ext = jnp.concatenate([halo, x_ref[...]], axis=0)
        dyv = dy_ref[...]
        extn = jnp.concatenate([dyv, jnp.where(i < ni - 1, dn_ref[...], 0.0)], axis=0)
        w = w_ref[...]
        row = _rows((SUBLANES, tc))
        dx = jnp.zeros((t, tc), F32)
        dw = jnp.where(row == CONV_K, jnp.sum(dyv, axis=0, keepdims=True), 0.0)
        for k in range(CONV_K):
            up = CONV_K - 1 - k
            dx = dx + w[k:k + 1] * (pltpu.roll(extn, t + SUBLANES - up, 0) if up else extn)[:t]
            dw = dw + jnp.where(row == k, jnp.sum(dyv * _shift_down(ext, up, t), axis=0, keepdims=True), 0.0)
        dx_ref[...] = dx.astype(dx_ref.dtype)

        @pl.when(i == 0)
        def _():
            dw_ref[...] = jnp.zeros_like(dw_ref)

        dw_ref[...] += dw

    return pl.pallas_call(
        body, name=name, grid=(width // tc, ni),
        in_specs=[pl.BlockSpec((t, tc), lambda j, i: (i, cb + j)),
                  pl.BlockSpec((SUBLANES, tc), lambda j, i: (jnp.maximum(i * tpb - 1, 0), cb + j)),
                  pl.BlockSpec((SUBLANES, tc), lambda j, i: (0, j)),
                  pl.BlockSpec((t, tc), lambda j, i: (i, j)),
                  pl.BlockSpec((SUBLANES, tc), lambda j, i: (jnp.minimum((i + 1) * tpb, s_len // SUBLANES - 1), j))],
        out_specs=[pl.BlockSpec((t, tc), lambda j, i: (i, j)), pl.BlockSpec((SUBLANES, tc), lambda j, i: (0, j))],
        out_shape=[jax.ShapeDtypeStruct((s_len, width), MXU_DTYPE), jax.ShapeDtypeStruct((SUBLANES, width), F32)],
        compiler_params=_params("parallel", "arbitrary"),
    )(src, src, w8, dy, dy)


LRU_T = 256


def _lru_gates(xc, wa, wx, ba, bx, lam):
    r = _sigmoid(_bdot(xc, wa) + ba)
    i = _sigmoid(_bdot(xc, wx) + bx)
    log_a = (-LRU_C) * r * _softplus(-lam)
    return jnp.exp(log_a), jnp.sqrt(_neg_expm1(2.0 * log_a)) * i * xc


def _block_scan(a, b, state, reverse):
    t = a.shape[0]
    row = _rows(a.shape) % SUBLANES
    s = 1
    while s < SUBLANES:
        shift, ok = (t - s, row < SUBLANES - s) if reverse else (s, row >= s)
        b = jnp.where(ok, a * pltpu.roll(b, shift, 0) + b, b)
        a = jnp.where(ok, a * pltpu.roll(a, shift, 0), a)
        s *= 2
    groups = range(t // SUBLANES)
    out = [None] * len(groups)
    for g in (reversed(groups) if reverse else groups):
        rows = slice(g * SUBLANES, (g + 1) * SUBLANES)
        out[g] = b[rows] + a[rows] * state
        state = out[g][0:1] if reverse else out[g][SUBLANES - 1:SUBLANES]
    return jnp.concatenate(out, axis=0)


def _lru_fwd(xc, proj, wa, wx, ba, bx, lam, *, name):
    s_len, w = xc.shape
    t = _tile(s_len, (LRU_T,))

    def body(xc_ref, gr_ref, wa_ref, wx_ref, ba_ref, bx_ref, lam_ref, h_ref, y_ref, carry):
        @pl.when(pl.program_id(0) == 0)
        def _():
            carry[...] = jnp.zeros_like(carry)

        a, b = _lru_gates(xc_ref[...], wa_ref[...], wx_ref[...], ba_ref[...], bx_ref[...], lam_ref[...])
        h = _block_scan(a, b, carry[0:1, :], False)
        h_ref[...] = h
        y_ref[...] = (h * _gelu(gr_ref[...])).astype(y_ref.dtype)
        carry[0:1, :] = h[t - 1:t, :]

    row = pl.BlockSpec((t, w), lambda i: (i, 0))
    mat = pl.BlockSpec((w, w), lambda i: (0, 0))
    vec = pl.BlockSpec((1, w), lambda i: (0, 0))
    return pl.pallas_call(
        body, name=name, grid=(s_len // t,),
        in_specs=[row, pl.BlockSpec((t, w), lambda i: (i, PROJ_AB_BLOCKS - 1)), mat, mat, vec, vec, vec],
        out_specs=[row, row], out_shape=[jax.ShapeDtypeStruct((s_len, w), F32), jax.ShapeDtypeStruct((s_len, w), MXU_DTYPE)],
        scratch_shapes=[pltpu.VMEM((SUBLANES, w), F32)], compiler_params=_params("arbitrary"),
    )(xc, proj, wa, wx, ba, bx, lam)


def _lru_bwd(xc, proj, hs, dcat, wa, wx, ba, bx, lam, *, name):
    s_len, w = xc.shape
    t = _tile(s_len, (LRU_T,))
    nb = s_len // t
    tpb = t // SUBLANES

    def body(xc_ref, gr_ref, h_ref, hp_ref, dy_ref, wa_ref, wx_ref, ba_ref, bx_ref, lam_ref,
             dxc_ref, dgr_ref, dwa_ref, dwx_ref, dba_ref, dbx_ref, dlam_ref, carry):
        step = pl.program_id(0)
        params = (wa_ref[...], wx_ref[...], ba_ref[...], bx_ref[...], lam_ref[...])

        @pl.when(step == 0)
        def _():
            carry[...] = jnp.zeros_like(carry)
            for r in (dwa_ref, dwx_ref, dba_ref, dbx_ref, dlam_ref):
                r[...] = jnp.zeros_like(r)

        (a, _), vjp = jax.vjp(_lru_gates, xc_ref[...], *params)
        gr, h, dy = gr_ref[...], h_ref[...], dy_ref[...]
        gel, gel_vjp = jax.vjp(_gelu, gr)
        dgr_ref[...] = gel_vjp(dy * h)[0].astype(dgr_ref.dtype)
        dh = dy * gel
        big_g = _block_scan(a, a * dh, carry[0:1, :], True)
        row = _rows((t, w))
        g = dh + jnp.where(row == t - 1, carry[0:1, :], pltpu.roll(big_g, t - 1, 0))
        carry[0:1, :] = big_g[0:1, :]
        h_last = jnp.where(step < nb - 1, hp_ref[SUBLANES - 1:SUBLANES, :], 0.0)
        h_prev = jnp.where(row == 0, h_last, pltpu.roll(h, 1, 0))
        dxc, dwa, dwx, dba, dbx, dlam = vjp((g * h_prev, g))
        dxc_ref[...] = dxc
        dwa_ref[...] += dwa
        dwx_ref[...] += dwx
        dba_ref[...] += dba
        dbx_ref[...] += dbx
        dlam_ref[...] += dlam

    rev = lambda i: nb - 1 - i
    row = pl.BlockSpec((t, w), lambda i: (rev(i), 0))
    mat = pl.BlockSpec((w, w), lambda i: (0, 0))
    vec = pl.BlockSpec((1, w), lambda i: (0, 0))
    return pl.pallas_call(
        body, name=name, grid=(nb,),
        in_specs=[row, pl.BlockSpec((t, w), lambda i: (rev(i), PROJ_AB_BLOCKS - 1)), row,
                  pl.BlockSpec((SUBLANES, w), lambda i: (jnp.maximum(rev(i) * tpb - 1, 0), 0)),
                  pl.BlockSpec((t, w), lambda i: (rev(i), 1)), mat, mat, vec, vec, vec],
        out_specs=[row, row, mat, mat, vec, vec, vec],
        out_shape=[jax.ShapeDtypeStruct((s_len, w), F32), jax.ShapeDtypeStruct((s_len, w), MXU_DTYPE)]
        + [jax.ShapeDtypeStruct((w, w), F32)] * 2 + [jax.ShapeDtypeStruct((1, w), F32)] * 3,
        scratch_shapes=[pltpu.VMEM((SUBLANES, w), F32)], compiler_params=_params("arbitrary"),
    )(xc, proj, hs, hs, dcat, wa, wx, ba, bx, lam)


XA_T = 256
XA_SCALE = XA_HEAD_DIM ** -0.5


def _xa_heads(q, k, v):
    s = _bmm_nt(q, k) * XA_SCALE
    e = jnp.exp(s - jnp.max(s, axis=-1, keepdims=True))
    return _bmm(e / jnp.sum(e, axis=-1, keepdims=True), v)


def _xa_stack(ref):
    return jnp.stack([ref[:, h * XA_HEAD_DIM:(h + 1) * XA_HEAD_DIM].astype(F32) for h in range(XA_HEADS)], axis=0)


def _xa_fwd(q, kv, *, name):
    s_len, d = q.shape
    n_mem = kv.shape[0]
    t = _tile(s_len, (XA_T,))

    def body(q_ref, k_ref, v_ref, o_ref):
        o = _xa_heads(_xa_stack(q_ref), _xa_stack(k_ref), _xa_stack(v_ref))
        for h in range(XA_HEADS):
            o_ref[:, h * XA_HEAD_DIM:(h + 1) * XA_HEAD_DIM] = o[h].astype(o_ref.dtype)

    return pl.pallas_call(
        body, name=name, grid=(s_len // t,),
        in_specs=[pl.BlockSpec((t, d), lambda i: (i, 0)), pl.BlockSpec((n_mem, d), lambda i: (0, 0)),
                  pl.BlockSpec((n_mem, d), lambda i: (0, 1))],
        out_specs=pl.BlockSpec((t, d), lambda i: (i, 0)), out_shape=jax.ShapeDtypeStruct((s_len, d), MXU_DTYPE),
        compiler_params=_params("parallel"),
    )(q, kv, kv)


def _xa_bwd(q, kv, dy, wo, *, name):
    s_len, d = q.shape
    n_mem = kv.shape[0]
    t = _tile(s_len, (XA_T,))

    def body(q_ref, k_ref, v_ref, dy_ref, wo_ref, dq_ref, dk_ref, dv_ref):
        @pl.when(pl.program_id(0) == 0)
        def _():
            dk_ref[...] = jnp.zeros_like(dk_ref)
            dv_ref[...] = jnp.zeros_like(dv_ref)

        do = _dg(dy_ref[...], wo_ref[...], ((1,), (1,)))
        do = jnp.stack([do[:, h * XA_HEAD_DIM:(h + 1) * XA_HEAD_DIM] for h in range(XA_HEADS)], axis=0)
        _, vjp = jax.vjp(_xa_heads, _xa_stack(q_ref), _xa_stack(k_ref), _xa_stack(v_ref))
        dq, dk, dv = vjp(do)
        for h in range(XA_HEADS):
            sl = slice(h * XA_HEAD_DIM, (h + 1) * XA_HEAD_DIM)
            dq_ref[:, sl] = dq[h].astype(dq_ref.dtype)
            dk_ref[:, sl] += dk[h]
            dv_ref[:, sl] += dv[h]

    row = pl.BlockSpec((t, d), lambda i: (i, 0))
    dq, dk, dv = pl.pallas_call(
        body, name=name, grid=(s_len // t,),
        in_specs=[row, pl.BlockSpec((n_mem, d), lambda i: (0, 0)), pl.BlockSpec((n_mem, d), lambda i: (0, 1)), row,
                  pl.BlockSpec(wo.shape, lambda i: (0, 0), pipeline_mode=RESIDENT)],
        out_specs=[row, pl.BlockSpec((n_mem, d), lambda i: (0, 0)), pl.BlockSpec((n_mem, d), lambda i: (0, 0))],
        out_shape=[jax.ShapeDtypeStruct((s_len, d), MXU_DTYPE)] + [jax.ShapeDtypeStruct((n_mem, d), F32)] * 2,
        compiler_params=_params("arbitrary"),
    )(q, kv, kv, dy, wo)
    return dq, jnp.concatenate([dk, dv], axis=1)


DN_Q_SCALE = DN_HEAD_DIM ** -0.5
L2_EPS = 1e-6


def _bdg(a, b, ca, cb):
    return lax.dot_general(a.astype(MXU_DTYPE), b.astype(MXU_DTYPE), (((ca,), (cb,)), ((0,), (0,))), preferred_element_type=F32)


@jax.custom_vjp
def _bmm(a, b):
    return _bdg(a, b, 2, 1)


_bmm.defvjp(lambda a, b: (_bdg(a, b, 2, 1), (a, b)), lambda r, g: (_bdg(g, r[1], 2, 2), _bdg(r[0], g, 1, 1)))


@jax.custom_vjp
def _bmm_nt(a, b):
    return _bdg(a, b, 2, 2)


_bmm_nt.defvjp(lambda a, b: (_bdg(a, b, 2, 2), (a, b)), lambda r, g: (_bdg(g, r[1], 2, 1), _bdg(g, r[0], 1, 1)))


@jax.custom_vjp
def _bmm_tn(a, b):
    return _bdg(a, b, 1, 1)


_bmm_tn.defvjp(lambda a, b: (_bdg(a, b, 1, 1), (a, b)), lambda r, g: (_bdg(r[1], g, 2, 2), _bdg(r[0], g, 2, 1)))


def _tri_inverse(n):
    eye = (lax.broadcasted_iota(jnp.int32, n.shape, 1) == lax.broadcasted_iota(jnp.int32, n.shape, 2)).astype(F32)
    inv, p = eye - n, n
    for _ in range(5):
        p = _bdg(p, p, 2, 1)
        inv = _bdg(inv, eye + p, 2, 1)
    return inv


@jax.custom_vjp
def _tri_solve2(n, r1, r2):
    t = _tri_inverse(n)
    return _bdg(t, r1, 2, 1), _bdg(t, r2, 2, 1)


def _tri_solve2_fwd(n, r1, r2):
    t = _tri_inverse(n)
    x1, x2 = _bdg(t, r1, 2, 1), _bdg(t, r2, 2, 1)
    return (x1, x2), (t, x1, x2)


def _tri_solve2_bwd(saved, cts):
    t, x1, x2 = saved
    d1, d2 = _bdg(t, cts[0], 1, 1), _bdg(t, cts[1], 1, 1)
    return -(_bdg(d1, x1, 2, 2) + _bdg(d2, x2, 2, 2)), d1, d2


_tri_solve2.defvjp(_tri_solve2_fwd, _tri_solve2_bwd)


def _dn_gates(ab, alog, dtb):
    return -jnp.exp(alog) * _softplus(ab + dtb), _sigmoid(ab)


def _dn_heads(cq, ck, cv, z, g, beta, onorm, state):
    h, c, _ = cq.shape
    l2 = lambda t: t * lax.rsqrt(jnp.sum(t * t, axis=-1, keepdims=True) + L2_EPS)
    q, k, v = l2(_silu(cq)) * DN_Q_SCALE, l2(_silu(ck)), _silu(cv)
    r, cc = lax.broadcasted_iota(jnp.int32, (h, c, c), 1), lax.broadcasted_iota(jnp.int32, (h, c, c), 2)
    tri, eye = r >= cc, r == cc
    g_row = jnp.sum(jnp.where(eye, g, 0.0), axis=1, keepdims=True)
    gcum_c = jnp.sum(jnp.where(tri, g_row, 0.0), axis=2, keepdims=True)
    gcum_r = jnp.sum(jnp.where(cc >= r, g, 0.0), axis=1, keepdims=True)
    decay = jnp.where(tri, jnp.exp(jnp.where(tri, gcum_c - gcum_r, 0.0)), 0.0)
    kb = k * beta
    n = jnp.where(r > cc, _bmm_nt(kb, k) * decay, 0.0)
    u, w = _tri_solve2(n, v * beta, kb * jnp.exp(gcum_c))
    v_new = u - _bmm(w, state)
    o = _bmm(q * jnp.exp(gcum_c), state) + _bmm(_bmm_nt(q, k) * decay, v_new)
    g_last = jnp.sum(g, axis=1, keepdims=True)
    new_state = state * jnp.exp(g_last) + _bmm_tn(k * jnp.exp(g_last - gcum_c), v_new)
    on = o * lax.rsqrt(jnp.mean(o * o, axis=-1, keepdims=True) + NORM_EPS) * onorm
    return on * _silu(z), new_state


def _dn_stack(ref, col0):
    return jnp.stack([ref[:, col0 + h * DN_HEAD_DIM:col0 + (h + 1) * DN_HEAD_DIM].astype(F32) for h in range(DN_HEADS)], axis=0)


def _dn_cols(block, col0):
    return jnp.stack([block[:, col0 + h:col0 + h + 1] for h in range(DN_HEADS)], axis=0)


def _dn_fwd(cqkv, proj, ab, alog, dtb, onorm, *, name):
    s_len = cqkv.shape[0]
    c, hd, w = DN_CHUNK, DN_HEAD_DIM, DN_WIDTH
    n_chunks = s_len // c

    def body(c_ref, z_ref, ab_ref, alog_ref, dtb_ref, on_ref, o_ref, st_ref, state):
        @pl.when(pl.program_id(0) == 0)
        def _():
            state[...] = jnp.zeros_like(state)

        g_all, beta_all = _dn_gates(ab_ref[...], alog_ref[...], dtb_ref[...])
        st = state[...]
        st_ref[0] = st
        out, new = _dn_heads(_dn_stack(c_ref, 0), _dn_stack(c_ref, w), _dn_stack(c_ref, 2 * w), _dn_stack(z_ref, 0),
                             _dn_cols(g_all, 0), _dn_cols(beta_all, DN_HEADS), on_ref[...], st)
        state[...] = new
        for h in range(DN_HEADS):
            o_ref[:, h * hd:(h + 1) * hd] = out[h].astype(o_ref.dtype)

    vec = pl.BlockSpec((1, LANES), lambda i: (0, 0))
    return pl.pallas_call(
        body, name=name, grid=(n_chunks,),
        in_specs=[pl.BlockSpec((c, 3 * w), lambda i: (i, 0)), pl.BlockSpec((c, w), lambda i: (i, 3)),
                  pl.BlockSpec((c, LANES), lambda i: (i, 0)), vec, vec, vec],
        out_specs=[pl.BlockSpec((c, w), lambda i: (i, 0)), pl.BlockSpec((1, DN_HEADS, hd, hd), lambda i: (i, 0, 0, 0))],
        out_shape=[jax.ShapeDtypeStruct((s_len, w), MXU_DTYPE), jax.ShapeDtypeStruct((n_chunks, DN_HEADS, hd, hd), F32)],
        scratch_shapes=[pltpu.VMEM((DN_HEADS, hd, hd), F32)], compiler_params=_params("arbitrary"),
    )(cqkv, proj, ab, alog, dtb, onorm)


def _dn_bwd(cqkv, proj, ab, alog, dtb, onorm, states, dout, *, name):
    s_len = cqkv.shape[0]
    c, hd, w = DN_CHUNK, DN_HEAD_DIM, DN_WIDTH
    n_chunks = s_len // c

    def body(c_ref, z_ref, ab_ref, alog_ref, dtb_ref, on_ref, st_ref, do_ref,
             dc_ref, dz_ref, dab_ref, dalog_ref, ddtb_ref, don_ref, dstate):
        @pl.when(pl.program_id(0) == 0)
        def _():
            dstate[...] = jnp.zeros_like(dstate)
            for r in (dalog_ref, ddtb_ref, don_ref):
                r[...] = jnp.zeros_like(r)

        (g_all, beta_all), gates_vjp = jax.vjp(_dn_gates, ab_ref[...], alog_ref[...], dtb_ref[...])
        _, vjp = jax.vjp(_dn_heads, _dn_stack(c_ref, 0), _dn_stack(c_ref, w), _dn_stack(c_ref, 2 * w), _dn_stack(z_ref, 0),
                         _dn_cols(g_all, 0), _dn_cols(beta_all, DN_HEADS), on_ref[...], st_ref[0])
        dcq, dck, dcv, dz, dg, dbeta, don, dst = vjp((_dn_stack(do_ref, 0), dstate[...]))
        dstate[...] = dst
        col = _cols((c, LANES))
        dg_all, dbeta_all = jnp.zeros((c, LANES), F32), jnp.zeros((c, LANES), F32)
        for h in range(DN_HEADS):
            sl = slice(h * hd, (h + 1) * hd)
            dc_ref[:, sl] = dcq[h]
            dc_ref[:, w + h * hd:w + (h + 1) * hd] = dck[h]
            dc_ref[:, 2 * w + h * hd:2 * w + (h + 1) * hd] = dcv[h]
            dz_ref[:, sl] = dz[h].astype(dz_ref.dtype)
            dg_all = dg_all + jnp.where(col == h, dg[h], 0.0)
            dbeta_all = dbeta_all + jnp.where(col == DN_HEADS + h, dbeta[h], 0.0)
        dab, dalog, ddtb = gates_vjp((dg_all, dbeta_all))
        dab_ref[...] = dab
        dalog_ref[...] += dalog
        ddtb_ref[...] += ddtb
        don_ref[...] += don

    rev = lambda i: n_chunks - 1 - i
    vec = pl.BlockSpec((1, LANES), lambda i: (0, 0))
    return pl.pallas_call(
        body, name=name, grid=(n_chunks,),
        in_specs=[pl.BlockSpec((c, 3 * w), lambda i: (rev(i), 0)), pl.BlockSpec((c, w), lambda i: (rev(i), 3)),
                  pl.BlockSpec((c, LANES), lambda i: (rev(i), 0)), vec, vec, vec,
                  pl.BlockSpec((1, DN_HEADS, hd, hd), lambda i: (rev(i), 0, 0, 0)), pl.BlockSpec((c, w), lambda i: (rev(i), 0))],
        out_specs=[pl.BlockSpec((c, 3 * w), lambda i: (rev(i), 0)), pl.BlockSpec((c, w), lambda i: (rev(i), 0)),
                   pl.BlockSpec((c, LANES), lambda i: (rev(i), 0)), vec, vec, vec],
        out_shape=[jax.ShapeDtypeStruct((s_len, 3 * w), F32), jax.ShapeDtypeStruct((s_len, w), MXU_DTYPE),
                   jax.ShapeDtypeStruct((s_len, LANES), F32)] + [jax.ShapeDtypeStruct((1, LANES), F32)] * 3,
        scratch_shapes=[pltpu.VMEM((DN_HEADS, hd, hd), F32)], compiler_params=_params("arbitrary"),
    )(cqkv, proj, ab, alog, dtb, onorm, states, dout)


def _final_loss(x, g, target, *, name):
    s, d = x.shape
    tm = _tile(s, (512, 256))

    def body(x_ref, g_ref, t_ref, loss_ref, dx_ref, dg_ref):
        @pl.when(pl.program_id(0) == 0)
        def _():
            loss_ref[...] = jnp.zeros_like(loss_ref)
            dg_ref[...] = jnp.zeros_like(dg_ref)

        xv, gv = x_ref[...], g_ref[...]
        r = lax.rsqrt(jnp.mean(xv * xv, axis=-1, keepdims=True) + NORM_EPS)
        xh = xv * r
        err = xh * gv - t_ref[...]
        loss_ref[...] += 0.5 * jnp.sum(jnp.mean(err * err, axis=-1, keepdims=True), axis=0, keepdims=True)
        dy = err * (1.0 / d)
        dxh = dy * gv
        dx_ref[...] = r * (dxh - xh * jnp.mean(dxh * xh, axis=-1, keepdims=True))
        dg_ref[...] += jnp.sum(dy * xh, axis=0, keepdims=True)

    row = pl.BlockSpec((tm, d), lambda i: (i, 0))
    vec = pl.BlockSpec((1, d), lambda i: (0, 0))
    return pl.pallas_call(
        body, name=name, grid=(s // tm,), in_specs=[row, vec, row],
        out_specs=[pl.BlockSpec((1, LANES), lambda i: (0, 0)), row, vec],
        out_shape=[jax.ShapeDtypeStruct((1, LANES), F32), jax.ShapeDtypeStruct((s, d), F32), jax.ShapeDtypeStruct((1, d), F32)],
        compiler_params=_params("arbitrary"),
    )(x, g.reshape(1, d), target)


def _adamw(w, g_layers, m, v, *, name):
    shape = w.shape
    cols = shape[-1]
    n_l = len(g_layers)
    rows = max(w.size // cols, 1) // n_l
    tr = _tile(rows, [t for t in (512, 352, 256, 128, 64, 32, 16, 8) if t * cols * 4 <= ADAMW_BLOCK_BYTES])
    c1, c2 = 1.0 - ADAM_B1 ** ADAM_STEP, 1.0 - ADAM_B2 ** ADAM_STEP

    def body(*refs):
        w_ref, m_ref, v_ref = refs[:3]
        d_ref, nm_ref, nv_ref, g_ref = refs[3 + n_l:]
        gv = refs[3][...]
        for k in range(1, n_l):
            gv = jnp.where(pl.program_id(0) == k, refs[3 + k][...], gv)
        nm = ADAM_B1 * m_ref[...] + (1.0 - ADAM_B1) * gv
        nv = ADAM_B2 * v_ref[...] + (1.0 - ADAM_B2) * (gv * gv)
        d_ref[...] = -ADAM_LR * ((nm / c1) / (jnp.sqrt(nv / c2) + ADAM_EPS) + ADAM_WD * w_ref[...])
        nm_ref[...] = nm
        nv_ref[...] = nv
        g_ref[...] = gv

    blk = pl.BlockSpec((None, tr, cols), lambda l, i: (l, i, 0))
    slab = pl.BlockSpec((tr, cols), lambda l, i: (i, 0))
    outs = pl.pallas_call(
        body, name=name, grid=(n_l, rows // tr), in_specs=[blk] * 3 + [slab] * n_l, out_specs=[blk] * 4,
        out_shape=[jax.ShapeDtypeStruct((n_l, rows, cols), F32)] * 4, compiler_params=_params("parallel", "parallel"),
    )(*(t.reshape(n_l, rows, cols) for t in (w, m, v)), *(t.reshape(rows, cols) for t in g_layers))
    return tuple(t.reshape(shape) for t in outs)


def _block_diag(w):
    n, j, k = w.shape
    eye = jnp.eye(n, dtype=w.dtype)
    return (eye[:, None, :, None] * w[:, :, None, :]).reshape(n * j, n * k)


def _block_diag_part(m, n):
    j, k = m.shape[0] // n, m.shape[1] // n
    m4 = m.reshape(n, j, n, k)
    return jnp.stack([m4[i, :, i, :] for i in range(n)], axis=0)


DN_AB = 2 * DN_HEADS
DEPTH = 2


def _row(v, width=None):
    v = v.reshape(1, -1)
    return v if width is None else jnp.pad(v, ((0, 0), (0, width - v.shape[1])))


def _conv_w8(conv_w, bias=None):
    w8 = jnp.zeros((SUBLANES, conv_w.shape[1]), F32).at[:CONV_K].set(conv_w)
    return w8 if bias is None else w8.at[CONV_K].set(bias)


def _mixer_ab_fwd(x, w, tag):
    h, (proj,) = _norm_mm(x, w["mix_norm"][0], [w["ab_w_in"][0]], [F32], name=f"{tag}_in")
    o, lse = _dattn_forward(proj, tag)
    w8 = _conv_w8(w["lru_conv_w"][0], w["lru_conv_b"][0])
    xc = _conv_fwd(proj, PROJ_AB_BLOCKS - 2, LRU_WIDTH, w8, name=f"{tag}_conv")
    wa, wx = _block_diag(w["lru_w_a"][0]), _block_diag(w["lru_w_x"][0])
    vecs = (_row(w["lru_b_a"][0]), _row(w["lru_b_x"][0]), _row(w["lru_lambda"][0]))
    hs, y = _lru_fwd(xc, proj, wa, wx, *vecs, name=f"{tag}_lru")
    w_out = w["ab_w_out"][0]
    x2 = _mm(o, w_out[:ATTN_WIDTH], res=x, name=f"{tag}_out_attn")
    x2 = _mm(y, w_out[ATTN_WIDTH:], res=x2, name=f"{tag}_out_lru")
    return x2, (x, h, proj, o, lse, w8, xc, wa, wx, vecs, hs, y)


def _mixer_ab_bwd(saved, w, dy, tag):
    x, h, proj, o, lse, w8, xc, wa, wx, vecs, hs, y = saved
    w_out = w["ab_w_out"][0]
    dcat = _mm(dy, w_out, mode="nt", name=f"{tag}_dcat")
    dw_out = jnp.concatenate([_mm(o, dy, mode="tn", name=f"{tag}_dwout_attn"), _mm(y, dy, mode="tn", name=f"{tag}_dwout_lru")], axis=0)
    dq, dk, dv = _dattn_backward(proj, o, lse, dcat, tag)
    dxc, dgr, dwa, dwx, dba, dbx, dlam = _lru_bwd(xc, proj, hs, dcat, wa, wx, *vecs, name=f"{tag}_dlru")
    dxr, dw8 = _conv_bwd(proj, PROJ_AB_BLOCKS - 2, LRU_WIDTH, w8, dxc, name=f"{tag}_dconv")
    dproj = jnp.concatenate([t.astype(MXU_DTYPE) for t in (dq, dk, dv, dxr, dgr)], axis=1)
    dw_in = _mm(h, dproj, mode="tn", name=f"{tag}_dwin")
    dx, dg = _mm_rms_bwd([(dproj, w["ab_w_in"][0])], x, w["mix_norm"][0], dy, name=f"{tag}_dh")
    grads = dict(mix_norm=dg[0], ab_w_in=dw_in, ab_w_out=dw_out, lru_conv_w=dw8[:CONV_K], lru_conv_b=dw8[CONV_K],
                 lru_w_a=_block_diag_part(dwa, LRU_BLOCKS), lru_b_a=dba[0], lru_w_x=_block_diag_part(dwx, LRU_BLOCKS),
                 lru_b_x=dbx[0], lru_lambda=dlam[0])
    return dx, grads


def _dn_split_w(w_in):
    return w_in[:, :4 * DN_WIDTH], jnp.pad(w_in[:, 4 * DN_WIDTH:], ((0, 0), (0, LANES - DN_AB)))


def _mixer_dn_fwd(x, w, tag):
    w_qkvz, w_ab = _dn_split_w(w["dn_w_in"][0])
    h, (proj, ab) = _norm_mm(x, w["mix_norm"][1], [w_qkvz, w_ab], [F32, F32], name=f"{tag}_in")
    w8 = _conv_w8(w["dn_conv_w"][0])
    cqkv = _conv_fwd(proj, 0, 3 * DN_WIDTH, w8, name=f"{tag}_conv")
    vecs = (_row(w["dn_a_log"][0], LANES), _row(w["dn_dt_bias"][0], LANES), _row(w["dn_o_norm"][0]))
    og, states = _dn_fwd(cqkv, proj, ab, *vecs, name=f"{tag}_dn")
    x2 = _mm(og, w["dn_w_out"][0], res=x, name=f"{tag}_out")
    return x2, (x, h, w_qkvz, w_ab, proj, ab, w8, cqkv, vecs, og, states)


def _mixer_dn_bwd(saved, w, dy, tag):
    x, h, w_qkvz, w_ab, proj, ab, w8, cqkv, vecs, og, states = saved
    dout = _mm(dy, w["dn_w_out"][0], mode="nt", name=f"{tag}_dout")
    dw_out = _mm(og, dy, mode="tn", name=f"{tag}_dwout")
    dcqkv, dz, dab, dalog, ddtb, don = _dn_bwd(cqkv, proj, ab, *vecs, states, dout, name=f"{tag}_ddn")
    dqkv, dw8 = _conv_bwd(proj, 0, 3 * DN_WIDTH, w8, dcqkv, name=f"{tag}_dconv")
    dproj = jnp.concatenate([dqkv.astype(MXU_DTYPE), dz.astype(MXU_DTYPE)], axis=1)
    dw_in = jnp.concatenate([_mm(h, dproj, mode="tn", name=f"{tag}_dwin"),
                             _mm(h, dab, mode="tn", name=f"{tag}_dwin_ab")[:, :DN_AB]], axis=1)
    dx, dg = _mm_rms_bwd([(dproj, w_qkvz), (dab, w_ab)], x, w["mix_norm"][1], dy, name=f"{tag}_dh")
    grads = dict(mix_norm=dg[0], dn_w_in=dw_in, dn_w_out=dw_out, dn_conv_w=dw8[:CONV_K], dn_a_log=dalog[0, :DN_HEADS],
                 dn_dt_bias=ddtb[0, :DN_HEADS], dn_o_norm=don[0])
    return dx, grads


def _xa_layer_fwd(x, mem, w, layer, tag):
    hq, (q,) = _norm_mm(x, w["xa_norm"][layer], [_layer_matrix(w["xa_wq"], layer)], [MXU_DTYPE], name=f"{tag}_q")
    hm = _rms_fwd(mem, w["xa_mem_norm"][layer], name=f"{tag}_mem_norm")
    kv = _mm(hm, _layer_matrix(w["xa_wkv"], layer), name=f"{tag}_kv")
    oa = _xa_fwd(q, kv, name=f"{tag}_core")
    x2 = _mm(oa, _layer_matrix(w["xa_wo"], layer), res=x, name=f"{tag}_out")
    return x2, (x, hq, q, hm, kv, oa)


def _xa_layer_bwd(saved, mem, w, layer, dy, tag):
    x, hq, q, hm, kv, oa = saved
    dwo = _mm(oa, dy, mode="tn", name=f"{tag}_dwo")
    dq, dkv = _xa_bwd(q, kv, dy, _layer_matrix(w["xa_wo"], layer), name=f"{tag}_dcore")
    dwq = _mm(hq, dq, mode="tn", name=f"{tag}_dwq")
    dx, dg = _mm_rms_bwd([(dq, _layer_matrix(w["xa_wq"], layer))], x, w["xa_norm"][layer], dy, name=f"{tag}_dhq")
    dwkv = _mm(hm, dkv, mode="tn", name=f"{tag}_dwkv")
    dhm = _mm(dkv, _layer_matrix(w["xa_wkv"], layer), mode="nt", name=f"{tag}_dhm")
    _, dgm = _rms_bwd(mem, w["xa_mem_norm"][layer], dhm, jnp.zeros_like(mem), name=f"{tag}_dmem_norm")
    return dx, dict(xa_norm=dg[0], xa_mem_norm=dgm[0], xa_wq=dwq, xa_wkv=dwkv, xa_wo=dwo)


def _local_step(x, mem, target, w, pending=None, reduce_big=False):
    saved = []
    plans, shards = pending if pending else ([], None)
    hosts = {("ffn1", 0): plans[0], ("ffn2", 0): plans[1]} if plans else {}

    def ffn(which, layer, x):
        plan = hosts.get((which, layer))
        hosted = ([b for *_, b in plan], [a for _, _, a, _ in plan]) if plan else None
        x, s, gathered = _ffn_fwd(x, w[f"{which}_norm"], w[f"{which}_w_in"], w[f"{which}_w_out"], layer, f"l{layer}_{which}", gather=hosted)
        if plan:
            _gathered(plan, gathered, shards, w)
        return x, s

    for layer in range(DEPTH):
        t = f"l{layer}"
        x, s1 = ffn("ffn1", layer, x)
        x, s2 = (_mixer_ab_fwd if layer % 2 == 0 else _mixer_dn_fwd)(x, w, f"{t}_mix")
        x, s3 = _xa_layer_fwd(x, mem, w, layer, f"{t}_xa")
        x, s4 = ffn("ffn2", layer, x)
        saved.append((s1, s2, s3, s4))
    loss, dx, dgf = _final_loss(x, w["final_norm"], target, name="final_loss")
    per_layer, reduced = [None] * DEPTH, [None] * DEPTH
    travelling = None
    for layer in reversed(range(DEPTH)):
        t = f"l{layer}"
        s1, s2, s3, s4 = saved[layer]
        g = {}
        dx, g["ffn2_norm"], g["ffn2_w_in"], g["ffn2_w_out"], parts = _ffn_bwd(
            s4, w["ffn2_norm"], w["ffn2_w_in"], w["ffn2_w_out"], layer, dx, f"{t}_ffn2",
            exchange=travelling[1] if travelling else None)
        if travelling:
            reduced[travelling[0]] = _reduce_end(travelling[1], parts, f"l{travelling[0]}")
        dx, gx = _xa_layer_bwd(s3, mem, w, layer, dx, f"{t}_xa")
        dx, gm = (_mixer_ab_bwd if layer % 2 == 0 else _mixer_dn_bwd)(s2, w, dx, f"{t}_mix")
        dx, g["ffn1_norm"], g["ffn1_w_in"], g["ffn1_w_out"], _ = _ffn_bwd(
            s1, w["ffn1_norm"], w["ffn1_w_in"], w["ffn1_w_out"], layer, dx, f"{t}_ffn1")
        per_layer[layer] = {**g, **gx, **gm}
        if reduce_big:
            chip_sum = _reduce_begin(_pack_grads(per_layer[layer]), t)
            if layer > 0:
                travelling = (layer, chip_sum)
            else:
                reduced[layer] = _reduce_end(chip_sum, _exchange_chips(chip_sum), t)
    grads = {"final_norm": [dgf[0]]}
    for layer_grads in per_layer:
        for name, value in layer_grads.items():
            grads.setdefault(name, []).append(value)
    return loss, dx, grads, list(zip(reduced, per_layer))


N_CHIPS = 4
WIRE_DTYPE = jnp.bfloat16
HBM_SPEC = pl.BlockSpec(memory_space=pltpu.HBM)
PACK_COLS = 1024


def _place():
    x, y, c = lax.axis_index("x"), lax.axis_index("y"), lax.axis_index("c")
    return x, y, c, [(1 - x, y), (x, 1 - y), (1 - x, 1 - y)]


def _remote(src, dst, sems, k, to):
    return pltpu.make_async_remote_copy(src_ref=src, dst_ref=dst, send_sem=sems[0].at[k], recv_sem=sems[1].at[k],
                                        device_id=to, device_id_type=MESH)


def _gather_weights(blocks, axes):
    n = len(blocks)
    out_shapes, sem_shapes, start, finish = _gather_plan(blocks, axes)

    def body(*refs):
        start(refs[:n], refs[n:2 * n], *refs[2 * n:])
        finish(refs[:n], refs[n:2 * n], *refs[2 * n:])

    return pl.pallas_call(
        body, name="gather_weights", in_specs=[HBM_SPEC] * n, out_specs=[HBM_SPEC] * n,
        out_shape=out_shapes, scratch_shapes=sem_shapes,
    )(*blocks)


def _gather_plan(blocks, axes):
    n = len(blocks)
    split = [b.shape[1] % 32 == 0 for b in blocks]

    def full_shape(i):
        l, r, c = blocks[i].shape
        return (l, N_CHIPS * r, c) if axes[i] == 1 else (l, r, N_CHIPS * c)

    def copies(ins, outs, send_sems, recv_sems):
        x, y, c, chips = _place()
        sems = (send_sems, recv_sems)
        sibling = (x, y, 1 - c)
        me = 2 * x + y

        def window(i, k, h):
            l, r, cc = blocks[i].shape
            r0, nr = (0, r) if h is None else (h * (r // 2), r // 2)
            if axes[i] == 1:
                return outs[i].at[:, pl.ds(k * r + r0, nr), :]
            return outs[i].at[:, pl.ds(r0, nr), pl.ds(k * cc, cc)]

        def mine(i, h):
            r = blocks[i].shape[1]
            return ins[i] if h is None else ins[i].at[:, pl.ds(h * (r // 2), r // 2), :]

        half = lambda i: c if split[i] else None
        first = [_remote(mine(i, half(i)), window(i, me, half(i)), sems, 3 * i + j, (*chip, c))
                 for i in range(n) for j, chip in enumerate(chips)]
        first += [_remote(ins[i], window(i, me, None), sems, 6 * n + i, sibling) for i in range(n)]
        arrive = lambda i, j, h, k, frm: _remote(window(i, 2 * chips[j][0] + chips[j][1], h), window(i, 2 * chips[j][0] + chips[j][1], h),
                                                 sems, k, frm)
        return first, arrive, chips, c, sibling

    def start(ins, outs, send_sems, recv_sems):
        for cp in copies(ins, outs, send_sems, recv_sems)[0]:
            cp.start()

    def finish(ins, outs, send_sems, recv_sems):
        first, arrive, chips, c, sibling = copies(ins, outs, send_sems, recv_sems)
        passed = []
        for i in range(n):
            for j, (cx, cy) in enumerate(chips):
                arrive(i, j, c if split[i] else None, 3 * i + j, (cx, cy, c)).wait_recv()
                if split[i]:
                    passed.append(arrive(i, j, c, 3 * (n + i) + j, sibling))
                    passed[-1].start()
        for i in range(n):
            if split[i]:
                for j in range(len(chips)):
                    arrive(i, j, 1 - c, 3 * (n + i) + j, sibling).wait_recv()
        for cp in first[3 * n:]:
            cp.wait_recv()
        for cp in first + passed:
            cp.wait_send()

    sem_shapes = [pltpu.SemaphoreType.DMA((7 * n,)), pltpu.SemaphoreType.DMA((7 * n,))]
    return [jax.ShapeDtypeStruct(full_shape(i), blocks[i].dtype) for i in range(n)], sem_shapes, start, finish


def _allreduce_small(v):
    rows, cols = v.shape
    n_dev = 2 * N_CHIPS

    def body(v_ref, out_ref, all_ref, send_sems, recv_sems, local_sem):
        x, y, c, chips = _place()
        sems = (send_sems, recv_sems)
        me, sibling = (x, y, c), (x, y, 1 - c)
        slot = lambda px, py, pc: all_ref.at[pl.ds((4 * px + 2 * py + pc) * rows, rows), :]
        mine = pltpu.make_async_copy(v_ref, slot(*me), local_sem)
        mine.start()
        first = [_remote(v_ref, slot(*me), sems, 0, sibling)]
        first += [_remote(v_ref, slot(*me), sems, 1 + j, (*chip, c)) for j, chip in enumerate(chips)]
        for cp in first:
            cp.start()
        passed = [_remote(slot(*chip, c), slot(*chip, c), sems, 4 + j, sibling) for j, chip in enumerate(chips)]
        for j, chip in enumerate(chips):
            _remote(slot(*chip, c), slot(*chip, c), sems, 1 + j, me).wait_recv()
            passed[j].start()
        _remote(slot(*sibling), slot(*sibling), sems, 0, me).wait_recv()
        for j, chip in enumerate(chips):
            _remote(slot(*chip, 1 - c), slot(*chip, 1 - c), sems, 4 + j, me).wait_recv()
        for cp in first + passed:
            cp.wait_send()
        mine.wait()
        acc = all_ref[pl.ds(0, rows), :]
        for k in range(1, n_dev):
            acc = acc + all_ref[pl.ds(k * rows, rows), :]
        out_ref[...] = acc

    vmem = pl.BlockSpec(memory_space=pltpu.VMEM)
    return pl.pallas_call(
        body, name="allreduce_small", in_specs=[vmem], out_specs=vmem, out_shape=jax.ShapeDtypeStruct((rows, cols), F32),
        scratch_shapes=[pltpu.VMEM((n_dev * rows, cols), F32), pltpu.SemaphoreType.DMA((7,)), pltpu.SemaphoreType.DMA((7,)),
                        pltpu.SemaphoreType.DMA],
    )(v)


def _swap_other_half(g4, tag):
    n, _, rows, cols = g4.shape

    def body(v_ref, out_ref, send_sems, recv_sems):
        x, y, c, _ = _place()
        cp = _remote(v_ref.at[:, 1 - c], out_ref, (send_sems, recv_sems), 0, (x, y, 1 - c))
        cp.start()
        cp.wait()

    return pl.pallas_call(
        body, name=f"{tag}_reduce_swap", in_specs=[HBM_SPEC], out_specs=HBM_SPEC, out_shape=jax.ShapeDtypeStruct((n, rows, cols), g4.dtype),
        scratch_shapes=[pltpu.SemaphoreType.DMA((1,)), pltpu.SemaphoreType.DMA((1,))],
    )(g4)


def _add_kept_half(g4, got, tag):
    n, _, rows, cols = g4.shape
    tr = _tile(rows, (256, 128, 64, 32, 16))
    nb = rows // tr

    def body(c_ref, a_ref, b_ref, o_ref):
        o_ref[...] = (a_ref[...] + b_ref[...]).astype(o_ref.dtype)

    return pl.pallas_call(
        body, name=f"{tag}_reduce_sum_cores",
        grid_spec=pltpu.PrefetchScalarGridSpec(
            num_scalar_prefetch=1, grid=(n, nb),
            in_specs=[pl.BlockSpec((None, None, tr, cols), lambda k, i, c_ref: (k, c_ref[0], i, 0)),
                      pl.BlockSpec((None, tr, cols), lambda k, i, c_ref: (k, i, 0))],
            out_specs=pl.BlockSpec((None, tr, cols), lambda k, i, c_ref: (k, i, 0))),
        out_shape=jax.ShapeDtypeStruct((n, rows, cols), WIRE_DTYPE), compiler_params=_params("parallel", "parallel"),
    )(lax.axis_index("c").astype(jnp.int32).reshape(1), g4, got)


def _exchange_plan(v):
    def copies(v_ref, out_ref, send_sems, recv_sems):
        x, y, c, chips = _place()
        return [_remote(v_ref.at[2 * cx + cy], out_ref.at[j], (send_sems, recv_sems), j, (cx, cy, c)) for j, (cx, cy) in enumerate(chips)]

    def start(*refs):
        for cp in copies(*refs):
            cp.start()

    def finish(*refs):
        for cp in copies(*refs):
            cp.wait_recv()
        for cp in copies(*refs):
            cp.wait_send()

    sem_shapes = [pltpu.SemaphoreType.DMA((N_CHIPS - 1,)), pltpu.SemaphoreType.DMA((N_CHIPS - 1,))]
    return jax.ShapeDtypeStruct((N_CHIPS - 1,) + v.shape[1:], v.dtype), sem_shapes, start, finish


def _exchange_chips(v):
    out_shape, sem_shapes, start, finish = _exchange_plan(v)

    def body(*refs):
        start(*refs)
        finish(*refs)

    return pl.pallas_call(body, name="exchange_chips", in_specs=[HBM_SPEC], out_specs=HBM_SPEC, out_shape=out_shape,
                          scratch_shapes=sem_shapes)(v)


def _swap_sibling(v, tag):
    def body(v_ref, out_ref, send_sems, recv_sems):
        x, y, c, _ = _place()
        cp = _remote(v_ref, out_ref, (send_sems, recv_sems), 0, (x, y, 1 - c))
        cp.start()
        cp.wait()

    return pl.pallas_call(
        body, name=f"{tag}_share_halves", in_specs=[HBM_SPEC], out_specs=HBM_SPEC, out_shape=jax.ShapeDtypeStruct(v.shape, v.dtype),
        scratch_shapes=[pltpu.SemaphoreType.DMA((1,)), pltpu.SemaphoreType.DMA((1,))],
    )(v)


def _sum_chips(own4, parts, tag):
    _, rows, cols = own4.shape
    tr = _tile(rows, (256, 128, 64, 32, 16))

    def body(me_ref, own_ref, p0_ref, p1_ref, p2_ref, o_ref):
        acc = own_ref[...].astype(F32)
        for r in (p0_ref, p1_ref, p2_ref):
            acc = acc + r[...].astype(F32)
        o_ref[...] = acc

    part = lambda j: pl.BlockSpec((None, tr, cols), lambda i, me_ref: (j, i, 0))
    chip = (2 * lax.axis_index("x") + lax.axis_index("y")).astype(jnp.int32).reshape(1)
    return pl.pallas_call(
        body, name=f"{tag}_reduce_sum_chips",
        grid_spec=pltpu.PrefetchScalarGridSpec(
            num_scalar_prefetch=1, grid=(rows // tr,),
            in_specs=[pl.BlockSpec((None, tr, cols), lambda i, me_ref: (me_ref[0], i, 0)), part(0), part(1), part(2)],
            out_specs=pl.BlockSpec((tr, cols), lambda i, me_ref: (i, 0))),
        out_shape=jax.ShapeDtypeStruct((rows, cols), F32), compiler_params=_params("parallel"),
    )(chip, own4, parts, parts, parts)


def _reduce_begin(g4, tag):
    return _add_kept_half(g4, _swap_other_half(g4, tag), tag)


def _reduce_end(chip_sum, parts, tag):
    half = _sum_chips(chip_sum, parts, tag)
    other = _swap_sibling(half, tag)
    return jnp.where(lax.axis_index("c") == 0, jnp.stack([half, other]), jnp.stack([other, half]))


BIG = (("ffn1_w_in", 2), ("ffn1_w_out", 1), ("xa_wq", 1), ("xa_wkv", 2), ("xa_wo", 1), ("ffn2_w_in", 2), ("ffn2_w_out", 1),
       ("ab_w_in", 2), ("ab_w_out", 1), ("dn_w_in", 2), ("dn_w_out", 1))
TINY_SHARDED = (("lru_conv_w", 2), ("dn_conv_w", 2))
REPLICATED = ("ffn1_norm", "mix_norm", "xa_norm", "xa_mem_norm", "ffn2_norm", "lru_conv_b", "lru_w_a", "lru_b_a", "lru_w_x",
              "lru_b_x", "lru_lambda", "dn_a_log", "dn_dt_bias", "dn_o_norm", "final_norm")
WEIGHTS = ("ffn1_norm", "ffn1_w_in", "ffn1_w_out", "mix_norm", "xa_norm", "xa_mem_norm", "xa_wq", "xa_wkv", "xa_wo", "ffn2_norm",
           "ffn2_w_in", "ffn2_w_out", "ab_w_in", "lru_conv_w", "lru_conv_b", "lru_w_a", "lru_b_a", "lru_w_x", "lru_b_x",
           "lru_lambda", "ab_w_out", "dn_w_in", "dn_conv_w", "dn_a_log", "dn_dt_bias", "dn_o_norm", "dn_w_out", "final_norm")


def _lane_padded(shape):
    return shape[:-1] + (-(-shape[-1] // LANES) * LANES,)


def _pad_lanes(t):
    return jnp.pad(t, [(0, 0)] * (t.ndim - 1) + [(0, _lane_padded(t.shape)[-1] - t.shape[-1])])


FIRST_USED = ("ffn1_w_in", "ffn1_w_out")
LAYER_1_ONLY = ("dn_w_in", "dn_w_out", "dn_conv_w")


def _gather_blocks(shards):
    groups = [], [], []
    for n, a in BIG + TINY_SHARDED:
        block = _pad_lanes(shards[n]).astype(MXU_DTYPE) if (n, a) in BIG else shards[n]
        if block.shape[0] == 1:
            groups[2 if n in LAYER_1_ONLY else 1].append((n, None, a, block))
        else:
            for layer in range(block.shape[0]):
                group = 2 if layer > 0 else 0 if n in FIRST_USED else 1
                groups[group].append((n, layer, a, block[layer:layer + 1]))
    return groups


def _gathered(plan, arrays, shards, into):
    for (n, layer, axis, _), full in zip(plan, arrays):
        width, padded = shards[n].shape[-1], _lane_padded(shards[n].shape)[-1]
        if padded != width:
            assert axis == 2
            full = jnp.concatenate([full[..., k * padded:k * padded + width] for k in range(N_CHIPS)], axis=-1)
        if layer is None:
            into[n] = full
        else:
            into.setdefault(n, [None] * shards[n].shape[0])[layer] = full
    return into


def _pack_parts(cols):
    whole = cols // PACK_COLS * PACK_COLS
    return [(c0, PACK_COLS) for c0 in range(0, whole, PACK_COLS)] + ([(whole, cols - whole)] if cols > whole else [])


def _to_rows(block):
    block = _pad_lanes(block)
    return jnp.concatenate([block[:, c0:c0 + n].reshape(-1, PACK_COLS) for c0, n in _pack_parts(block.shape[1])], axis=0)


def _from_rows(rows, r, c):
    padded = _lane_padded((r, c))[1]
    parts, off = [], 0
    for _, n in _pack_parts(padded):
        size = r * n // PACK_COLS
        parts.append(rows[off:off + size].reshape(r, n))
        off += size
    return jnp.concatenate(parts, axis=1)[:, :c]


def _pack_rows(r, c):
    return r * _lane_padded((r, c))[1] // PACK_COLS


def _pack_grads(layer_grads):
    def chip_block(k):
        blocks = []
        for n, axis in BIG:
            if n in layer_grads:
                width = layer_grads[n].shape[axis - 1] // N_CHIPS
                blocks.append(_to_rows(lax.slice_in_dim(layer_grads[n], k * width, (k + 1) * width, axis=axis - 1)))
        return jnp.concatenate(blocks, axis=0)

    g = jnp.stack([chip_block(k) for k in range(N_CHIPS)], axis=0)
    rows = -(-g.shape[1] // 512) * 512
    g = jnp.pad(g, ((0, 0), (0, rows - g.shape[1]), (0, 0)))
    return g.reshape(N_CHIPS, 2, rows // 2, PACK_COLS)


def _unpack_grads(reduced, layer_grads, shards):
    rows = reduced.reshape(-1, PACK_COLS)
    out, off = {}, 0
    for n, _ in BIG:
        if n in layer_grads:
            r, c = shards[n].shape[1:]
            out[n] = _from_rows(rows[off:off + _pack_rows(r, c)], r, c)[None]
            off += _pack_rows(r, c)
    return out


def _pack_small(grads, loss):
    parts = [p.reshape(-1) for n in REPLICATED + tuple(n for n, _ in TINY_SHARDED) for p in grads[n]] + [loss[0, :1]]
    flat = jnp.concatenate(parts)
    total = -(-flat.shape[0] // (SUBLANES * LANES)) * SUBLANES * LANES
    return jnp.pad(flat, (0, total - flat.shape[0])).reshape(-1, LANES)


def _unpack_small(summed, shards, chip):
    flat = summed.reshape(-1)
    out, off = {}, 0
    for n in REPLICATED:
        out[n] = flat[off:off + shards[n].size].reshape(shards[n].shape)
        off += shards[n].size
    for n, axis in TINY_SHARDED:
        width = shards[n].shape[axis]
        shape = shards[n].shape[:axis] + (N_CHIPS * width,) + shards[n].shape[axis + 1:]
        full = flat[off:off + N_CHIPS * shards[n].size].reshape(shape)
        out[n] = lax.dynamic_slice_in_dim(full, chip * width, width, axis=axis)
        off += N_CHIPS * shards[n].size
    return out, flat[off]


def kernel(x, mem, ffn1_norm, ffn1_w_in, ffn1_w_out, mix_norm, xa_norm, xa_mem_norm, xa_wq, xa_wkv, xa_wo, ffn2_norm,
           ffn2_w_in, ffn2_w_out, ab_w_in, lru_conv_w, lru_conv_b, lru_w_a, lru_b_a, lru_w_x, lru_b_x, lru_lambda,
           ab_w_out, dn_w_in, dn_conv_w, dn_a_log, dn_dt_bias, dn_o_norm, dn_w_out, final_norm, loss_target,
           m_ffn1_norm, m_ffn1_w_in, m_ffn1_w_out, m_mix_norm, m_xa_norm, m_xa_mem_norm, m_xa_wq, m_xa_wkv, m_xa_wo,
           m_ffn2_norm, m_ffn2_w_in, m_ffn2_w_out, m_ab_w_in, m_lru_conv_w, m_lru_conv_b, m_lru_w_a, m_lru_b_a,
           m_lru_w_x, m_lru_b_x, m_lru_lambda, m_ab_w_out, m_dn_w_in, m_dn_conv_w, m_dn_a_log, m_dn_dt_bias,
           m_dn_o_norm, m_dn_w_out, m_final_norm, v_ffn1_norm, v_ffn1_w_in, v_ffn1_w_out, v_mix_norm, v_xa_norm,
           v_xa_mem_norm, v_xa_wq, v_xa_wkv, v_xa_wo, v_ffn2_norm, v_ffn2_w_in, v_ffn2_w_out, v_ab_w_in,
           v_lru_conv_w, v_lru_conv_b, v_lru_w_a, v_lru_b_a, v_lru_w_x, v_lru_b_x, v_lru_lambda, v_ab_w_out,
           v_dn_w_in, v_dn_conv_w, v_dn_a_log, v_dn_dt_bias, v_dn_o_norm, v_dn_w_out, v_final_norm):
    given = dict(locals())
    shards = {n: given[n] for n in WEIGHTS}
    chip = 2 * lax.axis_index("x") + lax.axis_index("y")

    full = {n: shards[n] for n in REPLICATED}
    first, *later = _gather_blocks(shards)
    _gathered(first, _gather_weights([b for *_, b in first], [a for _, _, a, _ in first]), shards, full)
    loss, grad_x, grads, reduced = _local_step(x[0], mem[0], loss_target[0], full, pending=(later, shards), reduce_big=True)

    small, loss_sum = _unpack_small(_allreduce_small(_pack_small(grads, loss)), shards, chip)
    per_layer = [_unpack_grads(r, layer_grads, shards) for r, layer_grads in reduced]
    slabs = {n: [g] for n, g in small.items()}
    slabs.update({n: [p[n] for p in per_layer if n in p] for n, _ in BIG})

    grad, delta, new_m, new_v = {}, {}, {}, {}
    for n in WEIGHTS:
        delta[n], new_m[n], new_v[n], grad[n] = _adamw(shards[n], slabs[n], given["m_" + n], given["v_" + n], name=f"adamw_{n}")
    return (loss_sum, grad_x[None], *[grad[n] for n in WEIGHTS], *[delta[n] for n in WEIGHTS],
            *[new_m[n] for n in WEIGHTS], *[new_v[n] for n in WEIGHTS])
```

```python
import math

import jax
import jax.numpy as jnp
from jax import lax
from jax.experimental import pallas as pl
from jax.experimental.pallas import tpu as pltpu

F32 = jnp.float32
MXU_DTYPE = jnp.bfloat16
VMEM_LIMIT_BYTES = 48 * 1024 * 1024
MM_BLOCK_BYTES = 8 * 1024 * 1024
ADAMW_BLOCK_BYTES = 1024 * 1024
LANES = 128
SUBLANES = 8

NORM_EPS = 1e-6
CONV_K = 4
ATTN_PAIRS = 4
ATTN_HEAD_DIM = 64
ATTN_WIDTH = 512
ATTN_BLOCK = 128
DILATIONS = (1, 4, 16)
LRU_WIDTH = 512
LRU_BLOCKS = 8
LRU_C = 8.0
DN_HEADS = 8
DN_HEAD_DIM = 128
DN_WIDTH = 1024
DN_CHUNK = 64
XA_HEADS = 4
XA_HEAD_DIM = 256
D_FF = 2816
ADAM_LR, ADAM_B1, ADAM_B2, ADAM_EPS, ADAM_WD, ADAM_STEP = 0.001, 0.9, 0.999, 1e-08, 0.01, 10

MESH = pl.DeviceIdType.MESH


def _tile(n, prefs):
    for p in prefs:
        if n % p == 0:
            return p
    return n


def _params(*sem):
    return pltpu.CompilerParams(dimension_semantics=sem, vmem_limit_bytes=VMEM_LIMIT_BYTES)


def _dg(a, b, dims, hi=False):
    if hi:
        return lax.dot_general(a, b, (dims, ((), ())), precision=lax.Precision.HIGHEST, preferred_element_type=F32)
    return lax.dot_general(a.astype(MXU_DTYPE), b.astype(MXU_DTYPE), (dims, ((), ())), preferred_element_type=F32)


@jax.custom_vjp
def _bdot(a, b):
    return _dg(a, b, ((1,), (0,)))


def _bdot_fwd(a, b):
    return _bdot(a, b), (a, b)


def _bdot_bwd(r, g):
    a, b = r
    return _dg(g, b, ((1,), (1,))).astype(a.dtype), _dg(a, g, ((0,), (0,))).astype(b.dtype)


_bdot.defvjp(_bdot_fwd, _bdot_bwd)


def _log1p(t):
    return jnp.where(t < 0.01, t * (1.0 - t * (0.5 - t * (1.0 / 3.0))), jnp.log(1.0 + t))


def _neg_expm1(y):
    series = -y * (1.0 + 0.5 * y * (1.0 + (1.0 / 3.0) * y * (1.0 + 0.25 * y)))
    return jnp.where(y > -0.01, series, 1.0 - jnp.exp(y))


def _softplus(x):
    return jnp.maximum(x, 0.0) + _log1p(jnp.exp(-jnp.abs(x)))


def _sigmoid(x):
    return 0.5 * jnp.tanh(0.5 * x) + 0.5


def _silu(x):
    return x * _sigmoid(x)


def _gelu(x):
    return 0.5 * x * (1.0 + jnp.tanh(0.7978845608028654 * (x + 0.044715 * x * x * x)))


def _rows(shape):
    return lax.broadcasted_iota(jnp.int32, shape, 0)


def _cols(shape):
    return lax.broadcasted_iota(jnp.int32, shape, 1)


def _mm(a, b, *, mode="nn", out_dtype=F32, res=None, scale=1.0, name):
    if mode == "nn":
        (m, k), (k2, n) = a.shape, b.shape
    elif mode == "nt":
        (m, k), (n, k2) = a.shape, b.shape
    else:
        (k, m), (k2, n) = a.shape, b.shape
    assert k == k2, (a.shape, b.shape, mode)
    if mode == "tn":
        tm, tn, tk = _tile(m, (1024, 512, 256, 128)), _tile(n, (1024, 512, 256, 128)), _tile(k, (2048, 1024, 512, 256))
    else:
        tm, tn = _tile(m, (512, 256, 128)), _tile(n, (1024, 512, 256, 128))
        tk = k if k * tn * 2 <= MM_BLOCK_BYTES else _tile(k, (1024, 512, 256, 128))
    nk = k // tk
    dims = {"nn": ((1,), (0,)), "nt": ((1,), (1,)), "tn": ((0,), (0,))}[mode]

    def body(*refs):
        a_ref, b_ref = refs[:2]
        r_ref = refs[2] if res is not None else None
        o_ref = refs[3 if res is not None else 2]

        def finish(r):
            if scale != 1.0:
                r = r * scale
            if res is not None:
                r = r_ref[...] + r
            o_ref[...] = r.astype(out_dtype)

        if nk == 1:
            finish(_dg(a_ref[...], b_ref[...], dims))
            return
        acc = refs[-1]
        kk = pl.program_id(2)

        @pl.when(kk == 0)
        def _():
            acc[...] = jnp.zeros_like(acc)

        acc[...] += _dg(a_ref[...], b_ref[...], dims)

        @pl.when(kk == nk - 1)
        def _():
            finish(acc[...])

    a_spec = pl.BlockSpec((tk, tm), lambda i, j, kk: (kk, i)) if mode == "tn" else pl.BlockSpec((tm, tk), lambda i, j, kk: (i, kk))
    b_spec = pl.BlockSpec((tn, tk), lambda i, j, kk: (j, kk)) if mode == "nt" else pl.BlockSpec((tk, tn), lambda i, j, kk: (kk, j))
    o_spec = pl.BlockSpec((tm, tn), lambda i, j, kk: (i, j))
    in_specs = [a_spec, b_spec] + ([o_spec] if res is not None else [])
    args = (a, b) + ((res,) if res is not None else ())
    return pl.pallas_call(
        body, name=name, grid=(m // tm, n // tn, nk), in_specs=in_specs, out_specs=o_spec,
        out_shape=jax.ShapeDtypeStruct((m, n), out_dtype), scratch_shapes=[pltpu.VMEM((tm, tn), F32)] if nk > 1 else [],
        compiler_params=_params("parallel", "parallel", "arbitrary"),
    )(*args)


def _rms_fwd(x, g, *, name):
    s, d = x.shape
    tm = _tile(s, (512, 256))

    def body(x_ref, g_ref, o_ref):
        xv = x_ref[...]
        r = lax.rsqrt(jnp.mean(xv * xv, axis=-1, keepdims=True) + NORM_EPS)
        o_ref[...] = (xv * r * g_ref[...]).astype(o_ref.dtype)

    return pl.pallas_call(
        body, name=name, grid=(s // tm,),
        in_specs=[pl.BlockSpec((tm, d), lambda i: (i, 0)), pl.BlockSpec((1, d), lambda i: (0, 0))],
        out_specs=pl.BlockSpec((tm, d), lambda i: (i, 0)), out_shape=jax.ShapeDtypeStruct((s, d), MXU_DTYPE),
        compiler_params=_params("parallel"),
    )(x, g.reshape(1, d))


def _norm_mm(x, g, ws, out_dtypes, *, name):
    s, d = x.shape
    tm = _tile(s, (512, 256))
    nw = len(ws)

    def body(*refs):
        x_ref, g_ref = refs[:2]
        h_ref = refs[2 + nw]
        xv = x_ref[...]
        r = lax.rsqrt(jnp.mean(xv * xv, axis=-1, keepdims=True) + NORM_EPS)
        h = (xv * r * g_ref[...]).astype(MXU_DTYPE)
        h_ref[...] = h
        for w_ref, o_ref in zip(refs[2:2 + nw], refs[3 + nw:]):
            o_ref[...] = _dg(h, w_ref[...], ((1,), (0,))).astype(o_ref.dtype)

    row = lambda w: pl.BlockSpec((tm, w), lambda i: (i, 0))
    outs = pl.pallas_call(
        body, name=name, grid=(s // tm,),
        in_specs=[row(d), pl.BlockSpec((1, d), lambda i: (0, 0))]
        + [pl.BlockSpec(w.shape, lambda i: (0, 0), pipeline_mode=RESIDENT) for w in ws],
        out_specs=[row(d)] + [row(w.shape[1]) for w in ws],
        out_shape=[jax.ShapeDtypeStruct((s, d), MXU_DTYPE)] + [jax.ShapeDtypeStruct((s, w.shape[1]), t) for w, t in zip(ws, out_dtypes)],
        compiler_params=_params("parallel"),
    )(x, g.reshape(1, d), *ws)
    return outs[0], outs[1:]


def _mm_rms_bwd(pairs, x, g, dres, *, name):
    s, d = x.shape
    tm = _tile(s, (512, 256))
    n = len(pairs)

    def body(*refs):
        x_ref, g_ref, dr_ref = refs[2 * n:2 * n + 3]
        dx_ref, dg_ref = refs[2 * n + 3:]
        dh = _dg(refs[0][...], refs[n][...], ((1,), (1,)))
        for a_ref, w_ref in zip(refs[1:n], refs[n + 1:2 * n]):
            dh = dh + _dg(a_ref[...], w_ref[...], ((1,), (1,)))
        xv, gv = x_ref[...], g_ref[...]
        r = lax.rsqrt(jnp.mean(xv * xv, axis=-1, keepdims=True) + NORM_EPS)
        xh = xv * r
        dxh = dh * gv
        dx_ref[...] = dr_ref[...] + r * (dxh - xh * jnp.mean(dxh * xh, axis=-1, keepdims=True))

        @pl.when(pl.program_id(0) == 0)
        def _():
            dg_ref[...] = jnp.zeros_like(dg_ref)

        dg_ref[...] += jnp.sum(dh * xh, axis=0, keepdims=True)

    row = lambda w: pl.BlockSpec((tm, w), lambda i: (i, 0))
    vec = pl.BlockSpec((1, d), lambda i: (0, 0))
    return pl.pallas_call(
        body, name=name, grid=(s // tm,),
        in_specs=[row(a.shape[1]) for a, _ in pairs]
        + [pl.BlockSpec(w.shape, lambda i: (0, 0), pipeline_mode=RESIDENT) for _, w in pairs] + [row(d), vec, row(d)],
        out_specs=[row(d), vec], out_shape=[jax.ShapeDtypeStruct((s, d), F32), jax.ShapeDtypeStruct((1, d), F32)],
        compiler_params=_params("arbitrary"),
    )(*[a for a, _ in pairs], *[w for _, w in pairs], x, g.reshape(1, d), dres)


def _rms_bwd(x, g, dh, dres, *, name):
    s, d = x.shape
    tm = _tile(s, (512, 256))

    def body(x_ref, g_ref, dh_ref, dr_ref, dx_ref, dg_ref):
        xv = x_ref[...]
        r = lax.rsqrt(jnp.mean(xv * xv, axis=-1, keepdims=True) + NORM_EPS)
        xh = xv * r
        dhv = dh_ref[...].astype(F32)
        dxh = dhv * g_ref[...]
        dx = r * (dxh - xh * jnp.mean(dxh * xh, axis=-1, keepdims=True))
        dx_ref[...] = dr_ref[...] + dx

        @pl.when(pl.program_id(0) == 0)
        def _():
            dg_ref[...] = jnp.zeros_like(dg_ref)

        dg_ref[...] += jnp.sum(dhv * xh, axis=0, keepdims=True)

    row = pl.BlockSpec((tm, d), lambda i: (i, 0))
    vec = pl.BlockSpec((1, d), lambda i: (0, 0))
    return pl.pallas_call(
        body, name=name, grid=(s // tm,), in_specs=[row, vec, row, row], out_specs=[row, vec],
        out_shape=[jax.ShapeDtypeStruct((s, d), F32), jax.ShapeDtypeStruct((1, d), F32)],
        compiler_params=_params("arbitrary"),
    )(x, g.reshape(1, d), dh, dres)


FFN_CHUNK = 256
FFN_TM = 256
RESIDENT = pl.Buffered(1)


def _ffn_fwd_call(x, g, w_in, w_out, layer, *, name, gather=None):
    s, d = x.shape
    f = w_out.shape[1]
    tm = _tile(s, (2 * FFN_TM, FFN_TM))
    steps = s // tm
    n_g = len(gather[0]) if gather else 0
    g_shapes, g_sems, g_start, g_finish = _gather_plan(*gather) if gather else ([], [], None, None)

    def body(*refs):
        x_ref, g_ref, wi_ref, wo_ref = refs[:4]
        y_ref, u_ref = refs[4 + n_g:6 + n_g]
        act_ref = refs[6 + 2 * n_g]
        if gather:
            comm = (refs[4:4 + n_g], refs[6 + n_g:6 + 2 * n_g], *refs[7 + 2 * n_g:])
            pl.when(pl.program_id(0) == 0)(lambda: g_start(*comm))
        xv = x_ref[...]
        r = lax.rsqrt(jnp.mean(xv * xv, axis=-1, keepdims=True) + NORM_EPS)
        h = (xv * r * g_ref[...]).astype(MXU_DTYPE)
        for j in range(f // FFN_CHUNK):
            lo, hi = j * FFN_CHUNK, (j + 1) * FFN_CHUNK
            gate = _dg(h, wi_ref[:, lo:hi], ((1,), (0,))).astype(MXU_DTYPE)
            up = _dg(h, wi_ref[:, f + lo:f + hi], ((1,), (0,))).astype(MXU_DTYPE)
            u_ref[:, lo:hi] = gate
            u_ref[:, f + lo:f + hi] = up
            act_ref[:, lo:hi] = (_silu(gate.astype(F32)) * up.astype(F32)).astype(MXU_DTYPE)
        y_ref[...] = xv + 0.5 * _dg(act_ref[...], wo_ref[...], ((1,), (0,)))
        if gather:
            pl.when(pl.program_id(0) == steps - 1)(lambda: g_finish(*comm))

    row = lambda w: pl.BlockSpec((tm, w), lambda i: (i, 0))
    return pl.pallas_call(
        body, name=name, grid=(steps,),
        in_specs=[row(d), pl.BlockSpec((1, d), lambda i: (0, 0)),
                  pl.BlockSpec((None,) + w_in.shape[1:], lambda i: (layer, 0, 0), pipeline_mode=RESIDENT),
                  pl.BlockSpec((None,) + w_out.shape[1:], lambda i: (layer, 0, 0), pipeline_mode=RESIDENT)] + [HBM_SPEC] * n_g,
        out_specs=[row(d), row(2 * f)] + [HBM_SPEC] * n_g,
        out_shape=[jax.ShapeDtypeStruct((s, d), F32), jax.ShapeDtypeStruct((s, 2 * f), MXU_DTYPE)] + g_shapes,
        scratch_shapes=[pltpu.VMEM((tm, f), MXU_DTYPE)] + g_sems,
        compiler_params=_params("arbitrary" if gather else "parallel"),
    )(x, g.reshape(1, d), w_in, w_out, *(gather[0] if gather else ()))


def _ffn_bwd_call(x, g, u, dy, w_in, w_out, layer, *, name, exchange=None):
    s, d = x.shape
    f = w_out.shape[1]
    tm = _tile(s, (FFN_TM,))
    steps = s // tm
    hosted = exchange is not None
    e_shape, e_sems, e_start, e_finish = _exchange_plan(exchange) if hosted else (None, [], None, None)

    def body(*refs):
        x_ref, g_ref, u_ref, dy_ref, wi_ref, wo_ref = refs[:6]
        du_ref, dx_ref, dg_ref, h_ref = refs[6 + hosted:10 + hosted]
        if hosted:
            comm = (refs[6], *refs[10 + hosted:])
            pl.when(pl.program_id(0) == 0)(lambda: e_start(*comm))
        dyv = dy_ref[...]
        dyh = (0.5 * dyv).astype(MXU_DTYPE)
        for j in range(f // FFN_CHUNK):
            lo, hi = j * FFN_CHUNK, (j + 1) * FFN_CHUNK
            dact = _dg(dyh, wo_ref[lo:hi, :], ((1,), (1,)))
            gate, up = u_ref[:, lo:hi].astype(F32), u_ref[:, f + lo:f + hi].astype(F32)
            sg = _sigmoid(gate)
            du_ref[:, lo:hi] = (dact * up * sg * (1.0 + gate * (1.0 - sg))).astype(MXU_DTYPE)
            du_ref[:, f + lo:f + hi] = (dact * gate * sg).astype(MXU_DTYPE)
        dh = _dg(du_ref[...], wi_ref[...], ((1,), (1,)))
        xv, gv = x_ref[...], g_ref[...]
        r = lax.rsqrt(jnp.mean(xv * xv, axis=-1, keepdims=True) + NORM_EPS)
        xh = xv * r
        h_ref[...] = (xh * gv).astype(MXU_DTYPE)
        dxh = dh * gv
        dx_ref[...] = dyv + r * (dxh - xh * jnp.mean(dxh * xh, axis=-1, keepdims=True))

        @pl.when(pl.program_id(0) == 0)
        def _():
            dg_ref[...] = jnp.zeros_like(dg_ref)

        dg_ref[...] += jnp.sum(dh * xh, axis=0, keepdims=True)
        if hosted:
            pl.when(pl.program_id(0) == steps - 1)(lambda: e_finish(*comm))

    row = lambda w: pl.BlockSpec((tm, w), lambda i: (i, 0))
    vec = pl.BlockSpec((1, d), lambda i: (0, 0))
    return pl.pallas_call(
        body, name=name, grid=(steps,),
        in_specs=[row(d), vec, row(2 * f), row(d),
                  pl.BlockSpec((None,) + w_in.shape[1:], lambda i: (layer, 0, 0), pipeline_mode=RESIDENT),
                  pl.BlockSpec((None,) + w_out.shape[1:], lambda i: (layer, 0, 0), pipeline_mode=RESIDENT)] + [HBM_SPEC] * hosted,
        out_specs=[row(2 * f), row(d), vec, row(d)] + [HBM_SPEC] * hosted,
        out_shape=[jax.ShapeDtypeStruct((s, 2 * f), MXU_DTYPE), jax.ShapeDtypeStruct((s, d), F32),
                   jax.ShapeDtypeStruct((1, d), F32), jax.ShapeDtypeStruct((s, d), MXU_DTYPE)] + [e_shape] * hosted,
        scratch_shapes=e_sems, compiler_params=_params("arbitrary"),
    )(x, g.reshape(1, d), u, dy, w_in, w_out, *([exchange] if hosted else []))


def _ffn_dw_out(u, dy, *, name):
    s, f2 = u.shape
    f, d = f2 // 2, dy.shape[1]
    tf, tk = _tile(f, (1408, 256, 128)), _tile(s, (1024, 512, 256))
    nj = f // tf

    def body(g_ref, u_ref, dy_ref, o_ref):
        @pl.when(pl.program_id(1) == 0)
        def _():
            o_ref[...] = jnp.zeros_like(o_ref)

        act = _silu(g_ref[...].astype(F32)) * u_ref[...].astype(F32)
        o_ref[...] += _dg(act, 0.5 * dy_ref[...], ((0,), (0,)))

    return pl.pallas_call(
        body, name=name, grid=(nj, s // tk),
        in_specs=[pl.BlockSpec((tk, tf), lambda j, k: (k, j)), pl.BlockSpec((tk, tf), lambda j, k: (k, j + nj)),
                  pl.BlockSpec((tk, d), lambda j, k: (k, 0))],
        out_specs=pl.BlockSpec((tf, d), lambda j, k: (j, 0)), out_shape=jax.ShapeDtypeStruct((f, d), F32),
        compiler_params=_params("parallel", "arbitrary"),
    )(u, u, dy)


def _of_layer(w, layer):
    return (w[layer], 0) if isinstance(w, (list, tuple)) else (w, layer)


def _layer_matrix(w, layer):
    w, at = _of_layer(w, layer)
    return w[at]


def _ffn_fwd(x, g, w_in, w_out, layer, tag, gather=None):
    (w_in, at), (w_out, _) = _of_layer(w_in, layer), _of_layer(w_out, layer)
    y, u, *gathered = _ffn_fwd_call(x, g[layer], w_in, w_out, at, name=f"{tag}_fwd", gather=gather)
    return y, (x, u), gathered


def _ffn_bwd(saved, g, w_in, w_out, layer, dy, tag, exchange=None):
    x, u = saved
    (w_in, at), (w_out, _) = _of_layer(w_in, layer), _of_layer(w_out, layer)
    du, dx, dg, h, *parts = _ffn_bwd_call(x, g[layer], u, dy, w_in, w_out, at, name=f"{tag}_bwd", exchange=exchange)
    dw_out = _ffn_dw_out(u, dy, name=f"{tag}_dwout")
    dw_in = _mm(h, du, mode="tn", name=f"{tag}_dwin")
    return dx, dg[0], dw_in, dw_out, (parts[0] if parts else None)


ATTN_SCALE = ATTN_HEAD_DIM ** -0.5
NEG_BIG = -1e30
PROJ_AB_BLOCKS = 5


def _first_head(n):
    return _cols((n, LANES)) < ATTN_HEAD_DIM


def _per_head(tiles):
    first = _first_head(tiles[0].shape[0])
    return jnp.stack([jnp.where(first == (h == 0), t, 0.0) for t in tiles for h in (0, 1)], axis=0)


def _both(tiles):
    return jnp.stack([t for t in tiles for _ in (0, 1)], axis=0)


def _head_cols(tiles):
    return jnp.stack([t[:, c0:c0 + 1] for t in tiles for c0 in (0, ATTN_HEAD_DIM)], axis=0)


def _join_heads(v):
    return [v[2 * u] + v[2 * u + 1] for u in range(v.shape[0] // 2)]


def _spread_heads(v):
    first = _first_head(v.shape[1])
    return [jnp.where(first, v[2 * u], v[2 * u + 1]) for u in range(v.shape[0] // 2)]


def _band_masks(has_prev):
    qi, kj = _rows((ATTN_BLOCK, ATTN_BLOCK)), _cols((ATTN_BLOCK, ATTN_BLOCK))
    return (kj >= qi) & has_prev, kj <= qi


def _dattn_delta(o, dcat, *, name):
    s_len = o.shape[0]
    tm = _tile(s_len, (512, 256))

    def body(o_ref, do_ref, out_ref):
        r, c = _rows((ATTN_WIDTH, ATTN_WIDTH)), _cols((ATTN_WIDTH, ATTN_WIDTH))
        ones_bd = (r // ATTN_HEAD_DIM == c // ATTN_HEAD_DIM).astype(F32)
        out_ref[...] = _dg(o_ref[...] * do_ref[...], ones_bd, ((1,), (0,)), hi=True)

    blk = pl.BlockSpec((tm, ATTN_WIDTH), lambda i: (i, 0))
    return pl.pallas_call(
        body, name=name, grid=(s_len // tm,), in_specs=[blk, blk], out_specs=blk,
        out_shape=jax.ShapeDtypeStruct((s_len, ATTN_WIDTH), F32), compiler_params=_params("parallel"),
    )(o, dcat)


ATTN_UNITS = 4


def _units(it, d):
    if d == 1:
        return [(pl.ds(0, ATTN_BLOCK), pl.ds(p * LANES, LANES)) for p in range(ATTN_UNITS)]
    return [(pl.ds(it * ATTN_UNITS + u, ATTN_BLOCK, stride=d), pl.ds(0, LANES)) for u in range(ATTN_UNITS)]


def _tiles(ref, units):
    return [ref[rows, lanes] for rows, lanes in units]


def _store_tiles(ref, units, tiles):
    for (rows, lanes), t in zip(units, tiles):
        ref[rows, lanes] = t


def _stacked(a_tiles, b_tiles):
    return [jnp.concatenate([a, b], axis=0) for a, b in zip(a_tiles, b_tiles)]


def _passes(d):
    return max(d // ATTN_UNITS, 1)


def _pairs_per_step(d):
    return ATTN_PAIRS if d == 1 else 1


def _pair_specs(d, n_of):
    pairs = _pairs_per_step(d)
    groups = ATTN_PAIRS // pairs
    return lambda c: pl.BlockSpec((ATTN_BLOCK * d, LANES * pairs), lambda n, p: (n_of(n), c * groups + p))


def _sattn_fwd(proj, state, d, *, last, name):
    s_len = proj.shape[0]
    nb = s_len // (ATTN_BLOCK * d)
    first = state is None
    n_out = 2 if last else 3

    def body(*refs):
        q_ref, kp_ref, kc_ref, vp_ref, vc_ref = refs[:5]
        st_refs = () if first else refs[5:8]
        out_refs = refs[-n_out:]
        ok = jnp.concatenate(_band_masks(pl.program_id(0) > 0), axis=1)

        def one_pass(it, carry):
            units = _units(it, d)
            kcat = _stacked(_tiles(kp_ref, units), _tiles(kc_ref, units))
            vcat = _stacked(_tiles(vp_ref, units), _tiles(vc_ref, units))
            s = jnp.where(ok, _bdg(_per_head(_tiles(q_ref, units)), _both(kcat), 2, 2) * ATTN_SCALE, NEG_BIG)
            m_new = jnp.max(s, axis=2, keepdims=True)
            if not first:
                m_old = _head_cols(_tiles(st_refs[0], units))
                m_new = jnp.maximum(m_old, m_new)
                alpha = jnp.exp(m_old - m_new)
            p = jnp.exp(s - m_new)
            l_new = jnp.sum(p, axis=2, keepdims=True)
            acc = _join_heads(_bdg(p, _per_head(vcat), 2, 1))
            if not first:
                l_new = l_new + _head_cols(_tiles(st_refs[1], units)) * alpha
                acc = [a + a_in * sp for a, a_in, sp in zip(acc, _tiles(st_refs[2], units), _spread_heads(alpha))]
            m_pair, l_pair = _spread_heads(m_new), _spread_heads(l_new)
            if last:
                _store_tiles(out_refs[0], units, [a / l for a, l in zip(acc, l_pair)])
                _store_tiles(out_refs[1], units, [m + jnp.log(l) for m, l in zip(m_pair, l_pair)])
            else:
                _store_tiles(out_refs[0], units, m_pair)
                _store_tiles(out_refs[1], units, l_pair)
                _store_tiles(out_refs[2], units, acc)
            return carry

        lax.fori_loop(0, _passes(d), one_pass, 0)

    cur, prev = _pair_specs(d, lambda n: n), _pair_specs(d, lambda n: jnp.maximum(n - 1, 0))
    st = cur(0)
    return tuple(pl.pallas_call(
        body, name=name, grid=(nb, ATTN_PAIRS // _pairs_per_step(d)),
        in_specs=[cur(0), prev(1), cur(1), prev(2), cur(2)] + ([] if first else [st] * 3),
        out_specs=[st] * n_out, out_shape=[jax.ShapeDtypeStruct((s_len, ATTN_WIDTH), F32)] * n_out,
        compiler_params=_params("arbitrary", "parallel"),
    )(*([proj] * 5 + ([] if first else list(state)))))


def _dattn_forward(proj, tag):
    state = None
    for i, d in enumerate(DILATIONS):
        state = _sattn_fwd(proj, state, d, last=i == len(DILATIONS) - 1, name=f"{tag}_attn_d{d}")
    return state


def _sattn_bwd(proj, dcat, lse, delta, grads_in, d, *, name):
    s_len = proj.shape[0]
    nb = s_len // (ATTN_BLOCK * d)
    first = grads_in is None
    groups = ATTN_PAIRS // _pairs_per_step(d)

    def body(*refs):
        q_ref, kp_ref, kc_ref, vp_ref, vc_ref, do_ref, lse_ref, dl_ref = refs[:8]
        dq_in, dk_in, dv_in = (None, None, None) if first else refs[8:11]
        dq_ref, dk_ref, dv_ref, carry_k, carry_v = refs[-5:]
        n = pl.program_id(1)
        ok = jnp.concatenate(_band_masks(n > 0), axis=1)

        @pl.when(n == 0)
        def _():
            carry_k[...] = jnp.zeros_like(carry_k)
            carry_v[...] = jnp.zeros_like(carry_v)

        def leave(ref, carry, units, extra, into):
            out = [c + e for c, e in zip(_tiles(carry, units), extra)] if extra else _tiles(carry, units)
            if into is not None:
                out = [a + b for a, b in zip(out, _tiles(into, units))]
            _store_tiles(ref, units, out)

        def one_pass(it, carry):
            units = _units(it, d)
            kcat = _stacked(_tiles(kp_ref, units), _tiles(kc_ref, units))
            vcat = _stacked(_tiles(vp_ref, units), _tiles(vc_ref, units))
            q2, do2 = _per_head(_tiles(q_ref, units)), _per_head(_tiles(do_ref, units))
            s = _bdg(q2, _both(kcat), 2, 2) * ATTN_SCALE
            pr = jnp.where(ok, jnp.exp(jnp.where(ok, s, NEG_BIG) - _head_cols(_tiles(lse_ref, units))), 0.0)
            ds = pr * (_bdg(do2, _both(vcat), 2, 2) - _head_cols(_tiles(dl_ref, units)))
            dq = [t * ATTN_SCALE for t in _join_heads(_bdg(ds, _per_head(kcat), 2, 1))]
            if not first:
                dq = [a + b for a, b in zip(dq, _tiles(dq_in, units))]
            _store_tiles(dq_ref, units, dq)
            dk = [t * ATTN_SCALE for t in _join_heads(_bdg(ds, q2, 1, 1))]
            dv = _join_heads(_bdg(pr, do2, 1, 1))
            leave(dk_ref, carry_k, units, [t[:ATTN_BLOCK] for t in dk], dk_in)
            leave(dv_ref, carry_v, units, [t[:ATTN_BLOCK] for t in dv], dv_in)
            _store_tiles(carry_k, units, [t[ATTN_BLOCK:] for t in dk])
            _store_tiles(carry_v, units, [t[ATTN_BLOCK:] for t in dv])
            return carry

        def last_pass(it, carry):
            units = _units(it, d)
            leave(dk_ref, carry_k, units, None, dk_in)
            leave(dv_ref, carry_v, units, None, dv_in)
            return carry

        @pl.when(n < nb)
        def _():
            lax.fori_loop(0, _passes(d), one_pass, 0)

        @pl.when(n == nb)
        def _():
            lax.fori_loop(0, _passes(d), last_pass, 0)

    pairs = _pairs_per_step(d)
    blk = (ATTN_BLOCK * d, LANES * pairs)
    at = lambda n_of: (lambda c: pl.BlockSpec(blk, lambda p, n: (n_of(n), c * groups + p)))
    here = lambda n: jnp.minimum(n, nb - 1)
    cur, prev, lag = at(here), at(lambda n: jnp.maximum(here(n) - 1, 0)), at(lambda n: jnp.maximum(n - 1, 0))
    st, st_lag = cur(0), lag(0)
    return tuple(pl.pallas_call(
        body, name=name, grid=(groups, nb + 1),
        in_specs=[cur(0), prev(1), cur(1), prev(2), cur(2), st, st, st] + ([] if first else [st, st_lag, st_lag]),
        out_specs=[st, st_lag, st_lag], out_shape=[jax.ShapeDtypeStruct((s_len, ATTN_WIDTH), F32)] * 3,
        scratch_shapes=[pltpu.VMEM(blk, F32)] * 2, compiler_params=_params("parallel", "arbitrary"),
    )(*([proj] * 5 + [dcat, lse, delta] + ([] if first else list(grads_in)))))


def _dattn_backward(proj, o, lse, dcat, tag):
    delta = _dattn_delta(o, dcat, name=f"{tag}_attn_delta")
    grads = None
    for d in DILATIONS:
        grads = _sattn_bwd(proj, dcat, lse, delta, grads, d, name=f"{tag}_attn_bwd_d{d}")
    return grads


CONV_TC = 512
CONV_T = 512


def _conv_tiles(s_len, cb0, width):
    wide = 2 * CONV_TC
    tc = wide if width % wide == 0 and (cb0 * CONV_TC) % wide == 0 else CONV_TC
    return _tile(s_len, (CONV_T, CONV_T // 2)), tc, cb0 * CONV_TC // tc


def _shift_down(ext, k, t):
    return (pltpu.roll(ext, k, 0) if k else ext)[SUBLANES:SUBLANES + t]


def _conv_fwd(src, cb0, width, w8, *, name):
    s_len = src.shape[0]
    t, tc, cb = _conv_tiles(s_len, cb0, width)
    tpb = t // SUBLANES

    def body(x_ref, h_ref, w_ref, y_ref):
        halo = jnp.where(pl.program_id(0) > 0, h_ref[...], 0.0)
        ext = jnp.concatenate([halo, x_ref[...]], axis=0)
        w = w_ref[...]
        y = jnp.broadcast_to(w[CONV_K:CONV_K + 1], (t, tc))
        for k in range(CONV_K):
            y = y + w[k:k + 1] * _shift_down(ext, CONV_K - 1 - k, t)
        y_ref[...] = y

    return pl.pallas_call(
        body, name=name, grid=(s_len // t, width // tc),
        in_specs=[pl.BlockSpec((t, tc), lambda i, j: (i, cb + j)),
                  pl.BlockSpec((SUBLANES, tc), lambda i, j: (jnp.maximum(i * tpb - 1, 0), cb + j)),
                  pl.BlockSpec((SUBLANES, tc), lambda i, j: (0, j))],
        out_specs=pl.BlockSpec((t, tc), lambda i, j: (i, j)), out_shape=jax.ShapeDtypeStruct((s_len, width), F32),
        compiler_params=_params("parallel", "parallel"),
    )(src, src, w8)


def _conv_bwd(src, cb0, width, w8, dy, *, name):
    s_len = src.shape[0]
    t, tc, cb = _conv_tiles(s_len, cb0, width)
    tpb = t // SUBLANES
    ni = s_len // t

    def body(x_ref, h_ref, w_ref, dy_ref, dn_ref, dx_ref, dw_ref):
        i = pl.program_id(1)
        halo = jnp.where(i > 0, h_ref[...], 0.0)
        ext = jnp.concatenate([halo, x_ref[...]], axis=0)
        dyv = dy_ref[...]
        extn = jnp.concatenate([dyv, jnp.where(i < ni - 1, dn_ref[...], 0.0)], axis=0)
        w = w_ref[...]
        row = _rows((SUBLANES, tc))
        dx = jnp.zeros((t, tc), F32)
        dw = jnp.where(row == CONV_K, jnp.sum(dyv, axis=0, keepdims=True), 0.0)
        for k in range(CONV_K):
            up = CONV_K - 1 - k
            dx = dx + w[k:k + 1] * (pltpu.roll(extn, t + SUBLANES - up, 0) if up else extn)[:t]
            dw = dw + jnp.where(row == k, jnp.sum(dyv * _shift_down(ext, up, t), axis=0, keepdims=True), 0.0)
        dx_ref[...] = dx.astype(dx_ref.dtype)

        @pl.when(i == 0)
        def _():
            dw_ref[...] = jnp.zeros_like(dw_ref)

        dw_ref[...] += dw

    return pl.pallas_call(
        body, name=name, grid=(width // tc, ni),
        in_specs=[pl.BlockSpec((t, tc), lambda j, i: (i, cb + j)),
                  pl.BlockSpec((SUBLANES, tc), lambda j, i: (jnp.maximum(i * tpb - 1, 0), cb + j)),
                  pl.BlockSpec((SUBLANES, tc), lambda j, i: (0, j)),
                  pl.BlockSpec((t, tc), lambda j, i: (i, j)),
                  pl.BlockSpec((SUBLANES, tc), lambda j, i: (jnp.minimum((i + 1) * tpb, s_len // SUBLANES - 1), j))],
        out_specs=[pl.BlockSpec((t, tc), lambda j, i: (i, j)), pl.BlockSpec((SUBLANES, tc), lambda j, i: (0, j))],
        out_shape=[jax.ShapeDtypeStruct((s_len, width), MXU_DTYPE), jax.ShapeDtypeStruct((SUBLANES, width), F32)],
        compiler_params=_params("parallel", "arbitrary"),
    )(src, src, w8, dy, dy)


LRU_T = 256


def _lru_gates(xc, wa, wx, ba, bx, lam):
    r = _sigmoid(_bdot(xc, wa) + ba)
    i = _sigmoid(_bdot(xc, wx) + bx)
    log_a = (-LRU_C) * r * _softplus(-lam)
    return jnp.exp(log_a), jnp.sqrt(_neg_expm1(2.0 * log_a)) * i * xc


def _block_scan(a, b, state, reverse):
    t = a.shape[0]
    row = _rows(a.shape) % SUBLANES
    s = 1
    while s < SUBLANES:
        shift, ok = (t - s, row < SUBLANES - s) if reverse else (s, row >= s)
        b = jnp.where(ok, a * pltpu.roll(b, shift, 0) + b, b)
        a = jnp.where(ok, a * pltpu.roll(a, shift, 0), a)
        s *= 2
    groups = range(t // SUBLANES)
    out = [None] * len(groups)
    for g in (reversed(groups) if reverse else groups):
        rows = slice(g * SUBLANES, (g + 1) * SUBLANES)
        out[g] = b[rows] + a[rows] * state
        state = out[g][0:1] if reverse else out[g][SUBLANES - 1:SUBLANES]
    return jnp.concatenate(out, axis=0)


def _lru_fwd(xc, proj, wa, wx, ba, bx, lam, *, name):
    s_len, w = xc.shape
    t = _tile(s_len, (LRU_T,))

    def body(xc_ref, gr_ref, wa_ref, wx_ref, ba_ref, bx_ref, lam_ref, h_ref, y_ref, carry):
        @pl.when(pl.program_id(0) == 0)
        def _():
            carry[...] = jnp.zeros_like(carry)

        a, b = _lru_gates(xc_ref[...], wa_ref[...], wx_ref[...], ba_ref[...], bx_ref[...], lam_ref[...])
        h = _block_scan(a, b, carry[0:1, :], False)
        h_ref[...] = h
        y_ref[...] = (h * _gelu(gr_ref[...])).astype(y_ref.dtype)
        carry[0:1, :] = h[t - 1:t, :]

    row = pl.BlockSpec((t, w), lambda i: (i, 0))
    mat = pl.BlockSpec((w, w), lambda i: (0, 0))
    vec = pl.BlockSpec((1, w), lambda i: (0, 0))
    return pl.pallas_call(
        body, name=name, grid=(s_len // t,),
        in_specs=[row, pl.BlockSpec((t, w), lambda i: (i, PROJ_AB_BLOCKS - 1)), mat, mat, vec, vec, vec],
        out_specs=[row, row], out_shape=[jax.ShapeDtypeStruct((s_len, w), F32), jax.ShapeDtypeStruct((s_len, w), MXU_DTYPE)],
        scratch_shapes=[pltpu.VMEM((SUBLANES, w), F32)], compiler_params=_params("arbitrary"),
    )(xc, proj, wa, wx, ba, bx, lam)


def _lru_bwd(xc, proj, hs, dcat, wa, wx, ba, bx, lam, *, name):
    s_len, w = xc.shape
    t = _tile(s_len, (LRU_T,))
    nb = s_len // t
    tpb = t // SUBLANES

    def body(xc_ref, gr_ref, h_ref, hp_ref, dy_ref, wa_ref, wx_ref, ba_ref, bx_ref, lam_ref,
             dxc_ref, dgr_ref, dwa_ref, dwx_ref, dba_ref, dbx_ref, dlam_ref, carry):
        step = pl.program_id(0)
        params = (wa_ref[...], wx_ref[...], ba_ref[...], bx_ref[...], lam_ref[...])

        @pl.when(step == 0)
        def _():
            carry[...] = jnp.zeros_like(carry)
            for r in (dwa_ref, dwx_ref, dba_ref, dbx_ref, dlam_ref):
                r[...] = jnp.zeros_like(r)

        (a, _), vjp = jax.vjp(_lru_gates, xc_ref[...], *params)
        gr, h, dy = gr_ref[...], h_ref[...], dy_ref[...]
        gel, gel_vjp = jax.vjp(_gelu, gr)
        dgr_ref[...] = gel_vjp(dy * h)[0].astype(dgr_ref.dtype)
        dh = dy * gel
        big_g = _block_scan(a, a * dh, carry[0:1, :], True)
        row = _rows((t, w))
        g = dh + jnp.where(row == t - 1, carry[0:1, :], pltpu.roll(big_g, t - 1, 0))
        carry[0:1, :] = big_g[0:1, :]
        h_last = jnp.where(step < nb - 1, hp_ref[SUBLANES - 1:SUBLANES, :], 0.0)
        h_prev = jnp.where(row == 0, h_last, pltpu.roll(h, 1, 0))
        dxc, dwa, dwx, dba, dbx, dlam = vjp((g * h_prev, g))
        dxc_ref[...] = dxc
        dwa_ref[...] += dwa
        dwx_ref[...] += dwx
        dba_ref[...] += dba
        dbx_ref[...] += dbx
        dlam_ref[...] += dlam

    rev = lambda i: nb - 1 - i
    row = pl.BlockSpec((t, w), lambda i: (rev(i), 0))
    mat = pl.BlockSpec((w, w), lambda i: (0, 0))
    vec = pl.BlockSpec((1, w), lambda i: (0, 0))
    return pl.pallas_call(
        body, name=name, grid=(nb,),
        in_specs=[row, pl.BlockSpec((t, w), lambda i: (rev(i), PROJ_AB_BLOCKS - 1)), row,
                  pl.BlockSpec((SUBLANES, w), lambda i: (jnp.maximum(rev(i) * tpb - 1, 0), 0)),
                  pl.BlockSpec((t, w), lambda i: (rev(i), 1)), mat, mat, vec, vec, vec],
        out_specs=[row, row, mat, mat, vec, vec, vec],
        out_shape=[jax.ShapeDtypeStruct((s_len, w), F32), jax.ShapeDtypeStruct((s_len, w), MXU_DTYPE)]
        + [jax.ShapeDtypeStruct((w, w), F32)] * 2 + [jax.ShapeDtypeStruct((1, w), F32)] * 3,
        scratch_shapes=[pltpu.VMEM((SUBLANES, w), F32)], compiler_params=_params("arbitrary"),
    )(xc, proj, hs, hs, dcat, wa, wx, ba, bx, lam)


XA_T = 256
XA_SCALE = XA_HEAD_DIM ** -0.5


def _xa_heads(q, k, v):
    s = _bmm_nt(q, k) * XA_SCALE
    e = jnp.exp(s - jnp.max(s, axis=-1, keepdims=True))
    return _bmm(e / jnp.sum(e, axis=-1, keepdims=True), v)


def _xa_stack(ref):
    return jnp.stack([ref[:, h * XA_HEAD_DIM:(h + 1) * XA_HEAD_DIM].astype(F32) for h in range(XA_HEADS)], axis=0)


def _xa_fwd(q, kv, *, name):
    s_len, d = q.shape
    n_mem = kv.shape[0]
    t = _tile(s_len, (XA_T,))

    def body(q_ref, k_ref, v_ref, o_ref):
        o = _xa_heads(_xa_stack(q_ref), _xa_stack(k_ref), _xa_stack(v_ref))
        for h in range(XA_HEADS):
            o_ref[:, h * XA_HEAD_DIM:(h + 1) * XA_HEAD_DIM] = o[h].astype(o_ref.dtype)

    return pl.pallas_call(
        body, name=name, grid=(s_len // t,),
        in_specs=[pl.BlockSpec((t, d), lambda i: (i, 0)), pl.BlockSpec((n_mem, d), lambda i: (0, 0)),
                  pl.BlockSpec((n_mem, d), lambda i: (0, 1))],
        out_specs=pl.BlockSpec((t, d), lambda i: (i, 0)), out_shape=jax.ShapeDtypeStruct((s_len, d), MXU_DTYPE),
        compiler_params=_params("parallel"),
    )(q, kv, kv)


def _xa_bwd(q, kv, dy, wo, *, name):
    s_len, d = q.shape
    n_mem = kv.shape[0]
    t = _tile(s_len, (XA_T,))

    def body(q_ref, k_ref, v_ref, dy_ref, wo_ref, dq_ref, dk_ref, dv_ref):
        @pl.when(pl.program_id(0) == 0)
        def _():
            dk_ref[...] = jnp.zeros_like(dk_ref)
            dv_ref[...] = jnp.zeros_like(dv_ref)

        do = _dg(dy_ref[...], wo_ref[...], ((1,), (1,)))
        do = jnp.stack([do[:, h * XA_HEAD_DIM:(h + 1) * XA_HEAD_DIM] for h in range(XA_HEADS)], axis=0)
        _, vjp = jax.vjp(_xa_heads, _xa_stack(q_ref), _xa_stack(k_ref), _xa_stack(v_ref))
        dq, dk, dv = vjp(do)
        for h in range(XA_HEADS):
            sl = slice(h * XA_HEAD_DIM, (h + 1) * XA_HEAD_DIM)
            dq_ref[:, sl] = dq[h].astype(dq_ref.dtype)
            dk_ref[:, sl] += dk[h]
            dv_ref[:, sl] += dv[h]

    row = pl.BlockSpec((t, d), lambda i: (i, 0))
    dq, dk, dv = pl.pallas_call(
        body, name=name, grid=(s_len // t,),
        in_specs=[row, pl.BlockSpec((n_mem, d), lambda i: (0, 0)), pl.BlockSpec((n_mem, d), lambda i: (0, 1)), row,
                  pl.BlockSpec(wo.shape, lambda i: (0, 0), pipeline_mode=RESIDENT)],
        out_specs=[row, pl.BlockSpec((n_mem, d), lambda i: (0, 0)), pl.BlockSpec((n_mem, d), lambda i: (0, 0))],
        out_shape=[jax.ShapeDtypeStruct((s_len, d), MXU_DTYPE)] + [jax.ShapeDtypeStruct((n_mem, d), F32)] * 2,
        compiler_params=_params("arbitrary"),
    )(q, kv, kv, dy, wo)
    return dq, jnp.concatenate([dk, dv], axis=1)


DN_Q_SCALE = DN_HEAD_DIM ** -0.5
L2_EPS = 1e-6


def _bdg(a, b, ca, cb):
    return lax.dot_general(a.astype(MXU_DTYPE), b.astype(MXU_DTYPE), (((ca,), (cb,)), ((0,), (0,))), preferred_element_type=F32)


@jax.custom_vjp
def _bmm(a, b):
    return _bdg(a, b, 2, 1)


_bmm.defvjp(lambda a, b: (_bdg(a, b, 2, 1), (a, b)), lambda r, g: (_bdg(g, r[1], 2, 2), _bdg(r[0], g, 1, 1)))


@jax.custom_vjp
def _bmm_nt(a, b):
    return _bdg(a, b, 2, 2)


_bmm_nt.defvjp(lambda a, b: (_bdg(a, b, 2, 2), (a, b)), lambda r, g: (_bdg(g, r[1], 2, 1), _bdg(g, r[0], 1, 1)))


@jax.custom_vjp
def _bmm_tn(a, b):
    return _bdg(a, b, 1, 1)


_bmm_tn.defvjp(lambda a, b: (_bdg(a, b, 1, 1), (a, b)), lambda r, g: (_bdg(r[1], g, 2, 2), _bdg(r[0], g, 2, 1)))


def _tri_inverse(n):
    eye = (lax.broadcasted_iota(jnp.int32, n.shape, 1) == lax.broadcasted_iota(jnp.int32, n.shape, 2)).astype(F32)
    inv, p = eye - n, n
    for _ in range(5):
        p = _bdg(p, p, 2, 1)
        inv = _bdg(inv, eye + p, 2, 1)
    return inv


@jax.custom_vjp
def _tri_solve2(n, r1, r2):
    t = _tri_inverse(n)
    return _bdg(t, r1, 2, 1), _bdg(t, r2, 2, 1)


def _tri_solve2_fwd(n, r1, r2):
    t = _tri_inverse(n)
    x1, x2 = _bdg(t, r1, 2, 1), _bdg(t, r2, 2, 1)
    return (x1, x2), (t, x1, x2)


def _tri_solve2_bwd(saved, cts):
    t, x1, x2 = saved
    d1, d2 = _bdg(t, cts[0], 1, 1), _bdg(t, cts[1], 1, 1)
    return -(_bdg(d1, x1, 2, 2) + _bdg(d2, x2, 2, 2)), d1, d2


_tri_solve2.defvjp(_tri_solve2_fwd, _tri_solve2_bwd)


def _dn_gates(ab, alog, dtb):
    return -jnp.exp(alog) * _softplus(ab + dtb), _sigmoid(ab)


def _dn_heads(cq, ck, cv, z, g, beta, onorm, state):
    h, c, _ = cq.shape
    l2 = lambda t: t * lax.rsqrt(jnp.sum(t * t, axis=-1, keepdims=True) + L2_EPS)
    q, k, v = l2(_silu(cq)) * DN_Q_SCALE, l2(_silu(ck)), _silu(cv)
    r, cc = lax.broadcasted_iota(jnp.int32, (h, c, c), 1), lax.broadcasted_iota(jnp.int32, (h, c, c), 2)
    tri, eye = r >= cc, r == cc
    g_row = jnp.sum(jnp.where(eye, g, 0.0), axis=1, keepdims=True)
    gcum_c = jnp.sum(jnp.where(tri, g_row, 0.0), axis=2, keepdims=True)
    gcum_r = jnp.sum(jnp.where(cc >= r, g, 0.0), axis=1, keepdims=True)
    decay = jnp.where(tri, jnp.exp(jnp.where(tri, gcum_c - gcum_r, 0.0)), 0.0)
    kb = k * beta
    n = jnp.where(r > cc, _bmm_nt(kb, k) * decay, 0.0)
    u, w = _tri_solve2(n, v * beta, kb * jnp.exp(gcum_c))
    v_new = u - _bmm(w, state)
    o = _bmm(q * jnp.exp(gcum_c), state) + _bmm(_bmm_nt(q, k) * decay, v_new)
    g_last = jnp.sum(g, axis=1, keepdims=True)
    new_state = state * jnp.exp(g_last) + _bmm_tn(k * jnp.exp(g_last - gcum_c), v_new)
    on = o * lax.rsqrt(jnp.mean(o * o, axis=-1, keepdims=True) + NORM_EPS) * onorm
    return on * _silu(z), new_state


def _dn_stack(ref, col0):
    return jnp.stack([ref[:, col0 + h * DN_HEAD_DIM:col0 + (h + 1) * DN_HEAD_DIM].astype(F32) for h in range(DN_HEADS)], axis=0)


def _dn_cols(block, col0):
    return jnp.stack([block[:, col0 + h:col0 + h + 1] for h in range(DN_HEADS)], axis=0)


def _dn_fwd(cqkv, proj, ab, alog, dtb, onorm, *, name):
    s_len = cqkv.shape[0]
    c, hd, w = DN_CHUNK, DN_HEAD_DIM, DN_WIDTH
    n_chunks = s_len // c

    def body(c_ref, z_ref, ab_ref, alog_ref, dtb_ref, on_ref, o_ref, st_ref, state):
        @pl.when(pl.program_id(0) == 0)
        def _():
            state[...] = jnp.zeros_like(state)

        g_all, beta_all = _dn_gates(ab_ref[...], alog_ref[...], dtb_ref[...])
        st = state[...]
        st_ref[0] = st
        out, new = _dn_heads(_dn_stack(c_ref, 0), _dn_stack(c_ref, w), _dn_stack(c_ref, 2 * w), _dn_stack(z_ref, 0),
                             _dn_cols(g_all, 0), _dn_cols(beta_all, DN_HEADS), on_ref[...], st)
        state[...] = new
        for h in range(DN_HEADS):
            o_ref[:, h * hd:(h + 1) * hd] = out[h].astype(o_ref.dtype)

    vec = pl.BlockSpec((1, LANES), lambda i: (0, 0))
    return pl.pallas_call(
        body, name=name, grid=(n_chunks,),
        in_specs=[pl.BlockSpec((c, 3 * w), lambda i: (i, 0)), pl.BlockSpec((c, w), lambda i: (i, 3)),
                  pl.BlockSpec((c, LANES), lambda i: (i, 0)), vec, vec, vec],
        out_specs=[pl.BlockSpec((c, w), lambda i: (i, 0)), pl.BlockSpec((1, DN_HEADS, hd, hd), lambda i: (i, 0, 0, 0))],
        out_shape=[jax.ShapeDtypeStruct((s_len, w), MXU_DTYPE), jax.ShapeDtypeStruct((n_chunks, DN_HEADS, hd, hd), F32)],
        scratch_shapes=[pltpu.VMEM((DN_HEADS, hd, hd), F32)], compiler_params=_params("arbitrary"),
    )(cqkv, proj, ab, alog, dtb, onorm)


def _dn_bwd(cqkv, proj, ab, alog, dtb, onorm, states, dout, *, name):
    s_len = cqkv.shape[0]
    c, hd, w = DN_CHUNK, DN_HEAD_DIM, DN_WIDTH
    n_chunks = s_len // c

    def body(c_ref, z_ref, ab_ref, alog_ref, dtb_ref, on_ref, st_ref, do_ref,
             dc_ref, dz_ref, dab_ref, dalog_ref, ddtb_ref, don_ref, dstate):
        @pl.when(pl.program_id(0) == 0)
        def _():
            dstate[...] = jnp.zeros_like(dstate)
            for r in (dalog_ref, ddtb_ref, don_ref):
                r[...] = jnp.zeros_like(r)

        (g_all, beta_all), gates_vjp = jax.vjp(_dn_gates, ab_ref[...], alog_ref[...], dtb_ref[...])
        _, vjp = jax.vjp(_dn_heads, _dn_stack(c_ref, 0), _dn_stack(c_ref, w), _dn_stack(c_ref, 2 * w), _dn_stack(z_ref, 0),
                         _dn_cols(g_all, 0), _dn_cols(beta_all, DN_HEADS), on_ref[...], st_ref[0])
        dcq, dck, dcv, dz, dg, dbeta, don, dst = vjp((_dn_stack(do_ref, 0), dstate[...]))
        dstate[...] = dst
        col = _cols((c, LANES))
        dg_all, dbeta_all = jnp.zeros((c, LANES), F32), jnp.zeros((c, LANES), F32)
        for h in range(DN_HEADS):
            sl = slice(h * hd, (h + 1) * hd)
            dc_ref[:, sl] = dcq[h]
            dc_ref[:, w + h * hd:w + (h + 1) * hd] = dck[h]
            dc_ref[:, 2 * w + h * hd:2 * w + (h + 1) * hd] = dcv[h]
            dz_ref[:, sl] = dz[h].astype(dz_ref.dtype)
            dg_all = dg_all + jnp.where(col == h, dg[h], 0.0)
            dbeta_all = dbeta_all + jnp.where(col == DN_HEADS + h, dbeta[h], 0.0)
        dab, dalog, ddtb = gates_vjp((dg_all, dbeta_all))
        dab_ref[...] = dab
        dalog_ref[...] += dalog
        ddtb_ref[...] += ddtb
        don_ref[...] += don

    rev = lambda i: n_chunks - 1 - i
    vec = pl.BlockSpec((1, LANES), lambda i: (0, 0))
    return pl.pallas_call(
        body, name=name, grid=(n_chunks,),
        in_specs=[pl.BlockSpec((c, 3 * w), lambda i: (rev(i), 0)), pl.BlockSpec((c, w), lambda i: (rev(i), 3)),
                  pl.BlockSpec((c, LANES), lambda i: (rev(i), 0)), vec, vec, vec,
                  pl.BlockSpec((1, DN_HEADS, hd, hd), lambda i: (rev(i), 0, 0, 0)), pl.BlockSpec((c, w), lambda i: (rev(i), 0))],
        out_specs=[pl.BlockSpec((c, 3 * w), lambda i: (rev(i), 0)), pl.BlockSpec((c, w), lambda i: (rev(i), 0)),
                   pl.BlockSpec((c, LANES), lambda i: (rev(i), 0)), vec, vec, vec],
        out_shape=[jax.ShapeDtypeStruct((s_len, 3 * w), F32), jax.ShapeDtypeStruct((s_len, w), MXU_DTYPE),
                   jax.ShapeDtypeStruct((s_len, LANES), F32)] + [jax.ShapeDtypeStruct((1, LANES), F32)] * 3,
        scratch_shapes=[pltpu.VMEM((DN_HEADS, hd, hd), F32)], compiler_params=_params("arbitrary"),
    )(cqkv, proj, ab, alog, dtb, onorm, states, dout)


def _final_loss(x, g, target, *, name):
    s, d = x.shape
    tm = _tile(s, (512, 256))

    def body(x_ref, g_ref, t_ref, loss_ref, dx_ref, dg_ref):
        @pl.when(pl.program_id(0) == 0)
        def _():
            loss_ref[...] = jnp.zeros_like(loss_ref)
            dg_ref[...] = jnp.zeros_like(dg_ref)

        xv, gv = x_ref[...], g_ref[...]
        r = lax.rsqrt(jnp.mean(xv * xv, axis=-1, keepdims=True) + NORM_EPS)
        xh = xv * r
        err = xh * gv - t_ref[...]
        loss_ref[...] += 0.5 * jnp.sum(jnp.mean(err * err, axis=-1, keepdims=True), axis=0, keepdims=True)
        dy = err * (1.0 / d)
        dxh = dy * gv
        dx_ref[...] = r * (dxh - xh * jnp.mean(dxh * xh, axis=-1, keepdims=True))
        dg_ref[...] += jnp.sum(dy * xh, axis=0, keepdims=True)

    row = pl.BlockSpec((tm, d), lambda i: (i, 0))
    vec = pl.BlockSpec((1, d), lambda i: (0, 0))
    return pl.pallas_call(
        body, name=name, grid=(s // tm,), in_specs=[row, vec, row],
        out_specs=[pl.BlockSpec((1, LANES), lambda i: (0, 0)), row, vec],
        out_shape=[jax.ShapeDtypeStruct((1, LANES), F32), jax.ShapeDtypeStruct((s, d), F32), jax.ShapeDtypeStruct((1, d), F32)],
        compiler_params=_params("arbitrary"),
    )(x, g.reshape(1, d), target)


def _adamw(w, g_layers, m, v, *, name):
    shape = w.shape
    cols = shape[-1]
    n_l = len(g_layers)
    rows = max(w.size // cols, 1) // n_l
    tr = _tile(rows, [t for t in (512, 352, 256, 128, 64, 32, 16, 8) if t * cols * 4 <= ADAMW_BLOCK_BYTES])
    c1, c2 = 1.0 - ADAM_B1 ** ADAM_STEP, 1.0 - ADAM_B2 ** ADAM_STEP

    def body(*refs):
        w_ref, m_ref, v_ref = refs[:3]
        d_ref, nm_ref, nv_ref, g_ref = refs[3 + n_l:]
        gv = refs[3][...]
        for k in range(1, n_l):
            gv = jnp.where(pl.program_id(0) == k, refs[3 + k][...], gv)
        nm = ADAM_B1 * m_ref[...] + (1.0 - ADAM_B1) * gv
        nv = ADAM_B2 * v_ref[...] + (1.0 - ADAM_B2) * (gv * gv)
        d_ref[...] = -ADAM_LR * ((nm / c1) / (jnp.sqrt(nv / c2) + ADAM_EPS) + ADAM_WD * w_ref[...])
        nm_ref[...] = nm
        nv_ref[...] = nv
        g_ref[...] = gv

    blk = pl.BlockSpec((None, tr, cols), lambda l, i: (l, i, 0))
    slab = pl.BlockSpec((tr, cols), lambda l, i: (i, 0))
    outs = pl.pallas_call(
        body, name=name, grid=(n_l, rows // tr), in_specs=[blk] * 3 + [slab] * n_l, out_specs=[blk] * 4,
        out_shape=[jax.ShapeDtypeStruct((n_l, rows, cols), F32)] * 4, compiler_params=_params("parallel", "parallel"),
    )(*(t.reshape(n_l, rows, cols) for t in (w, m, v)), *(t.reshape(rows, cols) for t in g_layers))
    return tuple(t.reshape(shape) for t in outs)


def _block_diag(w):
    n, j, k = w.shape
    eye = jnp.eye(n, dtype=w.dtype)
    return (eye[:, None, :, None] * w[:, :, None, :]).reshape(n * j, n * k)


def _block_diag_part(m, n):
    j, k = m.shape[0] // n, m.shape[1] // n
    m4 = m.reshape(n, j, n, k)
    return jnp.stack([m4[i, :, i, :] for i in range(n)], axis=0)


DN_AB = 2 * DN_HEADS
DEPTH = 2


def _row(v, width=None):
    v = v.reshape(1, -1)
    return v if width is None else jnp.pad(v, ((0, 0), (0, width - v.shape[1])))


def _conv_w8(conv_w, bias=None):
    w8 = jnp.zeros((SUBLANES, conv_w.shape[1]), F32).at[:CONV_K].set(conv_w)
    return w8 if bias is None else w8.at[CONV_K].set(bias)


def _mixer_ab_fwd(x, w, tag):
    h, (proj,) = _norm_mm(x, w["mix_norm"][0], [w["ab_w_in"][0]], [F32], name=f"{tag}_in")
    o, lse = _dattn_forward(proj, tag)
    w8 = _conv_w8(w["lru_conv_w"][0], w["lru_conv_b"][0])
    xc = _conv_fwd(proj, PROJ_AB_BLOCKS - 2, LRU_WIDTH, w8, name=f"{tag}_conv")
    wa, wx = _block_diag(w["lru_w_a"][0]), _block_diag(w["lru_w_x"][0])
    vecs = (_row(w["lru_b_a"][0]), _row(w["lru_b_x"][0]), _row(w["lru_lambda"][0]))
    hs, y = _lru_fwd(xc, proj, wa, wx, *vecs, name=f"{tag}_lru")
    w_out = w["ab_w_out"][0]
    x2 = _mm(o, w_out[:ATTN_WIDTH], res=x, name=f"{tag}_out_attn")
    x2 = _mm(y, w_out[ATTN_WIDTH:], res=x2, name=f"{tag}_out_lru")
    return x2, (x, h, proj, o, lse, w8, xc, wa, wx, vecs, hs, y)


def _mixer_ab_bwd(saved, w, dy, tag):
    x, h, proj, o, lse, w8, xc, wa, wx, vecs, hs, y = saved
    w_out = w["ab_w_out"][0]
    dcat = _mm(dy, w_out, mode="nt", name=f"{tag}_dcat")
    dw_out = jnp.concatenate([_mm(o, dy, mode="tn", name=f"{tag}_dwout_attn"), _mm(y, dy, mode="tn", name=f"{tag}_dwout_lru")], axis=0)
    dq, dk, dv = _dattn_backward(proj, o, lse, dcat, tag)
    dxc, dgr, dwa, dwx, dba, dbx, dlam = _lru_bwd(xc, proj, hs, dcat, wa, wx, *vecs, name=f"{tag}_dlru")
    dxr, dw8 = _conv_bwd(proj, PROJ_AB_BLOCKS - 2, LRU_WIDTH, w8, dxc, name=f"{tag}_dconv")
    dproj = jnp.concatenate([t.astype(MXU_DTYPE) for t in (dq, dk, dv, dxr, dgr)], axis=1)
    dw_in = _mm(h, dproj, mode="tn", name=f"{tag}_dwin")
    dx, dg = _mm_rms_bwd([(dproj, w["ab_w_in"][0])], x, w["mix_norm"][0], dy, name=f"{tag}_dh")
    grads = dict(mix_norm=dg[0], ab_w_in=dw_in, ab_w_out=dw_out, lru_conv_w=dw8[:CONV_K], lru_conv_b=dw8[CONV_K],
                 lru_w_a=_block_diag_part(dwa, LRU_BLOCKS), lru_b_a=dba[0], lru_w_x=_block_diag_part(dwx, LRU_BLOCKS),
                 lru_b_x=dbx[0], lru_lambda=dlam[0])
    return dx, grads


def _dn_split_w(w_in):
    return w_in[:, :4 * DN_WIDTH], jnp.pad(w_in[:, 4 * DN_WIDTH:], ((0, 0), (0, LANES - DN_AB)))


def _mixer_dn_fwd(x, w, tag):
    w_qkvz, w_ab = _dn_split_w(w["dn_w_in"][0])
    h, (proj, ab) = _norm_mm(x, w["mix_norm"][1], [w_qkvz, w_ab], [F32, F32], name=f"{tag}_in")
    w8 = _conv_w8(w["dn_conv_w"][0])
    cqkv = _conv_fwd(proj, 0, 3 * DN_WIDTH, w8, name=f"{tag}_conv")
    vecs = (_row(w["dn_a_log"][0], LANES), _row(w["dn_dt_bias"][0], LANES), _row(w["dn_o_norm"][0]))
    og, states = _dn_fwd(cqkv, proj, ab, *vecs, name=f"{tag}_dn")
    x2 = _mm(og, w["dn_w_out"][0], res=x, name=f"{tag}_out")
    return x2, (x, h, w_qkvz, w_ab, proj, ab, w8, cqkv, vecs, og, states)


def _mixer_dn_bwd(saved, w, dy, tag):
    x, h, w_qkvz, w_ab, proj, ab, w8, cqkv, vecs, og, states = saved
    dout = _mm(dy, w["dn_w_out"][0], mode="nt", name=f"{tag}_dout")
    dw_out = _mm(og, dy, mode="tn", name=f"{tag}_dwout")
    dcqkv, dz, dab, dalog, ddtb, don = _dn_bwd(cqkv, proj, ab, *vecs, states, dout, name=f"{tag}_ddn")
    dqkv, dw8 = _conv_bwd(proj, 0, 3 * DN_WIDTH, w8, dcqkv, name=f"{tag}_dconv")
    dproj = jnp.concatenate([dqkv.astype(MXU_DTYPE), dz.astype(MXU_DTYPE)], axis=1)
    dw_in = jnp.concatenate([_mm(h, dproj, mode="tn", name=f"{tag}_dwin"),
                             _mm(h, dab, mode="tn", name=f"{tag}_dwin_ab")[:, :DN_AB]], axis=1)
    dx, dg = _mm_rms_bwd([(dproj, w_qkvz), (dab, w_ab)], x, w["mix_norm"][1], dy, name=f"{tag}_dh")
    grads = dict(mix_norm=dg[0], dn_w_in=dw_in, dn_w_out=dw_out, dn_conv_w=dw8[:CONV_K], dn_a_log=dalog[0, :DN_HEADS],
                 dn_dt_bias=ddtb[0, :DN_HEADS], dn_o_norm=don[0])
    return dx, grads


def _xa_layer_fwd(x, mem, w, layer, tag):
    hq, (q,) = _norm_mm(x, w["xa_norm"][layer], [_layer_matrix(w["xa_wq"], layer)], [MXU_DTYPE], name=f"{tag}_q")
    hm = _rms_fwd(mem, w["xa_mem_norm"][layer], name=f"{tag}_mem_norm")
    kv = _mm(hm, _layer_matrix(w["xa_wkv"], layer), name=f"{tag}_kv")
    oa = _xa_fwd(q, kv, name=f"{tag}_core")
    x2 = _mm(oa, _layer_matrix(w["xa_wo"], layer), res=x, name=f"{tag}_out")
    return x2, (x, hq, q, hm, kv, oa)


def _xa_layer_bwd(saved, mem, w, layer, dy, tag):
    x, hq, q, hm, kv, oa = saved
    dwo = _mm(oa, dy, mode="tn", name=f"{tag}_dwo")
    dq, dkv = _xa_bwd(q, kv, dy, _layer_matrix(w["xa_wo"], layer), name=f"{tag}_dcore")
    dwq = _mm(hq, dq, mode="tn", name=f"{tag}_dwq")
    dx, dg = _mm_rms_bwd([(dq, _layer_matrix(w["xa_wq"], layer))], x, w["xa_norm"][layer], dy, name=f"{tag}_dhq")
    dwkv = _mm(hm, dkv, mode="tn", name=f"{tag}_dwkv")
    dhm = _mm(dkv, _layer_matrix(w["xa_wkv"], layer), mode="nt", name=f"{tag}_dhm")
    _, dgm = _rms_bwd(mem, w["xa_mem_norm"][layer], dhm, jnp.zeros_like(mem), name=f"{tag}_dmem_norm")
    return dx, dict(xa_norm=dg[0], xa_mem_norm=dgm[0], xa_wq=dwq, xa_wkv=dwkv, xa_wo=dwo)


def _local_step(x, mem, target, w, pending=None, reduce_big=False):
    saved = []
    plans, shards = pending if pending else ([], None)
    hosts = {("ffn1", 0): plans[0], ("ffn2", 0): plans[1]} if plans else {}

    def ffn(which, layer, x):
        plan = hosts.get((which, layer))
        hosted = ([b for *_, b in plan], [a for _, _, a, _ in plan]) if plan else None
        x, s, gathered = _ffn_fwd(x, w[f"{which}_norm"], w[f"{which}_w_in"], w[f"{which}_w_out"], layer, f"l{layer}_{which}", gather=hosted)
        if plan:
            _gathered(plan, gathered, shards, w)
        return x, s

    for layer in range(DEPTH):
        t = f"l{layer}"
        x, s1 = ffn("ffn1", layer, x)
        x, s2 = (_mixer_ab_fwd if layer % 2 == 0 else _mixer_dn_fwd)(x, w, f"{t}_mix")
        x, s3 = _xa_layer_fwd(x, mem, w, layer, f"{t}_xa")
        x, s4 = ffn("ffn2", layer, x)
        saved.append((s1, s2, s3, s4))
    loss, dx, dgf = _final_loss(x, w["final_norm"], target, name="final_loss")
    per_layer, reduced = [None] * DEPTH, [None] * DEPTH
    travelling = None
    for layer in reversed(range(DEPTH)):
        t = f"l{layer}"
        s1, s2, s3, s4 = saved[layer]
        g = {}
        dx, g["ffn2_norm"], g["ffn2_w_in"], g["ffn2_w_out"], parts = _ffn_bwd(
            s4, w["ffn2_norm"], w["ffn2_w_in"], w["ffn2_w_out"], layer, dx, f"{t}_ffn2",
            exchange=travelling[1] if travelling else None)
        if travelling:
            reduced[travelling[0]] = _reduce_end(travelling[1], parts, f"l{travelling[0]}")
        dx, gx = _xa_layer_bwd(s3, mem, w, layer, dx, f"{t}_xa")
        dx, gm = (_mixer_ab_bwd if layer % 2 == 0 else _mixer_dn_bwd)(s2, w, dx, f"{t}_mix")
        dx, g["ffn1_norm"], g["ffn1_w_in"], g["ffn1_w_out"], _ = _ffn_bwd(
            s1, w["ffn1_norm"], w["ffn1_w_in"], w["ffn1_w_out"], layer, dx, f"{t}_ffn1")
        per_layer[layer] = {**g, **gx, **gm}
        if reduce_big:
            chip_sum = _reduce_begin(_pack_grads(per_layer[layer]), t)
            if layer > 0:
                travelling = (layer, chip_sum)
            else:
                reduced[layer] = _reduce_end(chip_sum, _exchange_chips(chip_sum), t)
    grads = {"final_norm": [dgf[0]]}
    for layer_grads in per_layer:
        for name, value in layer_grads.items():
            grads.setdefault(name, []).append(value)
    return loss, dx, grads, list(zip(reduced, per_layer))


N_CHIPS = 4
WIRE_DTYPE = jnp.bfloat16
HBM_SPEC = pl.BlockSpec(memory_space=pltpu.HBM)
PACK_COLS = 1024


def _place():
    x, y, c = lax.axis_index("x"), lax.axis_index("y"), lax.axis_index("c")
    return x, y, c, [(1 - x, y), (x, 1 - y), (1 - x, 1 - y)]


def _remote(src, dst, sems, k, to):
    return pltpu.make_async_remote_copy(src_ref=src, dst_ref=dst, send_sem=sems[0].at[k], recv_sem=sems[1].at[k],
                                        device_id=to, device_id_type=MESH)


def _gather_weights(blocks, axes):
    n = len(blocks)
    out_shapes, sem_shapes, start, finish = _gather_plan(blocks, axes)

    def body(*refs):
        start(refs[:n], refs[n:2 * n], *refs[2 * n:])
        finish(refs[:n], refs[n:2 * n], *refs[2 * n:])

    return pl.pallas_call(
        body, name="gather_weights", in_specs=[HBM_SPEC] * n, out_specs=[HBM_SPEC] * n,
        out_shape=out_shapes, scratch_shapes=sem_shapes,
    )(*blocks)


def _gather_plan(blocks, axes):
    n = len(blocks)
    split = [b.shape[1] % 32 == 0 for b in blocks]

    def full_shape(i):
        l, r, c = blocks[i].shape
        return (l, N_CHIPS * r, c) if axes[i] == 1 else (l, r, N_CHIPS * c)

    def copies(ins, outs, send_sems, recv_sems):
        x, y, c, chips = _place()
        sems = (send_sems, recv_sems)
        sibling = (x, y, 1 - c)
        me = 2 * x + y

        def window(i, k, h):
            l, r, cc = blocks[i].shape
            r0, nr = (0, r) if h is None else (h * (r // 2), r // 2)
            if axes[i] == 1:
                return outs[i].at[:, pl.ds(k * r + r0, nr), :]
            return outs[i].at[:, pl.ds(r0, nr), pl.ds(k * cc, cc)]

        def mine(i, h):
            r = blocks[i].shape[1]
            return ins[i] if h is None else ins[i].at[:, pl.ds(h * (r // 2), r // 2), :]

        half = lambda i: c if split[i] else None
        first = [_remote(mine(i, half(i)), window(i, me, half(i)), sems, 3 * i + j, (*chip, c))
                 for i in range(n) for j, chip in enumerate(chips)]
        first += [_remote(ins[i], window(i, me, None), sems, 6 * n + i, sibling) for i in range(n)]
        arrive = lambda i, j, h, k, frm: _remote(window(i, 2 * chips[j][0] + chips[j][1], h), window(i, 2 * chips[j][0] + chips[j][1], h),
                                                 sems, k, frm)
        return first, arrive, chips, c, sibling

    def start(ins, outs, send_sems, recv_sems):
        for cp in copies(ins, outs, send_sems, recv_sems)[0]:
            cp.start()

    def finish(ins, outs, send_sems, recv_sems):
        first, arrive, chips, c, sibling = copies(ins, outs, send_sems, recv_sems)
        passed = []
        for i in range(n):
            for j, (cx, cy) in enumerate(chips):
                arrive(i, j, c if split[i] else None, 3 * i + j, (cx, cy, c)).wait_recv()
                if split[i]:
                    passed.append(arrive(i, j, c, 3 * (n + i) + j, sibling))
                    passed[-1].start()
        for i in range(n):
            if split[i]:
                for j in range(len(chips)):
                    arrive(i, j, 1 - c, 3 * (n + i) + j, sibling).wait_recv()
        for cp in first[3 * n:]:
            cp.wait_recv()
        for cp in first + passed:
            cp.wait_send()

    sem_shapes = [pltpu.SemaphoreType.DMA((7 * n,)), pltpu.SemaphoreType.DMA((7 * n,))]
    return [jax.ShapeDtypeStruct(full_shape(i), blocks[i].dtype) for i in range(n)], sem_shapes, start, finish


def _allreduce_small(v):
    rows, cols = v.shape
    n_dev = 2 * N_CHIPS

    def body(v_ref, out_ref, all_ref, send_sems, recv_sems, local_sem):
        x, y, c, chips = _place()
        sems = (send_sems, recv_sems)
        me, sibling = (x, y, c), (x, y, 1 - c)
        slot = lambda px, py, pc: all_ref.at[pl.ds((4 * px + 2 * py + pc) * rows, rows), :]
        mine = pltpu.make_async_copy(v_ref, slot(*me), local_sem)
        mine.start()
        first = [_remote(v_ref, slot(*me), sems, 0, sibling)]
        first += [_remote(v_ref, slot(*me), sems, 1 + j, (*chip, c)) for j, chip in enumerate(chips)]
        for cp in first:
            cp.start()
        passed = [_remote(slot(*chip, c), slot(*chip, c), sems, 4 + j, sibling) for j, chip in enumerate(chips)]
        for j, chip in enumerate(chips):
            _remote(slot(*chip, c), slot(*chip, c), sems, 1 + j, me).wait_recv()
            passed[j].start()
        _remote(slot(*sibling), slot(*sibling), sems, 0, me).wait_recv()
        for j, chip in enumerate(chips):
            _remote(slot(*chip, 1 - c), slot(*chip, 1 - c), sems, 4 + j, me).wait_recv()
        for cp in first + passed:
            cp.wait_send()
        mine.wait()
        acc = all_ref[pl.ds(0, rows), :]
        for k in range(1, n_dev):
            acc = acc + all_ref[pl.ds(k * rows, rows), :]
        out_ref[...] = acc

    vmem = pl.BlockSpec(memory_space=pltpu.VMEM)
    return pl.pallas_call(
        body, name="allreduce_small", in_specs=[vmem], out_specs=vmem, out_shape=jax.ShapeDtypeStruct((rows, cols), F32),
        scratch_shapes=[pltpu.VMEM((n_dev * rows, cols), F32), pltpu.SemaphoreType.DMA((7,)), pltpu.SemaphoreType.DMA((7,)),
                        pltpu.SemaphoreType.DMA],
    )(v)


def _swap_other_half(g4, tag):
    n, _, rows, cols = g4.shape

    def body(v_ref, out_ref, send_sems, recv_sems):
        x, y, c, _ = _place()
        cp = _remote(v_ref.at[:, 1 - c], out_ref, (send_sems, recv_sems), 0, (x, y, 1 - c))
        cp.start()
        cp.wait()

    return pl.pallas_call(
        body, name=f"{tag}_reduce_swap", in_specs=[HBM_SPEC], out_specs=HBM_SPEC, out_shape=jax.ShapeDtypeStruct((n, rows, cols), g4.dtype),
        scratch_shapes=[pltpu.SemaphoreType.DMA((1,)), pltpu.SemaphoreType.DMA((1,))],
    )(g4)


def _add_kept_half(g4, got, tag):
    n, _, rows, cols = g4.shape
    tr = _tile(rows, (256, 128, 64, 32, 16))
    nb = rows // tr

    def body(c_ref, a_ref, b_ref, o_ref):
        o_ref[...] = (a_ref[...] + b_ref[...]).astype(o_ref.dtype)

    return pl.pallas_call(
        body, name=f"{tag}_reduce_sum_cores",
        grid_spec=pltpu.PrefetchScalarGridSpec(
            num_scalar_prefetch=1, grid=(n, nb),
            in_specs=[pl.BlockSpec((None, None, tr, cols), lambda k, i, c_ref: (k, c_ref[0], i, 0)),
                      pl.BlockSpec((None, tr, cols), lambda k, i, c_ref: (k, i, 0))],
            out_specs=pl.BlockSpec((None, tr, cols), lambda k, i, c_ref: (k, i, 0))),
        out_shape=jax.ShapeDtypeStruct((n, rows, cols), WIRE_DTYPE), compiler_params=_params("parallel", "parallel"),
    )(lax.axis_index("c").astype(jnp.int32).reshape(1), g4, got)


def _exchange_plan(v):
    def copies(v_ref, out_ref, send_sems, recv_sems):
        x, y, c, chips = _place()
        return [_remote(v_ref.at[2 * cx + cy], out_ref.at[j], (send_sems, recv_sems), j, (cx, cy, c)) for j, (cx, cy) in enumerate(chips)]

    def start(*refs):
        for cp in copies(*refs):
            cp.start()

    def finish(*refs):
        for cp in copies(*refs):
            cp.wait_recv()
        for cp in copies(*refs):
            cp.wait_send()

    sem_shapes = [pltpu.SemaphoreType.DMA((N_CHIPS - 1,)), pltpu.SemaphoreType.DMA((N_CHIPS - 1,))]
    return jax.ShapeDtypeStruct((N_CHIPS - 1,) + v.shape[1:], v.dtype), sem_shapes, start, finish


def _exchange_chips(v):
    out_shape, sem_shapes, start, finish = _exchange_plan(v)

    def body(*refs):
        start(*refs)
        finish(*refs)

    return pl.pallas_call(body, name="exchange_chips", in_specs=[HBM_SPEC], out_specs=HBM_SPEC, out_shape=out_shape,
                          scratch_shapes=sem_shapes)(v)


def _swap_sibling(v, tag):
    def body(v_ref, out_ref, send_sems, recv_sems):
        x, y, c, _ = _place()
        cp = _remote(v_ref, out_ref, (send_sems, recv_sems), 0, (x, y, 1 - c))
        cp.start()
        cp.wait()

    return pl.pallas_call(
        body, name=f"{tag}_share_halves", in_specs=[HBM_SPEC], out_specs=HBM_SPEC, out_shape=jax.ShapeDtypeStruct(v.shape, v.dtype),
        scratch_shapes=[pltpu.SemaphoreType.DMA((1,)), pltpu.SemaphoreType.DMA((1,))],
    )(v)


def _sum_chips(own4, parts, tag):
    _, rows, cols = own4.shape
    tr = _tile(rows, (256, 128, 64, 32, 16))

    def body(me_ref, own_ref, p0_ref, p1_ref, p2_ref, o_ref):
        acc = own_ref[...].astype(F32)
        for r in (p0_ref, p1_ref, p2_ref):
            acc = acc + r[...].astype(F32)
        o_ref[...] = acc

    part = lambda j: pl.BlockSpec((None, tr, cols), lambda i, me_ref: (j, i, 0))
    chip = (2 * lax.axis_index("x") + lax.axis_index("y")).astype(jnp.int32).reshape(1)
    return pl.pallas_call(
        body, name=f"{tag}_reduce_sum_chips",
        grid_spec=pltpu.PrefetchScalarGridSpec(
            num_scalar_prefetch=1, grid=(rows // tr,),
            in_specs=[pl.BlockSpec((None, tr, cols), lambda i, me_ref: (me_ref[0], i, 0)), part(0), part(1), part(2)],
            out_specs=pl.BlockSpec((tr, cols), lambda i, me_ref: (i, 0))),
        out_shape=jax.ShapeDtypeStruct((rows, cols), F32), compiler_params=_params("parallel"),
    )(chip, own4, parts, parts, parts)


def _reduce_begin(g4, tag):
    return _add_kept_half(g4, _swap_other_half(g4, tag), tag)


def _reduce_end(chip_sum, parts, tag):
    half = _sum_chips(chip_sum, parts, tag)
    other = _swap_sibling(half, tag)
    return jnp.where(lax.axis_index("c") == 0, jnp.stack([half, other]), jnp.stack([other, half]))


BIG = (("ffn1_w_in", 2), ("ffn1_w_out", 1), ("xa_wq", 1), ("xa_wkv", 2), ("xa_wo", 1), ("ffn2_w_in", 2), ("ffn2_w_out", 1),
       ("ab_w_in", 2), ("ab_w_out", 1), ("dn_w_in", 2), ("dn_w_out", 1))
TINY_SHARDED = (("lru_conv_w", 2), ("dn_conv_w", 2))
REPLICATED = ("ffn1_norm", "mix_norm", "xa_norm", "xa_mem_norm", "ffn2_norm", "lru_conv_b", "lru_w_a", "lru_b_a", "lru_w_x",
              "lru_b_x", "lru_lambda", "dn_a_log", "dn_dt_bias", "dn_o_norm", "final_norm")
WEIGHTS = ("ffn1_norm", "ffn1_w_in", "ffn1_w_out", "mix_norm", "xa_norm", "xa_mem_norm", "xa_wq", "xa_wkv", "xa_wo", "ffn2_norm",
           "ffn2_w_in", "ffn2_w_out", "ab_w_in", "lru_conv_w", "lru_conv_b", "lru_w_a", "lru_b_a", "lru_w_x", "lru_b_x",
           "lru_lambda", "ab_w_out", "dn_w_in", "dn_conv_w", "dn_a_log", "dn_dt_bias", "dn_o_norm", "dn_w_out", "final_norm")


def _lane_padded(shape):
    return shape[:-1] + (-(-shape[-1] // LANES) * LANES,)


def _pad_lanes(t):
    return jnp.pad(t, [(0, 0)] * (t.ndim - 1) + [(0, _lane_padded(t.shape)[-1] - t.shape[-1])])


FIRST_USED = ("ffn1_w_in", "ffn1_w_out")
LAYER_1_ONLY = ("dn_w_in", "dn_w_out", "dn_conv_w")


def _gather_blocks(shards):
    groups = [], [], []
    for n, a in BIG + TINY_SHARDED:
        block = _pad_lanes(shards[n]).astype(MXU_DTYPE) if (n, a) in BIG else shards[n]
        if block.shape[0] == 1:
            groups[2 if n in LAYER_1_ONLY else 1].append((n, None, a, block))
        else:
            for layer in range(block.shape[0]):
                group = 2 if layer > 0 else 0 if n in FIRST_USED else 1
                groups[group].append((n, layer, a, block[layer:layer + 1]))
    return groups


def _gathered(plan, arrays, shards, into):
    for (n, layer, axis, _), full in zip(plan, arrays):
        width, padded = shards[n].shape[-1], _lane_padded(shards[n].shape)[-1]
        if padded != width:
            assert axis == 2
            full = jnp.concatenate([full[..., k * padded:k * padded + width] for k in range(N_CHIPS)], axis=-1)
        if layer is None:
            into[n] = full
        else:
            into.setdefault(n, [None] * shards[n].shape[0])[layer] = full
    return into


def _pack_parts(cols):
    whole = cols // PACK_COLS * PACK_COLS
    return [(c0, PACK_COLS) for c0 in range(0, whole, PACK_COLS)] + ([(whole, cols - whole)] if cols > whole else [])


def _to_rows(block):
    block = _pad_lanes(block)
    return jnp.concatenate([block[:, c0:c0 + n].reshape(-1, PACK_COLS) for c0, n in _pack_parts(block.shape[1])], axis=0)


def _from_rows(rows, r, c):
    padded = _lane_padded((r, c))[1]
    parts, off = [], 0
    for _, n in _pack_parts(padded):
        size = r * n // PACK_COLS
        parts.append(rows[off:off + size].reshape(r, n))
        off += size
    return jnp.concatenate(parts, axis=1)[:, :c]


def _pack_rows(r, c):
    return r * _lane_padded((r, c))[1] // PACK_COLS


def _pack_grads(layer_grads):
    names = [(n, axis) for n, axis in BIG if n in layer_grads]
    used = sum(_pack_rows(layer_grads[n].shape[0] // (N_CHIPS if axis == 1 else 1),
                          layer_grads[n].shape[1] // (N_CHIPS if axis == 2 else 1)) for n, axis in names)
    rows = -(-used // 512) * 512

    def chip_block(k):
        blocks = []
        for n, axis in names:
            width = layer_grads[n].shape[axis - 1] // N_CHIPS
            blocks.append(_to_rows(lax.slice_in_dim(layer_grads[n], k * width, (k + 1) * width, axis=axis - 1)))
        if rows > used:
            blocks.append(jnp.zeros((rows - used, PACK_COLS), F32))
        return jnp.concatenate(blocks, axis=0)

    return jnp.stack([chip_block(k) for k in range(N_CHIPS)], axis=0).reshape(N_CHIPS, 2, rows // 2, PACK_COLS)


def _unpack_grads(reduced, layer_grads, shards):
    rows = reduced.reshape(-1, PACK_COLS)
    out, off = {}, 0
    for n, _ in BIG:
        if n in layer_grads:
            r, c = shards[n].shape[1:]
            out[n] = _from_rows(rows[off:off + _pack_rows(r, c)], r, c)[None]
            off += _pack_rows(r, c)
    return out


def _pack_small(grads, loss):
    parts = [p.reshape(-1) for n in REPLICATED + tuple(n for n, _ in TINY_SHARDED) for p in grads[n]] + [loss[0, :1]]
    flat = jnp.concatenate(parts)
    total = -(-flat.shape[0] // (SUBLANES * LANES)) * SUBLANES * LANES
    return jnp.pad(flat, (0, total - flat.shape[0])).reshape(-1, LANES)


def _unpack_small(summed, shards, chip):
    flat = summed.reshape(-1)
    out, off = {}, 0
    for n in REPLICATED:
        out[n] = flat[off:off + shards[n].size].reshape(shards[n].shape)
        off += shards[n].size
    for n, axis in TINY_SHARDED:
        width = shards[n].shape[axis]
        shape = shards[n].shape[:axis] + (N_CHIPS * width,) + shards[n].shape[axis + 1:]
        full = flat[off:off + N_CHIPS * shards[n].size].reshape(shape)
        out[n] = lax.dynamic_slice_in_dim(full, chip * width, width, axis=axis)
        off += N_CHIPS * shards[n].size
    return out, flat[off]


def kernel(x, mem, ffn1_norm, ffn1_w_in, ffn1_w_out, mix_norm, xa_norm, xa_mem_norm, xa_wq, xa_wkv, xa_wo, ffn2_norm,
           ffn2_w_in, ffn2_w_out, ab_w_in, lru_conv_w, lru_conv_b, lru_w_a, lru_b_a, lru_w_x, lru_b_x, lru_lambda,
           ab_w_out, dn_w_in, dn_conv_w, dn_a_log, dn_dt_bias, dn_o_norm, dn_w_out, final_norm, loss_target,
           m_ffn1_norm, m_ffn1_w_in, m_ffn1_w_out, m_mix_norm, m_xa_norm, m_xa_mem_norm, m_xa_wq, m_xa_wkv, m_xa_wo,
           m_ffn2_norm, m_ffn2_w_in, m_ffn2_w_out, m_ab_w_in, m_lru_conv_w, m_lru_conv_b, m_lru_w_a, m_lru_b_a,
           m_lru_w_x, m_lru_b_x, m_lru_lambda, m_ab_w_out, m_dn_w_in, m_dn_conv_w, m_dn_a_log, m_dn_dt_bias,
           m_dn_o_norm, m_dn_w_out, m_final_norm, v_ffn1_norm, v_ffn1_w_in, v_ffn1_w_out, v_mix_norm, v_xa_norm,
           v_xa_mem_norm, v_xa_wq, v_xa_wkv, v_xa_wo, v_ffn2_norm, v_ffn2_w_in, v_ffn2_w_out, v_ab_w_in,
           v_lru_conv_w, v_lru_conv_b, v_lru_w_a, v_lru_b_a, v_lru_w_x, v_lru_b_x, v_lru_lambda, v_ab_w_out,
           v_dn_w_in, v_dn_conv_w, v_dn_a_log, v_dn_dt_bias, v_dn_o_norm, v_dn_w_out, v_final_norm):
    given = dict(locals())
    shards = {n: given[n] for n in WEIGHTS}
    chip = 2 * lax.axis_index("x") + lax.axis_index("y")

    full = {n: shards[n] for n in REPLICATED}
    first, *later = _gather_blocks(shards)
    _gathered(first, _gather_weights([b for *_, b in first], [a for _, _, a, _ in first]), shards, full)
    loss, grad_x, grads, reduced = _local_step(x[0], mem[0], loss_target[0], full, pending=(later, shards), reduce_big=True)

    small, loss_sum = _unpack_small(_allreduce_small(_pack_small(grads, loss)), shards, chip)
    per_layer = [_unpack_grads(r, layer_grads, shards) for r, layer_grads in reduced]
    slabs = {n: [g] for n, g in small.items()}
    slabs.update({n: [p[n] for p in per_layer if n in p] for n, _ in BIG})

    grad, delta, new_m, new_v = {}, {}, {}, {}
    for n in WEIGHTS:
        delta[n], new_m[n], new_v[n], grad[n] = _adamw(shards[n], slabs[n], given["m_" + n], given["v_" + n], name=f"adamw_{n}")
    return (loss_sum, grad_x[None], *[grad[n] for n in WEIGHTS], *[delta[n] for n in WEIGHTS],
            *[new_m[n] for n in WEIGHTS], *[new_v[n] for n in WEIGHTS])
```

```python
import math

import jax
import jax.numpy as jnp
from jax import lax
from jax.experimental import pallas as pl
from jax.experimental.pallas import tpu as pltpu

F32 = jnp.float32
MXU_DTYPE = jnp.bfloat16
VMEM_LIMIT_BYTES = 48 * 1024 * 1024
MM_BLOCK_BYTES = 8 * 1024 * 1024
ADAMW_BLOCK_BYTES = 1024 * 1024
LANES = 128
SUBLANES = 8

NORM_EPS = 1e-6
CONV_K = 4
ATTN_PAIRS = 4
ATTN_HEAD_DIM = 64
ATTN_WIDTH = 512
ATTN_BLOCK = 128
DILATIONS = (1, 4, 16)
LRU_WIDTH = 512
LRU_BLOCKS = 8
LRU_C = 8.0
DN_HEADS = 8
DN_HEAD_DIM = 128
DN_WIDTH = 1024
DN_CHUNK = 64
XA_HEADS = 4
XA_HEAD_DIM = 256
D_FF = 2816
ADAM_LR, ADAM_B1, ADAM_B2, ADAM_EPS, ADAM_WD, ADAM_STEP = 0.001, 0.9, 0.999, 1e-08, 0.01, 10

MESH = pl.DeviceIdType.MESH


def _tile(n, prefs):
    for p in prefs:
        if n % p == 0:
            return p
    return n


def _params(*sem):
    return pltpu.CompilerParams(dimension_semantics=sem, vmem_limit_bytes=VMEM_LIMIT_BYTES)


def _dg(a, b, dims, hi=False):
    if hi:
        return lax.dot_general(a, b, (dims, ((), ())), precision=lax.Precision.HIGHEST, preferred_element_type=F32)
    return lax.dot_general(a.astype(MXU_DTYPE), b.astype(MXU_DTYPE), (dims, ((), ())), preferred_element_type=F32)


@jax.custom_vjp
def _bdot(a, b):
    return _dg(a, b, ((1,), (0,)))


def _bdot_fwd(a, b):
    return _bdot(a, b), (a, b)


def _bdot_bwd(r, g):
    a, b = r
    return _dg(g, b, ((1,), (1,))).astype(a.dtype), _dg(a, g, ((0,), (0,))).astype(b.dtype)


_bdot.defvjp(_bdot_fwd, _bdot_bwd)


def _log1p(t):
    return jnp.where(t < 0.01, t * (1.0 - t * (0.5 - t * (1.0 / 3.0))), jnp.log(1.0 + t))


def _neg_expm1(y):
    series = -y * (1.0 + 0.5 * y * (1.0 + (1.0 / 3.0) * y * (1.0 + 0.25 * y)))
    return jnp.where(y > -0.01, series, 1.0 - jnp.exp(y))


def _softplus(x):
    return jnp.maximum(x, 0.0) + _log1p(jnp.exp(-jnp.abs(x)))


def _sigmoid(x):
    return 0.5 * jnp.tanh(0.5 * x) + 0.5


def _silu(x):
    return x * _sigmoid(x)


def _gelu(x):
    return 0.5 * x * (1.0 + jnp.tanh(0.7978845608028654 * (x + 0.044715 * x * x * x)))


def _rows(shape):
    return lax.broadcasted_iota(jnp.int32, shape, 0)


def _cols(shape):
    return lax.broadcasted_iota(jnp.int32, shape, 1)


def _mm(a, b, *, mode="nn", out_dtype=F32, res=None, scale=1.0, name):
    if mode == "nn":
        (m, k), (k2, n) = a.shape, b.shape
    elif mode == "nt":
        (m, k), (n, k2) = a.shape, b.shape
    else:
        (k, m), (k2, n) = a.shape, b.shape
    assert k == k2, (a.shape, b.shape, mode)
    if mode == "tn":
        tm, tn, tk = _tile(m, (1024, 512, 256, 128)), _tile(n, (1024, 512, 256, 128)), _tile(k, (2048, 1024, 512, 256))
    else:
        tm, tn = _tile(m, (512, 256, 128)), _tile(n, (1024, 512, 256, 128))
        tk = k if k * tn * 2 <= MM_BLOCK_BYTES else _tile(k, (1024, 512, 256, 128))
    nk = k // tk
    dims = {"nn": ((1,), (0,)), "nt": ((1,), (1,)), "tn": ((0,), (0,))}[mode]

    def body(*refs):
        a_ref, b_ref = refs[:2]
        r_ref = refs[2] if res is not None else None
        o_ref = refs[3 if res is not None else 2]

        def finish(r):
            if scale != 1.0:
                r = r * scale
            if res is not None:
                r = r_ref[...] + r
            o_ref[...] = r.astype(out_dtype)

        if nk == 1:
            finish(_dg(a_ref[...], b_ref[...], dims))
            return
        acc = refs[-1]
        kk = pl.program_id(2)

        @pl.when(kk == 0)
        def _():
            acc[...] = jnp.zeros_like(acc)

        acc[...] += _dg(a_ref[...], b_ref[...], dims)

        @pl.when(kk == nk - 1)
        def _():
            finish(acc[...])

    a_spec = pl.BlockSpec((tk, tm), lambda i, j, kk: (kk, i)) if mode == "tn" else pl.BlockSpec((tm, tk), lambda i, j, kk: (i, kk))
    b_spec = pl.BlockSpec((tn, tk), lambda i, j, kk: (j, kk)) if mode == "nt" else pl.BlockSpec((tk, tn), lambda i, j, kk: (kk, j))
    o_spec = pl.BlockSpec((tm, tn), lambda i, j, kk: (i, j))
    in_specs = [a_spec, b_spec] + ([o_spec] if res is not None else [])
    args = (a, b) + ((res,) if res is not None else ())
    return pl.pallas_call(
        body, name=name, grid=(m // tm, n // tn, nk), in_specs=in_specs, out_specs=o_spec,
        out_shape=jax.ShapeDtypeStruct((m, n), out_dtype), scratch_shapes=[pltpu.VMEM((tm, tn), F32)] if nk > 1 else [],
        compiler_params=_params("parallel", "parallel", "arbitrary"),
    )(*args)


def _rms_fwd(x, g, *, name):
    s, d = x.shape
    tm = _tile(s, (512, 256))

    def body(x_ref, g_ref, o_ref):
        xv = x_ref[...]
        r = lax.rsqrt(jnp.mean(xv * xv, axis=-1, keepdims=True) + NORM_EPS)
        o_ref[...] = (xv * r * g_ref[...]).astype(o_ref.dtype)

    return pl.pallas_call(
        body, name=name, grid=(s // tm,),
        in_specs=[pl.BlockSpec((tm, d), lambda i: (i, 0)), pl.BlockSpec((1, d), lambda i: (0, 0))],
        out_specs=pl.BlockSpec((tm, d), lambda i: (i, 0)), out_shape=jax.ShapeDtypeStruct((s, d), MXU_DTYPE),
        compiler_params=_params("parallel"),
    )(x, g.reshape(1, d))


def _norm_mm(x, g, ws, out_dtypes, *, name):
    s, d = x.shape
    tm = _tile(s, (512, 256))
    nw = len(ws)

    def body(*refs):
        x_ref, g_ref = refs[:2]
        h_ref = refs[2 + nw]
        xv = x_ref[...]
        r = lax.rsqrt(jnp.mean(xv * xv, axis=-1, keepdims=True) + NORM_EPS)
        h = (xv * r * g_ref[...]).astype(MXU_DTYPE)
        h_ref[...] = h
        for w_ref, o_ref in zip(refs[2:2 + nw], refs[3 + nw:]):
            o_ref[...] = _dg(h, w_ref[...], ((1,), (0,))).astype(o_ref.dtype)

    row = lambda w: pl.BlockSpec((tm, w), lambda i: (i, 0))
    outs = pl.pallas_call(
        body, name=name, grid=(s // tm,),
        in_specs=[row(d), pl.BlockSpec((1, d), lambda i: (0, 0))]
        + [pl.BlockSpec(w.shape, lambda i: (0, 0), pipeline_mode=RESIDENT) for w in ws],
        out_specs=[row(d)] + [row(w.shape[1]) for w in ws],
        out_shape=[jax.ShapeDtypeStruct((s, d), MXU_DTYPE)] + [jax.ShapeDtypeStruct((s, w.shape[1]), t) for w, t in zip(ws, out_dtypes)],
        compiler_params=_params("parallel"),
    )(x, g.reshape(1, d), *ws)
    return outs[0], outs[1:]


def _mm_rms_bwd(pairs, x, g, dres, *, name):
    s, d = x.shape
    tm = _tile(s, (512, 256))
    n = len(pairs)

    def body(*refs):
        x_ref, g_ref, dr_ref = refs[2 * n:2 * n + 3]
        dx_ref, dg_ref = refs[2 * n + 3:]
        dh = _dg(refs[0][...], refs[n][...], ((1,), (1,)))
        for a_ref, w_ref in zip(refs[1:n], refs[n + 1:2 * n]):
            dh = dh + _dg(a_ref[...], w_ref[...], ((1,), (1,)))
        xv, gv = x_ref[...], g_ref[...]
        r = lax.rsqrt(jnp.mean(xv * xv, axis=-1, keepdims=True) + NORM_EPS)
        xh = xv * r
        dxh = dh * gv
        dx_ref[...] = dr_ref[...] + r * (dxh - xh * jnp.mean(dxh * xh, axis=-1, keepdims=True))

        @pl.when(pl.program_id(0) == 0)
        def _():
            dg_ref[...] = jnp.zeros_like(dg_ref)

        dg_ref[...] += jnp.sum(dh * xh, axis=0, keepdims=True)

    row = lambda w: pl.BlockSpec((tm, w), lambda i: (i, 0))
    vec = pl.BlockSpec((1, d), lambda i: (0, 0))
    return pl.pallas_call(
        body, name=name, grid=(s // tm,),
        in_specs=[row(a.shape[1]) for a, _ in pairs]
        + [pl.BlockSpec(w.shape, lambda i: (0, 0), pipeline_mode=RESIDENT) for _, w in pairs] + [row(d), vec, row(d)],
        out_specs=[row(d), vec], out_shape=[jax.ShapeDtypeStruct((s, d), F32), jax.ShapeDtypeStruct((1, d), F32)],
        compiler_params=_params("arbitrary"),
    )(*[a for a, _ in pairs], *[w for _, w in pairs], x, g.reshape(1, d), dres)


def _rms_bwd(x, g, dh, dres, *, name):
    s, d = x.shape
    tm = _tile(s, (512, 256))

    def body(x_ref, g_ref, dh_ref, dr_ref, dx_ref, dg_ref):
        xv = x_ref[...]
        r = lax.rsqrt(jnp.mean(xv * xv, axis=-1, keepdims=True) + NORM_EPS)
        xh = xv * r
        dhv = dh_ref[...].astype(F32)
        dxh = dhv * g_ref[...]
        dx = r * (dxh - xh * jnp.mean(dxh * xh, axis=-1, keepdims=True))
        dx_ref[...] = dr_ref[...] + dx

        @pl.when(pl.program_id(0) == 0)
        def _():
            dg_ref[...] = jnp.zeros_like(dg_ref)

        dg_ref[...] += jnp.sum(dhv * xh, axis=0, keepdims=True)

    row = pl.BlockSpec((tm, d), lambda i: (i, 0))
    vec = pl.BlockSpec((1, d), lambda i: (0, 0))
    return pl.pallas_call(
        body, name=name, grid=(s // tm,), in_specs=[row, vec, row, row], out_specs=[row, vec],
        out_shape=[jax.ShapeDtypeStruct((s, d), F32), jax.ShapeDtypeStruct((1, d), F32)],
        compiler_params=_params("arbitrary"),
    )(x, g.reshape(1, d), dh, dres)


FFN_CHUNK = 256
FFN_TM = 256
RESIDENT = pl.Buffered(1)


def _ffn_fwd_call(x, g, w_in, w_out, layer, *, name, gather=None):
    s, d = x.shape
    f = w_out.shape[1]
    tm = _tile(s, (2 * FFN_TM, FFN_TM))
    steps = s // tm
    n_g = len(gather[0]) if gather else 0
    g_shapes, g_sems, g_start, g_finish = _gather_plan(*gather) if gather else ([], [], None, None)

    def body(*refs):
        x_ref, g_ref, wi_ref, wo_ref = refs[:4]
        y_ref, u_ref = refs[4 + n_g:6 + n_g]
        act_ref = refs[6 + 2 * n_g]
        if gather:
            comm = (refs[4:4 + n_g], refs[6 + n_g:6 + 2 * n_g], *refs[7 + 2 * n_g:])
            pl.when(pl.program_id(0) == 0)(lambda: g_start(*comm))
        xv = x_ref[...]
        r = lax.rsqrt(jnp.mean(xv * xv, axis=-1, keepdims=True) + NORM_EPS)
        h = (xv * r * g_ref[...]).astype(MXU_DTYPE)
        for j in range(f // FFN_CHUNK):
            lo, hi = j * FFN_CHUNK, (j + 1) * FFN_CHUNK
            gate = _dg(h, wi_ref[:, lo:hi], ((1,), (0,))).astype(MXU_DTYPE)
            up = _dg(h, wi_ref[:, f + lo:f + hi], ((1,), (0,))).astype(MXU_DTYPE)
            u_ref[:, lo:hi] = gate
            u_ref[:, f + lo:f + hi] = up
            act_ref[:, lo:hi] = (_silu(gate.astype(F32)) * up.astype(F32)).astype(MXU_DTYPE)
        y_ref[...] = xv + 0.5 * _dg(act_ref[...], wo_ref[...], ((1,), (0,)))
        if gather:
            pl.when(pl.program_id(0) == steps - 1)(lambda: g_finish(*comm))

    row = lambda w: pl.BlockSpec((tm, w), lambda i: (i, 0))
    return pl.pallas_call(
        body, name=name, grid=(steps,),
        in_specs=[row(d), pl.BlockSpec((1, d), lambda i: (0, 0)),
                  pl.BlockSpec((None,) + w_in.shape[1:], lambda i: (layer, 0, 0), pipeline_mode=RESIDENT),
                  pl.BlockSpec((None,) + w_out.shape[1:], lambda i: (layer, 0, 0), pipeline_mode=RESIDENT)] + [HBM_SPEC] * n_g,
        out_specs=[row(d), row(2 * f)] + [HBM_SPEC] * n_g,
        out_shape=[jax.ShapeDtypeStruct((s, d), F32), jax.ShapeDtypeStruct((s, 2 * f), MXU_DTYPE)] + g_shapes,
        scratch_shapes=[pltpu.VMEM((tm, f), MXU_DTYPE)] + g_sems,
        compiler_params=_params("arbitrary" if gather else "parallel"),
    )(x, g.reshape(1, d), w_in, w_out, *(gather[0] if gather else ()))


def _ffn_bwd_call(x, g, u, dy, w_in, w_out, layer, *, name, exchange=None):
    s, d = x.shape
    f = w_out.shape[1]
    tm = _tile(s, (FFN_TM,))
    steps = s // tm
    hosted = exchange is not None
    e_shape, e_sems, e_start, e_finish = _exchange_plan(exchange) if hosted else (None, [], None, None)

    def body(*refs):
        x_ref, g_ref, u_ref, dy_ref, wi_ref, wo_ref = refs[:6]
        du_ref, dx_ref, dg_ref, h_ref = refs[6 + hosted:10 + hosted]
        if hosted:
            comm = (refs[6], *refs[10 + hosted:])
            pl.when(pl.program_id(0) == 0)(lambda: e_start(*comm))
        dyv = dy_ref[...]
        dyh = (0.5 * dyv).astype(MXU_DTYPE)
        for j in range(f // FFN_CHUNK):
            lo, hi = j * FFN_CHUNK, (j + 1) * FFN_CHUNK
            dact = _dg(dyh, wo_ref[lo:hi, :], ((1,), (1,)))
            gate, up = u_ref[:, lo:hi].astype(F32), u_ref[:, f + lo:f + hi].astype(F32)
            sg = _sigmoid(gate)
            du_ref[:, lo:hi] = (dact * up * sg * (1.0 + gate * (1.0 - sg))).astype(MXU_DTYPE)
            du_ref[:, f + lo:f + hi] = (dact * gate * sg).astype(MXU_DTYPE)
        dh = _dg(du_ref[...], wi_ref[...], ((1,), (1,)))
        xv, gv = x_ref[...], g_ref[...]
        r = lax.rsqrt(jnp.mean(xv * xv, axis=-1, keepdims=True) + NORM_EPS)
        xh = xv * r
        h_ref[...] = (xh * gv).astype(MXU_DTYPE)
        dxh = dh * gv
        dx_ref[...] = dyv + r * (dxh - xh * jnp.mean(dxh * xh, axis=-1, keepdims=True))

        @pl.when(pl.program_id(0) == 0)
        def _():
            dg_ref[...] = jnp.zeros_like(dg_ref)

        dg_ref[...] += jnp.sum(dh * xh, axis=0, keepdims=True)
        if hosted:
            pl.when(pl.program_id(0) == steps - 1)(lambda: e_finish(*comm))

    row = lambda w: pl.BlockSpec((tm, w), lambda i: (i, 0))
    vec = pl.BlockSpec((1, d), lambda i: (0, 0))
    return pl.pallas_call(
        body, name=name, grid=(steps,),
        in_specs=[row(d), vec, row(2 * f), row(d),
                  pl.BlockSpec((None,) + w_in.shape[1:], lambda i: (layer, 0, 0), pipeline_mode=RESIDENT),
                  pl.BlockSpec((None,) + w_out.shape[1:], lambda i: (layer, 0, 0), pipeline_mode=RESIDENT)] + [HBM_SPEC] * hosted,
        out_specs=[row(2 * f), row(d), vec, row(d)] + [HBM_SPEC] * hosted,
        out_shape=[jax.ShapeDtypeStruct((s, 2 * f), MXU_DTYPE), jax.ShapeDtypeStruct((s, d), F32),
                   jax.ShapeDtypeStruct((1, d), F32), jax.ShapeDtypeStruct((s, d), MXU_DTYPE)] + [e_shape] * hosted,
        scratch_shapes=e_sems, compiler_params=_params("arbitrary"),
    )(x, g.reshape(1, d), u, dy, w_in, w_out, *([exchange] if hosted else []))


def _ffn_dw_out(u, dy, *, name):
    s, f2 = u.shape
    f, d = f2 // 2, dy.shape[1]
    tf, tk = _tile(f, (1408, 256, 128)), _tile(s, (1024, 512, 256))
    nj = f // tf

    def body(g_ref, u_ref, dy_ref, o_ref):
        @pl.when(pl.program_id(1) == 0)
        def _():
            o_ref[...] = jnp.zeros_like(o_ref)

        act = _silu(g_ref[...].astype(F32)) * u_ref[...].astype(F32)
        o_ref[...] += _dg(act, 0.5 * dy_ref[...], ((0,), (0,)))

    return pl.pallas_call(
        body, name=name, grid=(nj, s // tk),
        in_specs=[pl.BlockSpec((tk, tf), lambda j, k: (k, j)), pl.BlockSpec((tk, tf), lambda j, k: (k, j + nj)),
                  pl.BlockSpec((tk, d), lambda j, k: (k, 0))],
        out_specs=pl.BlockSpec((tf, d), lambda j, k: (j, 0)), out_shape=jax.ShapeDtypeStruct((f, d), F32),
        compiler_params=_params("parallel", "arbitrary"),
    )(u, u, dy)


def _of_layer(w, layer):
    return (w[layer], 0) if isinstance(w, (list, tuple)) else (w, layer)


def _layer_matrix(w, layer):
    w, at = _of_layer(w, layer)
    return w[at]


def _ffn_fwd(x, g, w_in, w_out, layer, tag, gather=None):
    (w_in, at), (w_out, _) = _of_layer(w_in, layer), _of_layer(w_out, layer)
    y, u, *gathered = _ffn_fwd_call(x, g[layer], w_in, w_out, at, name=f"{tag}_fwd", gather=gather)
    return y, (x, u), gathered


def _ffn_bwd(saved, g, w_in, w_out, layer, dy, tag, exchange=None):
    x, u = saved
    (w_in, at), (w_out, _) = _of_layer(w_in, layer), _of_layer(w_out, layer)
    du, dx, dg, h, *parts = _ffn_bwd_call(x, g[layer], u, dy, w_in, w_out, at, name=f"{tag}_bwd", exchange=exchange)
    dw_out = _ffn_dw_out(u, dy, name=f"{tag}_dwout")
    dw_in = _mm(h, du, mode="tn", name=f"{tag}_dwin")
    return dx, dg[0], dw_in, dw_out, (parts[0] if parts else None)


ATTN_SCALE = ATTN_HEAD_DIM ** -0.5
NEG_BIG = -1e30
PROJ_AB_BLOCKS = 5


def _first_head(n):
    return _cols((n, LANES)) < ATTN_HEAD_DIM


def _per_head(tiles):
    first = _first_head(tiles[0].shape[0])
    return jnp.stack([jnp.where(first == (h == 0), t, 0.0) for t in tiles for h in (0, 1)], axis=0)


def _both(tiles):
    return jnp.stack([t for t in tiles for _ in (0, 1)], axis=0)


def _head_cols(tiles):
    return jnp.stack([t[:, c0:c0 + 1] for t in tiles for c0 in (0, ATTN_HEAD_DIM)], axis=0)


def _join_heads(v):
    return [v[2 * u] + v[2 * u + 1] for u in range(v.shape[0] // 2)]


def _spread_heads(v):
    first = _first_head(v.shape[1])
    return [jnp.where(first, v[2 * u], v[2 * u + 1]) for u in range(v.shape[0] // 2)]


def _band_masks(has_prev):
    qi, kj = _rows((ATTN_BLOCK, ATTN_BLOCK)), _cols((ATTN_BLOCK, ATTN_BLOCK))
    return (kj >= qi) & has_prev, kj <= qi


def _dattn_delta(o, dcat, *, name):
    s_len = o.shape[0]
    tm = _tile(s_len, (512, 256))

    def body(o_ref, do_ref, out_ref):
        r, c = _rows((ATTN_WIDTH, ATTN_WIDTH)), _cols((ATTN_WIDTH, ATTN_WIDTH))
        ones_bd = (r // ATTN_HEAD_DIM == c // ATTN_HEAD_DIM).astype(F32)
        out_ref[...] = _dg(o_ref[...] * do_ref[...], ones_bd, ((1,), (0,)), hi=True)

    blk = pl.BlockSpec((tm, ATTN_WIDTH), lambda i: (i, 0))
    return pl.pallas_call(
        body, name=name, grid=(s_len // tm,), in_specs=[blk, blk], out_specs=blk,
        out_shape=jax.ShapeDtypeStruct((s_len, ATTN_WIDTH), F32), compiler_params=_params("parallel"),
    )(o, dcat)


ATTN_UNITS = 4


def _units(it, d):
    if d == 1:
        return [(pl.ds(0, ATTN_BLOCK), pl.ds(p * LANES, LANES)) for p in range(ATTN_UNITS)]
    return [(pl.ds(it * ATTN_UNITS + u, ATTN_BLOCK, stride=d), pl.ds(0, LANES)) for u in range(ATTN_UNITS)]


def _tiles(ref, units):
    return [ref[rows, lanes] for rows, lanes in units]


def _store_tiles(ref, units, tiles):
    for (rows, lanes), t in zip(units, tiles):
        ref[rows, lanes] = t


def _stacked(a_tiles, b_tiles):
    return [jnp.concatenate([a, b], axis=0) for a, b in zip(a_tiles, b_tiles)]


def _passes(d):
    return max(d // ATTN_UNITS, 1)


def _pairs_per_step(d):
    return ATTN_PAIRS if d == 1 else 1


def _pair_specs(d, n_of):
    pairs = _pairs_per_step(d)
    groups = ATTN_PAIRS // pairs
    return lambda c: pl.BlockSpec((ATTN_BLOCK * d, LANES * pairs), lambda n, p: (n_of(n), c * groups + p))


def _sattn_fwd(proj, state, d, *, last, name):
    s_len = proj.shape[0]
    nb = s_len // (ATTN_BLOCK * d)
    first = state is None
    n_out = 2 if last else 3

    def body(*refs):
        q_ref, kp_ref, kc_ref, vp_ref, vc_ref = refs[:5]
        st_refs = () if first else refs[5:8]
        out_refs = refs[-n_out:]
        ok = jnp.concatenate(_band_masks(pl.program_id(0) > 0), axis=1)

        def one_pass(it, carry):
            units = _units(it, d)
            kcat = _stacked(_tiles(kp_ref, units), _tiles(kc_ref, units))
            vcat = _stacked(_tiles(vp_ref, units), _tiles(vc_ref, units))
            s = jnp.where(ok, _bdg(_per_head(_tiles(q_ref, units)), _both(kcat), 2, 2) * ATTN_SCALE, NEG_BIG)
            m_new = jnp.max(s, axis=2, keepdims=True)
            if not first:
                m_old = _head_cols(_tiles(st_refs[0], units))
                m_new = jnp.maximum(m_old, m_new)
                alpha = jnp.exp(m_old - m_new)
            p = jnp.exp(s - m_new)
            l_new = jnp.sum(p, axis=2, keepdims=True)
            acc = _join_heads(_bdg(p, _per_head(vcat), 2, 1))
            if not first:
                l_new = l_new + _head_cols(_tiles(st_refs[1], units)) * alpha
                acc = [a + a_in * sp for a, a_in, sp in zip(acc, _tiles(st_refs[2], units), _spread_heads(alpha))]
            m_pair, l_pair = _spread_heads(m_new), _spread_heads(l_new)
            if last:
                _store_tiles(out_refs[0], units, [a / l for a, l in zip(acc, l_pair)])
                _store_tiles(out_refs[1], units, [m + jnp.log(l) for m, l in zip(m_pair, l_pair)])
            else:
                _store_tiles(out_refs[0], units, m_pair)
                _store_tiles(out_refs[1], units, l_pair)
                _store_tiles(out_refs[2], units, acc)
            return carry

        lax.fori_loop(0, _passes(d), one_pass, 0)

    cur, prev = _pair_specs(d, lambda n: n), _pair_specs(d, lambda n: jnp.maximum(n - 1, 0))
    st = cur(0)
    return tuple(pl.pallas_call(
        body, name=name, grid=(nb, ATTN_PAIRS // _pairs_per_step(d)),
        in_specs=[cur(0), prev(1), cur(1), prev(2), cur(2)] + ([] if first else [st] * 3),
        out_specs=[st] * n_out, out_shape=[jax.ShapeDtypeStruct((s_len, ATTN_WIDTH), F32)] * n_out,
        compiler_params=_params("arbitrary", "parallel"),
    )(*([proj] * 5 + ([] if first else list(state)))))


def _dattn_forward(proj, tag):
    state = None
    for i, d in enumerate(DILATIONS):
        state = _sattn_fwd(proj, state, d, last=i == len(DILATIONS) - 1, name=f"{tag}_attn_d{d}")
    return state


def _sattn_bwd(proj, dcat, lse, delta, grads_in, d, *, name):
    s_len = proj.shape[0]
    nb = s_len // (ATTN_BLOCK * d)
    first = grads_in is None
    groups = ATTN_PAIRS // _pairs_per_step(d)

    def body(*refs):
        q_ref, kp_ref, kc_ref, vp_ref, vc_ref, do_ref, lse_ref, dl_ref = refs[:8]
        dq_in, dk_in, dv_in = (None, None, None) if first else refs[8:11]
        dq_ref, dk_ref, dv_ref, carry_k, carry_v = refs[-5:]
        n = pl.program_id(1)
        ok = jnp.concatenate(_band_masks(n > 0), axis=1)

        @pl.when(n == 0)
        def _():
            carry_k[...] = jnp.zeros_like(carry_k)
            carry_v[...] = jnp.zeros_like(carry_v)

        def leave(ref, carry, units, extra, into):
            out = [c + e for c, e in zip(_tiles(carry, units), extra)] if extra else _tiles(carry, units)
            if into is not None:
                out = [a + b for a, b in zip(out, _tiles(into, units))]
            _store_tiles(ref, units, out)

        def one_pass(it, carry):
            units = _units(it, d)
            kcat = _stacked(_tiles(kp_ref, units), _tiles(kc_ref, units))
            vcat = _stacked(_tiles(vp_ref, units), _tiles(vc_ref, units))
            q2, do2 = _per_head(_tiles(q_ref, units)), _per_head(_tiles(do_ref, units))
            s = _bdg(q2, _both(kcat), 2, 2) * ATTN_SCALE
            pr = jnp.where(ok, jnp.exp(jnp.where(ok, s, NEG_BIG) - _head_cols(_tiles(lse_ref, units))), 0.0)
            ds = pr * (_bdg(do2, _both(vcat), 2, 2) - _head_cols(_tiles(dl_ref, units)))
            dq = [t * ATTN_SCALE for t in _join_heads(_bdg(ds, _per_head(kcat), 2, 1))]
            if not first:
                dq = [a + b for a, b in zip(dq, _tiles(dq_in, units))]
            _store_tiles(dq_ref, units, dq)
            dk = [t * ATTN_SCALE for t in _join_heads(_bdg(ds, q2, 1, 1))]
            dv = _join_heads(_bdg(pr, do2, 1, 1))
            leave(dk_ref, carry_k, units, [t[:ATTN_BLOCK] for t in dk], dk_in)
            leave(dv_ref, carry_v, units, [t[:ATTN_BLOCK] for t in dv], dv_in)
            _store_tiles(carry_k, units, [t[ATTN_BLOCK:] for t in dk])
            _store_tiles(carry_v, units, [t[ATTN_BLOCK:] for t in dv])
            return carry

        def last_pass(it, carry):
            units = _units(it, d)
            leave(dk_ref, carry_k, units, None, dk_in)
            leave(dv_ref, carry_v, units, None, dv_in)
            return carry

        @pl.when(n < nb)
        def _():
            lax.fori_loop(0, _passes(d), one_pass, 0)

        @pl.when(n == nb)
        def _():
            lax.fori_loop(0, _passes(d), last_pass, 0)

    pairs = _pairs_per_step(d)
    blk = (ATTN_BLOCK * d, LANES * pairs)
    at = lambda n_of: (lambda c: pl.BlockSpec(blk, lambda p, n: (n_of(n), c * groups + p)))
    here = lambda n: jnp.minimum(n, nb - 1)
    cur, prev, lag = at(here), at(lambda n: jnp.maximum(here(n) - 1, 0)), at(lambda n: jnp.maximum(n - 1, 0))
    st, st_lag = cur(0), lag(0)
    return tuple(pl.pallas_call(
        body, name=name, grid=(groups, nb + 1),
        in_specs=[cur(0), prev(1), cur(1), prev(2), cur(2), st, st, st] + ([] if first else [st, st_lag, st_lag]),
        out_specs=[st, st_lag, st_lag], out_shape=[jax.ShapeDtypeStruct((s_len, ATTN_WIDTH), F32)] * 3,
        scratch_shapes=[pltpu.VMEM(blk, F32)] * 2, compiler_params=_params("parallel", "arbitrary"),
    )(*([proj] * 5 + [dcat, lse, delta] + ([] if first else list(grads_in)))))


def _dattn_backward(proj, o, lse, dcat, tag):
    delta = _dattn_delta(o, dcat, name=f"{tag}_attn_delta")
    grads = None
    for d in DILATIONS:
        grads = _sattn_bwd(proj, dcat, lse, delta, grads, d, name=f"{tag}_attn_bwd_d{d}")
    return grads


CONV_TC = 512
CONV_T = 512


def _conv_tiles(s_len, cb0, width):
    wide = 2 * CONV_TC
    tc = wide if width % wide == 0 and (cb0 * CONV_TC) % wide == 0 else CONV_TC
    return _tile(s_len, (CONV_T, CONV_T // 2)), tc, cb0 * CONV_TC // tc


def _shift_down(ext, k, t):
    return (pltpu.roll(ext, k, 0) if k else ext)[SUBLANES:SUBLANES + t]


def _conv_fwd(src, cb0, width, w8, *, name):
    s_len = src.shape[0]
    t, tc, cb = _conv_tiles(s_len, cb0, width)
    tpb = t // SUBLANES

    def body(x_ref, h_ref, w_ref, y_ref):
        halo = jnp.where(pl.program_id(0) > 0, h_ref[...], 0.0)
        ext = jnp.concatenate([halo, x_ref[...]], axis=0)
        w = w_ref[...]
        y = jnp.broadcast_to(w[CONV_K:CONV_K + 1], (t, tc))
        for k in range(CONV_K):
            y = y + w[k:k + 1] * _shift_down(ext, CONV_K - 1 - k, t)
        y_ref[...] = y

    return pl.pallas_call(
        body, name=name, grid=(s_len // t, width // tc),
        in_specs=[pl.BlockSpec((t, tc), lambda i, j: (i, cb + j)),
                  pl.BlockSpec((SUBLANES, tc), lambda i, j: (jnp.maximum(i * tpb - 1, 0), cb + j)),
                  pl.BlockSpec((SUBLANES, tc), lambda i, j: (0, j))],
        out_specs=pl.BlockSpec((t, tc), lambda i, j: (i, j)), out_shape=jax.ShapeDtypeStruct((s_len, width), F32),
        compiler_params=_params("parallel", "parallel"),
    )(src, src, w8)


def _conv_bwd(src, cb0, width, w8, dy, *, name):
    s_len = src.shape[0]
    t, tc, cb = _conv_tiles(s_len, cb0, width)
    tpb = t // SUBLANES
    ni = s_len // t

    def body(x_ref, h_ref, w_ref, dy_ref, dn_ref, dx_ref, dw_ref):
        i = pl.program_id(1)
        halo = jnp.where(i > 0, h_ref[...], 0.0)
        ext = jnp.concatenate([halo, x_ref[...]], axis=0)
        dyv = dy_ref[...]
        extn = jnp.concatenate([dyv, jnp.where(i < ni - 1, dn_ref[...], 0.0)], axis=0)
        w = w_ref[...]
        row = _rows((SUBLANES, tc))
        dx = jnp.zeros((t, tc), F32)
        dw = jnp.where(row == CONV_K, jnp.sum(dyv, axis=0, keepdims=True), 0.0)
        for k in range(CONV_K):
            up = CONV_K - 1 - k
            dx = dx + w[k:k + 1] * (pltpu.roll(extn, t + SUBLANES - up, 0) if up else extn)[:t]
            dw = dw + jnp.where(row == k, jnp.sum(dyv * _shift_down(ext, up, t), axis=0, keepdims=True), 0.0)
        dx_ref[...] = dx.astype(dx_ref.dtype)

        @pl.when(i == 0)
        def _():
            dw_ref[...] = jnp.zeros_like(dw_ref)

        dw_ref[...] += dw

    return pl.pallas_call(
        body, name=name, grid=(width // tc, ni),
        in_specs=[pl.BlockSpec((t, tc), lambda j, i: (i, cb + j)),
                  pl.BlockSpec((SUBLANES, tc), lambda j, i: (jnp.maximum(i * tpb - 1, 0), cb + j)),
                  pl.BlockSpec((SUBLANES, tc), lambda j, i: (0, j)),
                  pl.BlockSpec((t, tc), lambda j, i: (i, j)),
                  pl.BlockSpec((SUBLANES, tc), lambda j, i: (jnp.minimum((i + 1) * tpb, s_len // SUBLANES - 1), j))],
        out_specs=[pl.BlockSpec((t, tc), lambda j, i: (i, j)), pl.BlockSpec((SUBLANES, tc), lambda j, i: (0, j))],
        out_shape=[jax.ShapeDtypeStruct((s_len, width), MXU_DTYPE), jax.ShapeDtypeStruct((SUBLANES, width), F32)],
        compiler_params=_params("parallel", "arbitrary"),
    )(src, src, w8, dy, dy)


LRU_T = 256


def _lru_gates(xc, wa, wx, ba, bx, lam):
    r = _sigmoid(_bdot(xc, wa) + ba)
    i = _sigmoid(_bdot(xc, wx) + bx)
    log_a = (-LRU_C) * r * _softplus(-lam)
    return jnp.exp(log_a), jnp.sqrt(_neg_expm1(2.0 * log_a)) * i * xc


def _block_scan(a, b, state, reverse):
    t = a.shape[0]
    row = _rows(a.shape) % SUBLANES
    s = 1
    while s < SUBLANES:
        shift, ok = (t - s, row < SUBLANES - s) if reverse else (s, row >= s)
        b = jnp.where(ok, a * pltpu.roll(b, shift, 0) + b, b)
        a = jnp.where(ok, a * pltpu.roll(a, shift, 0), a)
        s *= 2
    groups = range(t // SUBLANES)
    out = [None] * len(groups)
    for g in (reversed(groups) if reverse else groups):
        rows = slice(g * SUBLANES, (g + 1) * SUBLANES)
        out[g] = b[rows] + a[rows] * state
        state = out[g][0:1] if reverse else out[g][SUBLANES - 1:SUBLANES]
    return jnp.concatenate(out, axis=0)


def _lru_fwd(xc, proj, wa, wx, ba, bx, lam, *, name):
    s_len, w = xc.shape
    t = _tile(s_len, (LRU_T,))

    def body(xc_ref, gr_ref, wa_ref, wx_ref, ba_ref, bx_ref, lam_ref, h_ref, y_ref, carry):
        @pl.when(pl.program_id(0) == 0)
        def _():
            carry[...] = jnp.zeros_like(carry)

        a, b = _lru_gates(xc_ref[...], wa_ref[...], wx_ref[...], ba_ref[...], bx_ref[...], lam_ref[...])
        h = _block_scan(a, b, carry[0:1, :], False)
        h_ref[...] = h
        y_ref[...] = (h * _gelu(gr_ref[...])).astype(y_ref.dtype)
        carry[0:1, :] = h[t - 1:t, :]

    row = pl.BlockSpec((t, w), lambda i: (i, 0))
    mat = pl.BlockSpec((w, w), lambda i: (0, 0))
    vec = pl.BlockSpec((1, w), lambda i: (0, 0))
    return pl.pallas_call(
        body, name=name, grid=(s_len // t,),
        in_specs=[row, pl.BlockSpec((t, w), lambda i: (i, PROJ_AB_BLOCKS - 1)), mat, mat, vec, vec, vec],
        out_specs=[row, row], out_shape=[jax.ShapeDtypeStruct((s_len, w), F32), jax.ShapeDtypeStruct((s_len, w), MXU_DTYPE)],
        scratch_shapes=[pltpu.VMEM((SUBLANES, w), F32)], compiler_params=_params("arbitrary"),
    )(xc, proj, wa, wx, ba, bx, lam)


def _lru_bwd(xc, proj, hs, dcat, wa, wx, ba, bx, lam, *, name):
    s_len, w = xc.shape
    t = _tile(s_len, (LRU_T,))
    nb = s_len // t
    tpb = t // SUBLANES

    def body(xc_ref, gr_ref, h_ref, hp_ref, dy_ref, wa_ref, wx_ref, ba_ref, bx_ref, lam_ref,
             dxc_ref, dgr_ref, dwa_ref, dwx_ref, dba_ref, dbx_ref, dlam_ref, carry):
        step = pl.program_id(0)
        params = (wa_ref[...], wx_ref[...], ba_ref[...], bx_ref[...], lam_ref[...])

        @pl.when(step == 0)
        def _():
            carry[...] = jnp.zeros_like(carry)
            for r in (dwa_ref, dwx_ref, dba_ref, dbx_ref, dlam_ref):
                r[...] = jnp.zeros_like(r)

        (a, _), vjp = jax.vjp(_lru_gates, xc_ref[...], *params)
        gr, h, dy = gr_ref[...], h_ref[...], dy_ref[...]
        gel, gel_vjp = jax.vjp(_gelu, gr)
        dgr_ref[...] = gel_vjp(dy * h)[0].astype(dgr_ref.dtype)
        dh = dy * gel
        big_g = _block_scan(a, a * dh, carry[0:1, :], True)
        row = _rows((t, w))
        g = dh + jnp.where(row == t - 1, carry[0:1, :], pltpu.roll(big_g, t - 1, 0))
        carry[0:1, :] = big_g[0:1, :]
        h_last = jnp.where(step < nb - 1, hp_ref[SUBLANES - 1:SUBLANES, :], 0.0)
        h_prev = jnp.where(row == 0, h_last, pltpu.roll(h, 1, 0))
        dxc, dwa, dwx, dba, dbx, dlam = vjp((g * h_prev, g))
        dxc_ref[...] = dxc
        dwa_ref[...] += dwa
        dwx_ref[...] += dwx
        dba_ref[...] += dba
        dbx_ref[...] += dbx
        dlam_ref[...] += dlam

    rev = lambda i: nb - 1 - i
    row = pl.BlockSpec((t, w), lambda i: (rev(i), 0))
    mat = pl.BlockSpec((w, w), lambda i: (0, 0))
    vec = pl.BlockSpec((1, w), lambda i: (0, 0))
    return pl.pallas_call(
        body, name=name, grid=(nb,),
        in_specs=[row, pl.BlockSpec((t, w), lambda i: (rev(i), PROJ_AB_BLOCKS - 1)), row,
                  pl.BlockSpec((SUBLANES, w), lambda i: (jnp.maximum(rev(i) * tpb - 1, 0), 0)),
                  pl.BlockSpec((t, w), lambda i: (rev(i), 1)), mat, mat, vec, vec, vec],
        out_specs=[row, row, mat, mat, vec, vec, vec],
        out_shape=[jax.ShapeDtypeStruct((s_len, w), F32), jax.ShapeDtypeStruct((s_len, w), MXU_DTYPE)]
        + [jax.ShapeDtypeStruct((w, w), F32)] * 2 + [jax.ShapeDtypeStruct((1, w), F32)] * 3,
        scratch_shapes=[pltpu.VMEM((SUBLANES, w), F32)], compiler_params=_params("arbitrary"),
    )(xc, proj, hs, hs, dcat, wa, wx, ba, bx, lam)


XA_T = 256
XA_SCALE = XA_HEAD_DIM ** -0.5


def _xa_heads(q, k, v):
    s = _bmm_nt(q, k) * XA_SCALE
    e = jnp.exp(s - jnp.max(s, axis=-1, keepdims=True))
    return _bmm(e / jnp.sum(e, axis=-1, keepdims=True), v)


def _xa_stack(ref):
    return jnp.stack([ref[:, h * XA_HEAD_DIM:(h + 1) * XA_HEAD_DIM].astype(F32) for h in range(XA_HEADS)], axis=0)


def _xa_fwd(q, kv, *, name):
    s_len, d = q.shape
    n_mem = kv.shape[0]
    t = _tile(s_len, (XA_T,))

    def body(q_ref, k_ref, v_ref, o_ref):
        o = _xa_heads(_xa_stack(q_ref), _xa_stack(k_ref), _xa_stack(v_ref))
        for h in range(XA_HEADS):
            o_ref[:, h * XA_HEAD_DIM:(h + 1) * XA_HEAD_DIM] = o[h].astype(o_ref.dtype)

    return pl.pallas_call(
        body, name=name, grid=(s_len // t,),
        in_specs=[pl.BlockSpec((t, d), lambda i: (i, 0)), pl.BlockSpec((n_mem, d), lambda i: (0, 0)),
                  pl.BlockSpec((n_mem, d), lambda i: (0, 1))],
        out_specs=pl.BlockSpec((t, d), lambda i: (i, 0)), out_shape=jax.ShapeDtypeStruct((s_len, d), MXU_DTYPE),
        compiler_params=_params("parallel"),
    )(q, kv, kv)


def _xa_bwd(q, kv, dy, wo, *, name):
    s_len, d = q.shape
    n_mem = kv.shape[0]
    t = _tile(s_len, (XA_T,))

    def body(q_ref, k_ref, v_ref, dy_ref, wo_ref, dq_ref, dk_ref, dv_ref):
        @pl.when(pl.program_id(0) == 0)
        def _():
            dk_ref[...] = jnp.zeros_like(dk_ref)
            dv_ref[...] = jnp.zeros_like(dv_ref)

        do = _dg(dy_ref[...], wo_ref[...], ((1,), (1,)))
        do = jnp.stack([do[:, h * XA_HEAD_DIM:(h + 1) * XA_HEAD_DIM] for h in range(XA_HEADS)], axis=0)
        _, vjp = jax.vjp(_xa_heads, _xa_stack(q_ref), _xa_stack(k_ref), _xa_stack(v_ref))
        dq, dk, dv = vjp(do)
        for h in range(XA_HEADS):
            sl = slice(h * XA_HEAD_DIM, (h + 1) * XA_HEAD_DIM)
            dq_ref[:, sl] = dq[h].astype(dq_ref.dtype)
            dk_ref[:, sl] += dk[h]
            dv_ref[:, sl] += dv[h]

    row = pl.BlockSpec((t, d), lambda i: (i, 0))
    dq, dk, dv = pl.pallas_call(
        body, name=name, grid=(s_len // t,),
        in_specs=[row, pl.BlockSpec((n_mem, d), lambda i: (0, 0)), pl.BlockSpec((n_mem, d), lambda i: (0, 1)), row,
                  pl.BlockSpec(wo.shape, lambda i: (0, 0), pipeline_mode=RESIDENT)],
        out_specs=[row, pl.BlockSpec((n_mem, d), lambda i: (0, 0)), pl.BlockSpec((n_mem, d), lambda i: (0, 0))],
        out_shape=[jax.ShapeDtypeStruct((s_len, d), MXU_DTYPE)] + [jax.ShapeDtypeStruct((n_mem, d), F32)] * 2,
        compiler_params=_params("arbitrary"),
    )(q, kv, kv, dy, wo)
    return dq, jnp.concatenate([dk, dv], axis=1)


DN_Q_SCALE = DN_HEAD_DIM ** -0.5
L2_EPS = 1e-6


def _bdg(a, b, ca, cb):
    return lax.dot_general(a.astype(MXU_DTYPE), b.astype(MXU_DTYPE), (((ca,), (cb,)), ((0,), (0,))), preferred_element_type=F32)


@jax.custom_vjp
def _bmm(a, b):
    return _bdg(a, b, 2, 1)


_bmm.defvjp(lambda a, b: (_bdg(a, b, 2, 1), (a, b)), lambda r, g: (_bdg(g, r[1], 2, 2), _bdg(r[0], g, 1, 1)))


@jax.custom_vjp
def _bmm_nt(a, b):
    return _bdg(a, b, 2, 2)


_bmm_nt.defvjp(lambda a, b: (_bdg(a, b, 2, 2), (a, b)), lambda r, g: (_bdg(g, r[1], 2, 1), _bdg(g, r[0], 1, 1)))


@jax.custom_vjp
def _bmm_tn(a, b):
    return _bdg(a, b, 1, 1)


_bmm_tn.defvjp(lambda a, b: (_bdg(a, b, 1, 1), (a, b)), lambda r, g: (_bdg(r[1], g, 2, 2), _bdg(r[0], g, 2, 1)))


def _tri_inverse(n):
    eye = (lax.broadcasted_iota(jnp.int32, n.shape, 1) == lax.broadcasted_iota(jnp.int32, n.shape, 2)).astype(F32)
    inv, p = eye - n, n
    for _ in range(5):
        p = _bdg(p, p, 2, 1)
        inv = _bdg(inv, eye + p, 2, 1)
    return inv


@jax.custom_vjp
def _tri_solve2(n, r1, r2):
    t = _tri_inverse(n)
    return _bdg(t, r1, 2, 1), _bdg(t, r2, 2, 1)


def _tri_solve2_fwd(n, r1, r2):
    t = _tri_inverse(n)
    x1, x2 = _bdg(t, r1, 2, 1), _bdg(t, r2, 2, 1)
    return (x1, x2), (t, x1, x2)


def _tri_solve2_bwd(saved, cts):
    t, x1, x2 = saved
    d1, d2 = _bdg(t, cts[0], 1, 1), _bdg(t, cts[1], 1, 1)
    return -(_bdg(d1, x1, 2, 2) + _bdg(d2, x2, 2, 2)), d1, d2


_tri_solve2.defvjp(_tri_solve2_fwd, _tri_solve2_bwd)


def _dn_gates(ab, alog, dtb):
    return -jnp.exp(alog) * _softplus(ab + dtb), _sigmoid(ab)


def _dn_heads(cq, ck, cv, z, g, beta, onorm, state):
    h, c, _ = cq.shape
    l2 = lambda t: t * lax.rsqrt(jnp.sum(t * t, axis=-1, keepdims=True) + L2_EPS)
    q, k, v = l2(_silu(cq)) * DN_Q_SCALE, l2(_silu(ck)), _silu(cv)
    r, cc = lax.broadcasted_iota(jnp.int32, (h, c, c), 1), lax.broadcasted_iota(jnp.int32, (h, c, c), 2)
    tri, eye = r >= cc, r == cc
    g_row = jnp.sum(jnp.where(eye, g, 0.0), axis=1, keepdims=True)
    gcum_c = jnp.sum(jnp.where(tri, g_row, 0.0), axis=2, keepdims=True)
    gcum_r = jnp.sum(jnp.where(cc >= r, g, 0.0), axis=1, keepdims=True)
    decay = jnp.where(tri, jnp.exp(jnp.where(tri, gcum_c - gcum_r, 0.0)), 0.0)
    kb = k * beta
    n = jnp.where(r > cc, _bmm_nt(kb, k) * decay, 0.0)
    u, w = _tri_solve2(n, v * beta, kb * jnp.exp(gcum_c))
    v_new = u - _bmm(w, state)
    o = _bmm(q * jnp.exp(gcum_c), state) + _bmm(_bmm_nt(q, k) * decay, v_new)
    g_last = jnp.sum(g, axis=1, keepdims=True)
    new_state = state * jnp.exp(g_last) + _bmm_tn(k * jnp.exp(g_last - gcum_c), v_new)
    on = o * lax.rsqrt(jnp.mean(o * o, axis=-1, keepdims=True) + NORM_EPS) * onorm
    return on * _silu(z), new_state


DN_STEP = 2


def _dn_stack(ref, col0, rows=slice(None)):
    return jnp.stack([ref[rows, col0 + h * DN_HEAD_DIM:col0 + (h + 1) * DN_HEAD_DIM].astype(F32) for h in range(DN_HEADS)], axis=0)


def _dn_cols(block, col0):
    return jnp.stack([block[:, col0 + h:col0 + h + 1] for h in range(DN_HEADS)], axis=0)


def _dn_fwd(cqkv, proj, ab, alog, dtb, onorm, *, name):
    s_len = cqkv.shape[0]
    c, hd, w = DN_CHUNK, DN_HEAD_DIM, DN_WIDTH
    n_chunks = s_len // c

    def body(c_ref, z_ref, ab_ref, alog_ref, dtb_ref, on_ref, o_ref, st_ref, state):
        @pl.when(pl.program_id(0) == 0)
        def _():
            state[...] = jnp.zeros_like(state)

        g_all, beta_all = _dn_gates(ab_ref[...], alog_ref[...], dtb_ref[...])
        st = state[...]
        for j in range(DN_STEP):
            rows = slice(j * c, (j + 1) * c)
            st_ref[j] = st
            out, st = _dn_heads(_dn_stack(c_ref, 0, rows), _dn_stack(c_ref, w, rows), _dn_stack(c_ref, 2 * w, rows),
                                _dn_stack(z_ref, 0, rows), _dn_cols(g_all[rows], 0), _dn_cols(beta_all[rows], DN_HEADS), on_ref[...], st)
            for h in range(DN_HEADS):
                o_ref[rows, h * hd:(h + 1) * hd] = out[h].astype(o_ref.dtype)
        state[...] = st

    vec = pl.BlockSpec((1, LANES), lambda i: (0, 0))
    t = DN_STEP * c
    return pl.pallas_call(
        body, name=name, grid=(n_chunks // DN_STEP,),
        in_specs=[pl.BlockSpec((t, 3 * w), lambda i: (i, 0)), pl.BlockSpec((t, w), lambda i: (i, 3)),
                  pl.BlockSpec((t, LANES), lambda i: (i, 0)), vec, vec, vec],
        out_specs=[pl.BlockSpec((t, w), lambda i: (i, 0)), pl.BlockSpec((DN_STEP, DN_HEADS, hd, hd), lambda i: (i, 0, 0, 0))],
        out_shape=[jax.ShapeDtypeStruct((s_len, w), MXU_DTYPE), jax.ShapeDtypeStruct((n_chunks, DN_HEADS, hd, hd), F32)],
        scratch_shapes=[pltpu.VMEM((DN_HEADS, hd, hd), F32)], compiler_params=_params("arbitrary"),
    )(cqkv, proj, ab, alog, dtb, onorm)


def _dn_bwd(cqkv, proj, ab, alog, dtb, onorm, states, dout, *, name):
    s_len = cqkv.shape[0]
    c, hd, w = DN_CHUNK, DN_HEAD_DIM, DN_WIDTH
    n_chunks = s_len // c

    def body(c_ref, z_ref, ab_ref, alog_ref, dtb_ref, on_ref, st_ref, do_ref,
             dc_ref, dz_ref, dab_ref, dalog_ref, ddtb_ref, don_ref, dstate):
        @pl.when(pl.program_id(0) == 0)
        def _():
            dstate[...] = jnp.zeros_like(dstate)
            for r in (dalog_ref, ddtb_ref, don_ref):
                r[...] = jnp.zeros_like(r)

        (g_all, beta_all), gates_vjp = jax.vjp(_dn_gates, ab_ref[...], alog_ref[...], dtb_ref[...])
        dst = dstate[...]
        col = _cols((c, LANES))
        dg_parts, dbeta_parts = [None] * DN_STEP, [None] * DN_STEP
        don = jnp.zeros((1, hd), F32)
        for j in reversed(range(DN_STEP)):
            rows = slice(j * c, (j + 1) * c)
            _, vjp = jax.vjp(_dn_heads, _dn_stack(c_ref, 0, rows), _dn_stack(c_ref, w, rows), _dn_stack(c_ref, 2 * w, rows),
                             _dn_stack(z_ref, 0, rows), _dn_cols(g_all[rows], 0), _dn_cols(beta_all[rows], DN_HEADS), on_ref[...], st_ref[j])
            dcq, dck, dcv, dz, dg, dbeta, don_j, dst = vjp((_dn_stack(do_ref, 0, rows), dst))
            dg_all, dbeta_all = jnp.zeros((c, LANES), F32), jnp.zeros((c, LANES), F32)
            for h in range(DN_HEADS):
                sl = slice(h * hd, (h + 1) * hd)
                dc_ref[rows, sl] = dcq[h]
                dc_ref[rows, w + h * hd:w + (h + 1) * hd] = dck[h]
                dc_ref[rows, 2 * w + h * hd:2 * w + (h + 1) * hd] = dcv[h]
                dz_ref[rows, sl] = dz[h].astype(dz_ref.dtype)
                dg_all = dg_all + jnp.where(col == h, dg[h], 0.0)
                dbeta_all = dbeta_all + jnp.where(col == DN_HEADS + h, dbeta[h], 0.0)
            dg_parts[j], dbeta_parts[j] = dg_all, dbeta_all
            don = don + don_j
        dstate[...] = dst
        dab, dalog, ddtb = gates_vjp((jnp.concatenate(dg_parts, axis=0), jnp.concatenate(dbeta_parts, axis=0)))
        dab_ref[...] = dab
        dalog_ref[...] += dalog
        ddtb_ref[...] += ddtb
        don_ref[...] += don

    steps = n_chunks // DN_STEP
    rev = lambda i: steps - 1 - i
    vec = pl.BlockSpec((1, LANES), lambda i: (0, 0))
    t = DN_STEP * c
    return pl.pallas_call(
        body, name=name, grid=(steps,),
        in_specs=[pl.BlockSpec((t, 3 * w), lambda i: (rev(i), 0)), pl.BlockSpec((t, w), lambda i: (rev(i), 3)),
                  pl.BlockSpec((t, LANES), lambda i: (rev(i), 0)), vec, vec, vec,
                  pl.BlockSpec((DN_STEP, DN_HEADS, hd, hd), lambda i: (rev(i), 0, 0, 0)), pl.BlockSpec((t, w), lambda i: (rev(i), 0))],
        out_specs=[pl.BlockSpec((t, 3 * w), lambda i: (rev(i), 0)), pl.BlockSpec((t, w), lambda i: (rev(i), 0)),
                   pl.BlockSpec((t, LANES), lambda i: (rev(i), 0)), vec, vec, vec],
        out_shape=[jax.ShapeDtypeStruct((s_len, 3 * w), F32), jax.ShapeDtypeStruct((s_len, w), MXU_DTYPE),
                   jax.ShapeDtypeStruct((s_len, LANES), F32)] + [jax.ShapeDtypeStruct((1, LANES), F32)] * 3,
        scratch_shapes=[pltpu.VMEM((DN_HEADS, hd, hd), F32)], compiler_params=_params("arbitrary"),
    )(cqkv, proj, ab, alog, dtb, onorm, states, dout)


def _final_loss(x, g, target, *, name):
    s, d = x.shape
    tm = _tile(s, (512, 256))

    def body(x_ref, g_ref, t_ref, loss_ref, dx_ref, dg_ref):
        @pl.when(pl.program_id(0) == 0)
        def _():
            loss_ref[...] = jnp.zeros_like(loss_ref)
            dg_ref[...] = jnp.zeros_like(dg_ref)

        xv, gv = x_ref[...], g_ref[...]
        r = lax.rsqrt(jnp.mean(xv * xv, axis=-1, keepdims=True) + NORM_EPS)
        xh = xv * r
        err = xh * gv - t_ref[...]
        loss_ref[...] += 0.5 * jnp.sum(jnp.mean(err * err, axis=-1, keepdims=True), axis=0, keepdims=True)
        dy = err * (1.0 / d)
        dxh = dy * gv
        dx_ref[...] = r * (dxh - xh * jnp.mean(dxh * xh, axis=-1, keepdims=True))
        dg_ref[...] += jnp.sum(dy * xh, axis=0, keepdims=True)

    row = pl.BlockSpec((tm, d), lambda i: (i, 0))
    vec = pl.BlockSpec((1, d), lambda i: (0, 0))
    return pl.pallas_call(
        body, name=name, grid=(s // tm,), in_specs=[row, vec, row],
        out_specs=[pl.BlockSpec((1, LANES), lambda i: (0, 0)), row, vec],
        out_shape=[jax.ShapeDtypeStruct((1, LANES), F32), jax.ShapeDtypeStruct((s, d), F32), jax.ShapeDtypeStruct((1, d), F32)],
        compiler_params=_params("arbitrary"),
    )(x, g.reshape(1, d), target)


def _adamw(w, g_layers, m, v, *, name):
    shape = w.shape
    cols = shape[-1]
    n_l = len(g_layers)
    rows = max(w.size // cols, 1) // n_l
    tr = _tile(rows, [t for t in (512, 352, 256, 128, 64, 32, 16, 8) if t * cols * 4 <= ADAMW_BLOCK_BYTES])
    c1, c2 = 1.0 - ADAM_B1 ** ADAM_STEP, 1.0 - ADAM_B2 ** ADAM_STEP

    def body(*refs):
        w_ref, m_ref, v_ref = refs[:3]
        d_ref, nm_ref, nv_ref, g_ref = refs[3 + n_l:]
        gv = refs[3][...]
        for k in range(1, n_l):
            gv = jnp.where(pl.program_id(0) == k, refs[3 + k][...], gv)
        nm = ADAM_B1 * m_ref[...] + (1.0 - ADAM_B1) * gv
        nv = ADAM_B2 * v_ref[...] + (1.0 - ADAM_B2) * (gv * gv)
        d_ref[...] = -ADAM_LR * ((nm / c1) / (jnp.sqrt(nv / c2) + ADAM_EPS) + ADAM_WD * w_ref[...])
        nm_ref[...] = nm
        nv_ref[...] = nv
        g_ref[...] = gv

    blk = pl.BlockSpec((None, tr, cols), lambda l, i: (l, i, 0))
    slab = pl.BlockSpec((tr, cols), lambda l, i: (i, 0))
    outs = pl.pallas_call(
        body, name=name, grid=(n_l, rows // tr), in_specs=[blk] * 3 + [slab] * n_l, out_specs=[blk] * 4,
        out_shape=[jax.ShapeDtypeStruct((n_l, rows, cols), F32)] * 4, compiler_params=_params("parallel", "parallel"),
    )(*(t.reshape(n_l, rows, cols) for t in (w, m, v)), *(t.reshape(rows, cols) for t in g_layers))
    return tuple(t.reshape(shape) for t in outs)


def _block_diag(w):
    n, j, k = w.shape
    eye = jnp.eye(n, dtype=w.dtype)
    return (eye[:, None, :, None] * w[:, :, None, :]).reshape(n * j, n * k)


def _block_diag_part(m, n):
    j, k = m.shape[0] // n, m.shape[1] // n
    m4 = m.reshape(n, j, n, k)
    return jnp.stack([m4[i, :, i, :] for i in range(n)], axis=0)


DN_AB = 2 * DN_HEADS
DEPTH = 2


def _row(v, width=None):
    v = v.reshape(1, -1)
    return v if width is None else jnp.pad(v, ((0, 0), (0, width - v.shape[1])))


def _conv_w8(conv_w, bias=None):
    w8 = jnp.zeros((SUBLANES, conv_w.shape[1]), F32).at[:CONV_K].set(conv_w)
    return w8 if bias is None else w8.at[CONV_K].set(bias)


def _mixer_ab_fwd(x, w, tag):
    h, (proj,) = _norm_mm(x, w["mix_norm"][0], [w["ab_w_in"][0]], [F32], name=f"{tag}_in")
    o, lse = _dattn_forward(proj, tag)
    w8 = _conv_w8(w["lru_conv_w"][0], w["lru_conv_b"][0])
    xc = _conv_fwd(proj, PROJ_AB_BLOCKS - 2, LRU_WIDTH, w8, name=f"{tag}_conv")
    wa, wx = _block_diag(w["lru_w_a"][0]), _block_diag(w["lru_w_x"][0])
    vecs = (_row(w["lru_b_a"][0]), _row(w["lru_b_x"][0]), _row(w["lru_lambda"][0]))
    hs, y = _lru_fwd(xc, proj, wa, wx, *vecs, name=f"{tag}_lru")
    w_out = w["ab_w_out"][0]
    x2 = _mm(o, w_out[:ATTN_WIDTH], res=x, name=f"{tag}_out_attn")
    x2 = _mm(y, w_out[ATTN_WIDTH:], res=x2, name=f"{tag}_out_lru")
    return x2, (x, h, proj, o, lse, w8, xc, wa, wx, vecs, hs, y)


def _mixer_ab_bwd(saved, w, dy, tag):
    x, h, proj, o, lse, w8, xc, wa, wx, vecs, hs, y = saved
    w_out = w["ab_w_out"][0]
    dcat = _mm(dy, w_out, mode="nt", name=f"{tag}_dcat")
    dw_out = jnp.concatenate([_mm(o, dy, mode="tn", name=f"{tag}_dwout_attn"), _mm(y, dy, mode="tn", name=f"{tag}_dwout_lru")], axis=0)
    dq, dk, dv = _dattn_backward(proj, o, lse, dcat, tag)
    dxc, dgr, dwa, dwx, dba, dbx, dlam = _lru_bwd(xc, proj, hs, dcat, wa, wx, *vecs, name=f"{tag}_dlru")
    dxr, dw8 = _conv_bwd(proj, PROJ_AB_BLOCKS - 2, LRU_WIDTH, w8, dxc, name=f"{tag}_dconv")
    dproj = jnp.concatenate([t.astype(MXU_DTYPE) for t in (dq, dk, dv, dxr, dgr)], axis=1)
    dw_in = _mm(h, dproj, mode="tn", name=f"{tag}_dwin")
    dx, dg = _mm_rms_bwd([(dproj, w["ab_w_in"][0])], x, w["mix_norm"][0], dy, name=f"{tag}_dh")
    grads = dict(mix_norm=dg[0], ab_w_in=dw_in, ab_w_out=dw_out, lru_conv_w=dw8[:CONV_K], lru_conv_b=dw8[CONV_K],
                 lru_w_a=_block_diag_part(dwa, LRU_BLOCKS), lru_b_a=dba[0], lru_w_x=_block_diag_part(dwx, LRU_BLOCKS),
                 lru_b_x=dbx[0], lru_lambda=dlam[0])
    return dx, grads


def _dn_split_w(w_in):
    return w_in[:, :4 * DN_WIDTH], jnp.pad(w_in[:, 4 * DN_WIDTH:], ((0, 0), (0, LANES - DN_AB)))


def _mixer_dn_fwd(x, w, tag):
    w_qkvz, w_ab = _dn_split_w(w["dn_w_in"][0])
    h, (proj, ab) = _norm_mm(x, w["mix_norm"][1], [w_qkvz, w_ab], [F32, F32], name=f"{tag}_in")
    w8 = _conv_w8(w["dn_conv_w"][0])
    cqkv = _conv_fwd(proj, 0, 3 * DN_WIDTH, w8, name=f"{tag}_conv")
    vecs = (_row(w["dn_a_log"][0], LANES), _row(w["dn_dt_bias"][0], LANES), _row(w["dn_o_norm"][0]))
    og, states = _dn_fwd(cqkv, proj, ab, *vecs, name=f"{tag}_dn")
    x2 = _mm(og, w["dn_w_out"][0], res=x, name=f"{tag}_out")
    return x2, (x, h, w_qkvz, w_ab, proj, ab, w8, cqkv, vecs, og, states)


def _mixer_dn_bwd(saved, w, dy, tag):
    x, h, w_qkvz, w_ab, proj, ab, w8, cqkv, vecs, og, states = saved
    dout = _mm(dy, w["dn_w_out"][0], mode="nt", name=f"{tag}_dout")
    dw_out = _mm(og, dy, mode="tn", name=f"{tag}_dwout")
    dcqkv, dz, dab, dalog, ddtb, don = _dn_bwd(cqkv, proj, ab, *vecs, states, dout, name=f"{tag}_ddn")
    dqkv, dw8 = _conv_bwd(proj, 0, 3 * DN_WIDTH, w8, dcqkv, name=f"{tag}_dconv")
    dproj = jnp.concatenate([dqkv.astype(MXU_DTYPE), dz.astype(MXU_DTYPE)], axis=1)
    dw_in = jnp.concatenate([_mm(h, dproj, mode="tn", name=f"{tag}_dwin"),
                             _mm(h, dab, mode="tn", name=f"{tag}_dwin_ab")[:, :DN_AB]], axis=1)
    dx, dg = _mm_rms_bwd([(dproj, w_qkvz), (dab, w_ab)], x, w["mix_norm"][1], dy, name=f"{tag}_dh")
    grads = dict(mix_norm=dg[0], dn_w_in=dw_in, dn_w_out=dw_out, dn_conv_w=dw8[:CONV_K], dn_a_log=dalog[0, :DN_HEADS],
                 dn_dt_bias=ddtb[0, :DN_HEADS], dn_o_norm=don[0])
    return dx, grads


def _xa_layer_fwd(x, mem, w, layer, tag):
    hq, (q,) = _norm_mm(x, w["xa_norm"][layer], [_layer_matrix(w["xa_wq"], layer)], [MXU_DTYPE], name=f"{tag}_q")
    hm = _rms_fwd(mem, w["xa_mem_norm"][layer], name=f"{tag}_mem_norm")
    kv = _mm(hm, _layer_matrix(w["xa_wkv"], layer), name=f"{tag}_kv")
    oa = _xa_fwd(q, kv, name=f"{tag}_core")
    x2 = _mm(oa, _layer_matrix(w["xa_wo"], layer), res=x, name=f"{tag}_out")
    return x2, (x, hq, q, hm, kv, oa)


def _xa_layer_bwd(saved, mem, w, layer, dy, tag):
    x, hq, q, hm, kv, oa = saved
    dwo = _mm(oa, dy, mode="tn", name=f"{tag}_dwo")
    dq, dkv = _xa_bwd(q, kv, dy, _layer_matrix(w["xa_wo"], layer), name=f"{tag}_dcore")
    dwq = _mm(hq, dq, mode="tn", name=f"{tag}_dwq")
    dx, dg = _mm_rms_bwd([(dq, _layer_matrix(w["xa_wq"], layer))], x, w["xa_norm"][layer], dy, name=f"{tag}_dhq")
    dwkv = _mm(hm, dkv, mode="tn", name=f"{tag}_dwkv")
    dhm = _mm(dkv, _layer_matrix(w["xa_wkv"], layer), mode="nt", name=f"{tag}_dhm")
    _, dgm = _rms_bwd(mem, w["xa_mem_norm"][layer], dhm, jnp.zeros_like(mem), name=f"{tag}_dmem_norm")
    return dx, dict(xa_norm=dg[0], xa_mem_norm=dgm[0], xa_wq=dwq, xa_wkv=dwkv, xa_wo=dwo)


def _local_step(x, mem, target, w, pending=None, reduce_big=False):
    saved = []
    plans, shards = pending if pending else ([], None)
    hosts = {("ffn1", 0): plans[0], ("ffn2", 0): plans[1]} if plans else {}

    def ffn(which, layer, x):
        plan = hosts.get((which, layer))
        hosted = ([b for *_, b in plan], [a for _, _, a, _ in plan]) if plan else None
        x, s, gathered = _ffn_fwd(x, w[f"{which}_norm"], w[f"{which}_w_in"], w[f"{which}_w_out"], layer, f"l{layer}_{which}", gather=hosted)
        if plan:
            _gathered(plan, gathered, shards, w)
        return x, s

    for layer in range(DEPTH):
        t = f"l{layer}"
        x, s1 = ffn("ffn1", layer, x)
        x, s2 = (_mixer_ab_fwd if layer % 2 == 0 else _mixer_dn_fwd)(x, w, f"{t}_mix")
        x, s3 = _xa_layer_fwd(x, mem, w, layer, f"{t}_xa")
        x, s4 = ffn("ffn2", layer, x)
        saved.append((s1, s2, s3, s4))
    loss, dx, dgf = _final_loss(x, w["final_norm"], target, name="final_loss")
    per_layer, reduced = [None] * DEPTH, [None] * DEPTH
    travelling = None
    for layer in reversed(range(DEPTH)):
        t = f"l{layer}"
        s1, s2, s3, s4 = saved[layer]
        g = {}
        dx, g["ffn2_norm"], g["ffn2_w_in"], g["ffn2_w_out"], parts = _ffn_bwd(
            s4, w["ffn2_norm"], w["ffn2_w_in"], w["ffn2_w_out"], layer, dx, f"{t}_ffn2",
            exchange=travelling[1] if travelling else None)
        if travelling:
            reduced[travelling[0]] = _reduce_end(travelling[1], parts, f"l{travelling[0]}")
        dx, gx = _xa_layer_bwd(s3, mem, w, layer, dx, f"{t}_xa")
        dx, gm = (_mixer_ab_bwd if layer % 2 == 0 else _mixer_dn_bwd)(s2, w, dx, f"{t}_mix")
        dx, g["ffn1_norm"], g["ffn1_w_in"], g["ffn1_w_out"], _ = _ffn_bwd(
            s1, w["ffn1_norm"], w["ffn1_w_in"], w["ffn1_w_out"], layer, dx, f"{t}_ffn1")
        per_layer[layer] = {**g, **gx, **gm}
        if reduce_big:
            chip_sum = _reduce_begin(_pack_grads(per_layer[layer]), t)
            if layer > 0:
                travelling = (layer, chip_sum)
            else:
                reduced[layer] = _reduce_end(chip_sum, _exchange_chips(chip_sum), t)
    grads = {"final_norm": [dgf[0]]}
    for layer_grads in per_layer:
        for name, value in layer_grads.items():
            grads.setdefault(name, []).append(value)
    return loss, dx, grads, list(zip(reduced, per_layer))


N_CHIPS = 4
WIRE_DTYPE = jnp.bfloat16
HBM_SPEC = pl.BlockSpec(memory_space=pltpu.HBM)
PACK_COLS = 1024


def _place():
    x, y, c = lax.axis_index("x"), lax.axis_index("y"), lax.axis_index("c")
    return x, y, c, [(1 - x, y), (x, 1 - y), (1 - x, 1 - y)]


def _remote(src, dst, sems, k, to):
    return pltpu.make_async_remote_copy(src_ref=src, dst_ref=dst, send_sem=sems[0].at[k], recv_sem=sems[1].at[k],
                                        device_id=to, device_id_type=MESH)


def _gather_weights(blocks, axes):
    n = len(blocks)
    out_shapes, sem_shapes, start, finish = _gather_plan(blocks, axes)

    def body(*refs):
        start(refs[:n], refs[n:2 * n], *refs[2 * n:])
        finish(refs[:n], refs[n:2 * n], *refs[2 * n:])

    return pl.pallas_call(
        body, name="gather_weights", in_specs=[HBM_SPEC] * n, out_specs=[HBM_SPEC] * n,
        out_shape=out_shapes, scratch_shapes=sem_shapes,
    )(*blocks)


def _gather_plan(blocks, axes):
    n = len(blocks)
    split = [b.shape[1] % 32 == 0 for b in blocks]

    def full_shape(i):
        l, r, c = blocks[i].shape
        return (l, N_CHIPS * r, c) if axes[i] == 1 else (l, r, N_CHIPS * c)

    def copies(ins, outs, send_sems, recv_sems):
        x, y, c, chips = _place()
        sems = (send_sems, recv_sems)
        sibling = (x, y, 1 - c)
        me = 2 * x + y

        def window(i, k, h):
            l, r, cc = blocks[i].shape
            r0, nr = (0, r) if h is None else (h * (r // 2), r // 2)
            if axes[i] == 1:
                return outs[i].at[:, pl.ds(k * r + r0, nr), :]
            return outs[i].at[:, pl.ds(r0, nr), pl.ds(k * cc, cc)]

        def mine(i, h):
            r = blocks[i].shape[1]
            return ins[i] if h is None else ins[i].at[:, pl.ds(h * (r // 2), r // 2), :]

        half = lambda i: c if split[i] else None
        first = [_remote(mine(i, half(i)), window(i, me, half(i)), sems, 3 * i + j, (*chip, c))
                 for i in range(n) for j, chip in enumerate(chips)]
        first += [_remote(ins[i], window(i, me, None), sems, 6 * n + i, sibling) for i in range(n)]
        arrive = lambda i, j, h, k, frm: _remote(window(i, 2 * chips[j][0] + chips[j][1], h), window(i, 2 * chips[j][0] + chips[j][1], h),
                                                 sems, k, frm)
        return first, arrive, chips, c, sibling

    def start(ins, outs, send_sems, recv_sems):
        for cp in copies(ins, outs, send_sems, recv_sems)[0]:
            cp.start()

    def finish(ins, outs, send_sems, recv_sems):
        first, arrive, chips, c, sibling = copies(ins, outs, send_sems, recv_sems)
        passed = []
        for i in range(n):
            for j, (cx, cy) in enumerate(chips):
                arrive(i, j, c if split[i] else None, 3 * i + j, (cx, cy, c)).wait_recv()
                if split[i]:
                    passed.append(arrive(i, j, c, 3 * (n + i) + j, sibling))
                    passed[-1].start()
        for i in range(n):
            if split[i]:
                for j in range(len(chips)):
                    arrive(i, j, 1 - c, 3 * (n + i) + j, sibling).wait_recv()
        for cp in first[3 * n:]:
            cp.wait_recv()
        for cp in first + passed:
            cp.wait_send()

    sem_shapes = [pltpu.SemaphoreType.DMA((7 * n,)), pltpu.SemaphoreType.DMA((7 * n,))]
    return [jax.ShapeDtypeStruct(full_shape(i), blocks[i].dtype) for i in range(n)], sem_shapes, start, finish


def _allreduce_small(v):
    rows, cols = v.shape
    n_dev = 2 * N_CHIPS

    def body(v_ref, out_ref, all_ref, send_sems, recv_sems, local_sem):
        x, y, c, chips = _place()
        sems = (send_sems, recv_sems)
        me, sibling = (x, y, c), (x, y, 1 - c)
        slot = lambda px, py, pc: all_ref.at[pl.ds((4 * px + 2 * py + pc) * rows, rows), :]
        mine = pltpu.make_async_copy(v_ref, slot(*me), local_sem)
        mine.start()
        first = [_remote(v_ref, slot(*me), sems, 0, sibling)]
        first += [_remote(v_ref, slot(*me), sems, 1 + j, (*chip, c)) for j, chip in enumerate(chips)]
        for cp in first:
            cp.start()
        passed = [_remote(slot(*chip, c), slot(*chip, c), sems, 4 + j, sibling) for j, chip in enumerate(chips)]
        for j, chip in enumerate(chips):
            _remote(slot(*chip, c), slot(*chip, c), sems, 1 + j, me).wait_recv()
            passed[j].start()
        _remote(slot(*sibling), slot(*sibling), sems, 0, me).wait_recv()
        for j, chip in enumerate(chips):
            _remote(slot(*chip, 1 - c), slot(*chip, 1 - c), sems, 4 + j, me).wait_recv()
        for cp in first + passed:
            cp.wait_send()
        mine.wait()
        acc = all_ref[pl.ds(0, rows), :]
        for k in range(1, n_dev):
            acc = acc + all_ref[pl.ds(k * rows, rows), :]
        out_ref[...] = acc

    vmem = pl.BlockSpec(memory_space=pltpu.VMEM)
    return pl.pallas_call(
        body, name="allreduce_small", in_specs=[vmem], out_specs=vmem, out_shape=jax.ShapeDtypeStruct((rows, cols), F32),
        scratch_shapes=[pltpu.VMEM((n_dev * rows, cols), F32), pltpu.SemaphoreType.DMA((7,)), pltpu.SemaphoreType.DMA((7,)),
                        pltpu.SemaphoreType.DMA],
    )(v)


def _swap_other_half(g4, tag):
    n, _, rows, cols = g4.shape

    def body(v_ref, out_ref, send_sems, recv_sems):
        x, y, c, _ = _place()
        cp = _remote(v_ref.at[:, 1 - c], out_ref, (send_sems, recv_sems), 0, (x, y, 1 - c))
        cp.start()
        cp.wait()

    return pl.pallas_call(
        body, name=f"{tag}_reduce_swap", in_specs=[HBM_SPEC], out_specs=HBM_SPEC, out_shape=jax.ShapeDtypeStruct((n, rows, cols), g4.dtype),
        scratch_shapes=[pltpu.SemaphoreType.DMA((1,)), pltpu.SemaphoreType.DMA((1,))],
    )(g4)


def _add_kept_half(g4, got, tag):
    n, _, rows, cols = g4.shape
    tr = _tile(rows, (256, 128, 64, 32, 16))
    nb = rows // tr

    def body(c_ref, a_ref, b_ref, o_ref):
        o_ref[...] = (a_ref[...] + b_ref[...]).astype(o_ref.dtype)

    return pl.pallas_call(
        body, name=f"{tag}_reduce_sum_cores",
        grid_spec=pltpu.PrefetchScalarGridSpec(
            num_scalar_prefetch=1, grid=(n, nb),
            in_specs=[pl.BlockSpec((None, None, tr, cols), lambda k, i, c_ref: (k, c_ref[0], i, 0)),
                      pl.BlockSpec((None, tr, cols), lambda k, i, c_ref: (k, i, 0))],
            out_specs=pl.BlockSpec((None, tr, cols), lambda k, i, c_ref: (k, i, 0))),
        out_shape=jax.ShapeDtypeStruct((n, rows, cols), WIRE_DTYPE), compiler_params=_params("parallel", "parallel"),
    )(lax.axis_index("c").astype(jnp.int32).reshape(1), g4, got)


def _exchange_plan(v):
    def copies(v_ref, out_ref, send_sems, recv_sems):
        x, y, c, chips = _place()
        return [_remote(v_ref.at[2 * cx + cy], out_ref.at[j], (send_sems, recv_sems), j, (cx, cy, c)) for j, (cx, cy) in enumerate(chips)]

    def start(*refs):
        for cp in copies(*refs):
            cp.start()

    def finish(*refs):
        for cp in copies(*refs):
            cp.wait_recv()
        for cp in copies(*refs):
            cp.wait_send()

    sem_shapes = [pltpu.SemaphoreType.DMA((N_CHIPS - 1,)), pltpu.SemaphoreType.DMA((N_CHIPS - 1,))]
    return jax.ShapeDtypeStruct((N_CHIPS - 1,) + v.shape[1:], v.dtype), sem_shapes, start, finish


def _exchange_chips(v):
    out_shape, sem_shapes, start, finish = _exchange_plan(v)

    def body(*refs):
        start(*refs)
        finish(*refs)

    return pl.pallas_call(body, name="exchange_chips", in_specs=[HBM_SPEC], out_specs=HBM_SPEC, out_shape=out_shape,
                          scratch_shapes=sem_shapes)(v)


def _swap_sibling(v, tag):
    def body(v_ref, out_ref, send_sems, recv_sems):
        x, y, c, _ = _place()
        cp = _remote(v_ref, out_ref, (send_sems, recv_sems), 0, (x, y, 1 - c))
        cp.start()
        cp.wait()

    return pl.pallas_call(
        body, name=f"{tag}_share_halves", in_specs=[HBM_SPEC], out_specs=HBM_SPEC, out_shape=jax.ShapeDtypeStruct(v.shape, v.dtype),
        scratch_shapes=[pltpu.SemaphoreType.DMA((1,)), pltpu.SemaphoreType.DMA((1,))],
    )(v)


def _sum_chips(own4, parts, tag):
    _, rows, cols = own4.shape
    tr = _tile(rows, (256, 128, 64, 32, 16))

    def body(me_ref, own_ref, p0_ref, p1_ref, p2_ref, o_ref):
        acc = own_ref[...].astype(F32)
        for r in (p0_ref, p1_ref, p2_ref):
            acc = acc + r[...].astype(F32)
        o_ref[...] = acc

    part = lambda j: pl.BlockSpec((None, tr, cols), lambda i, me_ref: (j, i, 0))
    chip = (2 * lax.axis_index("x") + lax.axis_index("y")).astype(jnp.int32).reshape(1)
    return pl.pallas_call(
        body, name=f"{tag}_reduce_sum_chips",
        grid_spec=pltpu.PrefetchScalarGridSpec(
            num_scalar_prefetch=1, grid=(rows // tr,),
            in_specs=[pl.BlockSpec((None, tr, cols), lambda i, me_ref: (me_ref[0], i, 0)), part(0), part(1), part(2)],
            out_specs=pl.BlockSpec((tr, cols), lambda i, me_ref: (i, 0))),
        out_shape=jax.ShapeDtypeStruct((rows, cols), F32), compiler_params=_params("parallel"),
    )(chip, own4, parts, parts, parts)


def _reduce_begin(g4, tag):
    return _add_kept_half(g4, _swap_other_half(g4, tag), tag)


def _reduce_end(chip_sum, parts, tag):
    half = _sum_chips(chip_sum, parts, tag)
    other = _swap_sibling(half, tag)
    return jnp.where(lax.axis_index("c") == 0, jnp.stack([half, other]), jnp.stack([other, half]))


BIG = (("ffn1_w_in", 2), ("ffn1_w_out", 1), ("xa_wq", 1), ("xa_wkv", 2), ("xa_wo", 1), ("ffn2_w_in", 2), ("ffn2_w_out", 1),
       ("ab_w_in", 2), ("ab_w_out", 1), ("dn_w_in", 2), ("dn_w_out", 1))
TINY_SHARDED = (("lru_conv_w", 2), ("dn_conv_w", 2))
REPLICATED = ("ffn1_norm", "mix_norm", "xa_norm", "xa_mem_norm", "ffn2_norm", "lru_conv_b", "lru_w_a", "lru_b_a", "lru_w_x",
              "lru_b_x", "lru_lambda", "dn_a_log", "dn_dt_bias", "dn_o_norm", "final_norm")
WEIGHTS = ("ffn1_norm", "ffn1_w_in", "ffn1_w_out", "mix_norm", "xa_norm", "xa_mem_norm", "xa_wq", "xa_wkv", "xa_wo", "ffn2_norm",
           "ffn2_w_in", "ffn2_w_out", "ab_w_in", "lru_conv_w", "lru_conv_b", "lru_w_a", "lru_b_a", "lru_w_x", "lru_b_x",
           "lru_lambda", "ab_w_out", "dn_w_in", "dn_conv_w", "dn_a_log", "dn_dt_bias", "dn_o_norm", "dn_w_out", "final_norm")


def _lane_padded(shape):
    return shape[:-1] + (-(-shape[-1] // LANES) * LANES,)


def _pad_lanes(t):
    return jnp.pad(t, [(0, 0)] * (t.ndim - 1) + [(0, _lane_padded(t.shape)[-1] - t.shape[-1])])


FIRST_USED = ("ffn1_w_in", "ffn1_w_out")
LAYER_1_ONLY = ("dn_w_in", "dn_w_out", "dn_conv_w")


def _gather_blocks(shards):
    groups = [], [], []
    for n, a in BIG + TINY_SHARDED:
        block = _pad_lanes(shards[n]).astype(MXU_DTYPE) if (n, a) in BIG else shards[n]
        if block.shape[0] == 1:
            groups[2 if n in LAYER_1_ONLY else 1].append((n, None, a, block))
        else:
            for layer in range(block.shape[0]):
                group = 2 if layer > 0 else 0 if n in FIRST_USED else 1
                groups[group].append((n, layer, a, block[layer:layer + 1]))
    return groups


def _gathered(plan, arrays, shards, into):
    for (n, layer, axis, _), full in zip(plan, arrays):
        width, padded = shards[n].shape[-1], _lane_padded(shards[n].shape)[-1]
        if padded != width:
            assert axis == 2
            full = jnp.concatenate([full[..., k * padded:k * padded + width] for k in range(N_CHIPS)], axis=-1)
        if layer is None:
            into[n] = full
        else:
            into.setdefault(n, [None] * shards[n].shape[0])[layer] = full
    return into


def _pack_parts(cols):
    whole = cols // PACK_COLS * PACK_COLS
    return [(c0, PACK_COLS) for c0 in range(0, whole, PACK_COLS)] + ([(whole, cols - whole)] if cols > whole else [])


def _to_rows(block):
    block = _pad_lanes(block)
    return jnp.concatenate([block[:, c0:c0 + n].reshape(-1, PACK_COLS) for c0, n in _pack_parts(block.shape[1])], axis=0)


def _from_rows(rows, r, c):
    padded = _lane_padded((r, c))[1]
    parts, off = [], 0
    for _, n in _pack_parts(padded):
        size = r * n // PACK_COLS
        parts.append(rows[off:off + size].reshape(r, n))
        off += size
    return jnp.concatenate(parts, axis=1)[:, :c]


def _pack_rows(r, c):
    return r * _lane_padded((r, c))[1] // PACK_COLS


def _pack_grads(layer_grads):
    names = [(n, axis) for n, axis in BIG if n in layer_grads]
    used = sum(_pack_rows(layer_grads[n].shape[0] // (N_CHIPS if axis == 1 else 1),
                          layer_grads[n].shape[1] // (N_CHIPS if axis == 2 else 1)) for n, axis in names)
    rows = -(-used // 512) * 512

    def chip_block(k):
        blocks = []
        for n, axis in names:
            width = layer_grads[n].shape[axis - 1] // N_CHIPS
            blocks.append(_to_rows(lax.slice_in_dim(layer_grads[n], k * width, (k + 1) * width, axis=axis - 1)))
        if rows > used:
            blocks.append(jnp.zeros((rows - used, PACK_COLS), F32))
        return jnp.concatenate(blocks, axis=0)

    return jnp.stack([chip_block(k) for k in range(N_CHIPS)], axis=0).reshape(N_CHIPS, 2, rows // 2, PACK_COLS)


def _unpack_grads(reduced, layer_grads, shards):
    rows = reduced.reshape(-1, PACK_COLS)
    out, off = {}, 0
    for n, _ in BIG:
        if n in layer_grads:
            r, c = shards[n].shape[1:]
            out[n] = _from_rows(rows[off:off + _pack_rows(r, c)], r, c)[None]
            off += _pack_rows(r, c)
    return out


def _pack_small(grads, loss):
    parts = [p.reshape(-1) for n in REPLICATED + tuple(n for n, _ in TINY_SHARDED) for p in grads[n]] + [loss[0, :1]]
    flat = jnp.concatenate(parts)
    total = -(-flat.shape[0] // (SUBLANES * LANES)) * SUBLANES * LANES
    return jnp.pad(flat, (0, total - flat.shape[0])).reshape(-1, LANES)


def _unpack_small(summed, shards, chip):
    flat = summed.reshape(-1)
    out, off = {}, 0
    for n in REPLICATED:
        out[n] = flat[off:off + shards[n].size].reshape(shards[n].shape)
        off += shards[n].size
    for n, axis in TINY_SHARDED:
        width = shards[n].shape[axis]
        shape = shards[n].shape[:axis] + (N_CHIPS * width,) + shards[n].shape[axis + 1:]
        full = flat[off:off + N_CHIPS * shards[n].size].reshape(shape)
        out[n] = lax.dynamic_slice_in_dim(full, chip * width, width, axis=axis)
        off += N_CHIPS * shards[n].size
    return out, flat[off]


def kernel(x, mem, ffn1_norm, ffn1_w_in, ffn1_w_out, mix_norm, xa_norm, xa_mem_norm, xa_wq, xa_wkv, xa_wo, ffn2_norm,
           ffn2_w_in, ffn2_w_out, ab_w_in, lru_conv_w, lru_conv_b, lru_w_a, lru_b_a, lru_w_x, lru_b_x, lru_lambda,
           ab_w_out, dn_w_in, dn_conv_w, dn_a_log, dn_dt_bias, dn_o_norm, dn_w_out, final_norm, loss_target,
           m_ffn1_norm, m_ffn1_w_in, m_ffn1_w_out, m_mix_norm, m_xa_norm, m_xa_mem_norm, m_xa_wq, m_xa_wkv, m_xa_wo,
           m_ffn2_norm, m_ffn2_w_in, m_ffn2_w_out, m_ab_w_in, m_lru_conv_w, m_lru_conv_b, m_lru_w_a, m_lru_b_a,
           m_lru_w_x, m_lru_b_x, m_lru_lambda, m_ab_w_out, m_dn_w_in, m_dn_conv_w, m_dn_a_log, m_dn_dt_bias,
           m_dn_o_norm, m_dn_w_out, m_final_norm, v_ffn1_norm, v_ffn1_w_in, v_ffn1_w_out, v_mix_norm, v_xa_norm,
           v_xa_mem_norm, v_xa_wq, v_xa_wkv, v_xa_wo, v_ffn2_norm, v_ffn2_w_in, v_ffn2_w_out, v_ab_w_in,
           v_lru_conv_w, v_lru_conv_b, v_lru_w_a, v_lru_b_a, v_lru_w_x, v_lru_b_x, v_lru_lambda, v_ab_w_out,
           v_dn_w_in, v_dn_conv_w, v_dn_a_log, v_dn_dt_bias, v_dn_o_norm, v_dn_w_out, v_final_norm):
    given = dict(locals())
    shards = {n: given[n] for n in WEIGHTS}
    chip = 2 * lax.axis_index("x") + lax.axis_index("y")

    full = {n: shards[n] for n in REPLICATED}
    first, *later = _gather_blocks(shards)
    _gathered(first, _gather_weights([b for *_, b in first], [a for _, _, a, _ in first]), shards, full)
    loss, grad_x, grads, reduced = _local_step(x[0], mem[0], loss_target[0], full, pending=(later, shards), reduce_big=True)

    small, loss_sum = _unpack_small(_allreduce_small(_pack_small(grads, loss)), shards, chip)
    per_layer = [_unpack_grads(r, layer_grads, shards) for r, layer_grads in reduced]
    slabs = {n: [g] for n, g in small.items()}
    slabs.update({n: [p[n] for p in per_layer if n in p] for n, _ in BIG})

    grad, delta, new_m, new_v = {}, {}, {}, {}
    for n in WEIGHTS:
        delta[n], new_m[n], new_v[n], grad[n] = _adamw(shards[n], slabs[n], given["m_" + n], given["v_" + n], name=f"adamw_{n}")
    return (loss_sum, grad_x[None], *[grad[n] for n in WEIGHTS], *[delta[n] for n in WEIGHTS],
            *[new_m[n] for n in WEIGHTS], *[new_v[n] for n in WEIGHTS])
```

```python
import math

import jax
import jax.numpy as jnp
from jax import lax
from jax.experimental import pallas as pl
from jax.experimental.pallas import tpu as pltpu

F32 = jnp.float32
MXU_DTYPE = jnp.bfloat16
VMEM_LIMIT_BYTES = 48 * 1024 * 1024
MM_BLOCK_BYTES = 8 * 1024 * 1024
ADAMW_BLOCK_BYTES = 1024 * 1024
LANES = 128
SUBLANES = 8

NORM_EPS = 1e-6
CONV_K = 4
ATTN_PAIRS = 4
ATTN_HEAD_DIM = 64
ATTN_WIDTH = 512
ATTN_BLOCK = 128
DILATIONS = (1, 4, 16)
LRU_WIDTH = 512
LRU_BLOCKS = 8
LRU_C = 8.0
DN_HEADS = 8
DN_HEAD_DIM = 128
DN_WIDTH = 1024
DN_CHUNK = 64
XA_HEADS = 4
XA_HEAD_DIM = 256
D_FF = 2816
ADAM_LR, ADAM_B1, ADAM_B2, ADAM_EPS, ADAM_WD, ADAM_STEP = 0.001, 0.9, 0.999, 1e-08, 0.01, 10

MESH = pl.DeviceIdType.MESH


def _tile(n, prefs):
    for p in prefs:
        if n % p == 0:
            return p
    return n


def _params(*sem):
    return pltpu.CompilerParams(dimension_semantics=sem, vmem_limit_bytes=VMEM_LIMIT_BYTES)


def _dg(a, b, dims, hi=False):
    if hi:
        return lax.dot_general(a, b, (dims, ((), ())), precision=lax.Precision.HIGHEST, preferred_element_type=F32)
    return lax.dot_general(a.astype(MXU_DTYPE), b.astype(MXU_DTYPE), (dims, ((), ())), preferred_element_type=F32)


@jax.custom_vjp
def _bdot(a, b):
    return _dg(a, b, ((1,), (0,)))


def _bdot_fwd(a, b):
    return _bdot(a, b), (a, b)


def _bdot_bwd(r, g):
    a, b = r
    return _dg(g, b, ((1,), (1,))).astype(a.dtype), _dg(a, g, ((0,), (0,))).astype(b.dtype)


_bdot.defvjp(_bdot_fwd, _bdot_bwd)


def _log1p(t):
    return jnp.where(t < 0.01, t * (1.0 - t * (0.5 - t * (1.0 / 3.0))), jnp.log(1.0 + t))


def _neg_expm1(y):
    series = -y * (1.0 + 0.5 * y * (1.0 + (1.0 / 3.0) * y * (1.0 + 0.25 * y)))
    return jnp.where(y > -0.01, series, 1.0 - jnp.exp(y))


def _softplus(x):
    return jnp.maximum(x, 0.0) + _log1p(jnp.exp(-jnp.abs(x)))


def _sigmoid(x):
    return 0.5 * jnp.tanh(0.5 * x) + 0.5


def _silu(x):
    return x * _sigmoid(x)


def _gelu(x):
    return 0.5 * x * (1.0 + jnp.tanh(0.7978845608028654 * (x + 0.044715 * x * x * x)))


def _rows(shape):
    return lax.broadcasted_iota(jnp.int32, shape, 0)


def _cols(shape):
    return lax.broadcasted_iota(jnp.int32, shape, 1)


def _mm(a, b, *, mode="nn", out_dtype=F32, res=None, scale=1.0, name):
    if mode == "nn":
        (m, k), (k2, n) = a.shape, b.shape
    elif mode == "nt":
        (m, k), (n, k2) = a.shape, b.shape
    else:
        (k, m), (k2, n) = a.shape, b.shape
    assert k == k2, (a.shape, b.shape, mode)
    if mode == "tn":
        tm, tn, tk = _tile(m, (1024, 512, 256, 128)), _tile(n, (1024, 512, 256, 128)), _tile(k, (2048, 1024, 512, 256))
    else:
        tm, tn = _tile(m, (512, 256, 128)), _tile(n, (1024, 512, 256, 128))
        tk = k if k * tn * 2 <= MM_BLOCK_BYTES else _tile(k, (1024, 512, 256, 128))
    nk = k // tk
    dims = {"nn": ((1,), (0,)), "nt": ((1,), (1,)), "tn": ((0,), (0,))}[mode]

    def body(*refs):
        a_ref, b_ref = refs[:2]
        r_ref = refs[2] if res is not None else None
        o_ref = refs[3 if res is not None else 2]

        def finish(r):
            if scale != 1.0:
                r = r * scale
            if res is not None:
                r = r_ref[...] + r
            o_ref[...] = r.astype(out_dtype)

        if nk == 1:
            finish(_dg(a_ref[...], b_ref[...], dims))
            return
        acc = refs[-1]
        kk = pl.program_id(2)

        @pl.when(kk == 0)
        def _():
            acc[...] = jnp.zeros_like(acc)

        acc[...] += _dg(a_ref[...], b_ref[...], dims)

        @pl.when(kk == nk - 1)
        def _():
            finish(acc[...])

    a_spec = pl.BlockSpec((tk, tm), lambda i, j, kk: (kk, i)) if mode == "tn" else pl.BlockSpec((tm, tk), lambda i, j, kk: (i, kk))
    b_spec = pl.BlockSpec((tn, tk), lambda i, j, kk: (j, kk)) if mode == "nt" else pl.BlockSpec((tk, tn), lambda i, j, kk: (kk, j))
    o_spec = pl.BlockSpec((tm, tn), lambda i, j, kk: (i, j))
    in_specs = [a_spec, b_spec] + ([o_spec] if res is not None else [])
    args = (a, b) + ((res,) if res is not None else ())
    return pl.pallas_call(
        body, name=name, grid=(m // tm, n // tn, nk), in_specs=in_specs, out_specs=o_spec,
        out_shape=jax.ShapeDtypeStruct((m, n), out_dtype), scratch_shapes=[pltpu.VMEM((tm, tn), F32)] if nk > 1 else [],
        compiler_params=_params("parallel", "parallel", "arbitrary"),
    )(*args)


def _rms_fwd(x, g, *, name):
    s, d = x.shape
    tm = _tile(s, (512, 256))

    def body(x_ref, g_ref, o_ref):
        xv = x_ref[...]
        r = lax.rsqrt(jnp.mean(xv * xv, axis=-1, keepdims=True) + NORM_EPS)
        o_ref[...] = (xv * r * g_ref[...]).astype(o_ref.dtype)

    return pl.pallas_call(
        body, name=name, grid=(s // tm,),
        in_specs=[pl.BlockSpec((tm, d), lambda i: (i, 0)), pl.BlockSpec((1, d), lambda i: (0, 0))],
        out_specs=pl.BlockSpec((tm, d), lambda i: (i, 0)), out_shape=jax.ShapeDtypeStruct((s, d), MXU_DTYPE),
        compiler_params=_params("parallel"),
    )(x, g.reshape(1, d))


def _norm_mm(x, g, ws, out_dtypes, *, name):
    s, d = x.shape
    tm = _tile(s, (512, 256))
    nw = len(ws)

    def body(*refs):
        x_ref, g_ref = refs[:2]
        h_ref = refs[2 + nw]
        xv = x_ref[...]
        r = lax.rsqrt(jnp.mean(xv * xv, axis=-1, keepdims=True) + NORM_EPS)
        h = (xv * r * g_ref[...]).astype(MXU_DTYPE)
        h_ref[...] = h
        for w_ref, o_ref in zip(refs[2:2 + nw], refs[3 + nw:]):
            o_ref[...] = _dg(h, w_ref[...], ((1,), (0,))).astype(o_ref.dtype)

    row = lambda w: pl.BlockSpec((tm, w), lambda i: (i, 0))
    outs = pl.pallas_call(
        body, name=name, grid=(s // tm,),
        in_specs=[row(d), pl.BlockSpec((1, d), lambda i: (0, 0))]
        + [pl.BlockSpec(w.shape, lambda i: (0, 0), pipeline_mode=RESIDENT) for w in ws],
        out_specs=[row(d)] + [row(w.shape[1]) for w in ws],
        out_shape=[jax.ShapeDtypeStruct((s, d), MXU_DTYPE)] + [jax.ShapeDtypeStruct((s, w.shape[1]), t) for w, t in zip(ws, out_dtypes)],
        compiler_params=_params("parallel"),
    )(x, g.reshape(1, d), *ws)
    return outs[0], outs[1:]


def _mm_rms_bwd(pairs, x, g, dres, *, name):
    s, d = x.shape
    tm = _tile(s, (512, 256))
    n = len(pairs)

    def body(*refs):
        x_ref, g_ref, dr_ref = refs[2 * n:2 * n + 3]
        dx_ref, dg_ref = refs[2 * n + 3:]
        dh = _dg(refs[0][...], refs[n][...], ((1,), (1,)))
        for a_ref, w_ref in zip(refs[1:n], refs[n + 1:2 * n]):
            dh = dh + _dg(a_ref[...], w_ref[...], ((1,), (1,)))
        xv, gv = x_ref[...], g_ref[...]
        r = lax.rsqrt(jnp.mean(xv * xv, axis=-1, keepdims=True) + NORM_EPS)
        xh = xv * r
        dxh = dh * gv
        dx_ref[...] = dr_ref[...] + r * (dxh - xh * jnp.mean(dxh * xh, axis=-1, keepdims=True))

        @pl.when(pl.program_id(0) == 0)
        def _():
            dg_ref[...] = jnp.zeros_like(dg_ref)

        dg_ref[...] += jnp.sum(dh * xh, axis=0, keepdims=True)

    row = lambda w: pl.BlockSpec((tm, w), lambda i: (i, 0))
    vec = pl.BlockSpec((1, d), lambda i: (0, 0))
    return pl.pallas_call(
        body, name=name, grid=(s // tm,),
        in_specs=[row(a.shape[1]) for a, _ in pairs]
        + [pl.BlockSpec(w.shape, lambda i: (0, 0), pipeline_mode=RESIDENT) for _, w in pairs] + [row(d), vec, row(d)],
        out_specs=[row(d), vec], out_shape=[jax.ShapeDtypeStruct((s, d), F32), jax.ShapeDtypeStruct((1, d), F32)],
        compiler_params=_params("arbitrary"),
    )(*[a for a, _ in pairs], *[w for _, w in pairs], x, g.reshape(1, d), dres)


def _rms_bwd(x, g, dh, dres, *, name):
    s, d = x.shape
    tm = _tile(s, (512, 256))

    def body(x_ref, g_ref, dh_ref, dr_ref, dx_ref, dg_ref):
        xv = x_ref[...]
        r = lax.rsqrt(jnp.mean(xv * xv, axis=-1, keepdims=True) + NORM_EPS)
        xh = xv * r
        dhv = dh_ref[...].astype(F32)
        dxh = dhv * g_ref[...]
        dx = r * (dxh - xh * jnp.mean(dxh * xh, axis=-1, keepdims=True))
        dx_ref[...] = dr_ref[...] + dx

        @pl.when(pl.program_id(0) == 0)
        def _():
            dg_ref[...] = jnp.zeros_like(dg_ref)

        dg_ref[...] += jnp.sum(dhv * xh, axis=0, keepdims=True)

    row = pl.BlockSpec((tm, d), lambda i: (i, 0))
    vec = pl.BlockSpec((1, d), lambda i: (0, 0))
    return pl.pallas_call(
        body, name=name, grid=(s // tm,), in_specs=[row, vec, row, row], out_specs=[row, vec],
        out_shape=[jax.ShapeDtypeStruct((s, d), F32), jax.ShapeDtypeStruct((1, d), F32)],
        compiler_params=_params("arbitrary"),
    )(x, g.reshape(1, d), dh, dres)


FFN_CHUNK = 256
FFN_TM = 256
RESIDENT = pl.Buffered(1)


def _ffn_fwd_call(x, g, w_in, w_out, layer, *, name, gather=None):
    s, d = x.shape
    f = w_out.shape[1]
    tm = _tile(s, (2 * FFN_TM, FFN_TM))
    steps = s // tm
    n_g = len(gather[0]) if gather else 0
    g_shapes, g_sems, g_start, g_finish = _gather_plan(*gather) if gather else ([], [], None, None)

    def body(*refs):
        x_ref, g_ref, wi_ref, wo_ref = refs[:4]
        y_ref, u_ref = refs[4 + n_g:6 + n_g]
        act_ref = refs[6 + 2 * n_g]
        if gather:
            comm = (refs[4:4 + n_g], refs[6 + n_g:6 + 2 * n_g], *refs[7 + 2 * n_g:])
            pl.when(pl.program_id(0) == 0)(lambda: g_start(*comm))
        xv = x_ref[...]
        r = lax.rsqrt(jnp.mean(xv * xv, axis=-1, keepdims=True) + NORM_EPS)
        h = (xv * r * g_ref[...]).astype(MXU_DTYPE)
        for j in range(f // FFN_CHUNK):
            lo, hi = j * FFN_CHUNK, (j + 1) * FFN_CHUNK
            gate = _dg(h, wi_ref[:, lo:hi], ((1,), (0,))).astype(MXU_DTYPE)
            up = _dg(h, wi_ref[:, f + lo:f + hi], ((1,), (0,))).astype(MXU_DTYPE)
            u_ref[:, lo:hi] = gate
            u_ref[:, f + lo:f + hi] = up
            act_ref[:, lo:hi] = (_silu(gate.astype(F32)) * up.astype(F32)).astype(MXU_DTYPE)
        y_ref[...] = xv + 0.5 * _dg(act_ref[...], wo_ref[...], ((1,), (0,)))
        if gather:
            pl.when(pl.program_id(0) == steps - 1)(lambda: g_finish(*comm))

    row = lambda w: pl.BlockSpec((tm, w), lambda i: (i, 0))
    return pl.pallas_call(
        body, name=name, grid=(steps,),
        in_specs=[row(d), pl.BlockSpec((1, d), lambda i: (0, 0)),
                  pl.BlockSpec((None,) + w_in.shape[1:], lambda i: (layer, 0, 0), pipeline_mode=RESIDENT),
                  pl.BlockSpec((None,) + w_out.shape[1:], lambda i: (layer, 0, 0), pipeline_mode=RESIDENT)] + [HBM_SPEC] * n_g,
        out_specs=[row(d), row(2 * f)] + [HBM_SPEC] * n_g,
        out_shape=[jax.ShapeDtypeStruct((s, d), F32), jax.ShapeDtypeStruct((s, 2 * f), MXU_DTYPE)] + g_shapes,
        scratch_shapes=[pltpu.VMEM((tm, f), MXU_DTYPE)] + g_sems,
        compiler_params=_params("arbitrary" if gather else "parallel"),
    )(x, g.reshape(1, d), w_in, w_out, *(gather[0] if gather else ()))


def _ffn_bwd_call(x, g, u, dy, w_in, w_out, layer, *, name, exchange=None):
    s, d = x.shape
    f = w_out.shape[1]
    tm = _tile(s, (FFN_TM,))
    steps = s // tm
    hosted = exchange is not None
    e_shape, e_sems, e_start, e_finish = _exchange_plan(exchange) if hosted else (None, [], None, None)

    def body(*refs):
        x_ref, g_ref, u_ref, dy_ref, wi_ref, wo_ref = refs[:6]
        du_ref, dx_ref, dg_ref, h_ref = refs[6 + hosted:10 + hosted]
        if hosted:
            comm = (refs[6], *refs[10 + hosted:])
            pl.when(pl.program_id(0) == 0)(lambda: e_start(*comm))
        dyv = dy_ref[...]
        dyh = (0.5 * dyv).astype(MXU_DTYPE)
        for j in range(f // FFN_CHUNK):
            lo, hi = j * FFN_CHUNK, (j + 1) * FFN_CHUNK
            dact = _dg(dyh, wo_ref[lo:hi, :], ((1,), (1,)))
            gate, up = u_ref[:, lo:hi].astype(F32), u_ref[:, f + lo:f + hi].astype(F32)
            sg = _sigmoid(gate)
            du_ref[:, lo:hi] = (dact * up * sg * (1.0 + gate * (1.0 - sg))).astype(MXU_DTYPE)
            du_ref[:, f + lo:f + hi] = (dact * gate * sg).astype(MXU_DTYPE)
        dh = _dg(du_ref[...], wi_ref[...], ((1,), (1,)))
        xv, gv = x_ref[...], g_ref[...]
        r = lax.rsqrt(jnp.mean(xv * xv, axis=-1, keepdims=True) + NORM_EPS)
        xh = xv * r
        h_ref[...] = (xh * gv).astype(MXU_DTYPE)
        dxh = dh * gv
        dx_ref[...] = dyv + r * (dxh - xh * jnp.mean(dxh * xh, axis=-1, keepdims=True))

        @pl.when(pl.program_id(0) == 0)
        def _():
            dg_ref[...] = jnp.zeros_like(dg_ref)

        dg_ref[...] += jnp.sum(dh * xh, axis=0, keepdims=True)
        if hosted:
            pl.when(pl.program_id(0) == steps - 1)(lambda: e_finish(*comm))

    row = lambda w: pl.BlockSpec((tm, w), lambda i: (i, 0))
    vec = pl.BlockSpec((1, d), lambda i: (0, 0))
    return pl.pallas_call(
        body, name=name, grid=(steps,),
        in_specs=[row(d), vec, row(2 * f), row(d),
                  pl.BlockSpec((None,) + w_in.shape[1:], lambda i: (layer, 0, 0), pipeline_mode=RESIDENT),
                  pl.BlockSpec((None,) + w_out.shape[1:], lambda i: (layer, 0, 0), pipeline_mode=RESIDENT)] + [HBM_SPEC] * hosted,
        out_specs=[row(2 * f), row(d), vec, row(d)] + [HBM_SPEC] * hosted,
        out_shape=[jax.ShapeDtypeStruct((s, 2 * f), MXU_DTYPE), jax.ShapeDtypeStruct((s, d), F32),
                   jax.ShapeDtypeStruct((1, d), F32), jax.ShapeDtypeStruct((s, d), MXU_DTYPE)] + [e_shape] * hosted,
        scratch_shapes=e_sems, compiler_params=_params("arbitrary"),
    )(x, g.reshape(1, d), u, dy, w_in, w_out, *([exchange] if hosted else []))


def _ffn_dw_out(u, dy, *, name):
    s, f2 = u.shape
    f, d = f2 // 2, dy.shape[1]
    tf, tk = _tile(f, (1408, 256, 128)), _tile(s, (1024, 512, 256))
    nj = f // tf

    def body(g_ref, u_ref, dy_ref, o_ref):
        @pl.when(pl.program_id(1) == 0)
        def _():
            o_ref[...] = jnp.zeros_like(o_ref)

        act = _silu(g_ref[...].astype(F32)) * u_ref[...].astype(F32)
        o_ref[...] += _dg(act, 0.5 * dy_ref[...], ((0,), (0,)))

    return pl.pallas_call(
        body, name=name, grid=(nj, s // tk),
        in_specs=[pl.BlockSpec((tk, tf), lambda j, k: (k, j)), pl.BlockSpec((tk, tf), lambda j, k: (k, j + nj)),
                  pl.BlockSpec((tk, d), lambda j, k: (k, 0))],
        out_specs=pl.BlockSpec((tf, d), lambda j, k: (j, 0)), out_shape=jax.ShapeDtypeStruct((f, d), F32),
        compiler_params=_params("parallel", "arbitrary"),
    )(u, u, dy)


def _of_layer(w, layer):
    return (w[layer], 0) if isinstance(w, (list, tuple)) else (w, layer)


def _layer_matrix(w, layer):
    w, at = _of_layer(w, layer)
    return w[at]


def _ffn_fwd(x, g, w_in, w_out, layer, tag, gather=None):
    (w_in, at), (w_out, _) = _of_layer(w_in, layer), _of_layer(w_out, layer)
    y, u, *gathered = _ffn_fwd_call(x, g[layer], w_in, w_out, at, name=f"{tag}_fwd", gather=gather)
    return y, (x, u), gathered


def _ffn_bwd(saved, g, w_in, w_out, layer, dy, tag, exchange=None):
    x, u = saved
    (w_in, at), (w_out, _) = _of_layer(w_in, layer), _of_layer(w_out, layer)
    du, dx, dg, h, *parts = _ffn_bwd_call(x, g[layer], u, dy, w_in, w_out, at, name=f"{tag}_bwd", exchange=exchange)
    dw_out = _ffn_dw_out(u, dy, name=f"{tag}_dwout")
    dw_in = _mm(h, du, mode="tn", name=f"{tag}_dwin")
    return dx, dg[0], dw_in, dw_out, (parts[0] if parts else None)


ATTN_SCALE = ATTN_HEAD_DIM ** -0.5
NEG_BIG = -1e30
PROJ_AB_BLOCKS = 5


def _first_head(n):
    return _cols((n, LANES)) < ATTN_HEAD_DIM


def _per_head(tiles):
    first = _first_head(tiles[0].shape[0])
    return jnp.stack([jnp.where(first == (h == 0), t, 0.0) for t in tiles for h in (0, 1)], axis=0)


def _both(tiles):
    return jnp.stack([t for t in tiles for _ in (0, 1)], axis=0)


def _head_cols(tiles):
    return jnp.stack([t[:, c0:c0 + 1] for t in tiles for c0 in (0, ATTN_HEAD_DIM)], axis=0)


def _join_heads(v):
    return [v[2 * u] + v[2 * u + 1] for u in range(v.shape[0] // 2)]


def _spread_heads(v):
    first = _first_head(v.shape[1])
    return [jnp.where(first, v[2 * u], v[2 * u + 1]) for u in range(v.shape[0] // 2)]


def _band_masks(has_prev):
    qi, kj = _rows((ATTN_BLOCK, ATTN_BLOCK)), _cols((ATTN_BLOCK, ATTN_BLOCK))
    return (kj >= qi) & has_prev, kj <= qi


def _dattn_delta(o, dcat, *, name):
    s_len = o.shape[0]
    tm = _tile(s_len, (512, 256))

    def body(o_ref, do_ref, out_ref):
        r, c = _rows((ATTN_WIDTH, ATTN_WIDTH)), _cols((ATTN_WIDTH, ATTN_WIDTH))
        ones_bd = (r // ATTN_HEAD_DIM == c // ATTN_HEAD_DIM).astype(F32)
        out_ref[...] = _dg(o_ref[...] * do_ref[...], ones_bd, ((1,), (0,)), hi=True)

    blk = pl.BlockSpec((tm, ATTN_WIDTH), lambda i: (i, 0))
    return pl.pallas_call(
        body, name=name, grid=(s_len // tm,), in_specs=[blk, blk], out_specs=blk,
        out_shape=jax.ShapeDtypeStruct((s_len, ATTN_WIDTH), F32), compiler_params=_params("parallel"),
    )(o, dcat)


ATTN_UNITS = 4


def _units(it, d):
    if d == 1:
        return [(pl.ds(0, ATTN_BLOCK), pl.ds(p * LANES, LANES)) for p in range(ATTN_UNITS)]
    return [(pl.ds(it * ATTN_UNITS + u, ATTN_BLOCK, stride=d), pl.ds(0, LANES)) for u in range(ATTN_UNITS)]


def _tiles(ref, units):
    return [ref[rows, lanes] for rows, lanes in units]


def _store_tiles(ref, units, tiles):
    for (rows, lanes), t in zip(units, tiles):
        ref[rows, lanes] = t


def _stacked(a_tiles, b_tiles):
    return [jnp.concatenate([a, b], axis=0) for a, b in zip(a_tiles, b_tiles)]


def _passes(d):
    return max(d // ATTN_UNITS, 1)


def _pairs_per_step(d):
    return ATTN_PAIRS if d == 1 else 1


def _pair_specs(d, n_of):
    pairs = _pairs_per_step(d)
    groups = ATTN_PAIRS // pairs
    return lambda c: pl.BlockSpec((ATTN_BLOCK * d, LANES * pairs), lambda n, p: (n_of(n), c * groups + p))


def _sattn_fwd(proj, state, d, *, last, name):
    s_len = proj.shape[0]
    nb = s_len // (ATTN_BLOCK * d)
    first = state is None
    n_out = 2 if last else 3

    def body(*refs):
        q_ref, kp_ref, kc_ref, vp_ref, vc_ref = refs[:5]
        st_refs = () if first else refs[5:8]
        out_refs = refs[-n_out:]
        ok = jnp.concatenate(_band_masks(pl.program_id(0) > 0), axis=1)

        def one_pass(it, carry):
            units = _units(it, d)
            kcat = _stacked(_tiles(kp_ref, units), _tiles(kc_ref, units))
            vcat = _stacked(_tiles(vp_ref, units), _tiles(vc_ref, units))
            s = jnp.where(ok, _bdg(_per_head(_tiles(q_ref, units)), _both(kcat), 2, 2) * ATTN_SCALE, NEG_BIG)
            m_new = jnp.max(s, axis=2, keepdims=True)
            if not first:
                m_old = _head_cols(_tiles(st_refs[0], units))
                m_new = jnp.maximum(m_old, m_new)
                alpha = jnp.exp(m_old - m_new)
            p = jnp.exp(s - m_new)
            l_new = jnp.sum(p, axis=2, keepdims=True)
            acc = _join_heads(_bdg(p, _per_head(vcat), 2, 1))
            if not first:
                l_new = l_new + _head_cols(_tiles(st_refs[1], units)) * alpha
                acc = [a + a_in * sp for a, a_in, sp in zip(acc, _tiles(st_refs[2], units), _spread_heads(alpha))]
            m_pair, l_pair = _spread_heads(m_new), _spread_heads(l_new)
            if last:
                _store_tiles(out_refs[0], units, [a / l for a, l in zip(acc, l_pair)])
                _store_tiles(out_refs[1], units, [m + jnp.log(l) for m, l in zip(m_pair, l_pair)])
            else:
                _store_tiles(out_refs[0], units, m_pair)
                _store_tiles(out_refs[1], units, l_pair)
                _store_tiles(out_refs[2], units, acc)
            return carry

        lax.fori_loop(0, _passes(d), one_pass, 0)

    cur, prev = _pair_specs(d, lambda n: n), _pair_specs(d, lambda n: jnp.maximum(n - 1, 0))
    st = cur(0)
    return tuple(pl.pallas_call(
        body, name=name, grid=(nb, ATTN_PAIRS // _pairs_per_step(d)),
        in_specs=[cur(0), prev(1), cur(1), prev(2), cur(2)] + ([] if first else [st] * 3),
        out_specs=[st] * n_out, out_shape=[jax.ShapeDtypeStruct((s_len, ATTN_WIDTH), F32)] * n_out,
        compiler_params=_params("arbitrary", "parallel"),
    )(*([proj] * 5 + ([] if first else list(state)))))


def _dattn_forward(proj, tag):
    state = None
    for i, d in enumerate(DILATIONS):
        state = _sattn_fwd(proj, state, d, last=i == len(DILATIONS) - 1, name=f"{tag}_attn_d{d}")
    return state


def _sattn_bwd(proj, dcat, lse, delta, grads_in, d, *, name):
    s_len = proj.shape[0]
    nb = s_len // (ATTN_BLOCK * d)
    first = grads_in is None
    groups = ATTN_PAIRS // _pairs_per_step(d)

    def body(*refs):
        q_ref, kp_ref, kc_ref, vp_ref, vc_ref, do_ref, lse_ref, dl_ref = refs[:8]
        dq_in, dk_in, dv_in = (None, None, None) if first else refs[8:11]
        dq_ref, dk_ref, dv_ref, carry_k, carry_v = refs[-5:]
        n = pl.program_id(1)
        ok = jnp.concatenate(_band_masks(n > 0), axis=1)

        @pl.when(n == 0)
        def _():
            carry_k[...] = jnp.zeros_like(carry_k)
            carry_v[...] = jnp.zeros_like(carry_v)

        def leave(ref, carry, units, extra, into):
            out = [c + e for c, e in zip(_tiles(carry, units), extra)] if extra else _tiles(carry, units)
            if into is not None:
                out = [a + b for a, b in zip(out, _tiles(into, units))]
            _store_tiles(ref, units, out)

        def one_pass(it, carry):
            units = _units(it, d)
            kcat = _stacked(_tiles(kp_ref, units), _tiles(kc_ref, units))
            vcat = _stacked(_tiles(vp_ref, units), _tiles(vc_ref, units))
            q2, do2 = _per_head(_tiles(q_ref, units)), _per_head(_tiles(do_ref, units))
            s = _bdg(q2, _both(kcat), 2, 2) * ATTN_SCALE
            pr = jnp.where(ok, jnp.exp(jnp.where(ok, s, NEG_BIG) - _head_cols(_tiles(lse_ref, units))), 0.0)
            ds = pr * (_bdg(do2, _both(vcat), 2, 2) - _head_cols(_tiles(dl_ref, units)))
            dq = [t * ATTN_SCALE for t in _join_heads(_bdg(ds, _per_head(kcat), 2, 1))]
            if not first:
                dq = [a + b for a, b in zip(dq, _tiles(dq_in, units))]
            _store_tiles(dq_ref, units, dq)
            dk = [t * ATTN_SCALE for t in _join_heads(_bdg(ds, q2, 1, 1))]
            dv = _join_heads(_bdg(pr, do2, 1, 1))
            leave(dk_ref, carry_k, units, [t[:ATTN_BLOCK] for t in dk], dk_in)
            leave(dv_ref, carry_v, units, [t[:ATTN_BLOCK] for t in dv], dv_in)
            _store_tiles(carry_k, units, [t[ATTN_BLOCK:] for t in dk])
            _store_tiles(carry_v, units, [t[ATTN_BLOCK:] for t in dv])
            return carry

        def last_pass(it, carry):
            units = _units(it, d)
            leave(dk_ref, carry_k, units, None, dk_in)
            leave(dv_ref, carry_v, units, None, dv_in)
            return carry

        @pl.when(n < nb)
        def _():
            lax.fori_loop(0, _passes(d), one_pass, 0)

        @pl.when(n == nb)
        def _():
            lax.fori_loop(0, _passes(d), last_pass, 0)

    pairs = _pairs_per_step(d)
    blk = (ATTN_BLOCK * d, LANES * pairs)
    at = lambda n_of: (lambda c: pl.BlockSpec(blk, lambda p, n: (n_of(n), c * groups + p)))
    here = lambda n: jnp.minimum(n, nb - 1)
    cur, prev, lag = at(here), at(lambda n: jnp.maximum(here(n) - 1, 0)), at(lambda n: jnp.maximum(n - 1, 0))
    st, st_lag = cur(0), lag(0)
    return tuple(pl.pallas_call(
        body, name=name, grid=(groups, nb + 1),
        in_specs=[cur(0), prev(1), cur(1), prev(2), cur(2), st, st, st] + ([] if first else [st, st_lag, st_lag]),
        out_specs=[st, st_lag, st_lag], out_shape=[jax.ShapeDtypeStruct((s_len, ATTN_WIDTH), F32)] * 3,
        scratch_shapes=[pltpu.VMEM(blk, F32)] * 2, compiler_params=_params("parallel", "arbitrary"),
    )(*([proj] * 5 + [dcat, lse, delta] + ([] if first else list(grads_in)))))


def _dattn_backward(proj, o, lse, dcat, tag):
    delta = _dattn_delta(o, dcat, name=f"{tag}_attn_delta")
    grads = None
    for d in DILATIONS:
        grads = _sattn_bwd(proj, dcat, lse, delta, grads, d, name=f"{tag}_attn_bwd_d{d}")
    return grads


CONV_TC = 512
CONV_T = 512


def _conv_tiles(s_len, cb0, width):
    wide = 2 * CONV_TC
    tc = wide if width % wide == 0 and (cb0 * CONV_TC) % wide == 0 else CONV_TC
    return _tile(s_len, (CONV_T, CONV_T // 2)), tc, cb0 * CONV_TC // tc


def _shift_down(ext, k, t):
    return (pltpu.roll(ext, k, 0) if k else ext)[SUBLANES:SUBLANES + t]


def _conv_fwd(src, cb0, width, w8, *, name):
    s_len = src.shape[0]
    t, tc, cb = _conv_tiles(s_len, cb0, width)
    tpb = t // SUBLANES

    def body(x_ref, h_ref, w_ref, y_ref):
        halo = jnp.where(pl.program_id(0) > 0, h_ref[...], 0.0)
        ext = jnp.concatenate([halo, x_ref[...]], axis=0)
        w = w_ref[...]
        y = jnp.broadcast_to(w[CONV_K:CONV_K + 1], (t, tc))
        for k in range(CONV_K):
            y = y + w[k:k + 1] * _shift_down(ext, CONV_K - 1 - k, t)
        y_ref[...] = y

    return pl.pallas_call(
        body, name=name, grid=(s_len // t, width // tc),
        in_specs=[pl.BlockSpec((t, tc), lambda i, j: (i, cb + j)),
                  pl.BlockSpec((SUBLANES, tc), lambda i, j: (jnp.maximum(i * tpb - 1, 0), cb + j)),
                  pl.BlockSpec((SUBLANES, tc), lambda i, j: (0, j))],
        out_specs=pl.BlockSpec((t, tc), lambda i, j: (i, j)), out_shape=jax.ShapeDtypeStruct((s_len, width), F32),
        compiler_params=_params("parallel", "parallel"),
    )(src, src, w8)


def _conv_bwd(src, cb0, width, w8, dy, *, name):
    s_len = src.shape[0]
    t, tc, cb = _conv_tiles(s_len, cb0, width)
    tpb = t // SUBLANES
    ni = s_len // t

    def body(x_ref, h_ref, w_ref, dy_ref, dn_ref, dx_ref, dw_ref):
        i = pl.program_id(1)
        halo = jnp.where(i > 0, h_ref[...], 0.0)
        ext = jnp.concatenate([halo, x_ref[...]], axis=0)
        dyv = dy_ref[...]
        extn = jnp.concatenate([dyv, jnp.where(i < ni - 1, dn_ref[...], 0.0)], axis=0)
        w = w_ref[...]
        row = _rows((SUBLANES, tc))
        dx = jnp.zeros((t, tc), F32)
        dw = jnp.where(row == CONV_K, jnp.sum(dyv, axis=0, keepdims=True), 0.0)
        for k in range(CONV_K):
            up = CONV_K - 1 - k
            dx = dx + w[k:k + 1] * (pltpu.roll(extn, t + SUBLANES - up, 0) if up else extn)[:t]
            dw = dw + jnp.where(row == k, jnp.sum(dyv * _shift_down(ext, up, t), axis=0, keepdims=True), 0.0)
        dx_ref[...] = dx.astype(dx_ref.dtype)

        @pl.when(i == 0)
        def _():
            dw_ref[...] = jnp.zeros_like(dw_ref)

        dw_ref[...] += dw

    return pl.pallas_call(
        body, name=name, grid=(width // tc, ni),
        in_specs=[pl.BlockSpec((t, tc), lambda j, i: (i, cb + j)),
                  pl.BlockSpec((SUBLANES, tc), lambda j, i: (jnp.maximum(i * tpb - 1, 0), cb + j)),
                  pl.BlockSpec((SUBLANES, tc), lambda j, i: (0, j)),
                  pl.BlockSpec((t, tc), lambda j, i: (i, j)),
                  pl.BlockSpec((SUBLANES, tc), lambda j, i: (jnp.minimum((i + 1) * tpb, s_len // SUBLANES - 1), j))],
        out_specs=[pl.BlockSpec((t, tc), lambda j, i: (i, j)), pl.BlockSpec((SUBLANES, tc), lambda j, i: (0, j))],
        out_shape=[jax.ShapeDtypeStruct((s_len, width), MXU_DTYPE), jax.ShapeDtypeStruct((SUBLANES, width), F32)],
        compiler_params=_params("parallel", "arbitrary"),
    )(src, src, w8, dy, dy)


LRU_T = 256


def _lru_gates(xc, wa, wx, ba, bx, lam):
    r = _sigmoid(_bdot(xc, wa) + ba)
    i = _sigmoid(_bdot(xc, wx) + bx)
    log_a = (-LRU_C) * r * _softplus(-lam)
    return jnp.exp(log_a), jnp.sqrt(_neg_expm1(2.0 * log_a)) * i * xc


def _block_scan(a, b, state, reverse):
    t = a.shape[0]
    row = _rows(a.shape) % SUBLANES
    s = 1
    while s < SUBLANES:
        shift, ok = (t - s, row < SUBLANES - s) if reverse else (s, row >= s)
        b = jnp.where(ok, a * pltpu.roll(b, shift, 0) + b, b)
        a = jnp.where(ok, a * pltpu.roll(a, shift, 0), a)
        s *= 2
    groups = range(t // SUBLANES)
    out = [None] * len(groups)
    for g in (reversed(groups) if reverse else groups):
        rows = slice(g * SUBLANES, (g + 1) * SUBLANES)
        out[g] = b[rows] + a[rows] * state
        state = out[g][0:1] if reverse else out[g][SUBLANES - 1:SUBLANES]
    return jnp.concatenate(out, axis=0)


def _lru_fwd(xc, proj, wa, wx, ba, bx, lam, *, name):
    s_len, w = xc.shape
    t = _tile(s_len, (LRU_T,))

    def body(xc_ref, gr_ref, wa_ref, wx_ref, ba_ref, bx_ref, lam_ref, h_ref, y_ref, carry):
        @pl.when(pl.program_id(0) == 0)
        def _():
            carry[...] = jnp.zeros_like(carry)

        a, b = _lru_gates(xc_ref[...], wa_ref[...], wx_ref[...], ba_ref[...], bx_ref[...], lam_ref[...])
        h = _block_scan(a, b, carry[0:1, :], False)
        h_ref[...] = h
        y_ref[...] = (h * _gelu(gr_ref[...])).astype(y_ref.dtype)
        carry[0:1, :] = h[t - 1:t, :]

    row = pl.BlockSpec((t, w), lambda i: (i, 0))
    mat = pl.BlockSpec((w, w), lambda i: (0, 0))
    vec = pl.BlockSpec((1, w), lambda i: (0, 0))
    return pl.pallas_call(
        body, name=name, grid=(s_len // t,),
        in_specs=[row, pl.BlockSpec((t, w), lambda i: (i, PROJ_AB_BLOCKS - 1)), mat, mat, vec, vec, vec],
        out_specs=[row, row], out_shape=[jax.ShapeDtypeStruct((s_len, w), F32), jax.ShapeDtypeStruct((s_len, w), MXU_DTYPE)],
        scratch_shapes=[pltpu.VMEM((SUBLANES, w), F32)], compiler_params=_params("arbitrary"),
    )(xc, proj, wa, wx, ba, bx, lam)


def _lru_bwd(xc, proj, hs, dcat, wa, wx, ba, bx, lam, *, name):
    s_len, w = xc.shape
    t = _tile(s_len, (LRU_T,))
    nb = s_len // t
    tpb = t // SUBLANES

    def body(xc_ref, gr_ref, h_ref, hp_ref, dy_ref, wa_ref, wx_ref, ba_ref, bx_ref, lam_ref,
             dxc_ref, dgr_ref, dwa_ref, dwx_ref, dba_ref, dbx_ref, dlam_ref, carry):
        step = pl.program_id(0)
        params = (wa_ref[...], wx_ref[...], ba_ref[...], bx_ref[...], lam_ref[...])

        @pl.when(step == 0)
        def _():
            carry[...] = jnp.zeros_like(carry)
            for r in (dwa_ref, dwx_ref, dba_ref, dbx_ref, dlam_ref):
                r[...] = jnp.zeros_like(r)

        (a, _), vjp = jax.vjp(_lru_gates, xc_ref[...], *params)
        gr, h, dy = gr_ref[...], h_ref[...], dy_ref[...]
        gel, gel_vjp = jax.vjp(_gelu, gr)
        dgr_ref[...] = gel_vjp(dy * h)[0].astype(dgr_ref.dtype)
        dh = dy * gel
        big_g = _block_scan(a, a * dh, carry[0:1, :], True)
        row = _rows((t, w))
        g = dh + jnp.where(row == t - 1, carry[0:1, :], pltpu.roll(big_g, t - 1, 0))
        carry[0:1, :] = big_g[0:1, :]
        h_last = jnp.where(step < nb - 1, hp_ref[SUBLANES - 1:SUBLANES, :], 0.0)
        h_prev = jnp.where(row == 0, h_last, pltpu.roll(h, 1, 0))
        dxc, dwa, dwx, dba, dbx, dlam = vjp((g * h_prev, g))
        dxc_ref[...] = dxc
        dwa_ref[...] += dwa
        dwx_ref[...] += dwx
        dba_ref[...] += dba
        dbx_ref[...] += dbx
        dlam_ref[...] += dlam

    rev = lambda i: nb - 1 - i
    row = pl.BlockSpec((t, w), lambda i: (rev(i), 0))
    mat = pl.BlockSpec((w, w), lambda i: (0, 0))
    vec = pl.BlockSpec((1, w), lambda i: (0, 0))
    return pl.pallas_call(
        body, name=name, grid=(nb,),
        in_specs=[row, pl.BlockSpec((t, w), lambda i: (rev(i), PROJ_AB_BLOCKS - 1)), row,
                  pl.BlockSpec((SUBLANES, w), lambda i: (jnp.maximum(rev(i) * tpb - 1, 0), 0)),
                  pl.BlockSpec((t, w), lambda i: (rev(i), 1)), mat, mat, vec, vec, vec],
        out_specs=[row, row, mat, mat, vec, vec, vec],
        out_shape=[jax.ShapeDtypeStruct((s_len, w), F32), jax.ShapeDtypeStruct((s_len, w), MXU_DTYPE)]
        + [jax.ShapeDtypeStruct((w, w), F32)] * 2 + [jax.ShapeDtypeStruct((1, w), F32)] * 3,
        scratch_shapes=[pltpu.VMEM((SUBLANES, w), F32)], compiler_params=_params("arbitrary"),
    )(xc, proj, hs, hs, dcat, wa, wx, ba, bx, lam)


XA_T = 256
XA_SCALE = XA_HEAD_DIM ** -0.5


def _xa_heads(q, k, v):
    s = _bmm_nt(q, k) * XA_SCALE
    e = jnp.exp(s - jnp.max(s, axis=-1, keepdims=True))
    return _bmm(e / jnp.sum(e, axis=-1, keepdims=True), v)


def _xa_stack(ref):
    return jnp.stack([ref[:, h * XA_HEAD_DIM:(h + 1) * XA_HEAD_DIM].astype(F32) for h in range(XA_HEADS)], axis=0)


def _xa_fwd(q, kv, *, name):
    s_len, d = q.shape
    n_mem = kv.shape[0]
    t = _tile(s_len, (XA_T,))

    def body(q_ref, k_ref, v_ref, o_ref):
        o = _xa_heads(_xa_stack(q_ref), _xa_stack(k_ref), _xa_stack(v_ref))
        for h in range(XA_HEADS):
            o_ref[:, h * XA_HEAD_DIM:(h + 1) * XA_HEAD_DIM] = o[h].astype(o_ref.dtype)

    return pl.pallas_call(
        body, name=name, grid=(s_len // t,),
        in_specs=[pl.BlockSpec((t, d), lambda i: (i, 0)), pl.BlockSpec((n_mem, d), lambda i: (0, 0)),
                  pl.BlockSpec((n_mem, d), lambda i: (0, 1))],
        out_specs=pl.BlockSpec((t, d), lambda i: (i, 0)), out_shape=jax.ShapeDtypeStruct((s_len, d), MXU_DTYPE),
        compiler_params=_params("parallel"),
    )(q, kv, kv)


def _xa_bwd(q, kv, dy, wo, *, name):
    s_len, d = q.shape
    n_mem = kv.shape[0]
    t = _tile(s_len, (XA_T,))

    def body(q_ref, k_ref, v_ref, dy_ref, wo_ref, dq_ref, dk_ref, dv_ref):
        @pl.when(pl.program_id(0) == 0)
        def _():
            dk_ref[...] = jnp.zeros_like(dk_ref)
            dv_ref[...] = jnp.zeros_like(dv_ref)

        do = _dg(dy_ref[...], wo_ref[...], ((1,), (1,)))
        do = jnp.stack([do[:, h * XA_HEAD_DIM:(h + 1) * XA_HEAD_DIM] for h in range(XA_HEADS)], axis=0)
        _, vjp = jax.vjp(_xa_heads, _xa_stack(q_ref), _xa_stack(k_ref), _xa_stack(v_ref))
        dq, dk, dv = vjp(do)
        for h in range(XA_HEADS):
            sl = slice(h * XA_HEAD_DIM, (h + 1) * XA_HEAD_DIM)
            dq_ref[:, sl] = dq[h].astype(dq_ref.dtype)
            dk_ref[:, sl] += dk[h]
            dv_ref[:, sl] += dv[h]

    row = pl.BlockSpec((t, d), lambda i: (i, 0))
    dq, dk, dv = pl.pallas_call(
        body, name=name, grid=(s_len // t,),
        in_specs=[row, pl.BlockSpec((n_mem, d), lambda i: (0, 0)), pl.BlockSpec((n_mem, d), lambda i: (0, 1)), row,
                  pl.BlockSpec(wo.shape, lambda i: (0, 0), pipeline_mode=RESIDENT)],
        out_specs=[row, pl.BlockSpec((n_mem, d), lambda i: (0, 0)), pl.BlockSpec((n_mem, d), lambda i: (0, 0))],
        out_shape=[jax.ShapeDtypeStruct((s_len, d), MXU_DTYPE)] + [jax.ShapeDtypeStruct((n_mem, d), F32)] * 2,
        compiler_params=_params("arbitrary"),
    )(q, kv, kv, dy, wo)
    return dq, jnp.concatenate([dk, dv], axis=1)


DN_Q_SCALE = DN_HEAD_DIM ** -0.5
L2_EPS = 1e-6


def _bdg(a, b, ca, cb):
    return lax.dot_general(a.astype(MXU_DTYPE), b.astype(MXU_DTYPE), (((ca,), (cb,)), ((0,), (0,))), preferred_element_type=F32)


@jax.custom_vjp
def _bmm(a, b):
    return _bdg(a, b, 2, 1)


_bmm.defvjp(lambda a, b: (_bdg(a, b, 2, 1), (a, b)), lambda r, g: (_bdg(g, r[1], 2, 2), _bdg(r[0], g, 1, 1)))


@jax.custom_vjp
def _bmm_nt(a, b):
    return _bdg(a, b, 2, 2)


_bmm_nt.defvjp(lambda a, b: (_bdg(a, b, 2, 2), (a, b)), lambda r, g: (_bdg(g, r[1], 2, 1), _bdg(g, r[0], 1, 1)))


@jax.custom_vjp
def _bmm_tn(a, b):
    return _bdg(a, b, 1, 1)


_bmm_tn.defvjp(lambda a, b: (_bdg(a, b, 1, 1), (a, b)), lambda r, g: (_bdg(r[1], g, 2, 2), _bdg(r[0], g, 2, 1)))


def _tri_inverse(n):
    eye = (lax.broadcasted_iota(jnp.int32, n.shape, 1) == lax.broadcasted_iota(jnp.int32, n.shape, 2)).astype(F32)
    inv, p = eye - n, n
    for _ in range(5):
        p = _bdg(p, p, 2, 1)
        inv = _bdg(inv, eye + p, 2, 1)
    return inv


@jax.custom_vjp
def _tri_solve2(n, r1, r2):
    t = _tri_inverse(n)
    return _bdg(t, r1, 2, 1), _bdg(t, r2, 2, 1)


def _tri_solve2_fwd(n, r1, r2):
    t = _tri_inverse(n)
    x1, x2 = _bdg(t, r1, 2, 1), _bdg(t, r2, 2, 1)
    return (x1, x2), (t, x1, x2)


def _tri_solve2_bwd(saved, cts):
    t, x1, x2 = saved
    d1, d2 = _bdg(t, cts[0], 1, 1), _bdg(t, cts[1], 1, 1)
    return -(_bdg(d1, x1, 2, 2) + _bdg(d2, x2, 2, 2)), d1, d2


_tri_solve2.defvjp(_tri_solve2_fwd, _tri_solve2_bwd)


def _dn_gates(ab, alog, dtb):
    return -jnp.exp(alog) * _softplus(ab + dtb), _sigmoid(ab)


def _dn_heads(cq, ck, cv, z, g, beta, onorm, state):
    h, c, _ = cq.shape
    l2 = lambda t: t * lax.rsqrt(jnp.sum(t * t, axis=-1, keepdims=True) + L2_EPS)
    q, k, v = l2(_silu(cq)) * DN_Q_SCALE, l2(_silu(ck)), _silu(cv)
    r, cc = lax.broadcasted_iota(jnp.int32, (h, c, c), 1), lax.broadcasted_iota(jnp.int32, (h, c, c), 2)
    tri, eye = r >= cc, r == cc
    g_row = jnp.sum(jnp.where(eye, g, 0.0), axis=1, keepdims=True)
    gcum_c = jnp.sum(jnp.where(tri, g_row, 0.0), axis=2, keepdims=True)
    gcum_r = jnp.sum(jnp.where(cc >= r, g, 0.0), axis=1, keepdims=True)
    decay = jnp.where(tri, jnp.exp(jnp.where(tri, gcum_c - gcum_r, 0.0)), 0.0)
    kb = k * beta
    n = jnp.where(r > cc, _bmm_nt(kb, k) * decay, 0.0)
    u, w = _tri_solve2(n, v * beta, kb * jnp.exp(gcum_c))
    v_new = u - _bmm(w, state)
    o = _bmm(q * jnp.exp(gcum_c), state) + _bmm(_bmm_nt(q, k) * decay, v_new)
    g_last = jnp.sum(g, axis=1, keepdims=True)
    new_state = state * jnp.exp(g_last) + _bmm_tn(k * jnp.exp(g_last - gcum_c), v_new)
    on = o * lax.rsqrt(jnp.mean(o * o, axis=-1, keepdims=True) + NORM_EPS) * onorm
    return on * _silu(z), new_state


DN_STEP = 4


def _dn_stack(ref, col0, rows=slice(None)):
    return jnp.stack([ref[rows, col0 + h * DN_HEAD_DIM:col0 + (h + 1) * DN_HEAD_DIM].astype(F32) for h in range(DN_HEADS)], axis=0)


def _dn_cols(block, col0):
    return jnp.stack([block[:, col0 + h:col0 + h + 1] for h in range(DN_HEADS)], axis=0)


def _dn_fwd(cqkv, proj, ab, alog, dtb, onorm, *, name):
    s_len = cqkv.shape[0]
    c, hd, w = DN_CHUNK, DN_HEAD_DIM, DN_WIDTH
    n_chunks = s_len // c

    def body(c_ref, z_ref, ab_ref, alog_ref, dtb_ref, on_ref, o_ref, st_ref, state):
        @pl.when(pl.program_id(0) == 0)
        def _():
            state[...] = jnp.zeros_like(state)

        g_all, beta_all = _dn_gates(ab_ref[...], alog_ref[...], dtb_ref[...])
        st = state[...]
        for j in range(DN_STEP):
            rows = slice(j * c, (j + 1) * c)
            st_ref[j] = st
            out, st = _dn_heads(_dn_stack(c_ref, 0, rows), _dn_stack(c_ref, w, rows), _dn_stack(c_ref, 2 * w, rows),
                                _dn_stack(z_ref, 0, rows), _dn_cols(g_all[rows], 0), _dn_cols(beta_all[rows], DN_HEADS), on_ref[...], st)
            for h in range(DN_HEADS):
                o_ref[rows, h * hd:(h + 1) * hd] = out[h].astype(o_ref.dtype)
        state[...] = st

    vec = pl.BlockSpec((1, LANES), lambda i: (0, 0))
    t = DN_STEP * c
    return pl.pallas_call(
        body, name=name, grid=(n_chunks // DN_STEP,),
        in_specs=[pl.BlockSpec((t, 3 * w), lambda i: (i, 0)), pl.BlockSpec((t, w), lambda i: (i, 3)),
                  pl.BlockSpec((t, LANES), lambda i: (i, 0)), vec, vec, vec],
        out_specs=[pl.BlockSpec((t, w), lambda i: (i, 0)), pl.BlockSpec((DN_STEP, DN_HEADS, hd, hd), lambda i: (i, 0, 0, 0))],
        out_shape=[jax.ShapeDtypeStruct((s_len, w), MXU_DTYPE), jax.ShapeDtypeStruct((n_chunks, DN_HEADS, hd, hd), F32)],
        scratch_shapes=[pltpu.VMEM((DN_HEADS, hd, hd), F32)], compiler_params=_params("arbitrary"),
    )(cqkv, proj, ab, alog, dtb, onorm)


def _dn_bwd(cqkv, proj, ab, alog, dtb, onorm, states, dout, *, name):
    s_len = cqkv.shape[0]
    c, hd, w = DN_CHUNK, DN_HEAD_DIM, DN_WIDTH
    n_chunks = s_len // c

    def body(c_ref, z_ref, ab_ref, alog_ref, dtb_ref, on_ref, st_ref, do_ref,
             dc_ref, dz_ref, dab_ref, dalog_ref, ddtb_ref, don_ref, dstate):
        @pl.when(pl.program_id(0) == 0)
        def _():
            dstate[...] = jnp.zeros_like(dstate)
            for r in (dalog_ref, ddtb_ref, don_ref):
                r[...] = jnp.zeros_like(r)

        (g_all, beta_all), gates_vjp = jax.vjp(_dn_gates, ab_ref[...], alog_ref[...], dtb_ref[...])
        dst = dstate[...]
        col = _cols((c, LANES))
        dg_parts, dbeta_parts = [None] * DN_STEP, [None] * DN_STEP
        don = jnp.zeros((1, hd), F32)
        for j in reversed(range(DN_STEP)):
            rows = slice(j * c, (j + 1) * c)
            _, vjp = jax.vjp(_dn_heads, _dn_stack(c_ref, 0, rows), _dn_stack(c_ref, w, rows), _dn_stack(c_ref, 2 * w, rows),
                             _dn_stack(z_ref, 0, rows), _dn_cols(g_all[rows], 0), _dn_cols(beta_all[rows], DN_HEADS), on_ref[...], st_ref[j])
            dcq, dck, dcv, dz, dg, dbeta, don_j, dst = vjp((_dn_stack(do_ref, 0, rows), dst))
            dg_all, dbeta_all = jnp.zeros((c, LANES), F32), jnp.zeros((c, LANES), F32)
            for h in range(DN_HEADS):
                sl = slice(h * hd, (h + 1) * hd)
                dc_ref[rows, sl] = dcq[h]
                dc_ref[rows, w + h * hd:w + (h + 1) * hd] = dck[h]
                dc_ref[rows, 2 * w + h * hd:2 * w + (h + 1) * hd] = dcv[h]
                dz_ref[rows, sl] = dz[h].astype(dz_ref.dtype)
                dg_all = dg_all + jnp.where(col == h, dg[h], 0.0)
                dbeta_all = dbeta_all + jnp.where(col == DN_HEADS + h, dbeta[h], 0.0)
            dg_parts[j], dbeta_parts[j] = dg_all, dbeta_all
            don = don + don_j
        dstate[...] = dst
        dab, dalog, ddtb = gates_vjp((jnp.concatenate(dg_parts, axis=0), jnp.concatenate(dbeta_parts, axis=0)))
        dab_ref[...] = dab
        dalog_ref[...] += dalog
        ddtb_ref[...] += ddtb
        don_ref[...] += don

    steps = n_chunks // DN_STEP
    rev = lambda i: steps - 1 - i
    vec = pl.BlockSpec((1, LANES), lambda i: (0, 0))
    t = DN_STEP * c
    return pl.pallas_call(
        body, name=name, grid=(steps,),
        in_specs=[pl.BlockSpec((t, 3 * w), lambda i: (rev(i), 0)), pl.BlockSpec((t, w), lambda i: (rev(i), 3)),
                  pl.BlockSpec((t, LANES), lambda i: (rev(i), 0)), vec, vec, vec,
                  pl.BlockSpec((DN_STEP, DN_HEADS, hd, hd), lambda i: (rev(i), 0, 0, 0)), pl.BlockSpec((t, w), lambda i: (rev(i), 0))],
        out_specs=[pl.BlockSpec((t, 3 * w), lambda i: (rev(i), 0)), pl.BlockSpec((t, w), lambda i: (rev(i), 0)),
                   pl.BlockSpec((t, LANES), lambda i: (rev(i), 0)), vec, vec, vec],
        out_shape=[jax.ShapeDtypeStruct((s_len, 3 * w), F32), jax.ShapeDtypeStruct((s_len, w), MXU_DTYPE),
                   jax.ShapeDtypeStruct((s_len, LANES), F32)] + [jax.ShapeDtypeStruct((1, LANES), F32)] * 3,
        scratch_shapes=[pltpu.VMEM((DN_HEADS, hd, hd), F32)], compiler_params=_params("arbitrary"),
    )(cqkv, proj, ab, alog, dtb, onorm, states, dout)


def _final_loss(x, g, target, *, name):
    s, d = x.shape
    tm = _tile(s, (512, 256))

    def body(x_ref, g_ref, t_ref, loss_ref, dx_ref, dg_ref):
        @pl.when(pl.program_id(0) == 0)
        def _():
            loss_ref[...] = jnp.zeros_like(loss_ref)
            dg_ref[...] = jnp.zeros_like(dg_ref)

        xv, gv = x_ref[...], g_ref[...]
        r = lax.rsqrt(jnp.mean(xv * xv, axis=-1, keepdims=True) + NORM_EPS)
        xh = xv * r
        err = xh * gv - t_ref[...]
        loss_ref[...] += 0.5 * jnp.sum(jnp.mean(err * err, axis=-1, keepdims=True), axis=0, keepdims=True)
        dy = err * (1.0 / d)
        dxh = dy * gv
        dx_ref[...] = r * (dxh - xh * jnp.mean(dxh * xh, axis=-1, keepdims=True))
        dg_ref[...] += jnp.sum(dy * xh, axis=0, keepdims=True)

    row = pl.BlockSpec((tm, d), lambda i: (i, 0))
    vec = pl.BlockSpec((1, d), lambda i: (0, 0))
    return pl.pallas_call(
        body, name=name, grid=(s // tm,), in_specs=[row, vec, row],
        out_specs=[pl.BlockSpec((1, LANES), lambda i: (0, 0)), row, vec],
        out_shape=[jax.ShapeDtypeStruct((1, LANES), F32), jax.ShapeDtypeStruct((s, d), F32), jax.ShapeDtypeStruct((1, d), F32)],
        compiler_params=_params("arbitrary"),
    )(x, g.reshape(1, d), target)


def _adamw(w, g_layers, m, v, *, name):
    shape = w.shape
    cols = shape[-1]
    n_l = len(g_layers)
    rows = max(w.size // cols, 1) // n_l
    tr = _tile(rows, [t for t in (512, 352, 256, 128, 64, 32, 16, 8) if t * cols * 4 <= ADAMW_BLOCK_BYTES])
    c1, c2 = 1.0 - ADAM_B1 ** ADAM_STEP, 1.0 - ADAM_B2 ** ADAM_STEP

    def body(*refs):
        w_ref, m_ref, v_ref = refs[:3]
        d_ref, nm_ref, nv_ref, g_ref = refs[3 + n_l:]
        gv = refs[3][...]
        for k in range(1, n_l):
            gv = jnp.where(pl.program_id(0) == k, refs[3 + k][...], gv)
        nm = ADAM_B1 * m_ref[...] + (1.0 - ADAM_B1) * gv
        nv = ADAM_B2 * v_ref[...] + (1.0 - ADAM_B2) * (gv * gv)
        d_ref[...] = -ADAM_LR * ((nm / c1) / (jnp.sqrt(nv / c2) + ADAM_EPS) + ADAM_WD * w_ref[...])
        nm_ref[...] = nm
        nv_ref[...] = nv
        g_ref[...] = gv

    blk = pl.BlockSpec((None, tr, cols), lambda l, i: (l, i, 0))
    slab = pl.BlockSpec((tr, cols), lambda l, i: (i, 0))
    outs = pl.pallas_call(
        body, name=name, grid=(n_l, rows // tr), in_specs=[blk] * 3 + [slab] * n_l, out_specs=[blk] * 4,
        out_shape=[jax.ShapeDtypeStruct((n_l, rows, cols), F32)] * 4, compiler_params=_params("parallel", "parallel"),
    )(*(t.reshape(n_l, rows, cols) for t in (w, m, v)), *(t.reshape(rows, cols) for t in g_layers))
    return tuple(t.reshape(shape) for t in outs)


def _block_diag(w):
    n, j, k = w.shape
    eye = jnp.eye(n, dtype=w.dtype)
    return (eye[:, None, :, None] * w[:, :, None, :]).reshape(n * j, n * k)


def _block_diag_part(m, n):
    j, k = m.shape[0] // n, m.shape[1] // n
    m4 = m.reshape(n, j, n, k)
    return jnp.stack([m4[i, :, i, :] for i in range(n)], axis=0)


DN_AB = 2 * DN_HEADS
DEPTH = 2


def _row(v, width=None):
    v = v.reshape(1, -1)
    return v if width is None else jnp.pad(v, ((0, 0), (0, width - v.shape[1])))


def _conv_w8(conv_w, bias=None):
    w8 = jnp.zeros((SUBLANES, conv_w.shape[1]), F32).at[:CONV_K].set(conv_w)
    return w8 if bias is None else w8.at[CONV_K].set(bias)


def _mixer_ab_fwd(x, w, tag):
    h, (proj,) = _norm_mm(x, w["mix_norm"][0], [w["ab_w_in"][0]], [F32], name=f"{tag}_in")
    o, lse = _dattn_forward(proj, tag)
    w8 = _conv_w8(w["lru_conv_w"][0], w["lru_conv_b"][0])
    xc = _conv_fwd(proj, PROJ_AB_BLOCKS - 2, LRU_WIDTH, w8, name=f"{tag}_conv")
    wa, wx = _block_diag(w["lru_w_a"][0]), _block_diag(w["lru_w_x"][0])
    vecs = (_row(w["lru_b_a"][0]), _row(w["lru_b_x"][0]), _row(w["lru_lambda"][0]))
    hs, y = _lru_fwd(xc, proj, wa, wx, *vecs, name=f"{tag}_lru")
    w_out = w["ab_w_out"][0]
    x2 = _mm(o, w_out[:ATTN_WIDTH], res=x, name=f"{tag}_out_attn")
    x2 = _mm(y, w_out[ATTN_WIDTH:], res=x2, name=f"{tag}_out_lru")
    return x2, (x, h, proj, o, lse, w8, xc, wa, wx, vecs, hs, y)


def _mixer_ab_bwd(saved, w, dy, tag):
    x, h, proj, o, lse, w8, xc, wa, wx, vecs, hs, y = saved
    w_out = w["ab_w_out"][0]
    dcat = _mm(dy, w_out, mode="nt", name=f"{tag}_dcat")
    dw_out = jnp.concatenate([_mm(o, dy, mode="tn", name=f"{tag}_dwout_attn"), _mm(y, dy, mode="tn", name=f"{tag}_dwout_lru")], axis=0)
    dq, dk, dv = _dattn_backward(proj, o, lse, dcat, tag)
    dxc, dgr, dwa, dwx, dba, dbx, dlam = _lru_bwd(xc, proj, hs, dcat, wa, wx, *vecs, name=f"{tag}_dlru")
    dxr, dw8 = _conv_bwd(proj, PROJ_AB_BLOCKS - 2, LRU_WIDTH, w8, dxc, name=f"{tag}_dconv")
    dproj = jnp.concatenate([t.astype(MXU_DTYPE) for t in (dq, dk, dv, dxr, dgr)], axis=1)
    dw_in = _mm(h, dproj, mode="tn", name=f"{tag}_dwin")
    dx, dg = _mm_rms_bwd([(dproj, w["ab_w_in"][0])], x, w["mix_norm"][0], dy, name=f"{tag}_dh")
    grads = dict(mix_norm=dg[0], ab_w_in=dw_in, ab_w_out=dw_out, lru_conv_w=dw8[:CONV_K], lru_conv_b=dw8[CONV_K],
                 lru_w_a=_block_diag_part(dwa, LRU_BLOCKS), lru_b_a=dba[0], lru_w_x=_block_diag_part(dwx, LRU_BLOCKS),
                 lru_b_x=dbx[0], lru_lambda=dlam[0])
    return dx, grads


def _dn_split_w(w_in):
    return w_in[:, :4 * DN_WIDTH], jnp.pad(w_in[:, 4 * DN_WIDTH:], ((0, 0), (0, LANES - DN_AB)))


def _mixer_dn_fwd(x, w, tag):
    w_qkvz, w_ab = _dn_split_w(w["dn_w_in"][0])
    h, (proj, ab) = _norm_mm(x, w["mix_norm"][1], [w_qkvz, w_ab], [F32, F32], name=f"{tag}_in")
    w8 = _conv_w8(w["dn_conv_w"][0])
    cqkv = _conv_fwd(proj, 0, 3 * DN_WIDTH, w8, name=f"{tag}_conv")
    vecs = (_row(w["dn_a_log"][0], LANES), _row(w["dn_dt_bias"][0], LANES), _row(w["dn_o_norm"][0]))
    og, states = _dn_fwd(cqkv, proj, ab, *vecs, name=f"{tag}_dn")
    x2 = _mm(og, w["dn_w_out"][0], res=x, name=f"{tag}_out")
    return x2, (x, h, w_qkvz, w_ab, proj, ab, w8, cqkv, vecs, og, states)


def _mixer_dn_bwd(saved, w, dy, tag):
    x, h, w_qkvz, w_ab, proj, ab, w8, cqkv, vecs, og, states = saved
    dout = _mm(dy, w["dn_w_out"][0], mode="nt", name=f"{tag}_dout")
    dw_out = _mm(og, dy, mode="tn", name=f"{tag}_dwout")
    dcqkv, dz, dab, dalog, ddtb, don = _dn_bwd(cqkv, proj, ab, *vecs, states, dout, name=f"{tag}_ddn")
    dqkv, dw8 = _conv_bwd(proj, 0, 3 * DN_WIDTH, w8, dcqkv, name=f"{tag}_dconv")
    dproj = jnp.concatenate([dqkv.astype(MXU_DTYPE), dz.astype(MXU_DTYPE)], axis=1)
    dw_in = jnp.concatenate([_mm(h, dproj, mode="tn", name=f"{tag}_dwin"),
                             _mm(h, dab, mode="tn", name=f"{tag}_dwin_ab")[:, :DN_AB]], axis=1)
    dx, dg = _mm_rms_bwd([(dproj, w_qkvz), (dab, w_ab)], x, w["mix_norm"][1], dy, name=f"{tag}_dh")
    grads = dict(mix_norm=dg[0], dn_w_in=dw_in, dn_w_out=dw_out, dn_conv_w=dw8[:CONV_K], dn_a_log=dalog[0, :DN_HEADS],
                 dn_dt_bias=ddtb[0, :DN_HEADS], dn_o_norm=don[0])
    return dx, grads


def _xa_layer_fwd(x, mem, w, layer, tag):
    hq, (q,) = _norm_mm(x, w["xa_norm"][layer], [_layer_matrix(w["xa_wq"], layer)], [MXU_DTYPE], name=f"{tag}_q")
    hm = _rms_fwd(mem, w["xa_mem_norm"][layer], name=f"{tag}_mem_norm")
    kv = _mm(hm, _layer_matrix(w["xa_wkv"], layer), name=f"{tag}_kv")
    oa = _xa_fwd(q, kv, name=f"{tag}_core")
    x2 = _mm(oa, _layer_matrix(w["xa_wo"], layer), res=x, name=f"{tag}_out")
    return x2, (x, hq, q, hm, kv, oa)


def _xa_layer_bwd(saved, mem, w, layer, dy, tag):
    x, hq, q, hm, kv, oa = saved
    dwo = _mm(oa, dy, mode="tn", name=f"{tag}_dwo")
    dq, dkv = _xa_bwd(q, kv, dy, _layer_matrix(w["xa_wo"], layer), name=f"{tag}_dcore")
    dwq = _mm(hq, dq, mode="tn", name=f"{tag}_dwq")
    dx, dg = _mm_rms_bwd([(dq, _layer_matrix(w["xa_wq"], layer))], x, w["xa_norm"][layer], dy, name=f"{tag}_dhq")
    dwkv = _mm(hm, dkv, mode="tn", name=f"{tag}_dwkv")
    dhm = _mm(dkv, _layer_matrix(w["xa_wkv"], layer), mode="nt", name=f"{tag}_dhm")
    _, dgm = _rms_bwd(mem, w["xa_mem_norm"][layer], dhm, jnp.zeros_like(mem), name=f"{tag}_dmem_norm")
    return dx, dict(xa_norm=dg[0], xa_mem_norm=dgm[0], xa_wq=dwq, xa_wkv=dwkv, xa_wo=dwo)


def _local_step(x, mem, target, w, pending=None, reduce_big=False):
    saved = []
    plans, shards = pending if pending else ([], None)
    hosts = {("ffn1", 0): plans[0], ("ffn2", 0): plans[1]} if plans else {}

    def ffn(which, layer, x):
        plan = hosts.get((which, layer))
        hosted = ([b for *_, b in plan], [a for _, _, a, _ in plan]) if plan else None
        x, s, gathered = _ffn_fwd(x, w[f"{which}_norm"], w[f"{which}_w_in"], w[f"{which}_w_out"], layer, f"l{layer}_{which}", gather=hosted)
        if plan:
            _gathered(plan, gathered, shards, w)
        return x, s

    for layer in range(DEPTH):
        t = f"l{layer}"
        x, s1 = ffn("ffn1", layer, x)
        x, s2 = (_mixer_ab_fwd if layer % 2 == 0 else _mixer_dn_fwd)(x, w, f"{t}_mix")
        x, s3 = _xa_layer_fwd(x, mem, w, layer, f"{t}_xa")
        x, s4 = ffn("ffn2", layer, x)
        saved.append((s1, s2, s3, s4))
    loss, dx, dgf = _final_loss(x, w["final_norm"], target, name="final_loss")
    per_layer, reduced = [None] * DEPTH, [None] * DEPTH
    travelling = None
    for layer in reversed(range(DEPTH)):
        t = f"l{layer}"
        s1, s2, s3, s4 = saved[layer]
        g = {}
        dx, g["ffn2_norm"], g["ffn2_w_in"], g["ffn2_w_out"], parts = _ffn_bwd(
            s4, w["ffn2_norm"], w["ffn2_w_in"], w["ffn2_w_out"], layer, dx, f"{t}_ffn2",
            exchange=travelling[1] if travelling else None)
        if travelling:
            reduced[travelling[0]] = _reduce_end(travelling[1], parts, f"l{travelling[0]}")
        dx, gx = _xa_layer_bwd(s3, mem, w, layer, dx, f"{t}_xa")
        dx, gm = (_mixer_ab_bwd if layer % 2 == 0 else _mixer_dn_bwd)(s2, w, dx, f"{t}_mix")
        dx, g["ffn1_norm"], g["ffn1_w_in"], g["ffn1_w_out"], _ = _ffn_bwd(
            s1, w["ffn1_norm"], w["ffn1_w_in"], w["ffn1_w_out"], layer, dx, f"{t}_ffn1")
        per_layer[layer] = {**g, **gx, **gm}
        if reduce_big:
            chip_sum = _reduce_begin(_pack_grads(per_layer[layer]), t)
            if layer > 0:
                travelling = (layer, chip_sum)
            else:
                reduced[layer] = _reduce_end(chip_sum, _exchange_chips(chip_sum), t)
    grads = {"final_norm": [dgf[0]]}
    for layer_grads in per_layer:
        for name, value in layer_grads.items():
            grads.setdefault(name, []).append(value)
    return loss, dx, grads, list(zip(reduced, per_layer))


N_CHIPS = 4
WIRE_DTYPE = jnp.bfloat16
HBM_SPEC = pl.BlockSpec(memory_space=pltpu.HBM)
PACK_COLS = 1024


def _place():
    x, y, c = lax.axis_index("x"), lax.axis_index("y"), lax.axis_index("c")
    return x, y, c, [(1 - x, y), (x, 1 - y), (1 - x, 1 - y)]


def _remote(src, dst, sems, k, to):
    return pltpu.make_async_remote_copy(src_ref=src, dst_ref=dst, send_sem=sems[0].at[k], recv_sem=sems[1].at[k],
                                        device_id=to, device_id_type=MESH)


def _gather_weights(blocks, axes):
    n = len(blocks)
    out_shapes, sem_shapes, start, finish = _gather_plan(blocks, axes)

    def body(*refs):
        start(refs[:n], refs[n:2 * n], *refs[2 * n:])
        finish(refs[:n], refs[n:2 * n], *refs[2 * n:])

    return pl.pallas_call(
        body, name="gather_weights", in_specs=[HBM_SPEC] * n, out_specs=[HBM_SPEC] * n,
        out_shape=out_shapes, scratch_shapes=sem_shapes,
    )(*blocks)


def _gather_plan(blocks, axes):
    n = len(blocks)
    split = [b.shape[1] % 32 == 0 for b in blocks]

    def full_shape(i):
        l, r, c = blocks[i].shape
        return (l, N_CHIPS * r, c) if axes[i] == 1 else (l, r, N_CHIPS * c)

    def copies(ins, outs, send_sems, recv_sems):
        x, y, c, chips = _place()
        sems = (send_sems, recv_sems)
        sibling = (x, y, 1 - c)
        me = 2 * x + y

        def window(i, k, h):
            l, r, cc = blocks[i].shape
            r0, nr = (0, r) if h is None else (h * (r // 2), r // 2)
            if axes[i] == 1:
                return outs[i].at[:, pl.ds(k * r + r0, nr), :]
            return outs[i].at[:, pl.ds(r0, nr), pl.ds(k * cc, cc)]

        def mine(i, h):
            r = blocks[i].shape[1]
            return ins[i] if h is None else ins[i].at[:, pl.ds(h * (r // 2), r // 2), :]

        half = lambda i: c if split[i] else None
        first = [_remote(mine(i, half(i)), window(i, me, half(i)), sems, 3 * i + j, (*chip, c))
                 for i in range(n) for j, chip in enumerate(chips)]
        first += [_remote(ins[i], window(i, me, None), sems, 6 * n + i, sibling) for i in range(n)]
        arrive = lambda i, j, h, k, frm: _remote(window(i, 2 * chips[j][0] + chips[j][1], h), window(i, 2 * chips[j][0] + chips[j][1], h),
                                                 sems, k, frm)
        return first, arrive, chips, c, sibling

    def start(ins, outs, send_sems, recv_sems):
        for cp in copies(ins, outs, send_sems, recv_sems)[0]:
            cp.start()

    def finish(ins, outs, send_sems, recv_sems):
        first, arrive, chips, c, sibling = copies(ins, outs, send_sems, recv_sems)
        passed = []
        for i in range(n):
            for j, (cx, cy) in enumerate(chips):
                arrive(i, j, c if split[i] else None, 3 * i + j, (cx, cy, c)).wait_recv()
                if split[i]:
                    passed.append(arrive(i, j, c, 3 * (n + i) + j, sibling))
                    passed[-1].start()
        for i in range(n):
            if split[i]:
                for j in range(len(chips)):
                    arrive(i, j, 1 - c, 3 * (n + i) + j, sibling).wait_recv()
        for cp in first[3 * n:]:
            cp.wait_recv()
        for cp in first + passed:
            cp.wait_send()

    sem_shapes = [pltpu.SemaphoreType.DMA((7 * n,)), pltpu.SemaphoreType.DMA((7 * n,))]
    return [jax.ShapeDtypeStruct(full_shape(i), blocks[i].dtype) for i in range(n)], sem_shapes, start, finish


def _allreduce_small(v):
    rows, cols = v.shape
    n_dev = 2 * N_CHIPS

    def body(v_ref, out_ref, all_ref, send_sems, recv_sems, local_sem):
        x, y, c, chips = _place()
        sems = (send_sems, recv_sems)
        me, sibling = (x, y, c), (x, y, 1 - c)
        slot = lambda px, py, pc: all_ref.at[pl.ds((4 * px + 2 * py + pc) * rows, rows), :]
        mine = pltpu.make_async_copy(v_ref, slot(*me), local_sem)
        mine.start()
        first = [_remote(v_ref, slot(*me), sems, 0, sibling)]
        first += [_remote(v_ref, slot(*me), sems, 1 + j, (*chip, c)) for j, chip in enumerate(chips)]
        for cp in first:
            cp.start()
        passed = [_remote(slot(*chip, c), slot(*chip, c), sems, 4 + j, sibling) for j, chip in enumerate(chips)]
        for j, chip in enumerate(chips):
            _remote(slot(*chip, c), slot(*chip, c), sems, 1 + j, me).wait_recv()
            passed[j].start()
        _remote(slot(*sibling), slot(*sibling), sems, 0, me).wait_recv()
        for j, chip in enumerate(chips):
            _remote(slot(*chip, 1 - c), slot(*chip, 1 - c), sems, 4 + j, me).wait_recv()
        for cp in first + passed:
            cp.wait_send()
        mine.wait()
        acc = all_ref[pl.ds(0, rows), :]
        for k in range(1, n_dev):
            acc = acc + all_ref[pl.ds(k * rows, rows), :]
        out_ref[...] = acc

    vmem = pl.BlockSpec(memory_space=pltpu.VMEM)
    return pl.pallas_call(
        body, name="allreduce_small", in_specs=[vmem], out_specs=vmem, out_shape=jax.ShapeDtypeStruct((rows, cols), F32),
        scratch_shapes=[pltpu.VMEM((n_dev * rows, cols), F32), pltpu.SemaphoreType.DMA((7,)), pltpu.SemaphoreType.DMA((7,)),
                        pltpu.SemaphoreType.DMA],
    )(v)


def _swap_other_half(g4, tag):
    n, _, rows, cols = g4.shape

    def body(v_ref, out_ref, send_sems, recv_sems):
        x, y, c, _ = _place()
        cp = _remote(v_ref.at[:, 1 - c], out_ref, (send_sems, recv_sems), 0, (x, y, 1 - c))
        cp.start()
        cp.wait()

    return pl.pallas_call(
        body, name=f"{tag}_reduce_swap", in_specs=[HBM_SPEC], out_specs=HBM_SPEC, out_shape=jax.ShapeDtypeStruct((n, rows, cols), g4.dtype),
        scratch_shapes=[pltpu.SemaphoreType.DMA((1,)), pltpu.SemaphoreType.DMA((1,))],
    )(g4)


def _add_kept_half(g4, got, tag):
    n, _, rows, cols = g4.shape
    tr = _tile(rows, (256, 128, 64, 32, 16))
    nb = rows // tr

    def body(c_ref, a_ref, b_ref, o_ref):
        o_ref[...] = (a_ref[...] + b_ref[...]).astype(o_ref.dtype)

    return pl.pallas_call(
        body, name=f"{tag}_reduce_sum_cores",
        grid_spec=pltpu.PrefetchScalarGridSpec(
            num_scalar_prefetch=1, grid=(n, nb),
            in_specs=[pl.BlockSpec((None, None, tr, cols), lambda k, i, c_ref: (k, c_ref[0], i, 0)),
                      pl.BlockSpec((None, tr, cols), lambda k, i, c_ref: (k, i, 0))],
            out_specs=pl.BlockSpec((None, tr, cols), lambda k, i, c_ref: (k, i, 0))),
        out_shape=jax.ShapeDtypeStruct((n, rows, cols), WIRE_DTYPE), compiler_params=_params("parallel", "parallel"),
    )(lax.axis_index("c").astype(jnp.int32).reshape(1), g4, got)


def _exchange_plan(v):
    def copies(v_ref, out_ref, send_sems, recv_sems):
        x, y, c, chips = _place()
        return [_remote(v_ref.at[2 * cx + cy], out_ref.at[j], (send_sems, recv_sems), j, (cx, cy, c)) for j, (cx, cy) in enumerate(chips)]

    def start(*refs):
        for cp in copies(*refs):
            cp.start()

    def finish(*refs):
        for cp in copies(*refs):
            cp.wait_recv()
        for cp in copies(*refs):
            cp.wait_send()

    sem_shapes = [pltpu.SemaphoreType.DMA((N_CHIPS - 1,)), pltpu.SemaphoreType.DMA((N_CHIPS - 1,))]
    return jax.ShapeDtypeStruct((N_CHIPS - 1,) + v.shape[1:], v.dtype), sem_shapes, start, finish


def _exchange_chips(v):
    out_shape, sem_shapes, start, finish = _exchange_plan(v)

    def body(*refs):
        start(*refs)
        finish(*refs)

    return pl.pallas_call(body, name="exchange_chips", in_specs=[HBM_SPEC], out_specs=HBM_SPEC, out_shape=out_shape,
                          scratch_shapes=sem_shapes)(v)


def _swap_sibling(v, tag):
    def body(v_ref, out_ref, send_sems, recv_sems):
        x, y, c, _ = _place()
        cp = _remote(v_ref, out_ref, (send_sems, recv_sems), 0, (x, y, 1 - c))
        cp.start()
        cp.wait()

    return pl.pallas_call(
        body, name=f"{tag}_share_halves", in_specs=[HBM_SPEC], out_specs=HBM_SPEC, out_shape=jax.ShapeDtypeStruct(v.shape, v.dtype),
        scratch_shapes=[pltpu.SemaphoreType.DMA((1,)), pltpu.SemaphoreType.DMA((1,))],
    )(v)


def _sum_chips(own4, parts, tag):
    _, rows, cols = own4.shape
    tr = _tile(rows, (256, 128, 64, 32, 16))

    def body(me_ref, own_ref, p0_ref, p1_ref, p2_ref, o_ref):
        acc = own_ref[...].astype(F32)
        for r in (p0_ref, p1_ref, p2_ref):
            acc = acc + r[...].astype(F32)
        o_ref[...] = acc

    part = lambda j: pl.BlockSpec((None, tr, cols), lambda i, me_ref: (j, i, 0))
    chip = (2 * lax.axis_index("x") + lax.axis_index("y")).astype(jnp.int32).reshape(1)
    return pl.pallas_call(
        body, name=f"{tag}_reduce_sum_chips",
        grid_spec=pltpu.PrefetchScalarGridSpec(
            num_scalar_prefetch=1, grid=(rows // tr,),
            in_specs=[pl.BlockSpec((None, tr, cols), lambda i, me_ref: (me_ref[0], i, 0)), part(0), part(1), part(2)],
            out_specs=pl.BlockSpec((tr, cols), lambda i, me_ref: (i, 0))),
        out_shape=jax.ShapeDtypeStruct((rows, cols), F32), compiler_params=_params("parallel"),
    )(chip, own4, parts, parts, parts)


def _reduce_begin(g4, tag):
    return _add_kept_half(g4, _swap_other_half(g4, tag), tag)


def _reduce_end(chip_sum, parts, tag):
    half = _sum_chips(chip_sum, parts, tag)
    other = _swap_sibling(half, tag)
    return jnp.where(lax.axis_index("c") == 0, jnp.stack([half, other]), jnp.stack([other, half]))


BIG = (("ffn1_w_in", 2), ("ffn1_w_out", 1), ("xa_wq", 1), ("xa_wkv", 2), ("xa_wo", 1), ("ffn2_w_in", 2), ("ffn2_w_out", 1),
       ("ab_w_in", 2), ("ab_w_out", 1), ("dn_w_in", 2), ("dn_w_out", 1))
TINY_SHARDED = (("lru_conv_w", 2), ("dn_conv_w", 2))
REPLICATED = ("ffn1_norm", "mix_norm", "xa_norm", "xa_mem_norm", "ffn2_norm", "lru_conv_b", "lru_w_a", "lru_b_a", "lru_w_x",
              "lru_b_x", "lru_lambda", "dn_a_log", "dn_dt_bias", "dn_o_norm", "final_norm")
WEIGHTS = ("ffn1_norm", "ffn1_w_in", "ffn1_w_out", "mix_norm", "xa_norm", "xa_mem_norm", "xa_wq", "xa_wkv", "xa_wo", "ffn2_norm",
           "ffn2_w_in", "ffn2_w_out", "ab_w_in", "lru_conv_w", "lru_conv_b", "lru_w_a", "lru_b_a", "lru_w_x", "lru_b_x",
           "lru_lambda", "ab_w_out", "dn_w_in", "dn_conv_w", "dn_a_log", "dn_dt_bias", "dn_o_norm", "dn_w_out", "final_norm")


def _lane_padded(shape):
    return shape[:-1] + (-(-shape[-1] // LANES) * LANES,)


def _pad_lanes(t):
    return jnp.pad(t, [(0, 0)] * (t.ndim - 1) + [(0, _lane_padded(t.shape)[-1] - t.shape[-1])])


FIRST_USED = ("ffn1_w_in", "ffn1_w_out")
LAYER_1_ONLY = ("dn_w_in", "dn_w_out", "dn_conv_w")


def _gather_blocks(shards):
    groups = [], [], []
    for n, a in BIG + TINY_SHARDED:
        block = _pad_lanes(shards[n]).astype(MXU_DTYPE) if (n, a) in BIG else shards[n]
        if block.shape[0] == 1:
            groups[2 if n in LAYER_1_ONLY else 1].append((n, None, a, block))
        else:
            for layer in range(block.shape[0]):
                group = 2 if layer > 0 else 0 if n in FIRST_USED else 1
                groups[group].append((n, layer, a, block[layer:layer + 1]))
    return groups


def _gathered(plan, arrays, shards, into):
    for (n, layer, axis, _), full in zip(plan, arrays):
        width, padded = shards[n].shape[-1], _lane_padded(shards[n].shape)[-1]
        if padded != width:
            assert axis == 2
            full = jnp.concatenate([full[..., k * padded:k * padded + width] for k in range(N_CHIPS)], axis=-1)
        if layer is None:
            into[n] = full
        else:
            into.setdefault(n, [None] * shards[n].shape[0])[layer] = full
    return into


def _pack_parts(cols):
    whole = cols // PACK_COLS * PACK_COLS
    return [(c0, PACK_COLS) for c0 in range(0, whole, PACK_COLS)] + ([(whole, cols - whole)] if cols > whole else [])


def _to_rows(block):
    block = _pad_lanes(block)
    return jnp.concatenate([block[:, c0:c0 + n].reshape(-1, PACK_COLS) for c0, n in _pack_parts(block.shape[1])], axis=0)


def _from_rows(rows, r, c):
    padded = _lane_padded((r, c))[1]
    parts, off = [], 0
    for _, n in _pack_parts(padded):
        size = r * n // PACK_COLS
        parts.append(rows[off:off + size].reshape(r, n))
        off += size
    return jnp.concatenate(parts, axis=1)[:, :c]


def _pack_rows(r, c):
    return r * _lane_padded((r, c))[1] // PACK_COLS


def _pack_grads(layer_grads):
    names = [(n, axis) for n, axis in BIG if n in layer_grads]
    used = sum(_pack_rows(layer_grads[n].shape[0] // (N_CHIPS if axis == 1 else 1),
                          layer_grads[n].shape[1] // (N_CHIPS if axis == 2 else 1)) for n, axis in names)
    rows = -(-used // 512) * 512

    def chip_block(k):
        blocks = []
        for n, axis in names:
            width = layer_grads[n].shape[axis - 1] // N_CHIPS
            blocks.append(_to_rows(lax.slice_in_dim(layer_grads[n], k * width, (k + 1) * width, axis=axis - 1)))
        if rows > used:
            blocks.append(jnp.zeros((rows - used, PACK_COLS), F32))
        return jnp.concatenate(blocks, axis=0)

    return jnp.stack([chip_block(k) for k in range(N_CHIPS)], axis=0).reshape(N_CHIPS, 2, rows // 2, PACK_COLS)


def _unpack_grads(reduced, layer_grads, shards):
    rows = reduced.reshape(-1, PACK_COLS)
    out, off = {}, 0
    for n, _ in BIG:
        if n in layer_grads:
            r, c = shards[n].shape[1:]
            out[n] = _from_rows(rows[off:off + _pack_rows(r, c)], r, c)[None]
            off += _pack_rows(r, c)
    return out


def _pack_small(grads, loss):
    parts = [p.reshape(-1) for n in REPLICATED + tuple(n for n, _ in TINY_SHARDED) for p in grads[n]] + [loss[0, :1]]
    flat = jnp.concatenate(parts)
    total = -(-flat.shape[0] // (SUBLANES * LANES)) * SUBLANES * LANES
    return jnp.pad(flat, (0, total - flat.shape[0])).reshape(-1, LANES)


def _unpack_small(summed, shards, chip):
    flat = summed.reshape(-1)
    out, off = {}, 0
    for n in REPLICATED:
        out[n] = flat[off:off + shards[n].size].reshape(shards[n].shape)
        off += shards[n].size
    for n, axis in TINY_SHARDED:
        width = shards[n].shape[axis]
        shape = shards[n].shape[:axis] + (N_CHIPS * width,) + shards[n].shape[axis + 1:]
        full = flat[off:off + N_CHIPS * shards[n].size].reshape(shape)
        out[n] = lax.dynamic_slice_in_dim(full, chip * width, width, axis=axis)
        off += N_CHIPS * shards[n].size
    return out, flat[off]


def kernel(x, mem, ffn1_norm, ffn1_w_in, ffn1_w_out, mix_norm, xa_norm, xa_mem_norm, xa_wq, xa_wkv, xa_wo, ffn2_norm,
           ffn2_w_in, ffn2_w_out, ab_w_in, lru_conv_w, lru_conv_b, lru_w_a, lru_b_a, lru_w_x, lru_b_x, lru_lambda,
           ab_w_out, dn_w_in, dn_conv_w, dn_a_log, dn_dt_bias, dn_o_norm, dn_w_out, final_norm, loss_target,
           m_ffn1_norm, m_ffn1_w_in, m_ffn1_w_out, m_mix_norm, m_xa_norm, m_xa_mem_norm, m_xa_wq, m_xa_wkv, m_xa_wo,
           m_ffn2_norm, m_ffn2_w_in, m_ffn2_w_out, m_ab_w_in, m_lru_conv_w, m_lru_conv_b, m_lru_w_a, m_lru_b_a,
           m_lru_w_x, m_lru_b_x, m_lru_lambda, m_ab_w_out, m_dn_w_in, m_dn_conv_w, m_dn_a_log, m_dn_dt_bias,
           m_dn_o_norm, m_dn_w_out, m_final_norm, v_ffn1_norm, v_ffn1_w_in, v_ffn1_w_out, v_mix_norm, v_xa_norm,
           v_xa_mem_norm, v_xa_wq, v_xa_wkv, v_xa_wo, v_ffn2_norm, v_ffn2_w_in, v_ffn2_w_out, v_ab_w_in,
           v_lru_conv_w, v_lru_conv_b, v_lru_w_a, v_lru_b_a, v_lru_w_x, v_lru_b_x, v_lru_lambda, v_ab_w_out,
           v_dn_w_in, v_dn_conv_w, v_dn_a_log, v_dn_dt_bias, v_dn_o_norm, v_dn_w_out, v_final_norm):
    given = dict(locals())
    shards = {n: given[n] for n in WEIGHTS}
    chip = 2 * lax.axis_index("x") + lax.axis_index("y")

    full = {n: shards[n] for n in REPLICATED}
    first, *later = _gather_blocks(shards)
    _gathered(first, _gather_weights([b for *_, b in first], [a for _, _, a, _ in first]), shards, full)
    loss, grad_x, grads, reduced = _local_step(x[0], mem[0], loss_target[0], full, pending=(later, shards), reduce_big=True)

    small, loss_sum = _unpack_small(_allreduce_small(_pack_small(grads, loss)), shards, chip)
    per_layer = [_unpack_grads(r, layer_grads, shards) for r, layer_grads in reduced]
    slabs = {n: [g] for n, g in small.items()}
    slabs.update({n: [p[n] for p in per_layer if n in p] for n, _ in BIG})

    grad, delta, new_m, new_v = {}, {}, {}, {}
    for n in WEIGHTS:
        delta[n], new_m[n], new_v[n], grad[n] = _adamw(shards[n], slabs[n], given["m_" + n], given["v_" + n], name=f"adamw_{n}")
    return (loss_sum, grad_x[None], *[grad[n] for n in WEIGHTS], *[delta[n] for n in WEIGHTS],
            *[new_m[n] for n in WEIGHTS], *[new_v[n] for n in WEIGHTS])
```

```python
import math

import jax
import jax.numpy as jnp
from jax import lax
from jax.experimental import pallas as pl
from jax.experimental.pallas import tpu as pltpu

F32 = jnp.float32
MXU_DTYPE = jnp.bfloat16
VMEM_LIMIT_BYTES = 48 * 1024 * 1024
MM_BLOCK_BYTES = 8 * 1024 * 1024
ADAMW_BLOCK_BYTES = 1024 * 1024
LANES = 128
SUBLANES = 8

NORM_EPS = 1e-6
CONV_K = 4
ATTN_PAIRS = 4
ATTN_HEAD_DIM = 64
ATTN_WIDTH = 512
ATTN_BLOCK = 128
DILATIONS = (1, 4, 16)
LRU_WIDTH = 512
LRU_BLOCKS = 8
LRU_C = 8.0
DN_HEADS = 8
DN_HEAD_DIM = 128
DN_WIDTH = 1024
DN_CHUNK = 64
XA_HEADS = 4
XA_HEAD_DIM = 256
D_FF = 2816
ADAM_LR, ADAM_B1, ADAM_B2, ADAM_EPS, ADAM_WD, ADAM_STEP = 0.001, 0.9, 0.999, 1e-08, 0.01, 10

MESH = pl.DeviceIdType.MESH


def _tile(n, prefs):
    for p in prefs:
        if n % p == 0:
            return p
    return n


def _params(*sem):
    return pltpu.CompilerParams(dimension_semantics=sem, vmem_limit_bytes=VMEM_LIMIT_BYTES)


def _dg(a, b, dims, hi=False):
    if hi:
        return lax.dot_general(a, b, (dims, ((), ())), precision=lax.Precision.HIGHEST, preferred_element_type=F32)
    return lax.dot_general(a.astype(MXU_DTYPE), b.astype(MXU_DTYPE), (dims, ((), ())), preferred_element_type=F32)


@jax.custom_vjp
def _bdot(a, b):
    return _dg(a, b, ((1,), (0,)))


def _bdot_fwd(a, b):
    return _bdot(a, b), (a, b)


def _bdot_bwd(r, g):
    a, b = r
    return _dg(g, b, ((1,), (1,))).astype(a.dtype), _dg(a, g, ((0,), (0,))).astype(b.dtype)


_bdot.defvjp(_bdot_fwd, _bdot_bwd)


def _log1p(t):
    return jnp.where(t < 0.01, t * (1.0 - t * (0.5 - t * (1.0 / 3.0))), jnp.log(1.0 + t))


def _neg_expm1(y):
    series = -y * (1.0 + 0.5 * y * (1.0 + (1.0 / 3.0) * y * (1.0 + 0.25 * y)))
    return jnp.where(y > -0.01, series, 1.0 - jnp.exp(y))


def _softplus(x):
    return jnp.maximum(x, 0.0) + _log1p(jnp.exp(-jnp.abs(x)))


def _sigmoid(x):
    return 0.5 * jnp.tanh(0.5 * x) + 0.5


def _silu(x):
    return x * _sigmoid(x)


def _gelu(x):
    return 0.5 * x * (1.0 + jnp.tanh(0.7978845608028654 * (x + 0.044715 * x * x * x)))


def _rows(shape):
    return lax.broadcasted_iota(jnp.int32, shape, 0)


def _cols(shape):
    return lax.broadcasted_iota(jnp.int32, shape, 1)


def _mm(a, b, *, mode="nn", out_dtype=F32, res=None, scale=1.0, name):
    if mode == "nn":
        (m, k), (k2, n) = a.shape, b.shape
    elif mode == "nt":
        (m, k), (n, k2) = a.shape, b.shape
    else:
        (k, m), (k2, n) = a.shape, b.shape
    assert k == k2, (a.shape, b.shape, mode)
    if mode == "tn":
        tm, tn, tk = _tile(m, (1024, 512, 256, 128)), _tile(n, (1024, 512, 256, 128)), _tile(k, (2048, 1024, 512, 256))
    else:
        tm, tn = _tile(m, (512, 256, 128)), _tile(n, (1024, 512, 256, 128))
        tk = k if k * tn * 2 <= MM_BLOCK_BYTES else _tile(k, (1024, 512, 256, 128))
    nk = k // tk
    dims = {"nn": ((1,), (0,)), "nt": ((1,), (1,)), "tn": ((0,), (0,))}[mode]

    def body(*refs):
        a_ref, b_ref = refs[:2]
        r_ref = refs[2] if res is not None else None
        o_ref = refs[3 if res is not None else 2]

        def finish(r):
            if scale != 1.0:
                r = r * scale
            if res is not None:
                r = r_ref[...] + r
            o_ref[...] = r.astype(out_dtype)

        if nk == 1:
            finish(_dg(a_ref[...], b_ref[...], dims))
            return
        acc = refs[-1]
        kk = pl.program_id(2)

        @pl.when(kk == 0)
        def _():
            acc[...] = jnp.zeros_like(acc)

        acc[...] += _dg(a_ref[...], b_ref[...], dims)

        @pl.when(kk == nk - 1)
        def _():
            finish(acc[...])

    a_spec = pl.BlockSpec((tk, tm), lambda i, j, kk: (kk, i)) if mode == "tn" else pl.BlockSpec((tm, tk), lambda i, j, kk: (i, kk))
    b_spec = pl.BlockSpec((tn, tk), lambda i, j, kk: (j, kk)) if mode == "nt" else pl.BlockSpec((tk, tn), lambda i, j, kk: (kk, j))
    o_spec = pl.BlockSpec((tm, tn), lambda i, j, kk: (i, j))
    in_specs = [a_spec, b_spec] + ([o_spec] if res is not None else [])
    args = (a, b) + ((res,) if res is not None else ())
    return pl.pallas_call(
        body, name=name, grid=(m // tm, n // tn, nk), in_specs=in_specs, out_specs=o_spec,
        out_shape=jax.ShapeDtypeStruct((m, n), out_dtype), scratch_shapes=[pltpu.VMEM((tm, tn), F32)] if nk > 1 else [],
        compiler_params=_params("parallel", "parallel", "arbitrary"),
    )(*args)


def _rms_fwd(x, g, *, name):
    s, d = x.shape
    tm = _tile(s, (512, 256))

    def body(x_ref, g_ref, o_ref):
        xv = x_ref[...]
        r = lax.rsqrt(jnp.mean(xv * xv, axis=-1, keepdims=True) + NORM_EPS)
        o_ref[...] = (xv * r * g_ref[...]).astype(o_ref.dtype)

    return pl.pallas_call(
        body, name=name, grid=(s // tm,),
        in_specs=[pl.BlockSpec((tm, d), lambda i: (i, 0)), pl.BlockSpec((1, d), lambda i: (0, 0))],
        out_specs=pl.BlockSpec((tm, d), lambda i: (i, 0)), out_shape=jax.ShapeDtypeStruct((s, d), MXU_DTYPE),
        compiler_params=_params("parallel"),
    )(x, g.reshape(1, d))


def _norm_mm(x, g, ws, out_dtypes, *, name):
    s, d = x.shape
    tm = _tile(s, (512, 256))
    nw = len(ws)

    def body(*refs):
        x_ref, g_ref = refs[:2]
        h_ref = refs[2 + nw]
        xv = x_ref[...]
        r = lax.rsqrt(jnp.mean(xv * xv, axis=-1, keepdims=True) + NORM_EPS)
        h = (xv * r * g_ref[...]).astype(MXU_DTYPE)
        h_ref[...] = h
        for w_ref, o_ref in zip(refs[2:2 + nw], refs[3 + nw:]):
            o_ref[...] = _dg(h, w_ref[...], ((1,), (0,))).astype(o_ref.dtype)

    row = lambda w: pl.BlockSpec((tm, w), lambda i: (i, 0))
    outs = pl.pallas_call(
        body, name=name, grid=(s // tm,),
        in_specs=[row(d), pl.BlockSpec((1, d), lambda i: (0, 0))]
        + [pl.BlockSpec(w.shape, lambda i: (0, 0), pipeline_mode=RESIDENT) for w in ws],
        out_specs=[row(d)] + [row(w.shape[1]) for w in ws],
        out_shape=[jax.ShapeDtypeStruct((s, d), MXU_DTYPE)] + [jax.ShapeDtypeStruct((s, w.shape[1]), t) for w, t in zip(ws, out_dtypes)],
        compiler_params=_params("parallel"),
    )(x, g.reshape(1, d), *ws)
    return outs[0], outs[1:]


def _mm_rms_bwd(pairs, x, g, dres, *, name):
    s, d = x.shape
    tm = _tile(s, (512, 256))
    n = len(pairs)

    def body(*refs):
        x_ref, g_ref, dr_ref = refs[2 * n:2 * n + 3]
        dx_ref, dg_ref = refs[2 * n + 3:]
        dh = _dg(refs[0][...], refs[n][...], ((1,), (1,)))
        for a_ref, w_ref in zip(refs[1:n], refs[n + 1:2 * n]):
            dh = dh + _dg(a_ref[...], w_ref[...], ((1,), (1,)))
        xv, gv = x_ref[...], g_ref[...]
        r = lax.rsqrt(jnp.mean(xv * xv, axis=-1, keepdims=True) + NORM_EPS)
        xh = xv * r
        dxh = dh * gv
        dx_ref[...] = dr_ref[...] + r * (dxh - xh * jnp.mean(dxh * xh, axis=-1, keepdims=True))

        @pl.when(pl.program_id(0) == 0)
        def _():
            dg_ref[...] = jnp.zeros_like(dg_ref)

        dg_ref[...] += jnp.sum(dh * xh, axis=0, keepdims=True)

    row = lambda w: pl.BlockSpec((tm, w), lambda i: (i, 0))
    vec = pl.BlockSpec((1, d), lambda i: (0, 0))
    return pl.pallas_call(
        body, name=name, grid=(s // tm,),
        in_specs=[row(a.shape[1]) for a, _ in pairs]
        + [pl.BlockSpec(w.shape, lambda i: (0, 0), pipeline_mode=RESIDENT) for _, w in pairs] + [row(d), vec, row(d)],
        out_specs=[row(d), vec], out_shape=[jax.ShapeDtypeStruct((s, d), F32), jax.ShapeDtypeStruct((1, d), F32)],
        compiler_params=_params("arbitrary"),
    )(*[a for a, _ in pairs], *[w for _, w in pairs], x, g.reshape(1, d), dres)


def _rms_bwd(x, g, dh, dres, *, name):
    s, d = x.shape
    tm = _tile(s, (512, 256))

    def body(x_ref, g_ref, dh_ref, dr_ref, dx_ref, dg_ref):
        xv = x_ref[...]
        r = lax.rsqrt(jnp.mean(xv * xv, axis=-1, keepdims=True) + NORM_EPS)
        xh = xv * r
        dhv = dh_ref[...].astype(F32)
        dxh = dhv * g_ref[...]
        dx = r * (dxh - xh * jnp.mean(dxh * xh, axis=-1, keepdims=True))
        dx_ref[...] = dr_ref[...] + dx

        @pl.when(pl.program_id(0) == 0)
        def _():
            dg_ref[...] = jnp.zeros_like(dg_ref)

        dg_ref[...] += jnp.sum(dhv * xh, axis=0, keepdims=True)

    row = pl.BlockSpec((tm, d), lambda i: (i, 0))
    vec = pl.BlockSpec((1, d), lambda i: (0, 0))
    return pl.pallas_call(
        body, name=name, grid=(s // tm,), in_specs=[row, vec, row, row], out_specs=[row, vec],
        out_shape=[jax.ShapeDtypeStruct((s, d), F32), jax.ShapeDtypeStruct((1, d), F32)],
        compiler_params=_params("arbitrary"),
    )(x, g.reshape(1, d), dh, dres)


FFN_CHUNK = 256
FFN_TM = 256
RESIDENT = pl.Buffered(1)


def _ffn_fwd_call(x, g, w_in, w_out, layer, *, name, gather=None):
    s, d = x.shape
    f = w_out.shape[1]
    tm = _tile(s, (2 * FFN_TM, FFN_TM))
    steps = s // tm
    n_g = len(gather[0]) if gather else 0
    g_shapes, g_sems, g_start, g_finish = _gather_plan(*gather) if gather else ([], [], None, None)

    def body(*refs):
        x_ref, g_ref, wi_ref, wo_ref = refs[:4]
        y_ref, u_ref = refs[4 + n_g:6 + n_g]
        act_ref = refs[6 + 2 * n_g]
        if gather:
            comm = (refs[4:4 + n_g], refs[6 + n_g:6 + 2 * n_g], *refs[7 + 2 * n_g:])
            pl.when(pl.program_id(0) == 0)(lambda: g_start(*comm))
        xv = x_ref[...]
        r = lax.rsqrt(jnp.mean(xv * xv, axis=-1, keepdims=True) + NORM_EPS)
        h = (xv * r * g_ref[...]).astype(MXU_DTYPE)
        for j in range(f // FFN_CHUNK):
            lo, hi = j * FFN_CHUNK, (j + 1) * FFN_CHUNK
            gate = _dg(h, wi_ref[:, lo:hi], ((1,), (0,))).astype(MXU_DTYPE)
            up = _dg(h, wi_ref[:, f + lo:f + hi], ((1,), (0,))).astype(MXU_DTYPE)
            u_ref[:, lo:hi] = gate
            u_ref[:, f + lo:f + hi] = up
            act_ref[:, lo:hi] = (_silu(gate.astype(F32)) * up.astype(F32)).astype(MXU_DTYPE)
        y_ref[...] = xv + 0.5 * _dg(act_ref[...], wo_ref[...], ((1,), (0,)))
        if gather:
            pl.when(pl.program_id(0) == steps - 1)(lambda: g_finish(*comm))

    row = lambda w: pl.BlockSpec((tm, w), lambda i: (i, 0))
    return pl.pallas_call(
        body, name=name, grid=(steps,),
        in_specs=[row(d), pl.BlockSpec((1, d), lambda i: (0, 0)),
                  pl.BlockSpec((None,) + w_in.shape[1:], lambda i: (layer, 0, 0), pipeline_mode=RESIDENT),
                  pl.BlockSpec((None,) + w_out.shape[1:], lambda i: (layer, 0, 0), pipeline_mode=RESIDENT)] + [HBM_SPEC] * n_g,
        out_specs=[row(d), row(2 * f)] + [HBM_SPEC] * n_g,
        out_shape=[jax.ShapeDtypeStruct((s, d), F32), jax.ShapeDtypeStruct((s, 2 * f), MXU_DTYPE)] + g_shapes,
        scratch_shapes=[pltpu.VMEM((tm, f), MXU_DTYPE)] + g_sems,
        compiler_params=_params("arbitrary" if gather else "parallel"),
    )(x, g.reshape(1, d), w_in, w_out, *(gather[0] if gather else ()))


def _ffn_bwd_call(x, g, u, dy, w_in, w_out, layer, *, name, exchange=None):
    s, d = x.shape
    f = w_out.shape[1]
    tm = _tile(s, (FFN_TM,))
    steps = s // tm
    hosted = exchange is not None
    e_shape, e_sems, e_start, e_finish = _exchange_plan(exchange) if hosted else (None, [], None, None)

    def body(*refs):
        x_ref, g_ref, u_ref, dy_ref, wi_ref, wo_ref = refs[:6]
        du_ref, dx_ref, dg_ref, h_ref = refs[6 + hosted:10 + hosted]
        if hosted:
            comm = (refs[6], *refs[10 + hosted:])
            pl.when(pl.program_id(0) == 0)(lambda: e_start(*comm))
        dyv = dy_ref[...]
        dyh = (0.5 * dyv).astype(MXU_DTYPE)
        for j in range(f // FFN_CHUNK):
            lo, hi = j * FFN_CHUNK, (j + 1) * FFN_CHUNK
            dact = _dg(dyh, wo_ref[lo:hi, :], ((1,), (1,)))
            gate, up = u_ref[:, lo:hi].astype(F32), u_ref[:, f + lo:f + hi].astype(F32)
            sg = _sigmoid(gate)
            du_ref[:, lo:hi] = (dact * up * sg * (1.0 + gate * (1.0 - sg))).astype(MXU_DTYPE)
            du_ref[:, f + lo:f + hi] = (dact * gate * sg).astype(MXU_DTYPE)
        dh = _dg(du_ref[...], wi_ref[...], ((1,), (1,)))
        xv, gv = x_ref[...], g_ref[...]
        r = lax.rsqrt(jnp.mean(xv * xv, axis=-1, keepdims=True) + NORM_EPS)
        xh = xv * r
        h_ref[...] = (xh * gv).astype(MXU_DTYPE)
        dxh = dh * gv
        dx_ref[...] = dyv + r * (dxh - xh * jnp.mean(dxh * xh, axis=-1, keepdims=True))

        @pl.when(pl.program_id(0) == 0)
        def _():
            dg_ref[...] = jnp.zeros_like(dg_ref)

        dg_ref[...] += jnp.sum(dh * xh, axis=0, keepdims=True)
        if hosted:
            pl.when(pl.program_id(0) == steps - 1)(lambda: e_finish(*comm))

    row = lambda w: pl.BlockSpec((tm, w), lambda i: (i, 0))
    vec = pl.BlockSpec((1, d), lambda i: (0, 0))
    return pl.pallas_call(
        body, name=name, grid=(steps,),
        in_specs=[row(d), vec, row(2 * f), row(d),
                  pl.BlockSpec((None,) + w_in.shape[1:], lambda i: (layer, 0, 0), pipeline_mode=RESIDENT),
                  pl.BlockSpec((None,) + w_out.shape[1:], lambda i: (layer, 0, 0), pipeline_mode=RESIDENT)] + [HBM_SPEC] * hosted,
        out_specs=[row(2 * f), row(d), vec, row(d)] + [HBM_SPEC] * hosted,
        out_shape=[jax.ShapeDtypeStruct((s, 2 * f), MXU_DTYPE), jax.ShapeDtypeStruct((s, d), F32),
                   jax.ShapeDtypeStruct((1, d), F32), jax.ShapeDtypeStruct((s, d), MXU_DTYPE)] + [e_shape] * hosted,
        scratch_shapes=e_sems, compiler_params=_params("arbitrary"),
    )(x, g.reshape(1, d), u, dy, w_in, w_out, *([exchange] if hosted else []))


def _ffn_dw_out(u, dy, *, name):
    s, f2 = u.shape
    f, d = f2 // 2, dy.shape[1]
    tf, tk = _tile(f, (1408, 256, 128)), _tile(s, (1024, 512, 256))
    nj = f // tf

    def body(g_ref, u_ref, dy_ref, o_ref):
        @pl.when(pl.program_id(1) == 0)
        def _():
            o_ref[...] = jnp.zeros_like(o_ref)

        act = _silu(g_ref[...].astype(F32)) * u_ref[...].astype(F32)
        o_ref[...] += _dg(act, 0.5 * dy_ref[...], ((0,), (0,)))

    return pl.pallas_call(
        body, name=name, grid=(nj, s // tk),
        in_specs=[pl.BlockSpec((tk, tf), lambda j, k: (k, j)), pl.BlockSpec((tk, tf), lambda j, k: (k, j + nj)),
                  pl.BlockSpec((tk, d), lambda j, k: (k, 0))],
        out_specs=pl.BlockSpec((tf, d), lambda j, k: (j, 0)), out_shape=jax.ShapeDtypeStruct((f, d), F32),
        compiler_params=_params("parallel", "arbitrary"),
    )(u, u, dy)


def _of_layer(w, layer):
    return (w[layer], 0) if isinstance(w, (list, tuple)) else (w, layer)


def _layer_matrix(w, layer):
    w, at = _of_layer(w, layer)
    return w[at]


def _ffn_fwd(x, g, w_in, w_out, layer, tag, gather=None):
    (w_in, at), (w_out, _) = _of_layer(w_in, layer), _of_layer(w_out, layer)
    y, u, *gathered = _ffn_fwd_call(x, g[layer], w_in, w_out, at, name=f"{tag}_fwd", gather=gather)
    return y, (x, u), gathered


def _ffn_bwd(saved, g, w_in, w_out, layer, dy, tag, exchange=None):
    x, u = saved
    (w_in, at), (w_out, _) = _of_layer(w_in, layer), _of_layer(w_out, layer)
    du, dx, dg, h, *parts = _ffn_bwd_call(x, g[layer], u, dy, w_in, w_out, at, name=f"{tag}_bwd", exchange=exchange)
    dw_out = _ffn_dw_out(u, dy, name=f"{tag}_dwout")
    dw_in = _mm(h, du, mode="tn", name=f"{tag}_dwin")
    return dx, dg[0], dw_in, dw_out, (parts[0] if parts else None)


ATTN_SCALE = ATTN_HEAD_DIM ** -0.5
NEG_BIG = -1e30
PROJ_AB_BLOCKS = 5


def _first_head(n):
    return _cols((n, LANES)) < ATTN_HEAD_DIM


def _per_head(tiles):
    first = _first_head(tiles[0].shape[0])
    return jnp.stack([jnp.where(first == (h == 0), t, 0.0) for t in tiles for h in (0, 1)], axis=0)


def _both(tiles):
    return jnp.stack([t for t in tiles for _ in (0, 1)], axis=0)


def _head_cols(tiles):
    return jnp.stack([t[:, c0:c0 + 1] for t in tiles for c0 in (0, ATTN_HEAD_DIM)], axis=0)


def _join_heads(v):
    return [v[2 * u] + v[2 * u + 1] for u in range(v.shape[0] // 2)]


def _spread_heads(v):
    first = _first_head(v.shape[1])
    return [jnp.where(first, v[2 * u], v[2 * u + 1]) for u in range(v.shape[0] // 2)]


def _band_masks(has_prev):
    qi, kj = _rows((ATTN_BLOCK, ATTN_BLOCK)), _cols((ATTN_BLOCK, ATTN_BLOCK))
    return (kj >= qi) & has_prev, kj <= qi


def _dattn_delta(o, dcat, *, name):
    s_len = o.shape[0]
    tm = _tile(s_len, (512, 256))

    def body(o_ref, do_ref, out_ref):
        r, c = _rows((ATTN_WIDTH, ATTN_WIDTH)), _cols((ATTN_WIDTH, ATTN_WIDTH))
        ones_bd = (r // ATTN_HEAD_DIM == c // ATTN_HEAD_DIM).astype(F32)
        out_ref[...] = _dg(o_ref[...] * do_ref[...], ones_bd, ((1,), (0,)), hi=True)

    blk = pl.BlockSpec((tm, ATTN_WIDTH), lambda i: (i, 0))
    return pl.pallas_call(
        body, name=name, grid=(s_len // tm,), in_specs=[blk, blk], out_specs=blk,
        out_shape=jax.ShapeDtypeStruct((s_len, ATTN_WIDTH), F32), compiler_params=_params("parallel"),
    )(o, dcat)


ATTN_UNITS = 4


def _units(it, d):
    if d == 1:
        return [(pl.ds(0, ATTN_BLOCK), pl.ds(p * LANES, LANES)) for p in range(ATTN_UNITS)]
    return [(pl.ds(it * ATTN_UNITS + u, ATTN_BLOCK, stride=d), pl.ds(0, LANES)) for u in range(ATTN_UNITS)]


def _tiles(ref, units):
    return [ref[rows, lanes] for rows, lanes in units]


def _store_tiles(ref, units, tiles):
    for (rows, lanes), t in zip(units, tiles):
        ref[rows, lanes] = t


def _stacked(a_tiles, b_tiles):
    return [jnp.concatenate([a, b], axis=0) for a, b in zip(a_tiles, b_tiles)]


def _passes(d):
    return max(d // ATTN_UNITS, 1)


def _pairs_per_step(d):
    return ATTN_PAIRS if d == 1 else 1


def _pair_specs(d, n_of):
    pairs = _pairs_per_step(d)
    groups = ATTN_PAIRS // pairs
    return lambda c: pl.BlockSpec((ATTN_BLOCK * d, LANES * pairs), lambda n, p: (n_of(n), c * groups + p))


def _sattn_fwd(proj, state, d, *, last, name):
    s_len = proj.shape[0]
    nb = s_len // (ATTN_BLOCK * d)
    first = state is None
    n_out = 2 if last else 3

    def body(*refs):
        q_ref, kp_ref, kc_ref, vp_ref, vc_ref = refs[:5]
        st_refs = () if first else refs[5:8]
        out_refs = refs[-n_out:]
        ok = jnp.concatenate(_band_masks(pl.program_id(0) > 0), axis=1)

        def one_pass(it, carry):
            units = _units(it, d)
            kcat = _stacked(_tiles(kp_ref, units), _tiles(kc_ref, units))
            vcat = _stacked(_tiles(vp_ref, units), _tiles(vc_ref, units))
            s = jnp.where(ok, _bdg(_per_head(_tiles(q_ref, units)), _both(kcat), 2, 2) * ATTN_SCALE, NEG_BIG)
            m_new = jnp.max(s, axis=2, keepdims=True)
            if not first:
                m_old = _head_cols(_tiles(st_refs[0], units))
                m_new = jnp.maximum(m_old, m_new)
                alpha = jnp.exp(m_old - m_new)
            p = jnp.exp(s - m_new)
            l_new = jnp.sum(p, axis=2, keepdims=True)
            acc = _join_heads(_bdg(p, _per_head(vcat), 2, 1))
            if not first:
                l_new = l_new + _head_cols(_tiles(st_refs[1], units)) * alpha
                acc = [a + a_in * sp for a, a_in, sp in zip(acc, _tiles(st_refs[2], units), _spread_heads(alpha))]
            m_pair, l_pair = _spread_heads(m_new), _spread_heads(l_new)
            if last:
                _store_tiles(out_refs[0], units, [a / l for a, l in zip(acc, l_pair)])
                _store_tiles(out_refs[1], units, [m + jnp.log(l) for m, l in zip(m_pair, l_pair)])
            else:
                _store_tiles(out_refs[0], units, m_pair)
                _store_tiles(out_refs[1], units, l_pair)
                _store_tiles(out_refs[2], units, acc)
            return carry

        lax.fori_loop(0, _passes(d), one_pass, 0)

    cur, prev = _pair_specs(d, lambda n: n), _pair_specs(d, lambda n: jnp.maximum(n - 1, 0))
    st = cur(0)
    return tuple(pl.pallas_call(
        body, name=name, grid=(nb, ATTN_PAIRS // _pairs_per_step(d)),
        in_specs=[cur(0), prev(1), cur(1), prev(2), cur(2)] + ([] if first else [st] * 3),
        out_specs=[st] * n_out, out_shape=[jax.ShapeDtypeStruct((s_len, ATTN_WIDTH), F32)] * n_out,
        compiler_params=_params("arbitrary", "parallel"),
    )(*([proj] * 5 + ([] if first else list(state)))))


def _dattn_forward(proj, tag):
    state = None
    for i, d in enumerate(DILATIONS):
        state = _sattn_fwd(proj, state, d, last=i == len(DILATIONS) - 1, name=f"{tag}_attn_d{d}")
    return state


def _sattn_bwd(proj, dcat, lse, delta, grads_in, d, *, name):
    s_len = proj.shape[0]
    nb = s_len // (ATTN_BLOCK * d)
    first = grads_in is None
    groups = ATTN_PAIRS // _pairs_per_step(d)

    def body(*refs):
        q_ref, kp_ref, kc_ref, vp_ref, vc_ref, do_ref, lse_ref, dl_ref = refs[:8]
        dq_in, dk_in, dv_in = (None, None, None) if first else refs[8:11]
        dq_ref, dk_ref, dv_ref, carry_k, carry_v = refs[-5:]
        n = pl.program_id(1)
        ok = jnp.concatenate(_band_masks(n > 0), axis=1)

        @pl.when(n == 0)
        def _():
            carry_k[...] = jnp.zeros_like(carry_k)
            carry_v[...] = jnp.zeros_like(carry_v)

        def leave(ref, carry, units, extra, into):
            out = [c + e for c, e in zip(_tiles(carry, units), extra)] if extra else _tiles(carry, units)
            if into is not None:
                out = [a + b for a, b in zip(out, _tiles(into, units))]
            _store_tiles(ref, units, out)

        def one_pass(it, carry):
            units = _units(it, d)
            kcat = _stacked(_tiles(kp_ref, units), _tiles(kc_ref, units))
            vcat = _stacked(_tiles(vp_ref, units), _tiles(vc_ref, units))
            q2, do2 = _per_head(_tiles(q_ref, units)), _per_head(_tiles(do_ref, units))
            s = _bdg(q2, _both(kcat), 2, 2) * ATTN_SCALE
            pr = jnp.where(ok, jnp.exp(jnp.where(ok, s, NEG_BIG) - _head_cols(_tiles(lse_ref, units))), 0.0)
            ds = pr * (_bdg(do2, _both(vcat), 2, 2) - _head_cols(_tiles(dl_ref, units)))
            dq = [t * ATTN_SCALE for t in _join_heads(_bdg(ds, _per_head(kcat), 2, 1))]
            if not first:
                dq = [a + b for a, b in zip(dq, _tiles(dq_in, units))]
            _store_tiles(dq_ref, units, dq)
            dk = [t * ATTN_SCALE for t in _join_heads(_bdg(ds, q2, 1, 1))]
            dv = _join_heads(_bdg(pr, do2, 1, 1))
            leave(dk_ref, carry_k, units, [t[:ATTN_BLOCK] for t in dk], dk_in)
            leave(dv_ref, carry_v, units, [t[:ATTN_BLOCK] for t in dv], dv_in)
            _store_tiles(carry_k, units, [t[ATTN_BLOCK:] for t in dk])
            _store_tiles(carry_v, units, [t[ATTN_BLOCK:] for t in dv])
            return carry

        def last_pass(it, carry):
            units = _units(it, d)
            leave(dk_ref, carry_k, units, None, dk_in)
            leave(dv_ref, carry_v, units, None, dv_in)
            return carry

        @pl.when(n < nb)
        def _():
            lax.fori_loop(0, _passes(d), one_pass, 0)

        @pl.when(n == nb)
        def _():
            lax.fori_loop(0, _passes(d), last_pass, 0)

    pairs = _pairs_per_step(d)
    blk = (ATTN_BLOCK * d, LANES * pairs)
    at = lambda n_of: (lambda c: pl.BlockSpec(blk, lambda p, n: (n_of(n), c * groups + p)))
    here = lambda n: jnp.minimum(n, nb - 1)
    cur, prev, lag = at(here), at(lambda n: jnp.maximum(here(n) - 1, 0)), at(lambda n: jnp.maximum(n - 1, 0))
    st, st_lag = cur(0), lag(0)
    return tuple(pl.pallas_call(
        body, name=name, grid=(groups, nb + 1),
        in_specs=[cur(0), prev(1), cur(1), prev(2), cur(2), st, st, st] + ([] if first else [st, st_lag, st_lag]),
        out_specs=[st, st_lag, st_lag], out_shape=[jax.ShapeDtypeStruct((s_len, ATTN_WIDTH), F32)] * 3,
        scratch_shapes=[pltpu.VMEM(blk, F32)] * 2, compiler_params=_params("parallel", "arbitrary"),
    )(*([proj] * 5 + [dcat, lse, delta] + ([] if first else list(grads_in)))))


def _dattn_backward(proj, o, lse, dcat, tag):
    delta = _dattn_delta(o, dcat, name=f"{tag}_attn_delta")
    grads = None
    for d in DILATIONS:
        grads = _sattn_bwd(proj, dcat, lse, delta, grads, d, name=f"{tag}_attn_bwd_d{d}")
    return grads


CONV_TC = 512
CONV_T = 512


def _conv_tiles(s_len, cb0, width):
    wide = 2 * CONV_TC
    tc = wide if width % wide == 0 and (cb0 * CONV_TC) % wide == 0 else CONV_TC
    return _tile(s_len, (CONV_T, CONV_T // 2)), tc, cb0 * CONV_TC // tc


def _shift_down(ext, k, t):
    return (pltpu.roll(ext, k, 0) if k else ext)[SUBLANES:SUBLANES + t]


def _conv_fwd(src, cb0, width, w8, *, name):
    s_len = src.shape[0]
    t, tc, cb = _conv_tiles(s_len, cb0, width)
    tpb = t // SUBLANES

    def body(x_ref, h_ref, w_ref, y_ref):
        halo = jnp.where(pl.program_id(0) > 0, h_ref[...], 0.0)
        ext = jnp.concatenate([halo, x_ref[...]], axis=0)
        w = w_ref[...]
        y = jnp.broadcast_to(w[CONV_K:CONV_K + 1], (t, tc))
        for k in range(CONV_K):
            y = y + w[k:k + 1] * _shift_down(ext, CONV_K - 1 - k, t)
        y_ref[...] = y

    return pl.pallas_call(
        body, name=name, grid=(s_len // t, width // tc),
        in_specs=[pl.BlockSpec((t, tc), lambda i, j: (i, cb + j)),
                  pl.BlockSpec((SUBLANES, tc), lambda i, j: (jnp.maximum(i * tpb - 1, 0), cb + j)),
                  pl.BlockSpec((SUBLANES, tc), lambda i, j: (0, j))],
        out_specs=pl.BlockSpec((t, tc), lambda i, j: (i, j)), out_shape=jax.ShapeDtypeStruct((s_len, width), F32),
        compiler_params=_params("parallel", "parallel"),
    )(src, src, w8)


def _conv_bwd(src, cb0, width, w8, dy, *, name):
    s_len = src.shape[0]
    t, tc, cb = _conv_tiles(s_len, cb0, width)
    tpb = t // SUBLANES
    ni = s_len // t

    def body(x_ref, h_ref, w_ref, dy_ref, dn_ref, dx_ref, dw_ref):
        i = pl.program_id(1)
        halo = jnp.where(i > 0, h_ref[...], 0.0)
        ext = jnp.concatenate([halo, x_ref[...]], axis=0)
        dyv = dy_ref[...]
        extn = jnp.concatenate([dyv, jnp.where(i < ni - 1, dn_ref[...], 0.0)], axis=0)
        w = w_ref[...]
        row = _rows((SUBLANES, tc))
        dx = jnp.zeros((t, tc), F32)
        dw = jnp.where(row == CONV_K, jnp.sum(dyv, axis=0, keepdims=True), 0.0)
        for k in range(CONV_K):
            up = CONV_K - 1 - k
            dx = dx + w[k:k + 1] * (pltpu.roll(extn, t + SUBLANES - up, 0) if up else extn)[:t]
            dw = dw + jnp.where(row == k, jnp.sum(dyv * _shift_down(ext, up, t), axis=0, keepdims=True), 0.0)
        dx_ref[...] = dx.astype(dx_ref.dtype)

        @pl.when(i == 0)
        def _():
            dw_ref[...] = jnp.zeros_like(dw_ref)

        dw_ref[...] += dw

    return pl.pallas_call(
        body, name=name, grid=(width // tc, ni),
        in_specs=[pl.BlockSpec((t, tc), lambda j, i: (i, cb + j)),
                  pl.BlockSpec((SUBLANES, tc), lambda j, i: (jnp.maximum(i * tpb - 1, 0), cb + j)),
                  pl.BlockSpec((SUBLANES, tc), lambda j, i: (0, j)),
                  pl.BlockSpec((t, tc), lambda j, i: (i, j)),
                  pl.BlockSpec((SUBLANES, tc), lambda j, i: (jnp.minimum((i + 1) * tpb, s_len // SUBLANES - 1), j))],
        out_specs=[pl.BlockSpec((t, tc), lambda j, i: (i, j)), pl.BlockSpec((SUBLANES, tc), lambda j, i: (0, j))],
        out_shape=[jax.ShapeDtypeStruct((s_len, width), MXU_DTYPE), jax.ShapeDtypeStruct((SUBLANES, width), F32)],
        compiler_params=_params("parallel", "arbitrary"),
    )(src, src, w8, dy, dy)


LRU_T = 256


def _lru_gates(xc, wa, wx, ba, bx, lam):
    r = _sigmoid(_bdot(xc, wa) + ba)
    i = _sigmoid(_bdot(xc, wx) + bx)
    log_a = (-LRU_C) * r * _softplus(-lam)
    return jnp.exp(log_a), jnp.sqrt(_neg_expm1(2.0 * log_a)) * i * xc


def _block_scan(a, b, state, reverse):
    t = a.shape[0]
    row = _rows(a.shape) % SUBLANES
    s = 1
    while s < SUBLANES:
        shift, ok = (t - s, row < SUBLANES - s) if reverse else (s, row >= s)
        b = jnp.where(ok, a * pltpu.roll(b, shift, 0) + b, b)
        a = jnp.where(ok, a * pltpu.roll(a, shift, 0), a)
        s *= 2
    groups = range(t // SUBLANES)
    out = [None] * len(groups)
    for g in (reversed(groups) if reverse else groups):
        rows = slice(g * SUBLANES, (g + 1) * SUBLANES)
        out[g] = b[rows] + a[rows] * state
        state = out[g][0:1] if reverse else out[g][SUBLANES - 1:SUBLANES]
    return jnp.concatenate(out, axis=0)


def _lru_fwd(xc, proj, wa, wx, ba, bx, lam, *, name):
    s_len, w = xc.shape
    t = _tile(s_len, (LRU_T,))

    def body(xc_ref, gr_ref, wa_ref, wx_ref, ba_ref, bx_ref, lam_ref, h_ref, y_ref, carry):
        @pl.when(pl.program_id(0) == 0)
        def _():
            carry[...] = jnp.zeros_like(carry)

        a, b = _lru_gates(xc_ref[...], wa_ref[...], wx_ref[...], ba_ref[...], bx_ref[...], lam_ref[...])
        h = _block_scan(a, b, carry[0:1, :], False)
        h_ref[...] = h
        y_ref[...] = (h * _gelu(gr_ref[...])).astype(y_ref.dtype)
        carry[0:1, :] = h[t - 1:t, :]

    row = pl.BlockSpec((t, w), lambda i: (i, 0))
    mat = pl.BlockSpec((w, w), lambda i: (0, 0))
    vec = pl.BlockSpec((1, w), lambda i: (0, 0))
    return pl.pallas_call(
        body, name=name, grid=(s_len // t,),
        in_specs=[row, pl.BlockSpec((t, w), lambda i: (i, PROJ_AB_BLOCKS - 1)), mat, mat, vec, vec, vec],
        out_specs=[row, row], out_shape=[jax.ShapeDtypeStruct((s_len, w), F32), jax.ShapeDtypeStruct((s_len, w), MXU_DTYPE)],
        scratch_shapes=[pltpu.VMEM((SUBLANES, w), F32)], compiler_params=_params("arbitrary"),
    )(xc, proj, wa, wx, ba, bx, lam)


def _lru_bwd(xc, proj, hs, dcat, wa, wx, ba, bx, lam, *, name):
    s_len, w = xc.shape
    t = _tile(s_len, (LRU_T,))
    nb = s_len // t
    tpb = t // SUBLANES

    def body(xc_ref, gr_ref, h_ref, hp_ref, dy_ref, wa_ref, wx_ref, ba_ref, bx_ref, lam_ref,
             dxc_ref, dgr_ref, dwa_ref, dwx_ref, dba_ref, dbx_ref, dlam_ref, carry):
        step = pl.program_id(0)
        params = (wa_ref[...], wx_ref[...], ba_ref[...], bx_ref[...], lam_ref[...])

        @pl.when(step == 0)
        def _():
            carry[...] = jnp.zeros_like(carry)
            for r in (dwa_ref, dwx_ref, dba_ref, dbx_ref, dlam_ref):
                r[...] = jnp.zeros_like(r)

        (a, _), vjp = jax.vjp(_lru_gates, xc_ref[...], *params)
        gr, h, dy = gr_ref[...], h_ref[...], dy_ref[...]
        gel, gel_vjp = jax.vjp(_gelu, gr)
        dgr_ref[...] = gel_vjp(dy * h)[0].astype(dgr_ref.dtype)
        dh = dy * gel
        big_g = _block_scan(a, a * dh, carry[0:1, :], True)
        row = _rows((t, w))
        g = dh + jnp.where(row == t - 1, carry[0:1, :], pltpu.roll(big_g, t - 1, 0))
        carry[0:1, :] = big_g[0:1, :]
        h_last = jnp.where(step < nb - 1, hp_ref[SUBLANES - 1:SUBLANES, :], 0.0)
        h_prev = jnp.where(row == 0, h_last, pltpu.roll(h, 1, 0))
        dxc, dwa, dwx, dba, dbx, dlam = vjp((g * h_prev, g))
        dxc_ref[...] = dxc
        dwa_ref[...] += dwa
        dwx_ref[...] += dwx
        dba_ref[...] += dba
        dbx_ref[...] += dbx
        dlam_ref[...] += dlam

    rev = lambda i: nb - 1 - i
    row = pl.BlockSpec((t, w), lambda i: (rev(i), 0))
    mat = pl.BlockSpec((w, w), lambda i: (0, 0))
    vec = pl.BlockSpec((1, w), lambda i: (0, 0))
    return pl.pallas_call(
        body, name=name, grid=(nb,),
        in_specs=[row, pl.BlockSpec((t, w), lambda i: (rev(i), PROJ_AB_BLOCKS - 1)), row,
                  pl.BlockSpec((SUBLANES, w), lambda i: (jnp.maximum(rev(i) * tpb - 1, 0), 0)),
                  pl.BlockSpec((t, w), lambda i: (rev(i), 1)), mat, mat, vec, vec, vec],
        out_specs=[row, row, mat, mat, vec, vec, vec],
        out_shape=[jax.ShapeDtypeStruct((s_len, w), F32), jax.ShapeDtypeStruct((s_len, w), MXU_DTYPE)]
        + [jax.ShapeDtypeStruct((w, w), F32)] * 2 + [jax.ShapeDtypeStruct((1, w), F32)] * 3,
        scratch_shapes=[pltpu.VMEM((SUBLANES, w), F32)], compiler_params=_params("arbitrary"),
    )(xc, proj, hs, hs, dcat, wa, wx, ba, bx, lam)


XA_T = 256
XA_SCALE = XA_HEAD_DIM ** -0.5


def _xa_heads(q, k, v):
    s = _bmm_nt(q, k) * XA_SCALE
    e = jnp.exp(s - jnp.max(s, axis=-1, keepdims=True))
    return _bmm(e / jnp.sum(e, axis=-1, keepdims=True), v)


def _xa_stack(ref):
    return jnp.stack([ref[:, h * XA_HEAD_DIM:(h + 1) * XA_HEAD_DIM].astype(F32) for h in range(XA_HEADS)], axis=0)


def _xa_fwd(q, kv, x, wo, *, name):
    s_len, d = q.shape
    n_mem = kv.shape[0]
    t = _tile(s_len, (XA_T,))

    def body(q_ref, k_ref, v_ref, x_ref, wo_ref, o_ref, y_ref):
        o = _xa_heads(_xa_stack(q_ref), _xa_stack(k_ref), _xa_stack(v_ref))
        for h in range(XA_HEADS):
            o_ref[:, h * XA_HEAD_DIM:(h + 1) * XA_HEAD_DIM] = o[h].astype(o_ref.dtype)
        y_ref[...] = x_ref[...] + _dg(o_ref[...], wo_ref[...], ((1,), (0,)))

    row = pl.BlockSpec((t, d), lambda i: (i, 0))
    return pl.pallas_call(
        body, name=name, grid=(s_len // t,),
        in_specs=[row, pl.BlockSpec((n_mem, d), lambda i: (0, 0)), pl.BlockSpec((n_mem, d), lambda i: (0, 1)), row,
                  pl.BlockSpec(wo.shape, lambda i: (0, 0), pipeline_mode=RESIDENT)],
        out_specs=[row, row], out_shape=[jax.ShapeDtypeStruct((s_len, d), MXU_DTYPE), jax.ShapeDtypeStruct((s_len, d), F32)],
        compiler_params=_params("parallel"),
    )(q, kv, kv, x, wo)


def _xa_bwd(q, kv, dy, wo, *, name):
    s_len, d = q.shape
    n_mem = kv.shape[0]
    t = _tile(s_len, (XA_T,))

    def body(q_ref, k_ref, v_ref, dy_ref, wo_ref, dq_ref, dk_ref, dv_ref):
        @pl.when(pl.program_id(0) == 0)
        def _():
            dk_ref[...] = jnp.zeros_like(dk_ref)
            dv_ref[...] = jnp.zeros_like(dv_ref)

        do = _dg(dy_ref[...], wo_ref[...], ((1,), (1,)))
        do = jnp.stack([do[:, h * XA_HEAD_DIM:(h + 1) * XA_HEAD_DIM] for h in range(XA_HEADS)], axis=0)
        _, vjp = jax.vjp(_xa_heads, _xa_stack(q_ref), _xa_stack(k_ref), _xa_stack(v_ref))
        dq, dk, dv = vjp(do)
        for h in range(XA_HEADS):
            sl = slice(h * XA_HEAD_DIM, (h + 1) * XA_HEAD_DIM)
            dq_ref[:, sl] = dq[h].astype(dq_ref.dtype)
            dk_ref[:, sl] += dk[h]
            dv_ref[:, sl] += dv[h]

    row = pl.BlockSpec((t, d), lambda i: (i, 0))
    dq, dk, dv = pl.pallas_call(
        body, name=name, grid=(s_len // t,),
        in_specs=[row, pl.BlockSpec((n_mem, d), lambda i: (0, 0)), pl.BlockSpec((n_mem, d), lambda i: (0, 1)), row,
                  pl.BlockSpec(wo.shape, lambda i: (0, 0), pipeline_mode=RESIDENT)],
        out_specs=[row, pl.BlockSpec((n_mem, d), lambda i: (0, 0)), pl.BlockSpec((n_mem, d), lambda i: (0, 0))],
        out_shape=[jax.ShapeDtypeStruct((s_len, d), MXU_DTYPE)] + [jax.ShapeDtypeStruct((n_mem, d), F32)] * 2,
        compiler_params=_params("arbitrary"),
    )(q, kv, kv, dy, wo)
    return dq, jnp.concatenate([dk, dv], axis=1)


DN_Q_SCALE = DN_HEAD_DIM ** -0.5
L2_EPS = 1e-6


def _bdg(a, b, ca, cb):
    return lax.dot_general(a.astype(MXU_DTYPE), b.astype(MXU_DTYPE), (((ca,), (cb,)), ((0,), (0,))), preferred_element_type=F32)


@jax.custom_vjp
def _bmm(a, b):
    return _bdg(a, b, 2, 1)


_bmm.defvjp(lambda a, b: (_bdg(a, b, 2, 1), (a, b)), lambda r, g: (_bdg(g, r[1], 2, 2), _bdg(r[0], g, 1, 1)))


@jax.custom_vjp
def _bmm_nt(a, b):
    return _bdg(a, b, 2, 2)


_bmm_nt.defvjp(lambda a, b: (_bdg(a, b, 2, 2), (a, b)), lambda r, g: (_bdg(g, r[1], 2, 1), _bdg(g, r[0], 1, 1)))


@jax.custom_vjp
def _bmm_tn(a, b):
    return _bdg(a, b, 1, 1)


_bmm_tn.defvjp(lambda a, b: (_bdg(a, b, 1, 1), (a, b)), lambda r, g: (_bdg(r[1], g, 2, 2), _bdg(r[0], g, 2, 1)))


def _tri_inverse(n):
    eye = (lax.broadcasted_iota(jnp.int32, n.shape, 1) == lax.broadcasted_iota(jnp.int32, n.shape, 2)).astype(F32)
    inv, p = eye - n, n
    for _ in range(5):
        p = _bdg(p, p, 2, 1)
        inv = _bdg(inv, eye + p, 2, 1)
    return inv


@jax.custom_vjp
def _tri_solve2(n, r1, r2):
    t = _tri_inverse(n)
    return _bdg(t, r1, 2, 1), _bdg(t, r2, 2, 1)


def _tri_solve2_fwd(n, r1, r2):
    t = _tri_inverse(n)
    x1, x2 = _bdg(t, r1, 2, 1), _bdg(t, r2, 2, 1)
    return (x1, x2), (t, x1, x2)


def _tri_solve2_bwd(saved, cts):
    t, x1, x2 = saved
    d1, d2 = _bdg(t, cts[0], 1, 1), _bdg(t, cts[1], 1, 1)
    return -(_bdg(d1, x1, 2, 2) + _bdg(d2, x2, 2, 2)), d1, d2


_tri_solve2.defvjp(_tri_solve2_fwd, _tri_solve2_bwd)


def _dn_gates(ab, alog, dtb):
    return -jnp.exp(alog) * _softplus(ab + dtb), _sigmoid(ab)


def _dn_heads(cq, ck, cv, z, g, beta, onorm, state):
    h, c, _ = cq.shape
    l2 = lambda t: t * lax.rsqrt(jnp.sum(t * t, axis=-1, keepdims=True) + L2_EPS)
    q, k, v = l2(_silu(cq)) * DN_Q_SCALE, l2(_silu(ck)), _silu(cv)
    r, cc = lax.broadcasted_iota(jnp.int32, (h, c, c), 1), lax.broadcasted_iota(jnp.int32, (h, c, c), 2)
    tri, eye = r >= cc, r == cc
    g_row = jnp.sum(jnp.where(eye, g, 0.0), axis=1, keepdims=True)
    gcum_c = jnp.sum(jnp.where(tri, g_row, 0.0), axis=2, keepdims=True)
    gcum_r = jnp.sum(jnp.where(cc >= r, g, 0.0), axis=1, keepdims=True)
    decay = jnp.where(tri, jnp.exp(jnp.where(tri, gcum_c - gcum_r, 0.0)), 0.0)
    kb = k * beta
    n = jnp.where(r > cc, _bmm_nt(kb, k) * decay, 0.0)
    u, w = _tri_solve2(n, v * beta, kb * jnp.exp(gcum_c))
    v_new = u - _bmm(w, state)
    o = _bmm(q * jnp.exp(gcum_c), state) + _bmm(_bmm_nt(q, k) * decay, v_new)
    g_last = jnp.sum(g, axis=1, keepdims=True)
    new_state = state * jnp.exp(g_last) + _bmm_tn(k * jnp.exp(g_last - gcum_c), v_new)
    on = o * lax.rsqrt(jnp.mean(o * o, axis=-1, keepdims=True) + NORM_EPS) * onorm
    return on * _silu(z), new_state


DN_STEP = 4


def _dn_stack(ref, col0, rows=slice(None)):
    return jnp.stack([ref[rows, col0 + h * DN_HEAD_DIM:col0 + (h + 1) * DN_HEAD_DIM].astype(F32) for h in range(DN_HEADS)], axis=0)


def _dn_cols(block, col0):
    return jnp.stack([block[:, col0 + h:col0 + h + 1] for h in range(DN_HEADS)], axis=0)


def _dn_fwd(cqkv, proj, ab, alog, dtb, onorm, *, name):
    s_len = cqkv.shape[0]
    c, hd, w = DN_CHUNK, DN_HEAD_DIM, DN_WIDTH
    n_chunks = s_len // c

    def body(c_ref, z_ref, ab_ref, alog_ref, dtb_ref, on_ref, o_ref, st_ref, state):
        @pl.when(pl.program_id(0) == 0)
        def _():
            state[...] = jnp.zeros_like(state)

        g_all, beta_all = _dn_gates(ab_ref[...], alog_ref[...], dtb_ref[...])
        st = state[...]
        for j in range(DN_STEP):
            rows = slice(j * c, (j + 1) * c)
            st_ref[j] = st
            out, st = _dn_heads(_dn_stack(c_ref, 0, rows), _dn_stack(c_ref, w, rows), _dn_stack(c_ref, 2 * w, rows),
                                _dn_stack(z_ref, 0, rows), _dn_cols(g_all[rows], 0), _dn_cols(beta_all[rows], DN_HEADS), on_ref[...], st)
            for h in range(DN_HEADS):
                o_ref[rows, h * hd:(h + 1) * hd] = out[h].astype(o_ref.dtype)
        state[...] = st

    vec = pl.BlockSpec((1, LANES), lambda i: (0, 0))
    t = DN_STEP * c
    return pl.pallas_call(
        body, name=name, grid=(n_chunks // DN_STEP,),
        in_specs=[pl.BlockSpec((t, 3 * w), lambda i: (i, 0)), pl.BlockSpec((t, w), lambda i: (i, 3)),
                  pl.BlockSpec((t, LANES), lambda i: (i, 0)), vec, vec, vec],
        out_specs=[pl.BlockSpec((t, w), lambda i: (i, 0)), pl.BlockSpec((DN_STEP, DN_HEADS, hd, hd), lambda i: (i, 0, 0, 0))],
        out_shape=[jax.ShapeDtypeStruct((s_len, w), MXU_DTYPE), jax.ShapeDtypeStruct((n_chunks, DN_HEADS, hd, hd), F32)],
        scratch_shapes=[pltpu.VMEM((DN_HEADS, hd, hd), F32)], compiler_params=_params("arbitrary"),
    )(cqkv, proj, ab, alog, dtb, onorm)


def _dn_bwd(cqkv, proj, ab, alog, dtb, onorm, states, dout, *, name):
    s_len = cqkv.shape[0]
    c, hd, w = DN_CHUNK, DN_HEAD_DIM, DN_WIDTH
    n_chunks = s_len // c

    def body(c_ref, z_ref, ab_ref, alog_ref, dtb_ref, on_ref, st_ref, do_ref,
             dc_ref, dz_ref, dab_ref, dalog_ref, ddtb_ref, don_ref, dstate):
        @pl.when(pl.program_id(0) == 0)
        def _():
            dstate[...] = jnp.zeros_like(dstate)
            for r in (dalog_ref, ddtb_ref, don_ref):
                r[...] = jnp.zeros_like(r)

        (g_all, beta_all), gates_vjp = jax.vjp(_dn_gates, ab_ref[...], alog_ref[...], dtb_ref[...])
        dst = dstate[...]
        col = _cols((c, LANES))
        dg_parts, dbeta_parts = [None] * DN_STEP, [None] * DN_STEP
        don = jnp.zeros((1, hd), F32)
        for j in reversed(range(DN_STEP)):
            rows = slice(j * c, (j + 1) * c)
            _, vjp = jax.vjp(_dn_heads, _dn_stack(c_ref, 0, rows), _dn_stack(c_ref, w, rows), _dn_stack(c_ref, 2 * w, rows),
                             _dn_stack(z_ref, 0, rows), _dn_cols(g_all[rows], 0), _dn_cols(beta_all[rows], DN_HEADS), on_ref[...], st_ref[j])
            dcq, dck, dcv, dz, dg, dbeta, don_j, dst = vjp((_dn_stack(do_ref, 0, rows), dst))
            dg_all, dbeta_all = jnp.zeros((c, LANES), F32), jnp.zeros((c, LANES), F32)
            for h in range(DN_HEADS):
                sl = slice(h * hd, (h + 1) * hd)
                dc_ref[rows, sl] = dcq[h]
                dc_ref[rows, w + h * hd:w + (h + 1) * hd] = dck[h]
                dc_ref[rows, 2 * w + h * hd:2 * w + (h + 1) * hd] = dcv[h]
                dz_ref[rows, sl] = dz[h].astype(dz_ref.dtype)
                dg_all = dg_all + jnp.where(col == h, dg[h], 0.0)
                dbeta_all = dbeta_all + jnp.where(col == DN_HEADS + h, dbeta[h], 0.0)
            dg_parts[j], dbeta_parts[j] = dg_all, dbeta_all
            don = don + don_j
        dstate[...] = dst
        dab, dalog, ddtb = gates_vjp((jnp.concatenate(dg_parts, axis=0), jnp.concatenate(dbeta_parts, axis=0)))
        dab_ref[...] = dab
        dalog_ref[...] += dalog
        ddtb_ref[...] += ddtb
        don_ref[...] += don

    steps = n_chunks // DN_STEP
    rev = lambda i: steps - 1 - i
    vec = pl.BlockSpec((1, LANES), lambda i: (0, 0))
    t = DN_STEP * c
    return pl.pallas_call(
        body, name=name, grid=(steps,),
        in_specs=[pl.BlockSpec((t, 3 * w), lambda i: (rev(i), 0)), pl.BlockSpec((t, w), lambda i: (rev(i), 3)),
                  pl.BlockSpec((t, LANES), lambda i: (rev(i), 0)), vec, vec, vec,
                  pl.BlockSpec((DN_STEP, DN_HEADS, hd, hd), lambda i: (rev(i), 0, 0, 0)), pl.BlockSpec((t, w), lambda i: (rev(i), 0))],
        out_specs=[pl.BlockSpec((t, 3 * w), lambda i: (rev(i), 0)), pl.BlockSpec((t, w), lambda i: (rev(i), 0)),
                   pl.BlockSpec((t, LANES), lambda i: (rev(i), 0)), vec, vec, vec],
        out_shape=[jax.ShapeDtypeStruct((s_len, 3 * w), F32), jax.ShapeDtypeStruct((s_len, w), MXU_DTYPE),
                   jax.ShapeDtypeStruct((s_len, LANES), F32)] + [jax.ShapeDtypeStruct((1, LANES), F32)] * 3,
        scratch_shapes=[pltpu.VMEM((DN_HEADS, hd, hd), F32)], compiler_params=_params("arbitrary"),
    )(cqkv, proj, ab, alog, dtb, onorm, states, dout)


def _final_loss(x, g, target, *, name):
    s, d = x.shape
    tm = _tile(s, (512, 256))

    def body(x_ref, g_ref, t_ref, loss_ref, dx_ref, dg_ref):
        @pl.when(pl.program_id(0) == 0)
        def _():
            loss_ref[...] = jnp.zeros_like(loss_ref)
            dg_ref[...] = jnp.zeros_like(dg_ref)

        xv, gv = x_ref[...], g_ref[...]
        r = lax.rsqrt(jnp.mean(xv * xv, axis=-1, keepdims=True) + NORM_EPS)
        xh = xv * r
        err = xh * gv - t_ref[...]
        loss_ref[...] += 0.5 * jnp.sum(jnp.mean(err * err, axis=-1, keepdims=True), axis=0, keepdims=True)
        dy = err * (1.0 / d)
        dxh = dy * gv
        dx_ref[...] = r * (dxh - xh * jnp.mean(dxh * xh, axis=-1, keepdims=True))
        dg_ref[...] += jnp.sum(dy * xh, axis=0, keepdims=True)

    row = pl.BlockSpec((tm, d), lambda i: (i, 0))
    vec = pl.BlockSpec((1, d), lambda i: (0, 0))
    return pl.pallas_call(
        body, name=name, grid=(s // tm,), in_specs=[row, vec, row],
        out_specs=[pl.BlockSpec((1, LANES), lambda i: (0, 0)), row, vec],
        out_shape=[jax.ShapeDtypeStruct((1, LANES), F32), jax.ShapeDtypeStruct((s, d), F32), jax.ShapeDtypeStruct((1, d), F32)],
        compiler_params=_params("arbitrary"),
    )(x, g.reshape(1, d), target)


def _adamw(w, g_layers, m, v, *, name):
    shape = w.shape
    cols = shape[-1]
    n_l = len(g_layers)
    rows = max(w.size // cols, 1) // n_l
    tr = _tile(rows, [t for t in (512, 352, 256, 128, 64, 32, 16, 8) if t * cols * 4 <= ADAMW_BLOCK_BYTES])
    c1, c2 = 1.0 - ADAM_B1 ** ADAM_STEP, 1.0 - ADAM_B2 ** ADAM_STEP

    def body(*refs):
        w_ref, m_ref, v_ref = refs[:3]
        d_ref, nm_ref, nv_ref, g_ref = refs[3 + n_l:]
        gv = refs[3][...]
        for k in range(1, n_l):
            gv = jnp.where(pl.program_id(0) == k, refs[3 + k][...], gv)
        nm = ADAM_B1 * m_ref[...] + (1.0 - ADAM_B1) * gv
        nv = ADAM_B2 * v_ref[...] + (1.0 - ADAM_B2) * (gv * gv)
        d_ref[...] = -ADAM_LR * ((nm / c1) / (jnp.sqrt(nv / c2) + ADAM_EPS) + ADAM_WD * w_ref[...])
        nm_ref[...] = nm
        nv_ref[...] = nv
        g_ref[...] = gv

    blk = pl.BlockSpec((None, tr, cols), lambda l, i: (l, i, 0))
    slab = pl.BlockSpec((tr, cols), lambda l, i: (i, 0))
    outs = pl.pallas_call(
        body, name=name, grid=(n_l, rows // tr), in_specs=[blk] * 3 + [slab] * n_l, out_specs=[blk] * 4,
        out_shape=[jax.ShapeDtypeStruct((n_l, rows, cols), F32)] * 4, compiler_params=_params("parallel", "parallel"),
    )(*(t.reshape(n_l, rows, cols) for t in (w, m, v)), *(t.reshape(rows, cols) for t in g_layers))
    return tuple(t.reshape(shape) for t in outs)


def _block_diag(w):
    n, j, k = w.shape
    eye = jnp.eye(n, dtype=w.dtype)
    return (eye[:, None, :, None] * w[:, :, None, :]).reshape(n * j, n * k)


def _block_diag_part(m, n):
    j, k = m.shape[0] // n, m.shape[1] // n
    m4 = m.reshape(n, j, n, k)
    return jnp.stack([m4[i, :, i, :] for i in range(n)], axis=0)


DN_AB = 2 * DN_HEADS
DEPTH = 2


def _row(v, width=None):
    v = v.reshape(1, -1)
    return v if width is None else jnp.pad(v, ((0, 0), (0, width - v.shape[1])))


def _conv_w8(conv_w, bias=None):
    w8 = jnp.zeros((SUBLANES, conv_w.shape[1]), F32).at[:CONV_K].set(conv_w)
    return w8 if bias is None else w8.at[CONV_K].set(bias)


def _mixer_ab_fwd(x, w, tag):
    h, (proj,) = _norm_mm(x, w["mix_norm"][0], [w["ab_w_in"][0]], [F32], name=f"{tag}_in")
    o, lse = _dattn_forward(proj, tag)
    w8 = _conv_w8(w["lru_conv_w"][0], w["lru_conv_b"][0])
    xc = _conv_fwd(proj, PROJ_AB_BLOCKS - 2, LRU_WIDTH, w8, name=f"{tag}_conv")
    wa, wx = _block_diag(w["lru_w_a"][0]), _block_diag(w["lru_w_x"][0])
    vecs = (_row(w["lru_b_a"][0]), _row(w["lru_b_x"][0]), _row(w["lru_lambda"][0]))
    hs, y = _lru_fwd(xc, proj, wa, wx, *vecs, name=f"{tag}_lru")
    w_out = w["ab_w_out"][0]
    x2 = _mm(o, w_out[:ATTN_WIDTH], res=x, name=f"{tag}_out_attn")
    x2 = _mm(y, w_out[ATTN_WIDTH:], res=x2, name=f"{tag}_out_lru")
    return x2, (x, h, proj, o, lse, w8, xc, wa, wx, vecs, hs, y)


def _mixer_ab_bwd(saved, w, dy, tag):
    x, h, proj, o, lse, w8, xc, wa, wx, vecs, hs, y = saved
    w_out = w["ab_w_out"][0]
    dcat = _mm(dy, w_out, mode="nt", name=f"{tag}_dcat")
    dw_out = jnp.concatenate([_mm(o, dy, mode="tn", name=f"{tag}_dwout_attn"), _mm(y, dy, mode="tn", name=f"{tag}_dwout_lru")], axis=0)
    dq, dk, dv = _dattn_backward(proj, o, lse, dcat, tag)
    dxc, dgr, dwa, dwx, dba, dbx, dlam = _lru_bwd(xc, proj, hs, dcat, wa, wx, *vecs, name=f"{tag}_dlru")
    dxr, dw8 = _conv_bwd(proj, PROJ_AB_BLOCKS - 2, LRU_WIDTH, w8, dxc, name=f"{tag}_dconv")
    dproj = jnp.concatenate([t.astype(MXU_DTYPE) for t in (dq, dk, dv, dxr, dgr)], axis=1)
    dw_in = _mm(h, dproj, mode="tn", name=f"{tag}_dwin")
    dx, dg = _mm_rms_bwd([(dproj, w["ab_w_in"][0])], x, w["mix_norm"][0], dy, name=f"{tag}_dh")
    grads = dict(mix_norm=dg[0], ab_w_in=dw_in, ab_w_out=dw_out, lru_conv_w=dw8[:CONV_K], lru_conv_b=dw8[CONV_K],
                 lru_w_a=_block_diag_part(dwa, LRU_BLOCKS), lru_b_a=dba[0], lru_w_x=_block_diag_part(dwx, LRU_BLOCKS),
                 lru_b_x=dbx[0], lru_lambda=dlam[0])
    return dx, grads


def _dn_split_w(w_in):
    return w_in[:, :4 * DN_WIDTH], jnp.pad(w_in[:, 4 * DN_WIDTH:], ((0, 0), (0, LANES - DN_AB)))


def _mixer_dn_fwd(x, w, tag):
    w_qkvz, w_ab = _dn_split_w(w["dn_w_in"][0])
    h, (proj, ab) = _norm_mm(x, w["mix_norm"][1], [w_qkvz, w_ab], [F32, F32], name=f"{tag}_in")
    w8 = _conv_w8(w["dn_conv_w"][0])
    cqkv = _conv_fwd(proj, 0, 3 * DN_WIDTH, w8, name=f"{tag}_conv")
    vecs = (_row(w["dn_a_log"][0], LANES), _row(w["dn_dt_bias"][0], LANES), _row(w["dn_o_norm"][0]))
    og, states = _dn_fwd(cqkv, proj, ab, *vecs, name=f"{tag}_dn")
    x2 = _mm(og, w["dn_w_out"][0], res=x, name=f"{tag}_out")
    return x2, (x, h, w_qkvz, w_ab, proj, ab, w8, cqkv, vecs, og, states)


def _mixer_dn_bwd(saved, w, dy, tag):
    x, h, w_qkvz, w_ab, proj, ab, w8, cqkv, vecs, og, states = saved
    dout = _mm(dy, w["dn_w_out"][0], mode="nt", name=f"{tag}_dout")
    dw_out = _mm(og, dy, mode="tn", name=f"{tag}_dwout")
    dcqkv, dz, dab, dalog, ddtb, don = _dn_bwd(cqkv, proj, ab, *vecs, states, dout, name=f"{tag}_ddn")
    dqkv, dw8 = _conv_bwd(proj, 0, 3 * DN_WIDTH, w8, dcqkv, name=f"{tag}_dconv")
    dproj = jnp.concatenate([dqkv.astype(MXU_DTYPE), dz.astype(MXU_DTYPE)], axis=1)
    dw_in = jnp.concatenate([_mm(h, dproj, mode="tn", name=f"{tag}_dwin"),
                             _mm(h, dab, mode="tn", name=f"{tag}_dwin_ab")[:, :DN_AB]], axis=1)
    dx, dg = _mm_rms_bwd([(dproj, w_qkvz), (dab, w_ab)], x, w["mix_norm"][1], dy, name=f"{tag}_dh")
    grads = dict(mix_norm=dg[0], dn_w_in=dw_in, dn_w_out=dw_out, dn_conv_w=dw8[:CONV_K], dn_a_log=dalog[0, :DN_HEADS],
                 dn_dt_bias=ddtb[0, :DN_HEADS], dn_o_norm=don[0])
    return dx, grads


def _xa_layer_fwd(x, mem, w, layer, tag):
    hq, (q,) = _norm_mm(x, w["xa_norm"][layer], [_layer_matrix(w["xa_wq"], layer)], [MXU_DTYPE], name=f"{tag}_q")
    hm = _rms_fwd(mem, w["xa_mem_norm"][layer], name=f"{tag}_mem_norm")
    kv = _mm(hm, _layer_matrix(w["xa_wkv"], layer), name=f"{tag}_kv")
    oa, x2 = _xa_fwd(q, kv, x, _layer_matrix(w["xa_wo"], layer), name=f"{tag}_core")
    return x2, (x, hq, q, hm, kv, oa)


def _xa_layer_bwd(saved, mem, w, layer, dy, tag):
    x, hq, q, hm, kv, oa = saved
    dwo = _mm(oa, dy, mode="tn", name=f"{tag}_dwo")
    dq, dkv = _xa_bwd(q, kv, dy, _layer_matrix(w["xa_wo"], layer), name=f"{tag}_dcore")
    dwq = _mm(hq, dq, mode="tn", name=f"{tag}_dwq")
    dx, dg = _mm_rms_bwd([(dq, _layer_matrix(w["xa_wq"], layer))], x, w["xa_norm"][layer], dy, name=f"{tag}_dhq")
    dwkv = _mm(hm, dkv, mode="tn", name=f"{tag}_dwkv")
    dhm = _mm(dkv, _layer_matrix(w["xa_wkv"], layer), mode="nt", name=f"{tag}_dhm")
    _, dgm = _rms_bwd(mem, w["xa_mem_norm"][layer], dhm, jnp.zeros_like(mem), name=f"{tag}_dmem_norm")
    return dx, dict(xa_norm=dg[0], xa_mem_norm=dgm[0], xa_wq=dwq, xa_wkv=dwkv, xa_wo=dwo)


def _local_step(x, mem, target, w, pending=None, reduce_big=False):
    saved = []
    plans, shards = pending if pending else ([], None)
    hosts = {("ffn1", 0): plans[0], ("ffn2", 0): plans[1]} if plans else {}

    def ffn(which, layer, x):
        plan = hosts.get((which, layer))
        hosted = ([b for *_, b in plan], [a for _, _, a, _ in plan]) if plan else None
        x, s, gathered = _ffn_fwd(x, w[f"{which}_norm"], w[f"{which}_w_in"], w[f"{which}_w_out"], layer, f"l{layer}_{which}", gather=hosted)
        if plan:
            _gathered(plan, gathered, shards, w)
        return x, s

    for layer in range(DEPTH):
        t = f"l{layer}"
        x, s1 = ffn("ffn1", layer, x)
        x, s2 = (_mixer_ab_fwd if layer % 2 == 0 else _mixer_dn_fwd)(x, w, f"{t}_mix")
        x, s3 = _xa_layer_fwd(x, mem, w, layer, f"{t}_xa")
        x, s4 = ffn("ffn2", layer, x)
        saved.append((s1, s2, s3, s4))
    loss, dx, dgf = _final_loss(x, w["final_norm"], target, name="final_loss")
    per_layer, reduced = [None] * DEPTH, [None] * DEPTH
    travelling = None
    for layer in reversed(range(DEPTH)):
        t = f"l{layer}"
        s1, s2, s3, s4 = saved[layer]
        g = {}
        dx, g["ffn2_norm"], g["ffn2_w_in"], g["ffn2_w_out"], parts = _ffn_bwd(
            s4, w["ffn2_norm"], w["ffn2_w_in"], w["ffn2_w_out"], layer, dx, f"{t}_ffn2",
            exchange=travelling[1] if travelling else None)
        if travelling:
            reduced[travelling[0]] = _reduce_end(travelling[1], parts, f"l{travelling[0]}")
        dx, gx = _xa_layer_bwd(s3, mem, w, layer, dx, f"{t}_xa")
        dx, gm = (_mixer_ab_bwd if layer % 2 == 0 else _mixer_dn_bwd)(s2, w, dx, f"{t}_mix")
        dx, g["ffn1_norm"], g["ffn1_w_in"], g["ffn1_w_out"], _ = _ffn_bwd(
            s1, w["ffn1_norm"], w["ffn1_w_in"], w["ffn1_w_out"], layer, dx, f"{t}_ffn1")
        per_layer[layer] = {**g, **gx, **gm}
        if reduce_big:
            chip_sum = _reduce_begin(_pack_grads(per_layer[layer]), t)
            if layer > 0:
                travelling = (layer, chip_sum)
            else:
                reduced[layer] = _reduce_end(chip_sum, _exchange_chips(chip_sum), t)
    grads = {"final_norm": [dgf[0]]}
    for layer_grads in per_layer:
        for name, value in layer_grads.items():
            grads.setdefault(name, []).append(value)
    return loss, dx, grads, list(zip(reduced, per_layer))


N_CHIPS = 4
WIRE_DTYPE = jnp.bfloat16
HBM_SPEC = pl.BlockSpec(memory_space=pltpu.HBM)
PACK_COLS = 1024


def _place():
    x, y, c = lax.axis_index("x"), lax.axis_index("y"), lax.axis_index("c")
    return x, y, c, [(1 - x, y), (x, 1 - y), (1 - x, 1 - y)]


def _remote(src, dst, sems, k, to):
    return pltpu.make_async_remote_copy(src_ref=src, dst_ref=dst, send_sem=sems[0].at[k], recv_sem=sems[1].at[k],
                                        device_id=to, device_id_type=MESH)


def _gather_weights(blocks, axes):
    n = len(blocks)
    out_shapes, sem_shapes, start, finish = _gather_plan(blocks, axes)

    def body(*refs):
        start(refs[:n], refs[n:2 * n], *refs[2 * n:])
        finish(refs[:n], refs[n:2 * n], *refs[2 * n:])

    return pl.pallas_call(
        body, name="gather_weights", in_specs=[HBM_SPEC] * n, out_specs=[HBM_SPEC] * n,
        out_shape=out_shapes, scratch_shapes=sem_shapes,
    )(*blocks)


def _gather_plan(blocks, axes):
    n = len(blocks)
    split = [b.shape[1] % 32 == 0 for b in blocks]

    def full_shape(i):
        l, r, c = blocks[i].shape
        return (l, N_CHIPS * r, c) if axes[i] == 1 else (l, r, N_CHIPS * c)

    def copies(ins, outs, send_sems, recv_sems):
        x, y, c, chips = _place()
        sems = (send_sems, recv_sems)
        sibling = (x, y, 1 - c)
        me = 2 * x + y

        def window(i, k, h):
            l, r, cc = blocks[i].shape
            r0, nr = (0, r) if h is None else (h * (r // 2), r // 2)
            if axes[i] == 1:
                return outs[i].at[:, pl.ds(k * r + r0, nr), :]
            return outs[i].at[:, pl.ds(r0, nr), pl.ds(k * cc, cc)]

        def mine(i, h):
            r = blocks[i].shape[1]
            return ins[i] if h is None else ins[i].at[:, pl.ds(h * (r // 2), r // 2), :]

        half = lambda i: c if split[i] else None
        first = [_remote(mine(i, half(i)), window(i, me, half(i)), sems, 3 * i + j, (*chip, c))
                 for i in range(n) for j, chip in enumerate(chips)]
        first += [_remote(ins[i], window(i, me, None), sems, 6 * n + i, sibling) for i in range(n)]
        arrive = lambda i, j, h, k, frm: _remote(window(i, 2 * chips[j][0] + chips[j][1], h), window(i, 2 * chips[j][0] + chips[j][1], h),
                                                 sems, k, frm)
        return first, arrive, chips, c, sibling

    def start(ins, outs, send_sems, recv_sems):
        for cp in copies(ins, outs, send_sems, recv_sems)[0]:
            cp.start()

    def finish(ins, outs, send_sems, recv_sems):
        first, arrive, chips, c, sibling = copies(ins, outs, send_sems, recv_sems)
        passed = []
        for i in range(n):
            for j, (cx, cy) in enumerate(chips):
                arrive(i, j, c if split[i] else None, 3 * i + j, (cx, cy, c)).wait_recv()
                if split[i]:
                    passed.append(arrive(i, j, c, 3 * (n + i) + j, sibling))
                    passed[-1].start()
        for i in range(n):
            if split[i]:
                for j in range(len(chips)):
                    arrive(i, j, 1 - c, 3 * (n + i) + j, sibling).wait_recv()
        for cp in first[3 * n:]:
            cp.wait_recv()
        for cp in first + passed:
            cp.wait_send()

    sem_shapes = [pltpu.SemaphoreType.DMA((7 * n,)), pltpu.SemaphoreType.DMA((7 * n,))]
    return [jax.ShapeDtypeStruct(full_shape(i), blocks[i].dtype) for i in range(n)], sem_shapes, start, finish


def _allreduce_small(v):
    rows, cols = v.shape
    n_dev = 2 * N_CHIPS

    def body(v_ref, out_ref, all_ref, send_sems, recv_sems, local_sem):
        x, y, c, chips = _place()
        sems = (send_sems, recv_sems)
        me, sibling = (x, y, c), (x, y, 1 - c)
        slot = lambda px, py, pc: all_ref.at[pl.ds((4 * px + 2 * py + pc) * rows, rows), :]
        mine = pltpu.make_async_copy(v_ref, slot(*me), local_sem)
        mine.start()
        first = [_remote(v_ref, slot(*me), sems, 0, sibling)]
        first += [_remote(v_ref, slot(*me), sems, 1 + j, (*chip, c)) for j, chip in enumerate(chips)]
        for cp in first:
            cp.start()
        passed = [_remote(slot(*chip, c), slot(*chip, c), sems, 4 + j, sibling) for j, chip in enumerate(chips)]
        for j, chip in enumerate(chips):
            _remote(slot(*chip, c), slot(*chip, c), sems, 1 + j, me).wait_recv()
            passed[j].start()
        _remote(slot(*sibling), slot(*sibling), sems, 0, me).wait_recv()
        for j, chip in enumerate(chips):
            _remote(slot(*chip, 1 - c), slot(*chip, 1 - c), sems, 4 + j, me).wait_recv()
        for cp in first + passed:
            cp.wait_send()
        mine.wait()
        acc = all_ref[pl.ds(0, rows), :]
        for k in range(1, n_dev):
            acc = acc + all_ref[pl.ds(k * rows, rows), :]
        out_ref[...] = acc

    vmem = pl.BlockSpec(memory_space=pltpu.VMEM)
    return pl.pallas_call(
        body, name="allreduce_small", in_specs=[vmem], out_specs=vmem, out_shape=jax.ShapeDtypeStruct((rows, cols), F32),
        scratch_shapes=[pltpu.VMEM((n_dev * rows, cols), F32), pltpu.SemaphoreType.DMA((7,)), pltpu.SemaphoreType.DMA((7,)),
                        pltpu.SemaphoreType.DMA],
    )(v)


def _swap_other_half(g4, tag):
    n, _, rows, cols = g4.shape

    def body(v_ref, out_ref, send_sems, recv_sems):
        x, y, c, _ = _place()
        cp = _remote(v_ref.at[:, 1 - c], out_ref, (send_sems, recv_sems), 0, (x, y, 1 - c))
        cp.start()
        cp.wait()

    return pl.pallas_call(
        body, name=f"{tag}_reduce_swap", in_specs=[HBM_SPEC], out_specs=HBM_SPEC, out_shape=jax.ShapeDtypeStruct((n, rows, cols), g4.dtype),
        scratch_shapes=[pltpu.SemaphoreType.DMA((1,)), pltpu.SemaphoreType.DMA((1,))],
    )(g4)


def _add_kept_half(g4, got, tag):
    n, _, rows, cols = g4.shape
    tr = _tile(rows, (256, 128, 64, 32, 16))
    nb = rows // tr

    def body(c_ref, a_ref, b_ref, o_ref):
        o_ref[...] = (a_ref[...] + b_ref[...]).astype(o_ref.dtype)

    return pl.pallas_call(
        body, name=f"{tag}_reduce_sum_cores",
        grid_spec=pltpu.PrefetchScalarGridSpec(
            num_scalar_prefetch=1, grid=(n, nb),
            in_specs=[pl.BlockSpec((None, None, tr, cols), lambda k, i, c_ref: (k, c_ref[0], i, 0)),
                      pl.BlockSpec((None, tr, cols), lambda k, i, c_ref: (k, i, 0))],
            out_specs=pl.BlockSpec((None, tr, cols), lambda k, i, c_ref: (k, i, 0))),
        out_shape=jax.ShapeDtypeStruct((n, rows, cols), WIRE_DTYPE), compiler_params=_params("parallel", "parallel"),
    )(lax.axis_index("c").astype(jnp.int32).reshape(1), g4, got)


def _exchange_plan(v):
    def copies(v_ref, out_ref, send_sems, recv_sems):
        x, y, c, chips = _place()
        return [_remote(v_ref.at[2 * cx + cy], out_ref.at[j], (send_sems, recv_sems), j, (cx, cy, c)) for j, (cx, cy) in enumerate(chips)]

    def start(*refs):
        for cp in copies(*refs):
            cp.start()

    def finish(*refs):
        for cp in copies(*refs):
            cp.wait_recv()
        for cp in copies(*refs):
            cp.wait_send()

    sem_shapes = [pltpu.SemaphoreType.DMA((N_CHIPS - 1,)), pltpu.SemaphoreType.DMA((N_CHIPS - 1,))]
    return jax.ShapeDtypeStruct((N_CHIPS - 1,) + v.shape[1:], v.dtype), sem_shapes, start, finish


def _exchange_chips(v):
    out_shape, sem_shapes, start, finish = _exchange_plan(v)

    def body(*refs):
        start(*refs)
        finish(*refs)

    return pl.pallas_call(body, name="exchange_chips", in_specs=[HBM_SPEC], out_specs=HBM_SPEC, out_shape=out_shape,
                          scratch_shapes=sem_shapes)(v)


def _swap_sibling(v, tag):
    def body(v_ref, out_ref, send_sems, recv_sems):
        x, y, c, _ = _place()
        cp = _remote(v_ref, out_ref, (send_sems, recv_sems), 0, (x, y, 1 - c))
        cp.start()
        cp.wait()

    return pl.pallas_call(
        body, name=f"{tag}_share_halves", in_specs=[HBM_SPEC], out_specs=HBM_SPEC, out_shape=jax.ShapeDtypeStruct(v.shape, v.dtype),
        scratch_shapes=[pltpu.SemaphoreType.DMA((1,)), pltpu.SemaphoreType.DMA((1,))],
    )(v)


def _sum_chips(own4, parts, tag):
    _, rows, cols = own4.shape
    tr = _tile(rows, (256, 128, 64, 32, 16))

    def body(me_ref, own_ref, p0_ref, p1_ref, p2_ref, o_ref):
        acc = own_ref[...].astype(F32)
        for r in (p0_ref, p1_ref, p2_ref):
            acc = acc + r[...].astype(F32)
        o_ref[...] = acc

    part = lambda j: pl.BlockSpec((None, tr, cols), lambda i, me_ref: (j, i, 0))
    chip = (2 * lax.axis_index("x") + lax.axis_index("y")).astype(jnp.int32).reshape(1)
    return pl.pallas_call(
        body, name=f"{tag}_reduce_sum_chips",
        grid_spec=pltpu.PrefetchScalarGridSpec(
            num_scalar_prefetch=1, grid=(rows // tr,),
            in_specs=[pl.BlockSpec((None, tr, cols), lambda i, me_ref: (me_ref[0], i, 0)), part(0), part(1), part(2)],
            out_specs=pl.BlockSpec((tr, cols), lambda i, me_ref: (i, 0))),
        out_shape=jax.ShapeDtypeStruct((rows, cols), F32), compiler_params=_params("parallel"),
    )(chip, own4, parts, parts, parts)


def _reduce_begin(g4, tag):
    return _add_kept_half(g4, _swap_other_half(g4, tag), tag)


def _reduce_end(chip_sum, parts, tag):
    half = _sum_chips(chip_sum, parts, tag)
    other = _swap_sibling(half, tag)
    return jnp.where(lax.axis_index("c") == 0, jnp.stack([half, other]), jnp.stack([other, half]))


BIG = (("ffn1_w_in", 2), ("ffn1_w_out", 1), ("xa_wq", 1), ("xa_wkv", 2), ("xa_wo", 1), ("ffn2_w_in", 2), ("ffn2_w_out", 1),
       ("ab_w_in", 2), ("ab_w_out", 1), ("dn_w_in", 2), ("dn_w_out", 1))
TINY_SHARDED = (("lru_conv_w", 2), ("dn_conv_w", 2))
REPLICATED = ("ffn1_norm", "mix_norm", "xa_norm", "xa_mem_norm", "ffn2_norm", "lru_conv_b", "lru_w_a", "lru_b_a", "lru_w_x",
              "lru_b_x", "lru_lambda", "dn_a_log", "dn_dt_bias", "dn_o_norm", "final_norm")
WEIGHTS = ("ffn1_norm", "ffn1_w_in", "ffn1_w_out", "mix_norm", "xa_norm", "xa_mem_norm", "xa_wq", "xa_wkv", "xa_wo", "ffn2_norm",
           "ffn2_w_in", "ffn2_w_out", "ab_w_in", "lru_conv_w", "lru_conv_b", "lru_w_a", "lru_b_a", "lru_w_x", "lru_b_x",
           "lru_lambda", "ab_w_out", "dn_w_in", "dn_conv_w", "dn_a_log", "dn_dt_bias", "dn_o_norm", "dn_w_out", "final_norm")


def _lane_padded(shape):
    return shape[:-1] + (-(-shape[-1] // LANES) * LANES,)


def _pad_lanes(t):
    return jnp.pad(t, [(0, 0)] * (t.ndim - 1) + [(0, _lane_padded(t.shape)[-1] - t.shape[-1])])


FIRST_USED = ("ffn1_w_in", "ffn1_w_out")
LAYER_1_ONLY = ("dn_w_in", "dn_w_out", "dn_conv_w")


def _gather_blocks(shards):
    groups = [], [], []
    for n, a in BIG + TINY_SHARDED:
        block = _pad_lanes(shards[n]).astype(MXU_DTYPE) if (n, a) in BIG else shards[n]
        if block.shape[0] == 1:
            groups[2 if n in LAYER_1_ONLY else 1].append((n, None, a, block))
        else:
            for layer in range(block.shape[0]):
                group = 2 if layer > 0 else 0 if n in FIRST_USED else 1
                groups[group].append((n, layer, a, block[layer:layer + 1]))
    return groups


def _gathered(plan, arrays, shards, into):
    for (n, layer, axis, _), full in zip(plan, arrays):
        width, padded = shards[n].shape[-1], _lane_padded(shards[n].shape)[-1]
        if padded != width:
            assert axis == 2
            full = jnp.concatenate([full[..., k * padded:k * padded + width] for k in range(N_CHIPS)], axis=-1)
        if layer is None:
            into[n] = full
        else:
            into.setdefault(n, [None] * shards[n].shape[0])[layer] = full
    return into


def _pack_parts(cols):
    whole = cols // PACK_COLS * PACK_COLS
    return [(c0, PACK_COLS) for c0 in range(0, whole, PACK_COLS)] + ([(whole, cols - whole)] if cols > whole else [])


def _to_rows(block):
    block = _pad_lanes(block)
    return jnp.concatenate([block[:, c0:c0 + n].reshape(-1, PACK_COLS) for c0, n in _pack_parts(block.shape[1])], axis=0)


def _from_rows(rows, r, c):
    padded = _lane_padded((r, c))[1]
    parts, off = [], 0
    for _, n in _pack_parts(padded):
        size = r * n // PACK_COLS
        parts.append(rows[off:off + size].reshape(r, n))
        off += size
    return jnp.concatenate(parts, axis=1)[:, :c]


def _pack_rows(r, c):
    return r * _lane_padded((r, c))[1] // PACK_COLS


def _pack_grads(layer_grads):
    names = [(n, axis) for n, axis in BIG if n in layer_grads]
    used = sum(_pack_rows(layer_grads[n].shape[0] // (N_CHIPS if axis == 1 else 1),
                          layer_grads[n].shape[1] // (N_CHIPS if axis == 2 else 1)) for n, axis in names)
    rows = -(-used // 512) * 512

    def chip_block(k):
        blocks = []
        for n, axis in names:
            width = layer_grads[n].shape[axis - 1] // N_CHIPS
            blocks.append(_to_rows(lax.slice_in_dim(layer_grads[n], k * width, (k + 1) * width, axis=axis - 1)))
        if rows > used:
            blocks.append(jnp.zeros((rows - used, PACK_COLS), F32))
        return jnp.concatenate(blocks, axis=0)

    return jnp.stack([chip_block(k) for k in range(N_CHIPS)], axis=0).reshape(N_CHIPS, 2, rows // 2, PACK_COLS)


def _unpack_grads(reduced, layer_grads, shards):
    rows = reduced.reshape(-1, PACK_COLS)
    out, off = {}, 0
    for n, _ in BIG:
        if n in layer_grads:
            r, c = shards[n].shape[1:]
            out[n] = _from_rows(rows[off:off + _pack_rows(r, c)], r, c)[None]
            off += _pack_rows(r, c)
    return out


def _pack_small(grads, loss):
    parts = [p.reshape(-1) for n in REPLICATED + tuple(n for n, _ in TINY_SHARDED) for p in grads[n]] + [loss[0, :1]]
    flat = jnp.concatenate(parts)
    total = -(-flat.shape[0] // (SUBLANES * LANES)) * SUBLANES * LANES
    return jnp.pad(flat, (0, total - flat.shape[0])).reshape(-1, LANES)


def _unpack_small(summed, shards, chip):
    flat = summed.reshape(-1)
    out, off = {}, 0
    for n in REPLICATED:
        out[n] = flat[off:off + shards[n].size].reshape(shards[n].shape)
        off += shards[n].size
    for n, axis in TINY_SHARDED:
        width = shards[n].shape[axis]
        shape = shards[n].shape[:axis] + (N_CHIPS * width,) + shards[n].shape[axis + 1:]
        full = flat[off:off + N_CHIPS * shards[n].size].reshape(shape)
        out[n] = lax.dynamic_slice_in_dim(full, chip * width, width, axis=axis)
        off += N_CHIPS * shards[n].size
    return out, flat[off]


def kernel(x, mem, ffn1_norm, ffn1_w_in, ffn1_w_out, mix_norm, xa_norm, xa_mem_norm, xa_wq, xa_wkv, xa_wo, ffn2_norm,
           ffn2_w_in, ffn2_w_out, ab_w_in, lru_conv_w, lru_conv_b, lru_w_a, lru_b_a, lru_w_x, lru_b_x, lru_lambda,
           ab_w_out, dn_w_in, dn_conv_w, dn_a_log, dn_dt_bias, dn_o_norm, dn_w_out, final_norm, loss_target,
           m_ffn1_norm, m_ffn1_w_in, m_ffn1_w_out, m_mix_norm, m_xa_norm, m_xa_mem_norm, m_xa_wq, m_xa_wkv, m_xa_wo,
           m_ffn2_norm, m_ffn2_w_in, m_ffn2_w_out, m_ab_w_in, m_lru_conv_w, m_lru_conv_b, m_lru_w_a, m_lru_b_a,
           m_lru_w_x, m_lru_b_x, m_lru_lambda, m_ab_w_out, m_dn_w_in, m_dn_conv_w, m_dn_a_log, m_dn_dt_bias,
           m_dn_o_norm, m_dn_w_out, m_final_norm, v_ffn1_norm, v_ffn1_w_in, v_ffn1_w_out, v_mix_norm, v_xa_norm,
           v_xa_mem_norm, v_xa_wq, v_xa_wkv, v_xa_wo, v_ffn2_norm, v_ffn2_w_in, v_ffn2_w_out, v_ab_w_in,
           v_lru_conv_w, v_lru_conv_b, v_lru_w_a, v_lru_b_a, v_lru_w_x, v_lru_b_x, v_lru_lambda, v_ab_w_out,
           v_dn_w_in, v_dn_conv_w, v_dn_a_log, v_dn_dt_bias, v_dn_o_norm, v_dn_w_out, v_final_norm):
    given = dict(locals())
    shards = {n: given[n] for n in WEIGHTS}
    chip = 2 * lax.axis_index("x") + lax.axis_index("y")

    full = {n: shards[n] for n in REPLICATED}
    first, *later = _gather_blocks(shards)
    _gathered(first, _gather_weights([b for *_, b in first], [a for _, _, a, _ in first]), shards, full)
    loss, grad_x, grads, reduced = _local_step(x[0], mem[0], loss_target[0], full, pending=(later, shards), reduce_big=True)

    small, loss_sum = _unpack_small(_allreduce_small(_pack_small(grads, loss)), shards, chip)
    per_layer = [_unpack_grads(r, layer_grads, shards) for r, layer_grads in reduced]
    slabs = {n: [g] for n, g in small.items()}
    slabs.update({n: [p[n] for p in per_layer if n in p] for n, _ in BIG})

    grad, delta, new_m, new_v = {}, {}, {}, {}
    for n in WEIGHTS:
        delta[n], new_m[n], new_v[n], grad[n] = _adamw(shards[n], slabs[n], given["m_" + n], given["v_" + n], name=f"adamw_{n}")
    return (loss_sum, grad_x[None], *[grad[n] for n in WEIGHTS], *[delta[n] for n in WEIGHTS],
            *[new_m[n] for n in WEIGHTS], *[new_v[n] for n in WEIGHTS])
```
